```python
import math
import jax, jax.numpy as jnp
from jax import lax
import numpy as np

D_MODEL = 2048
BATCH = 2
SEQ = 4096
DEPTH = 1

D_MIX = D_MODEL
D_HYENA = D_MIX // 2
D_GDN = D_MIX - D_HYENA
GDN_HEADS = 8
GDN_HEAD_DIM = D_GDN // GDN_HEADS
GDN_CHUNK = 64
HYENA_SHORT_CONV = 3
GDN_SHORT_CONV = 3
POS_EMB_DIM = 33
FILTER_WIDTH = 64
DECAY_TARGET = 1e-2
FAST_DECAY_PCT = 0.3
SLOW_DECAY_PCT = 1.5
N_GROUPS = 8
EXPERTS_PER_GROUP = 8
N_EXPERTS = N_GROUPS * EXPERTS_PER_GROUP
TOP_K = 2
D_EXPERT = D_MODEL // 4
ROW_BLOCK = 256
EPS = 1e-6
N_PROJ = 3 * D_HYENA + 4 * D_GDN + 4 * GDN_HEADS

kernel_name = 'hybrid_hyena_gdn_hmoe_encoder'


def rms_norm(x, w):
    xf = x.astype(jnp.float32)
    y = xf * lax.rsqrt(jnp.mean(xf * xf, axis=-1, keepdims=True) + EPS)
    return (y * w.astype(jnp.float32)).astype(x.dtype)


def centred_depthwise_conv(x, w):
    K = w.shape[0]
    L = x.shape[1]
    p = K // 2
    xp = jnp.pad(x, ((0, 0), (p, p), (0, 0)))
    y = xp[:, 0:L] * w[0]
    for j in range(1, K):
        y = y + xp[:, j:j + L] * w[j]
    return y


def l2_normalize(x):
    return x * lax.rsqrt(jnp.sum(x * x, axis=-1, keepdims=True) + EPS)


def hyena_implicit_filters(L, w1, b1, w2, b2, w3, b3, w4, freq):
    f32 = jnp.float32
    t = jnp.linspace(0.0, 1.0, L, dtype=f32)[:, None]
    bands = (POS_EMB_DIM - 1) // 2
    omega = (2.0 * math.pi / L) * jnp.arange(L, dtype=f32)
    fb = jnp.linspace(1e-4, bands - 1, bands, dtype=f32)
    ang = omega[:, None] * fb[None, :]
    z = jnp.concatenate([t, jnp.cos(ang), -jnp.sin(ang)], axis=-1)
    fr = freq.astype(f32)
    h = jnp.sin(fr * (z @ w1.astype(f32) + b1.astype(f32)))
    h = jnp.sin(fr * (h @ w2.astype(f32) + b2.astype(f32)))
    h = jnp.sin(fr * (h @ w3.astype(f32) + b3.astype(f32)))
    h = h @ w4.astype(f32)
    max_decay = math.log(DECAY_TARGET) / FAST_DECAY_PCT
    min_decay = math.log(DECAY_TARGET) / SLOW_DECAY_PCT
    deltas = jnp.abs(jnp.linspace(min_decay, max_decay, D_HYENA, dtype=f32))
    window = jnp.exp(-t * deltas[None, :])
    h = h.reshape(L, 2, D_HYENA) * window[:, None, :]
    return h[:, 0], h[:, 1]


def hyena_mixer(p, conv_w, conv_b, fw1, fb1, fw2, fb2, fw3, fb3, fw4, freq, skip, norm_w):
    f32 = jnp.float32
    B, L, _ = p.shape
    pc = centred_depthwise_conv(p, conv_w) + conv_b
    x0, x1, v = jnp.split(pc.astype(f32), 3, axis=-1)
    h_fwd, h_bwd = hyena_implicit_filters(L, fw1, fb1, fw2, fb2, fw3, fb3, fw4, freq)
    n = 2 * L
    vg = v * x1
    v_f = jnp.fft.rfft(vg, n=n, axis=1)
    h_f = jnp.fft.rfft(h_fwd, n=n, axis=0) + jnp.conj(jnp.fft.rfft(h_bwd, n=n, axis=0))
    y = jnp.fft.irfft(v_f * h_f[None], n=n, axis=1)[:, :L] + vg * skip.astype(f32)
    y = y * x0
    return rms_norm(y, norm_w).astype(p.dtype)


def chunked_gated_delta(q, k, v, g, beta):
    f32 = jnp.float32
    B, L, H, DK = q.shape
    DV = v.shape[-1]
    C = GDN_CHUNK
    N = L // C

    def chunks(t):
        return jnp.moveaxis(t.reshape(B, N, C, H, -1), 3, 1)

    qc = chunks(q) * (DK ** -0.5)
    kc = chunks(k)
    vc = chunks(v)
    bc = jnp.moveaxis(beta.reshape(B, N, C, H), 3, 1)
    gc = jnp.cumsum(jnp.moveaxis(g.reshape(B, N, C, H), 3, 1), axis=-1)
    incl = jnp.tril(jnp.ones((C, C), dtype=bool))
    strict = jnp.tril(jnp.ones((C, C), dtype=bool), -1)
    diff = gc[..., :, None] - gc[..., None, :]
    decay = jnp.where(incl, jnp.exp(jnp.where(incl, diff, 0.0)), 0.0)
    kb = kc * bc[..., None]
    a = jnp.where(strict, jnp.einsum('bhnid,bhnjd->bhnij', kb, kc) * decay, 0.0)
    eye = jnp.eye(C, dtype=f32)
    t_inv = lax.linalg.triangular_solve(eye + a, jnp.broadcast_to(eye, a.shape),
                                        left_side=True, lower=True, unit_diagonal=True)
    w = jnp.einsum('bhnij,bhnjd->bhnid', t_inv, kb * jnp.exp(gc)[..., None])
    u = jnp.einsum('bhnij,bhnjd->bhnid', t_inv, vc * bc[..., None])
    qk = jnp.where(incl, jnp.einsum('bhnid,bhnjd->bhnij', qc, kc) * decay, 0.0)
    q_dec = qc * jnp.exp(gc)[..., None]
    k_dec = kc * jnp.exp(gc[..., -1:] - gc)[..., None]
    chunk_decay = jnp.exp(gc[..., -1])

    def step(state, inp):
        qk_n, w_n, u_n, qd_n, kd_n, cd_n = inp
        v_new = u_n - jnp.einsum('bhcd,bhde->bhce', w_n, state)
        out = (jnp.einsum('bhcd,bhde->bhce', qd_n, state)
               + jnp.einsum('bhij,bhje->bhie', qk_n, v_new))
        state = state * cd_n[..., None, None] + jnp.einsum('bhcd,bhce->bhde', kd_n, v_new)
        return state, out

    xs = (jnp.moveaxis(qk, 2, 0), jnp.moveaxis(w, 2, 0), jnp.moveaxis(u, 2, 0),
          jnp.moveaxis(q_dec, 2, 0), jnp.moveaxis(k_dec, 2, 0), jnp.moveaxis(chunk_decay, 2, 0))
    s0 = jnp.zeros((B, H, DK, DV), f32)
    _, o = lax.scan(step, s0, xs)
    return o.transpose(1, 0, 3, 2, 4).reshape(B, L, H, DV)


def gdn_mixer(p, conv_w, a_log_f, a_log_b, dt_bias_f, dt_bias_b, norm_w):
    f32 = jnp.float32
    B, L, _ = p.shape
    H, Dh = GDN_HEADS, GDN_HEAD_DIM
    qkv = jax.nn.silu(centred_depthwise_conv(p[..., :3 * D_GDN], conv_w)).astype(f32)
    q, k, v = jnp.split(qkv, 3, axis=-1)
    q = l2_normalize(q.reshape(B, L, H, Dh))
    k = l2_normalize(k.reshape(B, L, H, Dh))
    v = v.reshape(B, L, H, Dh)
    z = p[..., 3 * D_GDN:4 * D_GDN].reshape(B, L, H, Dh).astype(f32)
    beta_f, beta_b, a_f, a_b = jnp.split(p[..., 4 * D_GDN:].astype(f32), 4, axis=-1)

    def log_decay(a, a_log, dt_bias):
        return -jnp.exp(a_log.astype(f32)) * jax.nn.softplus(a + dt_bias.astype(f32))

    def rev(t):
        return jnp.flip(t, axis=1)

    o_f = chunked_gated_delta(q, k, v, log_decay(a_f, a_log_f, dt_bias_f), jax.nn.sigmoid(beta_f))
    o_b = rev(chunked_gated_delta(rev(q), rev(k), rev(v),
                                  rev(log_decay(a_b, a_log_b, dt_bias_b)),
                                  rev(jax.nn.sigmoid(beta_b))))
    o = o_f + o_b
    o = (o * lax.rsqrt(jnp.mean(o * o, axis=-1, keepdims=True) + EPS)
         * norm_w.astype(f32) * jax.nn.silu(z))
    return o.reshape(B, L, D_GDN).astype(p.dtype)


def hierarchical_moe(u, rg_w, rg_b, re_w, re_b, w1, w3, w2):
    f32 = jnp.float32
    B, L, D = u.shape
    T = B * L
    xt = u.reshape(T, D)
    xf = xt.astype(f32)
    grp_logits = xf @ rg_w.astype(f32) + rg_b.astype(f32)
    grp_prob = jax.nn.softmax(grp_logits, axis=-1)
    grp_idx = jnp.argmax(grp_logits, axis=-1)
    grp_gate = jnp.take_along_axis(grp_prob, grp_idx[:, None], axis=-1)[:, 0]
    exp_logits = (xf @ re_w.astype(f32) + re_b.astype(f32)).reshape(T, N_GROUPS, EXPERTS_PER_GROUP)
    local_logits = jnp.take_along_axis(exp_logits, grp_idx[:, None, None], axis=1)[:, 0]
    local_prob = jax.nn.softmax(local_logits, axis=-1)
    top_p, top_i = lax.top_k(local_prob, TOP_K)
    top_p = top_p / jnp.sum(top_p, axis=-1, keepdims=True)
    expert_id = (grp_idx[:, None] * EXPERTS_PER_GROUP + top_i).astype(jnp.int32)
    gate = grp_gate[:, None] * top_p

    A = T * TOP_K
    flat_e = expert_id.reshape(A)
    flat_t = jnp.arange(A, dtype=jnp.int32) // TOP_K
    flat_g = gate.reshape(A)
    order = jnp.argsort(flat_e)
    se = flat_e[order]
    counts = jnp.bincount(flat_e, length=N_EXPERTS)
    padded = (counts + ROW_BLOCK - 1) // ROW_BLOCK * ROW_BLOCK
    pad_end = jnp.cumsum(padded)
    pad_start = pad_end - padded
    start = jnp.cumsum(counts) - counts
    dest = pad_start[se] + jnp.arange(A, dtype=jnp.int32) - start[se]
    n_blocks = -(-(A + N_EXPERTS * (ROW_BLOCK - 1)) // ROW_BLOCK)
    P = n_blocks * ROW_BLOCK
    slot_tok = jnp.full((P,), T, dtype=jnp.int32).at[dest].set(flat_t[order])
    slot_gate = jnp.zeros((P,), f32).at[dest].set(flat_g[order])
    block_start = jnp.arange(n_blocks, dtype=jnp.int32) * ROW_BLOCK
    block_expert = jnp.minimum(jnp.searchsorted(pad_end, block_start, side='right'), N_EXPERTS - 1)

    x_pad = jnp.concatenate([xt, jnp.zeros((1, D), xt.dtype)], axis=0)
    xb = x_pad[slot_tok].reshape(n_blocks, ROW_BLOCK, D)

    def run_block(args):
        xblk, e = args
        h = jax.nn.silu(xblk @ w1[e]) * (xblk @ w3[e])
        return h @ w2[e]

    yb = lax.map(run_block, (xb, block_expert)).reshape(P, D)
    y = jax.ops.segment_sum(yb * slot_gate[:, None].astype(yb.dtype), slot_tok, num_segments=T + 1)[:T]
    return y.reshape(B, L, D).astype(u.dtype)


def setup_inputs(seed: int = 0) -> dict:
    key = jax.random.key(seed)
    ks = jax.random.split(key, 32)
    f32 = jnp.float32

    def nrm(k, shape, s):
        return s * jax.random.normal(k, shape, f32)

    def gain(k, n):
        return 1.0 + 0.02 * jax.random.normal(k, (DEPTH, n), f32)

    def dt_bias(k):
        dt = jnp.exp(jax.random.uniform(k, (DEPTH, GDN_HEADS), f32, math.log(1e-3), math.log(1e-1)))
        return dt + jnp.log(-jnp.expm1(-dt))

    return {
        'x': jax.random.normal(ks[0], (BATCH, SEQ, D_MODEL), f32),
        'norm1_w': gain(ks[1], D_MODEL),
        'w_in': nrm(ks[2], (DEPTH, D_MODEL, N_PROJ), D_MODEL ** -0.5),
        'hy_conv_w': nrm(ks[3], (DEPTH, HYENA_SHORT_CONV, 3 * D_HYENA), HYENA_SHORT_CONV ** -0.5),
        'hy_conv_b': nrm(ks[4], (DEPTH, 3 * D_HYENA), 0.02),
        'hy_filt_w1': nrm(ks[5], (DEPTH, POS_EMB_DIM, FILTER_WIDTH), POS_EMB_DIM ** -0.5),
        'hy_filt_b1': nrm(ks[6], (DEPTH, FILTER_WIDTH), 0.1),
        'hy_filt_w2': nrm(ks[7], (DEPTH, FILTER_WIDTH, FILTER_WIDTH), FILTER_WIDTH ** -0.5),
        'hy_filt_b2': nrm(ks[8], (DEPTH, FILTER_WIDTH), 0.1),
        'hy_filt_w3': nrm(ks[9], (DEPTH, FILTER_WIDTH, FILTER_WIDTH), FILTER_WIDTH ** -0.5),
        'hy_filt_b3': nrm(ks[10], (DEPTH, FILTER_WIDTH), 0.1),
        'hy_filt_w4': nrm(ks[11], (DEPTH, FILTER_WIDTH, 2 * D_HYENA), FILTER_WIDTH ** -0.5),
        'hy_sin_freq': gain(ks[12], FILTER_WIDTH),
        'hy_skip': nrm(ks[13], (DEPTH, D_HYENA), 1.0),
        'hy_norm_w': gain(ks[14], D_HYENA),
        'gdn_conv_w': nrm(ks[15], (DEPTH, GDN_SHORT_CONV, 3 * D_GDN), GDN_SHORT_CONV ** -0.5),
        'gdn_a_log_f': jnp.log(jax.random.uniform(ks[16], (DEPTH, GDN_HEADS), f32, 1.0, 16.0)),
        'gdn_a_log_b': jnp.log(jax.random.uniform(ks[17], (DEPTH, GDN_HEADS), f32, 1.0, 16.0)),
        'gdn_dt_bias_f': dt_bias(ks[18]),
        'gdn_dt_bias_b': dt_bias(ks[19]),
        'gdn_norm_w': gain(ks[20], GDN_HEAD_DIM),
        'w_out': nrm(ks[21], (DEPTH, D_MIX, D_MODEL), D_MIX ** -0.5),
        'norm2_w': gain(ks[22], D_MODEL),
        'router_group_w': nrm(ks[23], (DEPTH, D_MODEL, N_GROUPS), D_MODEL ** -0.5),
        'router_group_b': nrm(ks[24], (DEPTH, N_GROUPS), 0.01),
        'router_expert_w': nrm(ks[25], (DEPTH, D_MODEL, N_EXPERTS), D_MODEL ** -0.5),
        'router_expert_b': nrm(ks[26], (DEPTH, N_EXPERTS), 0.01),
        'exp_w1': nrm(ks[27], (DEPTH, N_EXPERTS, D_MODEL, D_EXPERT), D_MODEL ** -0.5),
        'exp_w3': nrm(ks[28], (DEPTH, N_EXPERTS, D_MODEL, D_EXPERT), D_MODEL ** -0.5),
        'exp_w2': nrm(ks[29], (DEPTH, N_EXPERTS, D_EXPERT, D_MODEL), D_EXPERT ** -0.5),
        'final_norm_w': 1.0 + 0.02 * jax.random.normal(ks[30], (D_MODEL,), f32),
    }


def reference(x, norm1_w, w_in, hy_conv_w, hy_conv_b, hy_filt_w1, hy_filt_b1, hy_filt_w2, hy_filt_b2,
              hy_filt_w3, hy_filt_b3, hy_filt_w4, hy_sin_freq, hy_skip, hy_norm_w, gdn_conv_w,
              gdn_a_log_f, gdn_a_log_b, gdn_dt_bias_f, gdn_dt_bias_b, gdn_norm_w, w_out, norm2_w,
              router_group_w, router_group_b, router_expert_w, router_expert_b, exp_w1, exp_w3, exp_w2,
              final_norm_w):
    for l in range(DEPTH):
        h = rms_norm(x, norm1_w[l])
        p = h @ w_in[l]
        y_hy = hyena_mixer(p[..., :3 * D_HYENA], hy_conv_w[l], hy_conv_b[l],
                           hy_filt_w1[l], hy_filt_b1[l], hy_filt_w2[l], hy_filt_b2[l],
                           hy_filt_w3[l], hy_filt_b3[l], hy_filt_w4[l], hy_sin_freq[l],
                           hy_skip[l], hy_norm_w[l])
        y_gd = gdn_mixer(p[..., 3 * D_HYENA:], gdn_conv_w[l], gdn_a_log_f[l], gdn_a_log_b[l],
                         gdn_dt_bias_f[l], gdn_dt_bias_b[l], gdn_norm_w[l])
        x = x + jnp.concatenate([y_hy, y_gd], axis=-1) @ w_out[l]
        x = x + hierarchical_moe(rms_norm(x, norm2_w[l]), router_group_w[l], router_group_b[l],
                                 router_expert_w[l], router_expert_b[l],
                                 exp_w1[l], exp_w3[l], exp_w2[l])
    return rms_norm(x, final_norm_w)
```

```python
import functools
import math

import jax
import jax.numpy as jnp
import numpy as np
from jax import lax
from jax.experimental import pallas as pl
from jax.experimental.pallas import tpu as pltpu

F32 = jnp.float32
BF16 = jnp.bfloat16
EPS = 1e-6
LANES = 128
SUBLANES = 8
VMEM_BYTES_V7X = 64 * 1024 * 1024
GDN_CHUNK = 64
FFT_N2 = 128
EXPERT_ROWS = 256
DECAY_TARGET = 1e-2
FAST_DECAY_PCT = 0.3
SLOW_DECAY_PCT = 1.5


def _cparams(sem, vmem_mb):
    return pltpu.CompilerParams(dimension_semantics=sem, vmem_limit_bytes=int(vmem_mb * 1024 * 1024))


def _dot(a, b):
    return jnp.dot(a.astype(BF16), b.astype(BF16), preferred_element_type=F32)


def _dot_nt(a, b):
    return lax.dot_general(a.astype(BF16), b.astype(BF16), (((1,), (1,)), ((), ())), preferred_element_type=F32)


def _dot_tn(a, b):
    return lax.dot_general(a.astype(BF16), b.astype(BF16), (((0,), (0,)), ((), ())), preferred_element_type=F32)


def _dot_hi(a, b):
    return jnp.dot(a, b, preferred_element_type=F32, precision=lax.Precision.HIGHEST)


def _silu(x):
    return x * jax.nn.sigmoid(x)


def _inproj_kernel(x_ref, nw_ref, w_ref, wg_ref, p_ref, g_ref, h_scr):
    @pl.when(pl.program_id(1) == 0)
    def _():
        x = x_ref[...]
        h = x * lax.rsqrt(jnp.mean(x * x, axis=-1, keepdims=True) + EPS) * nw_ref[...]
        h_scr[...] = h.astype(BF16)
        g_ref[...] = _dot_hi(h, wg_ref[...])

    p_ref[...] = jnp.dot(h_scr[...], w_ref[...], preferred_element_type=F32)


def _inproj(xf, norm_w, w_main, w_gate, tm=1024, tn=512):
    M, D = xf.shape
    n_main = w_main.shape[1]
    return pl.pallas_call(
        _inproj_kernel,
        grid=(M // tm, n_main // tn),
        in_specs=[
            pl.BlockSpec((tm, D), lambda i, j: (i, 0)),
            pl.BlockSpec((1, D), lambda i, j: (0, 0)),
            pl.BlockSpec((D, tn), lambda i, j: (0, j)),
            pl.BlockSpec((D, LANES), lambda i, j: (0, 0)),
        ],
        out_specs=[
            pl.BlockSpec((tm, tn), lambda i, j: (i, j)),
            pl.BlockSpec((tm, LANES), lambda i, j: (i, 0)),
        ],
        out_shape=[jax.ShapeDtypeStruct((M, n_main), F32), jax.ShapeDtypeStruct((M, LANES), F32)],
        scratch_shapes=[pltpu.VMEM((tm, D), BF16)],
        compiler_params=_cparams(("parallel", "arbitrary"), 48),
        name="inproj",
    )(xf, norm_w.reshape(1, D), w_main, w_gate)


def _conv3_rows(ref, r0, rows, w, n_rows):
    cur = ref[pl.ds(r0, rows), :]
    lo = jnp.maximum(r0 - SUBLANES, 0)
    hi = jnp.minimum(r0 + rows, n_rows - SUBLANES)
    prev8 = ref[pl.ds(pl.multiple_of(lo, SUBLANES), SUBLANES), :]
    next8 = ref[pl.ds(pl.multiple_of(hi, SUBLANES), SUBLANES), :]
    prev_row = jnp.where(r0 > 0, prev8[SUBLANES - 1:SUBLANES, :], 0.0)
    next_row = jnp.where(r0 + rows < n_rows, next8[0:1, :], 0.0)
    row = lax.broadcasted_iota(jnp.int32, cur.shape, 0)
    xm = jnp.where(row == 0, prev_row, pltpu.roll(cur, 1, 0))
    xp = jnp.where(row == rows - 1, next_row, pltpu.roll(cur, rows - 1, 0))
    return xm * w[0:1, :] + cur * w[1:2, :] + xp * w[2:3, :]


CONV_ROWS = 256


def _hy_pre_kernel(x0_ref, x1_ref, v_ref, w0_ref, w1_ref, w2_ref, b0_ref, b1_ref, b2_ref, x0c_ref, vg_ref):
    L = x0_ref.shape[0]
    w0, w1, w2 = w0_ref[...], w1_ref[...], w2_ref[...]
    b0, b1, b2 = b0_ref[...], b1_ref[...], b2_ref[...]

    def body(c, carry):
        r0 = pl.multiple_of(c * CONV_ROWS, CONV_ROWS)
        x0c_ref[pl.ds(r0, CONV_ROWS), :] = _conv3_rows(x0_ref, r0, CONV_ROWS, w0, L) + b0
        x1c = _conv3_rows(x1_ref, r0, CONV_ROWS, w1, L) + b1
        vc = _conv3_rows(v_ref, r0, CONV_ROWS, w2, L) + b2
        vg_ref[pl.ds(r0, CONV_ROWS), :] = vc * x1c
        return carry

    lax.fori_loop(0, L // CONV_ROWS, body, 0)


def _hy_pre(p3, conv_w, conv_b, d_hy):
    B, L, _ = p3.shape
    nt = d_hy // LANES
    bias = conv_b.reshape(1, -1)
    pspec = lambda off: pl.BlockSpec((None, L, LANES), lambda b, c: (b, 0, c + off))
    wspec = lambda off: pl.BlockSpec((3, LANES), lambda b, c: (0, c + off))
    bspec = lambda off: pl.BlockSpec((1, LANES), lambda b, c: (0, c + off))
    ospec = pl.BlockSpec((None, L, LANES), lambda b, c: (b, 0, c))
    return pl.pallas_call(
        _hy_pre_kernel,
        grid=(B, nt),
        in_specs=[pspec(0), pspec(nt), pspec(2 * nt), wspec(0), wspec(nt), wspec(2 * nt),
                  bspec(0), bspec(nt), bspec(2 * nt)],
        out_specs=[ospec, ospec],
        out_shape=[jax.ShapeDtypeStruct((B, L, d_hy), F32)] * 2,
        compiler_params=_cparams(("parallel", "parallel"), 40),
        name="hy_pre",
    )(p3, p3, p3, conv_w, conv_w, conv_w, bias, bias, bias)


def _gdn_pre_kernel(x_ref, w_ref, o_ref, *, n_heads, head_dim):
    L = x_ref.shape[0]
    w = w_ref[...]
    c = pl.program_id(1)
    q_scale = jnp.where(c < n_heads, head_dim ** -0.5, 1.0)
    is_qk = c < 2 * n_heads

    def body(i, carry):
        r0 = pl.multiple_of(i * CONV_ROWS, CONV_ROWS)
        y = _silu(_conv3_rows(x_ref, r0, CONV_ROWS, w, L))
        inv = lax.rsqrt(jnp.sum(y * y, axis=-1, keepdims=True) + EPS) * q_scale
        o_ref[pl.ds(r0, CONV_ROWS), :] = y * jnp.where(is_qk, inv, 1.0)
        return carry

    lax.fori_loop(0, L // CONV_ROWS, body, 0)


def _gdn_pre(p3, conv_w, col0, n_heads, head_dim):
    B, L, _ = p3.shape
    assert head_dim == LANES
    nt = 3 * n_heads
    off = col0 // LANES
    return pl.pallas_call(
        functools.partial(_gdn_pre_kernel, n_heads=n_heads, head_dim=head_dim),
        grid=(B, nt),
        in_specs=[pl.BlockSpec((None, L, LANES), lambda b, c: (b, 0, c + off)),
                  pl.BlockSpec((3, LANES), lambda b, c: (0, c))],
        out_specs=pl.BlockSpec((None, L, LANES), lambda b, c: (b, 0, c)),
        out_shape=jax.ShapeDtypeStruct((B, L, nt * LANES), F32),
        compiler_params=_cparams(("parallel", "parallel"), 24),
        name="gdn_pre",
    )(p3, conv_w)


GATE_ROWS = 512


def _gdn_gates_kernel(pg_ref, alog_ref, dtb_ref, o_ref, *, n_heads):
    H = n_heads
    x = pg_ref[...]
    beta = jax.nn.sigmoid(x)
    z = x + dtb_ref[...]
    softplus = jnp.maximum(z, 0.0) + jnp.log1p(jnp.exp(-jnp.abs(z)))
    g = -jnp.exp(alog_ref[...]) * softplus
    ii = lax.broadcasted_iota(jnp.int32, (GATE_ROWS, GATE_ROWS), 0)
    jj = lax.broadcasted_iota(jnp.int32, (GATE_ROWS, GATE_ROWS), 1)
    same = (ii // GDN_CHUNK) == (jj // GDN_CHUNK)
    m_fwd = jnp.where(same & (jj <= ii), 1.0, 0.0)
    m_bwd = jnp.where(same & (jj >= ii), 1.0, 0.0)
    m_all = jnp.where(same, 1.0, 0.0)
    gc_f = _dot_hi(m_fwd, g)
    gc_b = _dot_hi(m_bwd, g)
    g_tot = pltpu.roll(_dot_hi(m_all, g), 4 * H, 1)
    lane = lax.broadcasted_iota(jnp.int32, x.shape, 1)
    out = jnp.where(lane < 2 * H, beta,
                    jnp.where(lane < 3 * H, gc_f,
                              jnp.where(lane < 4 * H, gc_b,
                                        jnp.where((lane >= 6 * H) & (lane < 8 * H), g_tot, 0.0))))
    o_ref[...] = out


def _gdn_gates(pg, a_log_f, a_log_b, dt_bias_f, dt_bias_b, n_heads):
    M = pg.shape[0]
    H = n_heads
    assert 8 * H <= LANES
    pad = lambda a, b: jnp.concatenate([jnp.zeros((2 * H,), F32), a.astype(F32), b.astype(F32),
                                        jnp.zeros((LANES - 4 * H,), F32)]).reshape(1, LANES)
    return pl.pallas_call(
        functools.partial(_gdn_gates_kernel, n_heads=H),
        grid=(M // GATE_ROWS,),
        in_specs=[pl.BlockSpec((GATE_ROWS, LANES), lambda i: (i, 0)),
                  pl.BlockSpec((1, LANES), lambda i: (0, 0)),
                  pl.BlockSpec((1, LANES), lambda i: (0, 0))],
        out_specs=pl.BlockSpec((GATE_ROWS, LANES), lambda i: (i, 0)),
        out_shape=jax.ShapeDtypeStruct((M, LANES), F32),
        compiler_params=_cparams(("parallel",), 24),
        name="gdn_gates",
    )(pg, pad(a_log_f, a_log_b), pad(dt_bias_f, dt_bias_b))


def _unit_tri_inverse(a):
    C = a.shape[0]
    ii = lax.broadcasted_iota(jnp.int32, (C, C), 0)
    jj = lax.broadcasted_iota(jnp.int32, (C, C), 1)
    eye = jnp.where(ii == jj, 1.0, 0.0)
    m = -a
    r = eye + m
    for _ in range(int(math.log2(C)) - 1):
        m = _dot(m, m)
        r = r + _dot(r, m)
    return r


def _delta_chunk(q, k, v, beta, gc_col, gc_row, gtot, state, lower):
    C = q.shape[0]
    ii = lax.broadcasted_iota(jnp.int32, (C, C), 0)
    jj = lax.broadcasted_iota(jnp.int32, (C, C), 1)
    incl = (ii >= jj) if lower else (ii <= jj)
    strict = (ii > jj) if lower else (ii < jj)
    diff = gc_col - gc_row
    decay = jnp.where(incl, jnp.exp(jnp.where(incl, diff, 0.0)), 0.0)
    kb = k * beta
    a = jnp.where(strict, _dot_nt(kb, k) * decay, 0.0)
    t_inv = _unit_tri_inverse(a)
    eg = jnp.exp(gc_col)
    w = _dot(t_inv, kb * eg)
    u = _dot(t_inv, v * beta)
    qk = jnp.where(incl, _dot_nt(q, k) * decay, 0.0)
    v_new = u - _dot(w, state)
    out = _dot(q * eg, state) + _dot(qk, v_new)
    k_dec = k * jnp.exp(gtot - gc_col)
    new_state = state * jnp.exp(gtot[0:1, :]) + _dot_tn(k_dec, v_new)
    return out, new_state


def _gdn_scan_kernel(qf_ref, kf_ref, vf_ref, qb_ref, kb_ref, vb_ref, gf_ref, gb_ref, rf_ref, rb_ref,
                     of_ref, ob_ref, s_scr, *, n_heads, head_dim):
    H, Dh = n_heads, head_dim

    @pl.when(pl.program_id(1) == 0)
    def _():
        s_scr[...] = jnp.zeros_like(s_scr)

    gf = gf_ref[...]
    gb = gb_ref[...]
    for h in range(H):
        sl = slice(h * Dh, (h + 1) * Dh)
        col = lambda g, j: g[:, j:j + 1]
        out, s_new = _delta_chunk(qf_ref[:, sl], kf_ref[:, sl], vf_ref[:, sl],
                                  col(gf, h), col(gf, 2 * H + h), rf_ref[h:h + 1, :], col(gf, 6 * H + h),
                                  s_scr[0, h], True)
        of_ref[:, sl] = out
        s_scr[0, h] = s_new
        out, s_new = _delta_chunk(qb_ref[:, sl], kb_ref[:, sl], vb_ref[:, sl],
                                  col(gb, H + h), col(gb, 3 * H + h), rb_ref[H + h:H + h + 1, :], col(gb, 7 * H + h),
                                  s_scr[1, h], False)
        ob_ref[:, sl] = out
        s_scr[1, h] = s_new


def _gdn_scan(qkv, gates, gates_row, n_heads, head_dim):
    B, L, _ = qkv.shape
    H, Dh = n_heads, head_dim
    d = H * Dh
    C = GDN_CHUNK
    N = L // C
    fwd = lambda col: pl.BlockSpec((None, C, d), lambda b, n: (b, n, col))
    bwd = lambda col: pl.BlockSpec((None, C, d), lambda b, n: (b, N - 1 - n, col))
    return pl.pallas_call(
        functools.partial(_gdn_scan_kernel, n_heads=H, head_dim=Dh),
        grid=(B, N),
        in_specs=[fwd(0), fwd(1), fwd(2), bwd(0), bwd(1), bwd(2),
                  pl.BlockSpec((None, C, LANES), lambda b, n: (b, n, 0)),
                  pl.BlockSpec((None, C, LANES), lambda b, n: (b, N - 1 - n, 0)),
                  pl.BlockSpec((None, None, 2 * H, C), lambda b, n: (b, n, 0, 0)),
                  pl.BlockSpec((None, None, 2 * H, C), lambda b, n: (b, N - 1 - n, 0, 0))],
        out_specs=[pl.BlockSpec((None, C, d), lambda b, n: (b, n, 0)),
                   pl.BlockSpec((None, C, d), lambda b, n: (b, N - 1 - n, 0))],
        out_shape=[jax.ShapeDtypeStruct((B, L, d), F32)] * 2,
        scratch_shapes=[pltpu.VMEM((2, H, Dh, Dh), F32)],
        compiler_params=_cparams(("parallel", "arbitrary"), 32),
        name="gdn_scan",
    )(qkv, qkv, qkv, qkv, qkv, qkv, gates, gates, gates_row, gates_row)


def _gdn_branch(p3, pg, conv_w, a_log_f, a_log_b, dt_bias_f, dt_bias_b, col0, n_heads, head_dim):
    B, L, _ = p3.shape
    H = n_heads
    qkv = _gdn_pre(p3, conv_w, col0, H, head_dim)
    gates = _gdn_gates(pg, a_log_f, a_log_b, dt_bias_f, dt_bias_b, H).reshape(B, L, LANES)
    N = L // GDN_CHUNK
    gates_row = gates[..., 2 * H:4 * H].reshape(B, N, GDN_CHUNK, 2 * H).transpose(0, 1, 3, 2)
    return _gdn_scan(qkv, gates, gates_row, H, head_dim)


@functools.lru_cache(maxsize=None)
def _filter_positions(L, pos_emb_dim):
    n = 2 * L
    r = np.arange(n)
    k = np.where(r < L, r, np.where(r == L, 0, n - r)).astype(np.float64)
    t = k / (L - 1)
    bands = (pos_emb_dim - 1) // 2
    fb = np.linspace(1e-4, bands - 1, bands)
    ang = (2.0 * math.pi / L) * k[:, None] * fb[None, :]
    z = np.concatenate([t[:, None], np.cos(ang), -np.sin(ang)], axis=-1)
    return z.astype(np.float32)


@functools.lru_cache(maxsize=None)
def _decay_rates(d_hy):
    max_decay = math.log(DECAY_TARGET) / FAST_DECAY_PCT
    min_decay = math.log(DECAY_TARGET) / SLOW_DECAY_PCT
    return np.abs(np.linspace(min_decay, max_decay, d_hy)).astype(np.float32).reshape(1, d_hy)


def _filt_mlp_kernel(z_ref, w1_ref, b1_ref, w2_ref, b2_ref, w3_ref, b3_ref, fr_ref, o_ref):
    fr = fr_ref[...]
    h = jnp.sin(fr * (_dot_hi(z_ref[...], w1_ref[...]) + b1_ref[...]))
    h = jnp.sin(fr * (_dot_hi(h, w2_ref[...]) + b2_ref[...]))
    o_ref[...] = jnp.sin(fr * (_dot_hi(h, w3_ref[...]) + b3_ref[...]))


def _filt_mlp(z, w1, b1, w2, b2, w3, b3, freq, tr=1024):
    n, pe = z.shape
    fw = w1.shape[1]
    row = lambda a: a.reshape(1, -1).astype(F32)
    full = lambda a: pl.BlockSpec(a.shape, lambda i: (0, 0))
    args = (z, w1, row(b1), w2, row(b2), w3, row(b3), row(freq))
    return pl.pallas_call(
        _filt_mlp_kernel,
        grid=(n // tr,),
        in_specs=[pl.BlockSpec((tr, pe), lambda i: (i, 0))] + [full(a) for a in args[1:]],
        out_specs=pl.BlockSpec((tr, fw), lambda i: (i, 0)),
        out_shape=jax.ShapeDtypeStruct((n, fw), F32),
        compiler_params=_cparams(("parallel",), 24),
        name="filt_mlp",
    )(*args)


@functools.lru_cache(maxsize=None)
def _dft_tables(L):
    n = 2 * L
    N2 = FFT_N2
    N1 = n // N2
    N1h = N1 // 2
    j2 = np.arange(N2)[:, None, None]
    k1 = np.arange(N1)[None, :, None]

    def stage1(n_j1):
        j1 = np.arange(n_j1)[None, None, :]
        m = (k1 * (N2 * j1 + j2)) % n
        th = 2.0 * np.pi * m / n
        return np.cos(th), np.sin(th)

    c, s = stage1(N1h)
    t1 = np.concatenate([np.concatenate([c, s], axis=2), np.concatenate([-s, c], axis=2)], axis=1)
    c, s = stage1(N1)
    t1g = np.concatenate([c, -s], axis=1)
    c, s = stage1(N1h)
    ct, st = np.swapaxes(c, 1, 2) / n, np.swapaxes(s, 1, 2) / n
    t2 = np.concatenate([np.concatenate([ct, -st], axis=2), np.concatenate([st, ct], axis=2)], axis=1)
    a = np.arange(N2)
    th = 2.0 * np.pi * ((a[:, None] * a[None, :]) % N2) / N2
    c2, s2 = np.cos(th), np.sin(th)
    f2f = np.block([[c2, s2], [-s2, c2]])
    f2i = np.block([[c2, -s2], [s2, c2]])
    as_bf16 = lambda x: jnp.asarray(x, dtype=F32).astype(BF16)
    return dict(N1=N1, N2=N2, t1=t1.astype(np.float32), t1g=t1g.astype(np.float32), t2=t2.astype(np.float32),
                f2f=f2f.astype(np.float32), f2i=f2i.astype(np.float32))


FILT_ROWS = 512
FFT_UNROLL = 4


def _filt_fft_kernel(h3_ref, w4f_ref, w4b_ref, delta_ref, t1g_ref, f2f_ref, hspec_ref, hb0_ref, g_scr, a_scr,
                     *, L, N1, N2):
    n = 2 * L
    delta = delta_ref[...]
    hb0_ref[...] = jnp.zeros_like(hb0_ref)

    def gen(c, carry):
        r0 = pl.multiple_of(c * FILT_ROWS, FILT_ROWS)
        row = r0 + lax.broadcasted_iota(jnp.int32, (FILT_ROWS, LANES), 0)
        lag = jnp.where(row < L, row, jnp.where(row == L, 0, n - row))
        window = jnp.exp(-(lag.astype(F32) * (1.0 / (L - 1))) * delta)
        w4 = jnp.where(r0 < L, w4f_ref[...], w4b_ref[...])
        g = _dot(h3_ref[pl.ds(r0, FILT_ROWS), :], w4) * window
        at_l = row == L
        hb0_ref[...] += jnp.sum(jnp.where(at_l, g, 0.0), axis=0, keepdims=True)
        g_scr[pl.ds(r0, FILT_ROWS), :] = jnp.where(at_l, 0.0, g)
        return carry

    lax.fori_loop(0, n // FILT_ROWS, gen, 0)

    def stage1(j2, carry):
        x = g_scr[pl.ds(j2, N1, stride=N2), :]
        a_scr[pl.ds(pl.multiple_of(j2 * 2 * N1, 2 * N1), 2 * N1), :] = _dot(t1g_ref[j2], x)
        return carry

    lax.fori_loop(0, N2, stage1, 0, unroll=FFT_UNROLL)

    def stage2(k1, carry):
        ar = a_scr[pl.ds(k1, N2, stride=2 * N1), :]
        ai = a_scr[pl.ds(N1 + k1, N2, stride=2 * N1), :]
        z = _dot(f2f_ref[...], jnp.concatenate([ar, ai], axis=0))
        hspec_ref[pl.ds(pl.multiple_of(k1 * 2 * N2, 2 * N2), 2 * N2), :] = z.astype(hspec_ref.dtype)
        return carry

    lax.fori_loop(0, N1, stage2, 0, unroll=FFT_UNROLL)


def _filt_fft(h3, w4, L, d_hy):
    tb = _dft_tables(L)
    N1, N2 = tb["N1"], tb["N2"]
    n = 2 * L
    fw = h3.shape[1]
    nt = d_hy // LANES
    t1g = jnp.asarray(tb["t1g"]).astype(BF16)
    f2f = jnp.asarray(tb["f2f"]).astype(BF16)
    return pl.pallas_call(
        functools.partial(_filt_fft_kernel, L=L, N1=N1, N2=N2),
        grid=(nt,),
        in_specs=[pl.BlockSpec((n, fw), lambda c: (0, 0)),
                  pl.BlockSpec((fw, LANES), lambda c: (0, c)),
                  pl.BlockSpec((fw, LANES), lambda c: (0, c + nt)),
                  pl.BlockSpec((1, LANES), lambda c: (0, c)),
                  pl.BlockSpec(t1g.shape, lambda c: (0, 0, 0)),
                  pl.BlockSpec(f2f.shape, lambda c: (0, 0))],
        out_specs=[pl.BlockSpec((2 * n, LANES), lambda c: (0, c)),
                   pl.BlockSpec((SUBLANES, LANES), lambda c: (0, c))],
        out_shape=[jax.ShapeDtypeStruct((2 * n, d_hy), BF16), jax.ShapeDtypeStruct((SUBLANES, d_hy), F32)],
        scratch_shapes=[pltpu.VMEM((n, LANES), F32), pltpu.VMEM((2 * n, LANES), F32)],
        compiler_params=_cparams(("parallel",), 48),
        name="filt_fft",
    )(h3, w4, w4, jnp.asarray(_decay_rates(d_hy)), t1g, f2f)


def _hy_conv_kernel(vg_ref, hspec_ref, skip_ref, hb0_ref, t1_ref, f2f_ref, f2i_ref, t2_ref, y_ref, a_scr, b_scr,
                    *, N1, N2):
    N1h = N1 // 2

    def stage1(j2, carry):
        x = jnp.concatenate([vg_ref[0, pl.ds(j2, N1h, stride=N2), :],
                             vg_ref[1, pl.ds(j2, N1h, stride=N2), :]], axis=0)
        a_scr[pl.ds(pl.multiple_of(j2 * 2 * N1, 2 * N1), 2 * N1), :] = _dot(t1_ref[j2], x)
        return carry

    lax.fori_loop(0, N2, stage1, 0, unroll=FFT_UNROLL)

    def stage2(k1, carry):
        ar = a_scr[pl.ds(k1, N2, stride=2 * N1), :]
        ai = a_scr[pl.ds(N1 + k1, N2, stride=2 * N1), :]
        z = _dot(f2f_ref[...], jnp.concatenate([ar, ai], axis=0))
        zr, zi = z[:N2], z[N2:]
        base = pl.multiple_of(k1 * 2 * N2, 2 * N2)
        hr = hspec_ref[pl.ds(base, N2), :].astype(F32)
        hi = hspec_ref[pl.ds(base + N2, N2), :].astype(F32)
        prod = jnp.concatenate([zr * hr - zi * hi, zr * hi + zi * hr], axis=0)
        b_scr[pl.ds(base, 2 * N2), :] = _dot(f2i_ref[...], prod)
        return carry

    lax.fori_loop(0, N1, stage2, 0, unroll=FFT_UNROLL)

    skip = skip_ref[...] + hb0_ref[0:1, :]

    def stage3(j2, carry):
        b = jnp.concatenate([b_scr[pl.ds(j2, N1, stride=2 * N2), :],
                             b_scr[pl.ds(N2 + j2, N1, stride=2 * N2), :]], axis=0)
        y = _dot(t2_ref[j2], b)
        y_ref[0, pl.ds(j2, N1h, stride=N2), :] = y[:N1h] + vg_ref[0, pl.ds(j2, N1h, stride=N2), :] * skip
        y_ref[1, pl.ds(j2, N1h, stride=N2), :] = y[N1h:] + vg_ref[1, pl.ds(j2, N1h, stride=N2), :] * skip
        return carry

    lax.fori_loop(0, N2, stage3, 0, unroll=FFT_UNROLL)


def _hy_conv(vg, hspec, skip, hb0):
    B, L, d_hy = vg.shape
    assert B % 2 == 0
    tb = _dft_tables(L)
    N1, N2 = tb["N1"], tb["N2"]
    n = 2 * L
    nt = d_hy // LANES
    t1, t2 = (jnp.asarray(tb[k]).astype(BF16) for k in ("t1", "t2"))
    f2f, f2i = (jnp.asarray(tb[k]).astype(BF16) for k in ("f2f", "f2i"))
    const3 = lambda a: pl.BlockSpec(a.shape, lambda b, c: (0, 0, 0))
    const2 = lambda a: pl.BlockSpec(a.shape, lambda b, c: (0, 0))
    return pl.pallas_call(
        functools.partial(_hy_conv_kernel, N1=N1, N2=N2),
        grid=(B // 2, nt),
        in_specs=[pl.BlockSpec((2, L, LANES), lambda b, c: (b, 0, c)),
                  pl.BlockSpec((2 * n, LANES), lambda b, c: (0, c)),
                  pl.BlockSpec((1, LANES), lambda b, c: (0, c)),
                  pl.BlockSpec((SUBLANES, LANES), lambda b, c: (0, c)),
                  const3(t1), const2(f2f), const2(f2i), const3(t2)],
        out_specs=pl.BlockSpec((2, L, LANES), lambda b, c: (b, 0, c)),
        out_shape=jax.ShapeDtypeStruct((B, L, d_hy), F32),
        scratch_shapes=[pltpu.VMEM((2 * n, LANES), F32), pltpu.VMEM((2 * n, LANES), F32)],
        compiler_params=_cparams(("parallel", "parallel"), 56),
        name="hy_conv",
    )(vg, hspec, skip.reshape(1, d_hy).astype(F32), hb0, t1, f2f, f2i, t2)


def _hyena_branch(p3, conv_w, conv_b, fw1, fb1, fw2, fb2, fw3, fb3, fw4, freq, skip, d_hy):
    B, L, _ = p3.shape
    x0c, vg = _hy_pre(p3, conv_w, conv_b, d_hy)
    z = jnp.asarray(_filter_positions(L, fw1.shape[0]))
    h3 = _filt_mlp(z, fw1, fb1, fw2, fb2, fw3, fb3, freq)
    hspec, hb0 = _filt_fft(h3, fw4, L, d_hy)
    return x0c, _hy_conv(vg, hspec, skip, hb0)


ROUTE_GATE, ROUTE_EXPERT, ROUTE_RANK = 0, 2, 4


def _mix_route_kernel(yc_ref, x0_ref, of_ref, ob_ref, z_ref, hnw_ref, gnw_ref, x_ref, wo_ref, n2w_ref, wr_ref, br_ref,
                      x2_ref, u_ref, route_ref, cnt_ref, run_scr, *, n_heads, head_dim, n_groups, per_group):
    tm = x_ref.shape[0]
    d_hy = yc_ref.shape[1]

    @pl.when(pl.program_id(0) == 0)
    def _():
        run_scr[...] = jnp.zeros_like(run_scr)

    yh = yc_ref[...] * x0_ref[...]
    yh = yh * lax.rsqrt(jnp.mean(yh * yh, axis=-1, keepdims=True) + EPS) * hnw_ref[...]
    parts = [yh.astype(BF16)]
    for h in range(n_heads):
        sl = slice(h * head_dim, (h + 1) * head_dim)
        o = of_ref[:, sl] + ob_ref[:, sl]
        o = o * lax.rsqrt(jnp.mean(o * o, axis=-1, keepdims=True) + EPS) * gnw_ref[...] * _silu(z_ref[:, sl])
        parts.append(o.astype(BF16))
    ymix = jnp.concatenate(parts, axis=-1)
    x2 = x_ref[...] + jnp.dot(ymix, wo_ref[...], preferred_element_type=F32)
    x2_ref[...] = x2
    u = x2 * lax.rsqrt(jnp.mean(x2 * x2, axis=-1, keepdims=True) + EPS) * n2w_ref[...]
    u_ref[...] = u

    G, P = n_groups, per_group
    logits = _dot_hi(u, wr_ref[...]) + br_ref[...]
    lane = lax.broadcasted_iota(jnp.int32, logits.shape, 1)
    neg = jnp.float32(-jnp.inf)
    big = jnp.int32(4 * LANES)
    first = lambda hit: jnp.min(jnp.where(hit, lane, big), axis=-1, keepdims=True)
    gl = jnp.where(lane < G, logits, neg)
    gmax = jnp.max(gl, axis=-1, keepdims=True)
    gidx = first(gl == gmax)
    grp_gate = 1.0 / jnp.sum(jnp.exp(gl - gmax), axis=-1, keepdims=True)
    in_grp = (lane >= G) & (lane < G + G * P) & (((lane - G) // P) == gidx)
    ll = jnp.where(in_grp, logits, neg)
    m1 = jnp.max(ll, axis=-1, keepdims=True)
    i1 = first(ll == m1)
    denom = jnp.sum(jnp.exp(ll - m1), axis=-1, keepdims=True)
    ll2 = jnp.where(lane == i1, neg, ll)
    m2 = jnp.max(ll2, axis=-1, keepdims=True)
    i2 = first(ll2 == m2)
    p1 = 1.0 / denom
    p2 = jnp.exp(m2 - m1) / denom
    gate1 = grp_gate * (p1 / (p1 + p2))
    gate2 = grp_gate * (p2 / (p1 + p2))
    e1 = i1 - G
    e2 = i2 - G

    oh1 = jnp.where(lane == e1, 1.0, 0.0)
    oh2 = jnp.where(lane == e2, 1.0, 0.0)
    oh = oh1 + oh2
    ii = lax.broadcasted_iota(jnp.int32, (tm, tm), 0)
    jj = lax.broadcasted_iota(jnp.int32, (tm, tm), 1)
    before = _dot(jnp.where(ii > jj, 1.0, 0.0), oh) + run_scr[0:1, :]
    r1 = jnp.sum(oh1 * before, axis=-1, keepdims=True)
    r2 = jnp.sum(oh2 * before, axis=-1, keepdims=True)
    run_scr[...] = run_scr[...] + jnp.sum(oh, axis=0, keepdims=True)
    cnt_ref[...] = run_scr[...]

    rec = jnp.where(lane == ROUTE_GATE, gate1, 0.0)
    rec = jnp.where(lane == ROUTE_GATE + 1, gate2, rec)
    rec = jnp.where(lane == ROUTE_EXPERT, e1.astype(F32), rec)
    rec = jnp.where(lane == ROUTE_EXPERT + 1, e2.astype(F32), rec)
    rec = jnp.where(lane == ROUTE_RANK, r1, rec)
    rec = jnp.where(lane == ROUTE_RANK + 1, r2, rec)
    route_ref[...] = rec


def _mix_route(yconv, x0c, o_f, o_b, p, z_col, hy_norm_w, gdn_norm_w, xf, w_out_bf16, norm2_w, wr, br,
               n_heads, head_dim, n_groups, per_group, tm=256):
    M, D = xf.shape
    d_hy = yconv.shape[1]
    d_gdn = o_f.shape[1]
    assert z_col % d_gdn == 0 and n_groups * (per_group + 1) <= LANES
    zb = z_col // d_gdn
    row = lambda i: (i, 0)
    const = lambda i: (0, 0)
    kern = functools.partial(_mix_route_kernel, n_heads=n_heads, head_dim=head_dim,
                             n_groups=n_groups, per_group=per_group)
    return pl.pallas_call(
        kern,
        grid=(M // tm,),
        in_specs=[pl.BlockSpec((tm, d_hy), row), pl.BlockSpec((tm, d_hy), row),
                  pl.BlockSpec((tm, d_gdn), row), pl.BlockSpec((tm, d_gdn), row),
                  pl.BlockSpec((tm, d_gdn), lambda i: (i, zb)),
                  pl.BlockSpec((1, d_hy), const), pl.BlockSpec((1, head_dim), const),
                  pl.BlockSpec((tm, D), row), pl.BlockSpec(w_out_bf16.shape, const),
                  pl.BlockSpec((1, D), const), pl.BlockSpec((D, LANES), const), pl.BlockSpec((1, LANES), const)],
        out_specs=[pl.BlockSpec((tm, D), row), pl.BlockSpec((tm, D), row),
                   pl.BlockSpec((tm, LANES), row), pl.BlockSpec((SUBLANES, LANES), const)],
        out_shape=[jax.ShapeDtypeStruct((M, D), F32), jax.ShapeDtypeStruct((M, D), F32),
                   jax.ShapeDtypeStruct((M, LANES), F32), jax.ShapeDtypeStruct((SUBLANES, LANES), F32)],
        scratch_shapes=[pltpu.VMEM((SUBLANES, LANES), F32)],
        compiler_params=_cparams(("arbitrary",), 48),
        name="mix_route",
    )(yconv, x0c, o_f, o_b, p, hy_norm_w.reshape(1, d_hy), gdn_norm_w.reshape(1, head_dim), xf, w_out_bf16,
      norm2_w.reshape(1, D), wr, br)


def _experts_kernel(te_ref, nv_ref, na_ref, tgt_ref, u_hbm, w1_ref, w3_ref, w2_ref, out_hbm,
                    xbuf, ybuf, w1b, w3b, w2b, gsem, ssem, *, n_tokens):
    i = pl.program_id(0)
    nv = nv_ref[i]

    @pl.when(i == 0)
    def _():
        xbuf[...] = jnp.zeros_like(xbuf)

    def gather(r):
        t = tgt_ref[r]
        tok = jnp.where(t >= n_tokens, t - n_tokens, t)
        return pltpu.make_async_copy(u_hbm.at[pl.ds(tok, 1)], xbuf.at[pl.ds(r, 1)], gsem)

    def scatter(r):
        return pltpu.make_async_copy(ybuf.at[pl.ds(r, 1)], out_hbm.at[pl.ds(tgt_ref[r], 1)], ssem)

    def each(fn):
        def body(r, carry):
            fn(r)
            return carry
        lax.fori_loop(0, nv, body, 0)

    @pl.when(i < na_ref[0])
    def _():
        each(lambda r: gather(r).start())

        @pl.when((i == 0) | (te_ref[i] != te_ref[jnp.maximum(i - 1, 0)]))
        def _():
            w1b[...] = w1_ref[...].astype(BF16)
            w3b[...] = w3_ref[...].astype(BF16)
            w2b[...] = w2_ref[...].astype(BF16)

        each(lambda r: gather(r).wait())
        x = xbuf[...].astype(BF16)
        h = _silu(jnp.dot(x, w1b[...], preferred_element_type=F32)) * jnp.dot(x, w3b[...], preferred_element_type=F32)
        ybuf[...] = jnp.dot(h.astype(BF16), w2b[...], preferred_element_type=F32)
        each(lambda r: scatter(r).start())
        each(lambda r: scatter(r).wait())


def _experts(u, slot_tgt, tile_expert, tile_valid, n_active, w1, w3, w2):
    T, D = u.shape
    E, _, de = w1.shape
    TB = EXPERT_ROWS
    n_tiles = slot_tgt.shape[0] // TB
    wmap = lambda i, te, nv, na: (te[i], 0, 0)
    grid_spec = pltpu.PrefetchScalarGridSpec(
        num_scalar_prefetch=3,
        grid=(n_tiles,),
        in_specs=[pl.BlockSpec((TB,), lambda i, te, nv, na: (i,), memory_space=pltpu.SMEM),
                  pl.BlockSpec(memory_space=pl.ANY),
                  pl.BlockSpec((None, D, de), wmap), pl.BlockSpec((None, D, de), wmap),
                  pl.BlockSpec((None, de, D), wmap)],
        out_specs=pl.BlockSpec(memory_space=pl.ANY),
        scratch_shapes=[pltpu.VMEM((TB, D), F32), pltpu.VMEM((TB, D), F32),
                        pltpu.VMEM((D, de), BF16), pltpu.VMEM((D, de), BF16), pltpu.VMEM((de, D), BF16),
                        pltpu.SemaphoreType.DMA, pltpu.SemaphoreType.DMA],
    )
    return pl.pallas_call(
        functools.partial(_experts_kernel, n_tokens=T),
        grid_spec=grid_spec,
        out_shape=jax.ShapeDtypeStruct((2 * T, D), F32),
        compiler_params=_cparams(("arbitrary",), 48),
        name="experts",
    )(tile_expert, tile_valid, n_active, slot_tgt, u, w1, w3, w2)


def _dispatch_tables(route, counts, n_experts):
    T = route.shape[0]
    TB = EXPERT_ROWS
    e = route[:, ROUTE_EXPERT:ROUTE_EXPERT + 2].astype(jnp.int32)
    rank = route[:, ROUTE_RANK:ROUTE_RANK + 2].astype(jnp.int32)
    cnt = counts[0, :n_experts].astype(jnp.int32)
    padded = (cnt + TB - 1) // TB * TB
    pad_end = jnp.cumsum(padded)
    pad_start = pad_end - padded
    dest = pad_start[e] + rank
    n_tiles = -(-(2 * T + n_experts * (TB - 1)) // TB)
    tgt = jnp.arange(T, dtype=jnp.int32)[:, None] + jnp.array([0, T], jnp.int32)[None, :]
    slot_tgt = jnp.zeros((n_tiles * TB,), jnp.int32).at[dest.reshape(-1)].set(tgt.reshape(-1))
    start = jnp.arange(n_tiles, dtype=jnp.int32) * TB
    tile_expert = jnp.minimum(jnp.searchsorted(pad_end, start, side="right"), n_experts - 1).astype(jnp.int32)
    tile_valid = jnp.clip(cnt[tile_expert] - (start - pad_start[tile_expert]), 0, TB)
    tile_valid = jnp.where(start < pad_end[-1], tile_valid, 0).astype(jnp.int32)
    n_active = (pad_end[-1] // TB).astype(jnp.int32).reshape(1)
    return slot_tgt, tile_expert, tile_valid, n_active


def _combine_kernel(x2_ref, e0_ref, e1_ref, route_ref, w_ref, o_ref, *, final_norm):
    r = route_ref[...]
    y = x2_ref[...] + r[:, ROUTE_GATE:ROUTE_GATE + 1] * e0_ref[...] + r[:, ROUTE_GATE + 1:ROUTE_GATE + 2] * e1_ref[...]
    if final_norm:
        y = y * lax.rsqrt(jnp.mean(y * y, axis=-1, keepdims=True) + EPS) * w_ref[...]
    o_ref[...] = y


def _combine(x2, planes, route, norm_w, final_norm, tm=512):
    M, D = x2.shape
    return pl.pallas_call(
        functools.partial(_combine_kernel, final_norm=final_norm),
        grid=(M // tm,),
        in_specs=[pl.BlockSpec((tm, D), lambda i: (i, 0)),
                  pl.BlockSpec((None, tm, D), lambda i: (0, i, 0)),
                  pl.BlockSpec((None, tm, D), lambda i: (1, i, 0)),
                  pl.BlockSpec((tm, LANES), lambda i: (i, 0)),
                  pl.BlockSpec((1, D), lambda i: (0, 0))],
        out_specs=pl.BlockSpec((tm, D), lambda i: (i, 0)),
        out_shape=jax.ShapeDtypeStruct((M, D), F32),
        compiler_params=_cparams(("parallel",), 48),
        name="combine",
    )(x2, planes, planes, route, norm_w.reshape(1, D))


def kernel(x, norm1_w, w_in, hy_conv_w, hy_conv_b, hy_filt_w1, hy_filt_b1, hy_filt_w2, hy_filt_b2, hy_filt_w3, hy_filt_b3, hy_filt_w4, hy_sin_freq, hy_skip, hy_norm_w, gdn_conv_w, gdn_a_log_f, gdn_a_log_b, gdn_dt_bias_f, gdn_dt_bias_b, gdn_norm_w, w_out, norm2_w, router_group_w, router_group_b, router_expert_w, router_expert_b, exp_w1, exp_w3, exp_w2, final_norm_w):
    B, L, D = x.shape
    M = B * L
    depth = w_in.shape[0]
    d_hy = hy_skip.shape[-1]
    H = gdn_a_log_f.shape[-1]
    Dh = gdn_norm_w.shape[-1]
    d_gdn = H * Dh
    n_main = 3 * d_hy + 4 * d_gdn
    G = router_group_w.shape[-1]
    E = router_expert_w.shape[-1]
    xf = x.reshape(M, D)
    for l in range(depth):
        w_main = w_in[l, :, :n_main].astype(BF16)
        w_gate = jnp.pad(w_in[l, :, n_main:], ((0, 0), (0, LANES - 4 * H)))
        p, pg = _inproj(xf, norm1_w[l], w_main, w_gate)
        p3 = p.reshape(B, L, n_main)
        x0c, yconv = _hyena_branch(p3, hy_conv_w[l], hy_conv_b[l], hy_filt_w1[l], hy_filt_b1[l], hy_filt_w2[l],
                                   hy_filt_b2[l], hy_filt_w3[l], hy_filt_b3[l], hy_filt_w4[l], hy_sin_freq[l],
                                   hy_skip[l], d_hy)
        o_f, o_b = _gdn_branch(p3, pg, gdn_conv_w[l], gdn_a_log_f[l], gdn_a_log_b[l], gdn_dt_bias_f[l],
                               gdn_dt_bias_b[l], 3 * d_hy, H, Dh)
        wr = jnp.pad(jnp.concatenate([router_group_w[l], router_expert_w[l]], axis=1), ((0, 0), (0, LANES - G - E)))
        br = jnp.pad(jnp.concatenate([router_group_b[l], router_expert_b[l]]), (0, LANES - G - E)).reshape(1, LANES)
        x2, u, route, counts = _mix_route(
            yconv.reshape(M, d_hy), x0c.reshape(M, d_hy), o_f.reshape(M, d_gdn), o_b.reshape(M, d_gdn), p,
            3 * d_hy + 3 * d_gdn, hy_norm_w[l], gdn_norm_w[l], xf, w_out[l].astype(BF16), norm2_w[l], wr, br,
            H, Dh, G, E // G)
        slot_tgt, tile_expert, tile_valid, n_active = _dispatch_tables(route, counts, E)
        planes = _experts(u, slot_tgt, tile_expert, tile_valid, n_active, exp_w1[l], exp_w3[l], exp_w2[l])
        xf = _combine(x2, planes.reshape(2, M, D), route, final_norm_w, final_norm=(l == depth - 1))
    return xf.reshape(B, L, D)
```

```python
import functools
import math

import jax
import jax.numpy as jnp
import numpy as np
from jax import lax
from jax.experimental import pallas as pl
from jax.experimental.pallas import tpu as pltpu

F32 = jnp.float32
BF16 = jnp.bfloat16
EPS = 1e-6
LANES = 128
SUBLANES = 8
VMEM_BYTES_V7X = 64 * 1024 * 1024
GDN_CHUNK = 64
FFT_N2 = 128
EXPERT_ROWS = 256
DECAY_TARGET = 1e-2
FAST_DECAY_PCT = 0.3
SLOW_DECAY_PCT = 1.5


def _cparams(sem, vmem_mb):
    return pltpu.CompilerParams(dimension_semantics=sem, vmem_limit_bytes=int(vmem_mb * 1024 * 1024))


def _dot(a, b):
    return jnp.dot(a.astype(BF16), b.astype(BF16), preferred_element_type=F32)


def _dot_nt(a, b):
    return lax.dot_general(a.astype(BF16), b.astype(BF16), (((1,), (1,)), ((), ())), preferred_element_type=F32)


def _dot_tn(a, b):
    return lax.dot_general(a.astype(BF16), b.astype(BF16), (((0,), (0,)), ((), ())), preferred_element_type=F32)


def _dot_hi(a, b):
    return jnp.dot(a, b, preferred_element_type=F32, precision=lax.Precision.HIGHEST)


def _silu(x):
    return x * jax.nn.sigmoid(x)


def _inproj_kernel(x_ref, nw_ref, w_ref, wg_ref, p_ref, g_ref, h_scr):
    @pl.when(pl.program_id(1) == 0)
    def _():
        x = x_ref[...]
        h = x * lax.rsqrt(jnp.mean(x * x, axis=-1, keepdims=True) + EPS) * nw_ref[...]
        h_scr[...] = h.astype(BF16)
        g_ref[...] = _dot(h, wg_ref[...])

    p_ref[...] = jnp.dot(h_scr[...], w_ref[...], preferred_element_type=F32)


def _inproj(xf, norm_w, w_main, w_gate, tm=1024, tn=512):
    M, D = xf.shape
    n_main = w_main.shape[1]
    return pl.pallas_call(
        _inproj_kernel,
        grid=(M // tm, n_main // tn),
        in_specs=[
            pl.BlockSpec((tm, D), lambda i, j: (i, 0)),
            pl.BlockSpec((1, D), lambda i, j: (0, 0)),
            pl.BlockSpec((D, tn), lambda i, j: (0, j)),
            pl.BlockSpec((D, LANES), lambda i, j: (0, 0)),
        ],
        out_specs=[
            pl.BlockSpec((tm, tn), lambda i, j: (i, j)),
            pl.BlockSpec((tm, LANES), lambda i, j: (i, 0)),
        ],
        out_shape=[jax.ShapeDtypeStruct((M, n_main), F32), jax.ShapeDtypeStruct((M, LANES), F32)],
        scratch_shapes=[pltpu.VMEM((tm, D), BF16)],
        compiler_params=_cparams(("parallel", "arbitrary"), 48),
        name="inproj",
    )(xf, norm_w.reshape(1, D), w_main, w_gate)


def _conv3_rows(ref, r0, rows, w, n_rows):
    cur = ref[pl.ds(r0, rows), :]
    lo = jnp.maximum(r0 - SUBLANES, 0)
    hi = jnp.minimum(r0 + rows, n_rows - SUBLANES)
    prev8 = ref[pl.ds(pl.multiple_of(lo, SUBLANES), SUBLANES), :]
    next8 = ref[pl.ds(pl.multiple_of(hi, SUBLANES), SUBLANES), :]
    prev_row = jnp.where(r0 > 0, prev8[SUBLANES - 1:SUBLANES, :], 0.0)
    next_row = jnp.where(r0 + rows < n_rows, next8[0:1, :], 0.0)
    row = lax.broadcasted_iota(jnp.int32, cur.shape, 0)
    xm = jnp.where(row == 0, prev_row, pltpu.roll(cur, 1, 0))
    xp = jnp.where(row == rows - 1, next_row, pltpu.roll(cur, rows - 1, 0))
    return xm * w[0:1, :] + cur * w[1:2, :] + xp * w[2:3, :]


CONV_ROWS = 256


def _hy_pre_kernel(x0_ref, x1_ref, v_ref, w0_ref, w1_ref, w2_ref, b0_ref, b1_ref, b2_ref, x0c_ref, vg_ref):
    L = x0_ref.shape[0]
    w0, w1, w2 = w0_ref[...], w1_ref[...], w2_ref[...]
    b0, b1, b2 = b0_ref[...], b1_ref[...], b2_ref[...]

    def body(c, carry):
        r0 = pl.multiple_of(c * CONV_ROWS, CONV_ROWS)
        x0c_ref[pl.ds(r0, CONV_ROWS), :] = _conv3_rows(x0_ref, r0, CONV_ROWS, w0, L) + b0
        x1c = _conv3_rows(x1_ref, r0, CONV_ROWS, w1, L) + b1
        vc = _conv3_rows(v_ref, r0, CONV_ROWS, w2, L) + b2
        vg_ref[pl.ds(r0, CONV_ROWS), :] = vc * x1c
        return carry

    lax.fori_loop(0, L // CONV_ROWS, body, 0)


def _hy_pre(p3, conv_w, conv_b, d_hy):
    B, L, _ = p3.shape
    nt = d_hy // LANES
    bias = conv_b.reshape(1, -1)
    pspec = lambda off: pl.BlockSpec((None, L, LANES), lambda b, c: (b, 0, c + off))
    wspec = lambda off: pl.BlockSpec((3, LANES), lambda b, c: (0, c + off))
    bspec = lambda off: pl.BlockSpec((1, LANES), lambda b, c: (0, c + off))
    ospec = pl.BlockSpec((None, L, LANES), lambda b, c: (b, 0, c))
    return pl.pallas_call(
        _hy_pre_kernel,
        grid=(B, nt),
        in_specs=[pspec(0), pspec(nt), pspec(2 * nt), wspec(0), wspec(nt), wspec(2 * nt),
                  bspec(0), bspec(nt), bspec(2 * nt)],
        out_specs=[ospec, ospec],
        out_shape=[jax.ShapeDtypeStruct((B, L, d_hy), F32)] * 2,
        compiler_params=_cparams(("parallel", "parallel"), 40),
        name="hy_pre",
    )(p3, p3, p3, conv_w, conv_w, conv_w, bias, bias, bias)


def _gdn_pre_kernel(x_ref, w_ref, o_ref, *, n_heads, head_dim):
    L = x_ref.shape[0]
    w = w_ref[...]
    c = pl.program_id(1)
    q_scale = jnp.where(c < n_heads, head_dim ** -0.5, 1.0)
    is_qk = c < 2 * n_heads

    def body(i, carry):
        r0 = pl.multiple_of(i * CONV_ROWS, CONV_ROWS)
        y = _silu(_conv3_rows(x_ref, r0, CONV_ROWS, w, L))
        inv = lax.rsqrt(jnp.sum(y * y, axis=-1, keepdims=True) + EPS) * q_scale
        o_ref[pl.ds(r0, CONV_ROWS), :] = y * jnp.where(is_qk, inv, 1.0)
        return carry

    lax.fori_loop(0, L // CONV_ROWS, body, 0)


def _gdn_pre(p3, conv_w, col0, n_heads, head_dim):
    B, L, _ = p3.shape
    assert head_dim == LANES
    nt = 3 * n_heads
    off = col0 // LANES
    return pl.pallas_call(
        functools.partial(_gdn_pre_kernel, n_heads=n_heads, head_dim=head_dim),
        grid=(B, nt),
        in_specs=[pl.BlockSpec((None, L, LANES), lambda b, c: (b, 0, c + off)),
                  pl.BlockSpec((3, LANES), lambda b, c: (0, c))],
        out_specs=pl.BlockSpec((None, L, LANES), lambda b, c: (b, 0, c)),
        out_shape=jax.ShapeDtypeStruct((B, L, nt * LANES), F32),
        compiler_params=_cparams(("parallel", "parallel"), 24),
        name="gdn_pre",
    )(p3, conv_w)


GATE_ROWS = 512


def _gdn_gates_kernel(pg_ref, alog_ref, dtb_ref, o_ref, *, n_heads):
    H = n_heads
    x = pg_ref[...]
    beta = jax.nn.sigmoid(x)
    z = x + dtb_ref[...]
    softplus = jnp.maximum(z, 0.0) + jnp.log1p(jnp.exp(-jnp.abs(z)))
    g = -jnp.exp(alog_ref[...]) * softplus
    ii = lax.broadcasted_iota(jnp.int32, (GATE_ROWS, GATE_ROWS), 0)
    jj = lax.broadcasted_iota(jnp.int32, (GATE_ROWS, GATE_ROWS), 1)
    same = (ii // GDN_CHUNK) == (jj // GDN_CHUNK)
    m_fwd = jnp.where(same & (jj <= ii), 1.0, 0.0)
    m_bwd = jnp.where(same & (jj >= ii), 1.0, 0.0)
    m_all = jnp.where(same, 1.0, 0.0)
    gc_f = _dot_hi(m_fwd, g)
    gc_b = _dot_hi(m_bwd, g)
    g_tot = pltpu.roll(_dot_hi(m_all, g), 4 * H, 1)
    lane = lax.broadcasted_iota(jnp.int32, x.shape, 1)
    out = jnp.where(lane < 2 * H, beta,
                    jnp.where(lane < 3 * H, gc_f,
                              jnp.where(lane < 4 * H, gc_b,
                                        jnp.where((lane >= 6 * H) & (lane < 8 * H), g_tot, 0.0))))
    o_ref[...] = out


def _gdn_gates(pg, a_log_f, a_log_b, dt_bias_f, dt_bias_b, n_heads):
    M = pg.shape[0]
    H = n_heads
    assert 8 * H <= LANES
    pad = lambda a, b: jnp.concatenate([jnp.zeros((2 * H,), F32), a.astype(F32), b.astype(F32),
                                        jnp.zeros((LANES - 4 * H,), F32)]).reshape(1, LANES)
    return pl.pallas_call(
        functools.partial(_gdn_gates_kernel, n_heads=H),
        grid=(M // GATE_ROWS,),
        in_specs=[pl.BlockSpec((GATE_ROWS, LANES), lambda i: (i, 0)),
                  pl.BlockSpec((1, LANES), lambda i: (0, 0)),
                  pl.BlockSpec((1, LANES), lambda i: (0, 0))],
        out_specs=pl.BlockSpec((GATE_ROWS, LANES), lambda i: (i, 0)),
        out_shape=jax.ShapeDtypeStruct((M, LANES), F32),
        compiler_params=_cparams(("parallel",), 24),
        name="gdn_gates",
    )(pg, pad(a_log_f, a_log_b), pad(dt_bias_f, dt_bias_b))


def _delta_chunks(q, k, v, beta, gc_col, gc_row, gtot, state, lower):
    n = len(q)
    C = q[0].shape[0]
    D = k[0].shape[1]
    ii = lax.broadcasted_iota(jnp.int32, (C, C), 0)
    jj = lax.broadcasted_iota(jnp.int32, (C, C), 1)
    eye = jnp.where(ii == jj, 1.0, 0.0)
    incl = [(ii >= jj) if lo else (ii <= jj) for lo in lower]
    strict = [(ii > jj) if lo else (ii < jj) for lo in lower]
    rng = range(n)
    decay = [jnp.where(incl[i], jnp.exp(jnp.where(incl[i], gc_col[i] - gc_row[i], 0.0)), 0.0) for i in rng]
    kb = [k[i] * beta[i] for i in rng]
    kk = [_dot_nt(kb[i], k[i]) for i in rng]
    qk = [_dot_nt(q[i], k[i]) for i in rng]
    m = [jnp.where(strict[i], -(kk[i] * decay[i]), 0.0) for i in rng]
    r = [eye + m[i] for i in rng]
    m = [_dot(m[i], m[i]) for i in rng]
    for _ in range(int(math.log2(C)) - 2):
        rm = [_dot(jnp.concatenate([r[i], m[i]], axis=0), m[i]) for i in rng]
        r = [r[i] + rm[i][:C] for i in rng]
        m = [rm[i][C:] for i in rng]
    r = [r[i] + _dot(r[i], m[i]) for i in rng]
    eg = [jnp.exp(gc_col[i]) for i in rng]
    wu = [_dot(r[i], jnp.concatenate([kb[i] * eg[i], v[i] * beta[i]], axis=1)) for i in rng]
    ws = [_dot(jnp.concatenate([wu[i][:, :D], q[i] * eg[i]], axis=0), state[i]) for i in rng]
    v_new = [wu[i][:, D:] - ws[i][:C] for i in rng]
    qkm = [jnp.where(incl[i], qk[i] * decay[i], 0.0) for i in rng]
    out = [ws[i][C:] + _dot(qkm[i], v_new[i]) for i in rng]
    k_dec = [k[i] * jnp.exp(gtot[i] - gc_col[i]) for i in rng]
    new_state = [state[i] * jnp.exp(gtot[i][0:1, :]) + _dot_tn(k_dec[i], v_new[i]) for i in rng]
    return out, new_state


def _gdn_scan_kernel(qf_ref, kf_ref, vf_ref, qb_ref, kb_ref, vb_ref, gf_ref, gb_ref, rf_ref, rb_ref,
                     of_ref, ob_ref, s_scr, *, n_heads, head_dim):
    H, Dh = n_heads, head_dim

    @pl.when(pl.program_id(1) == 0)
    def _():
        s_scr[...] = jnp.zeros_like(s_scr)

    gf = gf_ref[...]
    gb = gb_ref[...]
    col = lambda g, j: g[:, j:j + 1]
    sls = [slice(h * Dh, (h + 1) * Dh) for h in range(H)]
    q = [qf_ref[:, sl] for sl in sls] + [qb_ref[:, sl] for sl in sls]
    k = [kf_ref[:, sl] for sl in sls] + [kb_ref[:, sl] for sl in sls]
    v = [vf_ref[:, sl] for sl in sls] + [vb_ref[:, sl] for sl in sls]
    beta = [col(gf, h) for h in range(H)] + [col(gb, H + h) for h in range(H)]
    gc_col = [col(gf, 2 * H + h) for h in range(H)] + [col(gb, 3 * H + h) for h in range(H)]
    gc_row = [rf_ref[h:h + 1, :] for h in range(H)] + [rb_ref[H + h:H + h + 1, :] for h in range(H)]
    gtot = [col(gf, 6 * H + h) for h in range(H)] + [col(gb, 7 * H + h) for h in range(H)]
    state = [s_scr[0, h] for h in range(H)] + [s_scr[1, h] for h in range(H)]
    out, new_state = _delta_chunks(q, k, v, beta, gc_col, gc_row, gtot, state, [True] * H + [False] * H)
    for h in range(H):
        of_ref[:, sls[h]] = out[h]
        ob_ref[:, sls[h]] = out[H + h]
        s_scr[0, h] = new_state[h]
        s_scr[1, h] = new_state[H + h]


def _gdn_scan(qkv, gates, gates_row, n_heads, head_dim):
    B, L, _ = qkv.shape
    H, Dh = n_heads, head_dim
    d = H * Dh
    C = GDN_CHUNK
    N = L // C
    fwd = lambda col: pl.BlockSpec((None, C, d), lambda b, n: (b, n, col))
    bwd = lambda col: pl.BlockSpec((None, C, d), lambda b, n: (b, N - 1 - n, col))
    return pl.pallas_call(
        functools.partial(_gdn_scan_kernel, n_heads=H, head_dim=Dh),
        grid=(B, N),
        in_specs=[fwd(0), fwd(1), fwd(2), bwd(0), bwd(1), bwd(2),
                  pl.BlockSpec((None, C, LANES), lambda b, n: (b, n, 0)),
                  pl.BlockSpec((None, C, LANES), lambda b, n: (b, N - 1 - n, 0)),
                  pl.BlockSpec((None, None, 2 * H, C), lambda b, n: (b, n, 0, 0)),
                  pl.BlockSpec((None, None, 2 * H, C), lambda b, n: (b, N - 1 - n, 0, 0))],
        out_specs=[pl.BlockSpec((None, C, d), lambda b, n: (b, n, 0)),
                   pl.BlockSpec((None, C, d), lambda b, n: (b, N - 1 - n, 0))],
        out_shape=[jax.ShapeDtypeStruct((B, L, d), F32)] * 2,
        scratch_shapes=[pltpu.VMEM((2, H, Dh, Dh), F32)],
        compiler_params=_cparams(("parallel", "arbitrary"), 32),
        name="gdn_scan",
    )(qkv, qkv, qkv, qkv, qkv, qkv, gates, gates, gates_row, gates_row)


def _gdn_branch(p3, pg, conv_w, a_log_f, a_log_b, dt_bias_f, dt_bias_b, col0, n_heads, head_dim):
    B, L, _ = p3.shape
    H = n_heads
    qkv = _gdn_pre(p3, conv_w, col0, H, head_dim)
    gates = _gdn_gates(pg, a_log_f, a_log_b, dt_bias_f, dt_bias_b, H).reshape(B, L, LANES)
    N = L // GDN_CHUNK
    gates_row = gates[..., 2 * H:4 * H].reshape(B, N, GDN_CHUNK, 2 * H).transpose(0, 1, 3, 2)
    return _gdn_scan(qkv, gates, gates_row, H, head_dim)


@functools.lru_cache(maxsize=None)
def _filter_positions(L, pos_emb_dim):
    n = 2 * L
    r = np.arange(n)
    k = np.where(r < L, r, np.where(r == L, 0, n - r)).astype(np.float64)
    t = k / (L - 1)
    bands = (pos_emb_dim - 1) // 2
    fb = np.linspace(1e-4, bands - 1, bands)
    ang = (2.0 * math.pi / L) * k[:, None] * fb[None, :]
    z = np.concatenate([t[:, None], np.cos(ang), -np.sin(ang)], axis=-1)
    return z.astype(np.float32)


@functools.lru_cache(maxsize=None)
def _decay_rates(d_hy):
    max_decay = math.log(DECAY_TARGET) / FAST_DECAY_PCT
    min_decay = math.log(DECAY_TARGET) / SLOW_DECAY_PCT
    return np.abs(np.linspace(min_decay, max_decay, d_hy)).astype(np.float32).reshape(1, d_hy)


def _filt_mlp_kernel(z_ref, w1_ref, b1_ref, w2_ref, b2_ref, w3_ref, b3_ref, fr_ref, o_ref):
    fr = fr_ref[...]
    h = jnp.sin(fr * (_dot_hi(z_ref[...], w1_ref[...]) + b1_ref[...]))
    h = jnp.sin(fr * (_dot_hi(h, w2_ref[...]) + b2_ref[...]))
    o_ref[...] = jnp.sin(fr * (_dot_hi(h, w3_ref[...]) + b3_ref[...]))


def _filt_mlp(z, w1, b1, w2, b2, w3, b3, freq, tr=1024):
    n, pe = z.shape
    fw = w1.shape[1]
    row = lambda a: a.reshape(1, -1).astype(F32)
    full = lambda a: pl.BlockSpec(a.shape, lambda i: (0, 0))
    args = (z, w1, row(b1), w2, row(b2), w3, row(b3), row(freq))
    return pl.pallas_call(
        _filt_mlp_kernel,
        grid=(n // tr,),
        in_specs=[pl.BlockSpec((tr, pe), lambda i: (i, 0))] + [full(a) for a in args[1:]],
        out_specs=pl.BlockSpec((tr, fw), lambda i: (i, 0)),
        out_shape=jax.ShapeDtypeStruct((n, fw), F32),
        compiler_params=_cparams(("parallel",), 24),
        name="filt_mlp",
    )(*args)


@functools.lru_cache(maxsize=None)
def _dft_tables(L):
    n = 2 * L
    N2 = FFT_N2
    N1 = n // N2
    N1h = N1 // 2
    j2 = np.arange(N2)[:, None, None]
    k1 = np.arange(N1)[None, :, None]

    def stage1(n_j1):
        j1 = np.arange(n_j1)[None, None, :]
        m = (k1 * (N2 * j1 + j2)) % n
        th = 2.0 * np.pi * m / n
        return np.cos(th), np.sin(th)

    c, s = stage1(N1h)
    t1 = np.concatenate([np.concatenate([c, s], axis=2), np.concatenate([-s, c], axis=2)], axis=1)
    c, s = stage1(N1)
    t1g = np.concatenate([c, -s], axis=1)
    c, s = stage1(N1h)
    ct, st = np.swapaxes(c, 1, 2) / n, np.swapaxes(s, 1, 2) / n
    t2 = np.concatenate([np.concatenate([ct, -st], axis=2), np.concatenate([st, ct], axis=2)], axis=1)
    a = np.arange(N2)
    th = 2.0 * np.pi * ((a[:, None] * a[None, :]) % N2) / N2
    c2, s2 = np.cos(th), np.sin(th)
    f2f = np.block([[c2, s2], [-s2, c2]])
    f2i = np.block([[c2, -s2], [s2, c2]])
    as_bf16 = lambda x: jnp.asarray(x, dtype=F32).astype(BF16)
    return dict(N1=N1, N2=N2, t1=t1.astype(np.float32), t1g=t1g.astype(np.float32), t2=t2.astype(np.float32),
                f2f=f2f.astype(np.float32), f2i=f2i.astype(np.float32))


FILT_ROWS = 512
FFT_UNROLL = 8


def _filt_fft_kernel(h3_ref, w4f_ref, w4b_ref, delta_ref, t1g_ref, f2f_ref, hspec_ref, hb0_ref, g_scr, a_scr,
                     *, L, N1, N2):
    n = 2 * L
    delta = delta_ref[...]
    hb0_ref[...] = jnp.zeros_like(hb0_ref)

    def gen(c, carry):
        r0 = pl.multiple_of(c * FILT_ROWS, FILT_ROWS)
        row = r0 + lax.broadcasted_iota(jnp.int32, (FILT_ROWS, LANES), 0)
        lag = jnp.where(row < L, row, jnp.where(row == L, 0, n - row))
        window = jnp.exp(-(lag.astype(F32) * (1.0 / (L - 1))) * delta)
        w4 = jnp.where(r0 < L, w4f_ref[...], w4b_ref[...])
        g = _dot(h3_ref[pl.ds(r0, FILT_ROWS), :], w4) * window
        at_l = row == L
        hb0_ref[...] += jnp.sum(jnp.where(at_l, g, 0.0), axis=0, keepdims=True)
        g_scr[pl.ds(r0, FILT_ROWS), :] = jnp.where(at_l, 0.0, g)
        return carry

    lax.fori_loop(0, n // FILT_ROWS, gen, 0)

    def stage1(j2, carry):
        x = g_scr[pl.ds(j2, N1, stride=N2), :]
        a_scr[pl.ds(pl.multiple_of(j2 * 2 * N1, 2 * N1), 2 * N1), :] = _dot(t1g_ref[j2], x)
        return carry

    lax.fori_loop(0, N2, stage1, 0, unroll=FFT_UNROLL)

    def stage2(k1, carry):
        ar = a_scr[pl.ds(k1, N2, stride=2 * N1), :]
        ai = a_scr[pl.ds(N1 + k1, N2, stride=2 * N1), :]
        z = _dot(f2f_ref[...], jnp.concatenate([ar, ai], axis=0))
        hspec_ref[pl.ds(pl.multiple_of(k1 * 2 * N2, 2 * N2), 2 * N2), :] = z.astype(hspec_ref.dtype)
        return carry

    lax.fori_loop(0, N1, stage2, 0, unroll=FFT_UNROLL)


def _filt_fft(h3, w4, L, d_hy):
    tb = _dft_tables(L)
    N1, N2 = tb["N1"], tb["N2"]
    n = 2 * L
    fw = h3.shape[1]
    nt = d_hy // LANES
    t1g = jnp.asarray(tb["t1g"]).astype(BF16)
    f2f = jnp.asarray(tb["f2f"]).astype(BF16)
    return pl.pallas_call(
        functools.partial(_filt_fft_kernel, L=L, N1=N1, N2=N2),
        grid=(nt,),
        in_specs=[pl.BlockSpec((n, fw), lambda c: (0, 0)),
                  pl.BlockSpec((fw, LANES), lambda c: (0, c)),
                  pl.BlockSpec((fw, LANES), lambda c: (0, c + nt)),
                  pl.BlockSpec((1, LANES), lambda c: (0, c)),
                  pl.BlockSpec(t1g.shape, lambda c: (0, 0, 0)),
                  pl.BlockSpec(f2f.shape, lambda c: (0, 0))],
        out_specs=[pl.BlockSpec((2 * n, LANES), lambda c: (0, c)),
                   pl.BlockSpec((SUBLANES, LANES), lambda c: (0, c))],
        out_shape=[jax.ShapeDtypeStruct((2 * n, d_hy), BF16), jax.ShapeDtypeStruct((SUBLANES, d_hy), F32)],
        scratch_shapes=[pltpu.VMEM((n, LANES), F32), pltpu.VMEM((2 * n, LANES), F32)],
        compiler_params=_cparams(("parallel",), 48),
        name="filt_fft",
    )(h3, w4, w4, jnp.asarray(_decay_rates(d_hy)), t1g, f2f)


def _hy_conv_kernel(vg_ref, hspec_ref, skip_ref, hb0_ref, t1_ref, f2f_ref, f2i_ref, t2_ref, y_ref, a_scr, b_scr,
                    *, N1, N2):
    N1h = N1 // 2

    def stage1(j2, carry):
        x = jnp.concatenate([vg_ref[0, pl.ds(j2, N1h, stride=N2), :],
                             vg_ref[1, pl.ds(j2, N1h, stride=N2), :]], axis=0)
        a_scr[pl.ds(pl.multiple_of(j2 * 2 * N1, 2 * N1), 2 * N1), :] = _dot(t1_ref[j2], x)
        return carry

    lax.fori_loop(0, N2, stage1, 0, unroll=FFT_UNROLL)

    def stage2(k1, carry):
        ar = a_scr[pl.ds(k1, N2, stride=2 * N1), :]
        ai = a_scr[pl.ds(N1 + k1, N2, stride=2 * N1), :]
        z = _dot(f2f_ref[...], jnp.concatenate([ar, ai], axis=0))
        zr, zi = z[:N2], z[N2:]
        base = pl.multiple_of(k1 * 2 * N2, 2 * N2)
        hr = hspec_ref[pl.ds(base, N2), :].astype(F32)
        hi = hspec_ref[pl.ds(base + N2, N2), :].astype(F32)
        prod = jnp.concatenate([zr * hr - zi * hi, zr * hi + zi * hr], axis=0)
        b_scr[pl.ds(base, 2 * N2), :] = _dot(f2i_ref[...], prod)
        return carry

    lax.fori_loop(0, N1, stage2, 0, unroll=FFT_UNROLL)

    skip = skip_ref[...] + hb0_ref[0:1, :]

    def stage3(j2, carry):
        b = jnp.concatenate([b_scr[pl.ds(j2, N1, stride=2 * N2), :],
                             b_scr[pl.ds(N2 + j2, N1, stride=2 * N2), :]], axis=0)
        y = _dot(t2_ref[j2], b)
        y_ref[0, pl.ds(j2, N1h, stride=N2), :] = y[:N1h] + vg_ref[0, pl.ds(j2, N1h, stride=N2), :] * skip
        y_ref[1, pl.ds(j2, N1h, stride=N2), :] = y[N1h:] + vg_ref[1, pl.ds(j2, N1h, stride=N2), :] * skip
        return carry

    lax.fori_loop(0, N2, stage3, 0, unroll=FFT_UNROLL)


def _hy_conv(vg, hspec, skip, hb0):
    B, L, d_hy = vg.shape
    assert B % 2 == 0
    tb = _dft_tables(L)
    N1, N2 = tb["N1"], tb["N2"]
    n = 2 * L
    nt = d_hy // LANES
    t1, t2 = (jnp.asarray(tb[k]).astype(BF16) for k in ("t1", "t2"))
    f2f, f2i = (jnp.asarray(tb[k]).astype(BF16) for k in ("f2f", "f2i"))
    const3 = lambda a: pl.BlockSpec(a.shape, lambda b, c: (0, 0, 0))
    const2 = lambda a: pl.BlockSpec(a.shape, lambda b, c: (0, 0))
    return pl.pallas_call(
        functools.partial(_hy_conv_kernel, N1=N1, N2=N2),
        grid=(B // 2, nt),
        in_specs=[pl.BlockSpec((2, L, LANES), lambda b, c: (b, 0, c)),
                  pl.BlockSpec((2 * n, LANES), lambda b, c: (0, c)),
                  pl.BlockSpec((1, LANES), lambda b, c: (0, c)),
                  pl.BlockSpec((SUBLANES, LANES), lambda b, c: (0, c)),
                  const3(t1), const2(f2f), const2(f2i), const3(t2)],
        out_specs=pl.BlockSpec((2, L, LANES), lambda b, c: (b, 0, c)),
        out_shape=jax.ShapeDtypeStruct((B, L, d_hy), F32),
        scratch_shapes=[pltpu.VMEM((2 * n, LANES), F32), pltpu.VMEM((2 * n, LANES), F32)],
        compiler_params=_cparams(("parallel", "parallel"), 56),
        name="hy_conv",
    )(vg, hspec, skip.reshape(1, d_hy).astype(F32), hb0, t1, f2f, f2i, t2)


def _hyena_branch(p3, conv_w, conv_b, fw1, fb1, fw2, fb2, fw3, fb3, fw4, freq, skip, d_hy):
    B, L, _ = p3.shape
    x0c, vg = _hy_pre(p3, conv_w, conv_b, d_hy)
    z = jnp.asarray(_filter_positions(L, fw1.shape[0]))
    h3 = _filt_mlp(z, fw1, fb1, fw2, fb2, fw3, fb3, freq)
    hspec, hb0 = _filt_fft(h3, fw4, L, d_hy)
    return x0c, _hy_conv(vg, hspec, skip, hb0)


ROUTE_GATE, ROUTE_EXPERT, ROUTE_RANK = 0, 2, 4


def _mix_route_kernel(yc_ref, x0_ref, of_ref, ob_ref, z_ref, hnw_ref, gnw_ref, x_ref, wo_ref, n2w_ref, wr_ref, br_ref,
                      x2_ref, u_ref, route_ref, cnt_ref, run_scr, *, n_heads, head_dim, n_groups, per_group):
    tm = x_ref.shape[0]
    d_hy = yc_ref.shape[1]

    @pl.when(pl.program_id(0) == 0)
    def _():
        run_scr[...] = jnp.zeros_like(run_scr)

    yh = yc_ref[...] * x0_ref[...]
    yh = yh * lax.rsqrt(jnp.mean(yh * yh, axis=-1, keepdims=True) + EPS) * hnw_ref[...]
    parts = [yh.astype(BF16)]
    for h in range(n_heads):
        sl = slice(h * head_dim, (h + 1) * head_dim)
        o = of_ref[:, sl] + ob_ref[:, sl]
        o = o * lax.rsqrt(jnp.mean(o * o, axis=-1, keepdims=True) + EPS) * gnw_ref[...] * _silu(z_ref[:, sl])
        parts.append(o.astype(BF16))
    ymix = jnp.concatenate(parts, axis=-1)
    x2 = x_ref[...] + jnp.dot(ymix, wo_ref[...], preferred_element_type=F32)
    x2_ref[...] = x2
    u = x2 * lax.rsqrt(jnp.mean(x2 * x2, axis=-1, keepdims=True) + EPS) * n2w_ref[...]
    u_ref[...] = u

    G, P = n_groups, per_group
    logits = _dot(u, wr_ref[...]) + br_ref[...]
    lane = lax.broadcasted_iota(jnp.int32, logits.shape, 1)
    neg = jnp.float32(-jnp.inf)
    big = jnp.int32(4 * LANES)
    first = lambda hit: jnp.min(jnp.where(hit, lane, big), axis=-1, keepdims=True)
    gl = jnp.where(lane < G, logits, neg)
    gmax = jnp.max(gl, axis=-1, keepdims=True)
    gidx = first(gl == gmax)
    grp_gate = 1.0 / jnp.sum(jnp.exp(gl - gmax), axis=-1, keepdims=True)
    in_grp = (lane >= G) & (lane < G + G * P) & (((lane - G) // P) == gidx)
    ll = jnp.where(in_grp, logits, neg)
    m1 = jnp.max(ll, axis=-1, keepdims=True)
    i1 = first(ll == m1)
    denom = jnp.sum(jnp.exp(ll - m1), axis=-1, keepdims=True)
    ll2 = jnp.where(lane == i1, neg, ll)
    m2 = jnp.max(ll2, axis=-1, keepdims=True)
    i2 = first(ll2 == m2)
    p1 = 1.0 / denom
    p2 = jnp.exp(m2 - m1) / denom
    gate1 = grp_gate * (p1 / (p1 + p2))
    gate2 = grp_gate * (p2 / (p1 + p2))
    e1 = i1 - G
    e2 = i2 - G

    oh1 = jnp.where(lane == e1, 1.0, 0.0)
    oh2 = jnp.where(lane == e2, 1.0, 0.0)
    oh = oh1 + oh2
    ii = lax.broadcasted_iota(jnp.int32, (tm, tm), 0)
    jj = lax.broadcasted_iota(jnp.int32, (tm, tm), 1)
    before = _dot(jnp.where(ii > jj, 1.0, 0.0), oh) + run_scr[0:1, :]
    r1 = jnp.sum(oh1 * before, axis=-1, keepdims=True)
    r2 = jnp.sum(oh2 * before, axis=-1, keepdims=True)
    run_scr[...] = run_scr[...] + jnp.sum(oh, axis=0, keepdims=True)
    cnt_ref[...] = run_scr[...]

    rec = jnp.where(lane == ROUTE_GATE, gate1, 0.0)
    rec = jnp.where(lane == ROUTE_GATE + 1, gate2, rec)
    rec = jnp.where(lane == ROUTE_EXPERT, e1.astype(F32), rec)
    rec = jnp.where(lane == ROUTE_EXPERT + 1, e2.astype(F32), rec)
    rec = jnp.where(lane == ROUTE_RANK, r1, rec)
    rec = jnp.where(lane == ROUTE_RANK + 1, r2, rec)
    route_ref[...] = rec


def _mix_route(yconv, x0c, o_f, o_b, p, z_col, hy_norm_w, gdn_norm_w, xf, w_out_bf16, norm2_w, wr, br,
               n_heads, head_dim, n_groups, per_group, tm=256):
    M, D = xf.shape
    d_hy = yconv.shape[1]
    d_gdn = o_f.shape[1]
    assert z_col % d_gdn == 0 and n_groups * (per_group + 1) <= LANES
    zb = z_col // d_gdn
    row = lambda i: (i, 0)
    const = lambda i: (0, 0)
    kern = functools.partial(_mix_route_kernel, n_heads=n_heads, head_dim=head_dim,
                             n_groups=n_groups, per_group=per_group)
    return pl.pallas_call(
        kern,
        grid=(M // tm,),
        in_specs=[pl.BlockSpec((tm, d_hy), row), pl.BlockSpec((tm, d_hy), row),
                  pl.BlockSpec((tm, d_gdn), row), pl.BlockSpec((tm, d_gdn), row),
                  pl.BlockSpec((tm, d_gdn), lambda i: (i, zb)),
                  pl.BlockSpec((1, d_hy), const), pl.BlockSpec((1, head_dim), const),
                  pl.BlockSpec((tm, D), row), pl.BlockSpec(w_out_bf16.shape, const),
                  pl.BlockSpec((1, D), const), pl.BlockSpec((D, LANES), const), pl.BlockSpec((1, LANES), const)],
        out_specs=[pl.BlockSpec((tm, D), row), pl.BlockSpec((tm, D), row),
                   pl.BlockSpec((tm, LANES), row), pl.BlockSpec((SUBLANES, LANES), const)],
        out_shape=[jax.ShapeDtypeStruct((M, D), F32), jax.ShapeDtypeStruct((M, D), F32),
                   jax.ShapeDtypeStruct((M, LANES), F32), jax.ShapeDtypeStruct((SUBLANES, LANES), F32)],
        scratch_shapes=[pltpu.VMEM((SUBLANES, LANES), F32)],
        compiler_params=_cparams(("arbitrary",), 48),
        name="mix_route",
    )(yconv, x0c, o_f, o_b, p, hy_norm_w.reshape(1, d_hy), gdn_norm_w.reshape(1, head_dim), xf, w_out_bf16,
      norm2_w.reshape(1, D), wr, br)


DMA_GROUP = 8


def _experts_kernel(te_ref, nv_ref, na_ref, tgt_nxt_ref, tgt_ref, u_hbm, w1_ref, w3_ref, w2_ref, out_hbm,
                    xbuf, ybuf, w1b, w3b, w2b, gsem, ssem, *, n_tokens):
    i = pl.program_id(0)
    na = na_ref[0]
    slot = lax.rem(i, 2)
    other = 1 - slot

    def rows(count, one, group):
        n_groups = count // DMA_GROUP

        def grp(g, carry):
            r0 = g * DMA_GROUP
            if group is None:
                for j in range(DMA_GROUP):
                    one(r0 + j)
            else:
                group(r0)
            return carry

        lax.fori_loop(0, n_groups, grp, 0)

        def tail(r, carry):
            one(r)
            return carry

        lax.fori_loop(n_groups * DMA_GROUP, count, tail, 0)

    def gather(tgt, s, r):
        t = tgt[r]
        tok = jnp.where(t >= n_tokens, t - n_tokens, t)
        return pltpu.make_async_copy(u_hbm.at[pl.ds(tok, 1)], xbuf.at[s, pl.ds(r, 1)], gsem.at[s])

    def gather_wait(s, n):
        return pltpu.make_async_copy(u_hbm.at[pl.ds(0, n)], xbuf.at[s, pl.ds(0, n)], gsem.at[s])

    def scatter(s, r):
        return pltpu.make_async_copy(ybuf.at[s, pl.ds(r, 1)], out_hbm.at[pl.ds(tgt_ref[r], 1)], ssem.at[s])

    def scatter_wait(s, n):
        return pltpu.make_async_copy(ybuf.at[s, pl.ds(0, n)], out_hbm.at[pl.ds(0, n)], ssem.at[s])

    def wait_rows(count, mk):
        rows(count, lambda r: mk(1).wait(), lambda r0: mk(DMA_GROUP).wait())

    @pl.when(i < na)
    def _():
        @pl.when(i == 0)
        def _():
            xbuf[...] = jnp.zeros_like(xbuf)
            rows(nv_ref[0], lambda r: gather(tgt_ref, 0, r).start(), None)

        @pl.when(i + 1 < na)
        def _():
            rows(nv_ref[i + 1], lambda r: gather(tgt_nxt_ref, other, r).start(), None)

        @pl.when((i == 0) | (te_ref[i] != te_ref[jnp.maximum(i - 1, 0)]))
        def _():
            w1b[...] = w1_ref[...].astype(BF16)
            w3b[...] = w3_ref[...].astype(BF16)
            w2b[...] = w2_ref[...].astype(BF16)

        wait_rows(nv_ref[i], lambda n: gather_wait(slot, n))
        x = xbuf[slot].astype(BF16)
        h = _silu(jnp.dot(x, w1b[...], preferred_element_type=F32)) * jnp.dot(x, w3b[...], preferred_element_type=F32)
        ybuf[slot] = jnp.dot(h.astype(BF16), w2b[...], preferred_element_type=F32)
        rows(nv_ref[i], lambda r: scatter(slot, r).start(), None)

        @pl.when(i > 0)
        def _():
            wait_rows(nv_ref[jnp.maximum(i - 1, 0)], lambda n: scatter_wait(other, n))

        @pl.when(i == na - 1)
        def _():
            wait_rows(nv_ref[i], lambda n: scatter_wait(slot, n))


def _experts(u, slot_tgt, tile_expert, tile_valid, n_active, w1, w3, w2):
    T, D = u.shape
    E, _, de = w1.shape
    TB = EXPERT_ROWS
    n_tiles = slot_tgt.shape[0] // TB
    wmap = lambda i, te, nv, na: (te[i], 0, 0)
    grid_spec = pltpu.PrefetchScalarGridSpec(
        num_scalar_prefetch=3,
        grid=(n_tiles,),
        in_specs=[pl.BlockSpec((TB,), lambda i, te, nv, na: (jnp.minimum(i + 1, n_tiles - 1),),
                               memory_space=pltpu.SMEM),
                  pl.BlockSpec((TB,), lambda i, te, nv, na: (i,), memory_space=pltpu.SMEM),
                  pl.BlockSpec(memory_space=pl.ANY),
                  pl.BlockSpec((None, D, de), wmap), pl.BlockSpec((None, D, de), wmap),
                  pl.BlockSpec((None, de, D), wmap)],
        out_specs=pl.BlockSpec(memory_space=pl.ANY),
        scratch_shapes=[pltpu.VMEM((2, TB, D), F32), pltpu.VMEM((2, TB, D), F32),
                        pltpu.VMEM((D, de), BF16), pltpu.VMEM((D, de), BF16), pltpu.VMEM((de, D), BF16),
                        pltpu.SemaphoreType.DMA((2,)), pltpu.SemaphoreType.DMA((2,))],
    )
    return pl.pallas_call(
        functools.partial(_experts_kernel, n_tokens=T),
        grid_spec=grid_spec,
        out_shape=jax.ShapeDtypeStruct((2 * T, D), F32),
        compiler_params=_cparams(("arbitrary",), 52),
        name="experts",
    )(tile_expert, tile_valid, n_active, slot_tgt, slot_tgt, u, w1, w3, w2)


def _dispatch_tables(route, counts, n_experts):
    T = route.shape[0]
    TB = EXPERT_ROWS
    e = route[:, ROUTE_EXPERT:ROUTE_EXPERT + 2].astype(jnp.int32)
    rank = route[:, ROUTE_RANK:ROUTE_RANK + 2].astype(jnp.int32)
    cnt = counts[0, :n_experts].astype(jnp.int32)
    padded = (cnt + TB - 1) // TB * TB
    pad_end = jnp.cumsum(padded)
    pad_start = pad_end - padded
    dest = pad_start[e] + rank
    n_tiles = -(-(2 * T + n_experts * (TB - 1)) // TB)
    tgt = jnp.arange(T, dtype=jnp.int32)[:, None] + jnp.array([0, T], jnp.int32)[None, :]
    slot_tgt = jnp.zeros((n_tiles * TB,), jnp.int32).at[dest.reshape(-1)].set(tgt.reshape(-1))
    start = jnp.arange(n_tiles, dtype=jnp.int32) * TB
    tile_expert = jnp.minimum(jnp.searchsorted(pad_end, start, side="right"), n_experts - 1).astype(jnp.int32)
    tile_valid = jnp.clip(cnt[tile_expert] - (start - pad_start[tile_expert]), 0, TB)
    tile_valid = jnp.where(start < pad_end[-1], tile_valid, 0).astype(jnp.int32)
    n_active = (pad_end[-1] // TB).astype(jnp.int32).reshape(1)
    return slot_tgt, tile_expert, tile_valid, n_active


def _combine_kernel(x2_ref, e0_ref, e1_ref, route_ref, w_ref, o_ref, *, final_norm):
    r = route_ref[...]
    y = x2_ref[...] + r[:, ROUTE_GATE:ROUTE_GATE + 1] * e0_ref[...] + r[:, ROUTE_GATE + 1:ROUTE_GATE + 2] * e1_ref[...]
    if final_norm:
        y = y * lax.rsqrt(jnp.mean(y * y, axis=-1, keepdims=True) + EPS) * w_ref[...]
    o_ref[...] = y


def _combine(x2, planes, route, norm_w, final_norm, tm=512):
    M, D = x2.shape
    return pl.pallas_call(
        functools.partial(_combine_kernel, final_norm=final_norm),
        grid=(M // tm,),
        in_specs=[pl.BlockSpec((tm, D), lambda i: (i, 0)),
                  pl.BlockSpec((None, tm, D), lambda i: (0, i, 0)),
                  pl.BlockSpec((None, tm, D), lambda i: (1, i, 0)),
                  pl.BlockSpec((tm, LANES), lambda i: (i, 0)),
                  pl.BlockSpec((1, D), lambda i: (0, 0))],
        out_specs=pl.BlockSpec((tm, D), lambda i: (i, 0)),
        out_shape=jax.ShapeDtypeStruct((M, D), F32),
        compiler_params=_cparams(("parallel",), 48),
        name="combine",
    )(x2, planes, planes, route, norm_w.reshape(1, D))


def kernel(x, norm1_w, w_in, hy_conv_w, hy_conv_b, hy_filt_w1, hy_filt_b1, hy_filt_w2, hy_filt_b2, hy_filt_w3, hy_filt_b3, hy_filt_w4, hy_sin_freq, hy_skip, hy_norm_w, gdn_conv_w, gdn_a_log_f, gdn_a_log_b, gdn_dt_bias_f, gdn_dt_bias_b, gdn_norm_w, w_out, norm2_w, router_group_w, router_group_b, router_expert_w, router_expert_b, exp_w1, exp_w3, exp_w2, final_norm_w):
    B, L, D = x.shape
    M = B * L
    depth = w_in.shape[0]
    d_hy = hy_skip.shape[-1]
    H = gdn_a_log_f.shape[-1]
    Dh = gdn_norm_w.shape[-1]
    d_gdn = H * Dh
    n_main = 3 * d_hy + 4 * d_gdn
    G = router_group_w.shape[-1]
    E = router_expert_w.shape[-1]
    xf = x.reshape(M, D)
    for l in range(depth):
        w_main = w_in[l, :, :n_main].astype(BF16)
        w_gate = jnp.pad(w_in[l, :, n_main:], ((0, 0), (0, LANES - 4 * H)))
        p, pg = _inproj(xf, norm1_w[l], w_main, w_gate)
        p3 = p.reshape(B, L, n_main)
        x0c, yconv = _hyena_branch(p3, hy_conv_w[l], hy_conv_b[l], hy_filt_w1[l], hy_filt_b1[l], hy_filt_w2[l],
                                   hy_filt_b2[l], hy_filt_w3[l], hy_filt_b3[l], hy_filt_w4[l], hy_sin_freq[l],
                                   hy_skip[l], d_hy)
        o_f, o_b = _gdn_branch(p3, pg, gdn_conv_w[l], gdn_a_log_f[l], gdn_a_log_b[l], gdn_dt_bias_f[l],
                               gdn_dt_bias_b[l], 3 * d_hy, H, Dh)
        wr = jnp.pad(jnp.concatenate([router_group_w[l], router_expert_w[l]], axis=1), ((0, 0), (0, LANES - G - E)))
        br = jnp.pad(jnp.concatenate([router_group_b[l], router_expert_b[l]]), (0, LANES - G - E)).reshape(1, LANES)
        x2, u, route, counts = _mix_route(
            yconv.reshape(M, d_hy), x0c.reshape(M, d_hy), o_f.reshape(M, d_gdn), o_b.reshape(M, d_gdn), p,
            3 * d_hy + 3 * d_gdn, hy_norm_w[l], gdn_norm_w[l], xf, w_out[l].astype(BF16), norm2_w[l], wr, br,
            H, Dh, G, E // G)
        slot_tgt, tile_expert, tile_valid, n_active = _dispatch_tables(route, counts, E)
        planes = _experts(u, slot_tgt, tile_expert, tile_valid, n_active, exp_w1[l], exp_w3[l], exp_w2[l])
        xf = _combine(x2, planes.reshape(2, M, D), route, final_norm_w, final_norm=(l == depth - 1))
    return xf.reshape(B, L, D)
```

```python
import functools
import math

import jax
import jax.numpy as jnp
import numpy as np
from jax import lax
from jax.experimental import pallas as pl
from jax.experimental.pallas import tpu as pltpu

F32 = jnp.float32
BF16 = jnp.bfloat16
EPS = 1e-6
LANES = 128
SUBLANES = 8
VMEM_BYTES_V7X = 64 * 1024 * 1024
GDN_CHUNK = 64
FFT_N2 = 128
EXPERT_ROWS = 256
DECAY_TARGET = 1e-2
FAST_DECAY_PCT = 0.3
SLOW_DECAY_PCT = 1.5


def _cparams(sem, vmem_mb):
    return pltpu.CompilerParams(dimension_semantics=sem, vmem_limit_bytes=int(vmem_mb * 1024 * 1024))


def _dot(a, b):
    return jnp.dot(a.astype(BF16), b.astype(BF16), preferred_element_type=F32)


def _dot_nt(a, b):
    return lax.dot_general(a.astype(BF16), b.astype(BF16), (((1,), (1,)), ((), ())), preferred_element_type=F32)


def _dot_tn(a, b):
    return lax.dot_general(a.astype(BF16), b.astype(BF16), (((0,), (0,)), ((), ())), preferred_element_type=F32)


def _dot_hi(a, b):
    return jnp.dot(a, b, preferred_element_type=F32, precision=lax.Precision.HIGHEST)


def _silu(x):
    return x * jax.nn.sigmoid(x)


def _inproj_kernel(x_ref, nw_ref, w_ref, wg_ref, p_ref, g_ref, h_scr):
    @pl.when(pl.program_id(1) == 0)
    def _():
        x = x_ref[...]
        h = x * lax.rsqrt(jnp.mean(x * x, axis=-1, keepdims=True) + EPS) * nw_ref[...]
        h_scr[...] = h.astype(BF16)
        g_ref[...] = _dot(h, wg_ref[...])

    p_ref[...] = jnp.dot(h_scr[...], w_ref[...], preferred_element_type=F32)


def _inproj(xf, norm_w, w_main, n_main, w_gate, tm=1024, tn=1024):
    M, D = xf.shape
    assert n_main % tn == 0 and M % tm == 0
    return pl.pallas_call(
        _inproj_kernel,
        grid=(M // tm, n_main // tn),
        in_specs=[
            pl.BlockSpec((tm, D), lambda i, j: (i, 0)),
            pl.BlockSpec((1, D), lambda i, j: (0, 0)),
            pl.BlockSpec((D, tn), lambda i, j: (0, j)),
            pl.BlockSpec((D, LANES), lambda i, j: (0, 0)),
        ],
        out_specs=[
            pl.BlockSpec((tm, tn), lambda i, j: (i, j)),
            pl.BlockSpec((tm, LANES), lambda i, j: (i, 0)),
        ],
        out_shape=[jax.ShapeDtypeStruct((M, n_main), F32), jax.ShapeDtypeStruct((M, LANES), F32)],
        scratch_shapes=[pltpu.VMEM((tm, D), BF16)],
        compiler_params=_cparams(("parallel", "arbitrary"), 48),
        name="inproj",
    )(xf, norm_w.reshape(1, D), w_main, w_gate)


def _conv3_rows(ref, r0, rows, w, n_rows):
    cur = ref[pl.ds(r0, rows), :]
    lo = jnp.maximum(r0 - SUBLANES, 0)
    hi = jnp.minimum(r0 + rows, n_rows - SUBLANES)
    prev8 = ref[pl.ds(pl.multiple_of(lo, SUBLANES), SUBLANES), :]
    next8 = ref[pl.ds(pl.multiple_of(hi, SUBLANES), SUBLANES), :]
    prev_row = jnp.where(r0 > 0, prev8[SUBLANES - 1:SUBLANES, :], 0.0)
    next_row = jnp.where(r0 + rows < n_rows, next8[0:1, :], 0.0)
    row = lax.broadcasted_iota(jnp.int32, cur.shape, 0)
    xm = jnp.where(row == 0, prev_row, pltpu.roll(cur, 1, 0))
    xp = jnp.where(row == rows - 1, next_row, pltpu.roll(cur, rows - 1, 0))
    return xm * w[0:1, :] + cur * w[1:2, :] + xp * w[2:3, :]


CONV_ROWS = 256


def _hy_pre_kernel(x0_ref, x1_ref, v_ref, w0_ref, w1_ref, w2_ref, b0_ref, b1_ref, b2_ref, x0c_ref, vg_ref):
    L = x0_ref.shape[0]
    w0, w1, w2 = w0_ref[...], w1_ref[...], w2_ref[...]
    b0, b1, b2 = b0_ref[...], b1_ref[...], b2_ref[...]

    def body(c, carry):
        r0 = pl.multiple_of(c * CONV_ROWS, CONV_ROWS)
        x0c_ref[pl.ds(r0, CONV_ROWS), :] = _conv3_rows(x0_ref, r0, CONV_ROWS, w0, L) + b0
        x1c = _conv3_rows(x1_ref, r0, CONV_ROWS, w1, L) + b1
        vc = _conv3_rows(v_ref, r0, CONV_ROWS, w2, L) + b2
        vg_ref[pl.ds(r0, CONV_ROWS), :] = vc * x1c
        return carry

    lax.fori_loop(0, L // CONV_ROWS, body, 0)


def _hy_pre(p3, conv_w, conv_b, d_hy):
    B, L, _ = p3.shape
    nt = d_hy // LANES
    bias = conv_b.reshape(1, -1)
    pspec = lambda off: pl.BlockSpec((None, L, LANES), lambda b, c: (b, 0, c + off))
    wspec = lambda off: pl.BlockSpec((3, LANES), lambda b, c: (0, c + off))
    bspec = lambda off: pl.BlockSpec((1, LANES), lambda b, c: (0, c + off))
    ospec = pl.BlockSpec((None, L, LANES), lambda b, c: (b, 0, c))
    return pl.pallas_call(
        _hy_pre_kernel,
        grid=(B, nt),
        in_specs=[pspec(0), pspec(nt), pspec(2 * nt), wspec(0), wspec(nt), wspec(2 * nt),
                  bspec(0), bspec(nt), bspec(2 * nt)],
        out_specs=[ospec, ospec],
        out_shape=[jax.ShapeDtypeStruct((B, L, d_hy), F32)] * 2,
        compiler_params=_cparams(("parallel", "parallel"), 40),
        name="hy_pre",
    )(p3, p3, p3, conv_w, conv_w, conv_w, bias, bias, bias)


def _gdn_pre_kernel(x_ref, w_ref, o_ref, *, n_heads, head_dim):
    L = x_ref.shape[0]
    w = w_ref[...]
    c = pl.program_id(1)
    q_scale = jnp.where(c < n_heads, head_dim ** -0.5, 1.0)
    is_qk = c < 2 * n_heads

    def body(i, carry):
        r0 = pl.multiple_of(i * CONV_ROWS, CONV_ROWS)
        y = _silu(_conv3_rows(x_ref, r0, CONV_ROWS, w, L))
        inv = lax.rsqrt(jnp.sum(y * y, axis=-1, keepdims=True) + EPS) * q_scale
        o_ref[pl.ds(r0, CONV_ROWS), :] = y * jnp.where(is_qk, inv, 1.0)
        return carry

    lax.fori_loop(0, L // CONV_ROWS, body, 0)


def _gdn_pre(p3, conv_w, col0, n_heads, head_dim):
    B, L, _ = p3.shape
    assert head_dim == LANES
    nt = 3 * n_heads
    off = col0 // LANES
    return pl.pallas_call(
        functools.partial(_gdn_pre_kernel, n_heads=n_heads, head_dim=head_dim),
        grid=(B, nt),
        in_specs=[pl.BlockSpec((None, L, LANES), lambda b, c: (b, 0, c + off)),
                  pl.BlockSpec((3, LANES), lambda b, c: (0, c))],
        out_specs=pl.BlockSpec((None, L, LANES), lambda b, c: (b, 0, c)),
        out_shape=jax.ShapeDtypeStruct((B, L, nt * LANES), F32),
        compiler_params=_cparams(("parallel", "parallel"), 24),
        name="gdn_pre",
    )(p3, conv_w)


GATE_ROWS = 512


def _gdn_gates_kernel(pg_ref, alog_ref, dtb_ref, o_ref, *, n_heads):
    H = n_heads
    x = pg_ref[...]
    beta = jax.nn.sigmoid(x)
    z = x + dtb_ref[...]
    softplus = jnp.maximum(z, 0.0) + jnp.log1p(jnp.exp(-jnp.abs(z)))
    g = -jnp.exp(alog_ref[...]) * softplus
    pos = lax.broadcasted_iota(jnp.int32, x.shape, 0) & (GDN_CHUNK - 1)
    gc_f = g
    gc_b = g
    step = 1
    while step < GDN_CHUNK:
        gc_f = gc_f + jnp.where(pos >= step, pltpu.roll(gc_f, step, 0), 0.0)
        gc_b = gc_b + jnp.where(pos < GDN_CHUNK - step, pltpu.roll(gc_b, GATE_ROWS - step, 0), 0.0)
        step *= 2
    g_tot = pltpu.roll(gc_f + gc_b - g, 4 * H, 1)
    lane = lax.broadcasted_iota(jnp.int32, x.shape, 1)
    out = jnp.where(lane < 2 * H, beta,
                    jnp.where(lane < 3 * H, gc_f,
                              jnp.where(lane < 4 * H, gc_b,
                                        jnp.where((lane >= 6 * H) & (lane < 8 * H), g_tot, 0.0))))
    o_ref[...] = out


def _gdn_gates(pg, a_log_f, a_log_b, dt_bias_f, dt_bias_b, n_heads):
    M = pg.shape[0]
    H = n_heads
    assert 8 * H <= LANES
    pad = lambda a, b: jnp.concatenate([jnp.zeros((2 * H,), F32), a.astype(F32), b.astype(F32),
                                        jnp.zeros((LANES - 4 * H,), F32)]).reshape(1, LANES)
    return pl.pallas_call(
        functools.partial(_gdn_gates_kernel, n_heads=H),
        grid=(M // GATE_ROWS,),
        in_specs=[pl.BlockSpec((GATE_ROWS, LANES), lambda i: (i, 0)),
                  pl.BlockSpec((1, LANES), lambda i: (0, 0)),
                  pl.BlockSpec((1, LANES), lambda i: (0, 0))],
        out_specs=pl.BlockSpec((GATE_ROWS, LANES), lambda i: (i, 0)),
        out_shape=jax.ShapeDtypeStruct((M, LANES), F32),
        compiler_params=_cparams(("parallel",), 24),
        name="gdn_gates",
    )(pg, pad(a_log_f, a_log_b), pad(dt_bias_f, dt_bias_b))


def _delta_chunks(q, k, v, beta, gc_col, gc_row, gtot, state, lower):
    n = len(q)
    C = q[0].shape[0]
    D = k[0].shape[1]
    ii = lax.broadcasted_iota(jnp.int32, (C, C), 0)
    jj = lax.broadcasted_iota(jnp.int32, (C, C), 1)
    eye = jnp.where(ii == jj, 1.0, 0.0)
    incl = [(ii >= jj) if lo else (ii <= jj) for lo in lower]
    strict = [(ii > jj) if lo else (ii < jj) for lo in lower]
    rng = range(n)
    decay = [jnp.where(incl[i], jnp.exp(jnp.where(incl[i], gc_col[i] - gc_row[i], 0.0)), 0.0) for i in rng]
    kb = [k[i] * beta[i] for i in rng]
    kk = [_dot_nt(kb[i], k[i]) for i in rng]
    qk = [_dot_nt(q[i], k[i]) for i in rng]
    m = [jnp.where(strict[i], -(kk[i] * decay[i]), 0.0) for i in rng]
    r = [eye + m[i] for i in rng]
    m = [_dot(m[i], m[i]) for i in rng]
    for _ in range(int(math.log2(C)) - 2):
        rm = [_dot(jnp.concatenate([r[i], m[i]], axis=0), m[i]) for i in rng]
        r = [r[i] + rm[i][:C] for i in rng]
        m = [rm[i][C:] for i in rng]
    r = [r[i] + _dot(r[i], m[i]) for i in rng]
    eg = [jnp.exp(gc_col[i]) for i in rng]
    wu = [_dot(r[i], jnp.concatenate([kb[i] * eg[i], v[i] * beta[i]], axis=1)) for i in rng]
    ws = [_dot(jnp.concatenate([wu[i][:, :D], q[i] * eg[i]], axis=0), state[i]) for i in rng]
    v_new = [wu[i][:, D:] - ws[i][:C] for i in rng]
    qkm = [jnp.where(incl[i], qk[i] * decay[i], 0.0) for i in rng]
    out = [ws[i][C:] + _dot(qkm[i], v_new[i]) for i in rng]
    k_dec = [k[i] * jnp.exp(gtot[i] - gc_col[i]) for i in rng]
    new_state = [state[i] * jnp.exp(gtot[i][0:1, :]) + _dot_tn(k_dec[i], v_new[i]) for i in rng]
    return out, new_state


def _gdn_scan_kernel(qf_ref, kf_ref, vf_ref, qb_ref, kb_ref, vb_ref, gf_ref, gb_ref, rf_ref, rb_ref,
                     of_ref, ob_ref, s_scr, *, n_heads, head_dim):
    H, Dh = n_heads, head_dim

    @pl.when(pl.program_id(1) == 0)
    def _():
        s_scr[...] = jnp.zeros_like(s_scr)

    gf = gf_ref[...]
    gb = gb_ref[...]
    col = lambda g, j: g[:, j:j + 1]
    sls = [slice(h * Dh, (h + 1) * Dh) for h in range(H)]
    q = [qf_ref[:, sl] for sl in sls] + [qb_ref[:, sl] for sl in sls]
    k = [kf_ref[:, sl] for sl in sls] + [kb_ref[:, sl] for sl in sls]
    v = [vf_ref[:, sl] for sl in sls] + [vb_ref[:, sl] for sl in sls]
    beta = [col(gf, h) for h in range(H)] + [col(gb, H + h) for h in range(H)]
    gc_col = [col(gf, 2 * H + h) for h in range(H)] + [col(gb, 3 * H + h) for h in range(H)]
    gc_row = [rf_ref[h:h + 1, :] for h in range(H)] + [rb_ref[H + h:H + h + 1, :] for h in range(H)]
    gtot = [col(gf, 6 * H + h) for h in range(H)] + [col(gb, 7 * H + h) for h in range(H)]
    state = [s_scr[0, h] for h in range(H)] + [s_scr[1, h] for h in range(H)]
    out, new_state = _delta_chunks(q, k, v, beta, gc_col, gc_row, gtot, state, [True] * H + [False] * H)
    for h in range(H):
        of_ref[:, sls[h]] = out[h]
        ob_ref[:, sls[h]] = out[H + h]
        s_scr[0, h] = new_state[h]
        s_scr[1, h] = new_state[H + h]


def _gdn_scan(qkv, gates, gates_row, n_heads, head_dim):
    B, L, _ = qkv.shape
    H, Dh = n_heads, head_dim
    d = H * Dh
    C = GDN_CHUNK
    N = L // C
    fwd = lambda col: pl.BlockSpec((None, C, d), lambda b, n: (b, n, col))
    bwd = lambda col: pl.BlockSpec((None, C, d), lambda b, n: (b, N - 1 - n, col))
    return pl.pallas_call(
        functools.partial(_gdn_scan_kernel, n_heads=H, head_dim=Dh),
        grid=(B, N),
        in_specs=[fwd(0), fwd(1), fwd(2), bwd(0), bwd(1), bwd(2),
                  pl.BlockSpec((None, C, LANES), lambda b, n: (b, n, 0)),
                  pl.BlockSpec((None, C, LANES), lambda b, n: (b, N - 1 - n, 0)),
                  pl.BlockSpec((None, None, 2 * H, C), lambda b, n: (b, n, 0, 0)),
                  pl.BlockSpec((None, None, 2 * H, C), lambda b, n: (b, N - 1 - n, 0, 0))],
        out_specs=[pl.BlockSpec((None, C, d), lambda b, n: (b, n, 0)),
                   pl.BlockSpec((None, C, d), lambda b, n: (b, N - 1 - n, 0))],
        out_shape=[jax.ShapeDtypeStruct((B, L, d), F32)] * 2,
        scratch_shapes=[pltpu.VMEM((2, H, Dh, Dh), F32)],
        compiler_params=_cparams(("parallel", "arbitrary"), 32),
        name="gdn_scan",
    )(qkv, qkv, qkv, qkv, qkv, qkv, gates, gates, gates_row, gates_row)


def _gdn_branch(p3, pg, conv_w, a_log_f, a_log_b, dt_bias_f, dt_bias_b, col0, n_heads, head_dim):
    B, L, _ = p3.shape
    H = n_heads
    qkv = _gdn_pre(p3, conv_w, col0, H, head_dim)
    gates = _gdn_gates(pg, a_log_f, a_log_b, dt_bias_f, dt_bias_b, H).reshape(B, L, LANES)
    N = L // GDN_CHUNK
    gates_row = gates[..., 2 * H:4 * H].reshape(B, N, GDN_CHUNK, 2 * H).transpose(0, 1, 3, 2)
    return _gdn_scan(qkv, gates, gates_row, H, head_dim)


@functools.lru_cache(maxsize=None)
def _filter_positions(L, pos_emb_dim):
    n = 2 * L
    r = np.arange(n)
    k = np.where(r < L, r, np.where(r == L, 0, n - r)).astype(np.float64)
    t = k / (L - 1)
    bands = (pos_emb_dim - 1) // 2
    fb = np.linspace(1e-4, bands - 1, bands)
    ang = (2.0 * math.pi / L) * k[:, None] * fb[None, :]
    z = np.concatenate([t[:, None], np.cos(ang), -np.sin(ang)], axis=-1)
    return z.astype(np.float32)


@functools.lru_cache(maxsize=None)
def _decay_rates(d_hy):
    max_decay = math.log(DECAY_TARGET) / FAST_DECAY_PCT
    min_decay = math.log(DECAY_TARGET) / SLOW_DECAY_PCT
    return np.abs(np.linspace(min_decay, max_decay, d_hy)).astype(np.float32).reshape(1, d_hy)


def _filt_mlp_kernel(z_ref, w1_ref, b1_ref, w2_ref, b2_ref, w3_ref, b3_ref, fr_ref, o_ref):
    fr = fr_ref[...]
    h = jnp.sin(fr * (_dot_hi(z_ref[...], w1_ref[...]) + b1_ref[...]))
    h = jnp.sin(fr * (_dot_hi(h, w2_ref[...]) + b2_ref[...]))
    o_ref[...] = jnp.sin(fr * (_dot_hi(h, w3_ref[...]) + b3_ref[...]))


def _filt_mlp(z, w1, b1, w2, b2, w3, b3, freq, tr=1024):
    n, pe = z.shape
    fw = w1.shape[1]
    row = lambda a: a.reshape(1, -1).astype(F32)
    full = lambda a: pl.BlockSpec(a.shape, lambda i: (0, 0))
    args = (z, w1, row(b1), w2, row(b2), w3, row(b3), row(freq))
    return pl.pallas_call(
        _filt_mlp_kernel,
        grid=(n // tr,),
        in_specs=[pl.BlockSpec((tr, pe), lambda i: (i, 0))] + [full(a) for a in args[1:]],
        out_specs=pl.BlockSpec((tr, fw), lambda i: (i, 0)),
        out_shape=jax.ShapeDtypeStruct((n, fw), F32),
        compiler_params=_cparams(("parallel",), 24),
        name="filt_mlp",
    )(*args)


@functools.lru_cache(maxsize=None)
def _dft_tables(L):
    n = 2 * L
    N2 = FFT_N2
    N1 = n // N2
    N1h = N1 // 2
    j2 = np.arange(N2)[:, None, None]
    k1 = np.arange(N1)[None, :, None]

    def stage1(n_j1):
        j1 = np.arange(n_j1)[None, None, :]
        m = (k1 * (N2 * j1 + j2)) % n
        th = 2.0 * np.pi * m / n
        return np.cos(th), np.sin(th)

    c, s = stage1(N1h)
    t1 = np.concatenate([np.concatenate([c, s], axis=2), np.concatenate([-s, c], axis=2)], axis=1)
    c, s = stage1(N1)
    t1g = np.concatenate([c, -s], axis=1)
    c, s = stage1(N1h)
    ct, st = np.swapaxes(c, 1, 2) / n, np.swapaxes(s, 1, 2) / n
    t2 = np.concatenate([np.concatenate([ct, -st], axis=2), np.concatenate([st, ct], axis=2)], axis=1)
    a = np.arange(N2)
    th = 2.0 * np.pi * ((a[:, None] * a[None, :]) % N2) / N2
    c2, s2 = np.cos(th), np.sin(th)
    f2f = np.block([[c2, s2], [-s2, c2]])
    f2i = np.block([[c2, -s2], [s2, c2]])
    as_bf16 = lambda x: jnp.asarray(x, dtype=F32).astype(BF16)
    return dict(N1=N1, N2=N2, t1=t1.astype(np.float32), t1g=t1g.astype(np.float32), t2=t2.astype(np.float32),
                f2f=f2f.astype(np.float32), f2i=f2i.astype(np.float32))


FILT_ROWS = 512
FFT_UNROLL = 8
PITCH_PAD = 8


def _filt_fft_kernel(h3_ref, w4f_ref, w4b_ref, delta_ref, t1g_ref, f2f_ref, hspec_ref, hb0_ref, g_scr, a_scr,
                     *, L, N1, N2):
    n = 2 * L
    gp = N2 + PITCH_PAD
    ap = 2 * N1 + PITCH_PAD
    delta = delta_ref[...]
    hb0_ref[...] = jnp.zeros_like(hb0_ref)

    def gen(c, carry):
        r0 = pl.multiple_of(c * FILT_ROWS, FILT_ROWS)
        row = r0 + lax.broadcasted_iota(jnp.int32, (FILT_ROWS, LANES), 0)
        lag = jnp.where(row < L, row, jnp.where(row == L, 0, n - row))
        window = jnp.exp(-(lag.astype(F32) * (1.0 / (L - 1))) * delta)
        w4 = jnp.where(r0 < L, w4f_ref[...], w4b_ref[...])
        g = _dot(h3_ref[pl.ds(r0, FILT_ROWS), :], w4) * window
        at_l = row == L
        hb0_ref[...] += jnp.sum(jnp.where(at_l, g, 0.0), axis=0, keepdims=True)
        g = jnp.where(at_l, 0.0, g)
        for q in range(FILT_ROWS // N2):
            dst = pl.multiple_of((c * (FILT_ROWS // N2) + q) * gp, SUBLANES)
            g_scr[pl.ds(dst, N2), :] = g[q * N2:(q + 1) * N2]
        return carry

    lax.fori_loop(0, n // FILT_ROWS, gen, 0)

    def stage1(j2, carry):
        x = g_scr[pl.ds(j2, N1, stride=gp), :]
        a_scr[pl.ds(pl.multiple_of(j2 * ap, SUBLANES), 2 * N1), :] = _dot(t1g_ref[j2], x)
        return carry

    lax.fori_loop(0, N2, stage1, 0, unroll=FFT_UNROLL)

    def stage2(k1, carry):
        ar = a_scr[pl.ds(k1, N2, stride=ap), :]
        ai = a_scr[pl.ds(N1 + k1, N2, stride=ap), :]
        z = _dot(f2f_ref[...], jnp.concatenate([ar, ai], axis=0))
        hspec_ref[pl.ds(pl.multiple_of(k1 * 2 * N2, 2 * N2), 2 * N2), :] = z.astype(hspec_ref.dtype)
        return carry

    lax.fori_loop(0, N1, stage2, 0, unroll=FFT_UNROLL)


def _filt_fft(h3, w4, L, d_hy):
    tb = _dft_tables(L)
    N1, N2 = tb["N1"], tb["N2"]
    n = 2 * L
    fw = h3.shape[1]
    nt = d_hy // LANES
    t1g = jnp.asarray(tb["t1g"]).astype(BF16)
    f2f = jnp.asarray(tb["f2f"]).astype(BF16)
    return pl.pallas_call(
        functools.partial(_filt_fft_kernel, L=L, N1=N1, N2=N2),
        grid=(nt,),
        in_specs=[pl.BlockSpec((n, fw), lambda c: (0, 0)),
                  pl.BlockSpec((fw, LANES), lambda c: (0, c)),
                  pl.BlockSpec((fw, LANES), lambda c: (0, c + nt)),
                  pl.BlockSpec((1, LANES), lambda c: (0, c)),
                  pl.BlockSpec(t1g.shape, lambda c: (0, 0, 0)),
                  pl.BlockSpec(f2f.shape, lambda c: (0, 0))],
        out_specs=[pl.BlockSpec((2 * n, LANES), lambda c: (0, c)),
                   pl.BlockSpec((SUBLANES, LANES), lambda c: (0, c))],
        out_shape=[jax.ShapeDtypeStruct((2 * n, d_hy), BF16), jax.ShapeDtypeStruct((SUBLANES, d_hy), F32)],
        scratch_shapes=[pltpu.VMEM((N1 * (N2 + PITCH_PAD), LANES), F32),
                        pltpu.VMEM((N2 * (2 * N1 + PITCH_PAD), LANES), F32)],
        compiler_params=_cparams(("parallel",), 48),
        name="filt_fft",
    )(h3, w4, w4, jnp.asarray(_decay_rates(d_hy)), t1g, f2f)


def _hy_conv_kernel(vg_ref, hspec_ref, skip_ref, hb0_ref, t1_ref, f2f_ref, f2i_ref, t2_ref, y_ref,
                    x_scr, a_scr, b_scr, *, N1, N2):
    N1h = N1 // 2
    xp = N2 + PITCH_PAD
    ap = 2 * N1 + PITCH_PAD
    bp = 2 * N2 + PITCH_PAD

    for b in range(2):
        for j1 in range(N1h):
            x_scr[b, pl.ds(j1 * xp, N2), :] = vg_ref[b, pl.ds(j1 * N2, N2), :]

    def stage1(j2, carry):
        x = jnp.concatenate([x_scr[0, pl.ds(j2, N1h, stride=xp), :],
                             x_scr[1, pl.ds(j2, N1h, stride=xp), :]], axis=0)
        a_scr[pl.ds(pl.multiple_of(j2 * ap, SUBLANES), 2 * N1), :] = _dot(t1_ref[j2], x)
        return carry

    lax.fori_loop(0, N2, stage1, 0, unroll=FFT_UNROLL)

    def stage2(k1, carry):
        ar = a_scr[pl.ds(k1, N2, stride=ap), :]
        ai = a_scr[pl.ds(N1 + k1, N2, stride=ap), :]
        z = _dot(f2f_ref[...], jnp.concatenate([ar, ai], axis=0))
        zr, zi = z[:N2], z[N2:]
        base = pl.multiple_of(k1 * 2 * N2, 2 * N2)
        hr = hspec_ref[pl.ds(base, N2), :].astype(F32)
        hi = hspec_ref[pl.ds(base + N2, N2), :].astype(F32)
        prod = jnp.concatenate([zr * hr - zi * hi, zr * hi + zi * hr], axis=0)
        b_scr[pl.ds(pl.multiple_of(k1 * bp, SUBLANES), 2 * N2), :] = _dot(f2i_ref[...], prod)
        return carry

    lax.fori_loop(0, N1, stage2, 0, unroll=FFT_UNROLL)

    skip = skip_ref[...] + hb0_ref[0:1, :]

    def stage3(j2, carry):
        b = jnp.concatenate([b_scr[pl.ds(j2, N1, stride=bp), :],
                             b_scr[pl.ds(N2 + j2, N1, stride=bp), :]], axis=0)
        y = _dot(t2_ref[j2], b)
        x_scr[0, pl.ds(j2, N1h, stride=xp), :] = y[:N1h] + x_scr[0, pl.ds(j2, N1h, stride=xp), :] * skip
        x_scr[1, pl.ds(j2, N1h, stride=xp), :] = y[N1h:] + x_scr[1, pl.ds(j2, N1h, stride=xp), :] * skip
        return carry

    lax.fori_loop(0, N2, stage3, 0, unroll=FFT_UNROLL)

    for b in range(2):
        for j1 in range(N1h):
            y_ref[b, pl.ds(j1 * N2, N2), :] = x_scr[b, pl.ds(j1 * xp, N2), :]


def _hy_conv(vg, hspec, skip, hb0):
    B, L, d_hy = vg.shape
    assert B % 2 == 0
    tb = _dft_tables(L)
    N1, N2 = tb["N1"], tb["N2"]
    n = 2 * L
    nt = d_hy // LANES
    t1, t2 = (jnp.asarray(tb[k]).astype(BF16) for k in ("t1", "t2"))
    f2f, f2i = (jnp.asarray(tb[k]).astype(BF16) for k in ("f2f", "f2i"))
    const3 = lambda a: pl.BlockSpec(a.shape, lambda b, c: (0, 0, 0))
    const2 = lambda a: pl.BlockSpec(a.shape, lambda b, c: (0, 0))
    return pl.pallas_call(
        functools.partial(_hy_conv_kernel, N1=N1, N2=N2),
        grid=(B // 2, nt),
        in_specs=[pl.BlockSpec((2, L, LANES), lambda b, c: (b, 0, c)),
                  pl.BlockSpec((2 * n, LANES), lambda b, c: (0, c)),
                  pl.BlockSpec((1, LANES), lambda b, c: (0, c)),
                  pl.BlockSpec((SUBLANES, LANES), lambda b, c: (0, c)),
                  const3(t1), const2(f2f), const2(f2i), const3(t2)],
        out_specs=pl.BlockSpec((2, L, LANES), lambda b, c: (b, 0, c)),
        out_shape=jax.ShapeDtypeStruct((B, L, d_hy), F32),
        scratch_shapes=[pltpu.VMEM((2, (N1 // 2) * (N2 + PITCH_PAD), LANES), F32),
                        pltpu.VMEM((N2 * (2 * N1 + PITCH_PAD), LANES), F32),
                        pltpu.VMEM((N1 * (2 * N2 + PITCH_PAD), LANES), F32)],
        compiler_params=_cparams(("parallel", "parallel"), 58),
        name="hy_conv",
    )(vg, hspec, skip.reshape(1, d_hy).astype(F32), hb0, t1, f2f, f2i, t2)


def _hyena_branch(p3, conv_w, conv_b, fw1, fb1, fw2, fb2, fw3, fb3, fw4, freq, skip, d_hy):
    B, L, _ = p3.shape
    x0c, vg = _hy_pre(p3, conv_w, conv_b, d_hy)
    z = jnp.asarray(_filter_positions(L, fw1.shape[0]))
    h3 = _filt_mlp(z, fw1, fb1, fw2, fb2, fw3, fb3, freq)
    hspec, hb0 = _filt_fft(h3, fw4, L, d_hy)
    return x0c, _hy_conv(vg, hspec, skip, hb0)


ROUTE_GATE, ROUTE_EXPERT, ROUTE_RANK = 0, 2, 4


def _mix_route_kernel(yc_ref, x0_ref, of_ref, ob_ref, z_ref, hnw_ref, gnw_ref, x_ref, wo_ref, n2w_ref, wr_ref, br_ref,
                      x2_ref, u_ref, route_ref, cnt_ref, run_scr, *, n_heads, head_dim, n_groups, per_group):
    tm = x_ref.shape[0]
    d_hy = yc_ref.shape[1]

    @pl.when(pl.program_id(0) == 0)
    def _():
        run_scr[...] = jnp.zeros_like(run_scr)

    yh = yc_ref[...] * x0_ref[...]
    yh = yh * lax.rsqrt(jnp.mean(yh * yh, axis=-1, keepdims=True) + EPS) * hnw_ref[...]
    parts = [yh.astype(BF16)]
    for h in range(n_heads):
        sl = slice(h * head_dim, (h + 1) * head_dim)
        o = of_ref[:, sl] + ob_ref[:, sl]
        o = o * lax.rsqrt(jnp.mean(o * o, axis=-1, keepdims=True) + EPS) * gnw_ref[...] * _silu(z_ref[:, sl])
        parts.append(o.astype(BF16))
    ymix = jnp.concatenate(parts, axis=-1)
    x2 = x_ref[...] + jnp.dot(ymix, wo_ref[...], preferred_element_type=F32)
    x2_ref[...] = x2
    u = x2 * lax.rsqrt(jnp.mean(x2 * x2, axis=-1, keepdims=True) + EPS) * n2w_ref[...]
    u_ref[...] = u

    G, P = n_groups, per_group
    logits = _dot(u, wr_ref[...]) + br_ref[...]
    lane = lax.broadcasted_iota(jnp.int32, logits.shape, 1)
    neg = jnp.float32(-jnp.inf)
    big = jnp.int32(4 * LANES)
    first = lambda hit: jnp.min(jnp.where(hit, lane, big), axis=-1, keepdims=True)
    gl = jnp.where(lane < G, logits, neg)
    gmax = jnp.max(gl, axis=-1, keepdims=True)
    gidx = first(gl == gmax)
    grp_gate = 1.0 / jnp.sum(jnp.exp(gl - gmax), axis=-1, keepdims=True)
    in_grp = (lane >= G) & (lane < G + G * P) & (((lane - G) // P) == gidx)
    ll = jnp.where(in_grp, logits, neg)
    m1 = jnp.max(ll, axis=-1, keepdims=True)
    i1 = first(ll == m1)
    denom = jnp.sum(jnp.exp(ll - m1), axis=-1, keepdims=True)
    ll2 = jnp.where(lane == i1, neg, ll)
    m2 = jnp.max(ll2, axis=-1, keepdims=True)
    i2 = first(ll2 == m2)
    p1 = 1.0 / denom
    p2 = jnp.exp(m2 - m1) / denom
    gate1 = grp_gate * (p1 / (p1 + p2))
    gate2 = grp_gate * (p2 / (p1 + p2))
    e1 = i1 - G
    e2 = i2 - G

    oh1 = jnp.where(lane == e1, 1.0, 0.0)
    oh2 = jnp.where(lane == e2, 1.0, 0.0)
    oh = oh1 + oh2
    ii = lax.broadcasted_iota(jnp.int32, (tm, tm), 0)
    jj = lax.broadcasted_iota(jnp.int32, (tm, tm), 1)
    before = _dot(jnp.where(ii > jj, 1.0, 0.0), oh) + run_scr[0:1, :]
    r1 = jnp.sum(oh1 * before, axis=-1, keepdims=True)
    r2 = jnp.sum(oh2 * before, axis=-1, keepdims=True)
    run_scr[...] = run_scr[...] + jnp.sum(oh, axis=0, keepdims=True)
    cnt_ref[...] = run_scr[...]

    rec = jnp.where(lane == ROUTE_GATE, gate1, 0.0)
    rec = jnp.where(lane == ROUTE_GATE + 1, gate2, rec)
    rec = jnp.where(lane == ROUTE_EXPERT, e1.astype(F32), rec)
    rec = jnp.where(lane == ROUTE_EXPERT + 1, e2.astype(F32), rec)
    rec = jnp.where(lane == ROUTE_RANK, r1, rec)
    rec = jnp.where(lane == ROUTE_RANK + 1, r2, rec)
    route_ref[...] = rec


def _mix_route(yconv, x0c, o_f, o_b, p, z_col, hy_norm_w, gdn_norm_w, xf, w_out_bf16, norm2_w, wr, br,
               n_heads, head_dim, n_groups, per_group, tm=256):
    M, D = xf.shape
    d_hy = yconv.shape[1]
    d_gdn = o_f.shape[1]
    assert z_col % d_gdn == 0 and n_groups * (per_group + 1) <= LANES
    zb = z_col // d_gdn
    row = lambda i: (i, 0)
    const = lambda i: (0, 0)
    kern = functools.partial(_mix_route_kernel, n_heads=n_heads, head_dim=head_dim,
                             n_groups=n_groups, per_group=per_group)
    return pl.pallas_call(
        kern,
        grid=(M // tm,),
        in_specs=[pl.BlockSpec((tm, d_hy), row), pl.BlockSpec((tm, d_hy), row),
                  pl.BlockSpec((tm, d_gdn), row), pl.BlockSpec((tm, d_gdn), row),
                  pl.BlockSpec((tm, d_gdn), lambda i: (i, zb)),
                  pl.BlockSpec((1, d_hy), const), pl.BlockSpec((1, head_dim), const),
                  pl.BlockSpec((tm, D), row), pl.BlockSpec(w_out_bf16.shape, const),
                  pl.BlockSpec((1, D), const), pl.BlockSpec((D, LANES), const), pl.BlockSpec((1, LANES), const)],
        out_specs=[pl.BlockSpec((tm, D), row), pl.BlockSpec((tm, D), row),
                   pl.BlockSpec((tm, LANES), row), pl.BlockSpec((SUBLANES, LANES), const)],
        out_shape=[jax.ShapeDtypeStruct((M, D), F32), jax.ShapeDtypeStruct((M, D), F32),
                   jax.ShapeDtypeStruct((M, LANES), F32), jax.ShapeDtypeStruct((SUBLANES, LANES), F32)],
        scratch_shapes=[pltpu.VMEM((SUBLANES, LANES), F32)],
        compiler_params=_cparams(("arbitrary",), 48),
        name="mix_route",
    )(yconv, x0c, o_f, o_b, p, hy_norm_w.reshape(1, d_hy), gdn_norm_w.reshape(1, head_dim), xf, w_out_bf16,
      norm2_w.reshape(1, D), wr, br)


DMA_GROUP = 8


def _experts_kernel(te_ref, nv_ref, na_ref, tgt_nxt_ref, tgt_ref, u_hbm, w1_ref, w3_ref, w2_ref, out_hbm,
                    xbuf, ybuf, w1b, w3b, w2b, gsem, ssem, *, n_tokens):
    i = pl.program_id(0)
    na = na_ref[0]
    slot = lax.rem(i, 2)
    other = 1 - slot

    def rows(count, one, group):
        n_groups = count // DMA_GROUP

        def grp(g, carry):
            r0 = pl.multiple_of(g * DMA_GROUP, DMA_GROUP)
            if group is None:
                for j in range(DMA_GROUP):
                    one(r0 + j)
            else:
                group(r0)
            return carry

        lax.fori_loop(0, n_groups, grp, 0)

        def tail(r, carry):
            one(r)
            return carry

        lax.fori_loop(n_groups * DMA_GROUP, count, tail, 0)

    def gather(tgt, s, r):
        t = tgt[r]
        tok = jnp.where(t >= n_tokens, t - n_tokens, t)
        return pltpu.make_async_copy(u_hbm.at[pl.ds(tok, 1)], xbuf.at[s, pl.ds(r, 1)], gsem.at[s])

    def gather_wait(s, n):
        return pltpu.make_async_copy(u_hbm.at[pl.ds(0, n)], xbuf.at[s, pl.ds(0, n)], gsem.at[s])

    def scatter(s, r):
        return pltpu.make_async_copy(ybuf.at[s, pl.ds(r, 1)], out_hbm.at[pl.ds(tgt_ref[r], 1)], ssem.at[s])

    def scatter_wait(s, n):
        return pltpu.make_async_copy(ybuf.at[s, pl.ds(0, n)], out_hbm.at[pl.ds(0, n)], ssem.at[s])

    def wait_rows(count, mk):
        rows(count, lambda r: mk(1).wait(), lambda r0: mk(DMA_GROUP).wait())

    @pl.when(i < na)
    def _():
        @pl.when(i == 0)
        def _():
            xbuf[...] = jnp.zeros_like(xbuf)
            rows(nv_ref[0], lambda r: gather(tgt_ref, 0, r).start(), None)

        @pl.when(i + 1 < na)
        def _():
            rows(nv_ref[i + 1], lambda r: gather(tgt_nxt_ref, other, r).start(), None)

        @pl.when((i == 0) | (te_ref[i] != te_ref[jnp.maximum(i - 1, 0)]))
        def _():
            w1b[...] = w1_ref[...].astype(BF16)
            w3b[...] = w3_ref[...].astype(BF16)
            w2b[...] = w2_ref[...].astype(BF16)

        wait_rows(nv_ref[i], lambda n: gather_wait(slot, n))
        x = xbuf[slot].astype(BF16)
        h = _silu(jnp.dot(x, w1b[...], preferred_element_type=F32)) * jnp.dot(x, w3b[...], preferred_element_type=F32)
        ybuf[slot] = jnp.dot(h.astype(BF16), w2b[...], preferred_element_type=F32)
        rows(nv_ref[i], lambda r: scatter(slot, r).start(), None)

        @pl.when(i > 0)
        def _():
            wait_rows(nv_ref[jnp.maximum(i - 1, 0)], lambda n: scatter_wait(other, n))

        @pl.when(i == na - 1)
        def _():
            wait_rows(nv_ref[i], lambda n: scatter_wait(slot, n))


def _experts(u, slot_tgt, tile_expert, tile_valid, n_active, w1, w3, w2):
    T, D = u.shape
    E, _, de = w1.shape
    TB = EXPERT_ROWS
    n_tiles = slot_tgt.shape[0] // TB
    wmap = lambda i, te, nv, na: (te[i], 0, 0)
    grid_spec = pltpu.PrefetchScalarGridSpec(
        num_scalar_prefetch=3,
        grid=(n_tiles,),
        in_specs=[pl.BlockSpec((TB,), lambda i, te, nv, na: (jnp.minimum(i + 1, n_tiles - 1),),
                               memory_space=pltpu.SMEM),
                  pl.BlockSpec((TB,), lambda i, te, nv, na: (i,), memory_space=pltpu.SMEM),
                  pl.BlockSpec(memory_space=pl.ANY),
                  pl.BlockSpec((None, D, de), wmap), pl.BlockSpec((None, D, de), wmap),
                  pl.BlockSpec((None, de, D), wmap)],
        out_specs=pl.BlockSpec(memory_space=pl.ANY),
        scratch_shapes=[pltpu.VMEM((2, TB, D), F32), pltpu.VMEM((2, TB, D), F32),
                        pltpu.VMEM((D, de), BF16), pltpu.VMEM((D, de), BF16), pltpu.VMEM((de, D), BF16),
                        pltpu.SemaphoreType.DMA((2,)), pltpu.SemaphoreType.DMA((2,))],
    )
    return pl.pallas_call(
        functools.partial(_experts_kernel, n_tokens=T),
        grid_spec=grid_spec,
        out_shape=jax.ShapeDtypeStruct((2 * T, D), F32),
        compiler_params=_cparams(("arbitrary",), 52),
        name="experts",
    )(tile_expert, tile_valid, n_active, slot_tgt, slot_tgt, u, w1, w3, w2)


def _slot_table_kernel(dest_ref, zero_hbm, o_ref, sem, *, n_tokens):
    fill = pltpu.make_async_copy(zero_hbm, o_ref, sem)
    fill.start()
    fill.wait()

    def put(a, carry):
        o_ref[dest_ref[a]] = (a & 1) * n_tokens + (a >> 1)
        return carry

    lax.fori_loop(0, dest_ref.shape[0], put, 0, unroll=8)


def _slot_table(dest, n_slots, n_tokens):
    smem = pl.BlockSpec(memory_space=pltpu.SMEM)
    return pl.pallas_call(
        functools.partial(_slot_table_kernel, n_tokens=n_tokens),
        in_specs=[smem, pl.BlockSpec(memory_space=pl.ANY)],
        out_specs=smem,
        out_shape=jax.ShapeDtypeStruct((n_slots,), jnp.int32),
        scratch_shapes=[pltpu.SemaphoreType.DMA],
        name="slot_table",
    )(dest, jnp.zeros((n_slots,), jnp.int32))


def _dispatch_tables(route, counts, n_experts):
    T = route.shape[0]
    TB = EXPERT_ROWS
    e = route[:, ROUTE_EXPERT:ROUTE_EXPERT + 2].astype(jnp.int32)
    rank = route[:, ROUTE_RANK:ROUTE_RANK + 2].astype(jnp.int32)
    cnt = counts[0, :n_experts].astype(jnp.int32)
    padded = (cnt + TB - 1) // TB * TB
    pad_end = jnp.cumsum(padded)
    pad_start = pad_end - padded
    dest = pad_start[e] + rank
    n_tiles = -(-(2 * T + n_experts * (TB - 1)) // TB)
    slot_tgt = _slot_table(dest.reshape(-1), n_tiles * TB, T)
    start = jnp.arange(n_tiles, dtype=jnp.int32) * TB
    tile_expert = jnp.minimum(jnp.searchsorted(pad_end, start, side="right"), n_experts - 1).astype(jnp.int32)
    tile_valid = jnp.clip(cnt[tile_expert] - (start - pad_start[tile_expert]), 0, TB)
    tile_valid = jnp.where(start < pad_end[-1], tile_valid, 0).astype(jnp.int32)
    n_active = (pad_end[-1] // TB).astype(jnp.int32).reshape(1)
    return slot_tgt, tile_expert, tile_valid, n_active


def _combine_kernel(x2_ref, e0_ref, e1_ref, route_ref, w_ref, o_ref, *, final_norm):
    r = route_ref[...]
    y = x2_ref[...] + r[:, ROUTE_GATE:ROUTE_GATE + 1] * e0_ref[...] + r[:, ROUTE_GATE + 1:ROUTE_GATE + 2] * e1_ref[...]
    if final_norm:
        y = y * lax.rsqrt(jnp.mean(y * y, axis=-1, keepdims=True) + EPS) * w_ref[...]
    o_ref[...] = y


def _combine(x2, planes, route, norm_w, final_norm, tm=512):
    M, D = x2.shape
    return pl.pallas_call(
        functools.partial(_combine_kernel, final_norm=final_norm),
        grid=(M // tm,),
        in_specs=[pl.BlockSpec((tm, D), lambda i: (i, 0)),
                  pl.BlockSpec((None, tm, D), lambda i: (0, i, 0)),
                  pl.BlockSpec((None, tm, D), lambda i: (1, i, 0)),
                  pl.BlockSpec((tm, LANES), lambda i: (i, 0)),
                  pl.BlockSpec((1, D), lambda i: (0, 0))],
        out_specs=pl.BlockSpec((tm, D), lambda i: (i, 0)),
        out_shape=jax.ShapeDtypeStruct((M, D), F32),
        compiler_params=_cparams(("parallel",), 48),
        name="combine",
    )(x2, planes, planes, route, norm_w.reshape(1, D))


def kernel(x, norm1_w, w_in, hy_conv_w, hy_conv_b, hy_filt_w1, hy_filt_b1, hy_filt_w2, hy_filt_b2, hy_filt_w3, hy_filt_b3, hy_filt_w4, hy_sin_freq, hy_skip, hy_norm_w, gdn_conv_w, gdn_a_log_f, gdn_a_log_b, gdn_dt_bias_f, gdn_dt_bias_b, gdn_norm_w, w_out, norm2_w, router_group_w, router_group_b, router_expert_w, router_expert_b, exp_w1, exp_w3, exp_w2, final_norm_w):
    B, L, D = x.shape
    M = B * L
    depth = w_in.shape[0]
    d_hy = hy_skip.shape[-1]
    H = gdn_a_log_f.shape[-1]
    Dh = gdn_norm_w.shape[-1]
    d_gdn = H * Dh
    n_main = 3 * d_hy + 4 * d_gdn
    G = router_group_w.shape[-1]
    E = router_expert_w.shape[-1]
    xf = x.reshape(M, D)
    for l in range(depth):
        w_main = w_in[l].astype(BF16)
        w_gate = jnp.pad(w_in[l, :, n_main:], ((0, 0), (0, LANES - 4 * H)))
        p, pg = _inproj(xf, norm1_w[l], w_main, n_main, w_gate)
        p3 = p.reshape(B, L, n_main)
        x0c, yconv = _hyena_branch(p3, hy_conv_w[l], hy_conv_b[l], hy_filt_w1[l], hy_filt_b1[l], hy_filt_w2[l],
                                   hy_filt_b2[l], hy_filt_w3[l], hy_filt_b3[l], hy_filt_w4[l], hy_sin_freq[l],
                                   hy_skip[l], d_hy)
        o_f, o_b = _gdn_branch(p3, pg, gdn_conv_w[l], gdn_a_log_f[l], gdn_a_log_b[l], gdn_dt_bias_f[l],
                               gdn_dt_bias_b[l], 3 * d_hy, H, Dh)
        wr = jnp.pad(jnp.concatenate([router_group_w[l], router_expert_w[l]], axis=1), ((0, 0), (0, LANES - G - E)))
        br = jnp.pad(jnp.concatenate([router_group_b[l], router_expert_b[l]]), (0, LANES - G - E)).reshape(1, LANES)
        x2, u, route, counts = _mix_route(
            yconv.reshape(M, d_hy), x0c.reshape(M, d_hy), o_f.reshape(M, d_gdn), o_b.reshape(M, d_gdn), p,
            3 * d_hy + 3 * d_gdn, hy_norm_w[l], gdn_norm_w[l], xf, w_out[l].astype(BF16), norm2_w[l], wr, br,
            H, Dh, G, E // G)
        slot_tgt, tile_expert, tile_valid, n_active = _dispatch_tables(route, counts, E)
        planes = _experts(u, slot_tgt, tile_expert, tile_valid, n_active, exp_w1[l], exp_w3[l], exp_w2[l])
        xf = _combine(x2, planes.reshape(2, M, D), route, final_norm_w, final_norm=(l == depth - 1))
    return xf.reshape(B, L, D)
```

```python
import functools
import math

import jax
import jax.numpy as jnp
import numpy as np
from jax import lax
from jax.experimental import pallas as pl
from jax.experimental.pallas import tpu as pltpu

F32 = jnp.float32
BF16 = jnp.bfloat16
EPS = 1e-6
LANES = 128
SUBLANES = 8
VMEM_BYTES_V7X = 64 * 1024 * 1024
GDN_CHUNK = 64
FFT_N2 = 128
EXPERT_ROWS = 256
DECAY_TARGET = 1e-2
FAST_DECAY_PCT = 0.3
SLOW_DECAY_PCT = 1.5


def _cparams(sem, vmem_mb):
    return pltpu.CompilerParams(dimension_semantics=sem, vmem_limit_bytes=int(vmem_mb * 1024 * 1024))


def _dot(a, b):
    return jnp.dot(a.astype(BF16), b.astype(BF16), preferred_element_type=F32)


def _dot_nt(a, b):
    return lax.dot_general(a.astype(BF16), b.astype(BF16), (((1,), (1,)), ((), ())), preferred_element_type=F32)


def _dot_tn(a, b):
    return lax.dot_general(a.astype(BF16), b.astype(BF16), (((0,), (0,)), ((), ())), preferred_element_type=F32)


def _dot_hi(a, b):
    return jnp.dot(a, b, preferred_element_type=F32, precision=lax.Precision.HIGHEST)


def _silu(x):
    return x * jax.nn.sigmoid(x)


def _inproj_kernel(x_ref, nw_ref, w_ref, wg_ref, p_ref, g_ref, h_scr):
    @pl.when(pl.program_id(1) == 0)
    def _():
        x = x_ref[...]
        h = x * lax.rsqrt(jnp.mean(x * x, axis=-1, keepdims=True) + EPS) * nw_ref[...]
        h_scr[...] = h.astype(BF16)
        g_ref[...] = _dot(h, wg_ref[...])

    p_ref[...] = jnp.dot(h_scr[...], w_ref[...], preferred_element_type=F32)


def _inproj(xf, norm_w, w_main, n_main, w_gate, tm=1024, tn=1024):
    M, D = xf.shape
    assert n_main % tn == 0 and M % tm == 0
    return pl.pallas_call(
        _inproj_kernel,
        grid=(M // tm, n_main // tn),
        in_specs=[
            pl.BlockSpec((tm, D), lambda i, j: (i, 0)),
            pl.BlockSpec((1, D), lambda i, j: (0, 0)),
            pl.BlockSpec((D, tn), lambda i, j: (0, j)),
            pl.BlockSpec((D, LANES), lambda i, j: (0, 0)),
        ],
        out_specs=[
            pl.BlockSpec((tm, tn), lambda i, j: (i, j)),
            pl.BlockSpec((tm, LANES), lambda i, j: (i, 0)),
        ],
        out_shape=[jax.ShapeDtypeStruct((M, n_main), F32), jax.ShapeDtypeStruct((M, LANES), F32)],
        scratch_shapes=[pltpu.VMEM((tm, D), BF16)],
        compiler_params=_cparams(("parallel", "arbitrary"), 48),
        name="inproj",
    )(xf, norm_w.reshape(1, D), w_main, w_gate)


def _cast_kernel(w_ref, o_ref):
    o_ref[...] = w_ref[...].astype(o_ref.dtype)


def _cast_cols(w, n_cols, tn=512):
    rows = w.shape[0]
    assert n_cols % tn == 0
    return pl.pallas_call(
        _cast_kernel,
        grid=(n_cols // tn,),
        in_specs=[pl.BlockSpec((rows, tn), lambda j: (0, j))],
        out_specs=pl.BlockSpec((rows, tn), lambda j: (0, j)),
        out_shape=jax.ShapeDtypeStruct((rows, n_cols), BF16),
        compiler_params=_cparams(("parallel",), 32),
        name="w_cast",
    )(w)


def _conv3_rows(ref, r0, rows, w, n_rows):
    cur = ref[pl.ds(r0, rows), :]
    lo = jnp.maximum(r0 - SUBLANES, 0)
    hi = jnp.minimum(r0 + rows, n_rows - SUBLANES)
    prev8 = ref[pl.ds(pl.multiple_of(lo, SUBLANES), SUBLANES), :]
    next8 = ref[pl.ds(pl.multiple_of(hi, SUBLANES), SUBLANES), :]
    prev_row = jnp.where(r0 > 0, prev8[SUBLANES - 1:SUBLANES, :], 0.0)
    next_row = jnp.where(r0 + rows < n_rows, next8[0:1, :], 0.0)
    row = lax.broadcasted_iota(jnp.int32, cur.shape, 0)
    xm = jnp.where(row == 0, prev_row, pltpu.roll(cur, 1, 0))
    xp = jnp.where(row == rows - 1, next_row, pltpu.roll(cur, rows - 1, 0))
    return xm * w[0:1, :] + cur * w[1:2, :] + xp * w[2:3, :]


CONV_ROWS = 256


def _hy_pre_kernel(x0_ref, x1_ref, v_ref, w0_ref, w1_ref, w2_ref, b0_ref, b1_ref, b2_ref, x0c_ref, vg_ref):
    L = x0_ref.shape[0]
    w0, w1, w2 = w0_ref[...], w1_ref[...], w2_ref[...]
    b0, b1, b2 = b0_ref[...], b1_ref[...], b2_ref[...]

    def body(c, carry):
        r0 = pl.multiple_of(c * CONV_ROWS, CONV_ROWS)
        x0c_ref[pl.ds(r0, CONV_ROWS), :] = _conv3_rows(x0_ref, r0, CONV_ROWS, w0, L) + b0
        x1c = _conv3_rows(x1_ref, r0, CONV_ROWS, w1, L) + b1
        vc = _conv3_rows(v_ref, r0, CONV_ROWS, w2, L) + b2
        vg_ref[pl.ds(r0, CONV_ROWS), :] = vc * x1c
        return carry

    lax.fori_loop(0, L // CONV_ROWS, body, 0, unroll=2)


def _hy_pre(p3, conv_w, conv_b, d_hy):
    B, L, _ = p3.shape
    nt = d_hy // LANES
    bias = conv_b.reshape(1, -1)
    pspec = lambda off: pl.BlockSpec((None, L, LANES), lambda b, c: (b, 0, c + off))
    wspec = lambda off: pl.BlockSpec((3, LANES), lambda b, c: (0, c + off))
    bspec = lambda off: pl.BlockSpec((1, LANES), lambda b, c: (0, c + off))
    ospec = pl.BlockSpec((None, L, LANES), lambda b, c: (b, 0, c))
    return pl.pallas_call(
        _hy_pre_kernel,
        grid=(B, nt),
        in_specs=[pspec(0), pspec(nt), pspec(2 * nt), wspec(0), wspec(nt), wspec(2 * nt),
                  bspec(0), bspec(nt), bspec(2 * nt)],
        out_specs=[ospec, ospec],
        out_shape=[jax.ShapeDtypeStruct((B, L, d_hy), F32)] * 2,
        compiler_params=_cparams(("parallel", "parallel"), 40),
        name="hy_pre",
    )(p3, p3, p3, conv_w, conv_w, conv_w, bias, bias, bias)


def _gdn_pre_kernel(x_ref, w_ref, o_ref, *, n_heads, head_dim):
    L = x_ref.shape[0]
    w = w_ref[...]
    c = pl.program_id(1)
    q_scale = jnp.where(c < n_heads, head_dim ** -0.5, 1.0)
    is_qk = c < 2 * n_heads

    def body(i, carry):
        r0 = pl.multiple_of(i * CONV_ROWS, CONV_ROWS)
        y = _silu(_conv3_rows(x_ref, r0, CONV_ROWS, w, L))
        inv = lax.rsqrt(jnp.sum(y * y, axis=-1, keepdims=True) + EPS) * q_scale
        o_ref[pl.ds(r0, CONV_ROWS), :] = y * jnp.where(is_qk, inv, 1.0)
        return carry

    lax.fori_loop(0, L // CONV_ROWS, body, 0, unroll=2)


def _gdn_pre(p3, conv_w, col0, n_heads, head_dim):
    B, L, _ = p3.shape
    assert head_dim == LANES
    nt = 3 * n_heads
    off = col0 // LANES
    return pl.pallas_call(
        functools.partial(_gdn_pre_kernel, n_heads=n_heads, head_dim=head_dim),
        grid=(B, nt),
        in_specs=[pl.BlockSpec((None, L, LANES), lambda b, c: (b, 0, c + off)),
                  pl.BlockSpec((3, LANES), lambda b, c: (0, c))],
        out_specs=pl.BlockSpec((None, L, LANES), lambda b, c: (b, 0, c)),
        out_shape=jax.ShapeDtypeStruct((B, L, nt * LANES), F32),
        compiler_params=_cparams(("parallel", "parallel"), 24),
        name="gdn_pre",
    )(p3, conv_w)


GATE_ROWS = 512


def _gdn_gates_kernel(pg_ref, alog_ref, dtb_ref, o_ref, *, n_heads):
    H = n_heads
    x = pg_ref[...]
    beta = jax.nn.sigmoid(x)
    z = x + dtb_ref[...]
    softplus = jnp.maximum(z, 0.0) + jnp.log1p(jnp.exp(-jnp.abs(z)))
    g = -jnp.exp(alog_ref[...]) * softplus
    pos = lax.broadcasted_iota(jnp.int32, x.shape, 0) & (GDN_CHUNK - 1)
    gc_f = g
    gc_b = g
    step = 1
    while step < GDN_CHUNK:
        gc_f = gc_f + jnp.where(pos >= step, pltpu.roll(gc_f, step, 0), 0.0)
        gc_b = gc_b + jnp.where(pos < GDN_CHUNK - step, pltpu.roll(gc_b, GATE_ROWS - step, 0), 0.0)
        step *= 2
    g_tot = pltpu.roll(gc_f + gc_b - g, 4 * H, 1)
    lane = lax.broadcasted_iota(jnp.int32, x.shape, 1)
    out = jnp.where(lane < 2 * H, beta,
                    jnp.where(lane < 3 * H, gc_f,
                              jnp.where(lane < 4 * H, gc_b,
                                        jnp.where((lane >= 6 * H) & (lane < 8 * H), g_tot, 0.0))))
    o_ref[...] = out


def _gdn_gates(pg, a_log_f, a_log_b, dt_bias_f, dt_bias_b, n_heads):
    M = pg.shape[0]
    H = n_heads
    assert 8 * H <= LANES
    pad = lambda a, b: jnp.concatenate([jnp.zeros((2 * H,), F32), a.astype(F32), b.astype(F32),
                                        jnp.zeros((LANES - 4 * H,), F32)]).reshape(1, LANES)
    return pl.pallas_call(
        functools.partial(_gdn_gates_kernel, n_heads=H),
        grid=(M // GATE_ROWS,),
        in_specs=[pl.BlockSpec((GATE_ROWS, LANES), lambda i: (i, 0)),
                  pl.BlockSpec((1, LANES), lambda i: (0, 0)),
                  pl.BlockSpec((1, LANES), lambda i: (0, 0))],
        out_specs=pl.BlockSpec((GATE_ROWS, LANES), lambda i: (i, 0)),
        out_shape=jax.ShapeDtypeStruct((M, LANES), F32),
        compiler_params=_cparams(("parallel",), 24),
        name="gdn_gates",
    )(pg, pad(a_log_f, a_log_b), pad(dt_bias_f, dt_bias_b))


def _delta_chunks(q, k, v, beta, gc_col, gc_row, gtot, state, lower):
    n = len(q)
    C = q[0].shape[0]
    D = k[0].shape[1]
    ii = lax.broadcasted_iota(jnp.int32, (C, C), 0)
    jj = lax.broadcasted_iota(jnp.int32, (C, C), 1)
    eye = jnp.where(ii == jj, 1.0, 0.0)
    incl = [(ii >= jj) if lo else (ii <= jj) for lo in lower]
    strict = [(ii > jj) if lo else (ii < jj) for lo in lower]
    rng = range(n)
    decay = [jnp.where(incl[i], jnp.exp(jnp.where(incl[i], gc_col[i] - gc_row[i], 0.0)), 0.0) for i in rng]
    kb = [k[i] * beta[i] for i in rng]
    kk = [_dot_nt(kb[i], k[i]) for i in rng]
    qk = [_dot_nt(q[i], k[i]) for i in rng]
    m = [jnp.where(strict[i], -(kk[i] * decay[i]), 0.0) for i in rng]
    r = [eye + m[i] for i in rng]
    m = [_dot(m[i], m[i]) for i in rng]
    for _ in range(int(math.log2(C)) - 2):
        rm = [_dot(jnp.concatenate([r[i], m[i]], axis=0), m[i]) for i in rng]
        r = [r[i] + rm[i][:C] for i in rng]
        m = [rm[i][C:] for i in rng]
    r = [r[i] + _dot(r[i], m[i]) for i in rng]
    eg = [jnp.exp(gc_col[i]) for i in rng]
    wu = [_dot(r[i], jnp.concatenate([kb[i] * eg[i], v[i] * beta[i]], axis=1)) for i in rng]
    ws = [_dot(jnp.concatenate([wu[i][:, :D], q[i] * eg[i]], axis=0), state[i]) for i in rng]
    v_new = [wu[i][:, D:] - ws[i][:C] for i in rng]
    qkm = [jnp.where(incl[i], qk[i] * decay[i], 0.0) for i in rng]
    out = [ws[i][C:] + _dot(qkm[i], v_new[i]) for i in rng]
    k_dec = [k[i] * jnp.exp(gtot[i] - gc_col[i]) for i in rng]
    new_state = [state[i] * jnp.exp(gtot[i][0:1, :]) + _dot_tn(k_dec[i], v_new[i]) for i in rng]
    return out, new_state


def _gdn_scan_kernel(qf_ref, kf_ref, vf_ref, qb_ref, kb_ref, vb_ref, gf_ref, gb_ref, rf_ref, rb_ref,
                     of_ref, ob_ref, s_scr, *, n_heads, head_dim):
    H, Dh = n_heads, head_dim

    @pl.when(pl.program_id(1) == 0)
    def _():
        s_scr[...] = jnp.zeros_like(s_scr)

    gf = gf_ref[...]
    gb = gb_ref[...]
    col = lambda g, j: g[:, j:j + 1]
    sls = [slice(h * Dh, (h + 1) * Dh) for h in range(H)]
    q = [qf_ref[:, sl] for sl in sls] + [qb_ref[:, sl] for sl in sls]
    k = [kf_ref[:, sl] for sl in sls] + [kb_ref[:, sl] for sl in sls]
    v = [vf_ref[:, sl] for sl in sls] + [vb_ref[:, sl] for sl in sls]
    beta = [col(gf, h) for h in range(H)] + [col(gb, H + h) for h in range(H)]
    gc_col = [col(gf, 2 * H + h) for h in range(H)] + [col(gb, 3 * H + h) for h in range(H)]
    gc_row = [rf_ref[h:h + 1, :] for h in range(H)] + [rb_ref[H + h:H + h + 1, :] for h in range(H)]
    gtot = [col(gf, 6 * H + h) for h in range(H)] + [col(gb, 7 * H + h) for h in range(H)]
    state = [s_scr[0, h] for h in range(H)] + [s_scr[1, h] for h in range(H)]
    out, new_state = _delta_chunks(q, k, v, beta, gc_col, gc_row, gtot, state, [True] * H + [False] * H)
    for h in range(H):
        of_ref[:, sls[h]] = out[h]
        ob_ref[:, sls[h]] = out[H + h]
        s_scr[0, h] = new_state[h]
        s_scr[1, h] = new_state[H + h]


def _gdn_scan(qkv, gates, gates_row, n_heads, head_dim):
    B, L, _ = qkv.shape
    H, Dh = n_heads, head_dim
    d = H * Dh
    C = GDN_CHUNK
    N = L // C
    fwd = lambda col: pl.BlockSpec((None, C, d), lambda b, n: (b, n, col))
    bwd = lambda col: pl.BlockSpec((None, C, d), lambda b, n: (b, N - 1 - n, col))
    return pl.pallas_call(
        functools.partial(_gdn_scan_kernel, n_heads=H, head_dim=Dh),
        grid=(B, N),
        in_specs=[fwd(0), fwd(1), fwd(2), bwd(0), bwd(1), bwd(2),
                  pl.BlockSpec((None, C, LANES), lambda b, n: (b, n, 0)),
                  pl.BlockSpec((None, C, LANES), lambda b, n: (b, N - 1 - n, 0)),
                  pl.BlockSpec((None, None, 2 * H, C), lambda b, n: (b, n, 0, 0)),
                  pl.BlockSpec((None, None, 2 * H, C), lambda b, n: (b, N - 1 - n, 0, 0))],
        out_specs=[pl.BlockSpec((None, C, d), lambda b, n: (b, n, 0)),
                   pl.BlockSpec((None, C, d), lambda b, n: (b, N - 1 - n, 0))],
        out_shape=[jax.ShapeDtypeStruct((B, L, d), F32)] * 2,
        scratch_shapes=[pltpu.VMEM((2, H, Dh, Dh), F32)],
        compiler_params=_cparams(("parallel", "arbitrary"), 32),
        name="gdn_scan",
    )(qkv, qkv, qkv, qkv, qkv, qkv, gates, gates, gates_row, gates_row)


def _gdn_branch(p3, pg, conv_w, a_log_f, a_log_b, dt_bias_f, dt_bias_b, col0, n_heads, head_dim):
    B, L, _ = p3.shape
    H = n_heads
    qkv = _gdn_pre(p3, conv_w, col0, H, head_dim)
    gates = _gdn_gates(pg, a_log_f, a_log_b, dt_bias_f, dt_bias_b, H).reshape(B, L, LANES)
    N = L // GDN_CHUNK
    gates_row = gates[..., 2 * H:4 * H].reshape(B, N, GDN_CHUNK, 2 * H).transpose(0, 1, 3, 2)
    return _gdn_scan(qkv, gates, gates_row, H, head_dim)


@functools.lru_cache(maxsize=None)
def _filter_positions(L, pos_emb_dim):
    n = 2 * L
    r = np.arange(n)
    k = np.where(r < L, r, np.where(r == L, 0, n - r)).astype(np.float64)
    t = k / (L - 1)
    bands = (pos_emb_dim - 1) // 2
    fb = np.linspace(1e-4, bands - 1, bands)
    ang = (2.0 * math.pi / L) * k[:, None] * fb[None, :]
    z = np.concatenate([t[:, None], np.cos(ang), -np.sin(ang)], axis=-1)
    return z.astype(np.float32)


@functools.lru_cache(maxsize=None)
def _decay_rates(d_hy):
    max_decay = math.log(DECAY_TARGET) / FAST_DECAY_PCT
    min_decay = math.log(DECAY_TARGET) / SLOW_DECAY_PCT
    return np.abs(np.linspace(min_decay, max_decay, d_hy)).astype(np.float32).reshape(1, d_hy)


def _filt_mlp_kernel(zt_ref, w1t_ref, b1_ref, w2t_ref, b2_ref, w3t_ref, b3_ref, fr_ref, o_ref):
    fr = fr_ref[...]
    h = jnp.sin(fr * (_dot_hi(w1t_ref[...], zt_ref[...]) + b1_ref[...]))
    h = jnp.sin(fr * (_dot_hi(w2t_ref[...], h) + b2_ref[...]))
    o_ref[...] = jnp.sin(fr * (_dot_hi(w3t_ref[...], h) + b3_ref[...]))


def _filt_mlp(zt, w1, b1, w2, b2, w3, b3, freq, tc=1024):
    pe, n = zt.shape
    fw = w1.shape[1]
    col = lambda a: a.reshape(-1, 1).astype(F32)
    full = lambda a: pl.BlockSpec(a.shape, lambda i: (0, 0))
    args = (zt, w1.T, col(b1), w2.T, col(b2), w3.T, col(b3), col(freq))
    return pl.pallas_call(
        _filt_mlp_kernel,
        grid=(n // tc,),
        in_specs=[pl.BlockSpec((pe, tc), lambda i: (0, i))] + [full(a) for a in args[1:]],
        out_specs=pl.BlockSpec((fw, tc), lambda i: (0, i)),
        out_shape=jax.ShapeDtypeStruct((fw, n), F32),
        compiler_params=_cparams(("parallel",), 24),
        name="filt_mlp",
    )(*args)


@functools.lru_cache(maxsize=None)
def _dft_tables(L):
    n = 2 * L
    N2 = FFT_N2
    N1 = n // N2
    N1h = N1 // 2
    j2 = np.arange(N2)[:, None, None]
    k1 = np.arange(N1)[None, :, None]

    def stage1(n_j1):
        j1 = np.arange(n_j1)[None, None, :]
        m = (k1 * (N2 * j1 + j2)) % n
        th = 2.0 * np.pi * m / n
        return np.cos(th), np.sin(th)

    c, s = stage1(N1h)
    t1 = np.concatenate([np.concatenate([c, s], axis=2), np.concatenate([-s, c], axis=2)], axis=1)
    c, s = stage1(N1)
    t1g = np.concatenate([c, -s], axis=1)
    c, s = stage1(N1h)
    ct, st = np.swapaxes(c, 1, 2) / n, np.swapaxes(s, 1, 2) / n
    t2 = np.concatenate([np.concatenate([ct, -st], axis=2), np.concatenate([st, ct], axis=2)], axis=1)
    a = np.arange(N2)
    th = 2.0 * np.pi * ((a[:, None] * a[None, :]) % N2) / N2
    c2, s2 = np.cos(th), np.sin(th)
    f2f = np.block([[c2, s2], [-s2, c2]])
    f2i = np.block([[c2, -s2], [s2, c2]])
    as_bf16 = lambda x: jnp.asarray(x, dtype=F32).astype(BF16)
    return dict(N1=N1, N2=N2, t1=t1.astype(np.float32), t1g=t1g.astype(np.float32), t2=t2.astype(np.float32),
                f2f=f2f.astype(np.float32), f2i=f2i.astype(np.float32))


FILT_ROWS = 512
FFT_UNROLL = 8
PITCH_PAD = 8


def _filt_fft_kernel(h3_ref, w4f_ref, w4b_ref, delta_ref, t1g_ref, f2f_ref, hspec_ref, hb0_ref, g_scr, a_scr,
                     *, L, N1, N2):
    n = 2 * L
    gp = N2 + PITCH_PAD
    ap = 2 * N1 + PITCH_PAD
    delta = delta_ref[...]
    hb0_ref[...] = jnp.zeros_like(hb0_ref)

    def gen(c, carry):
        r0 = pl.multiple_of(c * FILT_ROWS, FILT_ROWS)
        row = r0 + lax.broadcasted_iota(jnp.int32, (FILT_ROWS, LANES), 0)
        lag = jnp.where(row < L, row, jnp.where(row == L, 0, n - row))
        window = jnp.exp(-(lag.astype(F32) * (1.0 / (L - 1))) * delta)
        w4 = jnp.where(r0 < L, w4f_ref[...], w4b_ref[...])
        g = _dot(h3_ref[pl.ds(r0, FILT_ROWS), :], w4) * window
        at_l = row == L
        hb0_ref[...] += jnp.sum(jnp.where(at_l, g, 0.0), axis=0, keepdims=True)
        g = jnp.where(at_l, 0.0, g)
        for q in range(FILT_ROWS // N2):
            dst = pl.multiple_of((c * (FILT_ROWS // N2) + q) * gp, SUBLANES)
            g_scr[pl.ds(dst, N2), :] = g[q * N2:(q + 1) * N2]
        return carry

    lax.fori_loop(0, n // FILT_ROWS, gen, 0)

    def stage1(j2, carry):
        x = g_scr[pl.ds(j2, N1, stride=gp), :]
        a_scr[pl.ds(pl.multiple_of(j2 * ap, SUBLANES), 2 * N1), :] = _dot(t1g_ref[j2], x)
        return carry

    lax.fori_loop(0, N2, stage1, 0, unroll=FFT_UNROLL)

    def stage2(k1, carry):
        ar = a_scr[pl.ds(k1, N2, stride=ap), :]
        ai = a_scr[pl.ds(N1 + k1, N2, stride=ap), :]
        z = _dot(f2f_ref[...], jnp.concatenate([ar, ai], axis=0))
        hspec_ref[pl.ds(pl.multiple_of(k1 * 2 * N2, 2 * N2), 2 * N2), :] = z.astype(hspec_ref.dtype)
        return carry

    lax.fori_loop(0, N1, stage2, 0, unroll=FFT_UNROLL)


def _filt_fft(h3, w4, L, d_hy):
    tb = _dft_tables(L)
    N1, N2 = tb["N1"], tb["N2"]
    n = 2 * L
    fw = h3.shape[1]
    nt = d_hy // LANES
    t1g = jnp.asarray(tb["t1g"]).astype(BF16)
    f2f = jnp.asarray(tb["f2f"]).astype(BF16)
    return pl.pallas_call(
        functools.partial(_filt_fft_kernel, L=L, N1=N1, N2=N2),
        grid=(nt,),
        in_specs=[pl.BlockSpec((n, fw), lambda c: (0, 0)),
                  pl.BlockSpec((fw, LANES), lambda c: (0, c)),
                  pl.BlockSpec((fw, LANES), lambda c: (0, c + nt)),
                  pl.BlockSpec((1, LANES), lambda c: (0, c)),
                  pl.BlockSpec(t1g.shape, lambda c: (0, 0, 0)),
                  pl.BlockSpec(f2f.shape, lambda c: (0, 0))],
        out_specs=[pl.BlockSpec((2 * n, LANES), lambda c: (0, c)),
                   pl.BlockSpec((SUBLANES, LANES), lambda c: (0, c))],
        out_shape=[jax.ShapeDtypeStruct((2 * n, d_hy), BF16), jax.ShapeDtypeStruct((SUBLANES, d_hy), F32)],
        scratch_shapes=[pltpu.VMEM((N1 * (N2 + PITCH_PAD), LANES), F32),
                        pltpu.VMEM((N2 * (2 * N1 + PITCH_PAD), LANES), F32)],
        compiler_params=_cparams(("parallel",), 48),
        name="filt_fft",
    )(h3, w4, w4, jnp.asarray(_decay_rates(d_hy)), t1g, f2f)


def _hy_conv_kernel(vg_ref, hspec_ref, skip_ref, hb0_ref, t1_ref, f2f_ref, f2i_ref, t2_ref, y_ref,
                    x_scr, a_scr, b_scr, *, N1, N2):
    N1h = N1 // 2
    xp = N2 + PITCH_PAD
    ap = 2 * N1 + PITCH_PAD
    bp = 2 * N2 + PITCH_PAD

    for b in range(2):
        for j1 in range(N1h):
            x_scr[b, pl.ds(j1 * xp, N2), :] = vg_ref[b, pl.ds(j1 * N2, N2), :]

    def stage1(j2, carry):
        x = jnp.concatenate([x_scr[0, pl.ds(j2, N1h, stride=xp), :],
                             x_scr[1, pl.ds(j2, N1h, stride=xp), :]], axis=0)
        a_scr[pl.ds(pl.multiple_of(j2 * ap, SUBLANES), 2 * N1), :] = _dot(t1_ref[j2], x)
        return carry

    lax.fori_loop(0, N2, stage1, 0, unroll=FFT_UNROLL)

    def stage2(k1, carry):
        ar = a_scr[pl.ds(k1, N2, stride=ap), :]
        ai = a_scr[pl.ds(N1 + k1, N2, stride=ap), :]
        z = _dot(f2f_ref[...], jnp.concatenate([ar, ai], axis=0))
        zr, zi = z[:N2], z[N2:]
        base = pl.multiple_of(k1 * 2 * N2, 2 * N2)
        hr = hspec_ref[pl.ds(base, N2), :].astype(F32)
        hi = hspec_ref[pl.ds(base + N2, N2), :].astype(F32)
        prod = jnp.concatenate([zr * hr - zi * hi, zr * hi + zi * hr], axis=0)
        b_scr[pl.ds(pl.multiple_of(k1 * bp, SUBLANES), 2 * N2), :] = _dot(f2i_ref[...], prod)
        return carry

    lax.fori_loop(0, N1, stage2, 0, unroll=FFT_UNROLL)

    skip = skip_ref[...] + hb0_ref[0:1, :]

    def stage3(j2, carry):
        b = jnp.concatenate([b_scr[pl.ds(j2, N1, stride=bp), :],
                             b_scr[pl.ds(N2 + j2, N1, stride=bp), :]], axis=0)
        y = _dot(t2_ref[j2], b)
        x_scr[0, pl.ds(j2, N1h, stride=xp), :] = y[:N1h] + x_scr[0, pl.ds(j2, N1h, stride=xp), :] * skip
        x_scr[1, pl.ds(j2, N1h, stride=xp), :] = y[N1h:] + x_scr[1, pl.ds(j2, N1h, stride=xp), :] * skip
        return carry

    lax.fori_loop(0, N2, stage3, 0, unroll=FFT_UNROLL)

    for b in range(2):
        for j1 in range(N1h):
            y_ref[b, pl.ds(j1 * N2, N2), :] = x_scr[b, pl.ds(j1 * xp, N2), :]


def _hy_conv(vg, hspec, skip, hb0):
    B, L, d_hy = vg.shape
    assert B % 2 == 0
    tb = _dft_tables(L)
    N1, N2 = tb["N1"], tb["N2"]
    n = 2 * L
    nt = d_hy // LANES
    t1, t2 = (jnp.asarray(tb[k]).astype(BF16) for k in ("t1", "t2"))
    f2f, f2i = (jnp.asarray(tb[k]).astype(BF16) for k in ("f2f", "f2i"))
    const3 = lambda a: pl.BlockSpec(a.shape, lambda b, c: (0, 0, 0))
    const2 = lambda a: pl.BlockSpec(a.shape, lambda b, c: (0, 0))
    return pl.pallas_call(
        functools.partial(_hy_conv_kernel, N1=N1, N2=N2),
        grid=(B // 2, nt),
        in_specs=[pl.BlockSpec((2, L, LANES), lambda b, c: (b, 0, c)),
                  pl.BlockSpec((2 * n, LANES), lambda b, c: (0, c)),
                  pl.BlockSpec((1, LANES), lambda b, c: (0, c)),
                  pl.BlockSpec((SUBLANES, LANES), lambda b, c: (0, c)),
                  const3(t1), const2(f2f), const2(f2i), const3(t2)],
        out_specs=pl.BlockSpec((2, L, LANES), lambda b, c: (b, 0, c)),
        out_shape=jax.ShapeDtypeStruct((B, L, d_hy), F32),
        scratch_shapes=[pltpu.VMEM((2, (N1 // 2) * (N2 + PITCH_PAD), LANES), F32),
                        pltpu.VMEM((N2 * (2 * N1 + PITCH_PAD), LANES), F32),
                        pltpu.VMEM((N1 * (2 * N2 + PITCH_PAD), LANES), F32)],
        compiler_params=_cparams(("parallel", "parallel"), 58),
        name="hy_conv",
    )(vg, hspec, skip.reshape(1, d_hy).astype(F32), hb0, t1, f2f, f2i, t2)


def _hyena_branch(p3, conv_w, conv_b, fw1, fb1, fw2, fb2, fw3, fb3, fw4, freq, skip, d_hy):
    B, L, _ = p3.shape
    x0c, vg = _hy_pre(p3, conv_w, conv_b, d_hy)
    zt = jnp.asarray(_filter_positions(L, fw1.shape[0]).T)
    h3 = _filt_mlp(zt, fw1, fb1, fw2, fb2, fw3, fb3, freq).T
    hspec, hb0 = _filt_fft(h3, fw4, L, d_hy)
    return x0c, _hy_conv(vg, hspec, skip, hb0)


ROUTE_GATE, ROUTE_EXPERT, ROUTE_RANK = 0, 2, 4
MIX_ROWS = 256


def _mix_route_kernel(yc_ref, x0_ref, of_ref, ob_ref, z_ref, hnw_ref, gnw_ref, x_ref, wo_ref, n2w_ref, wr_ref, br_ref,
                      x2_ref, u_ref, route_ref, route_t_ref, cnt_ref, run_scr, *, n_heads, head_dim, n_groups,
                      per_group):
    tm = x_ref.shape[0]
    G, P = n_groups, per_group

    @pl.when(pl.program_id(0) == 0)
    def _():
        run_scr[...] = jnp.zeros_like(run_scr)

    run = run_scr[0:1, :]
    blocks = [pl.ds(s * MIX_ROWS, MIX_ROWS) for s in range(tm // MIX_ROWS)]
    us = []
    for rs in blocks:
        yh = yc_ref[rs, :] * x0_ref[rs, :]
        yh = yh * lax.rsqrt(jnp.mean(yh * yh, axis=-1, keepdims=True) + EPS) * hnw_ref[...]
        parts = [yh.astype(BF16)]
        for h in range(n_heads):
            sl = slice(h * head_dim, (h + 1) * head_dim)
            o = of_ref[rs, sl] + ob_ref[rs, sl]
            o = o * lax.rsqrt(jnp.mean(o * o, axis=-1, keepdims=True) + EPS) * gnw_ref[...] * _silu(z_ref[rs, sl])
            parts.append(o.astype(BF16))
        ymix = jnp.concatenate(parts, axis=-1)
        x2 = x_ref[rs, :] + jnp.dot(ymix, wo_ref[...], preferred_element_type=F32)
        x2_ref[rs, :] = x2
        u = x2 * lax.rsqrt(jnp.mean(x2 * x2, axis=-1, keepdims=True) + EPS) * n2w_ref[...]
        u_ref[rs, :] = u
        us.append(u)

    all_logits = [_dot(u, wr_ref[...]) + br_ref[...] for u in us]
    for s, (rs, logits) in enumerate(zip(blocks, all_logits)):
        lane = lax.broadcasted_iota(jnp.int32, logits.shape, 1)
        neg = jnp.float32(-jnp.inf)
        big = jnp.int32(4 * LANES)
        first = lambda hit: jnp.min(jnp.where(hit, lane, big), axis=-1, keepdims=True)
        gl = jnp.where(lane < G, logits, neg)
        gmax = jnp.max(gl, axis=-1, keepdims=True)
        gidx = first(gl == gmax)
        grp_gate = 1.0 / jnp.sum(jnp.exp(gl - gmax), axis=-1, keepdims=True)
        in_grp = (lane >= G) & (lane < G + G * P) & (((lane - G) // P) == gidx)
        ll = jnp.where(in_grp, logits, neg)
        m1 = jnp.max(ll, axis=-1, keepdims=True)
        i1 = first(ll == m1)
        denom = jnp.sum(jnp.exp(ll - m1), axis=-1, keepdims=True)
        ll2 = jnp.where(lane == i1, neg, ll)
        m2 = jnp.max(ll2, axis=-1, keepdims=True)
        i2 = first(ll2 == m2)
        p1 = 1.0 / denom
        p2 = jnp.exp(m2 - m1) / denom
        gate1 = grp_gate * (p1 / (p1 + p2))
        gate2 = grp_gate * (p2 / (p1 + p2))
        e1 = i1 - G
        e2 = i2 - G

        oh1 = jnp.where(lane == e1, 1.0, 0.0)
        oh2 = jnp.where(lane == e2, 1.0, 0.0)
        oh = oh1 + oh2
        ii = lax.broadcasted_iota(jnp.int32, (MIX_ROWS, MIX_ROWS), 0)
        jj = lax.broadcasted_iota(jnp.int32, (MIX_ROWS, MIX_ROWS), 1)
        before = _dot(jnp.where(ii > jj, 1.0, 0.0), oh) + run
        r1 = jnp.sum(oh1 * before, axis=-1, keepdims=True)
        r2 = jnp.sum(oh2 * before, axis=-1, keepdims=True)
        run = run + jnp.sum(oh, axis=0, keepdims=True)

        rec = jnp.where(lane == ROUTE_GATE, gate1, 0.0)
        rec = jnp.where(lane == ROUTE_GATE + 1, gate2, rec)
        rec = jnp.where(lane == ROUTE_EXPERT, e1.astype(F32), rec)
        rec = jnp.where(lane == ROUTE_EXPERT + 1, e2.astype(F32), rec)
        rec = jnp.where(lane == ROUTE_RANK, r1, rec)
        rec = jnp.where(lane == ROUTE_RANK + 1, r2, rec)
        route_ref[rs, :] = rec
        route_t_ref[:, s * MIX_ROWS:(s + 1) * MIX_ROWS] = jnp.transpose(rec)[:SUBLANES, :]

    run_scr[...] = jnp.broadcast_to(run, run_scr.shape)
    cnt_ref[...] = run_scr[...]


def _mix_route(yconv, x0c, o_f, o_b, p, z_col, hy_norm_w, gdn_norm_w, xf, w_out_bf16, norm2_w, wr, br,
               n_heads, head_dim, n_groups, per_group, tm=512):
    M, D = xf.shape
    d_hy = yconv.shape[1]
    d_gdn = o_f.shape[1]
    assert z_col % d_gdn == 0 and n_groups * (per_group + 1) <= LANES
    zb = z_col // d_gdn
    row = lambda i: (i, 0)
    const = lambda i: (0, 0)
    kern = functools.partial(_mix_route_kernel, n_heads=n_heads, head_dim=head_dim,
                             n_groups=n_groups, per_group=per_group)
    return pl.pallas_call(
        kern,
        grid=(M // tm,),
        in_specs=[pl.BlockSpec((tm, d_hy), row), pl.BlockSpec((tm, d_hy), row),
                  pl.BlockSpec((tm, d_gdn), row), pl.BlockSpec((tm, d_gdn), row),
                  pl.BlockSpec((tm, d_gdn), lambda i: (i, zb)),
                  pl.BlockSpec((1, d_hy), const), pl.BlockSpec((1, head_dim), const),
                  pl.BlockSpec((tm, D), row), pl.BlockSpec(w_out_bf16.shape, const, pipeline_mode=pl.Buffered(1)),
                  pl.BlockSpec((1, D), const), pl.BlockSpec((D, LANES), const), pl.BlockSpec((1, LANES), const)],
        out_specs=[pl.BlockSpec((tm, D), row), pl.BlockSpec((tm, D), row),
                   pl.BlockSpec((tm, LANES), row), pl.BlockSpec((SUBLANES, tm), lambda i: (0, i)),
                   pl.BlockSpec((SUBLANES, LANES), const)],
        out_shape=[jax.ShapeDtypeStruct((M, D), F32), jax.ShapeDtypeStruct((M, D), F32),
                   jax.ShapeDtypeStruct((M, LANES), F32), jax.ShapeDtypeStruct((SUBLANES, M), F32),
                   jax.ShapeDtypeStruct((SUBLANES, LANES), F32)],
        scratch_shapes=[pltpu.VMEM((SUBLANES, LANES), F32)],
        compiler_params=_cparams(("arbitrary",), 58),
        name="mix_route",
    )(yconv, x0c, o_f, o_b, p, hy_norm_w.reshape(1, d_hy), gdn_norm_w.reshape(1, head_dim), xf, w_out_bf16,
      norm2_w.reshape(1, D), wr, br)


DMA_GROUP = 8


def _experts_kernel(te_ref, nv_ref, na_ref, tgt_nxt_ref, tgt_ref, u_hbm, w1_ref, w3_ref, w2_ref, out_hbm,
                    xbuf, ybuf, w1b, w3b, w2b, gsem, ssem, *, n_tokens):
    i = pl.program_id(0)
    na = na_ref[0]
    slot = lax.rem(i, 2)
    other = 1 - slot

    def rows(count, one, group):
        n_groups = count // DMA_GROUP

        def grp(g, carry):
            r0 = pl.multiple_of(g * DMA_GROUP, DMA_GROUP)
            if group is None:
                for j in range(DMA_GROUP):
                    one(r0 + j)
            else:
                group(r0)
            return carry

        lax.fori_loop(0, n_groups, grp, 0)

        def tail(r, carry):
            one(r)
            return carry

        lax.fori_loop(n_groups * DMA_GROUP, count, tail, 0)

    def gather(tgt, s, r):
        t = tgt[r]
        tok = jnp.where(t >= n_tokens, t - n_tokens, t)
        return pltpu.make_async_copy(u_hbm.at[pl.ds(tok, 1)], xbuf.at[s, pl.ds(r, 1)], gsem.at[s])

    def gather_wait(s, n):
        return pltpu.make_async_copy(u_hbm.at[pl.ds(0, n)], xbuf.at[s, pl.ds(0, n)], gsem.at[s])

    def scatter(s, r):
        return pltpu.make_async_copy(ybuf.at[s, pl.ds(r, 1)], out_hbm.at[pl.ds(tgt_ref[r], 1)], ssem.at[s])

    def scatter_wait(s, n):
        return pltpu.make_async_copy(ybuf.at[s, pl.ds(0, n)], out_hbm.at[pl.ds(0, n)], ssem.at[s])

    def wait_rows(count, mk):
        rows(count, lambda r: mk(1).wait(), lambda r0: mk(DMA_GROUP).wait())

    @pl.when(i < na)
    def _():
        @pl.when(i == 0)
        def _():
            xbuf[...] = jnp.zeros_like(xbuf)
            rows(nv_ref[0], lambda r: gather(tgt_ref, 0, r).start(), None)

        @pl.when(i + 1 < na)
        def _():
            rows(nv_ref[i + 1], lambda r: gather(tgt_nxt_ref, other, r).start(), None)

        @pl.when((i == 0) | (te_ref[i] != te_ref[jnp.maximum(i - 1, 0)]))
        def _():
            w1b[...] = w1_ref[...].astype(BF16)
            w3b[...] = w3_ref[...].astype(BF16)
            w2b[...] = w2_ref[...].astype(BF16)

        wait_rows(nv_ref[i], lambda n: gather_wait(slot, n))
        x = xbuf[slot].astype(BF16)
        h = _silu(jnp.dot(x, w1b[...], preferred_element_type=F32)) * jnp.dot(x, w3b[...], preferred_element_type=F32)
        ybuf[slot] = jnp.dot(h.astype(BF16), w2b[...], preferred_element_type=F32)
        rows(nv_ref[i], lambda r: scatter(slot, r).start(), None)

        @pl.when(i > 0)
        def _():
            wait_rows(nv_ref[jnp.maximum(i - 1, 0)], lambda n: scatter_wait(other, n))

        @pl.when(i == na - 1)
        def _():
            wait_rows(nv_ref[i], lambda n: scatter_wait(slot, n))


def _experts(u, slot_tgt, tile_expert, tile_valid, n_active, w1, w3, w2):
    T, D = u.shape
    E, _, de = w1.shape
    TB = EXPERT_ROWS
    n_tiles = slot_tgt.shape[0] // TB
    wmap = lambda i, te, nv, na: (te[i], 0, 0)
    grid_spec = pltpu.PrefetchScalarGridSpec(
        num_scalar_prefetch=3,
        grid=(n_tiles,),
        in_specs=[pl.BlockSpec((TB,), lambda i, te, nv, na: (jnp.minimum(i + 1, n_tiles - 1),),
                               memory_space=pltpu.SMEM),
                  pl.BlockSpec((TB,), lambda i, te, nv, na: (i,), memory_space=pltpu.SMEM),
                  pl.BlockSpec(memory_space=pl.ANY),
                  pl.BlockSpec((None, D, de), wmap), pl.BlockSpec((None, D, de), wmap),
                  pl.BlockSpec((None, de, D), wmap)],
        out_specs=pl.BlockSpec(memory_space=pl.ANY),
        scratch_shapes=[pltpu.VMEM((2, TB, D), F32), pltpu.VMEM((2, TB, D), F32),
                        pltpu.VMEM((D, de), BF16), pltpu.VMEM((D, de), BF16), pltpu.VMEM((de, D), BF16),
                        pltpu.SemaphoreType.DMA((2,)), pltpu.SemaphoreType.DMA((2,))],
    )
    return pl.pallas_call(
        functools.partial(_experts_kernel, n_tokens=T),
        grid_spec=grid_spec,
        out_shape=jax.ShapeDtypeStruct((2 * T, D), F32),
        compiler_params=_cparams(("arbitrary",), 52),
        name="experts",
    )(tile_expert, tile_valid, n_active, slot_tgt, slot_tgt, u, w1, w3, w2)


def _slot_table_kernel(dest_ref, zero_hbm, o_ref, sem, *, n_tokens):
    fill = pltpu.make_async_copy(zero_hbm, o_ref, sem)
    fill.start()
    fill.wait()

    def put(a, carry):
        o_ref[dest_ref[a]] = a
        return carry

    lax.fori_loop(0, dest_ref.shape[0], put, 0, unroll=8)


def _slot_table(dest, n_slots, n_tokens):
    smem = pl.BlockSpec(memory_space=pltpu.SMEM)
    return pl.pallas_call(
        functools.partial(_slot_table_kernel, n_tokens=n_tokens),
        in_specs=[smem, pl.BlockSpec(memory_space=pl.ANY)],
        out_specs=smem,
        out_shape=jax.ShapeDtypeStruct((n_slots,), jnp.int32),
        scratch_shapes=[pltpu.SemaphoreType.DMA],
        name="slot_table",
    )(dest, jnp.zeros((n_slots,), jnp.int32))


def _dispatch_tables(route_t, counts, n_experts):
    T = route_t.shape[1]
    TB = EXPERT_ROWS
    e = route_t[ROUTE_EXPERT:ROUTE_EXPERT + 2].astype(jnp.int32)
    rank = route_t[ROUTE_RANK:ROUTE_RANK + 2].astype(jnp.int32)
    cnt = counts[0, :n_experts].astype(jnp.int32)
    padded = (cnt + TB - 1) // TB * TB
    pad_end = jnp.cumsum(padded)
    pad_start = pad_end - padded
    dest = pad_start[e] + rank
    n_tiles = -(-(2 * T + n_experts * (TB - 1)) // TB)
    slot_tgt = _slot_table(dest.reshape(-1), n_tiles * TB, T)
    start = jnp.arange(n_tiles, dtype=jnp.int32) * TB
    tile_expert = jnp.minimum(jnp.searchsorted(pad_end, start, side="right"), n_experts - 1).astype(jnp.int32)
    tile_valid = jnp.clip(cnt[tile_expert] - (start - pad_start[tile_expert]), 0, TB)
    tile_valid = jnp.where(start < pad_end[-1], tile_valid, 0).astype(jnp.int32)
    n_active = (pad_end[-1] // TB).astype(jnp.int32).reshape(1)
    return slot_tgt, tile_expert, tile_valid, n_active


def _combine_kernel(x2_ref, e0_ref, e1_ref, route_ref, w_ref, o_ref, *, final_norm):
    r = route_ref[...]
    y = x2_ref[...] + r[:, ROUTE_GATE:ROUTE_GATE + 1] * e0_ref[...] + r[:, ROUTE_GATE + 1:ROUTE_GATE + 2] * e1_ref[...]
    if final_norm:
        y = y * lax.rsqrt(jnp.mean(y * y, axis=-1, keepdims=True) + EPS) * w_ref[...]
    o_ref[...] = y


def _combine(x2, planes, route, norm_w, final_norm, tm=512):
    M, D = x2.shape
    return pl.pallas_call(
        functools.partial(_combine_kernel, final_norm=final_norm),
        grid=(M // tm,),
        in_specs=[pl.BlockSpec((tm, D), lambda i: (i, 0)),
                  pl.BlockSpec((None, tm, D), lambda i: (0, i, 0)),
                  pl.BlockSpec((None, tm, D), lambda i: (1, i, 0)),
                  pl.BlockSpec((tm, LANES), lambda i: (i, 0)),
                  pl.BlockSpec((1, D), lambda i: (0, 0))],
        out_specs=pl.BlockSpec((tm, D), lambda i: (i, 0)),
        out_shape=jax.ShapeDtypeStruct((M, D), F32),
        compiler_params=_cparams(("parallel",), 48),
        name="combine",
    )(x2, planes, planes, route, norm_w.reshape(1, D))


def kernel(x, norm1_w, w_in, hy_conv_w, hy_conv_b, hy_filt_w1, hy_filt_b1, hy_filt_w2, hy_filt_b2, hy_filt_w3, hy_filt_b3, hy_filt_w4, hy_sin_freq, hy_skip, hy_norm_w, gdn_conv_w, gdn_a_log_f, gdn_a_log_b, gdn_dt_bias_f, gdn_dt_bias_b, gdn_norm_w, w_out, norm2_w, router_group_w, router_group_b, router_expert_w, router_expert_b, exp_w1, exp_w3, exp_w2, final_norm_w):
    B, L, D = x.shape
    M = B * L
    depth = w_in.shape[0]
    d_hy = hy_skip.shape[-1]
    H = gdn_a_log_f.shape[-1]
    Dh = gdn_norm_w.shape[-1]
    d_gdn = H * Dh
    n_main = 3 * d_hy + 4 * d_gdn
    G = router_group_w.shape[-1]
    E = router_expert_w.shape[-1]
    xf = x.reshape(M, D)
    for l in range(depth):
        w_main = _cast_cols(w_in[l], n_main)
        w_gate = jnp.pad(w_in[l, :, n_main:], ((0, 0), (0, LANES - 4 * H)))
        p, pg = _inproj(xf, norm1_w[l], w_main, n_main, w_gate)
        p3 = p.reshape(B, L, n_main)
        x0c, yconv = _hyena_branch(p3, hy_conv_w[l], hy_conv_b[l], hy_filt_w1[l], hy_filt_b1[l], hy_filt_w2[l],
                                   hy_filt_b2[l], hy_filt_w3[l], hy_filt_b3[l], hy_filt_w4[l], hy_sin_freq[l],
                                   hy_skip[l], d_hy)
        o_f, o_b = _gdn_branch(p3, pg, gdn_conv_w[l], gdn_a_log_f[l], gdn_a_log_b[l], gdn_dt_bias_f[l],
                               gdn_dt_bias_b[l], 3 * d_hy, H, Dh)
        wr = jnp.pad(jnp.concatenate([router_group_w[l], router_expert_w[l]], axis=1), ((0, 0), (0, LANES - G - E)))
        br = jnp.pad(jnp.concatenate([router_group_b[l], router_expert_b[l]]), (0, LANES - G - E)).reshape(1, LANES)
        x2, u, route, route_t, counts = _mix_route(
            yconv.reshape(M, d_hy), x0c.reshape(M, d_hy), o_f.reshape(M, d_gdn), o_b.reshape(M, d_gdn), p,
            3 * d_hy + 3 * d_gdn, hy_norm_w[l], gdn_norm_w[l], xf, w_out[l].astype(BF16), norm2_w[l], wr, br,
            H, Dh, G, E // G)
        slot_tgt, tile_expert, tile_valid, n_active = _dispatch_tables(route_t, counts, E)
        planes = _experts(u, slot_tgt, tile_expert, tile_valid, n_active, exp_w1[l], exp_w3[l], exp_w2[l])
        xf = _combine(x2, planes.reshape(2, M, D), route, final_norm_w, final_norm=(l == depth - 1))
    return xf.reshape(B, L, D)
```

```python
import functools
import math

import jax
import jax.numpy as jnp
import numpy as np
from jax import lax
from jax.experimental import pallas as pl
from jax.experimental.pallas import tpu as pltpu

F32 = jnp.float32
BF16 = jnp.bfloat16
EPS = 1e-6
LANES = 128
SUBLANES = 8
VMEM_BYTES_V7X = 64 * 1024 * 1024
GDN_CHUNK = 64
FFT_N2 = 128
EXPERT_ROWS = 256
DECAY_TARGET = 1e-2
FAST_DECAY_PCT = 0.3
SLOW_DECAY_PCT = 1.5


def _cparams(sem, vmem_mb):
    return pltpu.CompilerParams(dimension_semantics=sem, vmem_limit_bytes=int(vmem_mb * 1024 * 1024))


def _dot(a, b):
    return jnp.dot(a.astype(BF16), b.astype(BF16), preferred_element_type=F32)


def _dot_nt(a, b):
    return lax.dot_general(a.astype(BF16), b.astype(BF16), (((1,), (1,)), ((), ())), preferred_element_type=F32)


def _dot_tn(a, b):
    return lax.dot_general(a.astype(BF16), b.astype(BF16), (((0,), (0,)), ((), ())), preferred_element_type=F32)


def _dot_hi(a, b):
    return jnp.dot(a, b, preferred_element_type=F32, precision=lax.Precision.HIGHEST)


def _silu(x):
    return x * jax.nn.sigmoid(x)


def _inproj_kernel(x_ref, nw_ref, wt_ref, wgt_ref, p_ref, g_ref, h_scr):
    @pl.when(pl.program_id(1) == 0)
    def _():
        x = x_ref[...]
        h = x * lax.rsqrt(jnp.mean(x * x, axis=-1, keepdims=True) + EPS) * nw_ref[...]
        h_scr[...] = h.astype(BF16)
        g = _dot_nt(h, wgt_ref[...])
        g_ref[...] = jnp.concatenate([g, jnp.zeros((g.shape[0], LANES - g.shape[1]), F32)], axis=1)

    p_ref[...] = _dot_nt(h_scr[...], wt_ref[...])


def _inproj(xf, norm_w, wt, n_main, n_gate, tm=2048, tn=512):
    M, D = xf.shape
    assert n_main % tn == 0 and M % tm == 0 and n_main % n_gate == 0 and n_gate % SUBLANES == 0
    return pl.pallas_call(
        _inproj_kernel,
        grid=(M // tm, n_main // tn),
        in_specs=[
            pl.BlockSpec((tm, D), lambda i, j: (i, 0), pipeline_mode=pl.Buffered(1)),
            pl.BlockSpec((1, D), lambda i, j: (0, 0)),
            pl.BlockSpec((tn, D), lambda i, j: (j, 0)),
            pl.BlockSpec((n_gate, D), lambda i, j: (n_main // n_gate, 0)),
        ],
        out_specs=[
            pl.BlockSpec((tm, tn), lambda i, j: (i, j)),
            pl.BlockSpec((tm, LANES), lambda i, j: (i, 0)),
        ],
        out_shape=[jax.ShapeDtypeStruct((M, n_main), F32), jax.ShapeDtypeStruct((M, LANES), F32)],
        scratch_shapes=[pltpu.VMEM((tm, D), BF16)],
        compiler_params=_cparams(("parallel", "arbitrary"), 57),
        name="inproj",
    )(xf, norm_w.reshape(1, D), wt, wt)


def _conv3_rows(ref, r0, rows, w, n_rows):
    cur = ref[pl.ds(r0, rows), :]
    lo = jnp.maximum(r0 - SUBLANES, 0)
    hi = jnp.minimum(r0 + rows, n_rows - SUBLANES)
    prev8 = ref[pl.ds(pl.multiple_of(lo, SUBLANES), SUBLANES), :]
    next8 = ref[pl.ds(pl.multiple_of(hi, SUBLANES), SUBLANES), :]
    prev_row = jnp.where(r0 > 0, prev8[SUBLANES - 1:SUBLANES, :], 0.0)
    next_row = jnp.where(r0 + rows < n_rows, next8[0:1, :], 0.0)
    row = lax.broadcasted_iota(jnp.int32, cur.shape, 0)
    xm = jnp.where(row == 0, prev_row, pltpu.roll(cur, 1, 0))
    xp = jnp.where(row == rows - 1, next_row, pltpu.roll(cur, rows - 1, 0))
    return xm * w[0:1, :] + cur * w[1:2, :] + xp * w[2:3, :]


CONV_ROWS = 256


def _hy_pre_kernel(x0_ref, x1_ref, v_ref, w0_ref, w1_ref, w2_ref, b0_ref, b1_ref, b2_ref, x0c_ref, vg_ref):
    L = x0_ref.shape[0]
    w0, w1, w2 = w0_ref[...], w1_ref[...], w2_ref[...]
    b0, b1, b2 = b0_ref[...], b1_ref[...], b2_ref[...]

    def body(c, carry):
        r0 = pl.multiple_of(c * CONV_ROWS, CONV_ROWS)
        x0c_ref[pl.ds(r0, CONV_ROWS), :] = _conv3_rows(x0_ref, r0, CONV_ROWS, w0, L) + b0
        x1c = _conv3_rows(x1_ref, r0, CONV_ROWS, w1, L) + b1
        vc = _conv3_rows(v_ref, r0, CONV_ROWS, w2, L) + b2
        vg_ref[pl.ds(r0, CONV_ROWS), :] = vc * x1c
        return carry

    lax.fori_loop(0, L // CONV_ROWS, body, 0, unroll=2)


def _hy_pre(p3, conv_w, conv_b, d_hy):
    B, L, _ = p3.shape
    nt = d_hy // LANES
    bias = conv_b.reshape(1, -1)
    pspec = lambda off: pl.BlockSpec((None, L, LANES), lambda b, c: (b, 0, c + off))
    wspec = lambda off: pl.BlockSpec((3, LANES), lambda b, c: (0, c + off))
    bspec = lambda off: pl.BlockSpec((1, LANES), lambda b, c: (0, c + off))
    ospec = pl.BlockSpec((None, L, LANES), lambda b, c: (b, 0, c))
    return pl.pallas_call(
        _hy_pre_kernel,
        grid=(B, nt),
        in_specs=[pspec(0), pspec(nt), pspec(2 * nt), wspec(0), wspec(nt), wspec(2 * nt),
                  bspec(0), bspec(nt), bspec(2 * nt)],
        out_specs=[ospec, ospec],
        out_shape=[jax.ShapeDtypeStruct((B, L, d_hy), F32)] * 2,
        compiler_params=_cparams(("parallel", "parallel"), 40),
        name="hy_pre",
    )(p3, p3, p3, conv_w, conv_w, conv_w, bias, bias, bias)


def _gdn_pre_kernel(x_ref, w_ref, o_ref, *, n_heads, head_dim):
    L = x_ref.shape[0]
    w = w_ref[...]
    c = pl.program_id(1)
    q_scale = jnp.where(c < n_heads, head_dim ** -0.5, 1.0)
    is_qk = c < 2 * n_heads

    def body(i, carry):
        r0 = pl.multiple_of(i * CONV_ROWS, CONV_ROWS)
        y = _silu(_conv3_rows(x_ref, r0, CONV_ROWS, w, L))
        inv = lax.rsqrt(jnp.sum(y * y, axis=-1, keepdims=True) + EPS) * q_scale
        o_ref[pl.ds(r0, CONV_ROWS), :] = y * jnp.where(is_qk, inv, 1.0)
        return carry

    lax.fori_loop(0, L // CONV_ROWS, body, 0, unroll=2)


def _gdn_pre(p3, conv_w, col0, n_heads, head_dim):
    B, L, _ = p3.shape
    assert head_dim == LANES
    nt = 3 * n_heads
    off = col0 // LANES
    return pl.pallas_call(
        functools.partial(_gdn_pre_kernel, n_heads=n_heads, head_dim=head_dim),
        grid=(B, nt),
        in_specs=[pl.BlockSpec((None, L, LANES), lambda b, c: (b, 0, c + off)),
                  pl.BlockSpec((3, LANES), lambda b, c: (0, c))],
        out_specs=pl.BlockSpec((None, L, LANES), lambda b, c: (b, 0, c)),
        out_shape=jax.ShapeDtypeStruct((B, L, nt * LANES), F32),
        compiler_params=_cparams(("parallel", "parallel"), 24),
        name="gdn_pre",
    )(p3, conv_w)


GATE_ROWS = 512


def _gdn_gates_kernel(pg_ref, alog_ref, dtb_ref, o_ref, *, n_heads):
    H = n_heads
    x = pg_ref[...]
    beta = jax.nn.sigmoid(x)
    z = x + dtb_ref[...]
    softplus = jnp.maximum(z, 0.0) + jnp.log1p(jnp.exp(-jnp.abs(z)))
    g = -jnp.exp(alog_ref[...]) * softplus
    pos = lax.broadcasted_iota(jnp.int32, x.shape, 0) & (GDN_CHUNK - 1)
    gc_f = g
    gc_b = g
    step = 1
    while step < GDN_CHUNK:
        gc_f = gc_f + jnp.where(pos >= step, pltpu.roll(gc_f, step, 0), 0.0)
        gc_b = gc_b + jnp.where(pos < GDN_CHUNK - step, pltpu.roll(gc_b, GATE_ROWS - step, 0), 0.0)
        step *= 2
    g_tot = pltpu.roll(gc_f + gc_b - g, 4 * H, 1)
    lane = lax.broadcasted_iota(jnp.int32, x.shape, 1)
    out = jnp.where(lane < 2 * H, beta,
                    jnp.where(lane < 3 * H, gc_f,
                              jnp.where(lane < 4 * H, gc_b,
                                        jnp.where((lane >= 6 * H) & (lane < 8 * H), g_tot, 0.0))))
    o_ref[...] = out


def _gdn_gates(pg, a_log_f, a_log_b, dt_bias_f, dt_bias_b, n_heads):
    M = pg.shape[0]
    H = n_heads
    assert 8 * H <= LANES
    pad = lambda a, b: jnp.concatenate([jnp.zeros((2 * H,), F32), a.astype(F32), b.astype(F32),
                                        jnp.zeros((LANES - 4 * H,), F32)]).reshape(1, LANES)
    return pl.pallas_call(
        functools.partial(_gdn_gates_kernel, n_heads=H),
        grid=(M // GATE_ROWS,),
        in_specs=[pl.BlockSpec((GATE_ROWS, LANES), lambda i: (i, 0)),
                  pl.BlockSpec((1, LANES), lambda i: (0, 0)),
                  pl.BlockSpec((1, LANES), lambda i: (0, 0))],
        out_specs=pl.BlockSpec((GATE_ROWS, LANES), lambda i: (i, 0)),
        out_shape=jax.ShapeDtypeStruct((M, LANES), F32),
        compiler_params=_cparams(("parallel",), 24),
        name="gdn_gates",
    )(pg, pad(a_log_f, a_log_b), pad(dt_bias_f, dt_bias_b))


def _delta_chunks(q, k, v, beta, gc_col, gc_row, gtot, state, lower):
    n = len(q)
    C = q[0].shape[0]
    D = k[0].shape[1]
    ii = lax.broadcasted_iota(jnp.int32, (C, C), 0)
    jj = lax.broadcasted_iota(jnp.int32, (C, C), 1)
    eye = jnp.where(ii == jj, 1.0, 0.0)
    incl = [(ii >= jj) if lo else (ii <= jj) for lo in lower]
    strict = [(ii > jj) if lo else (ii < jj) for lo in lower]
    rng = range(n)
    decay = [jnp.where(incl[i], jnp.exp(jnp.where(incl[i], gc_col[i] - gc_row[i], 0.0)), 0.0) for i in rng]
    kb = [k[i] * beta[i] for i in rng]
    kk = [_dot_nt(kb[i], k[i]) for i in rng]
    qk = [_dot_nt(q[i], k[i]) for i in rng]
    m = [jnp.where(strict[i], -(kk[i] * decay[i]), 0.0) for i in rng]
    r = [eye + m[i] for i in rng]
    m = [_dot(m[i], m[i]) for i in rng]
    for _ in range(int(math.log2(C)) - 2):
        rm = [_dot(jnp.concatenate([r[i], m[i]], axis=0), m[i]) for i in rng]
        r = [r[i] + rm[i][:C] for i in rng]
        m = [rm[i][C:] for i in rng]
    r = [r[i] + _dot(r[i], m[i]) for i in rng]
    eg = [jnp.exp(gc_col[i]) for i in rng]
    wu = [_dot(r[i], jnp.concatenate([kb[i] * eg[i], v[i] * beta[i]], axis=1)) for i in rng]
    ws = [_dot(jnp.concatenate([wu[i][:, :D], q[i] * eg[i]], axis=0), state[i]) for i in rng]
    v_new = [wu[i][:, D:] - ws[i][:C] for i in rng]
    qkm = [jnp.where(incl[i], qk[i] * decay[i], 0.0) for i in rng]
    out = [ws[i][C:] + _dot(qkm[i], v_new[i]) for i in rng]
    k_dec = [k[i] * jnp.exp(gtot[i] - gc_col[i]) for i in rng]
    new_state = [state[i] * jnp.exp(gtot[i][0:1, :]) + _dot_tn(k_dec[i], v_new[i]) for i in rng]
    return out, new_state


def _gdn_scan_kernel(qf_ref, kf_ref, vf_ref, qb_ref, kb_ref, vb_ref, gf_ref, gb_ref, rf_ref, rb_ref,
                     of_ref, ob_ref, s_scr, *, n_heads, head_dim):
    H, Dh = n_heads, head_dim

    @pl.when(pl.program_id(1) == 0)
    def _():
        s_scr[...] = jnp.zeros_like(s_scr)

    gf = gf_ref[...]
    gb = gb_ref[...]
    col = lambda g, j: g[:, j:j + 1]
    sls = [slice(h * Dh, (h + 1) * Dh) for h in range(H)]
    q = [qf_ref[:, sl] for sl in sls] + [qb_ref[:, sl] for sl in sls]
    k = [kf_ref[:, sl] for sl in sls] + [kb_ref[:, sl] for sl in sls]
    v = [vf_ref[:, sl] for sl in sls] + [vb_ref[:, sl] for sl in sls]
    beta = [col(gf, h) for h in range(H)] + [col(gb, H + h) for h in range(H)]
    gc_col = [col(gf, 2 * H + h) for h in range(H)] + [col(gb, 3 * H + h) for h in range(H)]
    gc_row = [rf_ref[h:h + 1, :] for h in range(H)] + [rb_ref[H + h:H + h + 1, :] for h in range(H)]
    gtot = [col(gf, 6 * H + h) for h in range(H)] + [col(gb, 7 * H + h) for h in range(H)]
    state = [s_scr[0, h] for h in range(H)] + [s_scr[1, h] for h in range(H)]
    out, new_state = _delta_chunks(q, k, v, beta, gc_col, gc_row, gtot, state, [True] * H + [False] * H)
    for h in range(H):
        of_ref[:, sls[h]] = out[h]
        ob_ref[:, sls[h]] = out[H + h]
        s_scr[0, h] = new_state[h]
        s_scr[1, h] = new_state[H + h]


def _gdn_scan(qkv, gates, gates_row, n_heads, head_dim):
    B, L, _ = qkv.shape
    H, Dh = n_heads, head_dim
    d = H * Dh
    C = GDN_CHUNK
    N = L // C
    fwd = lambda col: pl.BlockSpec((None, C, d), lambda b, n: (b, n, col))
    bwd = lambda col: pl.BlockSpec((None, C, d), lambda b, n: (b, N - 1 - n, col))
    return pl.pallas_call(
        functools.partial(_gdn_scan_kernel, n_heads=H, head_dim=Dh),
        grid=(B, N),
        in_specs=[fwd(0), fwd(1), fwd(2), bwd(0), bwd(1), bwd(2),
                  pl.BlockSpec((None, C, LANES), lambda b, n: (b, n, 0)),
                  pl.BlockSpec((None, C, LANES), lambda b, n: (b, N - 1 - n, 0)),
                  pl.BlockSpec((None, None, 2 * H, C), lambda b, n: (b, n, 0, 0)),
                  pl.BlockSpec((None, None, 2 * H, C), lambda b, n: (b, N - 1 - n, 0, 0))],
        out_specs=[pl.BlockSpec((None, C, d), lambda b, n: (b, n, 0)),
                   pl.BlockSpec((None, C, d), lambda b, n: (b, N - 1 - n, 0))],
        out_shape=[jax.ShapeDtypeStruct((B, L, d), F32)] * 2,
        scratch_shapes=[pltpu.VMEM((2, H, Dh, Dh), F32)],
        compiler_params=_cparams(("parallel", "arbitrary"), 32),
        name="gdn_scan",
    )(qkv, qkv, qkv, qkv, qkv, qkv, gates, gates, gates_row, gates_row)


def _gdn_branch(p3, pg, conv_w, a_log_f, a_log_b, dt_bias_f, dt_bias_b, col0, n_heads, head_dim):
    B, L, _ = p3.shape
    H = n_heads
    qkv = _gdn_pre(p3, conv_w, col0, H, head_dim)
    gates = _gdn_gates(pg, a_log_f, a_log_b, dt_bias_f, dt_bias_b, H).reshape(B, L, LANES)
    N = L // GDN_CHUNK
    gates_row = gates[..., 2 * H:4 * H].reshape(B, N, GDN_CHUNK, 2 * H).transpose(0, 1, 3, 2)
    return _gdn_scan(qkv, gates, gates_row, H, head_dim)


@functools.lru_cache(maxsize=None)
def _filter_positions(L, pos_emb_dim):
    n = 2 * L
    r = np.arange(n)
    k = np.where(r < L, r, np.where(r == L, 0, n - r)).astype(np.float64)
    t = k / (L - 1)
    bands = (pos_emb_dim - 1) // 2
    fb = np.linspace(1e-4, bands - 1, bands)
    ang = (2.0 * math.pi / L) * k[:, None] * fb[None, :]
    z = np.concatenate([t[:, None], np.cos(ang), -np.sin(ang)], axis=-1)
    return z.astype(np.float32)


@functools.lru_cache(maxsize=None)
def _decay_rates(d_hy):
    max_decay = math.log(DECAY_TARGET) / FAST_DECAY_PCT
    min_decay = math.log(DECAY_TARGET) / SLOW_DECAY_PCT
    return np.abs(np.linspace(min_decay, max_decay, d_hy)).astype(np.float32).reshape(1, d_hy)


def _filt_mlp_kernel(zt_ref, w1t_ref, b1_ref, w2t_ref, b2_ref, w3t_ref, b3_ref, fr_ref, o_ref):
    fr = fr_ref[...]
    h = jnp.sin(fr * (_dot_hi(w1t_ref[...], zt_ref[...]) + b1_ref[...]))
    h = jnp.sin(fr * (_dot_hi(w2t_ref[...], h) + b2_ref[...]))
    o_ref[...] = jnp.sin(fr * (_dot_hi(w3t_ref[...], h) + b3_ref[...]))


def _filt_mlp(zt, w1, b1, w2, b2, w3, b3, freq, tc=1024):
    pe, n = zt.shape
    fw = w1.shape[1]
    col = lambda a: a.reshape(-1, 1).astype(F32)
    full = lambda a: pl.BlockSpec(a.shape, lambda i: (0, 0))
    args = (zt, w1.T, col(b1), w2.T, col(b2), w3.T, col(b3), col(freq))
    return pl.pallas_call(
        _filt_mlp_kernel,
        grid=(n // tc,),
        in_specs=[pl.BlockSpec((pe, tc), lambda i: (0, i))] + [full(a) for a in args[1:]],
        out_specs=pl.BlockSpec((fw, tc), lambda i: (0, i)),
        out_shape=jax.ShapeDtypeStruct((fw, n), F32),
        compiler_params=_cparams(("parallel",), 24),
        name="filt_mlp",
    )(*args)


@functools.lru_cache(maxsize=None)
def _dft_tables(L):
    n = 2 * L
    N2 = FFT_N2
    N1 = n // N2
    N1h = N1 // 2
    j2 = np.arange(N2)[:, None, None]
    k1 = np.arange(N1)[None, :, None]

    def stage1(n_j1):
        j1 = np.arange(n_j1)[None, None, :]
        m = (k1 * (N2 * j1 + j2)) % n
        th = 2.0 * np.pi * m / n
        return np.cos(th), np.sin(th)

    c, s = stage1(N1h)
    t1 = np.concatenate([np.concatenate([c, s], axis=2), np.concatenate([-s, c], axis=2)], axis=1)
    c, s = stage1(N1)
    t1g = np.concatenate([c, -s], axis=1)
    c, s = stage1(N1h)
    ct, st = np.swapaxes(c, 1, 2) / n, np.swapaxes(s, 1, 2) / n
    t2 = np.concatenate([np.concatenate([ct, -st], axis=2), np.concatenate([st, ct], axis=2)], axis=1)
    a = np.arange(N2)
    th = 2.0 * np.pi * ((a[:, None] * a[None, :]) % N2) / N2
    c2, s2 = np.cos(th), np.sin(th)
    f2f = np.block([[c2, s2], [-s2, c2]])
    f2i = np.block([[c2, -s2], [s2, c2]])
    as_bf16 = lambda x: jnp.asarray(x, dtype=F32).astype(BF16)
    return dict(N1=N1, N2=N2, t1=t1.astype(np.float32), t1g=t1g.astype(np.float32), t2=t2.astype(np.float32),
                f2f=f2f.astype(np.float32), f2i=f2i.astype(np.float32))


FILT_ROWS = 512
FFT_UNROLL = 8
PITCH_PAD = 8


def _filt_fft_kernel(h3_ref, w4f_ref, w4b_ref, delta_ref, t1g_ref, f2f_ref, hspec_ref, hb0_ref, g_scr, a_scr,
                     *, L, N1, N2):
    n = 2 * L
    gp = N2 + PITCH_PAD
    ap = 2 * N1 + PITCH_PAD
    delta = delta_ref[...]
    hb0_ref[...] = jnp.zeros_like(hb0_ref)

    def gen(c, carry):
        r0 = pl.multiple_of(c * FILT_ROWS, FILT_ROWS)
        row = r0 + lax.broadcasted_iota(jnp.int32, (FILT_ROWS, LANES), 0)
        lag = jnp.where(row < L, row, jnp.where(row == L, 0, n - row))
        window = jnp.exp(-(lag.astype(F32) * (1.0 / (L - 1))) * delta)
        w4 = jnp.where(r0 < L, w4f_ref[...], w4b_ref[...])
        g = _dot(h3_ref[pl.ds(r0, FILT_ROWS), :], w4) * window
        at_l = row == L
        hb0_ref[...] += jnp.sum(jnp.where(at_l, g, 0.0), axis=0, keepdims=True)
        g = jnp.where(at_l, 0.0, g)
        for q in range(FILT_ROWS // N2):
            dst = pl.multiple_of((c * (FILT_ROWS // N2) + q) * gp, SUBLANES)
            g_scr[pl.ds(dst, N2), :] = g[q * N2:(q + 1) * N2]
        return carry

    lax.fori_loop(0, n // FILT_ROWS, gen, 0)

    def stage1(j2, carry):
        x = g_scr[pl.ds(j2, N1, stride=gp), :]
        a_scr[pl.ds(pl.multiple_of(j2 * ap, SUBLANES), 2 * N1), :] = _dot(t1g_ref[j2], x)
        return carry

    lax.fori_loop(0, N2, stage1, 0, unroll=FFT_UNROLL)

    def stage2(k1, carry):
        ar = a_scr[pl.ds(k1, N2, stride=ap), :]
        ai = a_scr[pl.ds(N1 + k1, N2, stride=ap), :]
        z = _dot(f2f_ref[...], jnp.concatenate([ar, ai], axis=0))
        hspec_ref[pl.ds(pl.multiple_of(k1 * 2 * N2, 2 * N2), 2 * N2), :] = z.astype(hspec_ref.dtype)
        return carry

    lax.fori_loop(0, N1, stage2, 0, unroll=FFT_UNROLL)


def _filt_fft(h3, w4, L, d_hy):
    tb = _dft_tables(L)
    N1, N2 = tb["N1"], tb["N2"]
    n = 2 * L
    fw = h3.shape[1]
    nt = d_hy // LANES
    t1g = jnp.asarray(tb["t1g"]).astype(BF16)
    f2f = jnp.asarray(tb["f2f"]).astype(BF16)
    return pl.pallas_call(
        functools.partial(_filt_fft_kernel, L=L, N1=N1, N2=N2),
        grid=(nt,),
        in_specs=[pl.BlockSpec((n, fw), lambda c: (0, 0)),
                  pl.BlockSpec((fw, LANES), lambda c: (0, c)),
                  pl.BlockSpec((fw, LANES), lambda c: (0, c + nt)),
                  pl.BlockSpec((1, LANES), lambda c: (0, c)),
                  pl.BlockSpec(t1g.shape, lambda c: (0, 0, 0)),
                  pl.BlockSpec(f2f.shape, lambda c: (0, 0))],
        out_specs=[pl.BlockSpec((2 * n, LANES), lambda c: (0, c)),
                   pl.BlockSpec((SUBLANES, LANES), lambda c: (0, c))],
        out_shape=[jax.ShapeDtypeStruct((2 * n, d_hy), BF16), jax.ShapeDtypeStruct((SUBLANES, d_hy), F32)],
        scratch_shapes=[pltpu.VMEM((N1 * (N2 + PITCH_PAD), LANES), F32),
                        pltpu.VMEM((N2 * (2 * N1 + PITCH_PAD), LANES), F32)],
        compiler_params=_cparams(("parallel",), 48),
        name="filt_fft",
    )(h3, w4, w4, jnp.asarray(_decay_rates(d_hy)), t1g, f2f)


def _hy_conv_kernel(vg_ref, hspec_ref, skip_ref, hb0_ref, t1_ref, f2f_ref, f2i_ref, t2_ref, y_ref,
                    x_scr, a_scr, b_scr, *, N1, N2):
    N1h = N1 // 2
    xp = N2 + PITCH_PAD
    ap = 2 * N1 + PITCH_PAD
    bp = 2 * N2 + PITCH_PAD

    for b in range(2):
        for j1 in range(N1h):
            x_scr[b, pl.ds(j1 * xp, N2), :] = vg_ref[b, pl.ds(j1 * N2, N2), :]

    def stage1(j2, carry):
        x = jnp.concatenate([x_scr[0, pl.ds(j2, N1h, stride=xp), :],
                             x_scr[1, pl.ds(j2, N1h, stride=xp), :]], axis=0)
        a_scr[pl.ds(pl.multiple_of(j2 * ap, SUBLANES), 2 * N1), :] = _dot(t1_ref[j2], x)
        return carry

    lax.fori_loop(0, N2, stage1, 0, unroll=FFT_UNROLL)

    def stage2(k1, carry):
        ar = a_scr[pl.ds(k1, N2, stride=ap), :]
        ai = a_scr[pl.ds(N1 + k1, N2, stride=ap), :]
        z = _dot(f2f_ref[...], jnp.concatenate([ar, ai], axis=0))
        zr, zi = z[:N2], z[N2:]
        base = pl.multiple_of(k1 * 2 * N2, 2 * N2)
        hr = hspec_ref[pl.ds(base, N2), :].astype(F32)
        hi = hspec_ref[pl.ds(base + N2, N2), :].astype(F32)
        prod = jnp.concatenate([zr * hr - zi * hi, zr * hi + zi * hr], axis=0)
        b_scr[pl.ds(pl.multiple_of(k1 * bp, SUBLANES), 2 * N2), :] = _dot(f2i_ref[...], prod)
        return carry

    lax.fori_loop(0, N1, stage2, 0, unroll=FFT_UNROLL)

    skip = skip_ref[...] + hb0_ref[0:1, :]

    def stage3(j2, carry):
        b = jnp.concatenate([b_scr[pl.ds(j2, N1, stride=bp), :],
                             b_scr[pl.ds(N2 + j2, N1, stride=bp), :]], axis=0)
        y = _dot(t2_ref[j2], b)
        x_scr[0, pl.ds(j2, N1h, stride=xp), :] = y[:N1h] + x_scr[0, pl.ds(j2, N1h, stride=xp), :] * skip
        x_scr[1, pl.ds(j2, N1h, stride=xp), :] = y[N1h:] + x_scr[1, pl.ds(j2, N1h, stride=xp), :] * skip
        return carry

    lax.fori_loop(0, N2, stage3, 0, unroll=FFT_UNROLL)

    for b in range(2):
        for j1 in range(N1h):
            y_ref[b, pl.ds(j1 * N2, N2), :] = x_scr[b, pl.ds(j1 * xp, N2), :]


def _hy_conv(vg, hspec, skip, hb0):
    B, L, d_hy = vg.shape
    assert B % 2 == 0
    tb = _dft_tables(L)
    N1, N2 = tb["N1"], tb["N2"]
    n = 2 * L
    nt = d_hy // LANES
    t1, t2 = (jnp.asarray(tb[k]).astype(BF16) for k in ("t1", "t2"))
    f2f, f2i = (jnp.asarray(tb[k]).astype(BF16) for k in ("f2f", "f2i"))
    const3 = lambda a: pl.BlockSpec(a.shape, lambda b, c: (0, 0, 0))
    const2 = lambda a: pl.BlockSpec(a.shape, lambda b, c: (0, 0))
    return pl.pallas_call(
        functools.partial(_hy_conv_kernel, N1=N1, N2=N2),
        grid=(B // 2, nt),
        in_specs=[pl.BlockSpec((2, L, LANES), lambda b, c: (b, 0, c)),
                  pl.BlockSpec((2 * n, LANES), lambda b, c: (0, c)),
                  pl.BlockSpec((1, LANES), lambda b, c: (0, c)),
                  pl.BlockSpec((SUBLANES, LANES), lambda b, c: (0, c)),
                  const3(t1), const2(f2f), const2(f2i), const3(t2)],
        out_specs=pl.BlockSpec((2, L, LANES), lambda b, c: (b, 0, c)),
        out_shape=jax.ShapeDtypeStruct((B, L, d_hy), F32),
        scratch_shapes=[pltpu.VMEM((2, (N1 // 2) * (N2 + PITCH_PAD), LANES), F32),
                        pltpu.VMEM((N2 * (2 * N1 + PITCH_PAD), LANES), F32),
                        pltpu.VMEM((N1 * (2 * N2 + PITCH_PAD), LANES), F32)],
        compiler_params=_cparams(("parallel", "parallel"), 58),
        name="hy_conv",
    )(vg, hspec, skip.reshape(1, d_hy).astype(F32), hb0, t1, f2f, f2i, t2)


def _hyena_branch(p3, conv_w, conv_b, fw1, fb1, fw2, fb2, fw3, fb3, fw4, freq, skip, d_hy):
    B, L, _ = p3.shape
    x0c, vg = _hy_pre(p3, conv_w, conv_b, d_hy)
    zt = jnp.asarray(_filter_positions(L, fw1.shape[0]).T)
    h3 = _filt_mlp(zt, fw1, fb1, fw2, fb2, fw3, fb3, freq).T
    hspec, hb0 = _filt_fft(h3, fw4, L, d_hy)
    return x0c, _hy_conv(vg, hspec, skip, hb0)


ROUTE_GATE, ROUTE_EXPERT, ROUTE_RANK = 0, 2, 4
MIX_ROWS = 256


def _mix_route_kernel(yc_ref, x0_ref, of_ref, ob_ref, z_ref, hnw_ref, gnw_ref, x_ref, wo_ref, n2w_ref, wr_ref, br_ref,
                      x2_ref, u_ref, route_ref, route_t_ref, cnt_ref, run_scr, *, n_heads, head_dim, n_groups,
                      per_group):
    tm = x_ref.shape[0]
    G, P = n_groups, per_group

    @pl.when(pl.program_id(0) == 0)
    def _():
        run_scr[...] = jnp.zeros_like(run_scr)

    run = run_scr[0:1, :]
    blocks = [pl.ds(s * MIX_ROWS, MIX_ROWS) for s in range(tm // MIX_ROWS)]
    us = []
    for rs in blocks:
        yh = yc_ref[rs, :] * x0_ref[rs, :]
        yh = yh * lax.rsqrt(jnp.mean(yh * yh, axis=-1, keepdims=True) + EPS) * hnw_ref[...]
        parts = [yh.astype(BF16)]
        for h in range(n_heads):
            sl = slice(h * head_dim, (h + 1) * head_dim)
            o = of_ref[rs, sl] + ob_ref[rs, sl]
            o = o * lax.rsqrt(jnp.mean(o * o, axis=-1, keepdims=True) + EPS) * gnw_ref[...] * _silu(z_ref[rs, sl])
            parts.append(o.astype(BF16))
        ymix = jnp.concatenate(parts, axis=-1)
        x2 = x_ref[rs, :] + jnp.dot(ymix, wo_ref[...], preferred_element_type=F32)
        x2_ref[rs, :] = x2
        u = x2 * lax.rsqrt(jnp.mean(x2 * x2, axis=-1, keepdims=True) + EPS) * n2w_ref[...]
        u_ref[rs, :] = u
        us.append(u)

    all_logits = [_dot(u, wr_ref[...]) + br_ref[...] for u in us]
    for s, (rs, logits) in enumerate(zip(blocks, all_logits)):
        lane = lax.broadcasted_iota(jnp.int32, logits.shape, 1)
        neg = jnp.float32(-jnp.inf)
        big = jnp.int32(4 * LANES)
        first = lambda hit: jnp.min(jnp.where(hit, lane, big), axis=-1, keepdims=True)
        gl = jnp.where(lane < G, logits, neg)
        gmax = jnp.max(gl, axis=-1, keepdims=True)
        gidx = first(gl == gmax)
        grp_gate = 1.0 / jnp.sum(jnp.exp(gl - gmax), axis=-1, keepdims=True)
        in_grp = (lane >= G) & (lane < G + G * P) & (((lane - G) // P) == gidx)
        ll = jnp.where(in_grp, logits, neg)
        m1 = jnp.max(ll, axis=-1, keepdims=True)
        i1 = first(ll == m1)
        denom = jnp.sum(jnp.exp(ll - m1), axis=-1, keepdims=True)
        ll2 = jnp.where(lane == i1, neg, ll)
        m2 = jnp.max(ll2, axis=-1, keepdims=True)
        i2 = first(ll2 == m2)
        p1 = 1.0 / denom
        p2 = jnp.exp(m2 - m1) / denom
        gate1 = grp_gate * (p1 / (p1 + p2))
        gate2 = grp_gate * (p2 / (p1 + p2))
        e1 = i1 - G
        e2 = i2 - G

        oh1 = jnp.where(lane == e1, 1.0, 0.0)
        oh2 = jnp.where(lane == e2, 1.0, 0.0)
        oh = oh1 + oh2
        ii = lax.broadcasted_iota(jnp.int32, (MIX_ROWS, MIX_ROWS), 0)
        jj = lax.broadcasted_iota(jnp.int32, (MIX_ROWS, MIX_ROWS), 1)
        before = _dot(jnp.where(ii > jj, 1.0, 0.0), oh) + run
        r1 = jnp.sum(oh1 * before, axis=-1, keepdims=True)
        r2 = jnp.sum(oh2 * before, axis=-1, keepdims=True)
        run = run + jnp.sum(oh, axis=0, keepdims=True)

        rec = jnp.where(lane == ROUTE_GATE, gate1, 0.0)
        rec = jnp.where(lane == ROUTE_GATE + 1, gate2, rec)
        rec = jnp.where(lane == ROUTE_EXPERT, e1.astype(F32), rec)
        rec = jnp.where(lane == ROUTE_EXPERT + 1, e2.astype(F32), rec)
        rec = jnp.where(lane == ROUTE_RANK, r1, rec)
        rec = jnp.where(lane == ROUTE_RANK + 1, r2, rec)
        route_ref[rs, :] = rec
        route_t_ref[:, s * MIX_ROWS:(s + 1) * MIX_ROWS] = jnp.transpose(rec)[:SUBLANES, :]

    run_scr[...] = jnp.broadcast_to(run, run_scr.shape)
    cnt_ref[...] = run_scr[...]


def _mix_route(yconv, x0c, o_f, o_b, p, z_col, hy_norm_w, gdn_norm_w, xf, w_out_bf16, norm2_w, wr, br,
               n_heads, head_dim, n_groups, per_group, tm=512):
    M, D = xf.shape
    d_hy = yconv.shape[1]
    d_gdn = o_f.shape[1]
    assert z_col % d_gdn == 0 and n_groups * (per_group + 1) <= LANES
    zb = z_col // d_gdn
    row = lambda i: (i, 0)
    const = lambda i: (0, 0)
    kern = functools.partial(_mix_route_kernel, n_heads=n_heads, head_dim=head_dim,
                             n_groups=n_groups, per_group=per_group)
    return pl.pallas_call(
        kern,
        grid=(M // tm,),
        in_specs=[pl.BlockSpec((tm, d_hy), row), pl.BlockSpec((tm, d_hy), row),
                  pl.BlockSpec((tm, d_gdn), row), pl.BlockSpec((tm, d_gdn), row),
                  pl.BlockSpec((tm, d_gdn), lambda i: (i, zb)),
                  pl.BlockSpec((1, d_hy), const), pl.BlockSpec((1, head_dim), const),
                  pl.BlockSpec((tm, D), row), pl.BlockSpec(w_out_bf16.shape, const, pipeline_mode=pl.Buffered(1)),
                  pl.BlockSpec((1, D), const), pl.BlockSpec((D, LANES), const), pl.BlockSpec((1, LANES), const)],
        out_specs=[pl.BlockSpec((tm, D), row), pl.BlockSpec((tm, D), row),
                   pl.BlockSpec((tm, LANES), row), pl.BlockSpec((SUBLANES, tm), lambda i: (0, i)),
                   pl.BlockSpec((SUBLANES, LANES), const)],
        out_shape=[jax.ShapeDtypeStruct((M, D), F32), jax.ShapeDtypeStruct((M, D), F32),
                   jax.ShapeDtypeStruct((M, LANES), F32), jax.ShapeDtypeStruct((SUBLANES, M), F32),
                   jax.ShapeDtypeStruct((SUBLANES, LANES), F32)],
        scratch_shapes=[pltpu.VMEM((SUBLANES, LANES), F32)],
        compiler_params=_cparams(("arbitrary",), 58),
        name="mix_route",
    )(yconv, x0c, o_f, o_b, p, hy_norm_w.reshape(1, d_hy), gdn_norm_w.reshape(1, head_dim), xf, w_out_bf16,
      norm2_w.reshape(1, D), wr, br)


DMA_GROUP = 8


def _experts_kernel(te_ref, nv_ref, na_ref, tgt_nxt_ref, tgt_ref, u_hbm, w1_ref, w3_ref, w2_ref, out_hbm,
                    xbuf, ybuf, w1b, w3b, w2b, gsem, ssem, *, n_tokens):
    i = pl.program_id(0)
    na = na_ref[0]
    slot = lax.rem(i, 2)
    other = 1 - slot

    def rows(count, one, group):
        n_groups = count // DMA_GROUP

        def grp(g, carry):
            r0 = pl.multiple_of(g * DMA_GROUP, DMA_GROUP)
            if group is None:
                for j in range(DMA_GROUP):
                    one(r0 + j)
            else:
                group(r0)
            return carry

        lax.fori_loop(0, n_groups, grp, 0)

        def tail(r, carry):
            one(r)
            return carry

        lax.fori_loop(n_groups * DMA_GROUP, count, tail, 0)

    def gather(tgt, s, r):
        t = tgt[r]
        tok = jnp.where(t >= n_tokens, t - n_tokens, t)
        return pltpu.make_async_copy(u_hbm.at[pl.ds(tok, 1)], xbuf.at[s, pl.ds(r, 1)], gsem.at[s])

    def gather_wait(s, n):
        return pltpu.make_async_copy(u_hbm.at[pl.ds(0, n)], xbuf.at[s, pl.ds(0, n)], gsem.at[s])

    def scatter(s, r):
        return pltpu.make_async_copy(ybuf.at[s, pl.ds(r, 1)], out_hbm.at[pl.ds(tgt_ref[r], 1)], ssem.at[s])

    def scatter_wait(s, n):
        return pltpu.make_async_copy(ybuf.at[s, pl.ds(0, n)], out_hbm.at[pl.ds(0, n)], ssem.at[s])

    def wait_rows(count, mk):
        rows(count, lambda r: mk(1).wait(), lambda r0: mk(DMA_GROUP).wait())

    @pl.when(i < na)
    def _():
        @pl.when(i == 0)
        def _():
            xbuf[...] = jnp.zeros_like(xbuf)
            rows(nv_ref[0], lambda r: gather(tgt_ref, 0, r).start(), None)

        @pl.when(i + 1 < na)
        def _():
            rows(nv_ref[i + 1], lambda r: gather(tgt_nxt_ref, other, r).start(), None)

        @pl.when((i == 0) | (te_ref[i] != te_ref[jnp.maximum(i - 1, 0)]))
        def _():
            w1b[...] = w1_ref[...].astype(BF16)
            w3b[...] = w3_ref[...].astype(BF16)
            w2b[...] = w2_ref[...].astype(BF16)

        wait_rows(nv_ref[i], lambda n: gather_wait(slot, n))
        x = xbuf[slot].astype(BF16)
        h = _silu(jnp.dot(x, w1b[...], preferred_element_type=F32)) * jnp.dot(x, w3b[...], preferred_element_type=F32)
        ybuf[slot] = jnp.dot(h.astype(BF16), w2b[...], preferred_element_type=F32)
        rows(nv_ref[i], lambda r: scatter(slot, r).start(), None)

        @pl.when(i > 0)
        def _():
            wait_rows(nv_ref[jnp.maximum(i - 1, 0)], lambda n: scatter_wait(other, n))

        @pl.when(i == na - 1)
        def _():
            wait_rows(nv_ref[i], lambda n: scatter_wait(slot, n))


def _experts(u, slot_tgt, tile_expert, tile_valid, n_active, w1, w3, w2):
    T, D = u.shape
    E, _, de = w1.shape
    TB = EXPERT_ROWS
    n_tiles = slot_tgt.shape[0] // TB
    wmap = lambda i, te, nv, na: (te[i], 0, 0)
    grid_spec = pltpu.PrefetchScalarGridSpec(
        num_scalar_prefetch=3,
        grid=(n_tiles,),
        in_specs=[pl.BlockSpec((TB,), lambda i, te, nv, na: (jnp.minimum(i + 1, n_tiles - 1),),
                               memory_space=pltpu.SMEM),
                  pl.BlockSpec((TB,), lambda i, te, nv, na: (i,), memory_space=pltpu.SMEM),
                  pl.BlockSpec(memory_space=pl.ANY),
                  pl.BlockSpec((None, D, de), wmap), pl.BlockSpec((None, D, de), wmap),
                  pl.BlockSpec((None, de, D), wmap)],
        out_specs=pl.BlockSpec(memory_space=pl.ANY),
        scratch_shapes=[pltpu.VMEM((2, TB, D), F32), pltpu.VMEM((2, TB, D), F32),
                        pltpu.VMEM((D, de), BF16), pltpu.VMEM((D, de), BF16), pltpu.VMEM((de, D), BF16),
                        pltpu.SemaphoreType.DMA((2,)), pltpu.SemaphoreType.DMA((2,))],
    )
    return pl.pallas_call(
        functools.partial(_experts_kernel, n_tokens=T),
        grid_spec=grid_spec,
        out_shape=jax.ShapeDtypeStruct((2 * T, D), F32),
        compiler_params=_cparams(("arbitrary",), 52),
        name="experts",
    )(tile_expert, tile_valid, n_active, slot_tgt, slot_tgt, u, w1, w3, w2)


def _slot_table_kernel(dest_ref, zero_hbm, o_ref, sem, *, n_tokens):
    fill = pltpu.make_async_copy(zero_hbm, o_ref, sem)
    fill.start()
    fill.wait()

    def put(a, carry):
        o_ref[dest_ref[a]] = a
        return carry

    lax.fori_loop(0, dest_ref.shape[0], put, 0, unroll=8)


def _slot_table(dest, n_slots, n_tokens):
    smem = pl.BlockSpec(memory_space=pltpu.SMEM)
    return pl.pallas_call(
        functools.partial(_slot_table_kernel, n_tokens=n_tokens),
        in_specs=[smem, pl.BlockSpec(memory_space=pl.ANY)],
        out_specs=smem,
        out_shape=jax.ShapeDtypeStruct((n_slots,), jnp.int32),
        scratch_shapes=[pltpu.SemaphoreType.DMA],
        name="slot_table",
    )(dest, jnp.zeros((n_slots,), jnp.int32))


def _dispatch_tables(route_t, counts, n_experts):
    T = route_t.shape[1]
    TB = EXPERT_ROWS
    e = route_t[ROUTE_EXPERT:ROUTE_EXPERT + 2].astype(jnp.int32)
    rank = route_t[ROUTE_RANK:ROUTE_RANK + 2].astype(jnp.int32)
    cnt = counts[0, :n_experts].astype(jnp.int32)
    padded = (cnt + TB - 1) // TB * TB
    pad_end = jnp.cumsum(padded)
    pad_start = pad_end - padded
    ids = jnp.arange(n_experts, dtype=jnp.int32)[:, None, None]
    start_of = jnp.sum(jnp.where(e[None] == ids, pad_start[:, None, None], 0), axis=0)
    dest = start_of + rank
    n_tiles = -(-(2 * T + n_experts * (TB - 1)) // TB)
    slot_tgt = _slot_table(dest.reshape(-1), n_tiles * TB, T)
    start = jnp.arange(n_tiles, dtype=jnp.int32) * TB
    tile_expert = jnp.minimum(jnp.searchsorted(pad_end, start, side="right"), n_experts - 1).astype(jnp.int32)
    tile_valid = jnp.clip(cnt[tile_expert] - (start - pad_start[tile_expert]), 0, TB)
    tile_valid = jnp.where(start < pad_end[-1], tile_valid, 0).astype(jnp.int32)
    n_active = (pad_end[-1] // TB).astype(jnp.int32).reshape(1)
    return slot_tgt, tile_expert, tile_valid, n_active


def _combine_kernel(x2_ref, e0_ref, e1_ref, route_ref, w_ref, o_ref, *, final_norm):
    r = route_ref[...]
    y = x2_ref[...] + r[:, ROUTE_GATE:ROUTE_GATE + 1] * e0_ref[...] + r[:, ROUTE_GATE + 1:ROUTE_GATE + 2] * e1_ref[...]
    if final_norm:
        y = y * lax.rsqrt(jnp.mean(y * y, axis=-1, keepdims=True) + EPS) * w_ref[...]
    o_ref[...] = y


def _combine(x2, planes, route, norm_w, final_norm, tm=512):
    M, D = x2.shape
    return pl.pallas_call(
        functools.partial(_combine_kernel, final_norm=final_norm),
        grid=(M // tm,),
        in_specs=[pl.BlockSpec((tm, D), lambda i: (i, 0)),
                  pl.BlockSpec((None, tm, D), lambda i: (0, i, 0)),
                  pl.BlockSpec((None, tm, D), lambda i: (1, i, 0)),
                  pl.BlockSpec((tm, LANES), lambda i: (i, 0)),
                  pl.BlockSpec((1, D), lambda i: (0, 0))],
        out_specs=pl.BlockSpec((tm, D), lambda i: (i, 0)),
        out_shape=jax.ShapeDtypeStruct((M, D), F32),
        compiler_params=_cparams(("parallel",), 48),
        name="combine",
    )(x2, planes, planes, route, norm_w.reshape(1, D))


def kernel(x, norm1_w, w_in, hy_conv_w, hy_conv_b, hy_filt_w1, hy_filt_b1, hy_filt_w2, hy_filt_b2, hy_filt_w3, hy_filt_b3, hy_filt_w4, hy_sin_freq, hy_skip, hy_norm_w, gdn_conv_w, gdn_a_log_f, gdn_a_log_b, gdn_dt_bias_f, gdn_dt_bias_b, gdn_norm_w, w_out, norm2_w, router_group_w, router_group_b, router_expert_w, router_expert_b, exp_w1, exp_w3, exp_w2, final_norm_w):
    B, L, D = x.shape
    M = B * L
    depth = w_in.shape[0]
    d_hy = hy_skip.shape[-1]
    H = gdn_a_log_f.shape[-1]
    Dh = gdn_norm_w.shape[-1]
    d_gdn = H * Dh
    n_main = 3 * d_hy + 4 * d_gdn
    G = router_group_w.shape[-1]
    E = router_expert_w.shape[-1]
    xf = x.reshape(M, D)
    for l in range(depth):
        p, pg = _inproj(xf, norm1_w[l], jnp.swapaxes(w_in[l], 0, 1), n_main, 4 * H)
        p3 = p.reshape(B, L, n_main)
        x0c, yconv = _hyena_branch(p3, hy_conv_w[l], hy_conv_b[l], hy_filt_w1[l], hy_filt_b1[l], hy_filt_w2[l],
                                   hy_filt_b2[l], hy_filt_w3[l], hy_filt_b3[l], hy_filt_w4[l], hy_sin_freq[l],
                                   hy_skip[l], d_hy)
        o_f, o_b = _gdn_branch(p3, pg, gdn_conv_w[l], gdn_a_log_f[l], gdn_a_log_b[l], gdn_dt_bias_f[l],
                               gdn_dt_bias_b[l], 3 * d_hy, H, Dh)
        wr = jnp.pad(jnp.concatenate([router_group_w[l], router_expert_w[l]], axis=1), ((0, 0), (0, LANES - G - E)))
        br = jnp.pad(jnp.concatenate([router_group_b[l], router_expert_b[l]]), (0, LANES - G - E)).reshape(1, LANES)
        x2, u, route, route_t, counts = _mix_route(
            yconv.reshape(M, d_hy), x0c.reshape(M, d_hy), o_f.reshape(M, d_gdn), o_b.reshape(M, d_gdn), p,
            3 * d_hy + 3 * d_gdn, hy_norm_w[l], gdn_norm_w[l], xf, w_out[l].astype(BF16), norm2_w[l], wr, br,
            H, Dh, G, E // G)
        slot_tgt, tile_expert, tile_valid, n_active = _dispatch_tables(route_t, counts, E)
        planes = _experts(u, slot_tgt, tile_expert, tile_valid, n_active, exp_w1[l], exp_w3[l], exp_w2[l])
        xf = _combine(x2, planes.reshape(2, M, D), route, final_norm_w, final_norm=(l == depth - 1))
    return xf.reshape(B, L, D)
```

```python
import functools
import math

import jax
import jax.numpy as jnp
import numpy as np
from jax import lax
from jax.experimental import pallas as pl
from jax.experimental.pallas import tpu as pltpu

F32 = jnp.float32
BF16 = jnp.bfloat16
EPS = 1e-6
LANES = 128
SUBLANES = 8
VMEM_BYTES_V7X = 64 * 1024 * 1024
GDN_CHUNK = 64
FFT_N2 = 128
EXPERT_ROWS = 256
DECAY_TARGET = 1e-2
FAST_DECAY_PCT = 0.3
SLOW_DECAY_PCT = 1.5


def _cparams(sem, vmem_mb):
    return pltpu.CompilerParams(dimension_semantics=sem, vmem_limit_bytes=int(vmem_mb * 1024 * 1024))


def _dot(a, b):
    return jnp.dot(a.astype(BF16), b.astype(BF16), preferred_element_type=F32)


def _dot_nt(a, b):
    return lax.dot_general(a.astype(BF16), b.astype(BF16), (((1,), (1,)), ((), ())), preferred_element_type=F32)


def _dot_tn(a, b):
    return lax.dot_general(a.astype(BF16), b.astype(BF16), (((0,), (0,)), ((), ())), preferred_element_type=F32)


def _dot_hi(a, b):
    return jnp.dot(a, b, preferred_element_type=F32, precision=lax.Precision.HIGHEST)


def _silu(x):
    return x * jax.nn.sigmoid(x)


def _inproj_kernel(x_ref, nw_ref, wt_ref, wgt_ref, p_ref, g_ref, h_scr):
    @pl.when(pl.program_id(1) == 0)
    def _():
        x = x_ref[...]
        h = x * lax.rsqrt(jnp.mean(x * x, axis=-1, keepdims=True) + EPS) * nw_ref[...]
        h_scr[...] = h.astype(BF16)
        g = _dot_nt(h, wgt_ref[...])
        g_ref[...] = jnp.concatenate([g, jnp.zeros((g.shape[0], LANES - g.shape[1]), F32)], axis=1)

    p_ref[...] = _dot_nt(h_scr[...], wt_ref[...])


def _inproj(xf, norm_w, wt, n_main, n_gate, tm=2048, tn=512):
    M, D = xf.shape
    assert n_main % tn == 0 and M % tm == 0 and n_main % n_gate == 0 and n_gate % SUBLANES == 0
    return pl.pallas_call(
        _inproj_kernel,
        grid=(M // tm, n_main // tn),
        in_specs=[
            pl.BlockSpec((tm, D), lambda i, j: (i, 0), pipeline_mode=pl.Buffered(1)),
            pl.BlockSpec((1, D), lambda i, j: (0, 0)),
            pl.BlockSpec((tn, D), lambda i, j: (j, 0)),
            pl.BlockSpec((n_gate, D), lambda i, j: (n_main // n_gate, 0)),
        ],
        out_specs=[
            pl.BlockSpec((tm, tn), lambda i, j: (i, j)),
            pl.BlockSpec((tm, LANES), lambda i, j: (i, 0)),
        ],
        out_shape=[jax.ShapeDtypeStruct((M, n_main), F32), jax.ShapeDtypeStruct((M, LANES), F32)],
        scratch_shapes=[pltpu.VMEM((tm, D), BF16)],
        compiler_params=_cparams(("parallel", "arbitrary"), 57),
        name="inproj",
    )(xf, norm_w.reshape(1, D), wt, wt)


def _conv3_rows(ref, r0, rows, w, n_rows):
    cur = ref[pl.ds(r0, rows), :]
    lo = jnp.maximum(r0 - SUBLANES, 0)
    hi = jnp.minimum(r0 + rows, n_rows - SUBLANES)
    prev8 = ref[pl.ds(pl.multiple_of(lo, SUBLANES), SUBLANES), :]
    next8 = ref[pl.ds(pl.multiple_of(hi, SUBLANES), SUBLANES), :]
    prev_row = jnp.where(r0 > 0, prev8[SUBLANES - 1:SUBLANES, :], 0.0)
    next_row = jnp.where(r0 + rows < n_rows, next8[0:1, :], 0.0)
    row = lax.broadcasted_iota(jnp.int32, cur.shape, 0)
    xm = jnp.where(row == 0, prev_row, pltpu.roll(cur, 1, 0))
    xp = jnp.where(row == rows - 1, next_row, pltpu.roll(cur, rows - 1, 0))
    return xm * w[0:1, :] + cur * w[1:2, :] + xp * w[2:3, :]


CONV_ROWS = 256


def _hy_pre_kernel(x0_ref, x1_ref, v_ref, w0_ref, w1_ref, w2_ref, b0_ref, b1_ref, b2_ref, x0c_ref, vg_ref):
    L = x0_ref.shape[0]
    w0, w1, w2 = w0_ref[...], w1_ref[...], w2_ref[...]
    b0, b1, b2 = b0_ref[...], b1_ref[...], b2_ref[...]

    def body(c, carry):
        r0 = pl.multiple_of(c * CONV_ROWS, CONV_ROWS)
        x0c_ref[pl.ds(r0, CONV_ROWS), :] = _conv3_rows(x0_ref, r0, CONV_ROWS, w0, L) + b0
        x1c = _conv3_rows(x1_ref, r0, CONV_ROWS, w1, L) + b1
        vc = _conv3_rows(v_ref, r0, CONV_ROWS, w2, L) + b2
        vg_ref[pl.ds(r0, CONV_ROWS), :] = vc * x1c
        return carry

    lax.fori_loop(0, L // CONV_ROWS, body, 0, unroll=2)


def _hy_pre(p3, conv_w, conv_b, d_hy):
    B, L, _ = p3.shape
    nt = d_hy // LANES
    bias = conv_b.reshape(1, -1)
    pspec = lambda off: pl.BlockSpec((None, L, LANES), lambda b, c: (b, 0, c + off))
    wspec = lambda off: pl.BlockSpec((3, LANES), lambda b, c: (0, c + off))
    bspec = lambda off: pl.BlockSpec((1, LANES), lambda b, c: (0, c + off))
    ospec = pl.BlockSpec((None, L, LANES), lambda b, c: (b, 0, c))
    return pl.pallas_call(
        _hy_pre_kernel,
        grid=(B, nt),
        in_specs=[pspec(0), pspec(nt), pspec(2 * nt), wspec(0), wspec(nt), wspec(2 * nt),
                  bspec(0), bspec(nt), bspec(2 * nt)],
        out_specs=[ospec, ospec],
        out_shape=[jax.ShapeDtypeStruct((B, L, d_hy), F32)] * 2,
        compiler_params=_cparams(("parallel", "parallel"), 40),
        name="hy_pre",
    )(p3, p3, p3, conv_w, conv_w, conv_w, bias, bias, bias)


def _gdn_pre_kernel(x_ref, w_ref, o_ref, *, n_heads, head_dim):
    L = x_ref.shape[0]
    w = w_ref[...]
    c = pl.program_id(1)
    q_scale = jnp.where(c < n_heads, head_dim ** -0.5, 1.0)
    is_qk = c < 2 * n_heads

    def body(i, carry):
        r0 = pl.multiple_of(i * CONV_ROWS, CONV_ROWS)
        y = _silu(_conv3_rows(x_ref, r0, CONV_ROWS, w, L))
        inv = lax.rsqrt(jnp.sum(y * y, axis=-1, keepdims=True) + EPS) * q_scale
        o_ref[pl.ds(r0, CONV_ROWS), :] = y * jnp.where(is_qk, inv, 1.0)
        return carry

    lax.fori_loop(0, L // CONV_ROWS, body, 0, unroll=2)


def _gdn_pre(p3, conv_w, col0, n_heads, head_dim):
    B, L, _ = p3.shape
    assert head_dim == LANES
    nt = 3 * n_heads
    off = col0 // LANES
    return pl.pallas_call(
        functools.partial(_gdn_pre_kernel, n_heads=n_heads, head_dim=head_dim),
        grid=(B, nt),
        in_specs=[pl.BlockSpec((None, L, LANES), lambda b, c: (b, 0, c + off)),
                  pl.BlockSpec((3, LANES), lambda b, c: (0, c))],
        out_specs=pl.BlockSpec((None, L, LANES), lambda b, c: (b, 0, c)),
        out_shape=jax.ShapeDtypeStruct((B, L, nt * LANES), F32),
        compiler_params=_cparams(("parallel", "parallel"), 24),
        name="gdn_pre",
    )(p3, conv_w)


GATE_ROWS = 512


def _gdn_gates_kernel(pg_ref, alog_ref, dtb_ref, o_ref, *, n_heads):
    H = n_heads
    x = pg_ref[...]
    beta = jax.nn.sigmoid(x)
    z = x + dtb_ref[...]
    softplus = jnp.maximum(z, 0.0) + jnp.log1p(jnp.exp(-jnp.abs(z)))
    g = -jnp.exp(alog_ref[...]) * softplus
    pos = lax.broadcasted_iota(jnp.int32, x.shape, 0) & (GDN_CHUNK - 1)
    gc_f = g
    gc_b = g
    step = 1
    while step < GDN_CHUNK:
        gc_f = gc_f + jnp.where(pos >= step, pltpu.roll(gc_f, step, 0), 0.0)
        gc_b = gc_b + jnp.where(pos < GDN_CHUNK - step, pltpu.roll(gc_b, GATE_ROWS - step, 0), 0.0)
        step *= 2
    g_tot = pltpu.roll(gc_f + gc_b - g, 4 * H, 1)
    lane = lax.broadcasted_iota(jnp.int32, x.shape, 1)
    out = jnp.where(lane < 2 * H, beta,
                    jnp.where(lane < 3 * H, gc_f,
                              jnp.where(lane < 4 * H, gc_b,
                                        jnp.where((lane >= 6 * H) & (lane < 8 * H), g_tot, 0.0))))
    o_ref[...] = out


def _gdn_gates(pg, a_log_f, a_log_b, dt_bias_f, dt_bias_b, n_heads):
    M = pg.shape[0]
    H = n_heads
    assert 8 * H <= LANES
    pad = lambda a, b: jnp.concatenate([jnp.zeros((2 * H,), F32), a.astype(F32), b.astype(F32),
                                        jnp.zeros((LANES - 4 * H,), F32)]).reshape(1, LANES)
    return pl.pallas_call(
        functools.partial(_gdn_gates_kernel, n_heads=H),
        grid=(M // GATE_ROWS,),
        in_specs=[pl.BlockSpec((GATE_ROWS, LANES), lambda i: (i, 0)),
                  pl.BlockSpec((1, LANES), lambda i: (0, 0)),
                  pl.BlockSpec((1, LANES), lambda i: (0, 0))],
        out_specs=pl.BlockSpec((GATE_ROWS, LANES), lambda i: (i, 0)),
        out_shape=jax.ShapeDtypeStruct((M, LANES), F32),
        compiler_params=_cparams(("parallel",), 24),
        name="gdn_gates",
    )(pg, pad(a_log_f, a_log_b), pad(dt_bias_f, dt_bias_b))


def _delta_chunks(q, k, v, beta, gc_col, gc_row, gtot, state, lower):
    n = len(q)
    C = q[0].shape[0]
    D = k[0].shape[1]
    ii = lax.broadcasted_iota(jnp.int32, (C, C), 0)
    jj = lax.broadcasted_iota(jnp.int32, (C, C), 1)
    eye = jnp.where(ii == jj, 1.0, 0.0)
    incl = [(ii >= jj) if lo else (ii <= jj) for lo in lower]
    strict = [(ii > jj) if lo else (ii < jj) for lo in lower]
    rng = range(n)
    decay = [jnp.where(incl[i], jnp.exp(jnp.where(incl[i], gc_col[i] - gc_row[i], 0.0)), 0.0) for i in rng]
    kb = [k[i] * beta[i] for i in rng]
    kk = [_dot_nt(kb[i], k[i]) for i in rng]
    qk = [_dot_nt(q[i], k[i]) for i in rng]
    m = [jnp.where(strict[i], -(kk[i] * decay[i]), 0.0) for i in rng]
    r = [eye + m[i] for i in rng]
    m = [_dot(m[i], m[i]) for i in rng]
    for _ in range(int(math.log2(C)) - 2):
        rm = [_dot(jnp.concatenate([r[i], m[i]], axis=0), m[i]) for i in rng]
        r = [r[i] + rm[i][:C] for i in rng]
        m = [rm[i][C:] for i in rng]
    r = [r[i] + _dot(r[i], m[i]) for i in rng]
    eg = [jnp.exp(gc_col[i]) for i in rng]
    wu = [_dot(r[i], jnp.concatenate([kb[i] * eg[i], v[i] * beta[i]], axis=1)) for i in rng]
    ws = [_dot(jnp.concatenate([wu[i][:, :D], q[i] * eg[i]], axis=0), state[i]) for i in rng]
    v_new = [wu[i][:, D:] - ws[i][:C] for i in rng]
    qkm = [jnp.where(incl[i], qk[i] * decay[i], 0.0) for i in rng]
    out = [ws[i][C:] + _dot(qkm[i], v_new[i]) for i in rng]
    k_dec = [k[i] * jnp.exp(gtot[i] - gc_col[i]) for i in rng]
    new_state = [state[i] * jnp.exp(gtot[i][0:1, :]) + _dot_tn(k_dec[i], v_new[i]) for i in rng]
    return out, new_state


def _gdn_scan_kernel(qf_ref, kf_ref, vf_ref, qb_ref, kb_ref, vb_ref, gf_ref, gb_ref, rf_ref, rb_ref,
                     of_ref, ob_ref, s_scr, *, n_heads, head_dim):
    H, Dh = n_heads, head_dim

    @pl.when(pl.program_id(1) == 0)
    def _():
        s_scr[...] = jnp.zeros_like(s_scr)

    gf = gf_ref[...]
    gb = gb_ref[...]
    col = lambda g, j: g[:, j:j + 1]
    sls = [slice(h * Dh, (h + 1) * Dh) for h in range(H)]
    q = [qf_ref[:, sl] for sl in sls] + [qb_ref[:, sl] for sl in sls]
    k = [kf_ref[:, sl] for sl in sls] + [kb_ref[:, sl] for sl in sls]
    v = [vf_ref[:, sl] for sl in sls] + [vb_ref[:, sl] for sl in sls]
    beta = [col(gf, h) for h in range(H)] + [col(gb, H + h) for h in range(H)]
    gc_col = [col(gf, 2 * H + h) for h in range(H)] + [col(gb, 3 * H + h) for h in range(H)]
    gc_row = [rf_ref[h:h + 1, :] for h in range(H)] + [rb_ref[H + h:H + h + 1, :] for h in range(H)]
    gtot = [col(gf, 6 * H + h) for h in range(H)] + [col(gb, 7 * H + h) for h in range(H)]
    state = [s_scr[0, h] for h in range(H)] + [s_scr[1, h] for h in range(H)]
    out, new_state = _delta_chunks(q, k, v, beta, gc_col, gc_row, gtot, state, [True] * H + [False] * H)
    for h in range(H):
        of_ref[:, sls[h]] = out[h]
        ob_ref[:, sls[h]] = out[H + h]
        s_scr[0, h] = new_state[h]
        s_scr[1, h] = new_state[H + h]


def _gdn_scan(qkv, gates, gates_row, n_heads, head_dim):
    B, L, _ = qkv.shape
    H, Dh = n_heads, head_dim
    d = H * Dh
    C = GDN_CHUNK
    N = L // C
    fwd = lambda col: pl.BlockSpec((None, C, d), lambda b, n: (b, n, col))
    bwd = lambda col: pl.BlockSpec((None, C, d), lambda b, n: (b, N - 1 - n, col))
    return pl.pallas_call(
        functools.partial(_gdn_scan_kernel, n_heads=H, head_dim=Dh),
        grid=(B, N),
        in_specs=[fwd(0), fwd(1), fwd(2), bwd(0), bwd(1), bwd(2),
                  pl.BlockSpec((None, C, LANES), lambda b, n: (b, n, 0)),
                  pl.BlockSpec((None, C, LANES), lambda b, n: (b, N - 1 - n, 0)),
                  pl.BlockSpec((None, None, 2 * H, C), lambda b, n: (b, n, 0, 0)),
                  pl.BlockSpec((None, None, 2 * H, C), lambda b, n: (b, N - 1 - n, 0, 0))],
        out_specs=[pl.BlockSpec((None, C, d), lambda b, n: (b, n, 0)),
                   pl.BlockSpec((None, C, d), lambda b, n: (b, N - 1 - n, 0))],
        out_shape=[jax.ShapeDtypeStruct((B, L, d), F32)] * 2,
        scratch_shapes=[pltpu.VMEM((2, H, Dh, Dh), F32)],
        compiler_params=_cparams(("parallel", "arbitrary"), 32),
        name="gdn_scan",
    )(qkv, qkv, qkv, qkv, qkv, qkv, gates, gates, gates_row, gates_row)


def _gdn_branch(p3, pg, conv_w, a_log_f, a_log_b, dt_bias_f, dt_bias_b, col0, n_heads, head_dim):
    B, L, _ = p3.shape
    H = n_heads
    qkv = _gdn_pre(p3, conv_w, col0, H, head_dim)
    gates = _gdn_gates(pg, a_log_f, a_log_b, dt_bias_f, dt_bias_b, H).reshape(B, L, LANES)
    N = L // GDN_CHUNK
    gates_row = gates[..., 2 * H:4 * H].reshape(B, N, GDN_CHUNK, 2 * H).transpose(0, 1, 3, 2)
    return _gdn_scan(qkv, gates, gates_row, H, head_dim)


@functools.lru_cache(maxsize=None)
def _filter_positions(L, pos_emb_dim):
    n = 2 * L
    r = np.arange(n)
    k = np.where(r < L, r, np.where(r == L, 0, n - r)).astype(np.float64)
    t = k / (L - 1)
    bands = (pos_emb_dim - 1) // 2
    fb = np.linspace(1e-4, bands - 1, bands)
    ang = (2.0 * math.pi / L) * k[:, None] * fb[None, :]
    z = np.concatenate([t[:, None], np.cos(ang), -np.sin(ang)], axis=-1)
    return z.astype(np.float32)


@functools.lru_cache(maxsize=None)
def _decay_rates(d_hy):
    max_decay = math.log(DECAY_TARGET) / FAST_DECAY_PCT
    min_decay = math.log(DECAY_TARGET) / SLOW_DECAY_PCT
    return np.abs(np.linspace(min_decay, max_decay, d_hy)).astype(np.float32).reshape(1, d_hy)


def _filt_mlp_kernel(zt_ref, w1t_ref, b1_ref, w2t_ref, b2_ref, w3t_ref, b3_ref, fr_ref, o_ref):
    fr = fr_ref[...]
    h = jnp.sin(fr * (_dot_hi(w1t_ref[...], zt_ref[...]) + b1_ref[...]))
    h = jnp.sin(fr * (_dot_hi(w2t_ref[...], h) + b2_ref[...]))
    o_ref[...] = jnp.sin(fr * (_dot_hi(w3t_ref[...], h) + b3_ref[...]))


def _filt_mlp(zt, w1, b1, w2, b2, w3, b3, freq, tc=1024):
    pe, n = zt.shape
    fw = w1.shape[1]
    col = lambda a: a.reshape(-1, 1).astype(F32)
    full = lambda a: pl.BlockSpec(a.shape, lambda i: (0, 0))
    args = (zt, w1.T, col(b1), w2.T, col(b2), w3.T, col(b3), col(freq))
    return pl.pallas_call(
        _filt_mlp_kernel,
        grid=(n // tc,),
        in_specs=[pl.BlockSpec((pe, tc), lambda i: (0, i))] + [full(a) for a in args[1:]],
        out_specs=pl.BlockSpec((fw, tc), lambda i: (0, i)),
        out_shape=jax.ShapeDtypeStruct((fw, n), F32),
        compiler_params=_cparams(("parallel",), 24),
        name="filt_mlp",
    )(*args)


@functools.lru_cache(maxsize=None)
def _dft_tables(L):
    n = 2 * L
    N2 = FFT_N2
    N1 = n // N2
    N1h = N1 // 2
    j2 = np.arange(N2)[:, None, None]
    k1 = np.arange(N1)[None, :, None]

    def stage1(n_j1):
        j1 = np.arange(n_j1)[None, None, :]
        m = (k1 * (N2 * j1 + j2)) % n
        th = 2.0 * np.pi * m / n
        return np.cos(th), np.sin(th)

    c, s = stage1(N1h)
    t1 = np.concatenate([np.concatenate([c, s], axis=2), np.concatenate([-s, c], axis=2)], axis=1)
    c, s = stage1(N1)
    t1g = np.concatenate([c, -s], axis=1)
    c, s = stage1(N1h)
    ct, st = np.swapaxes(c, 1, 2) / n, np.swapaxes(s, 1, 2) / n
    t2 = np.concatenate([np.concatenate([ct, -st], axis=2), np.concatenate([st, ct], axis=2)], axis=1)
    a = np.arange(N2)
    th = 2.0 * np.pi * ((a[:, None] * a[None, :]) % N2) / N2
    c2, s2 = np.cos(th), np.sin(th)
    f2f = np.block([[c2, s2], [-s2, c2]])
    f2i = np.block([[c2, -s2], [s2, c2]])
    as_bf16 = lambda x: jnp.asarray(x, dtype=F32).astype(BF16)
    return dict(N1=N1, N2=N2, t1=t1.astype(np.float32), t1g=t1g.astype(np.float32), t2=t2.astype(np.float32),
                f2f=f2f.astype(np.float32), f2i=f2i.astype(np.float32))


FILT_ROWS = 512
FFT_UNROLL = 8
PITCH_PAD = 8


def _filt_fft_kernel(h3_ref, w4f_ref, w4b_ref, delta_ref, t1g_ref, f2f_ref, hspec_ref, hb0_ref, g_scr, a_scr,
                     *, L, N1, N2):
    n = 2 * L
    gp = N2 + PITCH_PAD
    ap = 2 * N1 + PITCH_PAD
    delta = delta_ref[...]
    hb0_ref[...] = jnp.zeros_like(hb0_ref)

    def gen(c, carry):
        r0 = pl.multiple_of(c * FILT_ROWS, FILT_ROWS)
        row = r0 + lax.broadcasted_iota(jnp.int32, (FILT_ROWS, LANES), 0)
        lag = jnp.where(row < L, row, jnp.where(row == L, 0, n - row))
        window = jnp.exp(-(lag.astype(F32) * (1.0 / (L - 1))) * delta)
        w4 = jnp.where(r0 < L, w4f_ref[...], w4b_ref[...])
        g = _dot(h3_ref[pl.ds(r0, FILT_ROWS), :], w4) * window
        at_l = row == L
        hb0_ref[...] += jnp.sum(jnp.where(at_l, g, 0.0), axis=0, keepdims=True)
        g = jnp.where(at_l, 0.0, g)
        for q in range(FILT_ROWS // N2):
            dst = pl.multiple_of((c * (FILT_ROWS // N2) + q) * gp, SUBLANES)
            g_scr[pl.ds(dst, N2), :] = g[q * N2:(q + 1) * N2]
        return carry

    lax.fori_loop(0, n // FILT_ROWS, gen, 0)

    def stage1(j2, carry):
        x = g_scr[pl.ds(j2, N1, stride=gp), :]
        a_scr[pl.ds(pl.multiple_of(j2 * ap, SUBLANES), 2 * N1), :] = _dot(t1g_ref[j2], x)
        return carry

    lax.fori_loop(0, N2, stage1, 0, unroll=FFT_UNROLL)

    def stage2(k1, carry):
        ar = a_scr[pl.ds(k1, N2, stride=ap), :]
        ai = a_scr[pl.ds(N1 + k1, N2, stride=ap), :]
        z = _dot(f2f_ref[...], jnp.concatenate([ar, ai], axis=0))
        hspec_ref[pl.ds(pl.multiple_of(k1 * 2 * N2, 2 * N2), 2 * N2), :] = z.astype(hspec_ref.dtype)
        return carry

    lax.fori_loop(0, N1, stage2, 0, unroll=FFT_UNROLL)


def _filt_fft(h3, w4, L, d_hy):
    tb = _dft_tables(L)
    N1, N2 = tb["N1"], tb["N2"]
    n = 2 * L
    fw = h3.shape[1]
    nt = d_hy // LANES
    t1g = jnp.asarray(tb["t1g"]).astype(BF16)
    f2f = jnp.asarray(tb["f2f"]).astype(BF16)
    return pl.pallas_call(
        functools.partial(_filt_fft_kernel, L=L, N1=N1, N2=N2),
        grid=(nt,),
        in_specs=[pl.BlockSpec((n, fw), lambda c: (0, 0)),
                  pl.BlockSpec((fw, LANES), lambda c: (0, c)),
                  pl.BlockSpec((fw, LANES), lambda c: (0, c + nt)),
                  pl.BlockSpec((1, LANES), lambda c: (0, c)),
                  pl.BlockSpec(t1g.shape, lambda c: (0, 0, 0)),
                  pl.BlockSpec(f2f.shape, lambda c: (0, 0))],
        out_specs=[pl.BlockSpec((2 * n, LANES), lambda c: (0, c)),
                   pl.BlockSpec((SUBLANES, LANES), lambda c: (0, c))],
        out_shape=[jax.ShapeDtypeStruct((2 * n, d_hy), BF16), jax.ShapeDtypeStruct((SUBLANES, d_hy), F32)],
        scratch_shapes=[pltpu.VMEM((N1 * (N2 + PITCH_PAD), LANES), F32),
                        pltpu.VMEM((N2 * (2 * N1 + PITCH_PAD), LANES), F32)],
        compiler_params=_cparams(("parallel",), 48),
        name="filt_fft",
    )(h3, w4, w4, jnp.asarray(_decay_rates(d_hy)), t1g, f2f)


def _hy_conv_kernel(vg_ref, hspec_ref, skip_ref, hb0_ref, t1_ref, f2f_ref, f2i_ref, t2_ref, y_ref,
                    x_scr, a_scr, b_scr, *, N1, N2):
    N1h = N1 // 2
    xp = N2 + PITCH_PAD
    ap = 2 * N1 + PITCH_PAD
    bp = 2 * N2 + PITCH_PAD

    for b in range(2):
        for j1 in range(N1h):
            x_scr[b, pl.ds(j1 * xp, N2), :] = vg_ref[b, pl.ds(j1 * N2, N2), :]

    def stage1(j2, carry):
        x = jnp.concatenate([x_scr[0, pl.ds(j2, N1h, stride=xp), :],
                             x_scr[1, pl.ds(j2, N1h, stride=xp), :]], axis=0)
        a_scr[pl.ds(pl.multiple_of(j2 * ap, SUBLANES), 2 * N1), :] = _dot(t1_ref[j2], x)
        return carry

    lax.fori_loop(0, N2, stage1, 0, unroll=FFT_UNROLL)

    def stage2(k1, carry):
        ar = a_scr[pl.ds(k1, N2, stride=ap), :]
        ai = a_scr[pl.ds(N1 + k1, N2, stride=ap), :]
        z = _dot(f2f_ref[...], jnp.concatenate([ar, ai], axis=0))
        zr, zi = z[:N2], z[N2:]
        base = pl.multiple_of(k1 * 2 * N2, 2 * N2)
        hr = hspec_ref[pl.ds(base, N2), :].astype(F32)
        hi = hspec_ref[pl.ds(base + N2, N2), :].astype(F32)
        prod = jnp.concatenate([zr * hr - zi * hi, zr * hi + zi * hr], axis=0)
        b_scr[pl.ds(pl.multiple_of(k1 * bp, SUBLANES), 2 * N2), :] = _dot(f2i_ref[...], prod)
        return carry

    lax.fori_loop(0, N1, stage2, 0, unroll=FFT_UNROLL)

    skip = skip_ref[...] + hb0_ref[0:1, :]

    def stage3(j2, carry):
        b = jnp.concatenate([b_scr[pl.ds(j2, N1, stride=bp), :],
                             b_scr[pl.ds(N2 + j2, N1, stride=bp), :]], axis=0)
        y = _dot(t2_ref[j2], b)
        x_scr[0, pl.ds(j2, N1h, stride=xp), :] = y[:N1h] + x_scr[0, pl.ds(j2, N1h, stride=xp), :] * skip
        x_scr[1, pl.ds(j2, N1h, stride=xp), :] = y[N1h:] + x_scr[1, pl.ds(j2, N1h, stride=xp), :] * skip
        return carry

    lax.fori_loop(0, N2, stage3, 0, unroll=FFT_UNROLL)

    for b in range(2):
        for j1 in range(N1h):
            y_ref[b, pl.ds(j1 * N2, N2), :] = x_scr[b, pl.ds(j1 * xp, N2), :]


def _hy_conv(vg, hspec, skip, hb0):
    B, L, d_hy = vg.shape
    assert B % 2 == 0
    tb = _dft_tables(L)
    N1, N2 = tb["N1"], tb["N2"]
    n = 2 * L
    nt = d_hy // LANES
    t1, t2 = (jnp.asarray(tb[k]).astype(BF16) for k in ("t1", "t2"))
    f2f, f2i = (jnp.asarray(tb[k]).astype(BF16) for k in ("f2f", "f2i"))
    const3 = lambda a: pl.BlockSpec(a.shape, lambda b, c: (0, 0, 0))
    const2 = lambda a: pl.BlockSpec(a.shape, lambda b, c: (0, 0))
    return pl.pallas_call(
        functools.partial(_hy_conv_kernel, N1=N1, N2=N2),
        grid=(B // 2, nt),
        in_specs=[pl.BlockSpec((2, L, LANES), lambda b, c: (b, 0, c)),
                  pl.BlockSpec((2 * n, LANES), lambda b, c: (0, c)),
                  pl.BlockSpec((1, LANES), lambda b, c: (0, c)),
                  pl.BlockSpec((SUBLANES, LANES), lambda b, c: (0, c)),
                  const3(t1), const2(f2f), const2(f2i), const3(t2)],
        out_specs=pl.BlockSpec((2, L, LANES), lambda b, c: (b, 0, c)),
        out_shape=jax.ShapeDtypeStruct((B, L, d_hy), F32),
        scratch_shapes=[pltpu.VMEM((2, (N1 // 2) * (N2 + PITCH_PAD), LANES), F32),
                        pltpu.VMEM((N2 * (2 * N1 + PITCH_PAD), LANES), F32),
                        pltpu.VMEM((N1 * (2 * N2 + PITCH_PAD), LANES), F32)],
        compiler_params=_cparams(("parallel", "parallel"), 58),
        name="hy_conv",
    )(vg, hspec, skip.reshape(1, d_hy).astype(F32), hb0, t1, f2f, f2i, t2)


def _hyena_branch(p3, conv_w, conv_b, fw1, fb1, fw2, fb2, fw3, fb3, fw4, freq, skip, d_hy):
    B, L, _ = p3.shape
    x0c, vg = _hy_pre(p3, conv_w, conv_b, d_hy)
    zt = jnp.asarray(_filter_positions(L, fw1.shape[0]).T)
    h3 = _filt_mlp(zt, fw1, fb1, fw2, fb2, fw3, fb3, freq).T
    hspec, hb0 = _filt_fft(h3, fw4, L, d_hy)
    return x0c, _hy_conv(vg, hspec, skip, hb0)


ROUTE_GATE, ROUTE_EXPERT, ROUTE_RANK = 0, 2, 4
MIX_ROWS = 256


def _mix_route_kernel(yc_ref, x0_ref, of_ref, ob_ref, z_ref, hnw_ref, gnw_ref, x_ref, wo_ref, n2w_ref, wr_ref, br_ref,
                      x2_ref, u_ref, route_ref, route_t_ref, cnt_ref, run_scr, *, n_heads, head_dim, n_groups,
                      per_group):
    tm = x_ref.shape[0]
    G, P = n_groups, per_group

    @pl.when(pl.program_id(0) == 0)
    def _():
        run_scr[...] = jnp.zeros_like(run_scr)

    run = run_scr[0:1, :]
    blocks = [pl.ds(s * MIX_ROWS, MIX_ROWS) for s in range(tm // MIX_ROWS)]
    us = []
    for rs in blocks:
        yh = yc_ref[rs, :] * x0_ref[rs, :]
        yh = yh * lax.rsqrt(jnp.mean(yh * yh, axis=-1, keepdims=True) + EPS) * hnw_ref[...]
        parts = [yh.astype(BF16)]
        for h in range(n_heads):
            sl = slice(h * head_dim, (h + 1) * head_dim)
            o = of_ref[rs, sl] + ob_ref[rs, sl]
            o = o * lax.rsqrt(jnp.mean(o * o, axis=-1, keepdims=True) + EPS) * gnw_ref[...] * _silu(z_ref[rs, sl])
            parts.append(o.astype(BF16))
        ymix = jnp.concatenate(parts, axis=-1)
        x2 = x_ref[rs, :] + jnp.dot(ymix, wo_ref[...], preferred_element_type=F32)
        x2_ref[rs, :] = x2
        u = x2 * lax.rsqrt(jnp.mean(x2 * x2, axis=-1, keepdims=True) + EPS) * n2w_ref[...]
        u_ref[rs, :] = u
        us.append(u)

    all_logits = [_dot(u, wr_ref[...]) + br_ref[...] for u in us]
    for s, (rs, logits) in enumerate(zip(blocks, all_logits)):
        lane = lax.broadcasted_iota(jnp.int32, logits.shape, 1)
        neg = jnp.float32(-jnp.inf)
        big = jnp.int32(4 * LANES)
        first = lambda hit: jnp.min(jnp.where(hit, lane, big), axis=-1, keepdims=True)
        gl = jnp.where(lane < G, logits, neg)
        gmax = jnp.max(gl, axis=-1, keepdims=True)
        gidx = first(gl == gmax)
        grp_gate = 1.0 / jnp.sum(jnp.exp(gl - gmax), axis=-1, keepdims=True)
        in_grp = (lane >= G) & (lane < G + G * P) & (((lane - G) // P) == gidx)
        ll = jnp.where(in_grp, logits, neg)
        m1 = jnp.max(ll, axis=-1, keepdims=True)
        i1 = first(ll == m1)
        denom = jnp.sum(jnp.exp(ll - m1), axis=-1, keepdims=True)
        ll2 = jnp.where(lane == i1, neg, ll)
        m2 = jnp.max(ll2, axis=-1, keepdims=True)
        i2 = first(ll2 == m2)
        p1 = 1.0 / denom
        p2 = jnp.exp(m2 - m1) / denom
        gate1 = grp_gate * (p1 / (p1 + p2))
        gate2 = grp_gate * (p2 / (p1 + p2))
        e1 = i1 - G
        e2 = i2 - G

        oh1 = jnp.where(lane == e1, 1.0, 0.0)
        oh2 = jnp.where(lane == e2, 1.0, 0.0)
        oh = oh1 + oh2
        ii = lax.broadcasted_iota(jnp.int32, (MIX_ROWS, MIX_ROWS), 0)
        jj = lax.broadcasted_iota(jnp.int32, (MIX_ROWS, MIX_ROWS), 1)
        before = _dot(jnp.where(ii > jj, 1.0, 0.0), oh) + run
        r1 = jnp.sum(oh1 * before, axis=-1, keepdims=True)
        r2 = jnp.sum(oh2 * before, axis=-1, keepdims=True)
        run = run + jnp.sum(oh, axis=0, keepdims=True)

        rec = jnp.where(lane == ROUTE_GATE, gate1, 0.0)
        rec = jnp.where(lane == ROUTE_GATE + 1, gate2, rec)
        rec = jnp.where(lane == ROUTE_EXPERT, e1.astype(F32), rec)
        rec = jnp.where(lane == ROUTE_EXPERT + 1, e2.astype(F32), rec)
        rec = jnp.where(lane == ROUTE_RANK, r1, rec)
        rec = jnp.where(lane == ROUTE_RANK + 1, r2, rec)
        route_ref[rs, :] = rec
        route_t_ref[:, s * MIX_ROWS:(s + 1) * MIX_ROWS] = jnp.transpose(rec)[:SUBLANES, :]

    run_scr[...] = jnp.broadcast_to(run, run_scr.shape)
    cnt_ref[...] = run_scr[...]


def _mix_route(yconv, x0c, o_f, o_b, p, z_col, hy_norm_w, gdn_norm_w, xf, w_out_bf16, norm2_w, wr, br,
               n_heads, head_dim, n_groups, per_group, tm=512):
    M, D = xf.shape
    d_hy = yconv.shape[1]
    d_gdn = o_f.shape[1]
    assert z_col % d_gdn == 0 and n_groups * (per_group + 1) <= LANES
    zb = z_col // d_gdn
    row = lambda i: (i, 0)
    const = lambda i: (0, 0)
    kern = functools.partial(_mix_route_kernel, n_heads=n_heads, head_dim=head_dim,
                             n_groups=n_groups, per_group=per_group)
    return pl.pallas_call(
        kern,
        grid=(M // tm,),
        in_specs=[pl.BlockSpec((tm, d_hy), row), pl.BlockSpec((tm, d_hy), row),
                  pl.BlockSpec((tm, d_gdn), row), pl.BlockSpec((tm, d_gdn), row),
                  pl.BlockSpec((tm, d_gdn), lambda i: (i, zb)),
                  pl.BlockSpec((1, d_hy), const), pl.BlockSpec((1, head_dim), const),
                  pl.BlockSpec((tm, D), row), pl.BlockSpec(w_out_bf16.shape, const, pipeline_mode=pl.Buffered(1)),
                  pl.BlockSpec((1, D), const), pl.BlockSpec((D, LANES), const), pl.BlockSpec((1, LANES), const)],
        out_specs=[pl.BlockSpec((tm, D), row), pl.BlockSpec((tm, D), row),
                   pl.BlockSpec((tm, LANES), row), pl.BlockSpec((SUBLANES, tm), lambda i: (0, i)),
                   pl.BlockSpec((SUBLANES, LANES), const)],
        out_shape=[jax.ShapeDtypeStruct((M, D), F32), jax.ShapeDtypeStruct((M, D), F32),
                   jax.ShapeDtypeStruct((M, LANES), F32), jax.ShapeDtypeStruct((SUBLANES, M), F32),
                   jax.ShapeDtypeStruct((SUBLANES, LANES), F32)],
        scratch_shapes=[pltpu.VMEM((SUBLANES, LANES), F32)],
        compiler_params=_cparams(("arbitrary",), 58),
        name="mix_route",
    )(yconv, x0c, o_f, o_b, p, hy_norm_w.reshape(1, d_hy), gdn_norm_w.reshape(1, head_dim), xf, w_out_bf16,
      norm2_w.reshape(1, D), wr, br)


def _experts_kernel(te_ref, na_ref, tgt_nxt_ref, tgt_ref, tgt_prv_ref, u_hbm, w1_ref, w3_ref, w2_ref, out_hbm,
                    x0, x1, y0, y1, w1b, w3b, w2b, gsem, ssem, *, n_tokens):
    i = pl.program_id(0)
    na = na_ref[0]
    TB = x0.shape[0]
    spare0 = 2 * n_tokens

    def token_of(v):
        if n_tokens & (n_tokens - 1) == 0:
            return v & (n_tokens - 1)
        return lax.rem(v, n_tokens)

    def gather(tgt, xbuf, s, r):
        return pltpu.make_async_copy(u_hbm.at[pl.ds(token_of(tgt[r]), 1)], xbuf.at[pl.ds(r, 1)], gsem.at[s])

    def scatter(ybuf, s, r, dst):
        return pltpu.make_async_copy(ybuf.at[pl.ds(r, 1)], out_hbm.at[pl.ds(dst, 1)], ssem.at[s])

    def gather_wait(xbuf, s):
        pltpu.make_async_copy(u_hbm.at[pl.ds(0, TB)], xbuf, gsem.at[s]).wait()

    def scatter_wait(ybuf, s):
        pltpu.make_async_copy(ybuf, out_hbm.at[pl.ds(0, TB)], ssem.at[s]).wait()

    def each_row(fn):
        def body(r, carry):
            fn(r)
            return carry
        lax.fori_loop(0, TB, body, 0, unroll=8)

    def step(p):
        q = 1 - p
        x_cur, x_nxt = (x0, x1) if p == 0 else (x1, x0)
        y_cur, y_prv = (y0, y1) if p == 0 else (y1, y0)
        gather_wait(x_cur, p)

        @pl.when(i > 0)
        def _():
            scatter_wait(y_cur, p)

        x = x_cur[...].astype(BF16)
        for r in range(TB):
            gather(tgt_nxt_ref, x_nxt, q, r).start()
        first = i == 0
        for r in range(TB):
            scatter(y_prv, q, r, jnp.where(first, spare0 + r, tgt_prv_ref[r])).start()
        h = _silu(jnp.dot(x, w1b[...], preferred_element_type=F32)) * jnp.dot(x, w3b[...], preferred_element_type=F32)
        y_cur[...] = jnp.dot(h.astype(BF16), w2b[...], preferred_element_type=F32)

        @pl.when(i == na - 1)
        def _():
            gather_wait(x_nxt, q)
            scatter_wait(y_prv, q)
            each_row(lambda r: scatter(y_cur, p, r, tgt_ref[r]).start())
            scatter_wait(y_cur, p)

    @pl.when(i < na)
    def _():
        @pl.when(i == 0)
        def _():
            y0[...] = jnp.zeros_like(y0)
            y1[...] = jnp.zeros_like(y1)
            fill = pltpu.make_async_copy(y1, out_hbm.at[pl.ds(spare0, TB)], ssem.at[1])
            fill.start()
            fill.wait()
            each_row(lambda r: gather(tgt_ref, x0, 0, r).start())

        @pl.when((i == 0) | (te_ref[i] != te_ref[jnp.maximum(i - 1, 0)]))
        def _():
            w1b[...] = w1_ref[...].astype(BF16)
            w3b[...] = w3_ref[...].astype(BF16)
            w2b[...] = w2_ref[...].astype(BF16)

        parity = lax.rem(i, 2)

        @pl.when(parity == 0)
        def _():
            step(0)

        @pl.when(parity == 1)
        def _():
            step(1)


def _experts(u, slot_tgt, tile_expert, n_active, w1, w3, w2):
    T, D = u.shape
    E, _, de = w1.shape
    TB = EXPERT_ROWS
    n_tiles = slot_tgt.shape[0] // TB
    wmap = lambda i, te, na: (te[i], 0, 0)
    table = lambda f: pl.BlockSpec((TB,), lambda i, te, na: (f(i),), memory_space=pltpu.SMEM)
    grid_spec = pltpu.PrefetchScalarGridSpec(
        num_scalar_prefetch=2,
        grid=(n_tiles,),
        in_specs=[table(lambda i: jnp.minimum(i + 1, n_tiles - 1)), table(lambda i: i),
                  table(lambda i: jnp.maximum(i - 1, 0)),
                  pl.BlockSpec(memory_space=pl.ANY),
                  pl.BlockSpec((None, D, de), wmap), pl.BlockSpec((None, D, de), wmap),
                  pl.BlockSpec((None, de, D), wmap)],
        out_specs=pl.BlockSpec(memory_space=pl.ANY),
        scratch_shapes=[pltpu.VMEM((TB, D), F32)] * 4
                       + [pltpu.VMEM((D, de), BF16), pltpu.VMEM((D, de), BF16), pltpu.VMEM((de, D), BF16),
                          pltpu.SemaphoreType.DMA((2,)), pltpu.SemaphoreType.DMA((2,))],
    )
    return pl.pallas_call(
        functools.partial(_experts_kernel, n_tokens=T),
        grid_spec=grid_spec,
        out_shape=jax.ShapeDtypeStruct((2 * T + TB, D), F32),
        compiler_params=_cparams(("arbitrary",), 52),
        name="experts",
    )(tile_expert, n_active, slot_tgt, slot_tgt, slot_tgt, u, w1, w3, w2)


def _slot_table_kernel(dest_ref, init_hbm, o_ref, sem):
    fill = pltpu.make_async_copy(init_hbm, o_ref, sem)
    fill.start()
    fill.wait()

    def put(a, carry):
        o_ref[dest_ref[a]] = a
        return carry

    lax.fori_loop(0, dest_ref.shape[0], put, 0, unroll=8)


def _slot_table(dest, init):
    smem = pl.BlockSpec(memory_space=pltpu.SMEM)
    return pl.pallas_call(
        _slot_table_kernel,
        in_specs=[smem, pl.BlockSpec(memory_space=pl.ANY)],
        out_specs=smem,
        out_shape=jax.ShapeDtypeStruct(init.shape, jnp.int32),
        scratch_shapes=[pltpu.SemaphoreType.DMA],
        name="slot_table",
    )(dest, init)


def _dispatch_tables(route_t, counts, n_experts):
    T = route_t.shape[1]
    TB = EXPERT_ROWS
    e = route_t[ROUTE_EXPERT:ROUTE_EXPERT + 2].astype(jnp.int32)
    rank = route_t[ROUTE_RANK:ROUTE_RANK + 2].astype(jnp.int32)
    cnt = counts[0, :n_experts].astype(jnp.int32)
    padded = (cnt + TB - 1) // TB * TB
    pad_end = jnp.cumsum(padded)
    pad_start = pad_end - padded
    ids = jnp.arange(n_experts, dtype=jnp.int32)[:, None, None]
    start_of = jnp.sum(jnp.where(e[None] == ids, pad_start[:, None, None], 0), axis=0)
    dest = start_of + rank
    n_tiles = -(-(2 * T + n_experts * (TB - 1)) // TB)
    padded_slot = 2 * T + jnp.arange(n_tiles * TB, dtype=jnp.int32) % TB
    slot_tgt = _slot_table(dest.reshape(-1), padded_slot)
    start = jnp.arange(n_tiles, dtype=jnp.int32) * TB
    tile_expert = jnp.sum(start[:, None] >= pad_end[None, :], axis=1)
    tile_expert = jnp.minimum(tile_expert, n_experts - 1).astype(jnp.int32)
    n_active = (pad_end[-1] // TB).astype(jnp.int32).reshape(1)
    return slot_tgt, tile_expert, n_active


def _combine_kernel(x2_ref, e0_ref, e1_ref, route_ref, w_ref, o_ref, *, final_norm):
    r = route_ref[...]
    y = x2_ref[...] + r[:, ROUTE_GATE:ROUTE_GATE + 1] * e0_ref[...] + r[:, ROUTE_GATE + 1:ROUTE_GATE + 2] * e1_ref[...]
    if final_norm:
        y = y * lax.rsqrt(jnp.mean(y * y, axis=-1, keepdims=True) + EPS) * w_ref[...]
    o_ref[...] = y


def _combine(x2, planes, route, norm_w, final_norm, tm=512):
    M, D = x2.shape
    return pl.pallas_call(
        functools.partial(_combine_kernel, final_norm=final_norm),
        grid=(M // tm,),
        in_specs=[pl.BlockSpec((tm, D), lambda i: (i, 0)),
                  pl.BlockSpec((tm, D), lambda i: (i, 0)),
                  pl.BlockSpec((tm, D), lambda i: (M // tm + i, 0)),
                  pl.BlockSpec((tm, LANES), lambda i: (i, 0)),
                  pl.BlockSpec((1, D), lambda i: (0, 0))],
        out_specs=pl.BlockSpec((tm, D), lambda i: (i, 0)),
        out_shape=jax.ShapeDtypeStruct((M, D), F32),
        compiler_params=_cparams(("parallel",), 48),
        name="combine",
    )(x2, planes, planes, route, norm_w.reshape(1, D))


def kernel(x, norm1_w, w_in, hy_conv_w, hy_conv_b, hy_filt_w1, hy_filt_b1, hy_filt_w2, hy_filt_b2, hy_filt_w3, hy_filt_b3, hy_filt_w4, hy_sin_freq, hy_skip, hy_norm_w, gdn_conv_w, gdn_a_log_f, gdn_a_log_b, gdn_dt_bias_f, gdn_dt_bias_b, gdn_norm_w, w_out, norm2_w, router_group_w, router_group_b, router_expert_w, router_expert_b, exp_w1, exp_w3, exp_w2, final_norm_w):
    B, L, D = x.shape
    M = B * L
    depth = w_in.shape[0]
    d_hy = hy_skip.shape[-1]
    H = gdn_a_log_f.shape[-1]
    Dh = gdn_norm_w.shape[-1]
    d_gdn = H * Dh
    n_main = 3 * d_hy + 4 * d_gdn
    G = router_group_w.shape[-1]
    E = router_expert_w.shape[-1]
    xf = x.reshape(M, D)
    for l in range(depth):
        p, pg = _inproj(xf, norm1_w[l], jnp.swapaxes(w_in[l], 0, 1), n_main, 4 * H)
        p3 = p.reshape(B, L, n_main)
        x0c, yconv = _hyena_branch(p3, hy_conv_w[l], hy_conv_b[l], hy_filt_w1[l], hy_filt_b1[l], hy_filt_w2[l],
                                   hy_filt_b2[l], hy_filt_w3[l], hy_filt_b3[l], hy_filt_w4[l], hy_sin_freq[l],
                                   hy_skip[l], d_hy)
        o_f, o_b = _gdn_branch(p3, pg, gdn_conv_w[l], gdn_a_log_f[l], gdn_a_log_b[l], gdn_dt_bias_f[l],
                               gdn_dt_bias_b[l], 3 * d_hy, H, Dh)
        wr = jnp.pad(jnp.concatenate([router_group_w[l], router_expert_w[l]], axis=1), ((0, 0), (0, LANES - G - E)))
        br = jnp.pad(jnp.concatenate([router_group_b[l], router_expert_b[l]]), (0, LANES - G - E)).reshape(1, LANES)
        x2, u, route, route_t, counts = _mix_route(
            yconv.reshape(M, d_hy), x0c.reshape(M, d_hy), o_f.reshape(M, d_gdn), o_b.reshape(M, d_gdn), p,
            3 * d_hy + 3 * d_gdn, hy_norm_w[l], gdn_norm_w[l], xf, w_out[l].astype(BF16), norm2_w[l], wr, br,
            H, Dh, G, E // G)
        slot_tgt, tile_expert, n_active = _dispatch_tables(route_t, counts, E)
        planes = _experts(u, slot_tgt, tile_expert, n_active, exp_w1[l], exp_w3[l], exp_w2[l])
        xf = _combine(x2, planes, route, final_norm_w, final_norm=(l == depth - 1))
    return xf.reshape(B, L, D)
```

```python
import functools
import math

import jax
import jax.numpy as jnp
import numpy as np
from jax import lax
from jax.experimental import pallas as pl
from jax.experimental.pallas import tpu as pltpu

F32 = jnp.float32
BF16 = jnp.bfloat16
EPS = 1e-6
LANES = 128
SUBLANES = 8
VMEM_BYTES_V7X = 64 * 1024 * 1024
GDN_CHUNK = 64
FFT_N2 = 128
EXPERT_ROWS = 256
DECAY_TARGET = 1e-2
FAST_DECAY_PCT = 0.3
SLOW_DECAY_PCT = 1.5


def _cparams(sem, vmem_mb):
    return pltpu.CompilerParams(dimension_semantics=sem, vmem_limit_bytes=int(vmem_mb * 1024 * 1024))


def _dot(a, b):
    return jnp.dot(a.astype(BF16), b.astype(BF16), preferred_element_type=F32)


def _dot_nt(a, b):
    return lax.dot_general(a.astype(BF16), b.astype(BF16), (((1,), (1,)), ((), ())), preferred_element_type=F32)


def _dot_tn(a, b):
    return lax.dot_general(a.astype(BF16), b.astype(BF16), (((0,), (0,)), ((), ())), preferred_element_type=F32)


def _dot_hi(a, b):
    return jnp.dot(a, b, preferred_element_type=F32, precision=lax.Precision.HIGHEST)


def _silu(x):
    return x * jax.nn.sigmoid(x)


def _pack_bf16_pairs(x):
    c = x.shape[1] // 2
    lo = lax.bitcast_convert_type(x[:, :c].astype(BF16).astype(F32), jnp.uint32) >> 16
    hi = lax.bitcast_convert_type(x[:, c:].astype(BF16).astype(F32), jnp.uint32) & jnp.uint32(0xFFFF0000)
    return hi | lo


def _unpack_bf16_pairs(w):
    lo = lax.bitcast_convert_type(w << 16, F32)
    hi = lax.bitcast_convert_type(w & jnp.uint32(0xFFFF0000), F32)
    return jnp.concatenate([lo, hi], axis=1)


def _inproj_kernel(x_ref, nw_ref, wt_ref, wgt_ref, p_ref, g_ref, h_scr):
    @pl.when(pl.program_id(1) == 0)
    def _():
        x = x_ref[...]
        h = x * lax.rsqrt(jnp.mean(x * x, axis=-1, keepdims=True) + EPS) * nw_ref[...]
        h_scr[...] = h.astype(BF16)
        g = _dot_nt(h, wgt_ref[...])
        g_ref[...] = jnp.concatenate([g, jnp.zeros((g.shape[0], LANES - g.shape[1]), F32)], axis=1)

    p_ref[...] = _dot_nt(h_scr[...], wt_ref[...])


def _inproj(xf, norm_w, wt, n_main, n_gate, tm=2048, tn=512):
    M, D = xf.shape
    assert n_main % tn == 0 and M % tm == 0 and n_main % n_gate == 0 and n_gate % SUBLANES == 0
    return pl.pallas_call(
        _inproj_kernel,
        grid=(M // tm, n_main // tn),
        in_specs=[
            pl.BlockSpec((tm, D), lambda i, j: (i, 0), pipeline_mode=pl.Buffered(1)),
            pl.BlockSpec((1, D), lambda i, j: (0, 0)),
            pl.BlockSpec((tn, D), lambda i, j: (j, 0)),
            pl.BlockSpec((n_gate, D), lambda i, j: (n_main // n_gate, 0)),
        ],
        out_specs=[
            pl.BlockSpec((tm, tn), lambda i, j: (i, j)),
            pl.BlockSpec((tm, LANES), lambda i, j: (i, 0)),
        ],
        out_shape=[jax.ShapeDtypeStruct((M, n_main), F32), jax.ShapeDtypeStruct((M, LANES), F32)],
        scratch_shapes=[pltpu.VMEM((tm, D), BF16)],
        compiler_params=_cparams(("parallel", "arbitrary"), 57),
        name="inproj",
    )(xf, norm_w.reshape(1, D), wt, wt)


def _conv3_rows(ref, r0, rows, w, n_rows):
    cur = ref[pl.ds(r0, rows), :]
    lo = jnp.maximum(r0 - SUBLANES, 0)
    hi = jnp.minimum(r0 + rows, n_rows - SUBLANES)
    prev8 = ref[pl.ds(pl.multiple_of(lo, SUBLANES), SUBLANES), :]
    next8 = ref[pl.ds(pl.multiple_of(hi, SUBLANES), SUBLANES), :]
    prev_row = jnp.where(r0 > 0, prev8[SUBLANES - 1:SUBLANES, :], 0.0)
    next_row = jnp.where(r0 + rows < n_rows, next8[0:1, :], 0.0)
    row = lax.broadcasted_iota(jnp.int32, cur.shape, 0)
    xm = jnp.where(row == 0, prev_row, pltpu.roll(cur, 1, 0))
    xp = jnp.where(row == rows - 1, next_row, pltpu.roll(cur, rows - 1, 0))
    return xm * w[0:1, :] + cur * w[1:2, :] + xp * w[2:3, :]


CONV_ROWS = 256


def _hy_pre_kernel(x0_ref, x1_ref, v_ref, w0_ref, w1_ref, w2_ref, b0_ref, b1_ref, b2_ref, x0c_ref, vg_ref):
    L = x0_ref.shape[0]
    w0, w1, w2 = w0_ref[...], w1_ref[...], w2_ref[...]
    b0, b1, b2 = b0_ref[...], b1_ref[...], b2_ref[...]

    def body(c, carry):
        r0 = pl.multiple_of(c * CONV_ROWS, CONV_ROWS)
        x0c_ref[pl.ds(r0, CONV_ROWS), :] = _conv3_rows(x0_ref, r0, CONV_ROWS, w0, L) + b0
        x1c = _conv3_rows(x1_ref, r0, CONV_ROWS, w1, L) + b1
        vc = _conv3_rows(v_ref, r0, CONV_ROWS, w2, L) + b2
        vg_ref[pl.ds(r0, CONV_ROWS), :] = vc * x1c
        return carry

    lax.fori_loop(0, L // CONV_ROWS, body, 0, unroll=2)


def _hy_pre(p3, conv_w, conv_b, d_hy):
    B, L, _ = p3.shape
    nt = d_hy // LANES
    bias = conv_b.reshape(1, -1)
    pspec = lambda off: pl.BlockSpec((None, L, LANES), lambda b, c: (b, 0, c + off))
    wspec = lambda off: pl.BlockSpec((3, LANES), lambda b, c: (0, c + off))
    bspec = lambda off: pl.BlockSpec((1, LANES), lambda b, c: (0, c + off))
    ospec = pl.BlockSpec((None, L, LANES), lambda b, c: (b, 0, c))
    return pl.pallas_call(
        _hy_pre_kernel,
        grid=(B, nt),
        in_specs=[pspec(0), pspec(nt), pspec(2 * nt), wspec(0), wspec(nt), wspec(2 * nt),
                  bspec(0), bspec(nt), bspec(2 * nt)],
        out_specs=[ospec, ospec],
        out_shape=[jax.ShapeDtypeStruct((B, L, d_hy), F32)] * 2,
        compiler_params=_cparams(("parallel", "parallel"), 40),
        name="hy_pre",
    )(p3, p3, p3, conv_w, conv_w, conv_w, bias, bias, bias)


def _gdn_pre_kernel(x_ref, w_ref, o_ref, *, n_heads, head_dim):
    L = x_ref.shape[0]
    w = w_ref[...]
    c = pl.program_id(1)
    q_scale = jnp.where(c < n_heads, head_dim ** -0.5, 1.0)
    is_qk = c < 2 * n_heads

    def body(i, carry):
        r0 = pl.multiple_of(i * CONV_ROWS, CONV_ROWS)
        y = _silu(_conv3_rows(x_ref, r0, CONV_ROWS, w, L))
        inv = lax.rsqrt(jnp.sum(y * y, axis=-1, keepdims=True) + EPS) * q_scale
        o_ref[pl.ds(r0, CONV_ROWS), :] = y * jnp.where(is_qk, inv, 1.0)
        return carry

    lax.fori_loop(0, L // CONV_ROWS, body, 0, unroll=2)


def _gdn_pre(p3, conv_w, col0, n_heads, head_dim):
    B, L, _ = p3.shape
    assert head_dim == LANES
    nt = 3 * n_heads
    off = col0 // LANES
    return pl.pallas_call(
        functools.partial(_gdn_pre_kernel, n_heads=n_heads, head_dim=head_dim),
        grid=(B, nt),
        in_specs=[pl.BlockSpec((None, L, LANES), lambda b, c: (b, 0, c + off)),
                  pl.BlockSpec((3, LANES), lambda b, c: (0, c))],
        out_specs=pl.BlockSpec((None, L, LANES), lambda b, c: (b, 0, c)),
        out_shape=jax.ShapeDtypeStruct((B, L, nt * LANES), F32),
        compiler_params=_cparams(("parallel", "parallel"), 24),
        name="gdn_pre",
    )(p3, conv_w)


GATE_ROWS = 512


def _gdn_gates_kernel(pg_ref, alog_ref, dtb_ref, o_ref, *, n_heads):
    H = n_heads
    x = pg_ref[...]
    beta = jax.nn.sigmoid(x)
    z = x + dtb_ref[...]
    softplus = jnp.maximum(z, 0.0) + jnp.log1p(jnp.exp(-jnp.abs(z)))
    g = -jnp.exp(alog_ref[...]) * softplus
    pos = lax.broadcasted_iota(jnp.int32, x.shape, 0) & (GDN_CHUNK - 1)
    gc_f = g
    gc_b = g
    step = 1
    while step < GDN_CHUNK:
        gc_f = gc_f + jnp.where(pos >= step, pltpu.roll(gc_f, step, 0), 0.0)
        gc_b = gc_b + jnp.where(pos < GDN_CHUNK - step, pltpu.roll(gc_b, GATE_ROWS - step, 0), 0.0)
        step *= 2
    g_tot = pltpu.roll(gc_f + gc_b - g, 4 * H, 1)
    lane = lax.broadcasted_iota(jnp.int32, x.shape, 1)
    out = jnp.where(lane < 2 * H, beta,
                    jnp.where(lane < 3 * H, gc_f,
                              jnp.where(lane < 4 * H, gc_b,
                                        jnp.where((lane >= 6 * H) & (lane < 8 * H), g_tot, 0.0))))
    o_ref[...] = out


def _gdn_gates(pg, a_log_f, a_log_b, dt_bias_f, dt_bias_b, n_heads):
    M = pg.shape[0]
    H = n_heads
    assert 8 * H <= LANES
    pad = lambda a, b: jnp.concatenate([jnp.zeros((2 * H,), F32), a.astype(F32), b.astype(F32),
                                        jnp.zeros((LANES - 4 * H,), F32)]).reshape(1, LANES)
    return pl.pallas_call(
        functools.partial(_gdn_gates_kernel, n_heads=H),
        grid=(M // GATE_ROWS,),
        in_specs=[pl.BlockSpec((GATE_ROWS, LANES), lambda i: (i, 0)),
                  pl.BlockSpec((1, LANES), lambda i: (0, 0)),
                  pl.BlockSpec((1, LANES), lambda i: (0, 0))],
        out_specs=pl.BlockSpec((GATE_ROWS, LANES), lambda i: (i, 0)),
        out_shape=jax.ShapeDtypeStruct((M, LANES), F32),
        compiler_params=_cparams(("parallel",), 24),
        name="gdn_gates",
    )(pg, pad(a_log_f, a_log_b), pad(dt_bias_f, dt_bias_b))


def _delta_chunks(q, k, v, beta, gc_col, gc_row, gtot, state, lower):
    n = len(q)
    C = q[0].shape[0]
    D = k[0].shape[1]
    ii = lax.broadcasted_iota(jnp.int32, (C, C), 0)
    jj = lax.broadcasted_iota(jnp.int32, (C, C), 1)
    eye = jnp.where(ii == jj, 1.0, 0.0)
    incl = [(ii >= jj) if lo else (ii <= jj) for lo in lower]
    strict = [(ii > jj) if lo else (ii < jj) for lo in lower]
    rng = range(n)
    decay = [jnp.where(incl[i], jnp.exp(jnp.where(incl[i], gc_col[i] - gc_row[i], 0.0)), 0.0) for i in rng]
    kb = [k[i] * beta[i] for i in rng]
    kk = [_dot_nt(kb[i], k[i]) for i in rng]
    qk = [_dot_nt(q[i], k[i]) for i in rng]
    m = [jnp.where(strict[i], -(kk[i] * decay[i]), 0.0) for i in rng]
    r = [eye + m[i] for i in rng]
    m = [_dot(m[i], m[i]) for i in rng]
    for _ in range(int(math.log2(C)) - 2):
        rm = [_dot(jnp.concatenate([r[i], m[i]], axis=0), m[i]) for i in rng]
        r = [r[i] + rm[i][:C] for i in rng]
        m = [rm[i][C:] for i in rng]
    r = [r[i] + _dot(r[i], m[i]) for i in rng]
    eg = [jnp.exp(gc_col[i]) for i in rng]
    wu = [_dot(r[i], jnp.concatenate([kb[i] * eg[i], v[i] * beta[i]], axis=1)) for i in rng]
    ws = [_dot(jnp.concatenate([wu[i][:, :D], q[i] * eg[i]], axis=0), state[i]) for i in rng]
    v_new = [wu[i][:, D:] - ws[i][:C] for i in rng]
    qkm = [jnp.where(incl[i], qk[i] * decay[i], 0.0) for i in rng]
    out = [ws[i][C:] + _dot(qkm[i], v_new[i]) for i in rng]
    k_dec = [k[i] * jnp.exp(gtot[i] - gc_col[i]) for i in rng]
    new_state = [state[i] * jnp.exp(gtot[i][0:1, :]) + _dot_tn(k_dec[i], v_new[i]) for i in rng]
    return out, new_state


def _gdn_scan_kernel(qf_ref, kf_ref, vf_ref, qb_ref, kb_ref, vb_ref, gf_ref, gb_ref, rf_ref, rb_ref,
                     of_ref, ob_ref, s_scr, *, n_heads, head_dim):
    H, Dh = n_heads, head_dim

    @pl.when(pl.program_id(1) == 0)
    def _():
        s_scr[...] = jnp.zeros_like(s_scr)

    gf = gf_ref[...]
    gb = gb_ref[...]
    col = lambda g, j: g[:, j:j + 1]
    sls = [slice(h * Dh, (h + 1) * Dh) for h in range(H)]
    q = [qf_ref[:, sl] for sl in sls] + [qb_ref[:, sl] for sl in sls]
    k = [kf_ref[:, sl] for sl in sls] + [kb_ref[:, sl] for sl in sls]
    v = [vf_ref[:, sl] for sl in sls] + [vb_ref[:, sl] for sl in sls]
    beta = [col(gf, h) for h in range(H)] + [col(gb, H + h) for h in range(H)]
    gc_col = [col(gf, 2 * H + h) for h in range(H)] + [col(gb, 3 * H + h) for h in range(H)]
    gc_row = [rf_ref[h:h + 1, :] for h in range(H)] + [rb_ref[H + h:H + h + 1, :] for h in range(H)]
    gtot = [col(gf, 6 * H + h) for h in range(H)] + [col(gb, 7 * H + h) for h in range(H)]
    state = [s_scr[0, h] for h in range(H)] + [s_scr[1, h] for h in range(H)]
    out, new_state = _delta_chunks(q, k, v, beta, gc_col, gc_row, gtot, state, [True] * H + [False] * H)
    for h in range(H):
        of_ref[:, sls[h]] = out[h]
        ob_ref[:, sls[h]] = out[H + h]
        s_scr[0, h] = new_state[h]
        s_scr[1, h] = new_state[H + h]


def _gdn_scan(qkv, gates, gates_row, n_heads, head_dim):
    B, L, _ = qkv.shape
    H, Dh = n_heads, head_dim
    d = H * Dh
    C = GDN_CHUNK
    N = L // C
    fwd = lambda col: pl.BlockSpec((None, C, d), lambda b, n: (b, n, col))
    bwd = lambda col: pl.BlockSpec((None, C, d), lambda b, n: (b, N - 1 - n, col))
    return pl.pallas_call(
        functools.partial(_gdn_scan_kernel, n_heads=H, head_dim=Dh),
        grid=(B, N),
        in_specs=[fwd(0), fwd(1), fwd(2), bwd(0), bwd(1), bwd(2),
                  pl.BlockSpec((None, C, LANES), lambda b, n: (b, n, 0)),
                  pl.BlockSpec((None, C, LANES), lambda b, n: (b, N - 1 - n, 0)),
                  pl.BlockSpec((None, None, 2 * H, C), lambda b, n: (b, n, 0, 0)),
                  pl.BlockSpec((None, None, 2 * H, C), lambda b, n: (b, N - 1 - n, 0, 0))],
        out_specs=[pl.BlockSpec((None, C, d), lambda b, n: (b, n, 0)),
                   pl.BlockSpec((None, C, d), lambda b, n: (b, N - 1 - n, 0))],
        out_shape=[jax.ShapeDtypeStruct((B, L, d), F32)] * 2,
        scratch_shapes=[pltpu.VMEM((2, H, Dh, Dh), F32)],
        compiler_params=_cparams(("parallel", "arbitrary"), 32),
        name="gdn_scan",
    )(qkv, qkv, qkv, qkv, qkv, qkv, gates, gates, gates_row, gates_row)


def _gdn_branch(p3, pg, conv_w, a_log_f, a_log_b, dt_bias_f, dt_bias_b, col0, n_heads, head_dim):
    B, L, _ = p3.shape
    H = n_heads
    qkv = _gdn_pre(p3, conv_w, col0, H, head_dim)
    gates = _gdn_gates(pg, a_log_f, a_log_b, dt_bias_f, dt_bias_b, H).reshape(B, L, LANES)
    N = L // GDN_CHUNK
    gates_row = gates[..., 2 * H:4 * H].reshape(B, N, GDN_CHUNK, 2 * H).transpose(0, 1, 3, 2)
    return _gdn_scan(qkv, gates, gates_row, H, head_dim)


@functools.lru_cache(maxsize=None)
def _filter_positions(L, pos_emb_dim):
    n = 2 * L
    r = np.arange(n)
    k = np.where(r < L, r, np.where(r == L, 0, n - r)).astype(np.float64)
    t = k / (L - 1)
    bands = (pos_emb_dim - 1) // 2
    fb = np.linspace(1e-4, bands - 1, bands)
    ang = (2.0 * math.pi / L) * k[:, None] * fb[None, :]
    z = np.concatenate([t[:, None], np.cos(ang), -np.sin(ang)], axis=-1)
    return z.astype(np.float32)


@functools.lru_cache(maxsize=None)
def _decay_rates(d_hy):
    max_decay = math.log(DECAY_TARGET) / FAST_DECAY_PCT
    min_decay = math.log(DECAY_TARGET) / SLOW_DECAY_PCT
    return np.abs(np.linspace(min_decay, max_decay, d_hy)).astype(np.float32).reshape(1, d_hy)


def _filt_mlp_kernel(zt_ref, w1t_ref, b1_ref, w2t_ref, b2_ref, w3t_ref, b3_ref, fr_ref, o_ref):
    fr = fr_ref[...]
    h = jnp.sin(fr * (_dot_hi(w1t_ref[...], zt_ref[...]) + b1_ref[...]))
    h = jnp.sin(fr * (_dot_hi(w2t_ref[...], h) + b2_ref[...]))
    o_ref[...] = jnp.sin(fr * (_dot_hi(w3t_ref[...], h) + b3_ref[...]))


def _filt_mlp(zt, w1, b1, w2, b2, w3, b3, freq, tc=1024):
    pe, n = zt.shape
    fw = w1.shape[1]
    col = lambda a: a.reshape(-1, 1).astype(F32)
    full = lambda a: pl.BlockSpec(a.shape, lambda i: (0, 0))
    args = (zt, w1.T, col(b1), w2.T, col(b2), w3.T, col(b3), col(freq))
    return pl.pallas_call(
        _filt_mlp_kernel,
        grid=(n // tc,),
        in_specs=[pl.BlockSpec((pe, tc), lambda i: (0, i))] + [full(a) for a in args[1:]],
        out_specs=pl.BlockSpec((fw, tc), lambda i: (0, i)),
        out_shape=jax.ShapeDtypeStruct((fw, n), F32),
        compiler_params=_cparams(("parallel",), 24),
        name="filt_mlp",
    )(*args)


@functools.lru_cache(maxsize=None)
def _dft_tables(L):
    n = 2 * L
    N2 = FFT_N2
    N1 = n // N2
    N1h = N1 // 2
    j2 = np.arange(N2)[:, None, None]
    k1 = np.arange(N1)[None, :, None]

    def stage1(n_j1):
        j1 = np.arange(n_j1)[None, None, :]
        m = (k1 * (N2 * j1 + j2)) % n
        th = 2.0 * np.pi * m / n
        return np.cos(th), np.sin(th)

    c, s = stage1(N1h)
    t1 = np.concatenate([np.concatenate([c, s], axis=2), np.concatenate([-s, c], axis=2)], axis=1)
    c, s = stage1(N1)
    t1g = np.concatenate([c, -s], axis=1)
    c, s = stage1(N1h)
    ct, st = np.swapaxes(c, 1, 2) / n, np.swapaxes(s, 1, 2) / n
    t2 = np.concatenate([np.concatenate([ct, -st], axis=2), np.concatenate([st, ct], axis=2)], axis=1)
    a = np.arange(N2)
    th = 2.0 * np.pi * ((a[:, None] * a[None, :]) % N2) / N2
    c2, s2 = np.cos(th), np.sin(th)
    f2f = np.block([[c2, s2], [-s2, c2]])
    f2i = np.block([[c2, -s2], [s2, c2]])
    as_bf16 = lambda x: jnp.asarray(x, dtype=F32).astype(BF16)
    return dict(N1=N1, N2=N2, t1=t1.astype(np.float32), t1g=t1g.astype(np.float32), t2=t2.astype(np.float32),
                f2f=f2f.astype(np.float32), f2i=f2i.astype(np.float32))


FILT_ROWS = 512
FFT_UNROLL = 8
PITCH_PAD = 8


def _filt_fft_kernel(h3_ref, w4f_ref, w4b_ref, delta_ref, t1g_ref, f2f_ref, hspec_ref, hb0_ref, g_scr, a_scr,
                     *, L, N1, N2):
    n = 2 * L
    gp = N2 + PITCH_PAD
    ap = 2 * N1 + PITCH_PAD
    delta = delta_ref[...]
    hb0_ref[...] = jnp.zeros_like(hb0_ref)

    def gen(c, carry):
        r0 = pl.multiple_of(c * FILT_ROWS, FILT_ROWS)
        row = r0 + lax.broadcasted_iota(jnp.int32, (FILT_ROWS, LANES), 0)
        lag = jnp.where(row < L, row, jnp.where(row == L, 0, n - row))
        window = jnp.exp(-(lag.astype(F32) * (1.0 / (L - 1))) * delta)
        w4 = jnp.where(r0 < L, w4f_ref[...], w4b_ref[...])
        g = _dot(h3_ref[pl.ds(r0, FILT_ROWS), :], w4) * window
        at_l = row == L
        hb0_ref[...] += jnp.sum(jnp.where(at_l, g, 0.0), axis=0, keepdims=True)
        g = jnp.where(at_l, 0.0, g)
        for q in range(FILT_ROWS // N2):
            dst = pl.multiple_of((c * (FILT_ROWS // N2) + q) * gp, SUBLANES)
            g_scr[pl.ds(dst, N2), :] = g[q * N2:(q + 1) * N2]
        return carry

    lax.fori_loop(0, n // FILT_ROWS, gen, 0)

    def stage1(j2, carry):
        x = g_scr[pl.ds(j2, N1, stride=gp), :]
        a_scr[pl.ds(pl.multiple_of(j2 * ap, SUBLANES), 2 * N1), :] = _dot(t1g_ref[j2], x)
        return carry

    lax.fori_loop(0, N2, stage1, 0, unroll=FFT_UNROLL)

    def stage2(k1, carry):
        ar = a_scr[pl.ds(k1, N2, stride=ap), :]
        ai = a_scr[pl.ds(N1 + k1, N2, stride=ap), :]
        z = _dot(f2f_ref[...], jnp.concatenate([ar, ai], axis=0))
        hspec_ref[pl.ds(pl.multiple_of(k1 * 2 * N2, 2 * N2), 2 * N2), :] = z.astype(hspec_ref.dtype)
        return carry

    lax.fori_loop(0, N1, stage2, 0, unroll=FFT_UNROLL)


def _filt_fft(h3, w4, L, d_hy):
    tb = _dft_tables(L)
    N1, N2 = tb["N1"], tb["N2"]
    n = 2 * L
    fw = h3.shape[1]
    nt = d_hy // LANES
    t1g = jnp.asarray(tb["t1g"]).astype(BF16)
    f2f = jnp.asarray(tb["f2f"]).astype(BF16)
    return pl.pallas_call(
        functools.partial(_filt_fft_kernel, L=L, N1=N1, N2=N2),
        grid=(nt,),
        in_specs=[pl.BlockSpec((n, fw), lambda c: (0, 0)),
                  pl.BlockSpec((fw, LANES), lambda c: (0, c)),
                  pl.BlockSpec((fw, LANES), lambda c: (0, c + nt)),
                  pl.BlockSpec((1, LANES), lambda c: (0, c)),
                  pl.BlockSpec(t1g.shape, lambda c: (0, 0, 0)),
                  pl.BlockSpec(f2f.shape, lambda c: (0, 0))],
        out_specs=[pl.BlockSpec((2 * n, LANES), lambda c: (0, c)),
                   pl.BlockSpec((SUBLANES, LANES), lambda c: (0, c))],
        out_shape=[jax.ShapeDtypeStruct((2 * n, d_hy), BF16), jax.ShapeDtypeStruct((SUBLANES, d_hy), F32)],
        scratch_shapes=[pltpu.VMEM((N1 * (N2 + PITCH_PAD), LANES), F32),
                        pltpu.VMEM((N2 * (2 * N1 + PITCH_PAD), LANES), F32)],
        compiler_params=_cparams(("parallel",), 48),
        name="filt_fft",
    )(h3, w4, w4, jnp.asarray(_decay_rates(d_hy)), t1g, f2f)


def _hy_conv_kernel(vg_ref, hspec_ref, skip_ref, hb0_ref, t1_ref, f2f_ref, f2i_ref, t2_ref, y_ref,
                    x_scr, a_scr, b_scr, *, N1, N2):
    N1h = N1 // 2
    xp = N2 + PITCH_PAD
    ap = 2 * N1 + PITCH_PAD
    bp = 2 * N2 + PITCH_PAD

    for b in range(2):
        for j1 in range(N1h):
            x_scr[b, pl.ds(j1 * xp, N2), :] = vg_ref[b, pl.ds(j1 * N2, N2), :]

    def stage1(j2, carry):
        x = jnp.concatenate([x_scr[0, pl.ds(j2, N1h, stride=xp), :],
                             x_scr[1, pl.ds(j2, N1h, stride=xp), :]], axis=0)
        a_scr[pl.ds(pl.multiple_of(j2 * ap, SUBLANES), 2 * N1), :] = _dot(t1_ref[j2], x)
        return carry

    lax.fori_loop(0, N2, stage1, 0, unroll=FFT_UNROLL)

    def stage2(k1, carry):
        ar = a_scr[pl.ds(k1, N2, stride=ap), :]
        ai = a_scr[pl.ds(N1 + k1, N2, stride=ap), :]
        z = _dot(f2f_ref[...], jnp.concatenate([ar, ai], axis=0))
        zr, zi = z[:N2], z[N2:]
        base = pl.multiple_of(k1 * 2 * N2, 2 * N2)
        hr = hspec_ref[pl.ds(base, N2), :].astype(F32)
        hi = hspec_ref[pl.ds(base + N2, N2), :].astype(F32)
        prod = jnp.concatenate([zr * hr - zi * hi, zr * hi + zi * hr], axis=0)
        b_scr[pl.ds(pl.multiple_of(k1 * bp, SUBLANES), 2 * N2), :] = _dot(f2i_ref[...], prod)
        return carry

    lax.fori_loop(0, N1, stage2, 0, unroll=FFT_UNROLL)

    skip = skip_ref[...] + hb0_ref[0:1, :]

    def stage3(j2, carry):
        b = jnp.concatenate([b_scr[pl.ds(j2, N1, stride=bp), :],
                             b_scr[pl.ds(N2 + j2, N1, stride=bp), :]], axis=0)
        y = _dot(t2_ref[j2], b)
        x_scr[0, pl.ds(j2, N1h, stride=xp), :] = y[:N1h] + x_scr[0, pl.ds(j2, N1h, stride=xp), :] * skip
        x_scr[1, pl.ds(j2, N1h, stride=xp), :] = y[N1h:] + x_scr[1, pl.ds(j2, N1h, stride=xp), :] * skip
        return carry

    lax.fori_loop(0, N2, stage3, 0, unroll=FFT_UNROLL)

    for b in range(2):
        for j1 in range(N1h):
            y_ref[b, pl.ds(j1 * N2, N2), :] = x_scr[b, pl.ds(j1 * xp, N2), :]


def _hy_conv(vg, hspec, skip, hb0):
    B, L, d_hy = vg.shape
    assert B % 2 == 0
    tb = _dft_tables(L)
    N1, N2 = tb["N1"], tb["N2"]
    n = 2 * L
    nt = d_hy // LANES
    t1, t2 = (jnp.asarray(tb[k]).astype(BF16) for k in ("t1", "t2"))
    f2f, f2i = (jnp.asarray(tb[k]).astype(BF16) for k in ("f2f", "f2i"))
    const3 = lambda a: pl.BlockSpec(a.shape, lambda b, c: (0, 0, 0))
    const2 = lambda a: pl.BlockSpec(a.shape, lambda b, c: (0, 0))
    return pl.pallas_call(
        functools.partial(_hy_conv_kernel, N1=N1, N2=N2),
        grid=(B // 2, nt),
        in_specs=[pl.BlockSpec((2, L, LANES), lambda b, c: (b, 0, c)),
                  pl.BlockSpec((2 * n, LANES), lambda b, c: (0, c)),
                  pl.BlockSpec((1, LANES), lambda b, c: (0, c)),
                  pl.BlockSpec((SUBLANES, LANES), lambda b, c: (0, c)),
                  const3(t1), const2(f2f), const2(f2i), const3(t2)],
        out_specs=pl.BlockSpec((2, L, LANES), lambda b, c: (b, 0, c)),
        out_shape=jax.ShapeDtypeStruct((B, L, d_hy), F32),
        scratch_shapes=[pltpu.VMEM((2, (N1 // 2) * (N2 + PITCH_PAD), LANES), F32),
                        pltpu.VMEM((N2 * (2 * N1 + PITCH_PAD), LANES), F32),
                        pltpu.VMEM((N1 * (2 * N2 + PITCH_PAD), LANES), F32)],
        compiler_params=_cparams(("parallel", "parallel"), 58),
        name="hy_conv",
    )(vg, hspec, skip.reshape(1, d_hy).astype(F32), hb0, t1, f2f, f2i, t2)


def _hyena_branch(p3, conv_w, conv_b, fw1, fb1, fw2, fb2, fw3, fb3, fw4, freq, skip, d_hy):
    B, L, _ = p3.shape
    x0c, vg = _hy_pre(p3, conv_w, conv_b, d_hy)
    zt = jnp.asarray(_filter_positions(L, fw1.shape[0]).T)
    h3 = _filt_mlp(zt, fw1, fb1, fw2, fb2, fw3, fb3, freq).T
    hspec, hb0 = _filt_fft(h3, fw4, L, d_hy)
    return x0c, _hy_conv(vg, hspec, skip, hb0)


ROUTE_GATE, ROUTE_EXPERT, ROUTE_RANK = 0, 2, 4
MIX_ROWS = 256


def _mix_route_kernel(yc_ref, x0_ref, of_ref, ob_ref, z_ref, hnw_ref, gnw_ref, x_ref, wo_ref, n2w_ref, wr_ref, br_ref,
                      x2_ref, u_ref, route_ref, route_t_ref, cnt_ref, run_scr, *, n_heads, head_dim, n_groups,
                      per_group):
    tm = x_ref.shape[0]
    G, P = n_groups, per_group

    @pl.when(pl.program_id(0) == 0)
    def _():
        run_scr[...] = jnp.zeros_like(run_scr)

    run = run_scr[0:1, :]
    blocks = [pl.ds(s * MIX_ROWS, MIX_ROWS) for s in range(tm // MIX_ROWS)]
    us = []
    for rs in blocks:
        yh = yc_ref[rs, :] * x0_ref[rs, :]
        yh = yh * lax.rsqrt(jnp.mean(yh * yh, axis=-1, keepdims=True) + EPS) * hnw_ref[...]
        parts = [yh.astype(BF16)]
        for h in range(n_heads):
            sl = slice(h * head_dim, (h + 1) * head_dim)
            o = of_ref[rs, sl] + ob_ref[rs, sl]
            o = o * lax.rsqrt(jnp.mean(o * o, axis=-1, keepdims=True) + EPS) * gnw_ref[...] * _silu(z_ref[rs, sl])
            parts.append(o.astype(BF16))
        ymix = jnp.concatenate(parts, axis=-1)
        x2 = x_ref[rs, :] + jnp.dot(ymix, wo_ref[...], preferred_element_type=F32)
        x2_ref[rs, :] = x2
        u = x2 * lax.rsqrt(jnp.mean(x2 * x2, axis=-1, keepdims=True) + EPS) * n2w_ref[...]
        u_ref[rs, :] = _pack_bf16_pairs(u)
        us.append(u)

    all_logits = [_dot(u, wr_ref[...]) + br_ref[...] for u in us]
    for s, (rs, logits) in enumerate(zip(blocks, all_logits)):
        lane = lax.broadcasted_iota(jnp.int32, logits.shape, 1)
        neg = jnp.float32(-jnp.inf)
        big = jnp.int32(4 * LANES)
        first = lambda hit: jnp.min(jnp.where(hit, lane, big), axis=-1, keepdims=True)
        gl = jnp.where(lane < G, logits, neg)
        gmax = jnp.max(gl, axis=-1, keepdims=True)
        gidx = first(gl == gmax)
        grp_gate = 1.0 / jnp.sum(jnp.exp(gl - gmax), axis=-1, keepdims=True)
        in_grp = (lane >= G) & (lane < G + G * P) & (((lane - G) // P) == gidx)
        ll = jnp.where(in_grp, logits, neg)
        m1 = jnp.max(ll, axis=-1, keepdims=True)
        i1 = first(ll == m1)
        denom = jnp.sum(jnp.exp(ll - m1), axis=-1, keepdims=True)
        ll2 = jnp.where(lane == i1, neg, ll)
        m2 = jnp.max(ll2, axis=-1, keepdims=True)
        i2 = first(ll2 == m2)
        p1 = 1.0 / denom
        p2 = jnp.exp(m2 - m1) / denom
        gate1 = grp_gate * (p1 / (p1 + p2))
        gate2 = grp_gate * (p2 / (p1 + p2))
        e1 = i1 - G
        e2 = i2 - G

        oh1 = jnp.where(lane == e1, 1.0, 0.0)
        oh2 = jnp.where(lane == e2, 1.0, 0.0)
        oh = oh1 + oh2
        ii = lax.broadcasted_iota(jnp.int32, (MIX_ROWS, MIX_ROWS), 0)
        jj = lax.broadcasted_iota(jnp.int32, (MIX_ROWS, MIX_ROWS), 1)
        before = _dot(jnp.where(ii > jj, 1.0, 0.0), oh) + run
        r1 = jnp.sum(oh1 * before, axis=-1, keepdims=True)
        r2 = jnp.sum(oh2 * before, axis=-1, keepdims=True)
        run = run + jnp.sum(oh, axis=0, keepdims=True)

        rec = jnp.where(lane == ROUTE_GATE, gate1, 0.0)
        rec = jnp.where(lane == ROUTE_GATE + 1, gate2, rec)
        rec = jnp.where(lane == ROUTE_EXPERT, e1.astype(F32), rec)
        rec = jnp.where(lane == ROUTE_EXPERT + 1, e2.astype(F32), rec)
        rec = jnp.where(lane == ROUTE_RANK, r1, rec)
        rec = jnp.where(lane == ROUTE_RANK + 1, r2, rec)
        route_ref[rs, :] = rec
        route_t_ref[:, s * MIX_ROWS:(s + 1) * MIX_ROWS] = jnp.transpose(rec)[:SUBLANES, :]

    run_scr[...] = jnp.broadcast_to(run, run_scr.shape)
    cnt_ref[...] = run_scr[...]


def _mix_route(yconv, x0c, o_f, o_b, p, z_col, hy_norm_w, gdn_norm_w, xf, w_out_bf16, norm2_w, wr, br,
               n_heads, head_dim, n_groups, per_group, tm=512):
    M, D = xf.shape
    d_hy = yconv.shape[1]
    d_gdn = o_f.shape[1]
    assert z_col % d_gdn == 0 and n_groups * (per_group + 1) <= LANES
    zb = z_col // d_gdn
    row = lambda i: (i, 0)
    const = lambda i: (0, 0)
    kern = functools.partial(_mix_route_kernel, n_heads=n_heads, head_dim=head_dim,
                             n_groups=n_groups, per_group=per_group)
    return pl.pallas_call(
        kern,
        grid=(M // tm,),
        in_specs=[pl.BlockSpec((tm, d_hy), row), pl.BlockSpec((tm, d_hy), row),
                  pl.BlockSpec((tm, d_gdn), row), pl.BlockSpec((tm, d_gdn), row),
                  pl.BlockSpec((tm, d_gdn), lambda i: (i, zb)),
                  pl.BlockSpec((1, d_hy), const), pl.BlockSpec((1, head_dim), const),
                  pl.BlockSpec((tm, D), row), pl.BlockSpec(w_out_bf16.shape, const, pipeline_mode=pl.Buffered(1)),
                  pl.BlockSpec((1, D), const), pl.BlockSpec((D, LANES), const), pl.BlockSpec((1, LANES), const)],
        out_specs=[pl.BlockSpec((tm, D), row), pl.BlockSpec((tm, D // 2), row),
                   pl.BlockSpec((tm, LANES), row), pl.BlockSpec((SUBLANES, tm), lambda i: (0, i)),
                   pl.BlockSpec((SUBLANES, LANES), const)],
        out_shape=[jax.ShapeDtypeStruct((M, D), F32), jax.ShapeDtypeStruct((M, D // 2), jnp.uint32),
                   jax.ShapeDtypeStruct((M, LANES), F32), jax.ShapeDtypeStruct((SUBLANES, M), F32),
                   jax.ShapeDtypeStruct((SUBLANES, LANES), F32)],
        scratch_shapes=[pltpu.VMEM((SUBLANES, LANES), F32)],
        compiler_params=_cparams(("arbitrary",), 58),
        name="mix_route",
    )(yconv, x0c, o_f, o_b, p, hy_norm_w.reshape(1, d_hy), gdn_norm_w.reshape(1, head_dim), xf, w_out_bf16,
      norm2_w.reshape(1, D), wr, br)


def _experts_kernel(te_ref, na_ref, tgt_nxt_ref, tgt_ref, tgt_prv_ref, u_hbm, w1_ref, w3_ref, w2_ref, out_hbm,
                    x0, x1, y0, y1, w1b, w3b, w2b, gsem, ssem, *, n_tokens):
    i = pl.program_id(0)
    na = na_ref[0]
    TB = x0.shape[0]
    spare0 = 2 * n_tokens

    def token_of(v):
        if n_tokens & (n_tokens - 1) == 0:
            return v & (n_tokens - 1)
        return lax.rem(v, n_tokens)

    def gather(tgt, xbuf, s, r):
        return pltpu.make_async_copy(u_hbm.at[pl.ds(token_of(tgt[r]), 1)], xbuf.at[pl.ds(r, 1)], gsem.at[s])

    def scatter(ybuf, s, r, dst):
        return pltpu.make_async_copy(ybuf.at[pl.ds(r, 1)], out_hbm.at[pl.ds(dst, 1)], ssem.at[s])

    def gather_wait(xbuf, s):
        pltpu.make_async_copy(u_hbm.at[pl.ds(0, TB)], xbuf, gsem.at[s]).wait()

    def scatter_wait(ybuf, s):
        pltpu.make_async_copy(ybuf, out_hbm.at[pl.ds(0, TB)], ssem.at[s]).wait()

    def each_row(fn):
        def body(r, carry):
            fn(r)
            return carry
        lax.fori_loop(0, TB, body, 0, unroll=8)

    def step(p):
        q = 1 - p
        x_cur, x_nxt = (x0, x1) if p == 0 else (x1, x0)
        y_cur, y_prv = (y0, y1) if p == 0 else (y1, y0)
        gather_wait(x_cur, p)

        @pl.when(i > 0)
        def _():
            scatter_wait(y_cur, p)

        x = _unpack_bf16_pairs(x_cur[...]).astype(BF16)
        for r in range(TB):
            gather(tgt_nxt_ref, x_nxt, q, r).start()
        first = i == 0
        for r in range(TB):
            scatter(y_prv, q, r, jnp.where(first, spare0 + r, tgt_prv_ref[r])).start()
        h = _silu(jnp.dot(x, w1b[...], preferred_element_type=F32)) * jnp.dot(x, w3b[...], preferred_element_type=F32)
        y_cur[...] = _pack_bf16_pairs(jnp.dot(h.astype(BF16), w2b[...], preferred_element_type=F32))

        @pl.when(i == na - 1)
        def _():
            gather_wait(x_nxt, q)
            scatter_wait(y_prv, q)
            each_row(lambda r: scatter(y_cur, p, r, tgt_ref[r]).start())
            scatter_wait(y_cur, p)

    @pl.when(i < na)
    def _():
        @pl.when(i == 0)
        def _():
            y0[...] = jnp.zeros_like(y0)
            y1[...] = jnp.zeros_like(y1)
            fill = pltpu.make_async_copy(y1, out_hbm.at[pl.ds(spare0, TB)], ssem.at[1])
            fill.start()
            fill.wait()
            each_row(lambda r: gather(tgt_ref, x0, 0, r).start())

        @pl.when((i == 0) | (te_ref[i] != te_ref[jnp.maximum(i - 1, 0)]))
        def _():
            w1b[...] = w1_ref[...].astype(BF16)
            w3b[...] = w3_ref[...].astype(BF16)
            w2b[...] = w2_ref[...].astype(BF16)

        parity = lax.rem(i, 2)

        @pl.when(parity == 0)
        def _():
            step(0)

        @pl.when(parity == 1)
        def _():
            step(1)


def _experts(u, slot_tgt, tile_expert, n_active, w1, w3, w2):
    T = u.shape[0]
    E, D, de = w1.shape
    TB = EXPERT_ROWS
    n_tiles = slot_tgt.shape[0] // TB
    wmap = lambda i, te, na: (te[i], 0, 0)
    table = lambda f: pl.BlockSpec((TB,), lambda i, te, na: (f(i),), memory_space=pltpu.SMEM)
    grid_spec = pltpu.PrefetchScalarGridSpec(
        num_scalar_prefetch=2,
        grid=(n_tiles,),
        in_specs=[table(lambda i: jnp.minimum(i + 1, n_tiles - 1)), table(lambda i: i),
                  table(lambda i: jnp.maximum(i - 1, 0)),
                  pl.BlockSpec(memory_space=pl.ANY),
                  pl.BlockSpec((None, D, de), wmap), pl.BlockSpec((None, D, de), wmap),
                  pl.BlockSpec((None, de, D), wmap)],
        out_specs=pl.BlockSpec(memory_space=pl.ANY),
        scratch_shapes=[pltpu.VMEM((TB, D // 2), jnp.uint32)] * 4
                       + [pltpu.VMEM((D, de), BF16), pltpu.VMEM((D, de), BF16), pltpu.VMEM((de, D), BF16),
                          pltpu.SemaphoreType.DMA((2,)), pltpu.SemaphoreType.DMA((2,))],
    )
    return pl.pallas_call(
        functools.partial(_experts_kernel, n_tokens=T),
        grid_spec=grid_spec,
        out_shape=jax.ShapeDtypeStruct((2 * T + TB, D // 2), jnp.uint32),
        compiler_params=_cparams(("arbitrary",), 52),
        name="experts",
    )(tile_expert, n_active, slot_tgt, slot_tgt, slot_tgt, u, w1, w3, w2)


def _slot_table_kernel(dest_ref, init_hbm, o_ref, sem):
    fill = pltpu.make_async_copy(init_hbm, o_ref, sem)
    fill.start()
    fill.wait()

    def put(a, carry):
        o_ref[dest_ref[a]] = a
        return carry

    lax.fori_loop(0, dest_ref.shape[0], put, 0, unroll=8)


def _slot_table(dest, init):
    smem = pl.BlockSpec(memory_space=pltpu.SMEM)
    return pl.pallas_call(
        _slot_table_kernel,
        in_specs=[smem, pl.BlockSpec(memory_space=pl.ANY)],
        out_specs=smem,
        out_shape=jax.ShapeDtypeStruct(init.shape, jnp.int32),
        scratch_shapes=[pltpu.SemaphoreType.DMA],
        name="slot_table",
    )(dest, init)


def _dispatch_tables(route_t, counts, n_experts):
    T = route_t.shape[1]
    TB = EXPERT_ROWS
    e = route_t[ROUTE_EXPERT:ROUTE_EXPERT + 2].astype(jnp.int32)
    rank = route_t[ROUTE_RANK:ROUTE_RANK + 2].astype(jnp.int32)
    cnt = counts[0, :n_experts].astype(jnp.int32)
    padded = (cnt + TB - 1) // TB * TB
    pad_end = jnp.cumsum(padded)
    pad_start = pad_end - padded
    ids = jnp.arange(n_experts, dtype=jnp.int32)[:, None, None]
    start_of = jnp.sum(jnp.where(e[None] == ids, pad_start[:, None, None], 0), axis=0)
    dest = start_of + rank
    n_tiles = -(-(2 * T + n_experts * (TB - 1)) // TB)
    padded_slot = 2 * T + jnp.arange(n_tiles * TB, dtype=jnp.int32) % TB
    slot_tgt = _slot_table(dest.reshape(-1), padded_slot)
    start = jnp.arange(n_tiles, dtype=jnp.int32) * TB
    tile_expert = jnp.sum(start[:, None] >= pad_end[None, :], axis=1)
    tile_expert = jnp.minimum(tile_expert, n_experts - 1).astype(jnp.int32)
    n_active = (pad_end[-1] // TB).astype(jnp.int32).reshape(1)
    return slot_tgt, tile_expert, n_active


def _combine_kernel(x2_ref, e0_ref, e1_ref, route_ref, w_ref, o_ref, *, final_norm):
    r = route_ref[...]
    e0 = _unpack_bf16_pairs(e0_ref[...])
    e1 = _unpack_bf16_pairs(e1_ref[...])
    y = x2_ref[...] + r[:, ROUTE_GATE:ROUTE_GATE + 1] * e0 + r[:, ROUTE_GATE + 1:ROUTE_GATE + 2] * e1
    if final_norm:
        y = y * lax.rsqrt(jnp.mean(y * y, axis=-1, keepdims=True) + EPS) * w_ref[...]
    o_ref[...] = y


def _combine(x2, planes, route, norm_w, final_norm, tm=512):
    M, D = x2.shape
    return pl.pallas_call(
        functools.partial(_combine_kernel, final_norm=final_norm),
        grid=(M // tm,),
        in_specs=[pl.BlockSpec((tm, D), lambda i: (i, 0)),
                  pl.BlockSpec((tm, D // 2), lambda i: (i, 0)),
                  pl.BlockSpec((tm, D // 2), lambda i: (M // tm + i, 0)),
                  pl.BlockSpec((tm, LANES), lambda i: (i, 0)),
                  pl.BlockSpec((1, D), lambda i: (0, 0))],
        out_specs=pl.BlockSpec((tm, D), lambda i: (i, 0)),
        out_shape=jax.ShapeDtypeStruct((M, D), F32),
        compiler_params=_cparams(("parallel",), 48),
        name="combine",
    )(x2, planes, planes, route, norm_w.reshape(1, D))


def kernel(x, norm1_w, w_in, hy_conv_w, hy_conv_b, hy_filt_w1, hy_filt_b1, hy_filt_w2, hy_filt_b2, hy_filt_w3, hy_filt_b3, hy_filt_w4, hy_sin_freq, hy_skip, hy_norm_w, gdn_conv_w, gdn_a_log_f, gdn_a_log_b, gdn_dt_bias_f, gdn_dt_bias_b, gdn_norm_w, w_out, norm2_w, router_group_w, router_group_b, router_expert_w, router_expert_b, exp_w1, exp_w3, exp_w2, final_norm_w):
    B, L, D = x.shape
    M = B * L
    depth = w_in.shape[0]
    d_hy = hy_skip.shape[-1]
    H = gdn_a_log_f.shape[-1]
    Dh = gdn_norm_w.shape[-1]
    d_gdn = H * Dh
    n_main = 3 * d_hy + 4 * d_gdn
    G = router_group_w.shape[-1]
    E = router_expert_w.shape[-1]
    xf = x.reshape(M, D)
    for l in range(depth):
        p, pg = _inproj(xf, norm1_w[l], jnp.swapaxes(w_in[l], 0, 1), n_main, 4 * H)
        p3 = p.reshape(B, L, n_main)
        x0c, yconv = _hyena_branch(p3, hy_conv_w[l], hy_conv_b[l], hy_filt_w1[l], hy_filt_b1[l], hy_filt_w2[l],
                                   hy_filt_b2[l], hy_filt_w3[l], hy_filt_b3[l], hy_filt_w4[l], hy_sin_freq[l],
                                   hy_skip[l], d_hy)
        o_f, o_b = _gdn_branch(p3, pg, gdn_conv_w[l], gdn_a_log_f[l], gdn_a_log_b[l], gdn_dt_bias_f[l],
                               gdn_dt_bias_b[l], 3 * d_hy, H, Dh)
        wr = jnp.pad(jnp.concatenate([router_group_w[l], router_expert_w[l]], axis=1), ((0, 0), (0, LANES - G - E)))
        br = jnp.pad(jnp.concatenate([router_group_b[l], router_expert_b[l]]), (0, LANES - G - E)).reshape(1, LANES)
        x2, u, route, route_t, counts = _mix_route(
            yconv.reshape(M, d_hy), x0c.reshape(M, d_hy), o_f.reshape(M, d_gdn), o_b.reshape(M, d_gdn), p,
            3 * d_hy + 3 * d_gdn, hy_norm_w[l], gdn_norm_w[l], xf, w_out[l].astype(BF16), norm2_w[l], wr, br,
            H, Dh, G, E // G)
        slot_tgt, tile_expert, n_active = _dispatch_tables(route_t, counts, E)
        planes = _experts(u, slot_tgt, tile_expert, n_active, exp_w1[l], exp_w3[l], exp_w2[l])
        xf = _combine(x2, planes, route, final_norm_w, final_norm=(l == depth - 1))
    return xf.reshape(B, L, D)
```

```python
import functools
import math

import jax
import jax.numpy as jnp
import numpy as np
from jax import lax
from jax.experimental import pallas as pl
from jax.experimental.pallas import tpu as pltpu

F32 = jnp.float32
BF16 = jnp.bfloat16
EPS = 1e-6
LANES = 128
SUBLANES = 8
VMEM_BYTES_V7X = 64 * 1024 * 1024
GDN_CHUNK = 64
FFT_N2 = 128
EXPERT_ROWS = 256
DECAY_TARGET = 1e-2
FAST_DECAY_PCT = 0.3
SLOW_DECAY_PCT = 1.5


def _cparams(sem, vmem_mb):
    return pltpu.CompilerParams(dimension_semantics=sem, vmem_limit_bytes=int(vmem_mb * 1024 * 1024))


def _dot(a, b):
    return jnp.dot(a.astype(BF16), b.astype(BF16), preferred_element_type=F32)


def _dot_nt(a, b):
    return lax.dot_general(a.astype(BF16), b.astype(BF16), (((1,), (1,)), ((), ())), preferred_element_type=F32)


def _dot_tn(a, b):
    return lax.dot_general(a.astype(BF16), b.astype(BF16), (((0,), (0,)), ((), ())), preferred_element_type=F32)


def _dot_hi(a, b):
    return jnp.dot(a, b, preferred_element_type=F32, precision=lax.Precision.HIGHEST)


def _silu(x):
    return x * jax.nn.sigmoid(x)


def _pack_bf16_pairs(x):
    c = x.shape[1] // 2
    lo = lax.bitcast_convert_type(x[:, :c].astype(BF16).astype(F32), jnp.uint32) >> 16
    hi = lax.bitcast_convert_type(x[:, c:].astype(BF16).astype(F32), jnp.uint32) & jnp.uint32(0xFFFF0000)
    return hi | lo


def _unpack_bf16_pairs(w):
    lo = lax.bitcast_convert_type(w << 16, F32)
    hi = lax.bitcast_convert_type(w & jnp.uint32(0xFFFF0000), F32)
    return jnp.concatenate([lo, hi], axis=1)


def _inproj_kernel(x_ref, nw_ref, wt_ref, wgt_ref, p_ref, g_ref, h_scr):
    @pl.when(pl.program_id(1) == 0)
    def _():
        x = x_ref[...]
        h = x * lax.rsqrt(jnp.mean(x * x, axis=-1, keepdims=True) + EPS) * nw_ref[...]
        h_scr[...] = h.astype(BF16)
        g = _dot_nt(h, wgt_ref[...])
        g_ref[...] = jnp.concatenate([g, jnp.zeros((g.shape[0], LANES - g.shape[1]), F32)], axis=1)

    p_ref[...] = _dot_nt(h_scr[...], wt_ref[...]).astype(p_ref.dtype)


def _inproj(xf, norm_w, wt, n_main, n_gate, tm=2048, tn=512):
    M, D = xf.shape
    assert n_main % tn == 0 and M % tm == 0 and n_main % n_gate == 0 and n_gate % SUBLANES == 0
    return pl.pallas_call(
        _inproj_kernel,
        grid=(M // tm, n_main // tn),
        in_specs=[
            pl.BlockSpec((tm, D), lambda i, j: (i, 0), pipeline_mode=pl.Buffered(1)),
            pl.BlockSpec((1, D), lambda i, j: (0, 0)),
            pl.BlockSpec((tn, D), lambda i, j: (j, 0)),
            pl.BlockSpec((n_gate, D), lambda i, j: (n_main // n_gate, 0)),
        ],
        out_specs=[
            pl.BlockSpec((tm, tn), lambda i, j: (i, j)),
            pl.BlockSpec((tm, LANES), lambda i, j: (i, 0)),
        ],
        out_shape=[jax.ShapeDtypeStruct((M, n_main), BF16), jax.ShapeDtypeStruct((M, LANES), F32)],
        scratch_shapes=[pltpu.VMEM((tm, D), BF16)],
        compiler_params=_cparams(("parallel", "arbitrary"), 57),
        name="inproj",
    )(xf, norm_w.reshape(1, D), wt, wt)


def _conv3_rows(ref, r0, rows, w, n_rows):
    cur = ref[pl.ds(r0, rows), :].astype(F32)
    lo = jnp.maximum(r0 - HALO_ROWS, 0)
    hi = jnp.minimum(r0 + rows, n_rows - HALO_ROWS)
    prev_grp = ref[pl.ds(pl.multiple_of(lo, HALO_ROWS), HALO_ROWS), :].astype(F32)
    next_grp = ref[pl.ds(pl.multiple_of(hi, HALO_ROWS), HALO_ROWS), :].astype(F32)
    prev_row = jnp.where(r0 > 0, prev_grp[HALO_ROWS - 1:HALO_ROWS, :], 0.0)
    next_row = jnp.where(r0 + rows < n_rows, next_grp[0:1, :], 0.0)
    row = lax.broadcasted_iota(jnp.int32, cur.shape, 0)
    xm = jnp.where(row == 0, prev_row, pltpu.roll(cur, 1, 0))
    xp = jnp.where(row == rows - 1, next_row, pltpu.roll(cur, rows - 1, 0))
    return xm * w[0:1, :] + cur * w[1:2, :] + xp * w[2:3, :]


CONV_ROWS = 256
HALO_ROWS = 16


def _hy_pre_kernel(x0_ref, x1_ref, v_ref, w0_ref, w1_ref, w2_ref, b0_ref, b1_ref, b2_ref, x0c_ref, vg_ref):
    L = x0_ref.shape[0]
    w0, w1, w2 = w0_ref[...], w1_ref[...], w2_ref[...]
    b0, b1, b2 = b0_ref[...], b1_ref[...], b2_ref[...]

    def body(c, carry):
        r0 = pl.multiple_of(c * CONV_ROWS, CONV_ROWS)
        x0c_ref[pl.ds(r0, CONV_ROWS), :] = (_conv3_rows(x0_ref, r0, CONV_ROWS, w0, L) + b0).astype(x0c_ref.dtype)
        x1c = _conv3_rows(x1_ref, r0, CONV_ROWS, w1, L) + b1
        vc = _conv3_rows(v_ref, r0, CONV_ROWS, w2, L) + b2
        vg_ref[pl.ds(r0, CONV_ROWS), :] = (vc * x1c).astype(vg_ref.dtype)
        return carry

    lax.fori_loop(0, L // CONV_ROWS, body, 0, unroll=2)


def _hy_pre(p3, conv_w, conv_b, d_hy):
    B, L, _ = p3.shape
    nt = d_hy // LANES
    bias = conv_b.reshape(1, -1)
    pspec = lambda off: pl.BlockSpec((None, L, LANES), lambda b, c: (b, 0, c + off))
    wspec = lambda off: pl.BlockSpec((3, LANES), lambda b, c: (0, c + off))
    bspec = lambda off: pl.BlockSpec((1, LANES), lambda b, c: (0, c + off))
    ospec = pl.BlockSpec((None, L, LANES), lambda b, c: (b, 0, c))
    return pl.pallas_call(
        _hy_pre_kernel,
        grid=(B, nt),
        in_specs=[pspec(0), pspec(nt), pspec(2 * nt), wspec(0), wspec(nt), wspec(2 * nt),
                  bspec(0), bspec(nt), bspec(2 * nt)],
        out_specs=[ospec, ospec],
        out_shape=[jax.ShapeDtypeStruct((B, L, d_hy), BF16)] * 2,
        compiler_params=_cparams(("parallel", "parallel"), 40),
        name="hy_pre",
    )(p3, p3, p3, conv_w, conv_w, conv_w, bias, bias, bias)


def _gdn_pre_kernel(x_ref, w_ref, o_ref, *, n_heads, head_dim):
    L = x_ref.shape[0]
    w = w_ref[...]
    c = pl.program_id(1)
    q_scale = jnp.where(c < n_heads, head_dim ** -0.5, 1.0)
    is_qk = c < 2 * n_heads

    def body(i, carry):
        r0 = pl.multiple_of(i * CONV_ROWS, CONV_ROWS)
        y = _silu(_conv3_rows(x_ref, r0, CONV_ROWS, w, L))
        inv = lax.rsqrt(jnp.sum(y * y, axis=-1, keepdims=True) + EPS) * q_scale
        o_ref[pl.ds(r0, CONV_ROWS), :] = y * jnp.where(is_qk, inv, 1.0)
        return carry

    lax.fori_loop(0, L // CONV_ROWS, body, 0, unroll=2)


def _gdn_pre(p3, conv_w, col0, n_heads, head_dim):
    B, L, _ = p3.shape
    assert head_dim == LANES
    nt = 3 * n_heads
    off = col0 // LANES
    return pl.pallas_call(
        functools.partial(_gdn_pre_kernel, n_heads=n_heads, head_dim=head_dim),
        grid=(B, nt),
        in_specs=[pl.BlockSpec((None, L, LANES), lambda b, c: (b, 0, c + off)),
                  pl.BlockSpec((3, LANES), lambda b, c: (0, c))],
        out_specs=pl.BlockSpec((None, L, LANES), lambda b, c: (b, 0, c)),
        out_shape=jax.ShapeDtypeStruct((B, L, nt * LANES), F32),
        compiler_params=_cparams(("parallel", "parallel"), 24),
        name="gdn_pre",
    )(p3, conv_w)


GATE_ROWS = 512


def _gdn_gates_kernel(pg_ref, alog_ref, dtb_ref, o_ref, *, n_heads):
    H = n_heads
    x = pg_ref[...]
    beta = jax.nn.sigmoid(x)
    z = x + dtb_ref[...]
    softplus = jnp.maximum(z, 0.0) + jnp.log1p(jnp.exp(-jnp.abs(z)))
    g = -jnp.exp(alog_ref[...]) * softplus
    pos = lax.broadcasted_iota(jnp.int32, x.shape, 0) & (GDN_CHUNK - 1)
    gc_f = g
    gc_b = g
    step = 1
    while step < GDN_CHUNK:
        gc_f = gc_f + jnp.where(pos >= step, pltpu.roll(gc_f, step, 0), 0.0)
        gc_b = gc_b + jnp.where(pos < GDN_CHUNK - step, pltpu.roll(gc_b, GATE_ROWS - step, 0), 0.0)
        step *= 2
    g_tot = pltpu.roll(gc_f + gc_b - g, 4 * H, 1)
    lane = lax.broadcasted_iota(jnp.int32, x.shape, 1)
    out = jnp.where(lane < 2 * H, beta,
                    jnp.where(lane < 3 * H, gc_f,
                              jnp.where(lane < 4 * H, gc_b,
                                        jnp.where((lane >= 6 * H) & (lane < 8 * H), g_tot, 0.0))))
    o_ref[...] = out


def _gdn_gates(pg, a_log_f, a_log_b, dt_bias_f, dt_bias_b, n_heads):
    M = pg.shape[0]
    H = n_heads
    assert 8 * H <= LANES
    pad = lambda a, b: jnp.concatenate([jnp.zeros((2 * H,), F32), a.astype(F32), b.astype(F32),
                                        jnp.zeros((LANES - 4 * H,), F32)]).reshape(1, LANES)
    return pl.pallas_call(
        functools.partial(_gdn_gates_kernel, n_heads=H),
        grid=(M // GATE_ROWS,),
        in_specs=[pl.BlockSpec((GATE_ROWS, LANES), lambda i: (i, 0)),
                  pl.BlockSpec((1, LANES), lambda i: (0, 0)),
                  pl.BlockSpec((1, LANES), lambda i: (0, 0))],
        out_specs=pl.BlockSpec((GATE_ROWS, LANES), lambda i: (i, 0)),
        out_shape=jax.ShapeDtypeStruct((M, LANES), F32),
        compiler_params=_cparams(("parallel",), 24),
        name="gdn_gates",
    )(pg, pad(a_log_f, a_log_b), pad(dt_bias_f, dt_bias_b))


def _delta_chunks(q, k, v, beta, gc_col, gc_row, gtot, state, lower):
    n = len(q)
    C = q[0].shape[0]
    D = k[0].shape[1]
    ii = lax.broadcasted_iota(jnp.int32, (C, C), 0)
    jj = lax.broadcasted_iota(jnp.int32, (C, C), 1)
    eye = jnp.where(ii == jj, 1.0, 0.0)
    incl = [(ii >= jj) if lo else (ii <= jj) for lo in lower]
    strict = [(ii > jj) if lo else (ii < jj) for lo in lower]
    rng = range(n)
    decay = [jnp.where(incl[i], jnp.exp(jnp.where(incl[i], gc_col[i] - gc_row[i], 0.0)), 0.0) for i in rng]
    kb = [k[i] * beta[i] for i in rng]
    kk = [_dot_nt(kb[i], k[i]) for i in rng]
    qk = [_dot_nt(q[i], k[i]) for i in rng]
    m = [jnp.where(strict[i], -(kk[i] * decay[i]), 0.0) for i in rng]
    r = [eye + m[i] for i in rng]
    m = [_dot(m[i], m[i]) for i in rng]
    for _ in range(int(math.log2(C)) - 2):
        rm = [_dot(jnp.concatenate([r[i], m[i]], axis=0), m[i]) for i in rng]
        r = [r[i] + rm[i][:C] for i in rng]
        m = [rm[i][C:] for i in rng]
    r = [r[i] + _dot(r[i], m[i]) for i in rng]
    eg = [jnp.exp(gc_col[i]) for i in rng]
    wu = [_dot(r[i], jnp.concatenate([kb[i] * eg[i], v[i] * beta[i]], axis=1)) for i in rng]
    ws = [_dot(jnp.concatenate([wu[i][:, :D], q[i] * eg[i]], axis=0), state[i]) for i in rng]
    v_new = [wu[i][:, D:] - ws[i][:C] for i in rng]
    qkm = [jnp.where(incl[i], qk[i] * decay[i], 0.0) for i in rng]
    out = [ws[i][C:] + _dot(qkm[i], v_new[i]) for i in rng]
    k_dec = [k[i] * jnp.exp(gtot[i] - gc_col[i]) for i in rng]
    new_state = [state[i] * jnp.exp(gtot[i][0:1, :]) + _dot_tn(k_dec[i], v_new[i]) for i in rng]
    return out, new_state


def _gdn_scan_kernel(qf_ref, kf_ref, vf_ref, qb_ref, kb_ref, vb_ref, gf_ref, gb_ref, rf_ref, rb_ref,
                     of_ref, ob_ref, s_scr, *, n_heads, head_dim):
    H, Dh = n_heads, head_dim
    B = qf_ref.shape[0]

    @pl.when(pl.program_id(0) == 0)
    def _():
        s_scr[...] = jnp.zeros_like(s_scr)

    col = lambda g, j: g[:, j:j + 1]
    sls = [slice(h * Dh, (h + 1) * Dh) for h in range(H)]
    q, k, v, beta, gc_col, gc_row, gtot, state = ([] for _ in range(8))
    for b in range(B):
        gf = gf_ref[b]
        gb = gb_ref[b]
        q += [qf_ref[b, :, sl] for sl in sls] + [qb_ref[b, :, sl] for sl in sls]
        k += [kf_ref[b, :, sl] for sl in sls] + [kb_ref[b, :, sl] for sl in sls]
        v += [vf_ref[b, :, sl] for sl in sls] + [vb_ref[b, :, sl] for sl in sls]
        beta += [col(gf, h) for h in range(H)] + [col(gb, H + h) for h in range(H)]
        gc_col += [col(gf, 2 * H + h) for h in range(H)] + [col(gb, 3 * H + h) for h in range(H)]
        gc_row += [rf_ref[b, h:h + 1, :] for h in range(H)] + [rb_ref[b, H + h:H + h + 1, :] for h in range(H)]
        gtot += [col(gf, 6 * H + h) for h in range(H)] + [col(gb, 7 * H + h) for h in range(H)]
        state += [s_scr[b, 0, h] for h in range(H)] + [s_scr[b, 1, h] for h in range(H)]
    out, new_state = _delta_chunks(q, k, v, beta, gc_col, gc_row, gtot, state, ([True] * H + [False] * H) * B)
    for b in range(B):
        for h in range(H):
            of_ref[b, :, sls[h]] = out[2 * H * b + h]
            ob_ref[b, :, sls[h]] = out[2 * H * b + H + h]
            s_scr[b, 0, h] = new_state[2 * H * b + h]
            s_scr[b, 1, h] = new_state[2 * H * b + H + h]


def _gdn_scan(qkv, gates, gates_row, n_heads, head_dim):
    B, L, _ = qkv.shape
    H, Dh = n_heads, head_dim
    d = H * Dh
    C = GDN_CHUNK
    N = L // C
    fwd = lambda col: pl.BlockSpec((B, C, d), lambda n: (0, n, col))
    bwd = lambda col: pl.BlockSpec((B, C, d), lambda n: (0, N - 1 - n, col))
    return pl.pallas_call(
        functools.partial(_gdn_scan_kernel, n_heads=H, head_dim=Dh),
        grid=(N,),
        in_specs=[fwd(0), fwd(1), fwd(2), bwd(0), bwd(1), bwd(2),
                  pl.BlockSpec((B, C, LANES), lambda n: (0, n, 0)),
                  pl.BlockSpec((B, C, LANES), lambda n: (0, N - 1 - n, 0)),
                  pl.BlockSpec((B, None, 2 * H, C), lambda n: (0, n, 0, 0)),
                  pl.BlockSpec((B, None, 2 * H, C), lambda n: (0, N - 1 - n, 0, 0))],
        out_specs=[pl.BlockSpec((B, C, d), lambda n: (0, n, 0)),
                   pl.BlockSpec((B, C, d), lambda n: (0, N - 1 - n, 0))],
        out_shape=[jax.ShapeDtypeStruct((B, L, d), F32)] * 2,
        scratch_shapes=[pltpu.VMEM((B, 2, H, Dh, Dh), F32)],
        compiler_params=_cparams(("arbitrary",), 32),
        name="gdn_scan",
    )(qkv, qkv, qkv, qkv, qkv, qkv, gates, gates, gates_row, gates_row)


def _gdn_branch(p3, pg, conv_w, a_log_f, a_log_b, dt_bias_f, dt_bias_b, col0, n_heads, head_dim):
    B, L, _ = p3.shape
    H = n_heads
    qkv = _gdn_pre(p3, conv_w, col0, H, head_dim)
    gates = _gdn_gates(pg, a_log_f, a_log_b, dt_bias_f, dt_bias_b, H).reshape(B, L, LANES)
    N = L // GDN_CHUNK
    gates_row = gates[..., 2 * H:4 * H].reshape(B, N, GDN_CHUNK, 2 * H).transpose(0, 1, 3, 2)
    return _gdn_scan(qkv, gates, gates_row, H, head_dim)


@functools.lru_cache(maxsize=None)
def _filter_positions(L, pos_emb_dim):
    n = 2 * L
    r = np.arange(n)
    k = np.where(r < L, r, np.where(r == L, 0, n - r)).astype(np.float64)
    t = k / (L - 1)
    bands = (pos_emb_dim - 1) // 2
    fb = np.linspace(1e-4, bands - 1, bands)
    ang = (2.0 * math.pi / L) * k[:, None] * fb[None, :]
    z = np.concatenate([t[:, None], np.cos(ang), -np.sin(ang)], axis=-1)
    return z.astype(np.float32)


@functools.lru_cache(maxsize=None)
def _decay_rates(d_hy):
    max_decay = math.log(DECAY_TARGET) / FAST_DECAY_PCT
    min_decay = math.log(DECAY_TARGET) / SLOW_DECAY_PCT
    return np.abs(np.linspace(min_decay, max_decay, d_hy)).astype(np.float32).reshape(1, d_hy)


def _filt_mlp_kernel(zt_ref, w1t_ref, b1_ref, w2t_ref, b2_ref, w3t_ref, b3_ref, fr_ref, o_ref):
    fr = fr_ref[...]
    h = jnp.sin(fr * (_dot_hi(w1t_ref[...], zt_ref[...]) + b1_ref[...]))
    h = jnp.sin(fr * (_dot_hi(w2t_ref[...], h) + b2_ref[...]))
    o_ref[...] = jnp.sin(fr * (_dot_hi(w3t_ref[...], h) + b3_ref[...]))


def _filt_mlp(zt, w1, b1, w2, b2, w3, b3, freq, tc=1024):
    pe, n = zt.shape
    fw = w1.shape[1]
    col = lambda a: a.reshape(-1, 1).astype(F32)
    full = lambda a: pl.BlockSpec(a.shape, lambda i: (0, 0))
    args = (zt, w1.T, col(b1), w2.T, col(b2), w3.T, col(b3), col(freq))
    return pl.pallas_call(
        _filt_mlp_kernel,
        grid=(n // tc,),
        in_specs=[pl.BlockSpec((pe, tc), lambda i: (0, i))] + [full(a) for a in args[1:]],
        out_specs=pl.BlockSpec((fw, tc), lambda i: (0, i)),
        out_shape=jax.ShapeDtypeStruct((fw, n), F32),
        compiler_params=_cparams(("parallel",), 24),
        name="filt_mlp",
    )(*args)


@functools.lru_cache(maxsize=None)
def _dft_tables(L):
    n = 2 * L
    N2 = FFT_N2
    N1 = n // N2
    N1h = N1 // 2
    j2 = np.arange(N2)[:, None, None]
    k1 = np.arange(N1)[None, :, None]

    def stage1(n_j1):
        j1 = np.arange(n_j1)[None, None, :]
        m = (k1 * (N2 * j1 + j2)) % n
        th = 2.0 * np.pi * m / n
        return np.cos(th), np.sin(th)

    c, s = stage1(N1h)
    t1 = np.concatenate([np.concatenate([c, s], axis=2), np.concatenate([-s, c], axis=2)], axis=1)
    c, s = stage1(N1)
    t1g = np.concatenate([c, -s], axis=1)
    c, s = stage1(N1h)
    ct, st = np.swapaxes(c, 1, 2) / n, np.swapaxes(s, 1, 2) / n
    t2 = np.concatenate([np.concatenate([ct, -st], axis=2), np.concatenate([st, ct], axis=2)], axis=1)
    a = np.arange(N2)
    th = 2.0 * np.pi * ((a[:, None] * a[None, :]) % N2) / N2
    c2, s2 = np.cos(th), np.sin(th)
    f2f = np.block([[c2, s2], [-s2, c2]])
    f2i = np.block([[c2, -s2], [s2, c2]])
    as_bf16 = lambda x: jnp.asarray(x, dtype=F32).astype(BF16)
    return dict(N1=N1, N2=N2, t1=t1.astype(np.float32), t1g=t1g.astype(np.float32), t2=t2.astype(np.float32),
                f2f=f2f.astype(np.float32), f2i=f2i.astype(np.float32))


FILT_ROWS = 512
FFT_UNROLL = 8
PITCH_PAD = 8


def _filt_fft_kernel(h3_ref, w4f_ref, w4b_ref, delta_ref, t1g_ref, f2f_ref, hspec_ref, hb0_ref, g_scr, a_scr,
                     *, L, N1, N2):
    n = 2 * L
    gp = N2 + PITCH_PAD
    ap = 2 * N1 + PITCH_PAD
    delta = delta_ref[...]
    hb0_ref[...] = jnp.zeros_like(hb0_ref)

    def gen(c, carry):
        r0 = pl.multiple_of(c * FILT_ROWS, FILT_ROWS)
        row = r0 + lax.broadcasted_iota(jnp.int32, (FILT_ROWS, LANES), 0)
        lag = jnp.where(row < L, row, jnp.where(row == L, 0, n - row))
        window = jnp.exp(-(lag.astype(F32) * (1.0 / (L - 1))) * delta)
        w4 = jnp.where(r0 < L, w4f_ref[...], w4b_ref[...])
        g = _dot(h3_ref[pl.ds(r0, FILT_ROWS), :], w4) * window
        at_l = row == L
        hb0_ref[...] += jnp.sum(jnp.where(at_l, g, 0.0), axis=0, keepdims=True)
        g = jnp.where(at_l, 0.0, g)
        for q in range(FILT_ROWS // N2):
            dst = pl.multiple_of((c * (FILT_ROWS // N2) + q) * gp, SUBLANES)
            g_scr[pl.ds(dst, N2), :] = g[q * N2:(q + 1) * N2]
        return carry

    lax.fori_loop(0, n // FILT_ROWS, gen, 0)

    def stage1(j2, carry):
        x = g_scr[pl.ds(j2, N1, stride=gp), :]
        a_scr[pl.ds(pl.multiple_of(j2 * ap, SUBLANES), 2 * N1), :] = _dot(t1g_ref[j2], x)
        return carry

    lax.fori_loop(0, N2, stage1, 0, unroll=2 * FFT_UNROLL)

    def stage2(k1, carry):
        ar = a_scr[pl.ds(k1, N2, stride=ap), :]
        ai = a_scr[pl.ds(N1 + k1, N2, stride=ap), :]
        z = _dot(f2f_ref[...], jnp.concatenate([ar, ai], axis=0))
        hspec_ref[pl.ds(pl.multiple_of(k1 * 2 * N2, 2 * N2), 2 * N2), :] = z.astype(hspec_ref.dtype)
        return carry

    lax.fori_loop(0, N1, stage2, 0, unroll=FFT_UNROLL)


def _filt_fft(h3, w4, L, d_hy):
    tb = _dft_tables(L)
    N1, N2 = tb["N1"], tb["N2"]
    n = 2 * L
    fw = h3.shape[1]
    nt = d_hy // LANES
    t1g = jnp.asarray(tb["t1g"]).astype(BF16)
    f2f = jnp.asarray(tb["f2f"]).astype(BF16)
    return pl.pallas_call(
        functools.partial(_filt_fft_kernel, L=L, N1=N1, N2=N2),
        grid=(nt,),
        in_specs=[pl.BlockSpec((n, fw), lambda c: (0, 0)),
                  pl.BlockSpec((fw, LANES), lambda c: (0, c)),
                  pl.BlockSpec((fw, LANES), lambda c: (0, c + nt)),
                  pl.BlockSpec((1, LANES), lambda c: (0, c)),
                  pl.BlockSpec(t1g.shape, lambda c: (0, 0, 0)),
                  pl.BlockSpec(f2f.shape, lambda c: (0, 0))],
        out_specs=[pl.BlockSpec((2 * n, LANES), lambda c: (0, c)),
                   pl.BlockSpec((SUBLANES, LANES), lambda c: (0, c))],
        out_shape=[jax.ShapeDtypeStruct((2 * n, d_hy), BF16), jax.ShapeDtypeStruct((SUBLANES, d_hy), F32)],
        scratch_shapes=[pltpu.VMEM((N1 * (N2 + PITCH_PAD), LANES), F32),
                        pltpu.VMEM((N2 * (2 * N1 + PITCH_PAD), LANES), F32)],
        compiler_params=_cparams(("parallel",), 48),
        name="filt_fft",
    )(h3, w4, w4, jnp.asarray(_decay_rates(d_hy)), t1g, f2f)


def _hy_conv_kernel(vg_ref, hspec_ref, skip_ref, hb0_ref, t1_ref, f2f_ref, f2i_ref, t2_ref, y_ref,
                    x_scr, a_scr, b_scr, *, N1, N2):
    N1h = N1 // 2
    xp = N2 + PITCH_PAD
    ap = 2 * N1 + PITCH_PAD
    bp = 2 * N2 + PITCH_PAD

    for b in range(2):
        for j1 in range(N1h):
            x_scr[b, pl.ds(j1 * xp, N2), :] = vg_ref[b, pl.ds(j1 * N2, N2), :].astype(F32)

    def stage1(j2, carry):
        x = jnp.concatenate([x_scr[0, pl.ds(j2, N1h, stride=xp), :],
                             x_scr[1, pl.ds(j2, N1h, stride=xp), :]], axis=0)
        a_scr[pl.ds(pl.multiple_of(j2 * ap, SUBLANES), 2 * N1), :] = _dot(t1_ref[j2], x)
        return carry

    lax.fori_loop(0, N2, stage1, 0, unroll=2 * FFT_UNROLL)

    def stage2(k1, carry):
        ar = a_scr[pl.ds(k1, N2, stride=ap), :]
        ai = a_scr[pl.ds(N1 + k1, N2, stride=ap), :]
        z = _dot(f2f_ref[...], jnp.concatenate([ar, ai], axis=0))
        zr, zi = z[:N2], z[N2:]
        base = pl.multiple_of(k1 * 2 * N2, 2 * N2)
        hr = hspec_ref[pl.ds(base, N2), :].astype(F32)
        hi = hspec_ref[pl.ds(base + N2, N2), :].astype(F32)
        prod = jnp.concatenate([zr * hr - zi * hi, zr * hi + zi * hr], axis=0)
        b_scr[pl.ds(pl.multiple_of(k1 * bp, SUBLANES), 2 * N2), :] = _dot(f2i_ref[...], prod)
        return carry

    lax.fori_loop(0, N1, stage2, 0, unroll=FFT_UNROLL)

    skip = skip_ref[...] + hb0_ref[0:1, :]

    def stage3(j2, carry):
        b = jnp.concatenate([b_scr[pl.ds(j2, N1, stride=bp), :],
                             b_scr[pl.ds(N2 + j2, N1, stride=bp), :]], axis=0)
        y = _dot(t2_ref[j2], b)
        x_scr[0, pl.ds(j2, N1h, stride=xp), :] = y[:N1h] + x_scr[0, pl.ds(j2, N1h, stride=xp), :] * skip
        x_scr[1, pl.ds(j2, N1h, stride=xp), :] = y[N1h:] + x_scr[1, pl.ds(j2, N1h, stride=xp), :] * skip
        return carry

    lax.fori_loop(0, N2, stage3, 0, unroll=2 * FFT_UNROLL)

    for b in range(2):
        for j1 in range(N1h):
            y_ref[b, pl.ds(j1 * N2, N2), :] = x_scr[b, pl.ds(j1 * xp, N2), :]


def _hy_conv(vg, hspec, skip, hb0):
    B, L, d_hy = vg.shape
    assert B % 2 == 0
    tb = _dft_tables(L)
    N1, N2 = tb["N1"], tb["N2"]
    n = 2 * L
    nt = d_hy // LANES
    t1, t2 = (jnp.asarray(tb[k]).astype(BF16) for k in ("t1", "t2"))
    f2f, f2i = (jnp.asarray(tb[k]).astype(BF16) for k in ("f2f", "f2i"))
    const3 = lambda a: pl.BlockSpec(a.shape, lambda b, c: (0, 0, 0))
    const2 = lambda a: pl.BlockSpec(a.shape, lambda b, c: (0, 0))
    return pl.pallas_call(
        functools.partial(_hy_conv_kernel, N1=N1, N2=N2),
        grid=(B // 2, nt),
        in_specs=[pl.BlockSpec((2, L, LANES), lambda b, c: (b, 0, c)),
                  pl.BlockSpec((2 * n, LANES), lambda b, c: (0, c)),
                  pl.BlockSpec((1, LANES), lambda b, c: (0, c)),
                  pl.BlockSpec((SUBLANES, LANES), lambda b, c: (0, c)),
                  const3(t1), const2(f2f), const2(f2i), const3(t2)],
        out_specs=pl.BlockSpec((2, L, LANES), lambda b, c: (b, 0, c)),
        out_shape=jax.ShapeDtypeStruct((B, L, d_hy), F32),
        scratch_shapes=[pltpu.VMEM((2, (N1 // 2) * (N2 + PITCH_PAD), LANES), F32),
                        pltpu.VMEM((N2 * (2 * N1 + PITCH_PAD), LANES), F32),
                        pltpu.VMEM((N1 * (2 * N2 + PITCH_PAD), LANES), F32)],
        compiler_params=_cparams(("parallel", "parallel"), 58),
        name="hy_conv",
    )(vg, hspec, skip.reshape(1, d_hy).astype(F32), hb0, t1, f2f, f2i, t2)


def _hyena_branch(p3, conv_w, conv_b, fw1, fb1, fw2, fb2, fw3, fb3, fw4, freq, skip, d_hy):
    B, L, _ = p3.shape
    x0c, vg = _hy_pre(p3, conv_w, conv_b, d_hy)
    zt = jnp.asarray(_filter_positions(L, fw1.shape[0]).T)
    h3 = _filt_mlp(zt, fw1, fb1, fw2, fb2, fw3, fb3, freq).T
    hspec, hb0 = _filt_fft(h3, fw4, L, d_hy)
    return x0c, _hy_conv(vg, hspec, skip, hb0)


ROUTE_GATE, ROUTE_EXPERT, ROUTE_RANK = 0, 2, 4
MIX_ROWS = 256


def _mix_route_kernel(yc_ref, x0_ref, of_ref, ob_ref, z_ref, hnw_ref, gnw_ref, x_ref, wo_ref, n2w_ref, wr_ref, br_ref,
                      x2_ref, u_ref, route_ref, route_t_ref, cnt_ref, run_scr, *, n_heads, head_dim, n_groups,
                      per_group):
    tm = x_ref.shape[0]
    G, P = n_groups, per_group

    @pl.when(pl.program_id(0) == 0)
    def _():
        run_scr[...] = jnp.zeros_like(run_scr)

    run = run_scr[0:1, :]
    blocks = [pl.ds(s * MIX_ROWS, MIX_ROWS) for s in range(tm // MIX_ROWS)]
    us = []
    for rs in blocks:
        yh = yc_ref[rs, :] * x0_ref[rs, :].astype(F32)
        yh = yh * lax.rsqrt(jnp.mean(yh * yh, axis=-1, keepdims=True) + EPS) * hnw_ref[...]
        parts = [yh.astype(BF16)]
        for h in range(n_heads):
            sl = slice(h * head_dim, (h + 1) * head_dim)
            o = of_ref[rs, sl] + ob_ref[rs, sl]
            z = z_ref[rs, sl].astype(F32)
            o = o * lax.rsqrt(jnp.mean(o * o, axis=-1, keepdims=True) + EPS) * gnw_ref[...] * _silu(z)
            parts.append(o.astype(BF16))
        ymix = jnp.concatenate(parts, axis=-1)
        x2 = x_ref[rs, :] + jnp.dot(ymix, wo_ref[...], preferred_element_type=F32)
        x2_ref[rs, :] = x2
        u = x2 * lax.rsqrt(jnp.mean(x2 * x2, axis=-1, keepdims=True) + EPS) * n2w_ref[...]
        u_ref[rs, :] = _pack_bf16_pairs(u)
        us.append(u)

    all_logits = [_dot(u, wr_ref[...]) + br_ref[...] for u in us]
    for s, (rs, logits) in enumerate(zip(blocks, all_logits)):
        lane = lax.broadcasted_iota(jnp.int32, logits.shape, 1)
        neg = jnp.float32(-jnp.inf)
        big = jnp.int32(4 * LANES)
        first = lambda hit: jnp.min(jnp.where(hit, lane, big), axis=-1, keepdims=True)
        gl = jnp.where(lane < G, logits, neg)
        gmax = jnp.max(gl, axis=-1, keepdims=True)
        gidx = first(gl == gmax)
        grp_gate = 1.0 / jnp.sum(jnp.exp(gl - gmax), axis=-1, keepdims=True)
        in_grp = (lane >= G) & (lane < G + G * P) & (((lane - G) // P) == gidx)
        ll = jnp.where(in_grp, logits, neg)
        m1 = jnp.max(ll, axis=-1, keepdims=True)
        i1 = first(ll == m1)
        denom = jnp.sum(jnp.exp(ll - m1), axis=-1, keepdims=True)
        ll2 = jnp.where(lane == i1, neg, ll)
        m2 = jnp.max(ll2, axis=-1, keepdims=True)
        i2 = first(ll2 == m2)
        p1 = 1.0 / denom
        p2 = jnp.exp(m2 - m1) / denom
        gate1 = grp_gate * (p1 / (p1 + p2))
        gate2 = grp_gate * (p2 / (p1 + p2))
        e1 = i1 - G
        e2 = i2 - G

        oh1 = jnp.where(lane == e1, 1.0, 0.0)
        oh2 = jnp.where(lane == e2, 1.0, 0.0)
        oh = oh1 + oh2
        ii = lax.broadcasted_iota(jnp.int32, (MIX_ROWS, MIX_ROWS), 0)
        jj = lax.broadcasted_iota(jnp.int32, (MIX_ROWS, MIX_ROWS), 1)
        before = _dot(jnp.where(ii > jj, 1.0, 0.0), oh) + run
        r1 = jnp.sum(oh1 * before, axis=-1, keepdims=True)
        r2 = jnp.sum(oh2 * before, axis=-1, keepdims=True)
        run = run + jnp.sum(oh, axis=0, keepdims=True)

        rec = jnp.where(lane == ROUTE_GATE, gate1, 0.0)
        rec = jnp.where(lane == ROUTE_GATE + 1, gate2, rec)
        rec = jnp.where(lane == ROUTE_EXPERT, e1.astype(F32), rec)
        rec = jnp.where(lane == ROUTE_EXPERT + 1, e2.astype(F32), rec)
        rec = jnp.where(lane == ROUTE_RANK, r1, rec)
        rec = jnp.where(lane == ROUTE_RANK + 1, r2, rec)
        route_ref[rs, :] = rec
        route_t_ref[:, s * MIX_ROWS:(s + 1) * MIX_ROWS] = jnp.transpose(rec)[:SUBLANES, :]

    run_scr[...] = jnp.broadcast_to(run, run_scr.shape)
    cnt_ref[...] = run_scr[...]


def _mix_route(yconv, x0c, o_f, o_b, p, z_col, hy_norm_w, gdn_norm_w, xf, w_out_bf16, norm2_w, wr, br,
               n_heads, head_dim, n_groups, per_group, tm=512):
    M, D = xf.shape
    d_hy = yconv.shape[1]
    d_gdn = o_f.shape[1]
    assert z_col % d_gdn == 0 and n_groups * (per_group + 1) <= LANES
    zb = z_col // d_gdn
    row = lambda i: (i, 0)
    const = lambda i: (0, 0)
    kern = functools.partial(_mix_route_kernel, n_heads=n_heads, head_dim=head_dim,
                             n_groups=n_groups, per_group=per_group)
    return pl.pallas_call(
        kern,
        grid=(M // tm,),
        in_specs=[pl.BlockSpec((tm, d_hy), row), pl.BlockSpec((tm, d_hy), row),
                  pl.BlockSpec((tm, d_gdn), row), pl.BlockSpec((tm, d_gdn), row),
                  pl.BlockSpec((tm, d_gdn), lambda i: (i, zb)),
                  pl.BlockSpec((1, d_hy), const), pl.BlockSpec((1, head_dim), const),
                  pl.BlockSpec((tm, D), row), pl.BlockSpec(w_out_bf16.shape, const, pipeline_mode=pl.Buffered(1)),
                  pl.BlockSpec((1, D), const), pl.BlockSpec((D, LANES), const), pl.BlockSpec((1, LANES), const)],
        out_specs=[pl.BlockSpec((tm, D), row), pl.BlockSpec((tm, D // 2), row),
                   pl.BlockSpec((tm, LANES), row), pl.BlockSpec((SUBLANES, tm), lambda i: (0, i)),
                   pl.BlockSpec((SUBLANES, LANES), const)],
        out_shape=[jax.ShapeDtypeStruct((M, D), F32), jax.ShapeDtypeStruct((M, D // 2), jnp.uint32),
                   jax.ShapeDtypeStruct((M, LANES), F32), jax.ShapeDtypeStruct((SUBLANES, M), F32),
                   jax.ShapeDtypeStruct((SUBLANES, LANES), F32)],
        scratch_shapes=[pltpu.VMEM((SUBLANES, LANES), F32)],
        compiler_params=_cparams(("arbitrary",), 58),
        name="mix_route",
    )(yconv, x0c, o_f, o_b, p, hy_norm_w.reshape(1, d_hy), gdn_norm_w.reshape(1, head_dim), xf, w_out_bf16,
      norm2_w.reshape(1, D), wr, br)


def _experts_kernel(te_ref, na_ref, tgt_nxt_ref, tgt_ref, tgt_prv_ref, u_hbm, w1_ref, w3_ref, w2_ref, out_hbm,
                    x0, x1, y0, y1, w1b, w3b, w2b, gsem, ssem, *, n_tokens):
    i = pl.program_id(0)
    na = na_ref[0]
    TB = x0.shape[0]
    spare0 = 2 * n_tokens

    def token_of(v):
        if n_tokens & (n_tokens - 1) == 0:
            return v & (n_tokens - 1)
        return lax.rem(v, n_tokens)

    def gather(tgt, xbuf, s, r):
        return pltpu.make_async_copy(u_hbm.at[pl.ds(token_of(tgt[r]), 1)], xbuf.at[pl.ds(r, 1)], gsem.at[s])

    def scatter(ybuf, s, r, dst):
        return pltpu.make_async_copy(ybuf.at[pl.ds(r, 1)], out_hbm.at[pl.ds(dst, 1)], ssem.at[s])

    def gather_wait(xbuf, s):
        pltpu.make_async_copy(u_hbm.at[pl.ds(0, TB)], xbuf, gsem.at[s]).wait()

    def scatter_wait(ybuf, s):
        pltpu.make_async_copy(ybuf, out_hbm.at[pl.ds(0, TB)], ssem.at[s]).wait()

    def each_row(fn):
        def body(r, carry):
            fn(r)
            return carry
        lax.fori_loop(0, TB, body, 0, unroll=8)

    def step(p):
        q = 1 - p
        x_cur, x_nxt = (x0, x1) if p == 0 else (x1, x0)
        y_cur, y_prv = (y0, y1) if p == 0 else (y1, y0)
        gather_wait(x_cur, p)

        @pl.when(i > 0)
        def _():
            scatter_wait(y_cur, p)

        x = _unpack_bf16_pairs(x_cur[...]).astype(BF16)
        for r in range(TB):
            gather(tgt_nxt_ref, x_nxt, q, r).start()
        first = i == 0
        for r in range(TB):
            scatter(y_prv, q, r, jnp.where(first, spare0 + r, tgt_prv_ref[r])).start()
        h = _silu(jnp.dot(x, w1b[...], preferred_element_type=F32)) * jnp.dot(x, w3b[...], preferred_element_type=F32)
        y_cur[...] = _pack_bf16_pairs(jnp.dot(h.astype(BF16), w2b[...], preferred_element_type=F32))

        @pl.when(i == na - 1)
        def _():
            gather_wait(x_nxt, q)
            scatter_wait(y_prv, q)
            each_row(lambda r: scatter(y_cur, p, r, tgt_ref[r]).start())
            scatter_wait(y_cur, p)

    @pl.when(i < na)
    def _():
        @pl.when(i == 0)
        def _():
            y0[...] = jnp.zeros_like(y0)
            y1[...] = jnp.zeros_like(y1)
            fill = pltpu.make_async_copy(y1, out_hbm.at[pl.ds(spare0, TB)], ssem.at[1])
            fill.start()
            fill.wait()
            each_row(lambda r: gather(tgt_ref, x0, 0, r).start())

        @pl.when((i == 0) | (te_ref[i] != te_ref[jnp.maximum(i - 1, 0)]))
        def _():
            w1b[...] = w1_ref[...].astype(BF16)
            w3b[...] = w3_ref[...].astype(BF16)
            w2b[...] = w2_ref[...].astype(BF16)

        parity = lax.rem(i, 2)

        @pl.when(parity == 0)
        def _():
            step(0)

        @pl.when(parity == 1)
        def _():
            step(1)


def _experts(u, slot_tgt, tile_expert, n_active, w1, w3, w2):
    T = u.shape[0]
    E, D, de = w1.shape
    TB = EXPERT_ROWS
    n_tiles = slot_tgt.shape[0] // TB
    wspec = lambda shape: pl.BlockSpec(shape, lambda i, te, na: (te[i], 0, 0))
    table = lambda f: pl.BlockSpec((TB,), lambda i, te, na: (f(i),), memory_space=pltpu.SMEM)
    grid_spec = pltpu.PrefetchScalarGridSpec(
        num_scalar_prefetch=2,
        grid=(n_tiles,),
        in_specs=[table(lambda i: jnp.minimum(i + 1, n_tiles - 1)), table(lambda i: i),
                  table(lambda i: jnp.maximum(i - 1, 0)),
                  pl.BlockSpec(memory_space=pl.ANY),
                  wspec((None, D, de)), wspec((None, D, de)), wspec((None, de, D))],
        out_specs=pl.BlockSpec(memory_space=pl.ANY),
        scratch_shapes=[pltpu.VMEM((TB, D // 2), jnp.uint32)] * 4
                       + [pltpu.VMEM((D, de), BF16), pltpu.VMEM((D, de), BF16), pltpu.VMEM((de, D), BF16),
                          pltpu.SemaphoreType.DMA((2,)), pltpu.SemaphoreType.DMA((2,))],
    )
    return pl.pallas_call(
        functools.partial(_experts_kernel, n_tokens=T),
        grid_spec=grid_spec,
        out_shape=jax.ShapeDtypeStruct((2 * T + TB, D // 2), jnp.uint32),
        compiler_params=_cparams(("arbitrary",), 52),
        name="experts",
    )(tile_expert, n_active, slot_tgt, slot_tgt, slot_tgt, u, w1, w3, w2)


def _slot_table_kernel(dest_ref, init_hbm, o_ref, sem):
    fill = pltpu.make_async_copy(init_hbm, o_ref, sem)
    fill.start()
    fill.wait()

    def put(a, carry):
        o_ref[dest_ref[a]] = a
        return carry

    lax.fori_loop(0, dest_ref.shape[0], put, 0, unroll=8)


def _slot_table(dest, init):
    smem = pl.BlockSpec(memory_space=pltpu.SMEM)
    return pl.pallas_call(
        _slot_table_kernel,
        in_specs=[smem, pl.BlockSpec(memory_space=pl.ANY)],
        out_specs=smem,
        out_shape=jax.ShapeDtypeStruct(init.shape, jnp.int32),
        scratch_shapes=[pltpu.SemaphoreType.DMA],
        name="slot_table",
    )(dest, init)


def _dispatch_tables(route_t, counts, n_experts):
    T = route_t.shape[1]
    TB = EXPERT_ROWS
    e = route_t[ROUTE_EXPERT:ROUTE_EXPERT + 2].astype(jnp.int32)
    rank = route_t[ROUTE_RANK:ROUTE_RANK + 2].astype(jnp.int32)
    cnt = counts[0, :n_experts].astype(jnp.int32)
    padded = (cnt + TB - 1) // TB * TB
    pad_end = jnp.cumsum(padded)
    pad_start = pad_end - padded
    ids = jnp.arange(n_experts, dtype=jnp.int32)[:, None, None]
    start_of = jnp.sum(jnp.where(e[None] == ids, pad_start[:, None, None], 0), axis=0)
    dest = start_of + rank
    n_tiles = -(-(2 * T + n_experts * (TB - 1)) // TB)
    padded_slot = 2 * T + jnp.arange(n_tiles * TB, dtype=jnp.int32) % TB
    slot_tgt = _slot_table(dest.reshape(-1), padded_slot)
    start = jnp.arange(n_tiles, dtype=jnp.int32) * TB
    tile_expert = jnp.sum(start[:, None] >= pad_end[None, :], axis=1)
    tile_expert = jnp.minimum(tile_expert, n_experts - 1).astype(jnp.int32)
    n_active = (pad_end[-1] // TB).astype(jnp.int32).reshape(1)
    return slot_tgt, tile_expert, n_active


def _combine_kernel(x2_ref, e0_ref, e1_ref, route_ref, w_ref, o_ref, *, final_norm):
    r = route_ref[...]
    e0 = _unpack_bf16_pairs(e0_ref[...])
    e1 = _unpack_bf16_pairs(e1_ref[...])
    y = x2_ref[...] + r[:, ROUTE_GATE:ROUTE_GATE + 1] * e0 + r[:, ROUTE_GATE + 1:ROUTE_GATE + 2] * e1
    if final_norm:
        y = y * lax.rsqrt(jnp.mean(y * y, axis=-1, keepdims=True) + EPS) * w_ref[...]
    o_ref[...] = y


def _combine(x2, planes, route, norm_w, final_norm, tm=512):
    M, D = x2.shape
    return pl.pallas_call(
        functools.partial(_combine_kernel, final_norm=final_norm),
        grid=(M // tm,),
        in_specs=[pl.BlockSpec((tm, D), lambda i: (i, 0)),
                  pl.BlockSpec((tm, D // 2), lambda i: (i, 0)),
                  pl.BlockSpec((tm, D // 2), lambda i: (M // tm + i, 0)),
                  pl.BlockSpec((tm, LANES), lambda i: (i, 0)),
                  pl.BlockSpec((1, D), lambda i: (0, 0))],
        out_specs=pl.BlockSpec((tm, D), lambda i: (i, 0)),
        out_shape=jax.ShapeDtypeStruct((M, D), F32),
        compiler_params=_cparams(("parallel",), 48),
        name="combine",
    )(x2, planes, planes, route, norm_w.reshape(1, D))


def kernel(x, norm1_w, w_in, hy_conv_w, hy_conv_b, hy_filt_w1, hy_filt_b1, hy_filt_w2, hy_filt_b2, hy_filt_w3, hy_filt_b3, hy_filt_w4, hy_sin_freq, hy_skip, hy_norm_w, gdn_conv_w, gdn_a_log_f, gdn_a_log_b, gdn_dt_bias_f, gdn_dt_bias_b, gdn_norm_w, w_out, norm2_w, router_group_w, router_group_b, router_expert_w, router_expert_b, exp_w1, exp_w3, exp_w2, final_norm_w):
    B, L, D = x.shape
    M = B * L
    depth = w_in.shape[0]
    d_hy = hy_skip.shape[-1]
    H = gdn_a_log_f.shape[-1]
    Dh = gdn_norm_w.shape[-1]
    d_gdn = H * Dh
    n_main = 3 * d_hy + 4 * d_gdn
    G = router_group_w.shape[-1]
    E = router_expert_w.shape[-1]
    xf = x.reshape(M, D)
    for l in range(depth):
        p, pg = _inproj(xf, norm1_w[l], jnp.swapaxes(w_in[l], 0, 1), n_main, 4 * H)
        p3 = p.reshape(B, L, n_main)
        x0c, yconv = _hyena_branch(p3, hy_conv_w[l], hy_conv_b[l], hy_filt_w1[l], hy_filt_b1[l], hy_filt_w2[l],
                                   hy_filt_b2[l], hy_filt_w3[l], hy_filt_b3[l], hy_filt_w4[l], hy_sin_freq[l],
                                   hy_skip[l], d_hy)
        o_f, o_b = _gdn_branch(p3, pg, gdn_conv_w[l], gdn_a_log_f[l], gdn_a_log_b[l], gdn_dt_bias_f[l],
                               gdn_dt_bias_b[l], 3 * d_hy, H, Dh)
        wr = jnp.pad(jnp.concatenate([router_group_w[l], router_expert_w[l]], axis=1), ((0, 0), (0, LANES - G - E)))
        br = jnp.pad(jnp.concatenate([router_group_b[l], router_expert_b[l]]), (0, LANES - G - E)).reshape(1, LANES)
        x2, u, route, route_t, counts = _mix_route(
            yconv.reshape(M, d_hy), x0c.reshape(M, d_hy), o_f.reshape(M, d_gdn), o_b.reshape(M, d_gdn), p,
            3 * d_hy + 3 * d_gdn, hy_norm_w[l], gdn_norm_w[l], xf, w_out[l].astype(BF16), norm2_w[l], wr, br,
            H, Dh, G, E // G)
        slot_tgt, tile_expert, n_active = _dispatch_tables(route_t, counts, E)
        planes = _experts(u, slot_tgt, tile_expert, n_active, exp_w1[l], exp_w3[l], exp_w2[l])
        xf = _combine(x2, planes, route, final_norm_w, final_norm=(l == depth - 1))
    return xf.reshape(B, L, D)
```

```python
import functools
import math

import jax
import jax.numpy as jnp
import numpy as np
from jax import lax
from jax.experimental import pallas as pl
from jax.experimental.pallas import tpu as pltpu

F32 = jnp.float32
BF16 = jnp.bfloat16
EPS = 1e-6
LANES = 128
SUBLANES = 8
VMEM_BYTES_V7X = 64 * 1024 * 1024
GDN_CHUNK = 64
FFT_N2 = 128
EXPERT_ROWS = 256
DECAY_TARGET = 1e-2
FAST_DECAY_PCT = 0.3
SLOW_DECAY_PCT = 1.5


def _cparams(sem, vmem_mb):
    return pltpu.CompilerParams(dimension_semantics=sem, vmem_limit_bytes=int(vmem_mb * 1024 * 1024))


def _dot(a, b):
    return jnp.dot(a.astype(BF16), b.astype(BF16), preferred_element_type=F32)


def _dot_nt(a, b):
    return lax.dot_general(a.astype(BF16), b.astype(BF16), (((1,), (1,)), ((), ())), preferred_element_type=F32)


def _dot_tn(a, b):
    return lax.dot_general(a.astype(BF16), b.astype(BF16), (((0,), (0,)), ((), ())), preferred_element_type=F32)


def _dot_hi(a, b):
    return jnp.dot(a, b, preferred_element_type=F32, precision=lax.Precision.HIGHEST)


def _silu(x):
    return x * jax.nn.sigmoid(x)


def _pack_bf16_pairs(x):
    c = x.shape[1] // 2
    lo = lax.bitcast_convert_type(x[:, :c].astype(BF16).astype(F32), jnp.uint32) >> 16
    hi = lax.bitcast_convert_type(x[:, c:].astype(BF16).astype(F32), jnp.uint32) & jnp.uint32(0xFFFF0000)
    return hi | lo


def _unpack_bf16_pairs(w):
    lo = lax.bitcast_convert_type(w << 16, F32)
    hi = lax.bitcast_convert_type(w & jnp.uint32(0xFFFF0000), F32)
    return jnp.concatenate([lo, hi], axis=1)


def _inproj_kernel(x_ref, nw_ref, wt_ref, wgt_ref, p_ref, g_ref, h_scr):
    @pl.when(pl.program_id(1) == 0)
    def _():
        x = x_ref[...]
        h = x * lax.rsqrt(jnp.mean(x * x, axis=-1, keepdims=True) + EPS) * nw_ref[...]
        h_scr[...] = h.astype(BF16)
        g = _dot_nt(h, wgt_ref[...])
        g_ref[...] = jnp.concatenate([g, jnp.zeros((g.shape[0], LANES - g.shape[1]), F32)], axis=1)

    p_ref[...] = _dot_nt(h_scr[...], wt_ref[...]).astype(p_ref.dtype)


def _inproj(xf, norm_w, wt, n_main, n_gate, tm=2048, tn=512):
    M, D = xf.shape
    assert n_main % tn == 0 and M % tm == 0 and n_main % n_gate == 0 and n_gate % SUBLANES == 0
    return pl.pallas_call(
        _inproj_kernel,
        grid=(M // tm, n_main // tn),
        in_specs=[
            pl.BlockSpec((tm, D), lambda i, j: (i, 0), pipeline_mode=pl.Buffered(1)),
            pl.BlockSpec((1, D), lambda i, j: (0, 0)),
            pl.BlockSpec((tn, D), lambda i, j: (j, 0)),
            pl.BlockSpec((n_gate, D), lambda i, j: (n_main // n_gate, 0)),
        ],
        out_specs=[
            pl.BlockSpec((tm, tn), lambda i, j: (i, j)),
            pl.BlockSpec((tm, LANES), lambda i, j: (i, 0)),
        ],
        out_shape=[jax.ShapeDtypeStruct((M, n_main), BF16), jax.ShapeDtypeStruct((M, LANES), F32)],
        scratch_shapes=[pltpu.VMEM((tm, D), BF16)],
        compiler_params=_cparams(("parallel", "arbitrary"), 57),
        name="inproj",
    )(xf, norm_w.reshape(1, D), wt, wt)


def _conv3_rows(ref, r0, rows, w, n_rows):
    cur = ref[pl.ds(r0, rows), :].astype(F32)
    lo = jnp.maximum(r0 - HALO_ROWS, 0)
    hi = jnp.minimum(r0 + rows, n_rows - HALO_ROWS)
    prev_grp = ref[pl.ds(pl.multiple_of(lo, HALO_ROWS), HALO_ROWS), :].astype(F32)
    next_grp = ref[pl.ds(pl.multiple_of(hi, HALO_ROWS), HALO_ROWS), :].astype(F32)
    prev_row = jnp.where(r0 > 0, prev_grp[HALO_ROWS - 1:HALO_ROWS, :], 0.0)
    next_row = jnp.where(r0 + rows < n_rows, next_grp[0:1, :], 0.0)
    row = lax.broadcasted_iota(jnp.int32, cur.shape, 0)
    xm = jnp.where(row == 0, prev_row, pltpu.roll(cur, 1, 0))
    xp = jnp.where(row == rows - 1, next_row, pltpu.roll(cur, rows - 1, 0))
    return xm * w[0:1, :] + cur * w[1:2, :] + xp * w[2:3, :]


CONV_ROWS = 256
HALO_ROWS = 16


def _hy_pre_kernel(x0_ref, x1_ref, v_ref, w0_ref, w1_ref, w2_ref, b0_ref, b1_ref, b2_ref, x0c_ref, vg_ref):
    L = x0_ref.shape[0]
    w0, w1, w2 = w0_ref[...], w1_ref[...], w2_ref[...]
    b0, b1, b2 = b0_ref[...], b1_ref[...], b2_ref[...]

    def body(c, carry):
        r0 = pl.multiple_of(c * CONV_ROWS, CONV_ROWS)
        x0c_ref[pl.ds(r0, CONV_ROWS), :] = (_conv3_rows(x0_ref, r0, CONV_ROWS, w0, L) + b0).astype(x0c_ref.dtype)
        x1c = _conv3_rows(x1_ref, r0, CONV_ROWS, w1, L) + b1
        vc = _conv3_rows(v_ref, r0, CONV_ROWS, w2, L) + b2
        vg_ref[pl.ds(r0, CONV_ROWS), :] = (vc * x1c).astype(vg_ref.dtype)
        return carry

    lax.fori_loop(0, L // CONV_ROWS, body, 0, unroll=2)


def _hy_pre(p3, conv_w, conv_b, d_hy):
    B, L, _ = p3.shape
    nt = d_hy // LANES
    bias = conv_b.reshape(1, -1)
    pspec = lambda off: pl.BlockSpec((None, L, LANES), lambda b, c: (b, 0, c + off))
    wspec = lambda off: pl.BlockSpec((3, LANES), lambda b, c: (0, c + off))
    bspec = lambda off: pl.BlockSpec((1, LANES), lambda b, c: (0, c + off))
    ospec = pl.BlockSpec((None, L, LANES), lambda b, c: (b, 0, c))
    return pl.pallas_call(
        _hy_pre_kernel,
        grid=(B, nt),
        in_specs=[pspec(0), pspec(nt), pspec(2 * nt), wspec(0), wspec(nt), wspec(2 * nt),
                  bspec(0), bspec(nt), bspec(2 * nt)],
        out_specs=[ospec, ospec],
        out_shape=[jax.ShapeDtypeStruct((B, L, d_hy), BF16)] * 2,
        compiler_params=_cparams(("parallel", "parallel"), 40),
        name="hy_pre",
    )(p3, p3, p3, conv_w, conv_w, conv_w, bias, bias, bias)


def _gdn_pre_kernel(x_ref, w_ref, o_ref, *, n_heads, head_dim):
    L = x_ref.shape[0]
    w = w_ref[...]
    c = pl.program_id(1)
    q_scale = jnp.where(c < n_heads, head_dim ** -0.5, 1.0)
    is_qk = c < 2 * n_heads

    def body(i, carry):
        r0 = pl.multiple_of(i * CONV_ROWS, CONV_ROWS)
        y = _silu(_conv3_rows(x_ref, r0, CONV_ROWS, w, L))
        inv = lax.rsqrt(jnp.sum(y * y, axis=-1, keepdims=True) + EPS) * q_scale
        o_ref[pl.ds(r0, CONV_ROWS), :] = y * jnp.where(is_qk, inv, 1.0)
        return carry

    lax.fori_loop(0, L // CONV_ROWS, body, 0, unroll=2)


def _gdn_pre(p3, conv_w, col0, n_heads, head_dim):
    B, L, _ = p3.shape
    assert head_dim == LANES
    nt = 3 * n_heads
    off = col0 // LANES
    return pl.pallas_call(
        functools.partial(_gdn_pre_kernel, n_heads=n_heads, head_dim=head_dim),
        grid=(B, nt),
        in_specs=[pl.BlockSpec((None, L, LANES), lambda b, c: (b, 0, c + off)),
                  pl.BlockSpec((3, LANES), lambda b, c: (0, c))],
        out_specs=pl.BlockSpec((None, L, LANES), lambda b, c: (b, 0, c)),
        out_shape=jax.ShapeDtypeStruct((B, L, nt * LANES), F32),
        compiler_params=_cparams(("parallel", "parallel"), 24),
        name="gdn_pre",
    )(p3, conv_w)


GATE_ROWS = 512


def _gdn_gates_kernel(pg_ref, alog_ref, dtb_ref, o_ref, *, n_heads):
    H = n_heads
    x = pg_ref[...]
    beta = jax.nn.sigmoid(x)
    z = x + dtb_ref[...]
    softplus = jnp.maximum(z, 0.0) + jnp.log1p(jnp.exp(-jnp.abs(z)))
    g = -jnp.exp(alog_ref[...]) * softplus
    pos = lax.broadcasted_iota(jnp.int32, x.shape, 0) & (GDN_CHUNK - 1)
    gc_f = g
    gc_b = g
    step = 1
    while step < GDN_CHUNK:
        gc_f = gc_f + jnp.where(pos >= step, pltpu.roll(gc_f, step, 0), 0.0)
        gc_b = gc_b + jnp.where(pos < GDN_CHUNK - step, pltpu.roll(gc_b, GATE_ROWS - step, 0), 0.0)
        step *= 2
    g_tot = pltpu.roll(gc_f + gc_b - g, 4 * H, 1)
    lane = lax.broadcasted_iota(jnp.int32, x.shape, 1)
    out = jnp.where(lane < 2 * H, beta,
                    jnp.where(lane < 3 * H, gc_f,
                              jnp.where(lane < 4 * H, gc_b,
                                        jnp.where((lane >= 6 * H) & (lane < 8 * H), g_tot, 0.0))))
    o_ref[...] = out


def _gdn_gates(pg, a_log_f, a_log_b, dt_bias_f, dt_bias_b, n_heads):
    M = pg.shape[0]
    H = n_heads
    assert 8 * H <= LANES
    pad = lambda a, b: jnp.concatenate([jnp.zeros((2 * H,), F32), a.astype(F32), b.astype(F32),
                                        jnp.zeros((LANES - 4 * H,), F32)]).reshape(1, LANES)
    return pl.pallas_call(
        functools.partial(_gdn_gates_kernel, n_heads=H),
        grid=(M // GATE_ROWS,),
        in_specs=[pl.BlockSpec((GATE_ROWS, LANES), lambda i: (i, 0)),
                  pl.BlockSpec((1, LANES), lambda i: (0, 0)),
                  pl.BlockSpec((1, LANES), lambda i: (0, 0))],
        out_specs=pl.BlockSpec((GATE_ROWS, LANES), lambda i: (i, 0)),
        out_shape=jax.ShapeDtypeStruct((M, LANES), F32),
        compiler_params=_cparams(("parallel",), 24),
        name="gdn_gates",
    )(pg, pad(a_log_f, a_log_b), pad(dt_bias_f, dt_bias_b))


def _delta_chunks(q, k, v, beta, gc_col, gc_row, gtot, state, lower):
    n = len(q)
    C = q[0].shape[0]
    D = k[0].shape[1]
    ii = lax.broadcasted_iota(jnp.int32, (C, C), 0)
    jj = lax.broadcasted_iota(jnp.int32, (C, C), 1)
    eye = jnp.where(ii == jj, 1.0, 0.0)
    incl = [(ii >= jj) if lo else (ii <= jj) for lo in lower]
    strict = [(ii > jj) if lo else (ii < jj) for lo in lower]
    rng = range(n)
    decay = [jnp.where(incl[i], jnp.exp(jnp.where(incl[i], gc_col[i] - gc_row[i], 0.0)), 0.0) for i in rng]
    kb = [k[i] * beta[i] for i in rng]
    kq = [_dot_nt(jnp.concatenate([kb[i], q[i]], axis=0), k[i]) for i in rng]
    kk = [kq[i][:C] for i in rng]
    qk = [kq[i][C:] for i in rng]
    m = [jnp.where(strict[i], -(kk[i] * decay[i]), 0.0) for i in rng]
    r = [eye + m[i] for i in rng]
    m = [_dot(m[i], m[i]) for i in rng]
    for _ in range(int(math.log2(C)) - 2):
        rm = [_dot(jnp.concatenate([r[i], m[i]], axis=0), m[i]) for i in rng]
        r = [r[i] + rm[i][:C] for i in rng]
        m = [rm[i][C:] for i in rng]
    r = [r[i] + _dot(r[i], m[i]) for i in rng]
    eg = [jnp.exp(gc_col[i]) for i in rng]
    wu = [_dot(r[i], jnp.concatenate([kb[i] * eg[i], v[i] * beta[i]], axis=1)) for i in rng]
    ws = [_dot(jnp.concatenate([wu[i][:, :D], q[i] * eg[i]], axis=0), state[i]) for i in rng]
    v_new = [wu[i][:, D:] - ws[i][:C] for i in rng]
    qkm = [jnp.where(incl[i], qk[i] * decay[i], 0.0) for i in rng]
    out = [ws[i][C:] + _dot(qkm[i], v_new[i]) for i in rng]
    k_dec = [k[i] * jnp.exp(gtot[i] - gc_col[i]) for i in rng]
    new_state = [state[i] * jnp.exp(gtot[i][0:1, :]) + _dot_tn(k_dec[i], v_new[i]) for i in rng]
    return out, new_state


def _gdn_scan_kernel(qf_ref, kf_ref, vf_ref, qb_ref, kb_ref, vb_ref, gf_ref, gb_ref, rf_ref, rb_ref,
                     of_ref, ob_ref, s_scr, *, n_heads, head_dim):
    H, Dh = n_heads, head_dim
    B = qf_ref.shape[0]

    @pl.when(pl.program_id(0) == 0)
    def _():
        s_scr[...] = jnp.zeros_like(s_scr)

    col = lambda g, j: g[:, j:j + 1]
    sls = [slice(h * Dh, (h + 1) * Dh) for h in range(H)]
    q, k, v, beta, gc_col, gc_row, gtot, state = ([] for _ in range(8))
    for b in range(B):
        gf = gf_ref[b]
        gb = gb_ref[b]
        q += [qf_ref[b, :, sl] for sl in sls] + [qb_ref[b, :, sl] for sl in sls]
        k += [kf_ref[b, :, sl] for sl in sls] + [kb_ref[b, :, sl] for sl in sls]
        v += [vf_ref[b, :, sl] for sl in sls] + [vb_ref[b, :, sl] for sl in sls]
        beta += [col(gf, h) for h in range(H)] + [col(gb, H + h) for h in range(H)]
        gc_col += [col(gf, 2 * H + h) for h in range(H)] + [col(gb, 3 * H + h) for h in range(H)]
        gc_row += [rf_ref[b, h:h + 1, :] for h in range(H)] + [rb_ref[b, H + h:H + h + 1, :] for h in range(H)]
        gtot += [col(gf, 6 * H + h) for h in range(H)] + [col(gb, 7 * H + h) for h in range(H)]
        state += [s_scr[b, 0, h] for h in range(H)] + [s_scr[b, 1, h] for h in range(H)]
    out, new_state = _delta_chunks(q, k, v, beta, gc_col, gc_row, gtot, state, ([True] * H + [False] * H) * B)
    for b in range(B):
        for h in range(H):
            of_ref[b, :, sls[h]] = out[2 * H * b + h]
            ob_ref[b, :, sls[h]] = out[2 * H * b + H + h]
            s_scr[b, 0, h] = new_state[2 * H * b + h]
            s_scr[b, 1, h] = new_state[2 * H * b + H + h]


def _gdn_scan(qkv, gates, gates_row, n_heads, head_dim):
    B, L, _ = qkv.shape
    H, Dh = n_heads, head_dim
    d = H * Dh
    C = GDN_CHUNK
    N = L // C
    fwd = lambda col: pl.BlockSpec((B, C, d), lambda n: (0, n, col))
    bwd = lambda col: pl.BlockSpec((B, C, d), lambda n: (0, N - 1 - n, col))
    return pl.pallas_call(
        functools.partial(_gdn_scan_kernel, n_heads=H, head_dim=Dh),
        grid=(N,),
        in_specs=[fwd(0), fwd(1), fwd(2), bwd(0), bwd(1), bwd(2),
                  pl.BlockSpec((B, C, LANES), lambda n: (0, n, 0)),
                  pl.BlockSpec((B, C, LANES), lambda n: (0, N - 1 - n, 0)),
                  pl.BlockSpec((B, None, 2 * H, C), lambda n: (0, n, 0, 0)),
                  pl.BlockSpec((B, None, 2 * H, C), lambda n: (0, N - 1 - n, 0, 0))],
        out_specs=[pl.BlockSpec((B, C, d), lambda n: (0, n, 0)),
                   pl.BlockSpec((B, C, d), lambda n: (0, N - 1 - n, 0))],
        out_shape=[jax.ShapeDtypeStruct((B, L, d), F32)] * 2,
        scratch_shapes=[pltpu.VMEM((B, 2, H, Dh, Dh), F32)],
        compiler_params=_cparams(("arbitrary",), 32),
        name="gdn_scan",
    )(qkv, qkv, qkv, qkv, qkv, qkv, gates, gates, gates_row, gates_row)


def _gdn_branch(p3, pg, conv_w, a_log_f, a_log_b, dt_bias_f, dt_bias_b, col0, n_heads, head_dim):
    B, L, _ = p3.shape
    H = n_heads
    qkv = _gdn_pre(p3, conv_w, col0, H, head_dim)
    gates = _gdn_gates(pg, a_log_f, a_log_b, dt_bias_f, dt_bias_b, H).reshape(B, L, LANES)
    N = L // GDN_CHUNK
    gates_row = gates[..., 2 * H:4 * H].reshape(B, N, GDN_CHUNK, 2 * H).transpose(0, 1, 3, 2)
    return _gdn_scan(qkv, gates, gates_row, H, head_dim)


@functools.lru_cache(maxsize=None)
def _filter_positions(L, pos_emb_dim):
    n = 2 * L
    r = np.arange(n)
    k = np.where(r < L, r, np.where(r == L, 0, n - r)).astype(np.float64)
    t = k / (L - 1)
    bands = (pos_emb_dim - 1) // 2
    fb = np.linspace(1e-4, bands - 1, bands)
    ang = (2.0 * math.pi / L) * k[:, None] * fb[None, :]
    z = np.concatenate([t[:, None], np.cos(ang), -np.sin(ang)], axis=-1)
    return z.astype(np.float32)


@functools.lru_cache(maxsize=None)
def _decay_rates(d_hy):
    max_decay = math.log(DECAY_TARGET) / FAST_DECAY_PCT
    min_decay = math.log(DECAY_TARGET) / SLOW_DECAY_PCT
    return np.abs(np.linspace(min_decay, max_decay, d_hy)).astype(np.float32).reshape(1, d_hy)


def _filt_mlp_kernel(zt_ref, w1t_ref, b1_ref, w2t_ref, b2_ref, w3t_ref, b3_ref, fr_ref, o_ref):
    fr = fr_ref[...]
    h = jnp.sin(fr * (_dot_hi(w1t_ref[...], zt_ref[...]) + b1_ref[...]))
    h = jnp.sin(fr * (_dot_hi(w2t_ref[...], h) + b2_ref[...]))
    o_ref[...] = jnp.sin(fr * (_dot_hi(w3t_ref[...], h) + b3_ref[...]))


def _filt_mlp(zt, w1, b1, w2, b2, w3, b3, freq, tc=1024):
    pe, n = zt.shape
    fw = w1.shape[1]
    col = lambda a: a.reshape(-1, 1).astype(F32)
    full = lambda a: pl.BlockSpec(a.shape, lambda i: (0, 0))
    args = (zt, w1.T, col(b1), w2.T, col(b2), w3.T, col(b3), col(freq))
    return pl.pallas_call(
        _filt_mlp_kernel,
        grid=(n // tc,),
        in_specs=[pl.BlockSpec((pe, tc), lambda i: (0, i))] + [full(a) for a in args[1:]],
        out_specs=pl.BlockSpec((fw, tc), lambda i: (0, i)),
        out_shape=jax.ShapeDtypeStruct((fw, n), F32),
        compiler_params=_cparams(("parallel",), 24),
        name="filt_mlp",
    )(*args)


@functools.lru_cache(maxsize=None)
def _dft_tables(L):
    n = 2 * L
    N2 = FFT_N2
    N1 = n // N2
    N1h = N1 // 2
    j2 = np.arange(N2)[:, None, None]
    k1 = np.arange(N1)[None, :, None]

    def stage1(n_j1):
        j1 = np.arange(n_j1)[None, None, :]
        m = (k1 * (N2 * j1 + j2)) % n
        th = 2.0 * np.pi * m / n
        return np.cos(th), np.sin(th)

    c, s = stage1(N1h)
    t1 = np.concatenate([np.concatenate([c, s], axis=2), np.concatenate([-s, c], axis=2)], axis=1)
    c, s = stage1(N1)
    t1g = np.concatenate([c, -s], axis=1)
    c, s = stage1(N1h)
    ct, st = np.swapaxes(c, 1, 2) / n, np.swapaxes(s, 1, 2) / n
    t2 = np.concatenate([np.concatenate([ct, -st], axis=2), np.concatenate([st, ct], axis=2)], axis=1)
    a = np.arange(N2)
    th = 2.0 * np.pi * ((a[:, None] * a[None, :]) % N2) / N2
    c2, s2 = np.cos(th), np.sin(th)
    f2f = np.block([[c2, s2], [-s2, c2]])
    f2i = np.block([[c2, -s2], [s2, c2]])
    as_bf16 = lambda x: jnp.asarray(x, dtype=F32).astype(BF16)
    return dict(N1=N1, N2=N2, t1=t1.astype(np.float32), t1g=t1g.astype(np.float32), t2=t2.astype(np.float32),
                f2f=f2f.astype(np.float32), f2i=f2i.astype(np.float32))


FILT_ROWS = 512
FFT_UNROLL = 8
PITCH_PAD = 8


def _filt_fft_kernel(h3_ref, w4f_ref, w4b_ref, delta_ref, t1g_ref, f2f_ref, hspec_ref, hb0_ref, g_scr, a_scr,
                     *, L, N1, N2):
    n = 2 * L
    gp = N2 + PITCH_PAD
    ap = 2 * N1 + PITCH_PAD
    delta = delta_ref[...]
    hb0_ref[...] = jnp.zeros_like(hb0_ref)

    def gen(c, carry):
        r0 = pl.multiple_of(c * FILT_ROWS, FILT_ROWS)
        row = r0 + lax.broadcasted_iota(jnp.int32, (FILT_ROWS, LANES), 0)
        lag = jnp.where(row < L, row, jnp.where(row == L, 0, n - row))
        window = jnp.exp(-(lag.astype(F32) * (1.0 / (L - 1))) * delta)
        w4 = jnp.where(r0 < L, w4f_ref[...], w4b_ref[...])
        g = _dot(h3_ref[pl.ds(r0, FILT_ROWS), :], w4) * window
        at_l = row == L
        hb0_ref[...] += jnp.sum(jnp.where(at_l, g, 0.0), axis=0, keepdims=True)
        g = jnp.where(at_l, 0.0, g)
        for q in range(FILT_ROWS // N2):
            dst = pl.multiple_of((c * (FILT_ROWS // N2) + q) * gp, SUBLANES)
            g_scr[pl.ds(dst, N2), :] = g[q * N2:(q + 1) * N2]
        return carry

    lax.fori_loop(0, n // FILT_ROWS, gen, 0)

    def stage1(j2, carry):
        x = g_scr[pl.ds(j2, N1, stride=gp), :]
        a_scr[pl.ds(pl.multiple_of(j2 * ap, SUBLANES), 2 * N1), :] = _dot(t1g_ref[j2], x)
        return carry

    lax.fori_loop(0, N2, stage1, 0, unroll=2 * FFT_UNROLL)

    def stage2(k1, carry):
        ar = a_scr[pl.ds(k1, N2, stride=ap), :]
        ai = a_scr[pl.ds(N1 + k1, N2, stride=ap), :]
        z = _dot(f2f_ref[...], jnp.concatenate([ar, ai], axis=0))
        hspec_ref[pl.ds(pl.multiple_of(k1 * 2 * N2, 2 * N2), 2 * N2), :] = z.astype(hspec_ref.dtype)
        return carry

    lax.fori_loop(0, N1, stage2, 0, unroll=FFT_UNROLL)


def _filt_fft(h3, w4, L, d_hy):
    tb = _dft_tables(L)
    N1, N2 = tb["N1"], tb["N2"]
    n = 2 * L
    fw = h3.shape[1]
    nt = d_hy // LANES
    t1g = jnp.asarray(tb["t1g"]).astype(BF16)
    f2f = jnp.asarray(tb["f2f"]).astype(BF16)
    return pl.pallas_call(
        functools.partial(_filt_fft_kernel, L=L, N1=N1, N2=N2),
        grid=(nt,),
        in_specs=[pl.BlockSpec((n, fw), lambda c: (0, 0)),
                  pl.BlockSpec((fw, LANES), lambda c: (0, c)),
                  pl.BlockSpec((fw, LANES), lambda c: (0, c + nt)),
                  pl.BlockSpec((1, LANES), lambda c: (0, c)),
                  pl.BlockSpec(t1g.shape, lambda c: (0, 0, 0)),
                  pl.BlockSpec(f2f.shape, lambda c: (0, 0))],
        out_specs=[pl.BlockSpec((2 * n, LANES), lambda c: (0, c)),
                   pl.BlockSpec((SUBLANES, LANES), lambda c: (0, c))],
        out_shape=[jax.ShapeDtypeStruct((2 * n, d_hy), BF16), jax.ShapeDtypeStruct((SUBLANES, d_hy), F32)],
        scratch_shapes=[pltpu.VMEM((N1 * (N2 + PITCH_PAD), LANES), F32),
                        pltpu.VMEM((N2 * (2 * N1 + PITCH_PAD), LANES), F32)],
        compiler_params=_cparams(("parallel",), 48),
        name="filt_fft",
    )(h3, w4, w4, jnp.asarray(_decay_rates(d_hy)), t1g, f2f)


def _hy_conv_kernel(vg_ref, hspec_ref, skip_ref, hb0_ref, t1_ref, f2f_ref, f2i_ref, t2_ref, y_ref,
                    x_scr, a_scr, b_scr, *, N1, N2):
    N1h = N1 // 2
    xp = N2 + PITCH_PAD
    ap = 2 * N1 + PITCH_PAD
    bp = 2 * N2 + PITCH_PAD

    for b in range(2):
        for j1 in range(N1h):
            x_scr[b, pl.ds(j1 * xp, N2), :] = vg_ref[b, pl.ds(j1 * N2, N2), :].astype(F32)

    def stage1(j2, carry):
        x = jnp.concatenate([x_scr[0, pl.ds(j2, N1h, stride=xp), :],
                             x_scr[1, pl.ds(j2, N1h, stride=xp), :]], axis=0)
        a_scr[pl.ds(pl.multiple_of(j2 * ap, SUBLANES), 2 * N1), :] = _dot(t1_ref[j2], x)
        return carry

    lax.fori_loop(0, N2, stage1, 0, unroll=2 * FFT_UNROLL)

    def stage2(k1, carry):
        ar = a_scr[pl.ds(k1, N2, stride=ap), :]
        ai = a_scr[pl.ds(N1 + k1, N2, stride=ap), :]
        z = _dot(f2f_ref[...], jnp.concatenate([ar, ai], axis=0))
        zr, zi = z[:N2], z[N2:]
        base = pl.multiple_of(k1 * 2 * N2, 2 * N2)
        hr = hspec_ref[pl.ds(base, N2), :].astype(F32)
        hi = hspec_ref[pl.ds(base + N2, N2), :].astype(F32)
        prod = jnp.concatenate([zr * hr - zi * hi, zr * hi + zi * hr], axis=0)
        b_scr[pl.ds(pl.multiple_of(k1 * bp, SUBLANES), 2 * N2), :] = _dot(f2i_ref[...], prod)
        return carry

    lax.fori_loop(0, N1, stage2, 0, unroll=FFT_UNROLL)

    skip = skip_ref[...] + hb0_ref[0:1, :]

    def stage3(j2, carry):
        b = jnp.concatenate([b_scr[pl.ds(j2, N1, stride=bp), :],
                             b_scr[pl.ds(N2 + j2, N1, stride=bp), :]], axis=0)
        y = _dot(t2_ref[j2], b)
        x_scr[0, pl.ds(j2, N1h, stride=xp), :] = y[:N1h] + x_scr[0, pl.ds(j2, N1h, stride=xp), :] * skip
        x_scr[1, pl.ds(j2, N1h, stride=xp), :] = y[N1h:] + x_scr[1, pl.ds(j2, N1h, stride=xp), :] * skip
        return carry

    lax.fori_loop(0, N2, stage3, 0, unroll=2 * FFT_UNROLL)

    for b in range(2):
        for j1 in range(N1h):
            y_ref[b, pl.ds(j1 * N2, N2), :] = x_scr[b, pl.ds(j1 * xp, N2), :]


def _hy_conv(vg, hspec, skip, hb0):
    B, L, d_hy = vg.shape
    assert B % 2 == 0
    tb = _dft_tables(L)
    N1, N2 = tb["N1"], tb["N2"]
    n = 2 * L
    nt = d_hy // LANES
    t1, t2 = (jnp.asarray(tb[k]).astype(BF16) for k in ("t1", "t2"))
    f2f, f2i = (jnp.asarray(tb[k]).astype(BF16) for k in ("f2f", "f2i"))
    const3 = lambda a: pl.BlockSpec(a.shape, lambda b, c: (0, 0, 0))
    const2 = lambda a: pl.BlockSpec(a.shape, lambda b, c: (0, 0))
    return pl.pallas_call(
        functools.partial(_hy_conv_kernel, N1=N1, N2=N2),
        grid=(B // 2, nt),
        in_specs=[pl.BlockSpec((2, L, LANES), lambda b, c: (b, 0, c)),
                  pl.BlockSpec((2 * n, LANES), lambda b, c: (0, c)),
                  pl.BlockSpec((1, LANES), lambda b, c: (0, c)),
                  pl.BlockSpec((SUBLANES, LANES), lambda b, c: (0, c)),
                  const3(t1), const2(f2f), const2(f2i), const3(t2)],
        out_specs=pl.BlockSpec((2, L, LANES), lambda b, c: (b, 0, c)),
        out_shape=jax.ShapeDtypeStruct((B, L, d_hy), F32),
        scratch_shapes=[pltpu.VMEM((2, (N1 // 2) * (N2 + PITCH_PAD), LANES), F32),
                        pltpu.VMEM((N2 * (2 * N1 + PITCH_PAD), LANES), F32),
                        pltpu.VMEM((N1 * (2 * N2 + PITCH_PAD), LANES), F32)],
        compiler_params=_cparams(("parallel", "parallel"), 58),
        name="hy_conv",
    )(vg, hspec, skip.reshape(1, d_hy).astype(F32), hb0, t1, f2f, f2i, t2)


def _hyena_branch(p3, conv_w, conv_b, fw1, fb1, fw2, fb2, fw3, fb3, fw4, freq, skip, d_hy):
    B, L, _ = p3.shape
    x0c, vg = _hy_pre(p3, conv_w, conv_b, d_hy)
    zt = jnp.asarray(_filter_positions(L, fw1.shape[0]).T)
    h3 = _filt_mlp(zt, fw1, fb1, fw2, fb2, fw3, fb3, freq).T
    hspec, hb0 = _filt_fft(h3, fw4, L, d_hy)
    return x0c, _hy_conv(vg, hspec, skip, hb0)


ROUTE_GATE, ROUTE_EXPERT, ROUTE_RANK = 0, 2, 4
MIX_ROWS = 256


def _mix_route_kernel(yc_ref, x0_ref, of_ref, ob_ref, z_ref, hnw_ref, gnw_ref, x_ref, wo_ref, n2w_ref, wr_ref, br_ref,
                      x2_ref, u_ref, route_ref, route_t_ref, cnt_ref, run_scr, *, n_heads, head_dim, n_groups,
                      per_group):
    tm = x_ref.shape[0]
    G, P = n_groups, per_group

    @pl.when(pl.program_id(0) == 0)
    def _():
        run_scr[...] = jnp.zeros_like(run_scr)

    run = run_scr[0:1, :]
    blocks = [pl.ds(s * MIX_ROWS, MIX_ROWS) for s in range(tm // MIX_ROWS)]
    us = []
    for rs in blocks:
        yh = yc_ref[rs, :] * x0_ref[rs, :].astype(F32)
        yh = yh * lax.rsqrt(jnp.mean(yh * yh, axis=-1, keepdims=True) + EPS) * hnw_ref[...]
        parts = [yh.astype(BF16)]
        for h in range(n_heads):
            sl = slice(h * head_dim, (h + 1) * head_dim)
            o = of_ref[rs, sl] + ob_ref[rs, sl]
            z = z_ref[rs, sl].astype(F32)
            o = o * lax.rsqrt(jnp.mean(o * o, axis=-1, keepdims=True) + EPS) * gnw_ref[...] * _silu(z)
            parts.append(o.astype(BF16))
        ymix = jnp.concatenate(parts, axis=-1)
        x2 = x_ref[rs, :] + jnp.dot(ymix, wo_ref[...], preferred_element_type=F32)
        x2_ref[rs, :] = x2
        u = x2 * lax.rsqrt(jnp.mean(x2 * x2, axis=-1, keepdims=True) + EPS) * n2w_ref[...]
        u_ref[rs, :] = _pack_bf16_pairs(u)
        us.append(u)

    all_logits = [_dot(u, wr_ref[...]) + br_ref[...] for u in us]
    for s, (rs, logits) in enumerate(zip(blocks, all_logits)):
        lane = lax.broadcasted_iota(jnp.int32, logits.shape, 1)
        neg = jnp.float32(-jnp.inf)
        big = jnp.int32(4 * LANES)
        first = lambda hit: jnp.min(jnp.where(hit, lane, big), axis=-1, keepdims=True)
        gl = jnp.where(lane < G, logits, neg)
        gmax = jnp.max(gl, axis=-1, keepdims=True)
        gidx = first(gl == gmax)
        grp_gate = 1.0 / jnp.sum(jnp.exp(gl - gmax), axis=-1, keepdims=True)
        in_grp = (lane >= G) & (lane < G + G * P) & (((lane - G) // P) == gidx)
        ll = jnp.where(in_grp, logits, neg)
        m1 = jnp.max(ll, axis=-1, keepdims=True)
        i1 = first(ll == m1)
        denom = jnp.sum(jnp.exp(ll - m1), axis=-1, keepdims=True)
        ll2 = jnp.where(lane == i1, neg, ll)
        m2 = jnp.max(ll2, axis=-1, keepdims=True)
        i2 = first(ll2 == m2)
        p1 = 1.0 / denom
        p2 = jnp.exp(m2 - m1) / denom
        gate1 = grp_gate * (p1 / (p1 + p2))
        gate2 = grp_gate * (p2 / (p1 + p2))
        e1 = i1 - G
        e2 = i2 - G

        oh1 = jnp.where(lane == e1, 1.0, 0.0)
        oh2 = jnp.where(lane == e2, 1.0, 0.0)
        oh = oh1 + oh2
        ii = lax.broadcasted_iota(jnp.int32, (MIX_ROWS, MIX_ROWS), 0)
        jj = lax.broadcasted_iota(jnp.int32, (MIX_ROWS, MIX_ROWS), 1)
        before = _dot(jnp.where(ii > jj, 1.0, 0.0), oh) + run
        r1 = jnp.sum(oh1 * before, axis=-1, keepdims=True)
        r2 = jnp.sum(oh2 * before, axis=-1, keepdims=True)
        run = run + jnp.sum(oh, axis=0, keepdims=True)

        rec = jnp.where(lane == ROUTE_GATE, gate1, 0.0)
        rec = jnp.where(lane == ROUTE_GATE + 1, gate2, rec)
        rec = jnp.where(lane == ROUTE_EXPERT, e1.astype(F32), rec)
        rec = jnp.where(lane == ROUTE_EXPERT + 1, e2.astype(F32), rec)
        rec = jnp.where(lane == ROUTE_RANK, r1, rec)
        rec = jnp.where(lane == ROUTE_RANK + 1, r2, rec)
        route_ref[rs, :] = rec
        route_t_ref[:, s * MIX_ROWS:(s + 1) * MIX_ROWS] = jnp.transpose(rec)[:SUBLANES, :]

    run_scr[...] = jnp.broadcast_to(run, run_scr.shape)
    cnt_ref[...] = run_scr[...]


def _mix_route(yconv, x0c, o_f, o_b, p, z_col, hy_norm_w, gdn_norm_w, xf, w_out_bf16, norm2_w, wr, br,
               n_heads, head_dim, n_groups, per_group, tm=512):
    M, D = xf.shape
    d_hy = yconv.shape[1]
    d_gdn = o_f.shape[1]
    assert z_col % d_gdn == 0 and n_groups * (per_group + 1) <= LANES
    zb = z_col // d_gdn
    row = lambda i: (i, 0)
    const = lambda i: (0, 0)
    kern = functools.partial(_mix_route_kernel, n_heads=n_heads, head_dim=head_dim,
                             n_groups=n_groups, per_group=per_group)
    return pl.pallas_call(
        kern,
        grid=(M // tm,),
        in_specs=[pl.BlockSpec((tm, d_hy), row), pl.BlockSpec((tm, d_hy), row),
                  pl.BlockSpec((tm, d_gdn), row), pl.BlockSpec((tm, d_gdn), row),
                  pl.BlockSpec((tm, d_gdn), lambda i: (i, zb)),
                  pl.BlockSpec((1, d_hy), const), pl.BlockSpec((1, head_dim), const),
                  pl.BlockSpec((tm, D), row), pl.BlockSpec(w_out_bf16.shape, const, pipeline_mode=pl.Buffered(1)),
                  pl.BlockSpec((1, D), const), pl.BlockSpec((D, LANES), const), pl.BlockSpec((1, LANES), const)],
        out_specs=[pl.BlockSpec((tm, D), row), pl.BlockSpec((tm, D // 2), row),
                   pl.BlockSpec((tm, LANES), row), pl.BlockSpec((SUBLANES, tm), lambda i: (0, i)),
                   pl.BlockSpec((SUBLANES, LANES), const)],
        out_shape=[jax.ShapeDtypeStruct((M, D), F32), jax.ShapeDtypeStruct((M, D // 2), jnp.uint32),
                   jax.ShapeDtypeStruct((M, LANES), F32), jax.ShapeDtypeStruct((SUBLANES, M), F32),
                   jax.ShapeDtypeStruct((SUBLANES, LANES), F32)],
        scratch_shapes=[pltpu.VMEM((SUBLANES, LANES), F32)],
        compiler_params=_cparams(("arbitrary",), 58),
        name="mix_route",
    )(yconv, x0c, o_f, o_b, p, hy_norm_w.reshape(1, d_hy), gdn_norm_w.reshape(1, head_dim), xf, w_out_bf16,
      norm2_w.reshape(1, D), wr, br)


def _experts_kernel(ts_ref, se_ref, nn_ref, tgt_nxt_ref, tgt_ref, tgt_prv_ref, u_hbm, w1_hbm, w3_hbm, w2_hbm, out_hbm,
                    x0, x1, y0, y1, wf1, wf3, wf2, w1b, w3b, w2b, gsem, ssem, wsem, *, n_tokens):
    i = pl.program_id(0)
    na = nn_ref[0]
    ns = nn_ref[1]
    TB = x0.shape[0]
    spare0 = 2 * n_tokens
    W_SLOTS = wf1.shape[0]

    def weight_copies(s):
        slot = lax.rem(s, W_SLOTS)
        e = se_ref[s]
        return [pltpu.make_async_copy(w_hbm.at[e], wf.at[slot], wsem.at[slot])
                for w_hbm, wf in ((w1_hbm, wf1), (w3_hbm, wf3), (w2_hbm, wf2))]

    def token_of(v):
        if n_tokens & (n_tokens - 1) == 0:
            return v & (n_tokens - 1)
        return lax.rem(v, n_tokens)

    def gather(tgt, xbuf, s, r):
        return pltpu.make_async_copy(u_hbm.at[pl.ds(token_of(tgt[r]), 1)], xbuf.at[pl.ds(r, 1)], gsem.at[s])

    def scatter(ybuf, s, r, dst):
        return pltpu.make_async_copy(ybuf.at[pl.ds(r, 1)], out_hbm.at[pl.ds(dst, 1)], ssem.at[s])

    def gather_wait(xbuf, s):
        pltpu.make_async_copy(u_hbm.at[pl.ds(0, TB)], xbuf, gsem.at[s]).wait()

    def scatter_wait(ybuf, s):
        pltpu.make_async_copy(ybuf, out_hbm.at[pl.ds(0, TB)], ssem.at[s]).wait()

    def each_row(fn):
        def body(r, carry):
            fn(r)
            return carry
        lax.fori_loop(0, TB, body, 0, unroll=8)

    def step(p):
        q = 1 - p
        x_cur, x_nxt = (x0, x1) if p == 0 else (x1, x0)
        y_cur, y_prv = (y0, y1) if p == 0 else (y1, y0)
        gather_wait(x_cur, p)

        @pl.when(i > 0)
        def _():
            scatter_wait(y_cur, p)

        x = _unpack_bf16_pairs(x_cur[...]).astype(BF16)
        for r in range(TB):
            gather(tgt_nxt_ref, x_nxt, q, r).start()
        first = i == 0
        for r in range(TB):
            scatter(y_prv, q, r, jnp.where(first, spare0 + r, tgt_prv_ref[r])).start()
        h = _silu(jnp.dot(x, w1b[...], preferred_element_type=F32)) * jnp.dot(x, w3b[...], preferred_element_type=F32)
        y_cur[...] = _pack_bf16_pairs(jnp.dot(h.astype(BF16), w2b[...], preferred_element_type=F32))

        @pl.when(i == na - 1)
        def _():
            gather_wait(x_nxt, q)
            scatter_wait(y_prv, q)
            each_row(lambda r: scatter(y_cur, p, r, tgt_ref[r]).start())
            scatter_wait(y_cur, p)

    @pl.when(i < na)
    def _():
        @pl.when(i == 0)
        def _():
            y0[...] = jnp.zeros_like(y0)
            y1[...] = jnp.zeros_like(y1)
            fill = pltpu.make_async_copy(y1, out_hbm.at[pl.ds(spare0, TB)], ssem.at[1])
            fill.start()
            fill.wait()
            each_row(lambda r: gather(tgt_ref, x0, 0, r).start())
            for s0 in range(W_SLOTS - 1):
                @pl.when(s0 < ns)
                def _():
                    for c in weight_copies(s0):
                        c.start()

        s = ts_ref[i]

        @pl.when((i == 0) | (s != ts_ref[jnp.maximum(i - 1, 0)]))
        def _():
            @pl.when(s + W_SLOTS - 1 < ns)
            def _():
                for c in weight_copies(s + W_SLOTS - 1):
                    c.start()

            for c in weight_copies(s):
                c.wait()
            slot = lax.rem(s, W_SLOTS)
            w1b[...] = wf1[slot].astype(BF16)
            w3b[...] = wf3[slot].astype(BF16)
            w2b[...] = wf2[slot].astype(BF16)

        parity = lax.rem(i, 2)

        @pl.when(parity == 0)
        def _():
            step(0)

        @pl.when(parity == 1)
        def _():
            step(1)


WEIGHT_SLOTS = 3


def _experts(u, slot_tgt, tile_seq, seq_expert, n_active_seq, w1, w3, w2):
    T = u.shape[0]
    E, D, de = w1.shape
    TB = EXPERT_ROWS
    n_tiles = slot_tgt.shape[0] // TB
    table = lambda f: pl.BlockSpec((TB,), lambda i, ts, se, nn: (f(i),), memory_space=pltpu.SMEM)
    hbm = pl.BlockSpec(memory_space=pl.ANY)
    grid_spec = pltpu.PrefetchScalarGridSpec(
        num_scalar_prefetch=3,
        grid=(n_tiles,),
        in_specs=[table(lambda i: jnp.minimum(i + 1, n_tiles - 1)), table(lambda i: i),
                  table(lambda i: jnp.maximum(i - 1, 0)), hbm, hbm, hbm, hbm],
        out_specs=hbm,
        scratch_shapes=[pltpu.VMEM((TB, D // 2), jnp.uint32)] * 4
                       + [pltpu.VMEM((WEIGHT_SLOTS, D, de), F32), pltpu.VMEM((WEIGHT_SLOTS, D, de), F32),
                          pltpu.VMEM((WEIGHT_SLOTS, de, D), F32),
                          pltpu.VMEM((D, de), BF16), pltpu.VMEM((D, de), BF16), pltpu.VMEM((de, D), BF16),
                          pltpu.SemaphoreType.DMA((2,)), pltpu.SemaphoreType.DMA((2,)),
                          pltpu.SemaphoreType.DMA((WEIGHT_SLOTS,))],
    )
    return pl.pallas_call(
        functools.partial(_experts_kernel, n_tokens=T),
        grid_spec=grid_spec,
        out_shape=jax.ShapeDtypeStruct((2 * T + TB, D // 2), jnp.uint32),
        compiler_params=_cparams(("arbitrary",), 58),
        name="experts",
    )(tile_seq, seq_expert, n_active_seq, slot_tgt, slot_tgt, slot_tgt, u, w1, w3, w2)


def _slot_table_kernel(dest_ref, init_hbm, o_ref, sem):
    fill = pltpu.make_async_copy(init_hbm, o_ref, sem)
    fill.start()
    fill.wait()

    def put(a, carry):
        o_ref[dest_ref[a]] = a
        return carry

    lax.fori_loop(0, dest_ref.shape[0], put, 0, unroll=8)


def _slot_table(dest, init):
    smem = pl.BlockSpec(memory_space=pltpu.SMEM)
    return pl.pallas_call(
        _slot_table_kernel,
        in_specs=[smem, pl.BlockSpec(memory_space=pl.ANY)],
        out_specs=smem,
        out_shape=jax.ShapeDtypeStruct(init.shape, jnp.int32),
        scratch_shapes=[pltpu.SemaphoreType.DMA],
        name="slot_table",
    )(dest, init)


def _dispatch_tables(route_t, counts, n_experts):
    T = route_t.shape[1]
    TB = EXPERT_ROWS
    e = route_t[ROUTE_EXPERT:ROUTE_EXPERT + 2].astype(jnp.int32)
    rank = route_t[ROUTE_RANK:ROUTE_RANK + 2].astype(jnp.int32)
    cnt = counts[0, :n_experts].astype(jnp.int32)
    padded = (cnt + TB - 1) // TB * TB
    pad_end = jnp.cumsum(padded)
    pad_start = pad_end - padded
    ids = jnp.arange(n_experts, dtype=jnp.int32)[:, None, None]
    start_of = jnp.sum(jnp.where(e[None] == ids, pad_start[:, None, None], 0), axis=0)
    dest = start_of + rank
    n_tiles = -(-(2 * T + n_experts * (TB - 1)) // TB)
    padded_slot = 2 * T + jnp.arange(n_tiles * TB, dtype=jnp.int32) % TB
    slot_tgt = _slot_table(dest.reshape(-1), padded_slot)
    start = jnp.arange(n_tiles, dtype=jnp.int32) * TB
    tile_expert = jnp.sum(start[:, None] >= pad_end[None, :], axis=1)
    tile_expert = jnp.minimum(tile_expert, n_experts - 1).astype(jnp.int32)
    owns = cnt > 0
    pos = jnp.cumsum(owns.astype(jnp.int32)) - 1
    experts = jnp.arange(n_experts, dtype=jnp.int32)
    seq_expert = jnp.sum(jnp.where(owns[None, :] & (pos[None, :] == experts[:, None]), experts[None, :], 0), axis=1)
    tile_seq = jnp.sum(jnp.where(tile_expert[:, None] == experts[None, :], pos[None, :], 0), axis=1)
    n_active_seq = jnp.stack([pad_end[-1] // TB, jnp.sum(owns.astype(jnp.int32))]).astype(jnp.int32)
    return slot_tgt, tile_seq.astype(jnp.int32), seq_expert.astype(jnp.int32), n_active_seq


def _combine_kernel(x2_ref, e0_ref, e1_ref, route_ref, w_ref, o_ref, *, final_norm):
    r = route_ref[...]
    e0 = _unpack_bf16_pairs(e0_ref[...])
    e1 = _unpack_bf16_pairs(e1_ref[...])
    y = x2_ref[...] + r[:, ROUTE_GATE:ROUTE_GATE + 1] * e0 + r[:, ROUTE_GATE + 1:ROUTE_GATE + 2] * e1
    if final_norm:
        y = y * lax.rsqrt(jnp.mean(y * y, axis=-1, keepdims=True) + EPS) * w_ref[...]
    o_ref[...] = y


def _combine(x2, planes, route, norm_w, final_norm, tm=512):
    M, D = x2.shape
    return pl.pallas_call(
        functools.partial(_combine_kernel, final_norm=final_norm),
        grid=(M // tm,),
        in_specs=[pl.BlockSpec((tm, D), lambda i: (i, 0)),
                  pl.BlockSpec((tm, D // 2), lambda i: (i, 0)),
                  pl.BlockSpec((tm, D // 2), lambda i: (M // tm + i, 0)),
                  pl.BlockSpec((tm, LANES), lambda i: (i, 0)),
                  pl.BlockSpec((1, D), lambda i: (0, 0))],
        out_specs=pl.BlockSpec((tm, D), lambda i: (i, 0)),
        out_shape=jax.ShapeDtypeStruct((M, D), F32),
        compiler_params=_cparams(("parallel",), 48),
        name="combine",
    )(x2, planes, planes, route, norm_w.reshape(1, D))


def kernel(x, norm1_w, w_in, hy_conv_w, hy_conv_b, hy_filt_w1, hy_filt_b1, hy_filt_w2, hy_filt_b2, hy_filt_w3, hy_filt_b3, hy_filt_w4, hy_sin_freq, hy_skip, hy_norm_w, gdn_conv_w, gdn_a_log_f, gdn_a_log_b, gdn_dt_bias_f, gdn_dt_bias_b, gdn_norm_w, w_out, norm2_w, router_group_w, router_group_b, router_expert_w, router_expert_b, exp_w1, exp_w3, exp_w2, final_norm_w):
    B, L, D = x.shape
    M = B * L
    depth = w_in.shape[0]
    d_hy = hy_skip.shape[-1]
    H = gdn_a_log_f.shape[-1]
    Dh = gdn_norm_w.shape[-1]
    d_gdn = H * Dh
    n_main = 3 * d_hy + 4 * d_gdn
    G = router_group_w.shape[-1]
    E = router_expert_w.shape[-1]
    xf = x.reshape(M, D)
    for l in range(depth):
        p, pg = _inproj(xf, norm1_w[l], jnp.swapaxes(w_in[l], 0, 1), n_main, 4 * H)
        p3 = p.reshape(B, L, n_main)
        x0c, yconv = _hyena_branch(p3, hy_conv_w[l], hy_conv_b[l], hy_filt_w1[l], hy_filt_b1[l], hy_filt_w2[l],
                                   hy_filt_b2[l], hy_filt_w3[l], hy_filt_b3[l], hy_filt_w4[l], hy_sin_freq[l],
                                   hy_skip[l], d_hy)
        o_f, o_b = _gdn_branch(p3, pg, gdn_conv_w[l], gdn_a_log_f[l], gdn_a_log_b[l], gdn_dt_bias_f[l],
                               gdn_dt_bias_b[l], 3 * d_hy, H, Dh)
        wr = jnp.pad(jnp.concatenate([router_group_w[l], router_expert_w[l]], axis=1), ((0, 0), (0, LANES - G - E)))
        br = jnp.pad(jnp.concatenate([router_group_b[l], router_expert_b[l]]), (0, LANES - G - E)).reshape(1, LANES)
        x2, u, route, route_t, counts = _mix_route(
            yconv.reshape(M, d_hy), x0c.reshape(M, d_hy), o_f.reshape(M, d_gdn), o_b.reshape(M, d_gdn), p,
            3 * d_hy + 3 * d_gdn, hy_norm_w[l], gdn_norm_w[l], xf, w_out[l].astype(BF16), norm2_w[l], wr, br,
            H, Dh, G, E // G)
        slot_tgt, tile_seq, seq_expert, n_active_seq = _dispatch_tables(route_t, counts, E)
        planes = _experts(u, slot_tgt, tile_seq, seq_expert, n_active_seq, exp_w1[l], exp_w3[l], exp_w2[l])
        xf = _combine(x2, planes, route, final_norm_w, final_norm=(l == depth - 1))
    return xf.reshape(B, L, D)
```

```python
import functools
import math

import jax
import jax.numpy as jnp
import numpy as np
from jax import lax
from jax.experimental import pallas as pl
from jax.experimental.pallas import tpu as pltpu

F32 = jnp.float32
BF16 = jnp.bfloat16
EPS = 1e-6
LANES = 128
SUBLANES = 8
VMEM_BYTES_V7X = 64 * 1024 * 1024
GDN_CHUNK = 64
FFT_N2 = 128
EXPERT_ROWS = 256
DECAY_TARGET = 1e-2
FAST_DECAY_PCT = 0.3
SLOW_DECAY_PCT = 1.5


def _cparams(sem, vmem_mb):
    return pltpu.CompilerParams(dimension_semantics=sem, vmem_limit_bytes=int(vmem_mb * 1024 * 1024))


def _dot(a, b):
    return jnp.dot(a.astype(BF16), b.astype(BF16), preferred_element_type=F32)


def _dot_nt(a, b):
    return lax.dot_general(a.astype(BF16), b.astype(BF16), (((1,), (1,)), ((), ())), preferred_element_type=F32)


def _dot_tn(a, b):
    return lax.dot_general(a.astype(BF16), b.astype(BF16), (((0,), (0,)), ((), ())), preferred_element_type=F32)


def _dot_hi(a, b):
    return jnp.dot(a, b, preferred_element_type=F32, precision=lax.Precision.HIGHEST)


def _silu(x):
    return x * jax.nn.sigmoid(x)


def _pack_bf16_pairs(x):
    c = x.shape[1] // 2
    lo = lax.bitcast_convert_type(x[:, :c].astype(BF16).astype(F32), jnp.uint32) >> 16
    hi = lax.bitcast_convert_type(x[:, c:].astype(BF16).astype(F32), jnp.uint32) & jnp.uint32(0xFFFF0000)
    return hi | lo


def _unpack_bf16_pairs(w):
    lo = lax.bitcast_convert_type(w << 16, F32)
    hi = lax.bitcast_convert_type(w & jnp.uint32(0xFFFF0000), F32)
    return jnp.concatenate([lo, hi], axis=1)


def _inproj_kernel(x_ref, nw_ref, wt_ref, wgt_ref, p_ref, g_ref, h_scr):
    @pl.when(pl.program_id(1) == 0)
    def _():
        x = x_ref[...]
        h = x * lax.rsqrt(jnp.mean(x * x, axis=-1, keepdims=True) + EPS) * nw_ref[...]
        h_scr[...] = h.astype(BF16)
        g = _dot_nt(h, wgt_ref[...])
        g_ref[...] = jnp.concatenate([g, jnp.zeros((g.shape[0], LANES - g.shape[1]), F32)], axis=1)

    p_ref[...] = _dot_nt(h_scr[...], wt_ref[...]).astype(p_ref.dtype)


def _inproj(xf, norm_w, wt, n_main, n_gate, tm=2048, tn=512):
    M, D = xf.shape
    assert n_main % tn == 0 and M % tm == 0 and n_main % n_gate == 0 and n_gate % SUBLANES == 0
    return pl.pallas_call(
        _inproj_kernel,
        grid=(M // tm, n_main // tn),
        in_specs=[
            pl.BlockSpec((tm, D), lambda i, j: (i, 0), pipeline_mode=pl.Buffered(1)),
            pl.BlockSpec((1, D), lambda i, j: (0, 0)),
            pl.BlockSpec((tn, D), lambda i, j: (j, 0)),
            pl.BlockSpec((n_gate, D), lambda i, j: (n_main // n_gate, 0)),
        ],
        out_specs=[
            pl.BlockSpec((tm, tn), lambda i, j: (i, j)),
            pl.BlockSpec((tm, LANES), lambda i, j: (i, 0)),
        ],
        out_shape=[jax.ShapeDtypeStruct((M, n_main), BF16), jax.ShapeDtypeStruct((M, LANES), F32)],
        scratch_shapes=[pltpu.VMEM((tm, D), BF16)],
        compiler_params=_cparams(("parallel", "arbitrary"), 57),
        name="inproj",
    )(xf, norm_w.reshape(1, D), wt, wt)


def _conv3_rows(ref, r0, rows, w, n_rows):
    cur = ref[pl.ds(r0, rows), :].astype(F32)
    lo = jnp.maximum(r0 - HALO_ROWS, 0)
    hi = jnp.minimum(r0 + rows, n_rows - HALO_ROWS)
    prev_grp = ref[pl.ds(pl.multiple_of(lo, HALO_ROWS), HALO_ROWS), :].astype(F32)
    next_grp = ref[pl.ds(pl.multiple_of(hi, HALO_ROWS), HALO_ROWS), :].astype(F32)
    prev_row = jnp.where(r0 > 0, prev_grp[HALO_ROWS - 1:HALO_ROWS, :], 0.0)
    next_row = jnp.where(r0 + rows < n_rows, next_grp[0:1, :], 0.0)
    row = lax.broadcasted_iota(jnp.int32, cur.shape, 0)
    xm = jnp.where(row == 0, prev_row, pltpu.roll(cur, 1, 0))
    xp = jnp.where(row == rows - 1, next_row, pltpu.roll(cur, rows - 1, 0))
    return xm * w[0:1, :] + cur * w[1:2, :] + xp * w[2:3, :]


CONV_ROWS = 256
HALO_ROWS = 16


def _hy_pre_kernel(x0_ref, x1_ref, v_ref, w0_ref, w1_ref, w2_ref, b0_ref, b1_ref, b2_ref, x0c_ref, vg_ref):
    L = x0_ref.shape[0]
    w0, w1, w2 = w0_ref[...], w1_ref[...], w2_ref[...]
    b0, b1, b2 = b0_ref[...], b1_ref[...], b2_ref[...]

    def body(c, carry):
        r0 = pl.multiple_of(c * CONV_ROWS, CONV_ROWS)
        x0c_ref[pl.ds(r0, CONV_ROWS), :] = (_conv3_rows(x0_ref, r0, CONV_ROWS, w0, L) + b0).astype(x0c_ref.dtype)
        x1c = _conv3_rows(x1_ref, r0, CONV_ROWS, w1, L) + b1
        vc = _conv3_rows(v_ref, r0, CONV_ROWS, w2, L) + b2
        vg_ref[pl.ds(r0, CONV_ROWS), :] = (vc * x1c).astype(vg_ref.dtype)
        return carry

    lax.fori_loop(0, L // CONV_ROWS, body, 0, unroll=2)


def _hy_pre(p3, conv_w, conv_b, d_hy):
    B, L, _ = p3.shape
    nt = d_hy // LANES
    bias = conv_b.reshape(1, -1)
    pspec = lambda off: pl.BlockSpec((None, L, LANES), lambda b, c: (b, 0, c + off))
    wspec = lambda off: pl.BlockSpec((3, LANES), lambda b, c: (0, c + off))
    bspec = lambda off: pl.BlockSpec((1, LANES), lambda b, c: (0, c + off))
    ospec = pl.BlockSpec((None, L, LANES), lambda b, c: (b, 0, c))
    return pl.pallas_call(
        _hy_pre_kernel,
        grid=(B, nt),
        in_specs=[pspec(0), pspec(nt), pspec(2 * nt), wspec(0), wspec(nt), wspec(2 * nt),
                  bspec(0), bspec(nt), bspec(2 * nt)],
        out_specs=[ospec, ospec],
        out_shape=[jax.ShapeDtypeStruct((B, L, d_hy), BF16)] * 2,
        compiler_params=_cparams(("parallel", "parallel"), 40),
        name="hy_pre",
    )(p3, p3, p3, conv_w, conv_w, conv_w, bias, bias, bias)


def _gdn_pre_kernel(x_ref, w_ref, o_ref, *, n_heads, head_dim):
    L = x_ref.shape[0]
    w = w_ref[...]
    c = pl.program_id(1)
    q_scale = jnp.where(c < n_heads, head_dim ** -0.5, 1.0)
    is_qk = c < 2 * n_heads

    def body(i, carry):
        r0 = pl.multiple_of(i * CONV_ROWS, CONV_ROWS)
        y = _silu(_conv3_rows(x_ref, r0, CONV_ROWS, w, L))
        inv = lax.rsqrt(jnp.sum(y * y, axis=-1, keepdims=True) + EPS) * q_scale
        o_ref[pl.ds(r0, CONV_ROWS), :] = y * jnp.where(is_qk, inv, 1.0)
        return carry

    lax.fori_loop(0, L // CONV_ROWS, body, 0, unroll=2)


def _gdn_pre(p3, conv_w, col0, n_heads, head_dim):
    B, L, _ = p3.shape
    assert head_dim == LANES
    nt = 3 * n_heads
    off = col0 // LANES
    return pl.pallas_call(
        functools.partial(_gdn_pre_kernel, n_heads=n_heads, head_dim=head_dim),
        grid=(B, nt),
        in_specs=[pl.BlockSpec((None, L, LANES), lambda b, c: (b, 0, c + off)),
                  pl.BlockSpec((3, LANES), lambda b, c: (0, c))],
        out_specs=pl.BlockSpec((None, L, LANES), lambda b, c: (b, 0, c)),
        out_shape=jax.ShapeDtypeStruct((B, L, nt * LANES), F32),
        compiler_params=_cparams(("parallel", "parallel"), 24),
        name="gdn_pre",
    )(p3, conv_w)


GATE_ROWS = 512


def _gdn_gates_kernel(pg_ref, alog_ref, dtb_ref, o_ref, *, n_heads):
    H = n_heads
    x = pg_ref[...]
    beta = jax.nn.sigmoid(x)
    z = x + dtb_ref[...]
    softplus = jnp.maximum(z, 0.0) + jnp.log1p(jnp.exp(-jnp.abs(z)))
    g = -jnp.exp(alog_ref[...]) * softplus
    pos = lax.broadcasted_iota(jnp.int32, x.shape, 0) & (GDN_CHUNK - 1)
    gc_f = g
    gc_b = g
    step = 1
    while step < GDN_CHUNK:
        gc_f = gc_f + jnp.where(pos >= step, pltpu.roll(gc_f, step, 0), 0.0)
        gc_b = gc_b + jnp.where(pos < GDN_CHUNK - step, pltpu.roll(gc_b, GATE_ROWS - step, 0), 0.0)
        step *= 2
    g_tot = pltpu.roll(gc_f + gc_b - g, 4 * H, 1)
    lane = lax.broadcasted_iota(jnp.int32, x.shape, 1)
    out = jnp.where(lane < 2 * H, beta,
                    jnp.where(lane < 3 * H, gc_f,
                              jnp.where(lane < 4 * H, gc_b,
                                        jnp.where((lane >= 6 * H) & (lane < 8 * H), g_tot, 0.0))))
    o_ref[...] = out


def _gdn_gates(pg, a_log_f, a_log_b, dt_bias_f, dt_bias_b, n_heads):
    M = pg.shape[0]
    H = n_heads
    assert 8 * H <= LANES
    pad = lambda a, b: jnp.concatenate([jnp.zeros((2 * H,), F32), a.astype(F32), b.astype(F32),
                                        jnp.zeros((LANES - 4 * H,), F32)]).reshape(1, LANES)
    return pl.pallas_call(
        functools.partial(_gdn_gates_kernel, n_heads=H),
        grid=(M // GATE_ROWS,),
        in_specs=[pl.BlockSpec((GATE_ROWS, LANES), lambda i: (i, 0)),
                  pl.BlockSpec((1, LANES), lambda i: (0, 0)),
                  pl.BlockSpec((1, LANES), lambda i: (0, 0))],
        out_specs=pl.BlockSpec((GATE_ROWS, LANES), lambda i: (i, 0)),
        out_shape=jax.ShapeDtypeStruct((M, LANES), F32),
        compiler_params=_cparams(("parallel",), 24),
        name="gdn_gates",
    )(pg, pad(a_log_f, a_log_b), pad(dt_bias_f, dt_bias_b))


def _delta_chunks(q, k, v, beta, gc_col, gc_row, gtot, state, lower):
    n = len(q)
    C = q[0].shape[0]
    D = k[0].shape[1]
    ii = lax.broadcasted_iota(jnp.int32, (C, C), 0)
    jj = lax.broadcasted_iota(jnp.int32, (C, C), 1)
    eye = jnp.where(ii == jj, 1.0, 0.0)
    incl = [(ii >= jj) if lo else (ii <= jj) for lo in lower]
    strict = [(ii > jj) if lo else (ii < jj) for lo in lower]
    rng = range(n)
    decay = [jnp.where(incl[i], jnp.exp(jnp.where(incl[i], gc_col[i] - gc_row[i], 0.0)), 0.0) for i in rng]
    kb = [k[i] * beta[i] for i in rng]
    kk = [_dot_nt(kb[i], k[i]) for i in rng]
    qk = [_dot_nt(q[i], k[i]) for i in rng]
    m = [jnp.where(strict[i], -(kk[i] * decay[i]), 0.0) for i in rng]
    r = [eye + m[i] for i in rng]
    m = [_dot(m[i], m[i]) for i in rng]
    for _ in range(int(math.log2(C)) - 2):
        rm = [_dot(jnp.concatenate([r[i], m[i]], axis=0), m[i]) for i in rng]
        r = [r[i] + rm[i][:C] for i in rng]
        m = [rm[i][C:] for i in rng]
    r = [r[i] + _dot(r[i], m[i]) for i in rng]
    eg = [jnp.exp(gc_col[i]) for i in rng]
    wu = [_dot(r[i], jnp.concatenate([kb[i] * eg[i], v[i] * beta[i]], axis=1)) for i in rng]
    ws = [_dot(jnp.concatenate([wu[i][:, :D], q[i] * eg[i]], axis=0), state[i]) for i in rng]
    v_new = [wu[i][:, D:] - ws[i][:C] for i in rng]
    qkm = [jnp.where(incl[i], qk[i] * decay[i], 0.0) for i in rng]
    out = [ws[i][C:] + _dot(qkm[i], v_new[i]) for i in rng]
    k_dec = [k[i] * jnp.exp(gtot[i] - gc_col[i]) for i in rng]
    new_state = [state[i] * jnp.exp(gtot[i][0:1, :]) + _dot_tn(k_dec[i], v_new[i]) for i in rng]
    return out, new_state


def _gdn_scan_kernel(qf_ref, kf_ref, vf_ref, qb_ref, kb_ref, vb_ref, gf_ref, gb_ref, rf_ref, rb_ref,
                     of_ref, ob_ref, s_scr, *, n_heads, head_dim):
    H, Dh = n_heads, head_dim
    B = qf_ref.shape[0]

    @pl.when(pl.program_id(0) == 0)
    def _():
        s_scr[...] = jnp.zeros_like(s_scr)

    col = lambda g, j: g[:, j:j + 1]
    sls = [slice(h * Dh, (h + 1) * Dh) for h in range(H)]
    q, k, v, beta, gc_col, gc_row, gtot, state = ([] for _ in range(8))
    for b in range(B):
        gf = gf_ref[b]
        gb = gb_ref[b]
        q += [qf_ref[b, :, sl] for sl in sls] + [qb_ref[b, :, sl] for sl in sls]
        k += [kf_ref[b, :, sl] for sl in sls] + [kb_ref[b, :, sl] for sl in sls]
        v += [vf_ref[b, :, sl] for sl in sls] + [vb_ref[b, :, sl] for sl in sls]
        beta += [col(gf, h) for h in range(H)] + [col(gb, H + h) for h in range(H)]
        gc_col += [col(gf, 2 * H + h) for h in range(H)] + [col(gb, 3 * H + h) for h in range(H)]
        gc_row += [rf_ref[b, h:h + 1, :] for h in range(H)] + [rb_ref[b, H + h:H + h + 1, :] for h in range(H)]
        gtot += [col(gf, 6 * H + h) for h in range(H)] + [col(gb, 7 * H + h) for h in range(H)]
        state += [s_scr[b, 0, h] for h in range(H)] + [s_scr[b, 1, h] for h in range(H)]
    out, new_state = _delta_chunks(q, k, v, beta, gc_col, gc_row, gtot, state, ([True] * H + [False] * H) * B)
    for b in range(B):
        for h in range(H):
            of_ref[b, :, sls[h]] = out[2 * H * b + h]
            ob_ref[b, :, sls[h]] = out[2 * H * b + H + h]
            s_scr[b, 0, h] = new_state[2 * H * b + h]
            s_scr[b, 1, h] = new_state[2 * H * b + H + h]


def _gdn_scan(qkv, gates, gates_row, n_heads, head_dim):
    B, L, _ = qkv.shape
    H, Dh = n_heads, head_dim
    d = H * Dh
    C = GDN_CHUNK
    N = L // C
    fwd = lambda col: pl.BlockSpec((B, C, d), lambda n: (0, n, col))
    bwd = lambda col: pl.BlockSpec((B, C, d), lambda n: (0, N - 1 - n, col))
    return pl.pallas_call(
        functools.partial(_gdn_scan_kernel, n_heads=H, head_dim=Dh),
        grid=(N,),
        in_specs=[fwd(0), fwd(1), fwd(2), bwd(0), bwd(1), bwd(2),
                  pl.BlockSpec((B, C, LANES), lambda n: (0, n, 0)),
                  pl.BlockSpec((B, C, LANES), lambda n: (0, N - 1 - n, 0)),
                  pl.BlockSpec((B, None, 2 * H, C), lambda n: (0, n, 0, 0)),
                  pl.BlockSpec((B, None, 2 * H, C), lambda n: (0, N - 1 - n, 0, 0))],
        out_specs=[pl.BlockSpec((B, C, d), lambda n: (0, n, 0)),
                   pl.BlockSpec((B, C, d), lambda n: (0, N - 1 - n, 0))],
        out_shape=[jax.ShapeDtypeStruct((B, L, d), F32)] * 2,
        scratch_shapes=[pltpu.VMEM((B, 2, H, Dh, Dh), F32)],
        compiler_params=_cparams(("arbitrary",), 32),
        name="gdn_scan",
    )(qkv, qkv, qkv, qkv, qkv, qkv, gates, gates, gates_row, gates_row)


def _gdn_branch(p3, pg, conv_w, a_log_f, a_log_b, dt_bias_f, dt_bias_b, col0, n_heads, head_dim):
    B, L, _ = p3.shape
    H = n_heads
    qkv = _gdn_pre(p3, conv_w, col0, H, head_dim)
    gates = _gdn_gates(pg, a_log_f, a_log_b, dt_bias_f, dt_bias_b, H).reshape(B, L, LANES)
    N = L // GDN_CHUNK
    gates_row = gates[..., 2 * H:4 * H].reshape(B, N, GDN_CHUNK, 2 * H).transpose(0, 1, 3, 2)
    return _gdn_scan(qkv, gates, gates_row, H, head_dim)


@functools.lru_cache(maxsize=None)
def _filter_positions(L, pos_emb_dim):
    n = 2 * L
    r = np.arange(n)
    k = np.where(r < L, r, np.where(r == L, 0, n - r)).astype(np.float64)
    t = k / (L - 1)
    bands = (pos_emb_dim - 1) // 2
    fb = np.linspace(1e-4, bands - 1, bands)
    ang = (2.0 * math.pi / L) * k[:, None] * fb[None, :]
    z = np.concatenate([t[:, None], np.cos(ang), -np.sin(ang)], axis=-1)
    return z.astype(np.float32)


@functools.lru_cache(maxsize=None)
def _decay_rates(d_hy):
    max_decay = math.log(DECAY_TARGET) / FAST_DECAY_PCT
    min_decay = math.log(DECAY_TARGET) / SLOW_DECAY_PCT
    return np.abs(np.linspace(min_decay, max_decay, d_hy)).astype(np.float32).reshape(1, d_hy)


def _filt_mlp_kernel(zt_ref, w1t_ref, b1_ref, w2t_ref, b2_ref, w3t_ref, b3_ref, fr_ref, o_ref):
    fr = fr_ref[...]
    h = jnp.sin(fr * (_dot_hi(w1t_ref[...], zt_ref[...]) + b1_ref[...]))
    h = jnp.sin(fr * (_dot_hi(w2t_ref[...], h) + b2_ref[...]))
    o_ref[...] = jnp.sin(fr * (_dot_hi(w3t_ref[...], h) + b3_ref[...]))


def _filt_mlp(zt, w1, b1, w2, b2, w3, b3, freq, tc=1024):
    pe, n = zt.shape
    fw = w1.shape[1]
    col = lambda a: a.reshape(-1, 1).astype(F32)
    full = lambda a: pl.BlockSpec(a.shape, lambda i: (0, 0))
    args = (zt, w1.T, col(b1), w2.T, col(b2), w3.T, col(b3), col(freq))
    return pl.pallas_call(
        _filt_mlp_kernel,
        grid=(n // tc,),
        in_specs=[pl.BlockSpec((pe, tc), lambda i: (0, i))] + [full(a) for a in args[1:]],
        out_specs=pl.BlockSpec((fw, tc), lambda i: (0, i)),
        out_shape=jax.ShapeDtypeStruct((fw, n), F32),
        compiler_params=_cparams(("parallel",), 24),
        name="filt_mlp",
    )(*args)


@functools.lru_cache(maxsize=None)
def _dft_tables(L):
    n = 2 * L
    N2 = FFT_N2
    N1 = n // N2
    N1h = N1 // 2
    j2 = np.arange(N2)[:, None, None]
    k1 = np.arange(N1)[None, :, None]

    def stage1(n_j1):
        j1 = np.arange(n_j1)[None, None, :]
        m = (k1 * (N2 * j1 + j2)) % n
        th = 2.0 * np.pi * m / n
        return np.cos(th), np.sin(th)

    c, s = stage1(N1h)
    t1 = np.concatenate([np.concatenate([c, s], axis=2), np.concatenate([-s, c], axis=2)], axis=1)
    c, s = stage1(N1)
    t1g = np.concatenate([c, -s], axis=1)
    c, s = stage1(N1h)
    ct, st = np.swapaxes(c, 1, 2) / n, np.swapaxes(s, 1, 2) / n
    t2 = np.concatenate([np.concatenate([ct, -st], axis=2), np.concatenate([st, ct], axis=2)], axis=1)
    a = np.arange(N2)
    th = 2.0 * np.pi * ((a[:, None] * a[None, :]) % N2) / N2
    c2, s2 = np.cos(th), np.sin(th)
    f2f = np.block([[c2, s2], [-s2, c2]])
    f2i = np.block([[c2, -s2], [s2, c2]])
    as_bf16 = lambda x: jnp.asarray(x, dtype=F32).astype(BF16)
    return dict(N1=N1, N2=N2, t1=t1.astype(np.float32), t1g=t1g.astype(np.float32), t2=t2.astype(np.float32),
                f2f=f2f.astype(np.float32), f2i=f2i.astype(np.float32))


FILT_ROWS = 512
FFT_UNROLL = 8
PITCH_PAD = 8


def _filt_fft_kernel(h3_ref, w4f_ref, w4b_ref, delta_ref, t1g_ref, f2f_ref, hspec_ref, hb0_ref, g_scr, a_scr,
                     *, L, N1, N2):
    n = 2 * L
    gp = N2 + PITCH_PAD
    ap = 2 * N1 + PITCH_PAD
    delta = delta_ref[...]
    hb0_ref[...] = jnp.zeros_like(hb0_ref)

    def gen(c, carry):
        r0 = pl.multiple_of(c * FILT_ROWS, FILT_ROWS)
        row = r0 + lax.broadcasted_iota(jnp.int32, (FILT_ROWS, LANES), 0)
        lag = jnp.where(row < L, row, jnp.where(row == L, 0, n - row))
        window = jnp.exp(-(lag.astype(F32) * (1.0 / (L - 1))) * delta)
        w4 = jnp.where(r0 < L, w4f_ref[...], w4b_ref[...])
        g = _dot(h3_ref[pl.ds(r0, FILT_ROWS), :], w4) * window
        at_l = row == L
        hb0_ref[...] += jnp.sum(jnp.where(at_l, g, 0.0), axis=0, keepdims=True)
        g = jnp.where(at_l, 0.0, g)
        for q in range(FILT_ROWS // N2):
            dst = pl.multiple_of((c * (FILT_ROWS // N2) + q) * gp, SUBLANES)
            g_scr[pl.ds(dst, N2), :] = g[q * N2:(q + 1) * N2]
        return carry

    lax.fori_loop(0, n // FILT_ROWS, gen, 0)

    def stage1(j2, carry):
        x = g_scr[pl.ds(j2, N1, stride=gp), :]
        a_scr[pl.ds(pl.multiple_of(j2 * ap, SUBLANES), 2 * N1), :] = _dot(t1g_ref[j2], x)
        return carry

    lax.fori_loop(0, N2, stage1, 0, unroll=2 * FFT_UNROLL)

    def stage2(k1, carry):
        ar = a_scr[pl.ds(k1, N2, stride=ap), :]
        ai = a_scr[pl.ds(N1 + k1, N2, stride=ap), :]
        z = _dot(f2f_ref[...], jnp.concatenate([ar, ai], axis=0))
        hspec_ref[pl.ds(pl.multiple_of(k1 * 2 * N2, 2 * N2), 2 * N2), :] = z.astype(hspec_ref.dtype)
        return carry

    lax.fori_loop(0, N1, stage2, 0, unroll=FFT_UNROLL)


def _filt_fft(h3, w4, L, d_hy):
    tb = _dft_tables(L)
    N1, N2 = tb["N1"], tb["N2"]
    n = 2 * L
    fw = h3.shape[1]
    nt = d_hy // LANES
    t1g = jnp.asarray(tb["t1g"]).astype(BF16)
    f2f = jnp.asarray(tb["f2f"]).astype(BF16)
    return pl.pallas_call(
        functools.partial(_filt_fft_kernel, L=L, N1=N1, N2=N2),
        grid=(nt,),
        in_specs=[pl.BlockSpec((n, fw), lambda c: (0, 0)),
                  pl.BlockSpec((fw, LANES), lambda c: (0, c)),
                  pl.BlockSpec((fw, LANES), lambda c: (0, c + nt)),
                  pl.BlockSpec((1, LANES), lambda c: (0, c)),
                  pl.BlockSpec(t1g.shape, lambda c: (0, 0, 0)),
                  pl.BlockSpec(f2f.shape, lambda c: (0, 0))],
        out_specs=[pl.BlockSpec((2 * n, LANES), lambda c: (0, c)),
                   pl.BlockSpec((SUBLANES, LANES), lambda c: (0, c))],
        out_shape=[jax.ShapeDtypeStruct((2 * n, d_hy), BF16), jax.ShapeDtypeStruct((SUBLANES, d_hy), F32)],
        scratch_shapes=[pltpu.VMEM((N1 * (N2 + PITCH_PAD), LANES), F32),
                        pltpu.VMEM((N2 * (2 * N1 + PITCH_PAD), LANES), F32)],
        compiler_params=_cparams(("parallel",), 48),
        name="filt_fft",
    )(h3, w4, w4, jnp.asarray(_decay_rates(d_hy)), t1g, f2f)


def _hy_conv_kernel(vg_ref, hspec_ref, skip_ref, hb0_ref, t1_ref, f2f_ref, f2i_ref, t2_ref, y_ref,
                    x_scr, a_scr, b_scr, *, N1, N2):
    N1h = N1 // 2
    xp = N2 + PITCH_PAD
    ap = 2 * N1 + PITCH_PAD
    bp = 2 * N2 + PITCH_PAD

    for b in range(2):
        for j1 in range(N1h):
            x_scr[b, pl.ds(j1 * xp, N2), :] = vg_ref[b, pl.ds(j1 * N2, N2), :].astype(F32)

    def stage1(j2, carry):
        x = jnp.concatenate([x_scr[0, pl.ds(j2, N1h, stride=xp), :],
                             x_scr[1, pl.ds(j2, N1h, stride=xp), :]], axis=0)
        a_scr[pl.ds(pl.multiple_of(j2 * ap, SUBLANES), 2 * N1), :] = _dot(t1_ref[j2], x)
        return carry

    lax.fori_loop(0, N2, stage1, 0, unroll=2 * FFT_UNROLL)

    def stage2(k1, carry):
        ar = a_scr[pl.ds(k1, N2, stride=ap), :]
        ai = a_scr[pl.ds(N1 + k1, N2, stride=ap), :]
        z = _dot(f2f_ref[...], jnp.concatenate([ar, ai], axis=0))
        zr, zi = z[:N2], z[N2:]
        base = pl.multiple_of(k1 * 2 * N2, 2 * N2)
        hr = hspec_ref[pl.ds(base, N2), :].astype(F32)
        hi = hspec_ref[pl.ds(base + N2, N2), :].astype(F32)
        prod = jnp.concatenate([zr * hr - zi * hi, zr * hi + zi * hr], axis=0)
        b_scr[pl.ds(pl.multiple_of(k1 * bp, SUBLANES), 2 * N2), :] = _dot(f2i_ref[...], prod)
        return carry

    lax.fori_loop(0, N1, stage2, 0, unroll=FFT_UNROLL)

    skip = skip_ref[...] + hb0_ref[0:1, :]

    def stage3(j2, carry):
        b = jnp.concatenate([b_scr[pl.ds(j2, N1, stride=bp), :],
                             b_scr[pl.ds(N2 + j2, N1, stride=bp), :]], axis=0)
        y = _dot(t2_ref[j2], b)
        x_scr[0, pl.ds(j2, N1h, stride=xp), :] = y[:N1h] + x_scr[0, pl.ds(j2, N1h, stride=xp), :] * skip
        x_scr[1, pl.ds(j2, N1h, stride=xp), :] = y[N1h:] + x_scr[1, pl.ds(j2, N1h, stride=xp), :] * skip
        return carry

    lax.fori_loop(0, N2, stage3, 0, unroll=2 * FFT_UNROLL)

    for b in range(2):
        for j1 in range(N1h):
            y_ref[b, pl.ds(j1 * N2, N2), :] = x_scr[b, pl.ds(j1 * xp, N2), :]


def _hy_conv(vg, hspec, skip, hb0):
    B, L, d_hy = vg.shape
    assert B % 2 == 0
    tb = _dft_tables(L)
    N1, N2 = tb["N1"], tb["N2"]
    n = 2 * L
    nt = d_hy // LANES
    t1, t2 = (jnp.asarray(tb[k]).astype(BF16) for k in ("t1", "t2"))
    f2f, f2i = (jnp.asarray(tb[k]).astype(BF16) for k in ("f2f", "f2i"))
    const3 = lambda a: pl.BlockSpec(a.shape, lambda b, c: (0, 0, 0))
    const2 = lambda a: pl.BlockSpec(a.shape, lambda b, c: (0, 0))
    return pl.pallas_call(
        functools.partial(_hy_conv_kernel, N1=N1, N2=N2),
        grid=(B // 2, nt),
        in_specs=[pl.BlockSpec((2, L, LANES), lambda b, c: (b, 0, c)),
                  pl.BlockSpec((2 * n, LANES), lambda b, c: (0, c)),
                  pl.BlockSpec((1, LANES), lambda b, c: (0, c)),
                  pl.BlockSpec((SUBLANES, LANES), lambda b, c: (0, c)),
                  const3(t1), const2(f2f), const2(f2i), const3(t2)],
        out_specs=pl.BlockSpec((2, L, LANES), lambda b, c: (b, 0, c)),
        out_shape=jax.ShapeDtypeStruct((B, L, d_hy), F32),
        scratch_shapes=[pltpu.VMEM((2, (N1 // 2) * (N2 + PITCH_PAD), LANES), F32),
                        pltpu.VMEM((N2 * (2 * N1 + PITCH_PAD), LANES), F32),
                        pltpu.VMEM((N1 * (2 * N2 + PITCH_PAD), LANES), F32)],
        compiler_params=_cparams(("parallel", "parallel"), 58),
        name="hy_conv",
    )(vg, hspec, skip.reshape(1, d_hy).astype(F32), hb0, t1, f2f, f2i, t2)


def _hyena_branch(p3, conv_w, conv_b, fw1, fb1, fw2, fb2, fw3, fb3, fw4, freq, skip, d_hy):
    B, L, _ = p3.shape
    x0c, vg = _hy_pre(p3, conv_w, conv_b, d_hy)
    zt = jnp.asarray(_filter_positions(L, fw1.shape[0]).T)
    h3 = _filt_mlp(zt, fw1, fb1, fw2, fb2, fw3, fb3, freq).T
    hspec, hb0 = _filt_fft(h3, fw4, L, d_hy)
    return x0c, _hy_conv(vg, hspec, skip, hb0)


ROUTE_GATE, ROUTE_EXPERT, ROUTE_RANK = 0, 2, 4
MIX_ROWS = 256


def _mix_route_kernel(yc_ref, x0_ref, of_ref, ob_ref, z_ref, hnw_ref, gnw_ref, x_ref, wo_ref, n2w_ref, wr_ref, br_ref,
                      x2_ref, u_ref, route_ref, route_t_ref, cnt_ref, run_scr, *, n_heads, head_dim, n_groups,
                      per_group):
    tm = x_ref.shape[0]
    G, P = n_groups, per_group

    @pl.when(pl.program_id(0) == 0)
    def _():
        run_scr[...] = jnp.zeros_like(run_scr)

    run = run_scr[0:1, :]
    blocks = [pl.ds(s * MIX_ROWS, MIX_ROWS) for s in range(tm // MIX_ROWS)]
    us = []
    for rs in blocks:
        yh = yc_ref[rs, :] * x0_ref[rs, :].astype(F32)
        yh = yh * lax.rsqrt(jnp.mean(yh * yh, axis=-1, keepdims=True) + EPS) * hnw_ref[...]
        parts = [yh.astype(BF16)]
        for h in range(n_heads):
            sl = slice(h * head_dim, (h + 1) * head_dim)
            o = of_ref[rs, sl] + ob_ref[rs, sl]
            z = z_ref[rs, sl].astype(F32)
            o = o * lax.rsqrt(jnp.mean(o * o, axis=-1, keepdims=True) + EPS) * gnw_ref[...] * _silu(z)
            parts.append(o.astype(BF16))
        ymix = jnp.concatenate(parts, axis=-1)
        x2 = x_ref[rs, :] + jnp.dot(ymix, wo_ref[...], preferred_element_type=F32)
        x2_ref[rs, :] = x2
        u = x2 * lax.rsqrt(jnp.mean(x2 * x2, axis=-1, keepdims=True) + EPS) * n2w_ref[...]
        u_ref[rs, :] = _pack_bf16_pairs(u)
        us.append(u)

    all_logits = [_dot(u, wr_ref[...]) + br_ref[...] for u in us]
    for s, (rs, logits) in enumerate(zip(blocks, all_logits)):
        lane = lax.broadcasted_iota(jnp.int32, logits.shape, 1)
        neg = jnp.float32(-jnp.inf)
        big = jnp.int32(4 * LANES)
        first = lambda hit: jnp.min(jnp.where(hit, lane, big), axis=-1, keepdims=True)
        gl = jnp.where(lane < G, logits, neg)
        gmax = jnp.max(gl, axis=-1, keepdims=True)
        gidx = first(gl == gmax)
        grp_gate = 1.0 / jnp.sum(jnp.exp(gl - gmax), axis=-1, keepdims=True)
        in_grp = (lane >= G) & (lane < G + G * P) & (((lane - G) // P) == gidx)
        ll = jnp.where(in_grp, logits, neg)
        m1 = jnp.max(ll, axis=-1, keepdims=True)
        i1 = first(ll == m1)
        denom = jnp.sum(jnp.exp(ll - m1), axis=-1, keepdims=True)
        ll2 = jnp.where(lane == i1, neg, ll)
        m2 = jnp.max(ll2, axis=-1, keepdims=True)
        i2 = first(ll2 == m2)
        p1 = 1.0 / denom
        p2 = jnp.exp(m2 - m1) / denom
        gate1 = grp_gate * (p1 / (p1 + p2))
        gate2 = grp_gate * (p2 / (p1 + p2))
        e1 = i1 - G
        e2 = i2 - G

        oh1 = jnp.where(lane == e1, 1.0, 0.0)
        oh2 = jnp.where(lane == e2, 1.0, 0.0)
        oh = oh1 + oh2
        ii = lax.broadcasted_iota(jnp.int32, (MIX_ROWS, MIX_ROWS), 0)
        jj = lax.broadcasted_iota(jnp.int32, (MIX_ROWS, MIX_ROWS), 1)
        before = _dot(jnp.where(ii > jj, 1.0, 0.0), oh) + run
        r1 = jnp.sum(oh1 * before, axis=-1, keepdims=True)
        r2 = jnp.sum(oh2 * before, axis=-1, keepdims=True)
        run = run + jnp.sum(oh, axis=0, keepdims=True)

        rec = jnp.where(lane == ROUTE_GATE, gate1, 0.0)
        rec = jnp.where(lane == ROUTE_GATE + 1, gate2, rec)
        rec = jnp.where(lane == ROUTE_EXPERT, e1.astype(F32), rec)
        rec = jnp.where(lane == ROUTE_EXPERT + 1, e2.astype(F32), rec)
        rec = jnp.where(lane == ROUTE_RANK, r1, rec)
        rec = jnp.where(lane == ROUTE_RANK + 1, r2, rec)
        route_ref[rs, :] = rec
        route_t_ref[:, s * MIX_ROWS:(s + 1) * MIX_ROWS] = jnp.transpose(rec)[:SUBLANES, :]

    run_scr[...] = jnp.broadcast_to(run, run_scr.shape)
    cnt_ref[...] = run_scr[...]


def _mix_route(yconv, x0c, o_f, o_b, p, z_col, hy_norm_w, gdn_norm_w, xf, w_out_bf16, norm2_w, wr, br,
               n_heads, head_dim, n_groups, per_group, tm=512):
    M, D = xf.shape
    d_hy = yconv.shape[1]
    d_gdn = o_f.shape[1]
    assert z_col % d_gdn == 0 and n_groups * (per_group + 1) <= LANES
    zb = z_col // d_gdn
    row = lambda i: (i, 0)
    const = lambda i: (0, 0)
    kern = functools.partial(_mix_route_kernel, n_heads=n_heads, head_dim=head_dim,
                             n_groups=n_groups, per_group=per_group)
    return pl.pallas_call(
        kern,
        grid=(M // tm,),
        in_specs=[pl.BlockSpec((tm, d_hy), row), pl.BlockSpec((tm, d_hy), row),
                  pl.BlockSpec((tm, d_gdn), row), pl.BlockSpec((tm, d_gdn), row),
                  pl.BlockSpec((tm, d_gdn), lambda i: (i, zb)),
                  pl.BlockSpec((1, d_hy), const), pl.BlockSpec((1, head_dim), const),
                  pl.BlockSpec((tm, D), row), pl.BlockSpec(w_out_bf16.shape, const, pipeline_mode=pl.Buffered(1)),
                  pl.BlockSpec((1, D), const), pl.BlockSpec((D, LANES), const), pl.BlockSpec((1, LANES), const)],
        out_specs=[pl.BlockSpec((tm, D), row), pl.BlockSpec((tm, D // 2), row),
                   pl.BlockSpec((tm, LANES), row), pl.BlockSpec((SUBLANES, tm), lambda i: (0, i)),
                   pl.BlockSpec((SUBLANES, LANES), const)],
        out_shape=[jax.ShapeDtypeStruct((M, D), F32), jax.ShapeDtypeStruct((M, D // 2), jnp.uint32),
                   jax.ShapeDtypeStruct((M, LANES), F32), jax.ShapeDtypeStruct((SUBLANES, M), F32),
                   jax.ShapeDtypeStruct((SUBLANES, LANES), F32)],
        scratch_shapes=[pltpu.VMEM((SUBLANES, LANES), F32)],
        compiler_params=_cparams(("arbitrary",), 58),
        name="mix_route",
    )(yconv, x0c, o_f, o_b, p, hy_norm_w.reshape(1, d_hy), gdn_norm_w.reshape(1, head_dim), xf, w_out_bf16,
      norm2_w.reshape(1, D), wr, br)


def _experts_kernel(ts_ref, se_ref, nn_ref, tgt_nxt_ref, tgt_ref, tgt_prv_ref, u_hbm, w1_hbm, w3_hbm, w2_hbm, out_hbm,
                    x0, x1, y0, y1, wf1, wf3, wf2, w1b, w3b, w2b, gsem, ssem, wsem, *, n_tokens):
    i = pl.program_id(0)
    na = nn_ref[0]
    ns = nn_ref[1]
    TB = x0.shape[0]
    spare0 = 2 * n_tokens
    W_SLOTS = wf1.shape[0]

    def weight_copies(s):
        slot = lax.rem(s, W_SLOTS)
        e = se_ref[s]
        return [pltpu.make_async_copy(w_hbm.at[e], wf.at[slot], wsem.at[slot])
                for w_hbm, wf in ((w1_hbm, wf1), (w3_hbm, wf3), (w2_hbm, wf2))]

    def token_of(v):
        if n_tokens & (n_tokens - 1) == 0:
            return v & (n_tokens - 1)
        return lax.rem(v, n_tokens)

    def gather(tgt, xbuf, s, r):
        return pltpu.make_async_copy(u_hbm.at[pl.ds(token_of(tgt[r]), 1)], xbuf.at[pl.ds(r, 1)], gsem.at[s])

    def scatter(ybuf, s, r, dst):
        return pltpu.make_async_copy(ybuf.at[pl.ds(r, 1)], out_hbm.at[pl.ds(dst, 1)], ssem.at[s])

    def gather_wait(xbuf, s):
        pltpu.make_async_copy(u_hbm.at[pl.ds(0, TB)], xbuf, gsem.at[s]).wait()

    def scatter_wait(ybuf, s):
        pltpu.make_async_copy(ybuf, out_hbm.at[pl.ds(0, TB)], ssem.at[s]).wait()

    def each_row(fn):
        def body(r, carry):
            fn(r)
            return carry
        lax.fori_loop(0, TB, body, 0, unroll=8)

    def step(p):
        q = 1 - p
        x_cur, x_nxt = (x0, x1) if p == 0 else (x1, x0)
        y_cur, y_prv = (y0, y1) if p == 0 else (y1, y0)
        gather_wait(x_cur, p)

        @pl.when(i > 0)
        def _():
            scatter_wait(y_cur, p)

        x = _unpack_bf16_pairs(x_cur[...]).astype(BF16)
        for r in range(TB):
            gather(tgt_nxt_ref, x_nxt, q, r).start()
        first = i == 0
        for r in range(TB):
            scatter(y_prv, q, r, jnp.where(first, spare0 + r, tgt_prv_ref[r])).start()
        h = _silu(jnp.dot(x, w1b[...], preferred_element_type=F32)) * jnp.dot(x, w3b[...], preferred_element_type=F32)
        y_cur[...] = _pack_bf16_pairs(jnp.dot(h.astype(BF16), w2b[...], preferred_element_type=F32))

        @pl.when(i == na - 1)
        def _():
            gather_wait(x_nxt, q)
            scatter_wait(y_prv, q)
            each_row(lambda r: scatter(y_cur, p, r, tgt_ref[r]).start())
            scatter_wait(y_cur, p)

    @pl.when(i < na)
    def _():
        @pl.when(i == 0)
        def _():
            y0[...] = jnp.zeros_like(y0)
            y1[...] = jnp.zeros_like(y1)
            fill = pltpu.make_async_copy(y1, out_hbm.at[pl.ds(spare0, TB)], ssem.at[1])
            fill.start()
            fill.wait()
            each_row(lambda r: gather(tgt_ref, x0, 0, r).start())
            for s0 in range(W_SLOTS - 1):
                @pl.when(s0 < ns)
                def _():
                    for c in weight_copies(s0):
                        c.start(priority=WEIGHT_DMA_PRIORITY)

        s = ts_ref[i]

        @pl.when((i == 0) | (s != ts_ref[jnp.maximum(i - 1, 0)]))
        def _():
            @pl.when(s + W_SLOTS - 1 < ns)
            def _():
                for c in weight_copies(s + W_SLOTS - 1):
                    c.start(priority=WEIGHT_DMA_PRIORITY)

            for c in weight_copies(s):
                c.wait()
            slot = lax.rem(s, W_SLOTS)
            w1b[...] = wf1[slot].astype(BF16)
            w3b[...] = wf3[slot].astype(BF16)
            w2b[...] = wf2[slot].astype(BF16)

        parity = lax.rem(i, 2)

        @pl.when(parity == 0)
        def _():
            step(0)

        @pl.when(parity == 1)
        def _():
            step(1)


WEIGHT_SLOTS = 3
WEIGHT_DMA_PRIORITY = 1


def _experts(u, slot_tgt, tile_seq, seq_expert, n_active_seq, w1, w3, w2):
    T = u.shape[0]
    E, D, de = w1.shape
    TB = EXPERT_ROWS
    n_tiles = slot_tgt.shape[0] // TB
    table = lambda f: pl.BlockSpec((TB,), lambda i, ts, se, nn: (f(i),), memory_space=pltpu.SMEM)
    hbm = pl.BlockSpec(memory_space=pl.ANY)
    grid_spec = pltpu.PrefetchScalarGridSpec(
        num_scalar_prefetch=3,
        grid=(n_tiles,),
        in_specs=[table(lambda i: jnp.minimum(i + 1, n_tiles - 1)), table(lambda i: i),
                  table(lambda i: jnp.maximum(i - 1, 0)), hbm, hbm, hbm, hbm],
        out_specs=hbm,
        scratch_shapes=[pltpu.VMEM((TB, D // 2), jnp.uint32)] * 4
                       + [pltpu.VMEM((WEIGHT_SLOTS, D, de), F32), pltpu.VMEM((WEIGHT_SLOTS, D, de), F32),
                          pltpu.VMEM((WEIGHT_SLOTS, de, D), F32),
                          pltpu.VMEM((D, de), BF16), pltpu.VMEM((D, de), BF16), pltpu.VMEM((de, D), BF16),
                          pltpu.SemaphoreType.DMA((2,)), pltpu.SemaphoreType.DMA((2,)),
                          pltpu.SemaphoreType.DMA((WEIGHT_SLOTS,))],
    )
    return pl.pallas_call(
        functools.partial(_experts_kernel, n_tokens=T),
        grid_spec=grid_spec,
        out_shape=jax.ShapeDtypeStruct((2 * T + TB, D // 2), jnp.uint32),
        compiler_params=_cparams(("arbitrary",), 58),
        name="experts",
    )(tile_seq, seq_expert, n_active_seq, slot_tgt, slot_tgt, slot_tgt, u, w1, w3, w2)


def _slot_table_kernel(dest_ref, init_hbm, o_ref, sem):
    fill = pltpu.make_async_copy(init_hbm, o_ref, sem)
    fill.start()
    fill.wait()

    def put(a, carry):
        o_ref[dest_ref[a]] = a
        return carry

    lax.fori_loop(0, dest_ref.shape[0], put, 0, unroll=8)


def _slot_table(dest, init):
    smem = pl.BlockSpec(memory_space=pltpu.SMEM)
    return pl.pallas_call(
        _slot_table_kernel,
        in_specs=[smem, pl.BlockSpec(memory_space=pl.ANY)],
        out_specs=smem,
        out_shape=jax.ShapeDtypeStruct(init.shape, jnp.int32),
        scratch_shapes=[pltpu.SemaphoreType.DMA],
        name="slot_table",
    )(dest, init)


def _dispatch_tables(route_t, counts, n_experts):
    T = route_t.shape[1]
    TB = EXPERT_ROWS
    e = route_t[ROUTE_EXPERT:ROUTE_EXPERT + 2].astype(jnp.int32)
    rank = route_t[ROUTE_RANK:ROUTE_RANK + 2].astype(jnp.int32)
    cnt = counts[0, :n_experts].astype(jnp.int32)
    padded = (cnt + TB - 1) // TB * TB
    pad_end = jnp.cumsum(padded)
    pad_start = pad_end - padded
    ids = jnp.arange(n_experts, dtype=jnp.int32)[:, None, None]
    start_of = jnp.sum(jnp.where(e[None] == ids, pad_start[:, None, None], 0), axis=0)
    dest = start_of + rank
    n_tiles = -(-(2 * T + n_experts * (TB - 1)) // TB)
    padded_slot = 2 * T + jnp.arange(n_tiles * TB, dtype=jnp.int32) % TB
    slot_tgt = _slot_table(dest.reshape(-1), padded_slot)
    start = jnp.arange(n_tiles, dtype=jnp.int32) * TB
    tile_expert = jnp.sum(start[:, None] >= pad_end[None, :], axis=1)
    tile_expert = jnp.minimum(tile_expert, n_experts - 1).astype(jnp.int32)
    owns = cnt > 0
    pos = jnp.cumsum(owns.astype(jnp.int32)) - 1
    experts = jnp.arange(n_experts, dtype=jnp.int32)
    seq_expert = jnp.sum(jnp.where(owns[None, :] & (pos[None, :] == experts[:, None]), experts[None, :], 0), axis=1)
    tile_seq = jnp.sum(jnp.where(tile_expert[:, None] == experts[None, :], pos[None, :], 0), axis=1)
    n_active_seq = jnp.stack([pad_end[-1] // TB, jnp.sum(owns.astype(jnp.int32))]).astype(jnp.int32)
    return slot_tgt, tile_seq.astype(jnp.int32), seq_expert.astype(jnp.int32), n_active_seq


def _combine_kernel(x2_ref, e0_ref, e1_ref, route_ref, w_ref, o_ref, *, final_norm):
    r = route_ref[...]
    e0 = _unpack_bf16_pairs(e0_ref[...])
    e1 = _unpack_bf16_pairs(e1_ref[...])
    y = x2_ref[...] + r[:, ROUTE_GATE:ROUTE_GATE + 1] * e0 + r[:, ROUTE_GATE + 1:ROUTE_GATE + 2] * e1
    if final_norm:
        y = y * lax.rsqrt(jnp.mean(y * y, axis=-1, keepdims=True) + EPS) * w_ref[...]
    o_ref[...] = y


def _combine(x2, planes, route, norm_w, final_norm, tm=512):
    M, D = x2.shape
    return pl.pallas_call(
        functools.partial(_combine_kernel, final_norm=final_norm),
        grid=(M // tm,),
        in_specs=[pl.BlockSpec((tm, D), lambda i: (i, 0)),
                  pl.BlockSpec((tm, D // 2), lambda i: (i, 0)),
                  pl.BlockSpec((tm, D // 2), lambda i: (M // tm + i, 0)),
                  pl.BlockSpec((tm, LANES), lambda i: (i, 0)),
                  pl.BlockSpec((1, D), lambda i: (0, 0))],
        out_specs=pl.BlockSpec((tm, D), lambda i: (i, 0)),
        out_shape=jax.ShapeDtypeStruct((M, D), F32),
        compiler_params=_cparams(("parallel",), 48),
        name="combine",
    )(x2, planes, planes, route, norm_w.reshape(1, D))


def kernel(x, norm1_w, w_in, hy_conv_w, hy_conv_b, hy_filt_w1, hy_filt_b1, hy_filt_w2, hy_filt_b2, hy_filt_w3, hy_filt_b3, hy_filt_w4, hy_sin_freq, hy_skip, hy_norm_w, gdn_conv_w, gdn_a_log_f, gdn_a_log_b, gdn_dt_bias_f, gdn_dt_bias_b, gdn_norm_w, w_out, norm2_w, router_group_w, router_group_b, router_expert_w, router_expert_b, exp_w1, exp_w3, exp_w2, final_norm_w):
    B, L, D = x.shape
    M = B * L
    depth = w_in.shape[0]
    d_hy = hy_skip.shape[-1]
    H = gdn_a_log_f.shape[-1]
    Dh = gdn_norm_w.shape[-1]
    d_gdn = H * Dh
    n_main = 3 * d_hy + 4 * d_gdn
    G = router_group_w.shape[-1]
    E = router_expert_w.shape[-1]
    xf = x.reshape(M, D)
    for l in range(depth):
        p, pg = _inproj(xf, norm1_w[l], jnp.swapaxes(w_in[l], 0, 1), n_main, 4 * H)
        p3 = p.reshape(B, L, n_main)
        x0c, yconv = _hyena_branch(p3, hy_conv_w[l], hy_conv_b[l], hy_filt_w1[l], hy_filt_b1[l], hy_filt_w2[l],
                                   hy_filt_b2[l], hy_filt_w3[l], hy_filt_b3[l], hy_filt_w4[l], hy_sin_freq[l],
                                   hy_skip[l], d_hy)
        o_f, o_b = _gdn_branch(p3, pg, gdn_conv_w[l], gdn_a_log_f[l], gdn_a_log_b[l], gdn_dt_bias_f[l],
                               gdn_dt_bias_b[l], 3 * d_hy, H, Dh)
        wr = jnp.pad(jnp.concatenate([router_group_w[l], router_expert_w[l]], axis=1), ((0, 0), (0, LANES - G - E)))
        br = jnp.pad(jnp.concatenate([router_group_b[l], router_expert_b[l]]), (0, LANES - G - E)).reshape(1, LANES)
        x2, u, route, route_t, counts = _mix_route(
            yconv.reshape(M, d_hy), x0c.reshape(M, d_hy), o_f.reshape(M, d_gdn), o_b.reshape(M, d_gdn), p,
            3 * d_hy + 3 * d_gdn, hy_norm_w[l], gdn_norm_w[l], xf, w_out[l].astype(BF16), norm2_w[l], wr, br,
            H, Dh, G, E // G)
        slot_tgt, tile_seq, seq_expert, n_active_seq = _dispatch_tables(route_t, counts, E)
        planes = _experts(u, slot_tgt, tile_seq, seq_expert, n_active_seq, exp_w1[l], exp_w3[l], exp_w2[l])
        xf = _combine(x2, planes, route, final_norm_w, final_norm=(l == depth - 1))
    return xf.reshape(B, L, D)
```

```python
import functools
import math

import jax
import jax.numpy as jnp
import numpy as np
from jax import lax
from jax.experimental import pallas as pl
from jax.experimental.pallas import tpu as pltpu

F32 = jnp.float32
BF16 = jnp.bfloat16
EPS = 1e-6
LANES = 128
SUBLANES = 8
VMEM_BYTES_V7X = 64 * 1024 * 1024
GDN_CHUNK = 64
FFT_N2 = 128
EXPERT_ROWS = 256
DECAY_TARGET = 1e-2
FAST_DECAY_PCT = 0.3
SLOW_DECAY_PCT = 1.5


def _cparams(sem, vmem_mb):
    return pltpu.CompilerParams(dimension_semantics=sem, vmem_limit_bytes=int(vmem_mb * 1024 * 1024))


def _dot(a, b):
    return jnp.dot(a.astype(BF16), b.astype(BF16), preferred_element_type=F32)


def _dot_nt(a, b):
    return lax.dot_general(a.astype(BF16), b.astype(BF16), (((1,), (1,)), ((), ())), preferred_element_type=F32)


def _dot_tn(a, b):
    return lax.dot_general(a.astype(BF16), b.astype(BF16), (((0,), (0,)), ((), ())), preferred_element_type=F32)


def _dot_hi(a, b):
    return jnp.dot(a, b, preferred_element_type=F32, precision=lax.Precision.HIGHEST)


def _silu(x):
    return x * jax.nn.sigmoid(x)


def _pack_bf16_pairs(x):
    c = x.shape[1] // 2
    lo = lax.bitcast_convert_type(x[:, :c].astype(BF16).astype(F32), jnp.uint32) >> 16
    hi = lax.bitcast_convert_type(x[:, c:].astype(BF16).astype(F32), jnp.uint32) & jnp.uint32(0xFFFF0000)
    return hi | lo


def _unpack_bf16_pairs(w):
    lo = lax.bitcast_convert_type(w << 16, F32)
    hi = lax.bitcast_convert_type(w & jnp.uint32(0xFFFF0000), F32)
    return jnp.concatenate([lo, hi], axis=1)


def _store_token_tiles(ref, row0, words):
    n, width = words.shape
    ch = width // LANES
    for c in range(ch):
        ref[pl.ds(row0 * ch + c, n, stride=ch), :] = words[:, c * LANES:(c + 1) * LANES]


def _load_token_tiles(ref, row0, n, ch):
    return jnp.concatenate([ref[pl.ds(row0 * ch + c, n, stride=ch), :] for c in range(ch)], axis=1)


def _inproj_kernel(x_ref, nw_ref, wt_ref, wgt_ref, p_ref, g_ref, h_scr):
    @pl.when(pl.program_id(1) == 0)
    def _():
        x = x_ref[...]
        h = x * lax.rsqrt(jnp.mean(x * x, axis=-1, keepdims=True) + EPS) * nw_ref[...]
        h_scr[...] = h.astype(BF16)
        g = _dot_nt(h, wgt_ref[...])
        g_ref[...] = jnp.concatenate([g, jnp.zeros((g.shape[0], LANES - g.shape[1]), F32)], axis=1)

    p_ref[...] = _dot_nt(h_scr[...], wt_ref[...]).astype(p_ref.dtype)


def _inproj(xf, norm_w, wt, n_main, n_gate, tm=2048, tn=512):
    M, D = xf.shape
    assert n_main % tn == 0 and M % tm == 0 and n_main % n_gate == 0 and n_gate % SUBLANES == 0
    return pl.pallas_call(
        _inproj_kernel,
        grid=(M // tm, n_main // tn),
        in_specs=[
            pl.BlockSpec((tm, D), lambda i, j: (i, 0), pipeline_mode=pl.Buffered(1)),
            pl.BlockSpec((1, D), lambda i, j: (0, 0)),
            pl.BlockSpec((tn, D), lambda i, j: (j, 0)),
            pl.BlockSpec((n_gate, D), lambda i, j: (n_main // n_gate, 0)),
        ],
        out_specs=[
            pl.BlockSpec((tm, tn), lambda i, j: (i, j)),
            pl.BlockSpec((tm, LANES), lambda i, j: (i, 0)),
        ],
        out_shape=[jax.ShapeDtypeStruct((M, n_main), BF16), jax.ShapeDtypeStruct((M, LANES), F32)],
        scratch_shapes=[pltpu.VMEM((tm, D), BF16)],
        compiler_params=_cparams(("parallel", "arbitrary"), 57),
        name="inproj",
    )(xf, norm_w.reshape(1, D), wt, wt)


def _conv3_rows(ref, r0, rows, w, n_rows):
    cur = ref[pl.ds(r0, rows), :].astype(F32)
    lo = jnp.maximum(r0 - HALO_ROWS, 0)
    hi = jnp.minimum(r0 + rows, n_rows - HALO_ROWS)
    prev_grp = ref[pl.ds(pl.multiple_of(lo, HALO_ROWS), HALO_ROWS), :].astype(F32)
    next_grp = ref[pl.ds(pl.multiple_of(hi, HALO_ROWS), HALO_ROWS), :].astype(F32)
    prev_row = jnp.where(r0 > 0, prev_grp[HALO_ROWS - 1:HALO_ROWS, :], 0.0)
    next_row = jnp.where(r0 + rows < n_rows, next_grp[0:1, :], 0.0)
    row = lax.broadcasted_iota(jnp.int32, cur.shape, 0)
    xm = jnp.where(row == 0, prev_row, pltpu.roll(cur, 1, 0))
    xp = jnp.where(row == rows - 1, next_row, pltpu.roll(cur, rows - 1, 0))
    return xm * w[0:1, :] + cur * w[1:2, :] + xp * w[2:3, :]


CONV_ROWS = 256
HALO_ROWS = 16


def _hy_pre_kernel(x0_ref, x1_ref, v_ref, w0_ref, w1_ref, w2_ref, b0_ref, b1_ref, b2_ref, x0c_ref, vg_ref):
    L = x0_ref.shape[0]
    w0, w1, w2 = w0_ref[...], w1_ref[...], w2_ref[...]
    b0, b1, b2 = b0_ref[...], b1_ref[...], b2_ref[...]

    def body(c, carry):
        r0 = pl.multiple_of(c * CONV_ROWS, CONV_ROWS)
        x0c_ref[pl.ds(r0, CONV_ROWS), :] = (_conv3_rows(x0_ref, r0, CONV_ROWS, w0, L) + b0).astype(x0c_ref.dtype)
        x1c = _conv3_rows(x1_ref, r0, CONV_ROWS, w1, L) + b1
        vc = _conv3_rows(v_ref, r0, CONV_ROWS, w2, L) + b2
        vg_ref[pl.ds(r0, CONV_ROWS), :] = (vc * x1c).astype(vg_ref.dtype)
        return carry

    lax.fori_loop(0, L // CONV_ROWS, body, 0, unroll=2)


def _hy_pre(p3, conv_w, conv_b, d_hy):
    B, L, _ = p3.shape
    nt = d_hy // LANES
    bias = conv_b.reshape(1, -1)
    pspec = lambda off: pl.BlockSpec((None, L, LANES), lambda b, c: (b, 0, c + off))
    wspec = lambda off: pl.BlockSpec((3, LANES), lambda b, c: (0, c + off))
    bspec = lambda off: pl.BlockSpec((1, LANES), lambda b, c: (0, c + off))
    ospec = pl.BlockSpec((None, L, LANES), lambda b, c: (b, 0, c))
    return pl.pallas_call(
        _hy_pre_kernel,
        grid=(B, nt),
        in_specs=[pspec(0), pspec(nt), pspec(2 * nt), wspec(0), wspec(nt), wspec(2 * nt),
                  bspec(0), bspec(nt), bspec(2 * nt)],
        out_specs=[ospec, ospec],
        out_shape=[jax.ShapeDtypeStruct((B, L, d_hy), BF16)] * 2,
        compiler_params=_cparams(("parallel", "parallel"), 40),
        name="hy_pre",
    )(p3, p3, p3, conv_w, conv_w, conv_w, bias, bias, bias)


def _gdn_pre_kernel(x_ref, w_ref, o_ref, *, n_heads, head_dim):
    L = x_ref.shape[0]
    w = w_ref[...]
    c = pl.program_id(1)
    q_scale = jnp.where(c < n_heads, head_dim ** -0.5, 1.0)
    is_qk = c < 2 * n_heads

    def body(i, carry):
        r0 = pl.multiple_of(i * CONV_ROWS, CONV_ROWS)
        y = _silu(_conv3_rows(x_ref, r0, CONV_ROWS, w, L))
        inv = lax.rsqrt(jnp.sum(y * y, axis=-1, keepdims=True) + EPS) * q_scale
        o_ref[pl.ds(r0, CONV_ROWS), :] = y * jnp.where(is_qk, inv, 1.0)
        return carry

    lax.fori_loop(0, L // CONV_ROWS, body, 0, unroll=2)


def _gdn_pre(p3, conv_w, col0, n_heads, head_dim):
    B, L, _ = p3.shape
    assert head_dim == LANES
    nt = 3 * n_heads
    off = col0 // LANES
    return pl.pallas_call(
        functools.partial(_gdn_pre_kernel, n_heads=n_heads, head_dim=head_dim),
        grid=(B, nt),
        in_specs=[pl.BlockSpec((None, L, LANES), lambda b, c: (b, 0, c + off)),
                  pl.BlockSpec((3, LANES), lambda b, c: (0, c))],
        out_specs=pl.BlockSpec((None, L, LANES), lambda b, c: (b, 0, c)),
        out_shape=jax.ShapeDtypeStruct((B, L, nt * LANES), F32),
        compiler_params=_cparams(("parallel", "parallel"), 24),
        name="gdn_pre",
    )(p3, conv_w)


GATE_ROWS = 512


def _gdn_gates_kernel(pg_ref, alog_ref, dtb_ref, o_ref, *, n_heads):
    H = n_heads
    x = pg_ref[...]
    beta = jax.nn.sigmoid(x)
    z = x + dtb_ref[...]
    softplus = jnp.maximum(z, 0.0) + jnp.log1p(jnp.exp(-jnp.abs(z)))
    g = -jnp.exp(alog_ref[...]) * softplus
    pos = lax.broadcasted_iota(jnp.int32, x.shape, 0) & (GDN_CHUNK - 1)
    gc_f = g
    gc_b = g
    step = 1
    while step < GDN_CHUNK:
        gc_f = gc_f + jnp.where(pos >= step, pltpu.roll(gc_f, step, 0), 0.0)
        gc_b = gc_b + jnp.where(pos < GDN_CHUNK - step, pltpu.roll(gc_b, GATE_ROWS - step, 0), 0.0)
        step *= 2
    g_tot = pltpu.roll(gc_f + gc_b - g, 4 * H, 1)
    lane = lax.broadcasted_iota(jnp.int32, x.shape, 1)
    out = jnp.where(lane < 2 * H, beta,
                    jnp.where(lane < 3 * H, gc_f,
                              jnp.where(lane < 4 * H, gc_b,
                                        jnp.where((lane >= 6 * H) & (lane < 8 * H), g_tot, 0.0))))
    o_ref[...] = out


def _gdn_gates(pg, a_log_f, a_log_b, dt_bias_f, dt_bias_b, n_heads):
    M = pg.shape[0]
    H = n_heads
    assert 8 * H <= LANES
    pad = lambda a, b: jnp.concatenate([jnp.zeros((2 * H,), F32), a.astype(F32), b.astype(F32),
                                        jnp.zeros((LANES - 4 * H,), F32)]).reshape(1, LANES)
    return pl.pallas_call(
        functools.partial(_gdn_gates_kernel, n_heads=H),
        grid=(M // GATE_ROWS,),
        in_specs=[pl.BlockSpec((GATE_ROWS, LANES), lambda i: (i, 0)),
                  pl.BlockSpec((1, LANES), lambda i: (0, 0)),
                  pl.BlockSpec((1, LANES), lambda i: (0, 0))],
        out_specs=pl.BlockSpec((GATE_ROWS, LANES), lambda i: (i, 0)),
        out_shape=jax.ShapeDtypeStruct((M, LANES), F32),
        compiler_params=_cparams(("parallel",), 24),
        name="gdn_gates",
    )(pg, pad(a_log_f, a_log_b), pad(dt_bias_f, dt_bias_b))


def _delta_chunks(q, k, v, beta, gc_col, gc_row, gtot, state, lower):
    n = len(q)
    C = q[0].shape[0]
    D = k[0].shape[1]
    ii = lax.broadcasted_iota(jnp.int32, (C, C), 0)
    jj = lax.broadcasted_iota(jnp.int32, (C, C), 1)
    eye = jnp.where(ii == jj, 1.0, 0.0)
    incl = [(ii >= jj) if lo else (ii <= jj) for lo in lower]
    strict = [(ii > jj) if lo else (ii < jj) for lo in lower]
    rng = range(n)
    decay = [jnp.where(incl[i], jnp.exp(jnp.where(incl[i], gc_col[i] - gc_row[i], 0.0)), 0.0) for i in rng]
    kb = [k[i] * beta[i] for i in rng]
    kk = [_dot_nt(kb[i], k[i]) for i in rng]
    qk = [_dot_nt(q[i], k[i]) for i in rng]
    m = [jnp.where(strict[i], -(kk[i] * decay[i]), 0.0) for i in rng]
    r = [eye + m[i] for i in rng]
    m = [_dot(m[i], m[i]) for i in rng]
    for _ in range(int(math.log2(C)) - 2):
        rm = [_dot(jnp.concatenate([r[i], m[i]], axis=0), m[i]) for i in rng]
        r = [r[i] + rm[i][:C] for i in rng]
        m = [rm[i][C:] for i in rng]
    r = [r[i] + _dot(r[i], m[i]) for i in rng]
    eg = [jnp.exp(gc_col[i]) for i in rng]
    wu = [_dot(r[i], jnp.concatenate([kb[i] * eg[i], v[i] * beta[i]], axis=1)) for i in rng]
    ws = [_dot(jnp.concatenate([wu[i][:, :D], q[i] * eg[i]], axis=0), state[i]) for i in rng]
    v_new = [wu[i][:, D:] - ws[i][:C] for i in rng]
    qkm = [jnp.where(incl[i], qk[i] * decay[i], 0.0) for i in rng]
    out = [ws[i][C:] + _dot(qkm[i], v_new[i]) for i in rng]
    k_dec = [k[i] * jnp.exp(gtot[i] - gc_col[i]) for i in rng]
    new_state = [state[i] * jnp.exp(gtot[i][0:1, :]) + _dot_tn(k_dec[i], v_new[i]) for i in rng]
    return out, new_state


def _gdn_scan_kernel(qf_ref, kf_ref, vf_ref, qb_ref, kb_ref, vb_ref, gf_ref, gb_ref, rf_ref, rb_ref,
                     of_ref, ob_ref, s_scr, *, n_heads, head_dim):
    H, Dh = n_heads, head_dim
    B = qf_ref.shape[0]

    @pl.when(pl.program_id(0) == 0)
    def _():
        s_scr[...] = jnp.zeros_like(s_scr)

    col = lambda g, j: g[:, j:j + 1]
    sls = [slice(h * Dh, (h + 1) * Dh) for h in range(H)]
    q, k, v, beta, gc_col, gc_row, gtot, state = ([] for _ in range(8))
    for b in range(B):
        gf = gf_ref[b]
        gb = gb_ref[b]
        q += [qf_ref[b, :, sl] for sl in sls] + [qb_ref[b, :, sl] for sl in sls]
        k += [kf_ref[b, :, sl] for sl in sls] + [kb_ref[b, :, sl] for sl in sls]
        v += [vf_ref[b, :, sl] for sl in sls] + [vb_ref[b, :, sl] for sl in sls]
        beta += [col(gf, h) for h in range(H)] + [col(gb, H + h) for h in range(H)]
        gc_col += [col(gf, 2 * H + h) for h in range(H)] + [col(gb, 3 * H + h) for h in range(H)]
        gc_row += [rf_ref[b, h:h + 1, :] for h in range(H)] + [rb_ref[b, H + h:H + h + 1, :] for h in range(H)]
        gtot += [col(gf, 6 * H + h) for h in range(H)] + [col(gb, 7 * H + h) for h in range(H)]
        state += [s_scr[b, 0, h] for h in range(H)] + [s_scr[b, 1, h] for h in range(H)]
    out, new_state = _delta_chunks(q, k, v, beta, gc_col, gc_row, gtot, state, ([True] * H + [False] * H) * B)
    for b in range(B):
        for h in range(H):
            of_ref[b, :, sls[h]] = out[2 * H * b + h]
            ob_ref[b, :, sls[h]] = out[2 * H * b + H + h]
            s_scr[b, 0, h] = new_state[2 * H * b + h]
            s_scr[b, 1, h] = new_state[2 * H * b + H + h]


def _gdn_scan(qkv, gates, gates_row, n_heads, head_dim):
    B, L, _ = qkv.shape
    H, Dh = n_heads, head_dim
    d = H * Dh
    C = GDN_CHUNK
    N = L // C
    fwd = lambda col: pl.BlockSpec((B, C, d), lambda n: (0, n, col))
    bwd = lambda col: pl.BlockSpec((B, C, d), lambda n: (0, N - 1 - n, col))
    return pl.pallas_call(
        functools.partial(_gdn_scan_kernel, n_heads=H, head_dim=Dh),
        grid=(N,),
        in_specs=[fwd(0), fwd(1), fwd(2), bwd(0), bwd(1), bwd(2),
                  pl.BlockSpec((B, C, LANES), lambda n: (0, n, 0)),
                  pl.BlockSpec((B, C, LANES), lambda n: (0, N - 1 - n, 0)),
                  pl.BlockSpec((B, None, 2 * H, C), lambda n: (0, n, 0, 0)),
                  pl.BlockSpec((B, None, 2 * H, C), lambda n: (0, N - 1 - n, 0, 0))],
        out_specs=[pl.BlockSpec((B, C, d), lambda n: (0, n, 0)),
                   pl.BlockSpec((B, C, d), lambda n: (0, N - 1 - n, 0))],
        out_shape=[jax.ShapeDtypeStruct((B, L, d), F32)] * 2,
        scratch_shapes=[pltpu.VMEM((B, 2, H, Dh, Dh), F32)],
        compiler_params=_cparams(("arbitrary",), 32),
        name="gdn_scan",
    )(qkv, qkv, qkv, qkv, qkv, qkv, gates, gates, gates_row, gates_row)


def _gdn_branch(p3, pg, conv_w, a_log_f, a_log_b, dt_bias_f, dt_bias_b, col0, n_heads, head_dim):
    B, L, _ = p3.shape
    H = n_heads
    qkv = _gdn_pre(p3, conv_w, col0, H, head_dim)
    gates = _gdn_gates(pg, a_log_f, a_log_b, dt_bias_f, dt_bias_b, H).reshape(B, L, LANES)
    N = L // GDN_CHUNK
    gates_row = gates[..., 2 * H:4 * H].reshape(B, N, GDN_CHUNK, 2 * H).transpose(0, 1, 3, 2)
    return _gdn_scan(qkv, gates, gates_row, H, head_dim)


@functools.lru_cache(maxsize=None)
def _filter_positions(L, pos_emb_dim):
    n = 2 * L
    r = np.arange(n)
    k = np.where(r < L, r, np.where(r == L, 0, n - r)).astype(np.float64)
    t = k / (L - 1)
    bands = (pos_emb_dim - 1) // 2
    fb = np.linspace(1e-4, bands - 1, bands)
    ang = (2.0 * math.pi / L) * k[:, None] * fb[None, :]
    z = np.concatenate([t[:, None], np.cos(ang), -np.sin(ang)], axis=-1)
    return z.astype(np.float32)


@functools.lru_cache(maxsize=None)
def _decay_rates(d_hy):
    max_decay = math.log(DECAY_TARGET) / FAST_DECAY_PCT
    min_decay = math.log(DECAY_TARGET) / SLOW_DECAY_PCT
    return np.abs(np.linspace(min_decay, max_decay, d_hy)).astype(np.float32).reshape(1, d_hy)


def _filt_mlp_kernel(zt_ref, w1t_ref, b1_ref, w2t_ref, b2_ref, w3t_ref, b3_ref, fr_ref, o_ref):
    fr = fr_ref[...]
    h = jnp.sin(fr * (_dot_hi(w1t_ref[...], zt_ref[...]) + b1_ref[...]))
    h = jnp.sin(fr * (_dot_hi(w2t_ref[...], h) + b2_ref[...]))
    o_ref[...] = jnp.sin(fr * (_dot_hi(w3t_ref[...], h) + b3_ref[...]))


def _filt_mlp(zt, w1, b1, w2, b2, w3, b3, freq, tc=1024):
    pe, n = zt.shape
    fw = w1.shape[1]
    col = lambda a: a.reshape(-1, 1).astype(F32)
    full = lambda a: pl.BlockSpec(a.shape, lambda i: (0, 0))
    args = (zt, w1.T, col(b1), w2.T, col(b2), w3.T, col(b3), col(freq))
    return pl.pallas_call(
        _filt_mlp_kernel,
        grid=(n // tc,),
        in_specs=[pl.BlockSpec((pe, tc), lambda i: (0, i))] + [full(a) for a in args[1:]],
        out_specs=pl.BlockSpec((fw, tc), lambda i: (0, i)),
        out_shape=jax.ShapeDtypeStruct((fw, n), F32),
        compiler_params=_cparams(("parallel",), 24),
        name="filt_mlp",
    )(*args)


@functools.lru_cache(maxsize=None)
def _dft_tables(L):
    n = 2 * L
    N2 = FFT_N2
    N1 = n // N2
    N1h = N1 // 2
    j2 = np.arange(N2)[:, None, None]
    k1 = np.arange(N1)[None, :, None]

    def stage1(n_j1):
        j1 = np.arange(n_j1)[None, None, :]
        m = (k1 * (N2 * j1 + j2)) % n
        th = 2.0 * np.pi * m / n
        return np.cos(th), np.sin(th)

    c, s = stage1(N1h)
    t1 = np.concatenate([np.concatenate([c, s], axis=2), np.concatenate([-s, c], axis=2)], axis=1)
    c, s = stage1(N1)
    t1g = np.concatenate([c, -s], axis=1)
    c, s = stage1(N1h)
    ct, st = np.swapaxes(c, 1, 2) / n, np.swapaxes(s, 1, 2) / n
    t2 = np.concatenate([np.concatenate([ct, -st], axis=2), np.concatenate([st, ct], axis=2)], axis=1)
    a = np.arange(N2)
    th = 2.0 * np.pi * ((a[:, None] * a[None, :]) % N2) / N2
    c2, s2 = np.cos(th), np.sin(th)
    f2f = np.block([[c2, s2], [-s2, c2]])
    f2i = np.block([[c2, -s2], [s2, c2]])
    as_bf16 = lambda x: jnp.asarray(x, dtype=F32).astype(BF16)
    return dict(N1=N1, N2=N2, t1=t1.astype(np.float32), t1g=t1g.astype(np.float32), t2=t2.astype(np.float32),
                f2f=f2f.astype(np.float32), f2i=f2i.astype(np.float32))


FILT_ROWS = 512
FFT_UNROLL = 8
PITCH_PAD = 8


def _filt_fft_kernel(h3_ref, w4f_ref, w4b_ref, delta_ref, t1g_ref, f2f_ref, hspec_ref, hb0_ref, g_scr, a_scr,
                     *, L, N1, N2):
    n = 2 * L
    gp = N2 + PITCH_PAD
    ap = 2 * N1 + PITCH_PAD
    delta = delta_ref[...]
    hb0_ref[...] = jnp.zeros_like(hb0_ref)

    def gen(c, carry):
        r0 = pl.multiple_of(c * FILT_ROWS, FILT_ROWS)
        row = r0 + lax.broadcasted_iota(jnp.int32, (FILT_ROWS, LANES), 0)
        lag = jnp.where(row < L, row, jnp.where(row == L, 0, n - row))
        window = jnp.exp(-(lag.astype(F32) * (1.0 / (L - 1))) * delta)
        w4 = jnp.where(r0 < L, w4f_ref[...], w4b_ref[...])
        g = _dot(h3_ref[pl.ds(r0, FILT_ROWS), :], w4) * window
        at_l = row == L
        hb0_ref[...] += jnp.sum(jnp.where(at_l, g, 0.0), axis=0, keepdims=True)
        g = jnp.where(at_l, 0.0, g)
        for q in range(FILT_ROWS // N2):
            dst = pl.multiple_of((c * (FILT_ROWS // N2) + q) * gp, SUBLANES)
            g_scr[pl.ds(dst, N2), :] = g[q * N2:(q + 1) * N2]
        return carry

    lax.fori_loop(0, n // FILT_ROWS, gen, 0)

    def stage1(j2, carry):
        x = g_scr[pl.ds(j2, N1, stride=gp), :]
        a_scr[pl.ds(pl.multiple_of(j2 * ap, SUBLANES), 2 * N1), :] = _dot(t1g_ref[j2], x)
        return carry

    lax.fori_loop(0, N2, stage1, 0, unroll=2 * FFT_UNROLL)

    def stage2(k1, carry):
        ar = a_scr[pl.ds(k1, N2, stride=ap), :]
        ai = a_scr[pl.ds(N1 + k1, N2, stride=ap), :]
        z = _dot(f2f_ref[...], jnp.concatenate([ar, ai], axis=0))
        hspec_ref[pl.ds(pl.multiple_of(k1 * 2 * N2, 2 * N2), 2 * N2), :] = z.astype(hspec_ref.dtype)
        return carry

    lax.fori_loop(0, N1, stage2, 0, unroll=FFT_UNROLL)


def _filt_fft(h3, w4, L, d_hy):
    tb = _dft_tables(L)
    N1, N2 = tb["N1"], tb["N2"]
    n = 2 * L
    fw = h3.shape[1]
    nt = d_hy // LANES
    t1g = jnp.asarray(tb["t1g"]).astype(BF16)
    f2f = jnp.asarray(tb["f2f"]).astype(BF16)
    return pl.pallas_call(
        functools.partial(_filt_fft_kernel, L=L, N1=N1, N2=N2),
        grid=(nt,),
        in_specs=[pl.BlockSpec((n, fw), lambda c: (0, 0)),
                  pl.BlockSpec((fw, LANES), lambda c: (0, c)),
                  pl.BlockSpec((fw, LANES), lambda c: (0, c + nt)),
                  pl.BlockSpec((1, LANES), lambda c: (0, c)),
                  pl.BlockSpec(t1g.shape, lambda c: (0, 0, 0)),
                  pl.BlockSpec(f2f.shape, lambda c: (0, 0))],
        out_specs=[pl.BlockSpec((2 * n, LANES), lambda c: (0, c)),
                   pl.BlockSpec((SUBLANES, LANES), lambda c: (0, c))],
        out_shape=[jax.ShapeDtypeStruct((2 * n, d_hy), BF16), jax.ShapeDtypeStruct((SUBLANES, d_hy), F32)],
        scratch_shapes=[pltpu.VMEM((N1 * (N2 + PITCH_PAD), LANES), F32),
                        pltpu.VMEM((N2 * (2 * N1 + PITCH_PAD), LANES), F32)],
        compiler_params=_cparams(("parallel",), 48),
        name="filt_fft",
    )(h3, w4, w4, jnp.asarray(_decay_rates(d_hy)), t1g, f2f)


def _hy_conv_kernel(vg_ref, hspec_ref, skip_ref, hb0_ref, t1_ref, f2f_ref, f2i_ref, t2_ref, y_ref,
                    x_scr, a_scr, b_scr, *, N1, N2):
    N1h = N1 // 2
    xp = N2 + PITCH_PAD
    ap = 2 * N1 + PITCH_PAD
    bp = 2 * N2 + PITCH_PAD

    for b in range(2):
        for j1 in range(N1h):
            x_scr[b, pl.ds(j1 * xp, N2), :] = vg_ref[b, pl.ds(j1 * N2, N2), :].astype(F32)

    def stage1(j2, carry):
        x = jnp.concatenate([x_scr[0, pl.ds(j2, N1h, stride=xp), :],
                             x_scr[1, pl.ds(j2, N1h, stride=xp), :]], axis=0)
        a_scr[pl.ds(pl.multiple_of(j2 * ap, SUBLANES), 2 * N1), :] = _dot(t1_ref[j2], x)
        return carry

    lax.fori_loop(0, N2, stage1, 0, unroll=2 * FFT_UNROLL)

    def stage2(k1, carry):
        ar = a_scr[pl.ds(k1, N2, stride=ap), :]
        ai = a_scr[pl.ds(N1 + k1, N2, stride=ap), :]
        z = _dot(f2f_ref[...], jnp.concatenate([ar, ai], axis=0))
        zr, zi = z[:N2], z[N2:]
        base = pl.multiple_of(k1 * 2 * N2, 2 * N2)
        hr = hspec_ref[pl.ds(base, N2), :].astype(F32)
        hi = hspec_ref[pl.ds(base + N2, N2), :].astype(F32)
        prod = jnp.concatenate([zr * hr - zi * hi, zr * hi + zi * hr], axis=0)
        b_scr[pl.ds(pl.multiple_of(k1 * bp, SUBLANES), 2 * N2), :] = _dot(f2i_ref[...], prod)
        return carry

    lax.fori_loop(0, N1, stage2, 0, unroll=FFT_UNROLL)

    skip = skip_ref[...] + hb0_ref[0:1, :]

    def stage3(j2, carry):
        b = jnp.concatenate([b_scr[pl.ds(j2, N1, stride=bp), :],
                             b_scr[pl.ds(N2 + j2, N1, stride=bp), :]], axis=0)
        y = _dot(t2_ref[j2], b)
        x_scr[0, pl.ds(j2, N1h, stride=xp), :] = y[:N1h] + x_scr[0, pl.ds(j2, N1h, stride=xp), :] * skip
        x_scr[1, pl.ds(j2, N1h, stride=xp), :] = y[N1h:] + x_scr[1, pl.ds(j2, N1h, stride=xp), :] * skip
        return carry

    lax.fori_loop(0, N2, stage3, 0, unroll=2 * FFT_UNROLL)

    for b in range(2):
        for j1 in range(N1h):
            y_ref[b, pl.ds(j1 * N2, N2), :] = x_scr[b, pl.ds(j1 * xp, N2), :]


def _hy_conv(vg, hspec, skip, hb0):
    B, L, d_hy = vg.shape
    assert B % 2 == 0
    tb = _dft_tables(L)
    N1, N2 = tb["N1"], tb["N2"]
    n = 2 * L
    nt = d_hy // LANES
    t1, t2 = (jnp.asarray(tb[k]).astype(BF16) for k in ("t1", "t2"))
    f2f, f2i = (jnp.asarray(tb[k]).astype(BF16) for k in ("f2f", "f2i"))
    const3 = lambda a: pl.BlockSpec(a.shape, lambda b, c: (0, 0, 0))
    const2 = lambda a: pl.BlockSpec(a.shape, lambda b, c: (0, 0))
    return pl.pallas_call(
        functools.partial(_hy_conv_kernel, N1=N1, N2=N2),
        grid=(B // 2, nt),
        in_specs=[pl.BlockSpec((2, L, LANES), lambda b, c: (b, 0, c)),
                  pl.BlockSpec((2 * n, LANES), lambda b, c: (0, c)),
                  pl.BlockSpec((1, LANES), lambda b, c: (0, c)),
                  pl.BlockSpec((SUBLANES, LANES), lambda b, c: (0, c)),
                  const3(t1), const2(f2f), const2(f2i), const3(t2)],
        out_specs=pl.BlockSpec((2, L, LANES), lambda b, c: (b, 0, c)),
        out_shape=jax.ShapeDtypeStruct((B, L, d_hy), F32),
        scratch_shapes=[pltpu.VMEM((2, (N1 // 2) * (N2 + PITCH_PAD), LANES), F32),
                        pltpu.VMEM((N2 * (2 * N1 + PITCH_PAD), LANES), F32),
                        pltpu.VMEM((N1 * (2 * N2 + PITCH_PAD), LANES), F32)],
        compiler_params=_cparams(("parallel", "parallel"), 58),
        name="hy_conv",
    )(vg, hspec, skip.reshape(1, d_hy).astype(F32), hb0, t1, f2f, f2i, t2)


def _hyena_branch(p3, conv_w, conv_b, fw1, fb1, fw2, fb2, fw3, fb3, fw4, freq, skip, d_hy):
    B, L, _ = p3.shape
    x0c, vg = _hy_pre(p3, conv_w, conv_b, d_hy)
    zt = jnp.asarray(_filter_positions(L, fw1.shape[0]).T)
    h3 = _filt_mlp(zt, fw1, fb1, fw2, fb2, fw3, fb3, freq).T
    hspec, hb0 = _filt_fft(h3, fw4, L, d_hy)
    return x0c, _hy_conv(vg, hspec, skip, hb0)


ROUTE_GATE, ROUTE_EXPERT, ROUTE_RANK = 0, 2, 4
MIX_ROWS = 256


def _mix_route_kernel(yc_ref, x0_ref, of_ref, ob_ref, z_ref, hnw_ref, gnw_ref, x_ref, wo_ref, n2w_ref, wr_ref, br_ref,
                      x2_ref, u_ref, route_ref, route_t_ref, cnt_ref, run_scr, *, n_heads, head_dim, n_groups,
                      per_group):
    tm = x_ref.shape[0]
    G, P = n_groups, per_group

    @pl.when(pl.program_id(0) == 0)
    def _():
        run_scr[...] = jnp.zeros_like(run_scr)

    run = run_scr[0:1, :]
    blocks = [pl.ds(s * MIX_ROWS, MIX_ROWS) for s in range(tm // MIX_ROWS)]
    us = []
    for blk, rs in enumerate(blocks):
        yh = yc_ref[rs, :] * x0_ref[rs, :].astype(F32)
        yh = yh * lax.rsqrt(jnp.mean(yh * yh, axis=-1, keepdims=True) + EPS) * hnw_ref[...]
        parts = [yh.astype(BF16)]
        for h in range(n_heads):
            sl = slice(h * head_dim, (h + 1) * head_dim)
            o = of_ref[rs, sl] + ob_ref[rs, sl]
            z = z_ref[rs, sl].astype(F32)
            o = o * lax.rsqrt(jnp.mean(o * o, axis=-1, keepdims=True) + EPS) * gnw_ref[...] * _silu(z)
            parts.append(o.astype(BF16))
        ymix = jnp.concatenate(parts, axis=-1)
        x2 = x_ref[rs, :] + jnp.dot(ymix, wo_ref[...], preferred_element_type=F32)
        x2_ref[rs, :] = x2
        u = x2 * lax.rsqrt(jnp.mean(x2 * x2, axis=-1, keepdims=True) + EPS) * n2w_ref[...]
        _store_token_tiles(u_ref, blk * MIX_ROWS, _pack_bf16_pairs(u))
        us.append(u)

    all_logits = [_dot(u, wr_ref[...]) + br_ref[...] for u in us]
    for s, (rs, logits) in enumerate(zip(blocks, all_logits)):
        lane = lax.broadcasted_iota(jnp.int32, logits.shape, 1)
        neg = jnp.float32(-jnp.inf)
        big = jnp.int32(4 * LANES)
        first = lambda hit: jnp.min(jnp.where(hit, lane, big), axis=-1, keepdims=True)
        gl = jnp.where(lane < G, logits, neg)
        gmax = jnp.max(gl, axis=-1, keepdims=True)
        gidx = first(gl == gmax)
        grp_gate = 1.0 / jnp.sum(jnp.exp(gl - gmax), axis=-1, keepdims=True)
        in_grp = (lane >= G) & (lane < G + G * P) & (((lane - G) // P) == gidx)
        ll = jnp.where(in_grp, logits, neg)
        m1 = jnp.max(ll, axis=-1, keepdims=True)
        i1 = first(ll == m1)
        denom = jnp.sum(jnp.exp(ll - m1), axis=-1, keepdims=True)
        ll2 = jnp.where(lane == i1, neg, ll)
        m2 = jnp.max(ll2, axis=-1, keepdims=True)
        i2 = first(ll2 == m2)
        p1 = 1.0 / denom
        p2 = jnp.exp(m2 - m1) / denom
        gate1 = grp_gate * (p1 / (p1 + p2))
        gate2 = grp_gate * (p2 / (p1 + p2))
        e1 = i1 - G
        e2 = i2 - G

        oh1 = jnp.where(lane == e1, 1.0, 0.0)
        oh2 = jnp.where(lane == e2, 1.0, 0.0)
        oh = oh1 + oh2
        ii = lax.broadcasted_iota(jnp.int32, (MIX_ROWS, MIX_ROWS), 0)
        jj = lax.broadcasted_iota(jnp.int32, (MIX_ROWS, MIX_ROWS), 1)
        before = _dot(jnp.where(ii > jj, 1.0, 0.0), oh) + run
        r1 = jnp.sum(oh1 * before, axis=-1, keepdims=True)
        r2 = jnp.sum(oh2 * before, axis=-1, keepdims=True)
        run = run + jnp.sum(oh, axis=0, keepdims=True)

        rec = jnp.where(lane == ROUTE_GATE, gate1, 0.0)
        rec = jnp.where(lane == ROUTE_GATE + 1, gate2, rec)
        rec = jnp.where(lane == ROUTE_EXPERT, e1.astype(F32), rec)
        rec = jnp.where(lane == ROUTE_EXPERT + 1, e2.astype(F32), rec)
        rec = jnp.where(lane == ROUTE_RANK, r1, rec)
        rec = jnp.where(lane == ROUTE_RANK + 1, r2, rec)
        route_ref[rs, :] = rec
        route_t_ref[:, s * MIX_ROWS:(s + 1) * MIX_ROWS] = jnp.transpose(rec)[:SUBLANES, :]

    run_scr[...] = jnp.broadcast_to(run, run_scr.shape)
    cnt_ref[...] = run_scr[...]


def _mix_route(yconv, x0c, o_f, o_b, p, z_col, hy_norm_w, gdn_norm_w, xf, w_out_bf16, norm2_w, wr, br,
               n_heads, head_dim, n_groups, per_group, tm=512):
    M, D = xf.shape
    d_hy = yconv.shape[1]
    d_gdn = o_f.shape[1]
    assert z_col % d_gdn == 0 and n_groups * (per_group + 1) <= LANES
    zb = z_col // d_gdn
    row = lambda i: (i, 0)
    const = lambda i: (0, 0)
    kern = functools.partial(_mix_route_kernel, n_heads=n_heads, head_dim=head_dim,
                             n_groups=n_groups, per_group=per_group)
    return pl.pallas_call(
        kern,
        grid=(M // tm,),
        in_specs=[pl.BlockSpec((tm, d_hy), row), pl.BlockSpec((tm, d_hy), row),
                  pl.BlockSpec((tm, d_gdn), row), pl.BlockSpec((tm, d_gdn), row),
                  pl.BlockSpec((tm, d_gdn), lambda i: (i, zb)),
                  pl.BlockSpec((1, d_hy), const), pl.BlockSpec((1, head_dim), const),
                  pl.BlockSpec((tm, D), row), pl.BlockSpec(w_out_bf16.shape, const, pipeline_mode=pl.Buffered(1)),
                  pl.BlockSpec((1, D), const), pl.BlockSpec((D, LANES), const), pl.BlockSpec((1, LANES), const)],
        out_specs=[pl.BlockSpec((tm, D), row), pl.BlockSpec((tm * (D // 2 // LANES), LANES), row),
                   pl.BlockSpec((tm, LANES), row), pl.BlockSpec((SUBLANES, tm), lambda i: (0, i)),
                   pl.BlockSpec((SUBLANES, LANES), const)],
        out_shape=[jax.ShapeDtypeStruct((M, D), F32), jax.ShapeDtypeStruct((M * (D // 2 // LANES), LANES), jnp.uint32),
                   jax.ShapeDtypeStruct((M, LANES), F32), jax.ShapeDtypeStruct((SUBLANES, M), F32),
                   jax.ShapeDtypeStruct((SUBLANES, LANES), F32)],
        scratch_shapes=[pltpu.VMEM((SUBLANES, LANES), F32)],
        compiler_params=_cparams(("arbitrary",), 58),
        name="mix_route",
    )(yconv, x0c, o_f, o_b, p, hy_norm_w.reshape(1, d_hy), gdn_norm_w.reshape(1, head_dim), xf, w_out_bf16,
      norm2_w.reshape(1, D), wr, br)


def _experts_kernel(ts_ref, se_ref, nn_ref, tgt_nxt_ref, tgt_ref, tgt_prv_ref, u_hbm, w1_hbm, w3_hbm, w2_hbm, out_hbm,
                    x0, x1, y0, y1, wf1, wf3, wf2, w1b, w3b, w2b, gsem, ssem, wsem, *, n_tokens):
    i = pl.program_id(0)
    na = nn_ref[0]
    ns = nn_ref[1]
    CH = w1b.shape[0] // 2 // LANES
    TB = x0.shape[0] // CH
    spare0 = 2 * n_tokens
    W_SLOTS = wf1.shape[0]

    def weight_copies(s):
        slot = lax.rem(s, W_SLOTS)
        e = se_ref[s]
        return [pltpu.make_async_copy(w_hbm.at[e], wf.at[slot], wsem.at[slot])
                for w_hbm, wf in ((w1_hbm, wf1), (w3_hbm, wf3), (w2_hbm, wf2))]

    def token_of(v):
        if n_tokens & (n_tokens - 1) == 0:
            return v & (n_tokens - 1)
        return lax.rem(v, n_tokens)

    def tile_of(row):
        return pl.ds(pl.multiple_of(row * CH, CH), CH)

    def gather(tgt, xbuf, s, r):
        return pltpu.make_async_copy(u_hbm.at[tile_of(token_of(tgt[r]))], xbuf.at[tile_of(r)], gsem.at[s])

    def scatter(ybuf, s, r, dst):
        return pltpu.make_async_copy(ybuf.at[tile_of(r)], out_hbm.at[tile_of(dst)], ssem.at[s])

    def gather_wait(xbuf, s):
        pltpu.make_async_copy(u_hbm.at[pl.ds(0, TB * CH)], xbuf, gsem.at[s]).wait()

    def scatter_wait(ybuf, s):
        pltpu.make_async_copy(ybuf, out_hbm.at[pl.ds(0, TB * CH)], ssem.at[s]).wait()

    def each_row(fn):
        def body(r, carry):
            fn(r)
            return carry
        lax.fori_loop(0, TB, body, 0, unroll=8)

    def step(p):
        q = 1 - p
        x_cur, x_nxt = (x0, x1) if p == 0 else (x1, x0)
        y_cur, y_prv = (y0, y1) if p == 0 else (y1, y0)
        gather_wait(x_cur, p)

        @pl.when(i > 0)
        def _():
            scatter_wait(y_cur, p)

        x = _unpack_bf16_pairs(_load_token_tiles(x_cur, 0, TB, CH)).astype(BF16)
        for r in range(TB):
            gather(tgt_nxt_ref, x_nxt, q, r).start()
        first = i == 0
        for r in range(TB):
            scatter(y_prv, q, r, jnp.where(first, spare0 + r, tgt_prv_ref[r])).start()
        h = _silu(jnp.dot(x, w1b[...], preferred_element_type=F32)) * jnp.dot(x, w3b[...], preferred_element_type=F32)
        _store_token_tiles(y_cur, 0, _pack_bf16_pairs(jnp.dot(h.astype(BF16), w2b[...], preferred_element_type=F32)))

        @pl.when(i == na - 1)
        def _():
            gather_wait(x_nxt, q)
            scatter_wait(y_prv, q)
            each_row(lambda r: scatter(y_cur, p, r, tgt_ref[r]).start())
            scatter_wait(y_cur, p)

    @pl.when(i < na)
    def _():
        @pl.when(i == 0)
        def _():
            y0[...] = jnp.zeros_like(y0)
            y1[...] = jnp.zeros_like(y1)
            fill = pltpu.make_async_copy(y1, out_hbm.at[pl.ds(spare0 * CH, TB * CH)], ssem.at[1])
            fill.start()
            fill.wait()
            each_row(lambda r: gather(tgt_ref, x0, 0, r).start())
            for s0 in range(W_SLOTS - 1):
                @pl.when(s0 < ns)
                def _():
                    for c in weight_copies(s0):
                        c.start(priority=WEIGHT_DMA_PRIORITY)

        s = ts_ref[i]

        @pl.when((i == 0) | (s != ts_ref[jnp.maximum(i - 1, 0)]))
        def _():
            @pl.when(s + W_SLOTS - 1 < ns)
            def _():
                for c in weight_copies(s + W_SLOTS - 1):
                    c.start(priority=WEIGHT_DMA_PRIORITY)

            for c in weight_copies(s):
                c.wait()
            slot = lax.rem(s, W_SLOTS)
            w1b[...] = wf1[slot].astype(BF16)
            w3b[...] = wf3[slot].astype(BF16)
            w2b[...] = wf2[slot].astype(BF16)

        parity = lax.rem(i, 2)

        @pl.when(parity == 0)
        def _():
            step(0)

        @pl.when(parity == 1)
        def _():
            step(1)


WEIGHT_SLOTS = 3
WEIGHT_DMA_PRIORITY = 1


def _experts(u, slot_tgt, tile_seq, seq_expert, n_active_seq, w1, w3, w2):
    E, D, de = w1.shape
    CH = D // 2 // LANES
    assert CH % SUBLANES == 0
    T = u.shape[0] // CH
    TB = EXPERT_ROWS
    n_tiles = slot_tgt.shape[0] // TB
    table = lambda f: pl.BlockSpec((TB,), lambda i, ts, se, nn: (f(i),), memory_space=pltpu.SMEM)
    hbm = pl.BlockSpec(memory_space=pl.ANY)
    grid_spec = pltpu.PrefetchScalarGridSpec(
        num_scalar_prefetch=3,
        grid=(n_tiles,),
        in_specs=[table(lambda i: jnp.minimum(i + 1, n_tiles - 1)), table(lambda i: i),
                  table(lambda i: jnp.maximum(i - 1, 0)), hbm, hbm, hbm, hbm],
        out_specs=hbm,
        scratch_shapes=[pltpu.VMEM((TB * CH, LANES), jnp.uint32)] * 4
                       + [pltpu.VMEM((WEIGHT_SLOTS, D, de), F32), pltpu.VMEM((WEIGHT_SLOTS, D, de), F32),
                          pltpu.VMEM((WEIGHT_SLOTS, de, D), F32),
                          pltpu.VMEM((D, de), BF16), pltpu.VMEM((D, de), BF16), pltpu.VMEM((de, D), BF16),
                          pltpu.SemaphoreType.DMA((2,)), pltpu.SemaphoreType.DMA((2,)),
                          pltpu.SemaphoreType.DMA((WEIGHT_SLOTS,))],
    )
    return pl.pallas_call(
        functools.partial(_experts_kernel, n_tokens=T),
        grid_spec=grid_spec,
        out_shape=jax.ShapeDtypeStruct(((2 * T + TB) * CH, LANES), jnp.uint32),
        compiler_params=_cparams(("arbitrary",), 58),
        name="experts",
    )(tile_seq, seq_expert, n_active_seq, slot_tgt, slot_tgt, slot_tgt, u, w1, w3, w2)


def _slot_table_kernel(dest_ref, init_hbm, o_ref, sem):
    fill = pltpu.make_async_copy(init_hbm, o_ref, sem)
    fill.start()
    fill.wait()

    def put(a, carry):
        o_ref[dest_ref[a]] = a
        return carry

    lax.fori_loop(0, dest_ref.shape[0], put, 0, unroll=8)


def _slot_table(dest, init):
    smem = pl.BlockSpec(memory_space=pltpu.SMEM)
    return pl.pallas_call(
        _slot_table_kernel,
        in_specs=[smem, pl.BlockSpec(memory_space=pl.ANY)],
        out_specs=smem,
        out_shape=jax.ShapeDtypeStruct(init.shape, jnp.int32),
        scratch_shapes=[pltpu.SemaphoreType.DMA],
        name="slot_table",
    )(dest, init)


def _dispatch_tables(route_t, counts, n_experts):
    T = route_t.shape[1]
    TB = EXPERT_ROWS
    e = route_t[ROUTE_EXPERT:ROUTE_EXPERT + 2].astype(jnp.int32)
    rank = route_t[ROUTE_RANK:ROUTE_RANK + 2].astype(jnp.int32)
    cnt = counts[0, :n_experts].astype(jnp.int32)
    padded = (cnt + TB - 1) // TB * TB
    pad_end = jnp.cumsum(padded)
    pad_start = pad_end - padded
    ids = jnp.arange(n_experts, dtype=jnp.int32)[:, None, None]
    start_of = jnp.sum(jnp.where(e[None] == ids, pad_start[:, None, None], 0), axis=0)
    dest = start_of + rank
    n_tiles = -(-(2 * T + n_experts * (TB - 1)) // TB)
    padded_slot = 2 * T + jnp.arange(n_tiles * TB, dtype=jnp.int32) % TB
    slot_tgt = _slot_table(dest.reshape(-1), padded_slot)
    start = jnp.arange(n_tiles, dtype=jnp.int32) * TB
    tile_expert = jnp.sum(start[:, None] >= pad_end[None, :], axis=1)
    tile_expert = jnp.minimum(tile_expert, n_experts - 1).astype(jnp.int32)
    owns = cnt > 0
    pos = jnp.cumsum(owns.astype(jnp.int32)) - 1
    experts = jnp.arange(n_experts, dtype=jnp.int32)
    seq_expert = jnp.sum(jnp.where(owns[None, :] & (pos[None, :] == experts[:, None]), experts[None, :], 0), axis=1)
    tile_seq = jnp.sum(jnp.where(tile_expert[:, None] == experts[None, :], pos[None, :], 0), axis=1)
    n_active_seq = jnp.stack([pad_end[-1] // TB, jnp.sum(owns.astype(jnp.int32))]).astype(jnp.int32)
    return slot_tgt, tile_seq.astype(jnp.int32), seq_expert.astype(jnp.int32), n_active_seq


def _combine_kernel(x2_ref, e0_ref, e1_ref, route_ref, w_ref, o_ref, *, final_norm):
    r = route_ref[...]
    tm = x2_ref.shape[0]
    ch = e0_ref.shape[0] // tm
    e0 = _unpack_bf16_pairs(_load_token_tiles(e0_ref, 0, tm, ch))
    e1 = _unpack_bf16_pairs(_load_token_tiles(e1_ref, 0, tm, ch))
    y = x2_ref[...] + r[:, ROUTE_GATE:ROUTE_GATE + 1] * e0 + r[:, ROUTE_GATE + 1:ROUTE_GATE + 2] * e1
    if final_norm:
        y = y * lax.rsqrt(jnp.mean(y * y, axis=-1, keepdims=True) + EPS) * w_ref[...]
    o_ref[...] = y


def _combine(x2, planes, route, norm_w, final_norm, tm=512):
    M, D = x2.shape
    return pl.pallas_call(
        functools.partial(_combine_kernel, final_norm=final_norm),
        grid=(M // tm,),
        in_specs=[pl.BlockSpec((tm, D), lambda i: (i, 0)),
                  pl.BlockSpec((tm * (D // 2 // LANES), LANES), lambda i: (i, 0)),
                  pl.BlockSpec((tm * (D // 2 // LANES), LANES), lambda i: (M // tm + i, 0)),
                  pl.BlockSpec((tm, LANES), lambda i: (i, 0)),
                  pl.BlockSpec((1, D), lambda i: (0, 0))],
        out_specs=pl.BlockSpec((tm, D), lambda i: (i, 0)),
        out_shape=jax.ShapeDtypeStruct((M, D), F32),
        compiler_params=_cparams(("parallel",), 48),
        name="combine",
    )(x2, planes, planes, route, norm_w.reshape(1, D))


def kernel(x, norm1_w, w_in, hy_conv_w, hy_conv_b, hy_filt_w1, hy_filt_b1, hy_filt_w2, hy_filt_b2, hy_filt_w3, hy_filt_b3, hy_filt_w4, hy_sin_freq, hy_skip, hy_norm_w, gdn_conv_w, gdn_a_log_f, gdn_a_log_b, gdn_dt_bias_f, gdn_dt_bias_b, gdn_norm_w, w_out, norm2_w, router_group_w, router_group_b, router_expert_w, router_expert_b, exp_w1, exp_w3, exp_w2, final_norm_w):
    B, L, D = x.shape
    M = B * L
    depth = w_in.shape[0]
    d_hy = hy_skip.shape[-1]
    H = gdn_a_log_f.shape[-1]
    Dh = gdn_norm_w.shape[-1]
    d_gdn = H * Dh
    n_main = 3 * d_hy + 4 * d_gdn
    G = router_group_w.shape[-1]
    E = router_expert_w.shape[-1]
    xf = x.reshape(M, D)
    for l in range(depth):
        p, pg = _inproj(xf, norm1_w[l], jnp.swapaxes(w_in[l], 0, 1), n_main, 4 * H)
        p3 = p.reshape(B, L, n_main)
        x0c, yconv = _hyena_branch(p3, hy_conv_w[l], hy_conv_b[l], hy_filt_w1[l], hy_filt_b1[l], hy_filt_w2[l],
                                   hy_filt_b2[l], hy_filt_w3[l], hy_filt_b3[l], hy_filt_w4[l], hy_sin_freq[l],
                                   hy_skip[l], d_hy)
        o_f, o_b = _gdn_branch(p3, pg, gdn_conv_w[l], gdn_a_log_f[l], gdn_a_log_b[l], gdn_dt_bias_f[l],
                               gdn_dt_bias_b[l], 3 * d_hy, H, Dh)
        wr = jnp.pad(jnp.concatenate([router_group_w[l], router_expert_w[l]], axis=1), ((0, 0), (0, LANES - G - E)))
        br = jnp.pad(jnp.concatenate([router_group_b[l], router_expert_b[l]]), (0, LANES - G - E)).reshape(1, LANES)
        x2, u, route, route_t, counts = _mix_route(
            yconv.reshape(M, d_hy), x0c.reshape(M, d_hy), o_f.reshape(M, d_gdn), o_b.reshape(M, d_gdn), p,
            3 * d_hy + 3 * d_gdn, hy_norm_w[l], gdn_norm_w[l], xf, w_out[l].astype(BF16), norm2_w[l], wr, br,
            H, Dh, G, E // G)
        slot_tgt, tile_seq, seq_expert, n_active_seq = _dispatch_tables(route_t, counts, E)
        planes = _experts(u, slot_tgt, tile_seq, seq_expert, n_active_seq, exp_w1[l], exp_w3[l], exp_w2[l])
        xf = _combine(x2, planes, route, final_norm_w, final_norm=(l == depth - 1))
    return xf.reshape(B, L, D)
```

```python
import functools
import math

import jax
import jax.numpy as jnp
import numpy as np
from jax import lax
from jax.experimental import pallas as pl
from jax.experimental.pallas import tpu as pltpu

F32 = jnp.float32
BF16 = jnp.bfloat16
EPS = 1e-6
LANES = 128
SUBLANES = 8
VMEM_BYTES_V7X = 64 * 1024 * 1024
GDN_CHUNK = 64
FFT_N2 = 128
EXPERT_ROWS = 256
DECAY_TARGET = 1e-2
FAST_DECAY_PCT = 0.3
SLOW_DECAY_PCT = 1.5


def _cparams(sem, vmem_mb):
    return pltpu.CompilerParams(dimension_semantics=sem, vmem_limit_bytes=int(vmem_mb * 1024 * 1024))


def _dot(a, b):
    return jnp.dot(a.astype(BF16), b.astype(BF16), preferred_element_type=F32)


def _dot_nt(a, b):
    return lax.dot_general(a.astype(BF16), b.astype(BF16), (((1,), (1,)), ((), ())), preferred_element_type=F32)


def _dot_tn(a, b):
    return lax.dot_general(a.astype(BF16), b.astype(BF16), (((0,), (0,)), ((), ())), preferred_element_type=F32)


def _dot_hi(a, b):
    return jnp.dot(a, b, preferred_element_type=F32, precision=lax.Precision.HIGHEST)


def _silu(x):
    return x * jax.nn.sigmoid(x)


def _pack_bf16_pairs(x):
    c = x.shape[1] // 2
    lo = lax.bitcast_convert_type(x[:, :c].astype(BF16).astype(F32), jnp.uint32) >> 16
    hi = lax.bitcast_convert_type(x[:, c:].astype(BF16).astype(F32), jnp.uint32) & jnp.uint32(0xFFFF0000)
    return hi | lo


def _unpack_bf16_pairs(w):
    lo = lax.bitcast_convert_type(w << 16, F32)
    hi = lax.bitcast_convert_type(w & jnp.uint32(0xFFFF0000), F32)
    return jnp.concatenate([lo, hi], axis=1)


def _store_token_tiles(ref, row0, words):
    n, width = words.shape
    ch = width // LANES
    for c in range(ch):
        ref[pl.ds(row0 * ch + c, n, stride=ch), :] = words[:, c * LANES:(c + 1) * LANES]


def _load_token_tiles(ref, row0, n, ch):
    return jnp.concatenate([ref[pl.ds(row0 * ch + c, n, stride=ch), :] for c in range(ch)], axis=1)


def _inproj_kernel(x_ref, nw_ref, wt_ref, wgt_ref, p_ref, g_ref, h_scr):
    @pl.when(pl.program_id(1) == 0)
    def _():
        x = x_ref[...]
        h = x * lax.rsqrt(jnp.mean(x * x, axis=-1, keepdims=True) + EPS) * nw_ref[...]
        h_scr[...] = h.astype(BF16)
        g = _dot_nt(h, wgt_ref[...])
        g_ref[...] = jnp.concatenate([g, jnp.zeros((g.shape[0], LANES - g.shape[1]), F32)], axis=1)

    p_ref[...] = _dot_nt(h_scr[...], wt_ref[...]).astype(p_ref.dtype)


def _inproj(xf, norm_w, wt, n_main, n_gate, tm=2048, tn=512):
    M, D = xf.shape
    assert n_main % tn == 0 and M % tm == 0 and n_main % n_gate == 0 and n_gate % SUBLANES == 0
    return pl.pallas_call(
        _inproj_kernel,
        grid=(M // tm, n_main // tn),
        in_specs=[
            pl.BlockSpec((tm, D), lambda i, j: (i, 0), pipeline_mode=pl.Buffered(1)),
            pl.BlockSpec((1, D), lambda i, j: (0, 0)),
            pl.BlockSpec((tn, D), lambda i, j: (j, 0)),
            pl.BlockSpec((n_gate, D), lambda i, j: (n_main // n_gate, 0)),
        ],
        out_specs=[
            pl.BlockSpec((tm, tn), lambda i, j: (i, j)),
            pl.BlockSpec((tm, LANES), lambda i, j: (i, 0)),
        ],
        out_shape=[jax.ShapeDtypeStruct((M, n_main), BF16), jax.ShapeDtypeStruct((M, LANES), F32)],
        scratch_shapes=[pltpu.VMEM((tm, D), BF16)],
        compiler_params=_cparams(("parallel", "arbitrary"), 57),
        name="inproj",
    )(xf, norm_w.reshape(1, D), wt, wt)


def _conv3_rows(ref, r0, rows, w, n_rows):
    cur = ref[pl.ds(r0, rows), :].astype(F32)
    lo = jnp.maximum(r0 - HALO_ROWS, 0)
    hi = jnp.minimum(r0 + rows, n_rows - HALO_ROWS)
    prev_grp = ref[pl.ds(pl.multiple_of(lo, HALO_ROWS), HALO_ROWS), :].astype(F32)
    next_grp = ref[pl.ds(pl.multiple_of(hi, HALO_ROWS), HALO_ROWS), :].astype(F32)
    prev_row = jnp.where(r0 > 0, prev_grp[HALO_ROWS - 1:HALO_ROWS, :], 0.0)
    next_row = jnp.where(r0 + rows < n_rows, next_grp[0:1, :], 0.0)
    row = lax.broadcasted_iota(jnp.int32, cur.shape, 0)
    xm = jnp.where(row == 0, prev_row, pltpu.roll(cur, 1, 0))
    xp = jnp.where(row == rows - 1, next_row, pltpu.roll(cur, rows - 1, 0))
    return xm * w[0:1, :] + cur * w[1:2, :] + xp * w[2:3, :]


CONV_ROWS = 256
HALO_ROWS = 16


def _hy_pre_kernel(x0_ref, x1_ref, v_ref, w0_ref, w1_ref, w2_ref, b0_ref, b1_ref, b2_ref, x0c_ref, vg_ref):
    L = x0_ref.shape[0]
    w0, w1, w2 = w0_ref[...], w1_ref[...], w2_ref[...]
    b0, b1, b2 = b0_ref[...], b1_ref[...], b2_ref[...]

    def body(c, carry):
        r0 = pl.multiple_of(c * CONV_ROWS, CONV_ROWS)
        x0c_ref[pl.ds(r0, CONV_ROWS), :] = (_conv3_rows(x0_ref, r0, CONV_ROWS, w0, L) + b0).astype(x0c_ref.dtype)
        x1c = _conv3_rows(x1_ref, r0, CONV_ROWS, w1, L) + b1
        vc = _conv3_rows(v_ref, r0, CONV_ROWS, w2, L) + b2
        vg_ref[pl.ds(r0, CONV_ROWS), :] = (vc * x1c).astype(vg_ref.dtype)
        return carry

    lax.fori_loop(0, L // CONV_ROWS, body, 0, unroll=2)


def _hy_pre(p3, conv_w, conv_b, d_hy):
    B, L, _ = p3.shape
    nt = d_hy // LANES
    bias = conv_b.reshape(1, -1)
    pspec = lambda off: pl.BlockSpec((None, L, LANES), lambda b, c: (b, 0, c + off))
    wspec = lambda off: pl.BlockSpec((3, LANES), lambda b, c: (0, c + off))
    bspec = lambda off: pl.BlockSpec((1, LANES), lambda b, c: (0, c + off))
    ospec = pl.BlockSpec((None, L, LANES), lambda b, c: (b, 0, c))
    return pl.pallas_call(
        _hy_pre_kernel,
        grid=(B, nt),
        in_specs=[pspec(0), pspec(nt), pspec(2 * nt), wspec(0), wspec(nt), wspec(2 * nt),
                  bspec(0), bspec(nt), bspec(2 * nt)],
        out_specs=[ospec, ospec],
        out_shape=[jax.ShapeDtypeStruct((B, L, d_hy), BF16)] * 2,
        compiler_params=_cparams(("parallel", "parallel"), 40),
        name="hy_pre",
    )(p3, p3, p3, conv_w, conv_w, conv_w, bias, bias, bias)


def _gdn_pre_kernel(x_ref, w_ref, o_ref, *, n_heads, head_dim):
    L = x_ref.shape[0]
    w = w_ref[...]
    c = pl.program_id(1)
    q_scale = jnp.where(c < n_heads, head_dim ** -0.5, 1.0)
    is_qk = c < 2 * n_heads

    def body(i, carry):
        r0 = pl.multiple_of(i * CONV_ROWS, CONV_ROWS)
        y = _silu(_conv3_rows(x_ref, r0, CONV_ROWS, w, L))
        inv = lax.rsqrt(jnp.sum(y * y, axis=-1, keepdims=True) + EPS) * q_scale
        o_ref[pl.ds(r0, CONV_ROWS), :] = y * jnp.where(is_qk, inv, 1.0)
        return carry

    lax.fori_loop(0, L // CONV_ROWS, body, 0, unroll=2)


def _gdn_pre(p3, conv_w, col0, n_heads, head_dim):
    B, L, _ = p3.shape
    assert head_dim == LANES
    nt = 3 * n_heads
    off = col0 // LANES
    return pl.pallas_call(
        functools.partial(_gdn_pre_kernel, n_heads=n_heads, head_dim=head_dim),
        grid=(B, nt),
        in_specs=[pl.BlockSpec((None, L, LANES), lambda b, c: (b, 0, c + off)),
                  pl.BlockSpec((3, LANES), lambda b, c: (0, c))],
        out_specs=pl.BlockSpec((None, L, LANES), lambda b, c: (b, 0, c)),
        out_shape=jax.ShapeDtypeStruct((B, L, nt * LANES), F32),
        compiler_params=_cparams(("parallel", "parallel"), 24),
        name="gdn_pre",
    )(p3, conv_w)


GATE_ROWS = 512


def _gdn_gates_kernel(pg_ref, alog_ref, dtb_ref, o_ref, *, n_heads):
    H = n_heads
    x = pg_ref[...]
    beta = jax.nn.sigmoid(x)
    z = x + dtb_ref[...]
    softplus = jnp.maximum(z, 0.0) + jnp.log1p(jnp.exp(-jnp.abs(z)))
    g = -jnp.exp(alog_ref[...]) * softplus
    pos = lax.broadcasted_iota(jnp.int32, x.shape, 0) & (GDN_CHUNK - 1)
    gc_f = g
    gc_b = g
    step = 1
    while step < GDN_CHUNK:
        gc_f = gc_f + jnp.where(pos >= step, pltpu.roll(gc_f, step, 0), 0.0)
        gc_b = gc_b + jnp.where(pos < GDN_CHUNK - step, pltpu.roll(gc_b, GATE_ROWS - step, 0), 0.0)
        step *= 2
    g_tot = pltpu.roll(gc_f + gc_b - g, 4 * H, 1)
    lane = lax.broadcasted_iota(jnp.int32, x.shape, 1)
    out = jnp.where(lane < 2 * H, beta,
                    jnp.where(lane < 3 * H, gc_f,
                              jnp.where(lane < 4 * H, gc_b,
                                        jnp.where((lane >= 6 * H) & (lane < 8 * H), g_tot, 0.0))))
    o_ref[...] = out


def _gdn_gates(pg, a_log_f, a_log_b, dt_bias_f, dt_bias_b, n_heads):
    M = pg.shape[0]
    H = n_heads
    assert 8 * H <= LANES
    pad = lambda a, b: jnp.concatenate([jnp.zeros((2 * H,), F32), a.astype(F32), b.astype(F32),
                                        jnp.zeros((LANES - 4 * H,), F32)]).reshape(1, LANES)
    return pl.pallas_call(
        functools.partial(_gdn_gates_kernel, n_heads=H),
        grid=(M // GATE_ROWS,),
        in_specs=[pl.BlockSpec((GATE_ROWS, LANES), lambda i: (i, 0)),
                  pl.BlockSpec((1, LANES), lambda i: (0, 0)),
                  pl.BlockSpec((1, LANES), lambda i: (0, 0))],
        out_specs=pl.BlockSpec((GATE_ROWS, LANES), lambda i: (i, 0)),
        out_shape=jax.ShapeDtypeStruct((M, LANES), F32),
        compiler_params=_cparams(("parallel",), 24),
        name="gdn_gates",
    )(pg, pad(a_log_f, a_log_b), pad(dt_bias_f, dt_bias_b))


def _delta_chunks(q, k, v, beta, gc_col, gc_row, gtot, state, lower):
    n = len(q)
    C = q[0].shape[0]
    D = k[0].shape[1]
    ii = lax.broadcasted_iota(jnp.int32, (C, C), 0)
    jj = lax.broadcasted_iota(jnp.int32, (C, C), 1)
    eye = jnp.where(ii == jj, 1.0, 0.0)
    incl = [(ii >= jj) if lo else (ii <= jj) for lo in lower]
    strict = [(ii > jj) if lo else (ii < jj) for lo in lower]
    rng = range(n)
    decay = [jnp.where(incl[i], jnp.exp(jnp.where(incl[i], gc_col[i] - gc_row[i], 0.0)), 0.0) for i in rng]
    kb = [k[i] * beta[i] for i in rng]
    kk = [_dot_nt(kb[i], k[i]) for i in rng]
    qk = [_dot_nt(q[i], k[i]) for i in rng]
    m = [jnp.where(strict[i], -(kk[i] * decay[i]), 0.0) for i in rng]
    r = [eye + m[i] for i in rng]
    m = [_dot(m[i], m[i]) for i in rng]
    for _ in range(int(math.log2(C)) - 2):
        rm = [_dot(jnp.concatenate([r[i], m[i]], axis=0), m[i]) for i in rng]
        r = [r[i] + rm[i][:C] for i in rng]
        m = [rm[i][C:] for i in rng]
    r = [r[i] + _dot(r[i], m[i]) for i in rng]
    eg = [jnp.exp(gc_col[i]) for i in rng]
    wu = [_dot(r[i], jnp.concatenate([kb[i] * eg[i], v[i] * beta[i]], axis=1)) for i in rng]
    ws = [_dot(jnp.concatenate([wu[i][:, :D], q[i] * eg[i]], axis=0), state[i]) for i in rng]
    v_new = [wu[i][:, D:] - ws[i][:C] for i in rng]
    qkm = [jnp.where(incl[i], qk[i] * decay[i], 0.0) for i in rng]
    out = [ws[i][C:] + _dot(qkm[i], v_new[i]) for i in rng]
    k_dec = [k[i] * jnp.exp(gtot[i] - gc_col[i]) for i in rng]
    new_state = [state[i] * jnp.exp(gtot[i][0:1, :]) + _dot_tn(k_dec[i], v_new[i]) for i in rng]
    return out, new_state


def _gdn_scan_kernel(qf_ref, kf_ref, vf_ref, qb_ref, kb_ref, vb_ref, gf_ref, gb_ref, rf_ref, rb_ref,
                     of_ref, ob_ref, s_scr, *, n_heads, head_dim):
    H, Dh = n_heads, head_dim
    B = qf_ref.shape[0]

    @pl.when(pl.program_id(0) == 0)
    def _():
        s_scr[...] = jnp.zeros_like(s_scr)

    col = lambda g, j: g[:, j:j + 1]
    sls = [slice(h * Dh, (h + 1) * Dh) for h in range(H)]
    q, k, v, beta, gc_col, gc_row, gtot, state = ([] for _ in range(8))
    for b in range(B):
        gf = gf_ref[b]
        gb = gb_ref[b]
        q += [qf_ref[b, :, sl] for sl in sls] + [qb_ref[b, :, sl] for sl in sls]
        k += [kf_ref[b, :, sl] for sl in sls] + [kb_ref[b, :, sl] for sl in sls]
        v += [vf_ref[b, :, sl] for sl in sls] + [vb_ref[b, :, sl] for sl in sls]
        beta += [col(gf, h) for h in range(H)] + [col(gb, H + h) for h in range(H)]
        gc_col += [col(gf, 2 * H + h) for h in range(H)] + [col(gb, 3 * H + h) for h in range(H)]
        gc_row += [rf_ref[b, h:h + 1, :] for h in range(H)] + [rb_ref[b, H + h:H + h + 1, :] for h in range(H)]
        gtot += [col(gf, 6 * H + h) for h in range(H)] + [col(gb, 7 * H + h) for h in range(H)]
        state += [s_scr[b, 0, h] for h in range(H)] + [s_scr[b, 1, h] for h in range(H)]
    out, new_state = _delta_chunks(q, k, v, beta, gc_col, gc_row, gtot, state, ([True] * H + [False] * H) * B)
    for b in range(B):
        for h in range(H):
            of_ref[b, :, sls[h]] = out[2 * H * b + h]
            ob_ref[b, :, sls[h]] = out[2 * H * b + H + h]
            s_scr[b, 0, h] = new_state[2 * H * b + h]
            s_scr[b, 1, h] = new_state[2 * H * b + H + h]


def _gdn_scan(qkv, gates, gates_row, n_heads, head_dim):
    B, L, _ = qkv.shape
    H, Dh = n_heads, head_dim
    d = H * Dh
    C = GDN_CHUNK
    N = L // C
    fwd = lambda col: pl.BlockSpec((B, C, d), lambda n: (0, n, col))
    bwd = lambda col: pl.BlockSpec((B, C, d), lambda n: (0, N - 1 - n, col))
    return pl.pallas_call(
        functools.partial(_gdn_scan_kernel, n_heads=H, head_dim=Dh),
        grid=(N,),
        in_specs=[fwd(0), fwd(1), fwd(2), bwd(0), bwd(1), bwd(2),
                  pl.BlockSpec((B, C, LANES), lambda n: (0, n, 0)),
                  pl.BlockSpec((B, C, LANES), lambda n: (0, N - 1 - n, 0)),
                  pl.BlockSpec((B, None, 2 * H, C), lambda n: (0, n, 0, 0)),
                  pl.BlockSpec((B, None, 2 * H, C), lambda n: (0, N - 1 - n, 0, 0))],
        out_specs=[pl.BlockSpec((B, C, d), lambda n: (0, n, 0)),
                   pl.BlockSpec((B, C, d), lambda n: (0, N - 1 - n, 0))],
        out_shape=[jax.ShapeDtypeStruct((B, L, d), F32)] * 2,
        scratch_shapes=[pltpu.VMEM((B, 2, H, Dh, Dh), F32)],
        compiler_params=_cparams(("arbitrary",), 32),
        name="gdn_scan",
    )(qkv, qkv, qkv, qkv, qkv, qkv, gates, gates, gates_row, gates_row)


def _gdn_branch(p3, pg, conv_w, a_log_f, a_log_b, dt_bias_f, dt_bias_b, col0, n_heads, head_dim):
    B, L, _ = p3.shape
    H = n_heads
    qkv = _gdn_pre(p3, conv_w, col0, H, head_dim)
    gates = _gdn_gates(pg, a_log_f, a_log_b, dt_bias_f, dt_bias_b, H).reshape(B, L, LANES)
    N = L // GDN_CHUNK
    gates_row = gates[..., 2 * H:4 * H].reshape(B, N, GDN_CHUNK, 2 * H).transpose(0, 1, 3, 2)
    return _gdn_scan(qkv, gates, gates_row, H, head_dim)


@functools.lru_cache(maxsize=None)
def _filter_positions(L, pos_emb_dim):
    n = 2 * L
    r = np.arange(n)
    k = np.where(r < L, r, np.where(r == L, 0, n - r)).astype(np.float64)
    t = k / (L - 1)
    bands = (pos_emb_dim - 1) // 2
    fb = np.linspace(1e-4, bands - 1, bands)
    ang = (2.0 * math.pi / L) * k[:, None] * fb[None, :]
    z = np.concatenate([t[:, None], np.cos(ang), -np.sin(ang)], axis=-1)
    return z.astype(np.float32)


@functools.lru_cache(maxsize=None)
def _decay_rates(d_hy):
    max_decay = math.log(DECAY_TARGET) / FAST_DECAY_PCT
    min_decay = math.log(DECAY_TARGET) / SLOW_DECAY_PCT
    return np.abs(np.linspace(min_decay, max_decay, d_hy)).astype(np.float32).reshape(1, d_hy)


def _filt_mlp_kernel(zt_ref, w1t_ref, b1_ref, w2t_ref, b2_ref, w3t_ref, b3_ref, fr_ref, o_ref):
    fr = fr_ref[...]
    h = jnp.sin(fr * (_dot_hi(w1t_ref[...], zt_ref[...]) + b1_ref[...]))
    h = jnp.sin(fr * (_dot_hi(w2t_ref[...], h) + b2_ref[...]))
    o_ref[...] = jnp.sin(fr * (_dot_hi(w3t_ref[...], h) + b3_ref[...]))


def _filt_mlp(zt, w1, b1, w2, b2, w3, b3, freq, tc=1024):
    pe, n = zt.shape
    fw = w1.shape[1]
    col = lambda a: a.reshape(-1, 1).astype(F32)
    full = lambda a: pl.BlockSpec(a.shape, lambda i: (0, 0))
    args = (zt, w1.T, col(b1), w2.T, col(b2), w3.T, col(b3), col(freq))
    return pl.pallas_call(
        _filt_mlp_kernel,
        grid=(n // tc,),
        in_specs=[pl.BlockSpec((pe, tc), lambda i: (0, i))] + [full(a) for a in args[1:]],
        out_specs=pl.BlockSpec((fw, tc), lambda i: (0, i)),
        out_shape=jax.ShapeDtypeStruct((fw, n), F32),
        compiler_params=_cparams(("parallel",), 24),
        name="filt_mlp",
    )(*args)


@functools.lru_cache(maxsize=None)
def _dft_tables(L):
    n = 2 * L
    N2 = FFT_N2
    N1 = n // N2
    N1h = N1 // 2
    j2 = np.arange(N2)[:, None, None]
    k1 = np.arange(N1)[None, :, None]

    def stage1(n_j1):
        j1 = np.arange(n_j1)[None, None, :]
        m = (k1 * (N2 * j1 + j2)) % n
        th = 2.0 * np.pi * m / n
        return np.cos(th), np.sin(th)

    c, s = stage1(N1h)
    t1 = np.concatenate([np.concatenate([c, s], axis=2), np.concatenate([-s, c], axis=2)], axis=1)
    c, s = stage1(N1)
    t1g = np.concatenate([c, -s], axis=1)
    c, s = stage1(N1h)
    ct, st = np.swapaxes(c, 1, 2) / n, np.swapaxes(s, 1, 2) / n
    t2 = np.concatenate([np.concatenate([ct, -st], axis=2), np.concatenate([st, ct], axis=2)], axis=1)
    a = np.arange(N2)
    th = 2.0 * np.pi * ((a[:, None] * a[None, :]) % N2) / N2
    c2, s2 = np.cos(th), np.sin(th)
    f2f = np.block([[c2, s2], [-s2, c2]])
    f2i = np.block([[c2, -s2], [s2, c2]])
    as_bf16 = lambda x: jnp.asarray(x, dtype=F32).astype(BF16)
    return dict(N1=N1, N2=N2, t1=t1.astype(np.float32), t1g=t1g.astype(np.float32), t2=t2.astype(np.float32),
                f2f=f2f.astype(np.float32), f2i=f2i.astype(np.float32))


FILT_ROWS = 512
FFT_UNROLL = 8
PITCH_PAD = 8


def _filt_fft_kernel(h3_ref, w4f_ref, w4b_ref, delta_ref, t1g_ref, f2f_ref, hspec_ref, hb0_ref, g_scr, a_scr,
                     *, L, N1, N2):
    n = 2 * L
    gp = N2 + PITCH_PAD
    ap = 2 * N1 + PITCH_PAD
    delta = delta_ref[...]
    hb0_ref[...] = jnp.zeros_like(hb0_ref)

    def gen(c, carry):
        r0 = pl.multiple_of(c * FILT_ROWS, FILT_ROWS)
        row = r0 + lax.broadcasted_iota(jnp.int32, (FILT_ROWS, LANES), 0)
        lag = jnp.where(row < L, row, jnp.where(row == L, 0, n - row))
        window = jnp.exp(-(lag.astype(F32) * (1.0 / (L - 1))) * delta)
        w4 = jnp.where(r0 < L, w4f_ref[...], w4b_ref[...])
        g = _dot(h3_ref[pl.ds(r0, FILT_ROWS), :], w4) * window
        at_l = row == L
        hb0_ref[...] += jnp.sum(jnp.where(at_l, g, 0.0), axis=0, keepdims=True)
        g = jnp.where(at_l, 0.0, g)
        for q in range(FILT_ROWS // N2):
            dst = pl.multiple_of((c * (FILT_ROWS // N2) + q) * gp, SUBLANES)
            g_scr[pl.ds(dst, N2), :] = g[q * N2:(q + 1) * N2]
        return carry

    lax.fori_loop(0, n // FILT_ROWS, gen, 0)

    def stage1(j2, carry):
        x = g_scr[pl.ds(j2, N1, stride=gp), :]
        a_scr[pl.ds(pl.multiple_of(j2 * ap, SUBLANES), 2 * N1), :] = _dot(t1g_ref[j2], x)
        return carry

    lax.fori_loop(0, N2, stage1, 0, unroll=2 * FFT_UNROLL)

    def stage2(k1, carry):
        ar = a_scr[pl.ds(k1, N2, stride=ap), :]
        ai = a_scr[pl.ds(N1 + k1, N2, stride=ap), :]
        z = _dot(f2f_ref[...], jnp.concatenate([ar, ai], axis=0))
        hspec_ref[pl.ds(pl.multiple_of(k1 * 2 * N2, 2 * N2), 2 * N2), :] = z.astype(hspec_ref.dtype)
        return carry

    lax.fori_loop(0, N1, stage2, 0, unroll=FFT_UNROLL)


def _filt_fft(h3, w4, L, d_hy):
    tb = _dft_tables(L)
    N1, N2 = tb["N1"], tb["N2"]
    n = 2 * L
    fw = h3.shape[1]
    nt = d_hy // LANES
    t1g = jnp.asarray(tb["t1g"]).astype(BF16)
    f2f = jnp.asarray(tb["f2f"]).astype(BF16)
    return pl.pallas_call(
        functools.partial(_filt_fft_kernel, L=L, N1=N1, N2=N2),
        grid=(nt,),
        in_specs=[pl.BlockSpec((n, fw), lambda c: (0, 0)),
                  pl.BlockSpec((fw, LANES), lambda c: (0, c)),
                  pl.BlockSpec((fw, LANES), lambda c: (0, c + nt)),
                  pl.BlockSpec((1, LANES), lambda c: (0, c)),
                  pl.BlockSpec(t1g.shape, lambda c: (0, 0, 0)),
                  pl.BlockSpec(f2f.shape, lambda c: (0, 0))],
        out_specs=[pl.BlockSpec((2 * n, LANES), lambda c: (0, c)),
                   pl.BlockSpec((SUBLANES, LANES), lambda c: (0, c))],
        out_shape=[jax.ShapeDtypeStruct((2 * n, d_hy), BF16), jax.ShapeDtypeStruct((SUBLANES, d_hy), F32)],
        scratch_shapes=[pltpu.VMEM((N1 * (N2 + PITCH_PAD), LANES), F32),
                        pltpu.VMEM((N2 * (2 * N1 + PITCH_PAD), LANES), F32)],
        compiler_params=_cparams(("parallel",), 48),
        name="filt_fft",
    )(h3, w4, w4, jnp.asarray(_decay_rates(d_hy)), t1g, f2f)


def _hy_conv_kernel(vg_ref, hspec_ref, skip_ref, hb0_ref, t1_ref, f2f_ref, f2i_ref, t2_ref, y_ref,
                    x_scr, a_scr, b_scr, *, N1, N2):
    N1h = N1 // 2
    xp = N2 + PITCH_PAD
    ap = 2 * N1 + PITCH_PAD
    bp = 2 * N2 + PITCH_PAD

    for b in range(2):
        for j1 in range(N1h):
            x_scr[b, pl.ds(j1 * xp, N2), :] = vg_ref[b, pl.ds(j1 * N2, N2), :].astype(F32)

    def stage1(j2, carry):
        x = jnp.concatenate([x_scr[0, pl.ds(j2, N1h, stride=xp), :],
                             x_scr[1, pl.ds(j2, N1h, stride=xp), :]], axis=0)
        a_scr[pl.ds(pl.multiple_of(j2 * ap, SUBLANES), 2 * N1), :] = _dot(t1_ref[j2], x)
        return carry

    lax.fori_loop(0, N2, stage1, 0, unroll=2 * FFT_UNROLL)

    def stage2(k1, carry):
        ar = a_scr[pl.ds(k1, N2, stride=ap), :]
        ai = a_scr[pl.ds(N1 + k1, N2, stride=ap), :]
        z = _dot(f2f_ref[...], jnp.concatenate([ar, ai], axis=0))
        zr, zi = z[:N2], z[N2:]
        base = pl.multiple_of(k1 * 2 * N2, 2 * N2)
        hr = hspec_ref[pl.ds(base, N2), :].astype(F32)
        hi = hspec_ref[pl.ds(base + N2, N2), :].astype(F32)
        prod = jnp.concatenate([zr * hr - zi * hi, zr * hi + zi * hr], axis=0)
        b_scr[pl.ds(pl.multiple_of(k1 * bp, SUBLANES), 2 * N2), :] = _dot(f2i_ref[...], prod)
        return carry

    lax.fori_loop(0, N1, stage2, 0, unroll=FFT_UNROLL)

    skip = skip_ref[...] + hb0_ref[0:1, :]

    def stage3(j2, carry):
        b = jnp.concatenate([b_scr[pl.ds(j2, N1, stride=bp), :],
                             b_scr[pl.ds(N2 + j2, N1, stride=bp), :]], axis=0)
        y = _dot(t2_ref[j2], b)
        x_scr[0, pl.ds(j2, N1h, stride=xp), :] = y[:N1h] + x_scr[0, pl.ds(j2, N1h, stride=xp), :] * skip
        x_scr[1, pl.ds(j2, N1h, stride=xp), :] = y[N1h:] + x_scr[1, pl.ds(j2, N1h, stride=xp), :] * skip
        return carry

    lax.fori_loop(0, N2, stage3, 0, unroll=2 * FFT_UNROLL)

    for b in range(2):
        for j1 in range(N1h):
            y_ref[b, pl.ds(j1 * N2, N2), :] = x_scr[b, pl.ds(j1 * xp, N2), :]


def _hy_conv(vg, hspec, skip, hb0):
    B, L, d_hy = vg.shape
    assert B % 2 == 0
    tb = _dft_tables(L)
    N1, N2 = tb["N1"], tb["N2"]
    n = 2 * L
    nt = d_hy // LANES
    t1, t2 = (jnp.asarray(tb[k]).astype(BF16) for k in ("t1", "t2"))
    f2f, f2i = (jnp.asarray(tb[k]).astype(BF16) for k in ("f2f", "f2i"))
    const3 = lambda a: pl.BlockSpec(a.shape, lambda b, c: (0, 0, 0))
    const2 = lambda a: pl.BlockSpec(a.shape, lambda b, c: (0, 0))
    return pl.pallas_call(
        functools.partial(_hy_conv_kernel, N1=N1, N2=N2),
        grid=(B // 2, nt),
        in_specs=[pl.BlockSpec((2, L, LANES), lambda b, c: (b, 0, c)),
                  pl.BlockSpec((2 * n, LANES), lambda b, c: (0, c)),
                  pl.BlockSpec((1, LANES), lambda b, c: (0, c)),
                  pl.BlockSpec((SUBLANES, LANES), lambda b, c: (0, c)),
                  const3(t1), const2(f2f), const2(f2i), const3(t2)],
        out_specs=pl.BlockSpec((2, L, LANES), lambda b, c: (b, 0, c)),
        out_shape=jax.ShapeDtypeStruct((B, L, d_hy), F32),
        scratch_shapes=[pltpu.VMEM((2, (N1 // 2) * (N2 + PITCH_PAD), LANES), F32),
                        pltpu.VMEM((N2 * (2 * N1 + PITCH_PAD), LANES), F32),
                        pltpu.VMEM((N1 * (2 * N2 + PITCH_PAD), LANES), F32)],
        compiler_params=_cparams(("parallel", "parallel"), 58),
        name="hy_conv",
    )(vg, hspec, skip.reshape(1, d_hy).astype(F32), hb0, t1, f2f, f2i, t2)


def _hyena_branch(p3, conv_w, conv_b, fw1, fb1, fw2, fb2, fw3, fb3, fw4, freq, skip, d_hy):
    B, L, _ = p3.shape
    x0c, vg = _hy_pre(p3, conv_w, conv_b, d_hy)
    zt = jnp.asarray(_filter_positions(L, fw1.shape[0]).T)
    h3 = _filt_mlp(zt, fw1, fb1, fw2, fb2, fw3, fb3, freq).T
    hspec, hb0 = _filt_fft(h3, fw4, L, d_hy)
    return x0c, _hy_conv(vg, hspec, skip, hb0)


ROUTE_GATE, ROUTE_EXPERT, ROUTE_RANK = 0, 2, 4
MIX_ROWS = 256


def _mix_route_kernel(yc_ref, x0_ref, of_ref, ob_ref, z_ref, hnw_ref, gnw_ref, x_ref, wo_ref, n2w_ref, wr_ref, br_ref,
                      x2_ref, u_ref, route_ref, route_t_ref, cnt_ref, run_scr, *, n_heads, head_dim, n_groups,
                      per_group):
    tm = x_ref.shape[0]
    G, P = n_groups, per_group

    @pl.when(pl.program_id(0) == 0)
    def _():
        run_scr[...] = jnp.zeros_like(run_scr)

    run = run_scr[0:1, :]
    blocks = [pl.ds(s * MIX_ROWS, MIX_ROWS) for s in range(tm // MIX_ROWS)]
    us = []
    for blk, rs in enumerate(blocks):
        yh = yc_ref[rs, :] * x0_ref[rs, :].astype(F32)
        yh = yh * lax.rsqrt(jnp.mean(yh * yh, axis=-1, keepdims=True) + EPS) * hnw_ref[...]
        parts = [yh.astype(BF16)]
        for h in range(n_heads):
            sl = slice(h * head_dim, (h + 1) * head_dim)
            o = of_ref[rs, sl] + ob_ref[rs, sl]
            z = z_ref[rs, sl].astype(F32)
            o = o * lax.rsqrt(jnp.mean(o * o, axis=-1, keepdims=True) + EPS) * gnw_ref[...] * _silu(z)
            parts.append(o.astype(BF16))
        ymix = jnp.concatenate(parts, axis=-1)
        x2 = x_ref[rs, :] + jnp.dot(ymix, wo_ref[...], preferred_element_type=F32)
        x2_ref[rs, :] = x2
        u = x2 * lax.rsqrt(jnp.mean(x2 * x2, axis=-1, keepdims=True) + EPS) * n2w_ref[...]
        _store_token_tiles(u_ref, blk * MIX_ROWS, _pack_bf16_pairs(u))
        us.append(u)

    all_logits = [_dot(u, wr_ref[...]) + br_ref[...] for u in us]
    for s, (rs, logits) in enumerate(zip(blocks, all_logits)):
        lane = lax.broadcasted_iota(jnp.int32, logits.shape, 1)
        neg = jnp.float32(-jnp.inf)
        big = jnp.int32(4 * LANES)
        first = lambda hit: jnp.min(jnp.where(hit, lane, big), axis=-1, keepdims=True)
        gl = jnp.where(lane < G, logits, neg)
        gmax = jnp.max(gl, axis=-1, keepdims=True)
        gidx = first(gl == gmax)
        grp_gate = 1.0 / jnp.sum(jnp.exp(gl - gmax), axis=-1, keepdims=True)
        in_grp = (lane >= G) & (lane < G + G * P) & (((lane - G) // P) == gidx)
        ll = jnp.where(in_grp, logits, neg)
        m1 = jnp.max(ll, axis=-1, keepdims=True)
        i1 = first(ll == m1)
        denom = jnp.sum(jnp.exp(ll - m1), axis=-1, keepdims=True)
        ll2 = jnp.where(lane == i1, neg, ll)
        m2 = jnp.max(ll2, axis=-1, keepdims=True)
        i2 = first(ll2 == m2)
        p1 = 1.0 / denom
        p2 = jnp.exp(m2 - m1) / denom
        gate1 = grp_gate * (p1 / (p1 + p2))
        gate2 = grp_gate * (p2 / (p1 + p2))
        e1 = i1 - G
        e2 = i2 - G

        oh1 = jnp.where(lane == e1, 1.0, 0.0)
        oh2 = jnp.where(lane == e2, 1.0, 0.0)
        oh = oh1 + oh2
        ii = lax.broadcasted_iota(jnp.int32, (MIX_ROWS, MIX_ROWS), 0)
        jj = lax.broadcasted_iota(jnp.int32, (MIX_ROWS, MIX_ROWS), 1)
        before = _dot(jnp.where(ii > jj, 1.0, 0.0), oh) + run
        r1 = jnp.sum(oh1 * before, axis=-1, keepdims=True)
        r2 = jnp.sum(oh2 * before, axis=-1, keepdims=True)
        run = run + jnp.sum(oh, axis=0, keepdims=True)

        rec = jnp.where(lane == ROUTE_GATE, gate1, 0.0)
        rec = jnp.where(lane == ROUTE_GATE + 1, gate2, rec)
        rec = jnp.where(lane == ROUTE_EXPERT, e1.astype(F32), rec)
        rec = jnp.where(lane == ROUTE_EXPERT + 1, e2.astype(F32), rec)
        rec = jnp.where(lane == ROUTE_RANK, r1, rec)
        rec = jnp.where(lane == ROUTE_RANK + 1, r2, rec)
        route_ref[rs, :] = rec
        route_t_ref[:, s * MIX_ROWS:(s + 1) * MIX_ROWS] = jnp.transpose(rec)[:SUBLANES, :]

    run_scr[...] = jnp.broadcast_to(run, run_scr.shape)
    cnt_ref[...] = run_scr[...]


def _mix_route(yconv, x0c, o_f, o_b, p, z_col, hy_norm_w, gdn_norm_w, xf, w_out_bf16, norm2_w, wr, br,
               n_heads, head_dim, n_groups, per_group, tm=512):
    M, D = xf.shape
    d_hy = yconv.shape[1]
    d_gdn = o_f.shape[1]
    assert z_col % d_gdn == 0 and n_groups * (per_group + 1) <= LANES
    zb = z_col // d_gdn
    row = lambda i: (i, 0)
    const = lambda i: (0, 0)
    kern = functools.partial(_mix_route_kernel, n_heads=n_heads, head_dim=head_dim,
                             n_groups=n_groups, per_group=per_group)
    return pl.pallas_call(
        kern,
        grid=(M // tm,),
        in_specs=[pl.BlockSpec((tm, d_hy), row), pl.BlockSpec((tm, d_hy), row),
                  pl.BlockSpec((tm, d_gdn), row), pl.BlockSpec((tm, d_gdn), row),
                  pl.BlockSpec((tm, d_gdn), lambda i: (i, zb)),
                  pl.BlockSpec((1, d_hy), const), pl.BlockSpec((1, head_dim), const),
                  pl.BlockSpec((tm, D), row), pl.BlockSpec(w_out_bf16.shape, const, pipeline_mode=pl.Buffered(1)),
                  pl.BlockSpec((1, D), const), pl.BlockSpec((D, LANES), const), pl.BlockSpec((1, LANES), const)],
        out_specs=[pl.BlockSpec((tm, D), row), pl.BlockSpec((tm * (D // 2 // LANES), LANES), row),
                   pl.BlockSpec((tm, LANES), row), pl.BlockSpec((SUBLANES, tm), lambda i: (0, i)),
                   pl.BlockSpec((SUBLANES, LANES), const)],
        out_shape=[jax.ShapeDtypeStruct((M, D), F32), jax.ShapeDtypeStruct((M * (D // 2 // LANES), LANES), jnp.uint32),
                   jax.ShapeDtypeStruct((M, LANES), F32), jax.ShapeDtypeStruct((SUBLANES, M), F32),
                   jax.ShapeDtypeStruct((SUBLANES, LANES), F32)],
        scratch_shapes=[pltpu.VMEM((SUBLANES, LANES), F32)],
        compiler_params=_cparams(("arbitrary",), 58),
        name="mix_route",
    )(yconv, x0c, o_f, o_b, p, hy_norm_w.reshape(1, d_hy), gdn_norm_w.reshape(1, head_dim), xf, w_out_bf16,
      norm2_w.reshape(1, D), wr, br)


WAIT_GROUP = 8


def _experts_kernel(ts_ref, se_ref, nn_ref, nv_ref, tgt_nxt_ref, tgt_ref, tgt_prv_ref, u_hbm, w1_hbm, w3_hbm, w2_hbm,
                    out_hbm, x0, x1, y0, y1, wf1, wf3, wf2, w1b, w3b, w2b, gsem, ssem, wsem, *, n_tokens):
    i = pl.program_id(0)
    na = nn_ref[0]
    ns = nn_ref[1]
    CH = w1b.shape[0] // 2 // LANES
    TB = x0.shape[0] // CH
    W_SLOTS = wf1.shape[0]
    last_tile = pl.num_programs(0) - 1
    nv_cur = nv_ref[i]
    nv_nxt = jnp.where(i + 1 < na, nv_ref[jnp.minimum(i + 1, last_tile)], 0)
    nv_prv = jnp.where(i > 0, nv_ref[jnp.maximum(i - 1, 0)], 0)
    nv_pp = jnp.where(i > 1, nv_ref[jnp.maximum(i - 2, 0)], 0)

    def weight_copies(s):
        slot = lax.rem(s, W_SLOTS)
        e = se_ref[s]
        return [pltpu.make_async_copy(w_hbm.at[e], wf.at[slot], wsem.at[slot])
                for w_hbm, wf in ((w1_hbm, wf1), (w3_hbm, wf3), (w2_hbm, wf2))]

    def token_of(v):
        if n_tokens & (n_tokens - 1) == 0:
            return v & (n_tokens - 1)
        return lax.rem(v, n_tokens)

    def tile_of(row):
        return pl.ds(pl.multiple_of(row * CH, CH), CH)

    def gather(tgt, xbuf, s, r):
        return pltpu.make_async_copy(u_hbm.at[tile_of(token_of(tgt[r]))], xbuf.at[tile_of(r)], gsem.at[s])

    def scatter(ybuf, s, r, dst):
        return pltpu.make_async_copy(ybuf.at[tile_of(r)], out_hbm.at[tile_of(dst)], ssem.at[s])

    def wait_tokens(count, make):
        groups = count // WAIT_GROUP

        def grp(g, carry):
            make(WAIT_GROUP).wait()
            return carry

        lax.fori_loop(0, groups, grp, 0)

        def one(g, carry):
            make(1).wait()
            return carry

        lax.fori_loop(groups * WAIT_GROUP, count, one, 0)

    def wait_gather(xbuf, s, count):
        wait_tokens(count, lambda n: pltpu.make_async_copy(u_hbm.at[pl.ds(0, n * CH)], xbuf.at[pl.ds(0, n * CH)],
                                                             gsem.at[s]))

    def wait_scatter(ybuf, s, count):
        wait_tokens(count, lambda n: pltpu.make_async_copy(ybuf.at[pl.ds(0, n * CH)], out_hbm.at[pl.ds(0, n * CH)],
                                                             ssem.at[s]))

    def each_row(count, fn):
        def body(r, carry):
            fn(r)
            return carry
        lax.fori_loop(0, count, body, 0)

    def step(p):
        q = 1 - p
        x_cur, x_nxt = (x0, x1) if p == 0 else (x1, x0)
        y_cur, y_prv = (y0, y1) if p == 0 else (y1, y0)
        wait_gather(x_cur, p, nv_cur)
        wait_scatter(y_cur, p, nv_pp)

        x = _unpack_bf16_pairs(_load_token_tiles(x_cur, 0, TB, CH)).astype(BF16)
        for r in range(TB):
            @pl.when(r < nv_nxt)
            def _():
                gather(tgt_nxt_ref, x_nxt, q, r).start()
        for r in range(TB):
            @pl.when(r < nv_prv)
            def _():
                scatter(y_prv, q, r, tgt_prv_ref[r]).start()
        h = _silu(jnp.dot(x, w1b[...], preferred_element_type=F32)) * jnp.dot(x, w3b[...], preferred_element_type=F32)
        _store_token_tiles(y_cur, 0, _pack_bf16_pairs(jnp.dot(h.astype(BF16), w2b[...], preferred_element_type=F32)))

        @pl.when(i == na - 1)
        def _():
            wait_scatter(y_prv, q, nv_prv)
            each_row(nv_cur, lambda r: scatter(y_cur, p, r, tgt_ref[r]).start())
            wait_scatter(y_cur, p, nv_cur)

    @pl.when(i < na)
    def _():
        @pl.when(i == 0)
        def _():
            x0[...] = jnp.zeros_like(x0)
            x1[...] = jnp.zeros_like(x1)
            each_row(nv_cur, lambda r: gather(tgt_ref, x0, 0, r).start())
            for s0 in range(W_SLOTS - 1):
                @pl.when(s0 < ns)
                def _():
                    for c in weight_copies(s0):
                        c.start(priority=WEIGHT_DMA_PRIORITY)

        s = ts_ref[i]

        @pl.when((i == 0) | (s != ts_ref[jnp.maximum(i - 1, 0)]))
        def _():
            @pl.when(s + W_SLOTS - 1 < ns)
            def _():
                for c in weight_copies(s + W_SLOTS - 1):
                    c.start(priority=WEIGHT_DMA_PRIORITY)

            for c in weight_copies(s):
                c.wait()
            slot = lax.rem(s, W_SLOTS)
            w1b[...] = wf1[slot].astype(BF16)
            w3b[...] = wf3[slot].astype(BF16)
            w2b[...] = wf2[slot].astype(BF16)

        parity = lax.rem(i, 2)

        @pl.when(parity == 0)
        def _():
            step(0)

        @pl.when(parity == 1)
        def _():
            step(1)


WEIGHT_SLOTS = 3
WEIGHT_DMA_PRIORITY = 1


def _experts(u, slot_tgt, tile_seq, seq_expert, n_active_seq, tile_valid, w1, w3, w2):
    E, D, de = w1.shape
    CH = D // 2 // LANES
    assert CH % SUBLANES == 0
    T = u.shape[0] // CH
    TB = EXPERT_ROWS
    n_tiles = slot_tgt.shape[0] // TB
    table = lambda f: pl.BlockSpec((TB,), lambda i, ts, se, nn, nv: (f(i),), memory_space=pltpu.SMEM)
    hbm = pl.BlockSpec(memory_space=pl.ANY)
    grid_spec = pltpu.PrefetchScalarGridSpec(
        num_scalar_prefetch=4,
        grid=(n_tiles,),
        in_specs=[table(lambda i: jnp.minimum(i + 1, n_tiles - 1)), table(lambda i: i),
                  table(lambda i: jnp.maximum(i - 1, 0)), hbm, hbm, hbm, hbm],
        out_specs=hbm,
        scratch_shapes=[pltpu.VMEM((TB * CH, LANES), jnp.uint32)] * 4
                       + [pltpu.VMEM((WEIGHT_SLOTS, D, de), F32), pltpu.VMEM((WEIGHT_SLOTS, D, de), F32),
                          pltpu.VMEM((WEIGHT_SLOTS, de, D), F32),
                          pltpu.VMEM((D, de), BF16), pltpu.VMEM((D, de), BF16), pltpu.VMEM((de, D), BF16),
                          pltpu.SemaphoreType.DMA((2,)), pltpu.SemaphoreType.DMA((2,)),
                          pltpu.SemaphoreType.DMA((WEIGHT_SLOTS,))],
    )
    return pl.pallas_call(
        functools.partial(_experts_kernel, n_tokens=T),
        grid_spec=grid_spec,
        out_shape=jax.ShapeDtypeStruct((2 * T * CH, LANES), jnp.uint32),
        compiler_params=_cparams(("arbitrary",), 58),
        name="experts",
    )(tile_seq, seq_expert, n_active_seq, tile_valid, slot_tgt, slot_tgt, slot_tgt, u, w1, w3, w2)


def _slot_table_kernel(dest_ref, init_hbm, o_ref, sem):
    fill = pltpu.make_async_copy(init_hbm, o_ref, sem)
    fill.start()
    fill.wait()

    def put(a, carry):
        o_ref[dest_ref[a]] = a
        return carry

    lax.fori_loop(0, dest_ref.shape[0], put, 0, unroll=8)


def _slot_table(dest, init):
    smem = pl.BlockSpec(memory_space=pltpu.SMEM)
    return pl.pallas_call(
        _slot_table_kernel,
        in_specs=[smem, pl.BlockSpec(memory_space=pl.ANY)],
        out_specs=smem,
        out_shape=jax.ShapeDtypeStruct(init.shape, jnp.int32),
        scratch_shapes=[pltpu.SemaphoreType.DMA],
        name="slot_table",
    )(dest, init)


def _dispatch_tables(route_t, counts, n_experts):
    T = route_t.shape[1]
    TB = EXPERT_ROWS
    e = route_t[ROUTE_EXPERT:ROUTE_EXPERT + 2].astype(jnp.int32)
    rank = route_t[ROUTE_RANK:ROUTE_RANK + 2].astype(jnp.int32)
    cnt = counts[0, :n_experts].astype(jnp.int32)
    padded = (cnt + TB - 1) // TB * TB
    pad_end = jnp.cumsum(padded)
    pad_start = pad_end - padded
    ids = jnp.arange(n_experts, dtype=jnp.int32)[:, None, None]
    start_of = jnp.sum(jnp.where(e[None] == ids, pad_start[:, None, None], 0), axis=0)
    dest = start_of + rank
    n_tiles = -(-(2 * T + n_experts * (TB - 1)) // TB)
    slot_tgt = _slot_table(dest.reshape(-1), jnp.zeros((n_tiles * TB,), jnp.int32))
    start = jnp.arange(n_tiles, dtype=jnp.int32) * TB
    tile_expert = jnp.sum(start[:, None] >= pad_end[None, :], axis=1)
    tile_expert = jnp.minimum(tile_expert, n_experts - 1).astype(jnp.int32)
    owns = cnt > 0
    pos = jnp.cumsum(owns.astype(jnp.int32)) - 1
    experts = jnp.arange(n_experts, dtype=jnp.int32)
    seq_expert = jnp.sum(jnp.where(owns[None, :] & (pos[None, :] == experts[:, None]), experts[None, :], 0), axis=1)
    tile_seq = jnp.sum(jnp.where(tile_expert[:, None] == experts[None, :], pos[None, :], 0), axis=1)
    n_active_seq = jnp.stack([pad_end[-1] // TB, jnp.sum(owns.astype(jnp.int32))]).astype(jnp.int32)
    pick = tile_expert[:, None] == experts[None, :]
    in_expert = start - jnp.sum(jnp.where(pick, pad_start[None, :], 0), axis=1)
    tile_cnt = jnp.sum(jnp.where(pick, cnt[None, :], 0), axis=1)
    tile_valid = jnp.where(start < pad_end[-1], jnp.clip(tile_cnt - in_expert, 0, TB), 0).astype(jnp.int32)
    return slot_tgt, tile_seq.astype(jnp.int32), seq_expert.astype(jnp.int32), n_active_seq, tile_valid


def _combine_kernel(x2_ref, e0_ref, e1_ref, route_ref, w_ref, o_ref, *, final_norm):
    r = route_ref[...]
    tm = x2_ref.shape[0]
    ch = e0_ref.shape[0] // tm
    e0 = _unpack_bf16_pairs(_load_token_tiles(e0_ref, 0, tm, ch))
    e1 = _unpack_bf16_pairs(_load_token_tiles(e1_ref, 0, tm, ch))
    y = x2_ref[...] + r[:, ROUTE_GATE:ROUTE_GATE + 1] * e0 + r[:, ROUTE_GATE + 1:ROUTE_GATE + 2] * e1
    if final_norm:
        y = y * lax.rsqrt(jnp.mean(y * y, axis=-1, keepdims=True) + EPS) * w_ref[...]
    o_ref[...] = y


def _combine(x2, planes, route, norm_w, final_norm, tm=512):
    M, D = x2.shape
    return pl.pallas_call(
        functools.partial(_combine_kernel, final_norm=final_norm),
        grid=(M // tm,),
        in_specs=[pl.BlockSpec((tm, D), lambda i: (i, 0)),
                  pl.BlockSpec((tm * (D // 2 // LANES), LANES), lambda i: (i, 0)),
                  pl.BlockSpec((tm * (D // 2 // LANES), LANES), lambda i: (M // tm + i, 0)),
                  pl.BlockSpec((tm, LANES), lambda i: (i, 0)),
                  pl.BlockSpec((1, D), lambda i: (0, 0))],
        out_specs=pl.BlockSpec((tm, D), lambda i: (i, 0)),
        out_shape=jax.ShapeDtypeStruct((M, D), F32),
        compiler_params=_cparams(("parallel",), 48),
        name="combine",
    )(x2, planes, planes, route, norm_w.reshape(1, D))


def kernel(x, norm1_w, w_in, hy_conv_w, hy_conv_b, hy_filt_w1, hy_filt_b1, hy_filt_w2, hy_filt_b2, hy_filt_w3, hy_filt_b3, hy_filt_w4, hy_sin_freq, hy_skip, hy_norm_w, gdn_conv_w, gdn_a_log_f, gdn_a_log_b, gdn_dt_bias_f, gdn_dt_bias_b, gdn_norm_w, w_out, norm2_w, router_group_w, router_group_b, router_expert_w, router_expert_b, exp_w1, exp_w3, exp_w2, final_norm_w):
    B, L, D = x.shape
    M = B * L
    depth = w_in.shape[0]
    d_hy = hy_skip.shape[-1]
    H = gdn_a_log_f.shape[-1]
    Dh = gdn_norm_w.shape[-1]
    d_gdn = H * Dh
    n_main = 3 * d_hy + 4 * d_gdn
    G = router_group_w.shape[-1]
    E = router_expert_w.shape[-1]
    xf = x.reshape(M, D)
    for l in range(depth):
        p, pg = _inproj(xf, norm1_w[l], jnp.swapaxes(w_in[l], 0, 1), n_main, 4 * H)
        p3 = p.reshape(B, L, n_main)
        x0c, yconv = _hyena_branch(p3, hy_conv_w[l], hy_conv_b[l], hy_filt_w1[l], hy_filt_b1[l], hy_filt_w2[l],
                                   hy_filt_b2[l], hy_filt_w3[l], hy_filt_b3[l], hy_filt_w4[l], hy_sin_freq[l],
                                   hy_skip[l], d_hy)
        o_f, o_b = _gdn_branch(p3, pg, gdn_conv_w[l], gdn_a_log_f[l], gdn_a_log_b[l], gdn_dt_bias_f[l],
                               gdn_dt_bias_b[l], 3 * d_hy, H, Dh)
        wr = jnp.pad(jnp.concatenate([router_group_w[l], router_expert_w[l]], axis=1), ((0, 0), (0, LANES - G - E)))
        br = jnp.pad(jnp.concatenate([router_group_b[l], router_expert_b[l]]), (0, LANES - G - E)).reshape(1, LANES)
        x2, u, route, route_t, counts = _mix_route(
            yconv.reshape(M, d_hy), x0c.reshape(M, d_hy), o_f.reshape(M, d_gdn), o_b.reshape(M, d_gdn), p,
            3 * d_hy + 3 * d_gdn, hy_norm_w[l], gdn_norm_w[l], xf, w_out[l].astype(BF16), norm2_w[l], wr, br,
            H, Dh, G, E // G)
        slot_tgt, tile_seq, seq_expert, n_active_seq, tile_valid = _dispatch_tables(route_t, counts, E)
        planes = _experts(u, slot_tgt, tile_seq, seq_expert, n_active_seq, tile_valid, exp_w1[l], exp_w3[l],
                          exp_w2[l])
        xf = _combine(x2, planes, route, final_norm_w, final_norm=(l == depth - 1))
    return xf.reshape(B, L, D)
```

```python
import functools
import math

import jax
import jax.numpy as jnp
import numpy as np
from jax import lax
from jax.experimental import pallas as pl
from jax.experimental.pallas import tpu as pltpu

F32 = jnp.float32
BF16 = jnp.bfloat16
EPS = 1e-6
LANES = 128
SUBLANES = 8
VMEM_BYTES_V7X = 64 * 1024 * 1024
GDN_CHUNK = 64
FFT_N2 = 128
EXPERT_ROWS = 256
DECAY_TARGET = 1e-2
FAST_DECAY_PCT = 0.3
SLOW_DECAY_PCT = 1.5


def _cparams(sem, vmem_mb):
    return pltpu.CompilerParams(dimension_semantics=sem, vmem_limit_bytes=int(vmem_mb * 1024 * 1024))


def _dot(a, b):
    return jnp.dot(a.astype(BF16), b.astype(BF16), preferred_element_type=F32)


def _dot_nt(a, b):
    return lax.dot_general(a.astype(BF16), b.astype(BF16), (((1,), (1,)), ((), ())), preferred_element_type=F32)


def _dot_tn(a, b):
    return lax.dot_general(a.astype(BF16), b.astype(BF16), (((0,), (0,)), ((), ())), preferred_element_type=F32)


def _dot_hi(a, b):
    return jnp.dot(a, b, preferred_element_type=F32, precision=lax.Precision.HIGHEST)


def _silu(x):
    return x * jax.nn.sigmoid(x)


def _pack_bf16_pairs(x):
    c = x.shape[1] // 2
    lo = lax.bitcast_convert_type(x[:, :c].astype(BF16).astype(F32), jnp.uint32) >> 16
    hi = lax.bitcast_convert_type(x[:, c:].astype(BF16).astype(F32), jnp.uint32) & jnp.uint32(0xFFFF0000)
    return hi | lo


def _unpack_bf16_pairs(w):
    lo = lax.bitcast_convert_type(w << 16, F32)
    hi = lax.bitcast_convert_type(w & jnp.uint32(0xFFFF0000), F32)
    return jnp.concatenate([lo, hi], axis=1)


def _store_token_tiles(ref, row0, words):
    n, width = words.shape
    ch = width // LANES
    for c in range(ch):
        ref[pl.ds(row0 * ch + c, n, stride=ch), :] = words[:, c * LANES:(c + 1) * LANES]


def _load_token_tiles(ref, row0, n, ch):
    return jnp.concatenate([ref[pl.ds(row0 * ch + c, n, stride=ch), :] for c in range(ch)], axis=1)


def _inproj_kernel(x_ref, nw_ref, wt_ref, wgt_ref, p_ref, g_ref, h_scr):
    @pl.when(pl.program_id(1) == 0)
    def _():
        x = x_ref[...]
        h = x * lax.rsqrt(jnp.mean(x * x, axis=-1, keepdims=True) + EPS) * nw_ref[...]
        h_scr[...] = h.astype(BF16)
        g = _dot_nt(h, wgt_ref[...])
        g_ref[...] = jnp.concatenate([g, jnp.zeros((g.shape[0], LANES - g.shape[1]), F32)], axis=1)

    p_ref[...] = _dot_nt(h_scr[...], wt_ref[...]).astype(p_ref.dtype)


def _inproj(xf, norm_w, wt, n_main, n_gate, tm=1024, tn=512):
    M, D = xf.shape
    assert n_main % tn == 0 and M % tm == 0 and n_main % n_gate == 0 and n_gate % SUBLANES == 0
    return pl.pallas_call(
        _inproj_kernel,
        grid=(M // tm, n_main // tn),
        in_specs=[
            pl.BlockSpec((tm, D), lambda i, j: (i, 0)),
            pl.BlockSpec((1, D), lambda i, j: (0, 0)),
            pl.BlockSpec((tn, D), lambda i, j: (j, 0)),
            pl.BlockSpec((n_gate, D), lambda i, j: (n_main // n_gate, 0)),
        ],
        out_specs=[
            pl.BlockSpec((tm, tn), lambda i, j: (i, j)),
            pl.BlockSpec((tm, LANES), lambda i, j: (i, 0)),
        ],
        out_shape=[jax.ShapeDtypeStruct((M, n_main), BF16), jax.ShapeDtypeStruct((M, LANES), F32)],
        scratch_shapes=[pltpu.VMEM((tm, D), BF16)],
        compiler_params=_cparams(("parallel", "arbitrary"), 57),
        name="inproj",
    )(xf, norm_w.reshape(1, D), wt, wt)


def _conv3_rows(ref, r0, rows, w, n_rows):
    cur = ref[pl.ds(r0, rows), :].astype(F32)
    lo = jnp.maximum(r0 - HALO_ROWS, 0)
    hi = jnp.minimum(r0 + rows, n_rows - HALO_ROWS)
    prev_grp = ref[pl.ds(pl.multiple_of(lo, HALO_ROWS), HALO_ROWS), :].astype(F32)
    next_grp = ref[pl.ds(pl.multiple_of(hi, HALO_ROWS), HALO_ROWS), :].astype(F32)
    prev_row = jnp.where(r0 > 0, prev_grp[HALO_ROWS - 1:HALO_ROWS, :], 0.0)
    next_row = jnp.where(r0 + rows < n_rows, next_grp[0:1, :], 0.0)
    row = lax.broadcasted_iota(jnp.int32, cur.shape, 0)
    xm = jnp.where(row == 0, prev_row, pltpu.roll(cur, 1, 0))
    xp = jnp.where(row == rows - 1, next_row, pltpu.roll(cur, rows - 1, 0))
    return xm * w[0:1, :] + cur * w[1:2, :] + xp * w[2:3, :]


CONV_ROWS = 256
HALO_ROWS = 16


def _hy_pre_kernel(x0_ref, x1_ref, v_ref, w0_ref, w1_ref, w2_ref, b0_ref, b1_ref, b2_ref, x0c_ref, vg_ref):
    L = x0_ref.shape[0]
    w0, w1, w2 = w0_ref[...], w1_ref[...], w2_ref[...]
    b0, b1, b2 = b0_ref[...], b1_ref[...], b2_ref[...]

    def body(c, carry):
        r0 = pl.multiple_of(c * CONV_ROWS, CONV_ROWS)
        x0c_ref[pl.ds(r0, CONV_ROWS), :] = (_conv3_rows(x0_ref, r0, CONV_ROWS, w0, L) + b0).astype(x0c_ref.dtype)
        x1c = _conv3_rows(x1_ref, r0, CONV_ROWS, w1, L) + b1
        vc = _conv3_rows(v_ref, r0, CONV_ROWS, w2, L) + b2
        vg_ref[pl.ds(r0, CONV_ROWS), :] = (vc * x1c).astype(vg_ref.dtype)
        return carry

    lax.fori_loop(0, L // CONV_ROWS, body, 0, unroll=2)


def _hy_pre(p3, conv_w, conv_b, d_hy):
    B, L, _ = p3.shape
    nt = d_hy // LANES
    bias = conv_b.reshape(1, -1)
    pspec = lambda off: pl.BlockSpec((None, L, LANES), lambda b, c: (b, 0, c + off))
    wspec = lambda off: pl.BlockSpec((3, LANES), lambda b, c: (0, c + off))
    bspec = lambda off: pl.BlockSpec((1, LANES), lambda b, c: (0, c + off))
    ospec = pl.BlockSpec((None, L, LANES), lambda b, c: (b, 0, c))
    return pl.pallas_call(
        _hy_pre_kernel,
        grid=(B, nt),
        in_specs=[pspec(0), pspec(nt), pspec(2 * nt), wspec(0), wspec(nt), wspec(2 * nt),
                  bspec(0), bspec(nt), bspec(2 * nt)],
        out_specs=[ospec, ospec],
        out_shape=[jax.ShapeDtypeStruct((B, L, d_hy), BF16)] * 2,
        compiler_params=_cparams(("parallel", "parallel"), 40),
        name="hy_pre",
    )(p3, p3, p3, conv_w, conv_w, conv_w, bias, bias, bias)


def _gdn_pre_kernel(x_ref, w_ref, o_ref, *, n_heads, head_dim):
    L = x_ref.shape[0]
    w = w_ref[...]
    c = pl.program_id(1)
    q_scale = jnp.where(c < n_heads, head_dim ** -0.5, 1.0)
    is_qk = c < 2 * n_heads

    def body(i, carry):
        r0 = pl.multiple_of(i * CONV_ROWS, CONV_ROWS)
        y = _silu(_conv3_rows(x_ref, r0, CONV_ROWS, w, L))
        inv = lax.rsqrt(jnp.sum(y * y, axis=-1, keepdims=True) + EPS) * q_scale
        o_ref[pl.ds(r0, CONV_ROWS), :] = (y * jnp.where(is_qk, inv, 1.0)).astype(o_ref.dtype)
        return carry

    lax.fori_loop(0, L // CONV_ROWS, body, 0, unroll=2)


def _gdn_pre(p3, conv_w, col0, n_heads, head_dim):
    B, L, _ = p3.shape
    assert head_dim == LANES
    nt = 3 * n_heads
    off = col0 // LANES
    return pl.pallas_call(
        functools.partial(_gdn_pre_kernel, n_heads=n_heads, head_dim=head_dim),
        grid=(B, nt),
        in_specs=[pl.BlockSpec((None, L, LANES), lambda b, c: (b, 0, c + off)),
                  pl.BlockSpec((3, LANES), lambda b, c: (0, c))],
        out_specs=pl.BlockSpec((None, L, LANES), lambda b, c: (b, 0, c)),
        out_shape=jax.ShapeDtypeStruct((B, L, nt * LANES), BF16),
        compiler_params=_cparams(("parallel", "parallel"), 24),
        name="gdn_pre",
    )(p3, conv_w)


GATE_ROWS = 512


def _gdn_gates_kernel(pg_ref, alog_ref, dtb_ref, o_ref, *, n_heads):
    H = n_heads
    x = pg_ref[...]
    beta = jax.nn.sigmoid(x)
    z = x + dtb_ref[...]
    softplus = jnp.maximum(z, 0.0) + jnp.log1p(jnp.exp(-jnp.abs(z)))
    g = -jnp.exp(alog_ref[...]) * softplus
    pos = lax.broadcasted_iota(jnp.int32, x.shape, 0) & (GDN_CHUNK - 1)
    gc_f = g
    gc_b = g
    step = 1
    while step < GDN_CHUNK:
        gc_f = gc_f + jnp.where(pos >= step, pltpu.roll(gc_f, step, 0), 0.0)
        gc_b = gc_b + jnp.where(pos < GDN_CHUNK - step, pltpu.roll(gc_b, GATE_ROWS - step, 0), 0.0)
        step *= 2
    g_tot = pltpu.roll(gc_f + gc_b - g, 4 * H, 1)
    lane = lax.broadcasted_iota(jnp.int32, x.shape, 1)
    out = jnp.where(lane < 2 * H, beta,
                    jnp.where(lane < 3 * H, gc_f,
                              jnp.where(lane < 4 * H, gc_b,
                                        jnp.where((lane >= 6 * H) & (lane < 8 * H), g_tot, 0.0))))
    o_ref[...] = out


def _gdn_gates(pg, a_log_f, a_log_b, dt_bias_f, dt_bias_b, n_heads):
    M = pg.shape[0]
    H = n_heads
    assert 8 * H <= LANES
    pad = lambda a, b: jnp.concatenate([jnp.zeros((2 * H,), F32), a.astype(F32), b.astype(F32),
                                        jnp.zeros((LANES - 4 * H,), F32)]).reshape(1, LANES)
    return pl.pallas_call(
        functools.partial(_gdn_gates_kernel, n_heads=H),
        grid=(M // GATE_ROWS,),
        in_specs=[pl.BlockSpec((GATE_ROWS, LANES), lambda i: (i, 0)),
                  pl.BlockSpec((1, LANES), lambda i: (0, 0)),
                  pl.BlockSpec((1, LANES), lambda i: (0, 0))],
        out_specs=pl.BlockSpec((GATE_ROWS, LANES), lambda i: (i, 0)),
        out_shape=jax.ShapeDtypeStruct((M, LANES), F32),
        compiler_params=_cparams(("parallel",), 24),
        name="gdn_gates",
    )(pg, pad(a_log_f, a_log_b), pad(dt_bias_f, dt_bias_b))


def _delta_chunks(q, k, v, beta, gc_col, gc_row, gtot, state, lower):
    n = len(q)
    C = q[0].shape[0]
    D = k[0].shape[1]
    ii = lax.broadcasted_iota(jnp.int32, (C, C), 0)
    jj = lax.broadcasted_iota(jnp.int32, (C, C), 1)
    eye = jnp.where(ii == jj, 1.0, 0.0)
    incl = [(ii >= jj) if lo else (ii <= jj) for lo in lower]
    strict = [(ii > jj) if lo else (ii < jj) for lo in lower]
    rng = range(n)
    decay = [jnp.where(incl[i], jnp.exp(jnp.where(incl[i], gc_col[i] - gc_row[i], 0.0)), 0.0) for i in rng]
    kb = [k[i] * beta[i] for i in rng]
    kk = [_dot_nt(kb[i], k[i]) for i in rng]
    qk = [_dot_nt(q[i], k[i]) for i in rng]
    m = [jnp.where(strict[i], -(kk[i] * decay[i]), 0.0) for i in rng]
    r = [eye + m[i] for i in rng]
    m = [_dot(m[i], m[i]) for i in rng]
    for _ in range(int(math.log2(C)) - 2):
        rm = [_dot(jnp.concatenate([r[i], m[i]], axis=0), m[i]) for i in rng]
        r = [r[i] + rm[i][:C] for i in rng]
        m = [rm[i][C:] for i in rng]
    r = [r[i] + _dot(r[i], m[i]) for i in rng]
    eg = [jnp.exp(gc_col[i]) for i in rng]
    wu = [_dot(r[i], jnp.concatenate([kb[i] * eg[i], v[i] * beta[i]], axis=1)) for i in rng]
    ws = [_dot(jnp.concatenate([wu[i][:, :D], q[i] * eg[i]], axis=0), state[i]) for i in rng]
    v_new = [wu[i][:, D:] - ws[i][:C] for i in rng]
    qkm = [jnp.where(incl[i], qk[i] * decay[i], 0.0) for i in rng]
    out = [ws[i][C:] + _dot(qkm[i], v_new[i]) for i in rng]
    k_dec = [k[i] * jnp.exp(gtot[i] - gc_col[i]) for i in rng]
    new_state = [state[i] * jnp.exp(gtot[i][0:1, :]) + _dot_tn(k_dec[i], v_new[i]) for i in rng]
    return out, new_state


def _gdn_scan_kernel(qf_ref, kf_ref, vf_ref, qb_ref, kb_ref, vb_ref, gf_ref, gb_ref, rf_ref, rb_ref,
                     of_ref, ob_ref, s_scr, *, n_heads, head_dim):
    H, Dh = n_heads, head_dim
    B = qf_ref.shape[0]

    @pl.when(pl.program_id(0) == 0)
    def _():
        s_scr[...] = jnp.zeros_like(s_scr)

    col = lambda g, j: g[:, j:j + 1]
    sls = [slice(h * Dh, (h + 1) * Dh) for h in range(H)]
    q, k, v, beta, gc_col, gc_row, gtot, state = ([] for _ in range(8))
    for b in range(B):
        gf = gf_ref[b]
        gb = gb_ref[b]
        q += [qf_ref[b, :, sl] for sl in sls] + [qb_ref[b, :, sl] for sl in sls]
        k += [kf_ref[b, :, sl] for sl in sls] + [kb_ref[b, :, sl] for sl in sls]
        v += [vf_ref[b, :, sl] for sl in sls] + [vb_ref[b, :, sl] for sl in sls]
        beta += [col(gf, h) for h in range(H)] + [col(gb, H + h) for h in range(H)]
        gc_col += [col(gf, 2 * H + h) for h in range(H)] + [col(gb, 3 * H + h) for h in range(H)]
        gc_row += [rf_ref[b, h:h + 1, :] for h in range(H)] + [rb_ref[b, H + h:H + h + 1, :] for h in range(H)]
        gtot += [col(gf, 6 * H + h) for h in range(H)] + [col(gb, 7 * H + h) for h in range(H)]
        state += [s_scr[b, 0, h] for h in range(H)] + [s_scr[b, 1, h] for h in range(H)]
    out, new_state = _delta_chunks(q, k, v, beta, gc_col, gc_row, gtot, state, ([True] * H + [False] * H) * B)
    for b in range(B):
        for h in range(H):
            of_ref[b, :, sls[h]] = out[2 * H * b + h]
            ob_ref[b, :, sls[h]] = out[2 * H * b + H + h]
            s_scr[b, 0, h] = new_state[2 * H * b + h]
            s_scr[b, 1, h] = new_state[2 * H * b + H + h]


def _gdn_scan(qkv, gates, gates_row, n_heads, head_dim):
    B, L, _ = qkv.shape
    H, Dh = n_heads, head_dim
    d = H * Dh
    C = GDN_CHUNK
    N = L // C
    fwd = lambda col: pl.BlockSpec((B, C, d), lambda n: (0, n, col))
    bwd = lambda col: pl.BlockSpec((B, C, d), lambda n: (0, N - 1 - n, col))
    return pl.pallas_call(
        functools.partial(_gdn_scan_kernel, n_heads=H, head_dim=Dh),
        grid=(N,),
        in_specs=[fwd(0), fwd(1), fwd(2), bwd(0), bwd(1), bwd(2),
                  pl.BlockSpec((B, C, LANES), lambda n: (0, n, 0)),
                  pl.BlockSpec((B, C, LANES), lambda n: (0, N - 1 - n, 0)),
                  pl.BlockSpec((B, None, 2 * H, C), lambda n: (0, n, 0, 0)),
                  pl.BlockSpec((B, None, 2 * H, C), lambda n: (0, N - 1 - n, 0, 0))],
        out_specs=[pl.BlockSpec((B, C, d), lambda n: (0, n, 0)),
                   pl.BlockSpec((B, C, d), lambda n: (0, N - 1 - n, 0))],
        out_shape=[jax.ShapeDtypeStruct((B, L, d), F32)] * 2,
        scratch_shapes=[pltpu.VMEM((B, 2, H, Dh, Dh), F32)],
        compiler_params=_cparams(("arbitrary",), 32),
        name="gdn_scan",
    )(qkv, qkv, qkv, qkv, qkv, qkv, gates, gates, gates_row, gates_row)


def _gdn_branch(p3, pg, conv_w, a_log_f, a_log_b, dt_bias_f, dt_bias_b, col0, n_heads, head_dim):
    B, L, _ = p3.shape
    H = n_heads
    qkv = _gdn_pre(p3, conv_w, col0, H, head_dim)
    gates = _gdn_gates(pg, a_log_f, a_log_b, dt_bias_f, dt_bias_b, H).reshape(B, L, LANES)
    N = L // GDN_CHUNK
    gates_row = gates[..., 2 * H:4 * H].reshape(B, N, GDN_CHUNK, 2 * H).transpose(0, 1, 3, 2)
    return _gdn_scan(qkv, gates, gates_row, H, head_dim)


@functools.lru_cache(maxsize=None)
def _filter_positions(L, pos_emb_dim):
    n = 2 * L
    r = np.arange(n)
    k = np.where(r < L, r, np.where(r == L, 0, n - r)).astype(np.float64)
    t = k / (L - 1)
    bands = (pos_emb_dim - 1) // 2
    fb = np.linspace(1e-4, bands - 1, bands)
    ang = (2.0 * math.pi / L) * k[:, None] * fb[None, :]
    z = np.concatenate([t[:, None], np.cos(ang), -np.sin(ang)], axis=-1)
    return z.astype(np.float32)


@functools.lru_cache(maxsize=None)
def _decay_rates(d_hy):
    max_decay = math.log(DECAY_TARGET) / FAST_DECAY_PCT
    min_decay = math.log(DECAY_TARGET) / SLOW_DECAY_PCT
    return np.abs(np.linspace(min_decay, max_decay, d_hy)).astype(np.float32).reshape(1, d_hy)


def _filt_mlp_kernel(zt_ref, w1t_ref, b1_ref, w2t_ref, b2_ref, w3t_ref, b3_ref, fr_ref, o_ref):
    fr = fr_ref[...]
    h = jnp.sin(fr * (_dot_hi(w1t_ref[...], zt_ref[...]) + b1_ref[...]))
    h = jnp.sin(fr * (_dot_hi(w2t_ref[...], h) + b2_ref[...]))
    o_ref[...] = jnp.sin(fr * (_dot_hi(w3t_ref[...], h) + b3_ref[...]))


def _filt_mlp(zt, w1, b1, w2, b2, w3, b3, freq, tc=1024):
    pe, n = zt.shape
    fw = w1.shape[1]
    col = lambda a: a.reshape(-1, 1).astype(F32)
    full = lambda a: pl.BlockSpec(a.shape, lambda i: (0, 0))
    args = (zt, w1.T, col(b1), w2.T, col(b2), w3.T, col(b3), col(freq))
    return pl.pallas_call(
        _filt_mlp_kernel,
        grid=(n // tc,),
        in_specs=[pl.BlockSpec((pe, tc), lambda i: (0, i))] + [full(a) for a in args[1:]],
        out_specs=pl.BlockSpec((fw, tc), lambda i: (0, i)),
        out_shape=jax.ShapeDtypeStruct((fw, n), F32),
        compiler_params=_cparams(("parallel",), 24),
        name="filt_mlp",
    )(*args)


@functools.lru_cache(maxsize=None)
def _dft_tables(L):
    n = 2 * L
    N2 = FFT_N2
    N1 = n // N2
    N1h = N1 // 2
    j2 = np.arange(N2)[:, None, None]
    k1 = np.arange(N1)[None, :, None]

    def stage1(n_j1):
        j1 = np.arange(n_j1)[None, None, :]
        m = (k1 * (N2 * j1 + j2)) % n
        th = 2.0 * np.pi * m / n
        return np.cos(th), np.sin(th)

    c, s = stage1(N1h)
    t1 = np.concatenate([np.concatenate([c, s], axis=2), np.concatenate([-s, c], axis=2)], axis=1)
    c, s = stage1(N1)
    t1g = np.concatenate([c, -s], axis=1)
    c, s = stage1(N1h)
    ct, st = np.swapaxes(c, 1, 2) / n, np.swapaxes(s, 1, 2) / n
    t2 = np.concatenate([np.concatenate([ct, -st], axis=2), np.concatenate([st, ct], axis=2)], axis=1)
    a = np.arange(N2)
    th = 2.0 * np.pi * ((a[:, None] * a[None, :]) % N2) / N2
    c2, s2 = np.cos(th), np.sin(th)
    f2f = np.block([[c2, s2], [-s2, c2]])
    f2i = np.block([[c2, -s2], [s2, c2]])
    as_bf16 = lambda x: jnp.asarray(x, dtype=F32).astype(BF16)
    return dict(N1=N1, N2=N2, t1=t1.astype(np.float32), t1g=t1g.astype(np.float32), t2=t2.astype(np.float32),
                f2f=f2f.astype(np.float32), f2i=f2i.astype(np.float32))


FILT_ROWS = 512
FFT_UNROLL = 8
PITCH_PAD = 8


def _filt_fft_kernel(h3_ref, w4f_ref, w4b_ref, delta_ref, t1g_ref, f2f_ref, hspec_ref, hb0_ref, g_scr, a_scr,
                     *, L, N1, N2):
    n = 2 * L
    gp = N2 + PITCH_PAD
    ap = 2 * N1 + PITCH_PAD
    delta = delta_ref[...]
    hb0_ref[...] = jnp.zeros_like(hb0_ref)

    def gen(c, carry):
        r0 = pl.multiple_of(c * FILT_ROWS, FILT_ROWS)
        row = r0 + lax.broadcasted_iota(jnp.int32, (FILT_ROWS, LANES), 0)
        lag = jnp.where(row < L, row, jnp.where(row == L, 0, n - row))
        window = jnp.exp(-(lag.astype(F32) * (1.0 / (L - 1))) * delta)
        w4 = jnp.where(r0 < L, w4f_ref[...], w4b_ref[...])
        g = _dot(h3_ref[pl.ds(r0, FILT_ROWS), :], w4) * window
        at_l = row == L
        hb0_ref[...] += jnp.sum(jnp.where(at_l, g, 0.0), axis=0, keepdims=True)
        g = jnp.where(at_l, 0.0, g)
        for q in range(FILT_ROWS // N2):
            dst = pl.multiple_of((c * (FILT_ROWS // N2) + q) * gp, SUBLANES)
            g_scr[pl.ds(dst, N2), :] = g[q * N2:(q + 1) * N2]
        return carry

    lax.fori_loop(0, n // FILT_ROWS, gen, 0)

    def stage1(j2, carry):
        x = g_scr[pl.ds(j2, N1, stride=gp), :]
        a_scr[pl.ds(pl.multiple_of(j2 * ap, SUBLANES), 2 * N1), :] = _dot(t1g_ref[j2], x)
        return carry

    lax.fori_loop(0, N2, stage1, 0, unroll=2 * FFT_UNROLL)

    def stage2(k1, carry):
        ar = a_scr[pl.ds(k1, N2, stride=ap), :]
        ai = a_scr[pl.ds(N1 + k1, N2, stride=ap), :]
        z = _dot(f2f_ref[...], jnp.concatenate([ar, ai], axis=0))
        hspec_ref[pl.ds(pl.multiple_of(k1 * 2 * N2, 2 * N2), 2 * N2), :] = z.astype(hspec_ref.dtype)
        return carry

    lax.fori_loop(0, N1, stage2, 0, unroll=FFT_UNROLL)


def _filt_fft(h3, w4, L, d_hy):
    tb = _dft_tables(L)
    N1, N2 = tb["N1"], tb["N2"]
    n = 2 * L
    fw = h3.shape[1]
    nt = d_hy // LANES
    t1g = jnp.asarray(tb["t1g"]).astype(BF16)
    f2f = jnp.asarray(tb["f2f"]).astype(BF16)
    return pl.pallas_call(
        functools.partial(_filt_fft_kernel, L=L, N1=N1, N2=N2),
        grid=(nt,),
        in_specs=[pl.BlockSpec((n, fw), lambda c: (0, 0)),
                  pl.BlockSpec((fw, LANES), lambda c: (0, c)),
                  pl.BlockSpec((fw, LANES), lambda c: (0, c + nt)),
                  pl.BlockSpec((1, LANES), lambda c: (0, c)),
                  pl.BlockSpec(t1g.shape, lambda c: (0, 0, 0)),
                  pl.BlockSpec(f2f.shape, lambda c: (0, 0))],
        out_specs=[pl.BlockSpec((2 * n, LANES), lambda c: (0, c)),
                   pl.BlockSpec((SUBLANES, LANES), lambda c: (0, c))],
        out_shape=[jax.ShapeDtypeStruct((2 * n, d_hy), BF16), jax.ShapeDtypeStruct((SUBLANES, d_hy), F32)],
        scratch_shapes=[pltpu.VMEM((N1 * (N2 + PITCH_PAD), LANES), F32),
                        pltpu.VMEM((N2 * (2 * N1 + PITCH_PAD), LANES), F32)],
        compiler_params=_cparams(("parallel",), 48),
        name="filt_fft",
    )(h3, w4, w4, jnp.asarray(_decay_rates(d_hy)), t1g, f2f)


def _hy_conv_kernel(vg_ref, hspec_ref, skip_ref, hb0_ref, t1_ref, f2f_ref, f2i_ref, t2_ref, y_ref,
                    x_scr, a_scr, b_scr, *, N1, N2):
    N1h = N1 // 2
    xp = N2 + PITCH_PAD
    ap = 2 * N1 + PITCH_PAD
    bp = 2 * N2 + PITCH_PAD

    for b in range(2):
        for j1 in range(N1h):
            x_scr[b, pl.ds(j1 * xp, N2), :] = vg_ref[b, pl.ds(j1 * N2, N2), :].astype(F32)

    def stage1(j2, carry):
        x = jnp.concatenate([x_scr[0, pl.ds(j2, N1h, stride=xp), :],
                             x_scr[1, pl.ds(j2, N1h, stride=xp), :]], axis=0)
        a_scr[pl.ds(pl.multiple_of(j2 * ap, SUBLANES), 2 * N1), :] = _dot(t1_ref[j2], x)
        return carry

    lax.fori_loop(0, N2, stage1, 0, unroll=2 * FFT_UNROLL)

    def stage2(k1, carry):
        ar = a_scr[pl.ds(k1, N2, stride=ap), :]
        ai = a_scr[pl.ds(N1 + k1, N2, stride=ap), :]
        z = _dot(f2f_ref[...], jnp.concatenate([ar, ai], axis=0))
        zr, zi = z[:N2], z[N2:]
        base = pl.multiple_of(k1 * 2 * N2, 2 * N2)
        hr = hspec_ref[pl.ds(base, N2), :].astype(F32)
        hi = hspec_ref[pl.ds(base + N2, N2), :].astype(F32)
        prod = jnp.concatenate([zr * hr - zi * hi, zr * hi + zi * hr], axis=0)
        b_scr[pl.ds(pl.multiple_of(k1 * bp, SUBLANES), 2 * N2), :] = _dot(f2i_ref[...], prod)
        return carry

    lax.fori_loop(0, N1, stage2, 0, unroll=FFT_UNROLL)

    skip = skip_ref[...] + hb0_ref[0:1, :]

    def stage3(j2, carry):
        b = jnp.concatenate([b_scr[pl.ds(j2, N1, stride=bp), :],
                             b_scr[pl.ds(N2 + j2, N1, stride=bp), :]], axis=0)
        y = _dot(t2_ref[j2], b)
        x_scr[0, pl.ds(j2, N1h, stride=xp), :] = y[:N1h] + x_scr[0, pl.ds(j2, N1h, stride=xp), :] * skip
        x_scr[1, pl.ds(j2, N1h, stride=xp), :] = y[N1h:] + x_scr[1, pl.ds(j2, N1h, stride=xp), :] * skip
        return carry

    lax.fori_loop(0, N2, stage3, 0, unroll=2 * FFT_UNROLL)

    for b in range(2):
        for j1 in range(N1h):
            y_ref[b, pl.ds(j1 * N2, N2), :] = x_scr[b, pl.ds(j1 * xp, N2), :]


def _hy_conv(vg, hspec, skip, hb0):
    B, L, d_hy = vg.shape
    assert B % 2 == 0
    tb = _dft_tables(L)
    N1, N2 = tb["N1"], tb["N2"]
    n = 2 * L
    nt = d_hy // LANES
    t1, t2 = (jnp.asarray(tb[k]).astype(BF16) for k in ("t1", "t2"))
    f2f, f2i = (jnp.asarray(tb[k]).astype(BF16) for k in ("f2f", "f2i"))
    const3 = lambda a: pl.BlockSpec(a.shape, lambda b, c: (0, 0, 0))
    const2 = lambda a: pl.BlockSpec(a.shape, lambda b, c: (0, 0))
    return pl.pallas_call(
        functools.partial(_hy_conv_kernel, N1=N1, N2=N2),
        grid=(B // 2, nt),
        in_specs=[pl.BlockSpec((2, L, LANES), lambda b, c: (b, 0, c)),
                  pl.BlockSpec((2 * n, LANES), lambda b, c: (0, c)),
                  pl.BlockSpec((1, LANES), lambda b, c: (0, c)),
                  pl.BlockSpec((SUBLANES, LANES), lambda b, c: (0, c)),
                  const3(t1), const2(f2f), const2(f2i), const3(t2)],
        out_specs=pl.BlockSpec((2, L, LANES), lambda b, c: (b, 0, c)),
        out_shape=jax.ShapeDtypeStruct((B, L, d_hy), F32),
        scratch_shapes=[pltpu.VMEM((2, (N1 // 2) * (N2 + PITCH_PAD), LANES), F32),
                        pltpu.VMEM((N2 * (2 * N1 + PITCH_PAD), LANES), F32),
                        pltpu.VMEM((N1 * (2 * N2 + PITCH_PAD), LANES), F32)],
        compiler_params=_cparams(("parallel", "parallel"), 58),
        name="hy_conv",
    )(vg, hspec, skip.reshape(1, d_hy).astype(F32), hb0, t1, f2f, f2i, t2)


def _hyena_branch(p3, conv_w, conv_b, fw1, fb1, fw2, fb2, fw3, fb3, fw4, freq, skip, d_hy):
    B, L, _ = p3.shape
    x0c, vg = _hy_pre(p3, conv_w, conv_b, d_hy)
    zt = jnp.asarray(_filter_positions(L, fw1.shape[0]).T)
    h3 = _filt_mlp(zt, fw1, fb1, fw2, fb2, fw3, fb3, freq).T
    hspec, hb0 = _filt_fft(h3, fw4, L, d_hy)
    return x0c, _hy_conv(vg, hspec, skip, hb0)


ROUTE_GATE, ROUTE_EXPERT, ROUTE_RANK = 0, 2, 4
MIX_ROWS = 256


def _mix_route_kernel(yc_ref, x0_ref, of_ref, ob_ref, z_ref, hnw_ref, gnw_ref, x_ref, wo_ref, n2w_ref, wr_ref, br_ref,
                      x2_ref, u_ref, route_ref, route_t_ref, cnt_ref, run_scr, *, n_heads, head_dim, n_groups,
                      per_group):
    tm = x_ref.shape[0]
    G, P = n_groups, per_group

    @pl.when(pl.program_id(0) == 0)
    def _():
        run_scr[...] = jnp.zeros_like(run_scr)

    run = run_scr[0:1, :]
    blocks = [pl.ds(s * MIX_ROWS, MIX_ROWS) for s in range(tm // MIX_ROWS)]
    us = []
    for blk, rs in enumerate(blocks):
        yh = yc_ref[rs, :] * x0_ref[rs, :].astype(F32)
        yh = yh * lax.rsqrt(jnp.mean(yh * yh, axis=-1, keepdims=True) + EPS) * hnw_ref[...]
        parts = [yh.astype(BF16)]
        for h in range(n_heads):
            sl = slice(h * head_dim, (h + 1) * head_dim)
            o = of_ref[rs, sl] + ob_ref[rs, sl]
            z = z_ref[rs, sl].astype(F32)
            o = o * lax.rsqrt(jnp.mean(o * o, axis=-1, keepdims=True) + EPS) * gnw_ref[...] * _silu(z)
            parts.append(o.astype(BF16))
        ymix = jnp.concatenate(parts, axis=-1)
        x2 = x_ref[rs, :] + jnp.dot(ymix, wo_ref[...], preferred_element_type=F32)
        x2_ref[rs, :] = x2
        u = x2 * lax.rsqrt(jnp.mean(x2 * x2, axis=-1, keepdims=True) + EPS) * n2w_ref[...]
        _store_token_tiles(u_ref, blk * MIX_ROWS, _pack_bf16_pairs(u))
        us.append(u)

    all_logits = [_dot(u, wr_ref[...]) + br_ref[...] for u in us]
    for s, (rs, logits) in enumerate(zip(blocks, all_logits)):
        lane = lax.broadcasted_iota(jnp.int32, logits.shape, 1)
        neg = jnp.float32(-jnp.inf)
        big = jnp.int32(4 * LANES)
        first = lambda hit: jnp.min(jnp.where(hit, lane, big), axis=-1, keepdims=True)
        gl = jnp.where(lane < G, logits, neg)
        gmax = jnp.max(gl, axis=-1, keepdims=True)
        gidx = first(gl == gmax)
        grp_gate = 1.0 / jnp.sum(jnp.exp(gl - gmax), axis=-1, keepdims=True)
        in_grp = (lane >= G) & (lane < G + G * P) & (((lane - G) // P) == gidx)
        ll = jnp.where(in_grp, logits, neg)
        m1 = jnp.max(ll, axis=-1, keepdims=True)
        i1 = first(ll == m1)
        denom = jnp.sum(jnp.exp(ll - m1), axis=-1, keepdims=True)
        ll2 = jnp.where(lane == i1, neg, ll)
        m2 = jnp.max(ll2, axis=-1, keepdims=True)
        i2 = first(ll2 == m2)
        p1 = 1.0 / denom
        p2 = jnp.exp(m2 - m1) / denom
        gate1 = grp_gate * (p1 / (p1 + p2))
        gate2 = grp_gate * (p2 / (p1 + p2))
        e1 = i1 - G
        e2 = i2 - G

        oh1 = jnp.where(lane == e1, 1.0, 0.0)
        oh2 = jnp.where(lane == e2, 1.0, 0.0)
        oh = oh1 + oh2
        ii = lax.broadcasted_iota(jnp.int32, (MIX_ROWS, MIX_ROWS), 0)
        jj = lax.broadcasted_iota(jnp.int32, (MIX_ROWS, MIX_ROWS), 1)
        before = _dot(jnp.where(ii > jj, 1.0, 0.0), oh) + run
        r1 = jnp.sum(oh1 * before, axis=-1, keepdims=True)
        r2 = jnp.sum(oh2 * before, axis=-1, keepdims=True)
        run = run + jnp.sum(oh, axis=0, keepdims=True)

        rec = jnp.where(lane == ROUTE_GATE, gate1, 0.0)
        rec = jnp.where(lane == ROUTE_GATE + 1, gate2, rec)
        rec = jnp.where(lane == ROUTE_EXPERT, e1.astype(F32), rec)
        rec = jnp.where(lane == ROUTE_EXPERT + 1, e2.astype(F32), rec)
        rec = jnp.where(lane == ROUTE_RANK, r1, rec)
        rec = jnp.where(lane == ROUTE_RANK + 1, r2, rec)
        route_ref[rs, :] = rec
        route_t_ref[:, s * MIX_ROWS:(s + 1) * MIX_ROWS] = jnp.transpose(rec)[:SUBLANES, :]

    run_scr[...] = jnp.broadcast_to(run, run_scr.shape)
    cnt_ref[...] = run_scr[...]


def _mix_route(yconv, x0c, o_f, o_b, p, z_col, hy_norm_w, gdn_norm_w, xf, w_out_bf16, norm2_w, wr, br,
               n_heads, head_dim, n_groups, per_group, tm=512):
    M, D = xf.shape
    d_hy = yconv.shape[1]
    d_gdn = o_f.shape[1]
    assert z_col % d_gdn == 0 and n_groups * (per_group + 1) <= LANES
    zb = z_col // d_gdn
    row = lambda i: (i, 0)
    const = lambda i: (0, 0)
    kern = functools.partial(_mix_route_kernel, n_heads=n_heads, head_dim=head_dim,
                             n_groups=n_groups, per_group=per_group)
    return pl.pallas_call(
        kern,
        grid=(M // tm,),
        in_specs=[pl.BlockSpec((tm, d_hy), row), pl.BlockSpec((tm, d_hy), row),
                  pl.BlockSpec((tm, d_gdn), row), pl.BlockSpec((tm, d_gdn), row),
                  pl.BlockSpec((tm, d_gdn), lambda i: (i, zb)),
                  pl.BlockSpec((1, d_hy), const), pl.BlockSpec((1, head_dim), const),
                  pl.BlockSpec((tm, D), row), pl.BlockSpec(w_out_bf16.shape, const, pipeline_mode=pl.Buffered(1)),
                  pl.BlockSpec((1, D), const), pl.BlockSpec((D, LANES), const), pl.BlockSpec((1, LANES), const)],
        out_specs=[pl.BlockSpec((tm, D), row), pl.BlockSpec((tm * (D // 2 // LANES), LANES), row),
                   pl.BlockSpec((tm, LANES), row), pl.BlockSpec((SUBLANES, tm), lambda i: (0, i)),
                   pl.BlockSpec((SUBLANES, LANES), const)],
        out_shape=[jax.ShapeDtypeStruct((M, D), F32), jax.ShapeDtypeStruct((M * (D // 2 // LANES), LANES), jnp.uint32),
                   jax.ShapeDtypeStruct((M, LANES), F32), jax.ShapeDtypeStruct((SUBLANES, M), F32),
                   jax.ShapeDtypeStruct((SUBLANES, LANES), F32)],
        scratch_shapes=[pltpu.VMEM((SUBLANES, LANES), F32)],
        compiler_params=_cparams(("arbitrary",), 58),
        name="mix_route",
    )(yconv, x0c, o_f, o_b, p, hy_norm_w.reshape(1, d_hy), gdn_norm_w.reshape(1, head_dim), xf, w_out_bf16,
      norm2_w.reshape(1, D), wr, br)


WAIT_GROUP = 8


def _experts_kernel(ts_ref, se_ref, nn_ref, nv_ref, tgt_nxt_ref, tgt_ref, tgt_prv_ref, u_hbm, w1_hbm, w3_hbm, w2_hbm,
                    out_hbm, x0, x1, y0, y1, wf1, wf3, wf2, w1b, w3b, w2b, gsem, ssem, wsem, *, n_tokens):
    i = pl.program_id(0)
    na = nn_ref[0]
    ns = nn_ref[1]
    CH = w1b.shape[0] // 2 // LANES
    TB = x0.shape[0] // CH
    W_SLOTS = wf1.shape[0]
    last_tile = pl.num_programs(0) - 1
    nv_cur = nv_ref[i]
    nv_nxt = jnp.where(i + 1 < na, nv_ref[jnp.minimum(i + 1, last_tile)], 0)
    nv_prv = jnp.where(i > 0, nv_ref[jnp.maximum(i - 1, 0)], 0)
    nv_pp = jnp.where(i > 1, nv_ref[jnp.maximum(i - 2, 0)], 0)

    def weight_copies(s):
        slot = lax.rem(s, W_SLOTS)
        e = se_ref[s]
        return [pltpu.make_async_copy(w_hbm.at[e], wf.at[slot], wsem.at[slot])
                for w_hbm, wf in ((w1_hbm, wf1), (w3_hbm, wf3), (w2_hbm, wf2))]

    def token_of(v):
        if n_tokens & (n_tokens - 1) == 0:
            return v & (n_tokens - 1)
        return lax.rem(v, n_tokens)

    def tile_of(row):
        return pl.ds(pl.multiple_of(row * CH, CH), CH)

    def gather(tgt, xbuf, s, r):
        return pltpu.make_async_copy(u_hbm.at[tile_of(token_of(tgt[r]))], xbuf.at[tile_of(r)], gsem.at[s])

    def scatter(ybuf, s, r, dst):
        return pltpu.make_async_copy(ybuf.at[tile_of(r)], out_hbm.at[tile_of(dst)], ssem.at[s])

    def wait_tokens(count, make):
        groups = count // WAIT_GROUP

        def grp(g, carry):
            make(WAIT_GROUP).wait()
            return carry

        lax.fori_loop(0, groups, grp, 0)

        def one(g, carry):
            make(1).wait()
            return carry

        lax.fori_loop(groups * WAIT_GROUP, count, one, 0)

    def wait_gather(xbuf, s, count):
        wait_tokens(count, lambda n: pltpu.make_async_copy(u_hbm.at[pl.ds(0, n * CH)], xbuf.at[pl.ds(0, n * CH)],
                                                             gsem.at[s]))

    def wait_scatter(ybuf, s, count):
        wait_tokens(count, lambda n: pltpu.make_async_copy(ybuf.at[pl.ds(0, n * CH)], out_hbm.at[pl.ds(0, n * CH)],
                                                             ssem.at[s]))

    def each_row(count, fn):
        def body(r, carry):
            fn(r)
            return carry
        lax.fori_loop(0, count, body, 0)

    def step(p):
        q = 1 - p
        x_cur, x_nxt = (x0, x1) if p == 0 else (x1, x0)
        y_cur, y_prv = (y0, y1) if p == 0 else (y1, y0)
        wait_gather(x_cur, p, nv_cur)
        wait_scatter(y_cur, p, nv_pp)

        x = _unpack_bf16_pairs(_load_token_tiles(x_cur, 0, TB, CH)).astype(BF16)
        for r in range(TB):
            @pl.when(r < nv_nxt)
            def _():
                gather(tgt_nxt_ref, x_nxt, q, r).start()
        for r in range(TB):
            @pl.when(r < nv_prv)
            def _():
                scatter(y_prv, q, r, tgt_prv_ref[r]).start()
        h = _silu(jnp.dot(x, w1b[...], preferred_element_type=F32)) * jnp.dot(x, w3b[...], preferred_element_type=F32)
        _store_token_tiles(y_cur, 0, _pack_bf16_pairs(jnp.dot(h.astype(BF16), w2b[...], preferred_element_type=F32)))

        @pl.when(i == na - 1)
        def _():
            wait_scatter(y_prv, q, nv_prv)
            each_row(nv_cur, lambda r: scatter(y_cur, p, r, tgt_ref[r]).start())
            wait_scatter(y_cur, p, nv_cur)

    @pl.when(i < na)
    def _():
        @pl.when(i == 0)
        def _():
            x0[...] = jnp.zeros_like(x0)
            x1[...] = jnp.zeros_like(x1)
            each_row(nv_cur, lambda r: gather(tgt_ref, x0, 0, r).start())
            for s0 in range(W_SLOTS - 1):
                @pl.when(s0 < ns)
                def _():
                    for c in weight_copies(s0):
                        c.start(priority=WEIGHT_DMA_PRIORITY)

        s = ts_ref[i]

        @pl.when((i == 0) | (s != ts_ref[jnp.maximum(i - 1, 0)]))
        def _():
            @pl.when(s + W_SLOTS - 1 < ns)
            def _():
                for c in weight_copies(s + W_SLOTS - 1):
                    c.start(priority=WEIGHT_DMA_PRIORITY)

            for c in weight_copies(s):
                c.wait()
            slot = lax.rem(s, W_SLOTS)
            w1b[...] = wf1[slot].astype(BF16)
            w3b[...] = wf3[slot].astype(BF16)
            w2b[...] = wf2[slot].astype(BF16)

        parity = lax.rem(i, 2)

        @pl.when(parity == 0)
        def _():
            step(0)

        @pl.when(parity == 1)
        def _():
            step(1)


WEIGHT_SLOTS = 3
WEIGHT_DMA_PRIORITY = 1


def _experts(u, slot_tgt, tile_seq, seq_expert, n_active_seq, tile_valid, w1, w3, w2):
    E, D, de = w1.shape
    CH = D // 2 // LANES
    assert CH % SUBLANES == 0
    T = u.shape[0] // CH
    TB = EXPERT_ROWS
    n_tiles = slot_tgt.shape[0] // TB
    table = lambda f: pl.BlockSpec((TB,), lambda i, ts, se, nn, nv: (f(i),), memory_space=pltpu.SMEM)
    hbm = pl.BlockSpec(memory_space=pl.ANY)
    grid_spec = pltpu.PrefetchScalarGridSpec(
        num_scalar_prefetch=4,
        grid=(n_tiles,),
        in_specs=[table(lambda i: jnp.minimum(i + 1, n_tiles - 1)), table(lambda i: i),
                  table(lambda i: jnp.maximum(i - 1, 0)), hbm, hbm, hbm, hbm],
        out_specs=hbm,
        scratch_shapes=[pltpu.VMEM((TB * CH, LANES), jnp.uint32)] * 4
                       + [pltpu.VMEM((WEIGHT_SLOTS, D, de), F32), pltpu.VMEM((WEIGHT_SLOTS, D, de), F32),
                          pltpu.VMEM((WEIGHT_SLOTS, de, D), F32),
                          pltpu.VMEM((D, de), BF16), pltpu.VMEM((D, de), BF16), pltpu.VMEM((de, D), BF16),
                          pltpu.SemaphoreType.DMA((2,)), pltpu.SemaphoreType.DMA((2,)),
                          pltpu.SemaphoreType.DMA((WEIGHT_SLOTS,))],
    )
    return pl.pallas_call(
        functools.partial(_experts_kernel, n_tokens=T),
        grid_spec=grid_spec,
        out_shape=jax.ShapeDtypeStruct((2 * T * CH, LANES), jnp.uint32),
        compiler_params=_cparams(("arbitrary",), 58),
        name="experts",
    )(tile_seq, seq_expert, n_active_seq, tile_valid, slot_tgt, slot_tgt, slot_tgt, u, w1, w3, w2)


def _slot_table_kernel(dest_ref, init_hbm, o_ref, sem):
    fill = pltpu.make_async_copy(init_hbm, o_ref, sem)
    fill.start()
    fill.wait()

    def put(a, carry):
        o_ref[dest_ref[a]] = a
        return carry

    lax.fori_loop(0, dest_ref.shape[0], put, 0, unroll=16)


def _slot_table(dest, init):
    smem = pl.BlockSpec(memory_space=pltpu.SMEM)
    return pl.pallas_call(
        _slot_table_kernel,
        in_specs=[smem, pl.BlockSpec(memory_space=pl.ANY)],
        out_specs=smem,
        out_shape=jax.ShapeDtypeStruct(init.shape, jnp.int32),
        scratch_shapes=[pltpu.SemaphoreType.DMA],
        name="slot_table",
    )(dest, init)


def _dispatch_tables(route_t, counts, n_experts):
    T = route_t.shape[1]
    TB = EXPERT_ROWS
    e = route_t[ROUTE_EXPERT:ROUTE_EXPERT + 2].astype(jnp.int32)
    rank = route_t[ROUTE_RANK:ROUTE_RANK + 2].astype(jnp.int32)
    cnt = counts[0, :n_experts].astype(jnp.int32)
    padded = (cnt + TB - 1) // TB * TB
    pad_end = jnp.cumsum(padded)
    pad_start = pad_end - padded
    ids = jnp.arange(n_experts, dtype=jnp.int32)[:, None, None]
    start_of = jnp.sum(jnp.where(e[None] == ids, pad_start[:, None, None], 0), axis=0)
    dest = start_of + rank
    n_tiles = -(-(2 * T + n_experts * (TB - 1)) // TB)
    slot_tgt = _slot_table(dest.reshape(-1), jnp.zeros((n_tiles * TB,), jnp.int32))
    start = jnp.arange(n_tiles, dtype=jnp.int32) * TB
    tile_expert = jnp.sum(start[:, None] >= pad_end[None, :], axis=1)
    tile_expert = jnp.minimum(tile_expert, n_experts - 1).astype(jnp.int32)
    owns = cnt > 0
    pos = jnp.cumsum(owns.astype(jnp.int32)) - 1
    experts = jnp.arange(n_experts, dtype=jnp.int32)
    seq_expert = jnp.sum(jnp.where(owns[None, :] & (pos[None, :] == experts[:, None]), experts[None, :], 0), axis=1)
    tile_seq = jnp.sum(jnp.where(tile_expert[:, None] == experts[None, :], pos[None, :], 0), axis=1)
    n_active_seq = jnp.stack([pad_end[-1] // TB, jnp.sum(owns.astype(jnp.int32))]).astype(jnp.int32)
    pick = tile_expert[:, None] == experts[None, :]
    in_expert = start - jnp.sum(jnp.where(pick, pad_start[None, :], 0), axis=1)
    tile_cnt = jnp.sum(jnp.where(pick, cnt[None, :], 0), axis=1)
    tile_valid = jnp.where(start < pad_end[-1], jnp.clip(tile_cnt - in_expert, 0, TB), 0).astype(jnp.int32)
    return slot_tgt, tile_seq.astype(jnp.int32), seq_expert.astype(jnp.int32), n_active_seq, tile_valid


def _combine_kernel(x2_ref, e0_ref, e1_ref, route_ref, w_ref, o_ref, *, final_norm):
    r = route_ref[...]
    tm = x2_ref.shape[0]
    ch = e0_ref.shape[0] // tm
    e0 = _unpack_bf16_pairs(_load_token_tiles(e0_ref, 0, tm, ch))
    e1 = _unpack_bf16_pairs(_load_token_tiles(e1_ref, 0, tm, ch))
    y = x2_ref[...] + r[:, ROUTE_GATE:ROUTE_GATE + 1] * e0 + r[:, ROUTE_GATE + 1:ROUTE_GATE + 2] * e1
    if final_norm:
        y = y * lax.rsqrt(jnp.mean(y * y, axis=-1, keepdims=True) + EPS) * w_ref[...]
    o_ref[...] = y


def _combine(x2, planes, route, norm_w, final_norm, tm=512):
    M, D = x2.shape
    return pl.pallas_call(
        functools.partial(_combine_kernel, final_norm=final_norm),
        grid=(M // tm,),
        in_specs=[pl.BlockSpec((tm, D), lambda i: (i, 0)),
                  pl.BlockSpec((tm * (D // 2 // LANES), LANES), lambda i: (i, 0)),
                  pl.BlockSpec((tm * (D // 2 // LANES), LANES), lambda i: (M // tm + i, 0)),
                  pl.BlockSpec((tm, LANES), lambda i: (i, 0)),
                  pl.BlockSpec((1, D), lambda i: (0, 0))],
        out_specs=pl.BlockSpec((tm, D), lambda i: (i, 0)),
        out_shape=jax.ShapeDtypeStruct((M, D), F32),
        compiler_params=_cparams(("parallel",), 48),
        name="combine",
    )(x2, planes, planes, route, norm_w.reshape(1, D))


def kernel(x, norm1_w, w_in, hy_conv_w, hy_conv_b, hy_filt_w1, hy_filt_b1, hy_filt_w2, hy_filt_b2, hy_filt_w3, hy_filt_b3, hy_filt_w4, hy_sin_freq, hy_skip, hy_norm_w, gdn_conv_w, gdn_a_log_f, gdn_a_log_b, gdn_dt_bias_f, gdn_dt_bias_b, gdn_norm_w, w_out, norm2_w, router_group_w, router_group_b, router_expert_w, router_expert_b, exp_w1, exp_w3, exp_w2, final_norm_w):
    B, L, D = x.shape
    M = B * L
    depth = w_in.shape[0]
    d_hy = hy_skip.shape[-1]
    H = gdn_a_log_f.shape[-1]
    Dh = gdn_norm_w.shape[-1]
    d_gdn = H * Dh
    n_main = 3 * d_hy + 4 * d_gdn
    G = router_group_w.shape[-1]
    E = router_expert_w.shape[-1]
    xf = x.reshape(M, D)
    for l in range(depth):
        p, pg = _inproj(xf, norm1_w[l], jnp.swapaxes(w_in[l], 0, 1), n_main, 4 * H)
        p3 = p.reshape(B, L, n_main)
        x0c, yconv = _hyena_branch(p3, hy_conv_w[l], hy_conv_b[l], hy_filt_w1[l], hy_filt_b1[l], hy_filt_w2[l],
                                   hy_filt_b2[l], hy_filt_w3[l], hy_filt_b3[l], hy_filt_w4[l], hy_sin_freq[l],
                                   hy_skip[l], d_hy)
        o_f, o_b = _gdn_branch(p3, pg, gdn_conv_w[l], gdn_a_log_f[l], gdn_a_log_b[l], gdn_dt_bias_f[l],
                               gdn_dt_bias_b[l], 3 * d_hy, H, Dh)
        wr = jnp.pad(jnp.concatenate([router_group_w[l], router_expert_w[l]], axis=1), ((0, 0), (0, LANES - G - E)))
        br = jnp.pad(jnp.concatenate([router_group_b[l], router_expert_b[l]]), (0, LANES - G - E)).reshape(1, LANES)
        x2, u, route, route_t, counts = _mix_route(
            yconv.reshape(M, d_hy), x0c.reshape(M, d_hy), o_f.reshape(M, d_gdn), o_b.reshape(M, d_gdn), p,
            3 * d_hy + 3 * d_gdn, hy_norm_w[l], gdn_norm_w[l], xf, w_out[l].astype(BF16), norm2_w[l], wr, br,
            H, Dh, G, E // G)
        slot_tgt, tile_seq, seq_expert, n_active_seq, tile_valid = _dispatch_tables(route_t, counts, E)
        planes = _experts(u, slot_tgt, tile_seq, seq_expert, n_active_seq, tile_valid, exp_w1[l], exp_w3[l],
                          exp_w2[l])
        xf = _combine(x2, planes, route, final_norm_w, final_norm=(l == depth - 1))
    return xf.reshape(B, L, D)
```

```python
import functools
import math

import jax
import jax.numpy as jnp
import numpy as np
from jax import lax
from jax.experimental import pallas as pl
from jax.experimental.pallas import tpu as pltpu

F32 = jnp.float32
BF16 = jnp.bfloat16
EPS = 1e-6
LANES = 128
SUBLANES = 8
VMEM_BYTES_V7X = 64 * 1024 * 1024
GDN_CHUNK = 64
FFT_N2 = 128
EXPERT_ROWS = 256
DECAY_TARGET = 1e-2
FAST_DECAY_PCT = 0.3
SLOW_DECAY_PCT = 1.5


def _cparams(sem, vmem_mb):
    return pltpu.CompilerParams(dimension_semantics=sem, vmem_limit_bytes=int(vmem_mb * 1024 * 1024))


def _dot(a, b):
    return jnp.dot(a.astype(BF16), b.astype(BF16), preferred_element_type=F32)


def _dot_nt(a, b):
    return lax.dot_general(a.astype(BF16), b.astype(BF16), (((1,), (1,)), ((), ())), preferred_element_type=F32)


def _dot_tn(a, b):
    return lax.dot_general(a.astype(BF16), b.astype(BF16), (((0,), (0,)), ((), ())), preferred_element_type=F32)


def _dot_hi(a, b):
    return jnp.dot(a, b, preferred_element_type=F32, precision=lax.Precision.HIGHEST)


def _silu(x):
    return x * jax.nn.sigmoid(x)


def _pack_bf16_pairs(x):
    c = x.shape[1] // 2
    lo = lax.bitcast_convert_type(x[:, :c].astype(BF16).astype(F32), jnp.uint32) >> 16
    hi = lax.bitcast_convert_type(x[:, c:].astype(BF16).astype(F32), jnp.uint32) & jnp.uint32(0xFFFF0000)
    return hi | lo


def _unpack_bf16_pairs(w):
    lo = lax.bitcast_convert_type(w << 16, F32)
    hi = lax.bitcast_convert_type(w & jnp.uint32(0xFFFF0000), F32)
    return jnp.concatenate([lo, hi], axis=1)


def _store_token_tiles(ref, row0, words):
    n, width = words.shape
    ch = width // LANES
    for c in range(ch):
        ref[pl.ds(row0 * ch + c, n, stride=ch), :] = words[:, c * LANES:(c + 1) * LANES]


def _load_token_tiles(ref, row0, n, ch):
    return jnp.concatenate([ref[pl.ds(row0 * ch + c, n, stride=ch), :] for c in range(ch)], axis=1)


def _inproj_kernel(x_ref, nw_ref, wt_ref, wgt_ref, p_ref, g_ref, h_scr):
    @pl.when(pl.program_id(1) == 0)
    def _():
        x = x_ref[...]
        h = x * lax.rsqrt(jnp.mean(x * x, axis=-1, keepdims=True) + EPS) * nw_ref[...]
        h_scr[...] = h.astype(BF16)
        g = _dot_nt(h, wgt_ref[...])
        g_ref[...] = jnp.concatenate([g, jnp.zeros((g.shape[0], LANES - g.shape[1]), F32)], axis=1)

    p_ref[...] = _dot_nt(h_scr[...], wt_ref[...]).astype(p_ref.dtype)


def _inproj(xf, norm_w, wt, n_main, n_gate, tm=2048, tn=512):
    M, D = xf.shape
    assert n_main % tn == 0 and M % tm == 0 and n_main % n_gate == 0 and n_gate % SUBLANES == 0
    return pl.pallas_call(
        _inproj_kernel,
        grid=(M // tm, n_main // tn),
        in_specs=[
            pl.BlockSpec((tm, D), lambda i, j: (i, 0), pipeline_mode=pl.Buffered(1)),
            pl.BlockSpec((1, D), lambda i, j: (0, 0)),
            pl.BlockSpec((tn, D), lambda i, j: (j, 0)),
            pl.BlockSpec((n_gate, D), lambda i, j: (n_main // n_gate, 0)),
        ],
        out_specs=[
            pl.BlockSpec((tm, tn), lambda i, j: (i, j)),
            pl.BlockSpec((tm, LANES), lambda i, j: (i, 0)),
        ],
        out_shape=[jax.ShapeDtypeStruct((M, n_main), BF16), jax.ShapeDtypeStruct((M, LANES), F32)],
        scratch_shapes=[pltpu.VMEM((tm, D), BF16)],
        compiler_params=_cparams(("parallel", "arbitrary"), 57),
        name="inproj",
    )(xf, norm_w.reshape(1, D), wt, wt)


def _conv3_rows(ref, r0, rows, w, n_rows):
    cur = ref[pl.ds(r0, rows), :].astype(F32)
    lo = jnp.maximum(r0 - HALO_ROWS, 0)
    hi = jnp.minimum(r0 + rows, n_rows - HALO_ROWS)
    prev_grp = ref[pl.ds(pl.multiple_of(lo, HALO_ROWS), HALO_ROWS), :].astype(F32)
    next_grp = ref[pl.ds(pl.multiple_of(hi, HALO_ROWS), HALO_ROWS), :].astype(F32)
    prev_row = jnp.where(r0 > 0, prev_grp[HALO_ROWS - 1:HALO_ROWS, :], 0.0)
    next_row = jnp.where(r0 + rows < n_rows, next_grp[0:1, :], 0.0)
    row = lax.broadcasted_iota(jnp.int32, cur.shape, 0)
    xm = jnp.where(row == 0, prev_row, pltpu.roll(cur, 1, 0))
    xp = jnp.where(row == rows - 1, next_row, pltpu.roll(cur, rows - 1, 0))
    return xm * w[0:1, :] + cur * w[1:2, :] + xp * w[2:3, :]


CONV_ROWS = 256
HALO_ROWS = 16


def _hy_pre_kernel(x0_ref, x1_ref, v_ref, w0_ref, w1_ref, w2_ref, b0_ref, b1_ref, b2_ref, x0c_ref, vg_ref):
    L = x0_ref.shape[0]
    w0, w1, w2 = w0_ref[...], w1_ref[...], w2_ref[...]
    b0, b1, b2 = b0_ref[...], b1_ref[...], b2_ref[...]

    def body(c, carry):
        r0 = pl.multiple_of(c * CONV_ROWS, CONV_ROWS)
        x0c_ref[pl.ds(r0, CONV_ROWS), :] = (_conv3_rows(x0_ref, r0, CONV_ROWS, w0, L) + b0).astype(x0c_ref.dtype)
        x1c = _conv3_rows(x1_ref, r0, CONV_ROWS, w1, L) + b1
        vc = _conv3_rows(v_ref, r0, CONV_ROWS, w2, L) + b2
        vg_ref[pl.ds(r0, CONV_ROWS), :] = (vc * x1c).astype(vg_ref.dtype)
        return carry

    lax.fori_loop(0, L // CONV_ROWS, body, 0, unroll=2)


def _hy_pre(p3, conv_w, conv_b, d_hy):
    B, L, _ = p3.shape
    nt = d_hy // LANES
    bias = conv_b.reshape(1, -1)
    pspec = lambda off: pl.BlockSpec((None, L, LANES), lambda b, c: (b, 0, c + off))
    wspec = lambda off: pl.BlockSpec((3, LANES), lambda b, c: (0, c + off))
    bspec = lambda off: pl.BlockSpec((1, LANES), lambda b, c: (0, c + off))
    ospec = pl.BlockSpec((None, L, LANES), lambda b, c: (b, 0, c))
    return pl.pallas_call(
        _hy_pre_kernel,
        grid=(B, nt),
        in_specs=[pspec(0), pspec(nt), pspec(2 * nt), wspec(0), wspec(nt), wspec(2 * nt),
                  bspec(0), bspec(nt), bspec(2 * nt)],
        out_specs=[ospec, ospec],
        out_shape=[jax.ShapeDtypeStruct((B, L, d_hy), BF16)] * 2,
        compiler_params=_cparams(("parallel", "parallel"), 40),
        name="hy_pre",
    )(p3, p3, p3, conv_w, conv_w, conv_w, bias, bias, bias)


def _gdn_pre_kernel(x_ref, w_ref, o_ref, *, n_heads, head_dim):
    L = x_ref.shape[0]
    w = w_ref[...]
    c = pl.program_id(1)
    q_scale = jnp.where(c < n_heads, head_dim ** -0.5, 1.0)
    is_qk = c < 2 * n_heads

    def body(i, carry):
        r0 = pl.multiple_of(i * CONV_ROWS, CONV_ROWS)
        y = _silu(_conv3_rows(x_ref, r0, CONV_ROWS, w, L))
        inv = lax.rsqrt(jnp.sum(y * y, axis=-1, keepdims=True) + EPS) * q_scale
        o_ref[pl.ds(r0, CONV_ROWS), :] = (y * jnp.where(is_qk, inv, 1.0)).astype(o_ref.dtype)
        return carry

    lax.fori_loop(0, L // CONV_ROWS, body, 0, unroll=2)


def _gdn_pre(p3, conv_w, col0, n_heads, head_dim):
    B, L, _ = p3.shape
    assert head_dim == LANES
    nt = 3 * n_heads
    off = col0 // LANES
    return pl.pallas_call(
        functools.partial(_gdn_pre_kernel, n_heads=n_heads, head_dim=head_dim),
        grid=(B, nt),
        in_specs=[pl.BlockSpec((None, L, LANES), lambda b, c: (b, 0, c + off)),
                  pl.BlockSpec((3, LANES), lambda b, c: (0, c))],
        out_specs=pl.BlockSpec((None, L, LANES), lambda b, c: (b, 0, c)),
        out_shape=jax.ShapeDtypeStruct((B, L, nt * LANES), BF16),
        compiler_params=_cparams(("parallel", "parallel"), 24),
        name="gdn_pre",
    )(p3, conv_w)


GATE_ROWS = 512


def _gdn_gates_kernel(pg_ref, alog_ref, dtb_ref, o_ref, *, n_heads):
    H = n_heads
    x = pg_ref[...]
    beta = jax.nn.sigmoid(x)
    z = x + dtb_ref[...]
    softplus = jnp.maximum(z, 0.0) + jnp.log1p(jnp.exp(-jnp.abs(z)))
    g = -jnp.exp(alog_ref[...]) * softplus
    pos = lax.broadcasted_iota(jnp.int32, x.shape, 0) & (GDN_CHUNK - 1)
    gc_f = g
    gc_b = g
    step = 1
    while step < GDN_CHUNK:
        gc_f = gc_f + jnp.where(pos >= step, pltpu.roll(gc_f, step, 0), 0.0)
        gc_b = gc_b + jnp.where(pos < GDN_CHUNK - step, pltpu.roll(gc_b, GATE_ROWS - step, 0), 0.0)
        step *= 2
    g_tot = pltpu.roll(gc_f + gc_b - g, 4 * H, 1)
    lane = lax.broadcasted_iota(jnp.int32, x.shape, 1)
    out = jnp.where(lane < 2 * H, beta,
                    jnp.where(lane < 3 * H, gc_f,
                              jnp.where(lane < 4 * H, gc_b,
                                        jnp.where((lane >= 6 * H) & (lane < 8 * H), g_tot, 0.0))))
    o_ref[...] = out


def _gdn_gates(pg, a_log_f, a_log_b, dt_bias_f, dt_bias_b, n_heads):
    M = pg.shape[0]
    H = n_heads
    assert 8 * H <= LANES
    pad = lambda a, b: jnp.concatenate([jnp.zeros((2 * H,), F32), a.astype(F32), b.astype(F32),
                                        jnp.zeros((LANES - 4 * H,), F32)]).reshape(1, LANES)
    return pl.pallas_call(
        functools.partial(_gdn_gates_kernel, n_heads=H),
        grid=(M // GATE_ROWS,),
        in_specs=[pl.BlockSpec((GATE_ROWS, LANES), lambda i: (i, 0)),
                  pl.BlockSpec((1, LANES), lambda i: (0, 0)),
                  pl.BlockSpec((1, LANES), lambda i: (0, 0))],
        out_specs=pl.BlockSpec((GATE_ROWS, LANES), lambda i: (i, 0)),
        out_shape=jax.ShapeDtypeStruct((M, LANES), F32),
        compiler_params=_cparams(("parallel",), 24),
        name="gdn_gates",
    )(pg, pad(a_log_f, a_log_b), pad(dt_bias_f, dt_bias_b))


def _delta_chunks(q, k, v, beta, gc_col, gc_row, gtot, state, lower):
    n = len(q)
    C = q[0].shape[0]
    D = k[0].shape[1]
    ii = lax.broadcasted_iota(jnp.int32, (C, C), 0)
    jj = lax.broadcasted_iota(jnp.int32, (C, C), 1)
    eye = jnp.where(ii == jj, 1.0, 0.0)
    incl = [(ii >= jj) if lo else (ii <= jj) for lo in lower]
    strict = [(ii > jj) if lo else (ii < jj) for lo in lower]
    rng = range(n)
    decay = [jnp.where(incl[i], jnp.exp(jnp.where(incl[i], gc_col[i] - gc_row[i], 0.0)), 0.0) for i in rng]
    kb = [k[i] * beta[i] for i in rng]
    kk = [_dot_nt(kb[i], k[i]) for i in rng]
    qk = [_dot_nt(q[i], k[i]) for i in rng]
    m = [jnp.where(strict[i], -(kk[i] * decay[i]), 0.0) for i in rng]
    r = [eye + m[i] for i in rng]
    m = [_dot(m[i], m[i]) for i in rng]
    for _ in range(int(math.log2(C)) - 2):
        rm = [_dot(jnp.concatenate([r[i], m[i]], axis=0), m[i]) for i in rng]
        r = [r[i] + rm[i][:C] for i in rng]
        m = [rm[i][C:] for i in rng]
    r = [r[i] + _dot(r[i], m[i]) for i in rng]
    eg = [jnp.exp(gc_col[i]) for i in rng]
    wu = [_dot(r[i], jnp.concatenate([kb[i] * eg[i], v[i] * beta[i]], axis=1)) for i in rng]
    ws = [_dot(jnp.concatenate([wu[i][:, :D], q[i] * eg[i]], axis=0), state[i]) for i in rng]
    v_new = [wu[i][:, D:] - ws[i][:C] for i in rng]
    qkm = [jnp.where(incl[i], qk[i] * decay[i], 0.0) for i in rng]
    out = [ws[i][C:] + _dot(qkm[i], v_new[i]) for i in rng]
    k_dec = [k[i] * jnp.exp(gtot[i] - gc_col[i]) for i in rng]
    new_state = [state[i] * jnp.exp(gtot[i][0:1, :]) + _dot_tn(k_dec[i], v_new[i]) for i in rng]
    return out, new_state


def _gdn_scan_kernel(qf_ref, kf_ref, vf_ref, qb_ref, kb_ref, vb_ref, gf_ref, gb_ref, rf_ref, rb_ref,
                     of_ref, ob_ref, s_scr, *, n_heads, head_dim):
    H, Dh = n_heads, head_dim
    B = qf_ref.shape[0]

    @pl.when(pl.program_id(0) == 0)
    def _():
        s_scr[...] = jnp.zeros_like(s_scr)

    col = lambda g, j: g[:, j:j + 1]
    sls = [slice(h * Dh, (h + 1) * Dh) for h in range(H)]
    q, k, v, beta, gc_col, gc_row, gtot, state = ([] for _ in range(8))
    for b in range(B):
        gf = gf_ref[b]
        gb = gb_ref[b]
        q += [qf_ref[b, :, sl] for sl in sls] + [qb_ref[b, :, sl] for sl in sls]
        k += [kf_ref[b, :, sl] for sl in sls] + [kb_ref[b, :, sl] for sl in sls]
        v += [vf_ref[b, :, sl] for sl in sls] + [vb_ref[b, :, sl] for sl in sls]
        beta += [col(gf, h) for h in range(H)] + [col(gb, H + h) for h in range(H)]
        gc_col += [col(gf, 2 * H + h) for h in range(H)] + [col(gb, 3 * H + h) for h in range(H)]
        gc_row += [rf_ref[b, h:h + 1, :] for h in range(H)] + [rb_ref[b, H + h:H + h + 1, :] for h in range(H)]
        gtot += [col(gf, 6 * H + h) for h in range(H)] + [col(gb, 7 * H + h) for h in range(H)]
        state += [s_scr[b, 0, h] for h in range(H)] + [s_scr[b, 1, h] for h in range(H)]
    out, new_state = _delta_chunks(q, k, v, beta, gc_col, gc_row, gtot, state, ([True] * H + [False] * H) * B)
    for b in range(B):
        for h in range(H):
            of_ref[b, :, sls[h]] = out[2 * H * b + h]
            ob_ref[b, :, sls[h]] = out[2 * H * b + H + h]
            s_scr[b, 0, h] = new_state[2 * H * b + h]
            s_scr[b, 1, h] = new_state[2 * H * b + H + h]


def _gdn_scan(qkv, gates, gates_row, n_heads, head_dim):
    B, L, _ = qkv.shape
    H, Dh = n_heads, head_dim
    d = H * Dh
    C = GDN_CHUNK
    N = L // C
    fwd = lambda col: pl.BlockSpec((B, C, d), lambda n: (0, n, col))
    bwd = lambda col: pl.BlockSpec((B, C, d), lambda n: (0, N - 1 - n, col))
    return pl.pallas_call(
        functools.partial(_gdn_scan_kernel, n_heads=H, head_dim=Dh),
        grid=(N,),
        in_specs=[fwd(0), fwd(1), fwd(2), bwd(0), bwd(1), bwd(2),
                  pl.BlockSpec((B, C, LANES), lambda n: (0, n, 0)),
                  pl.BlockSpec((B, C, LANES), lambda n: (0, N - 1 - n, 0)),
                  pl.BlockSpec((B, None, 2 * H, C), lambda n: (0, n, 0, 0)),
                  pl.BlockSpec((B, None, 2 * H, C), lambda n: (0, N - 1 - n, 0, 0))],
        out_specs=[pl.BlockSpec((B, C, d), lambda n: (0, n, 0)),
                   pl.BlockSpec((B, C, d), lambda n: (0, N - 1 - n, 0))],
        out_shape=[jax.ShapeDtypeStruct((B, L, d), F32)] * 2,
        scratch_shapes=[pltpu.VMEM((B, 2, H, Dh, Dh), F32)],
        compiler_params=_cparams(("arbitrary",), 32),
        name="gdn_scan",
    )(qkv, qkv, qkv, qkv, qkv, qkv, gates, gates, gates_row, gates_row)


def _gdn_branch(p3, pg, conv_w, a_log_f, a_log_b, dt_bias_f, dt_bias_b, col0, n_heads, head_dim):
    B, L, _ = p3.shape
    H = n_heads
    qkv = _gdn_pre(p3, conv_w, col0, H, head_dim)
    gates = _gdn_gates(pg, a_log_f, a_log_b, dt_bias_f, dt_bias_b, H).reshape(B, L, LANES)
    N = L // GDN_CHUNK
    gates_row = gates[..., 2 * H:4 * H].reshape(B, N, GDN_CHUNK, 2 * H).transpose(0, 1, 3, 2)
    return _gdn_scan(qkv, gates, gates_row, H, head_dim)


@functools.lru_cache(maxsize=None)
def _filter_positions(L, pos_emb_dim):
    n = 2 * L
    r = np.arange(n)
    k = np.where(r < L, r, np.where(r == L, 0, n - r)).astype(np.float64)
    t = k / (L - 1)
    bands = (pos_emb_dim - 1) // 2
    fb = np.linspace(1e-4, bands - 1, bands)
    ang = (2.0 * math.pi / L) * k[:, None] * fb[None, :]
    z = np.concatenate([t[:, None], np.cos(ang), -np.sin(ang)], axis=-1)
    return z.astype(np.float32)


@functools.lru_cache(maxsize=None)
def _decay_rates(d_hy):
    max_decay = math.log(DECAY_TARGET) / FAST_DECAY_PCT
    min_decay = math.log(DECAY_TARGET) / SLOW_DECAY_PCT
    return np.abs(np.linspace(min_decay, max_decay, d_hy)).astype(np.float32).reshape(1, d_hy)


def _filt_mlp_kernel(zt_ref, w1t_ref, b1_ref, w2t_ref, b2_ref, w3t_ref, b3_ref, fr_ref, o_ref):
    fr = fr_ref[...]
    h = jnp.sin(fr * (_dot_hi(w1t_ref[...], zt_ref[...]) + b1_ref[...]))
    h = jnp.sin(fr * (_dot_hi(w2t_ref[...], h) + b2_ref[...]))
    o_ref[...] = jnp.sin(fr * (_dot_hi(w3t_ref[...], h) + b3_ref[...]))


def _filt_mlp(zt, w1, b1, w2, b2, w3, b3, freq, tc=1024):
    pe, n = zt.shape
    fw = w1.shape[1]
    col = lambda a: a.reshape(-1, 1).astype(F32)
    full = lambda a: pl.BlockSpec(a.shape, lambda i: (0, 0))
    args = (zt, w1.T, col(b1), w2.T, col(b2), w3.T, col(b3), col(freq))
    return pl.pallas_call(
        _filt_mlp_kernel,
        grid=(n // tc,),
        in_specs=[pl.BlockSpec((pe, tc), lambda i: (0, i))] + [full(a) for a in args[1:]],
        out_specs=pl.BlockSpec((fw, tc), lambda i: (0, i)),
        out_shape=jax.ShapeDtypeStruct((fw, n), F32),
        compiler_params=_cparams(("parallel",), 24),
        name="filt_mlp",
    )(*args)


@functools.lru_cache(maxsize=None)
def _dft_tables(L):
    n = 2 * L
    N2 = FFT_N2
    N1 = n // N2
    N1h = N1 // 2
    j2 = np.arange(N2)[:, None, None]
    k1 = np.arange(N1)[None, :, None]

    def stage1(n_j1):
        j1 = np.arange(n_j1)[None, None, :]
        m = (k1 * (N2 * j1 + j2)) % n
        th = 2.0 * np.pi * m / n
        return np.cos(th), np.sin(th)

    c, s = stage1(N1h)
    t1 = np.concatenate([np.concatenate([c, s], axis=2), np.concatenate([-s, c], axis=2)], axis=1)
    c, s = stage1(N1)
    t1g = np.concatenate([c, -s], axis=1)
    c, s = stage1(N1h)
    ct, st = np.swapaxes(c, 1, 2) / n, np.swapaxes(s, 1, 2) / n
    t2 = np.concatenate([np.concatenate([ct, -st], axis=2), np.concatenate([st, ct], axis=2)], axis=1)
    a = np.arange(N2)
    th = 2.0 * np.pi * ((a[:, None] * a[None, :]) % N2) / N2
    c2, s2 = np.cos(th), np.sin(th)
    f2f = np.block([[c2, s2], [-s2, c2]])
    f2i = np.block([[c2, -s2], [s2, c2]])
    as_bf16 = lambda x: jnp.asarray(x, dtype=F32).astype(BF16)
    return dict(N1=N1, N2=N2, t1=t1.astype(np.float32), t1g=t1g.astype(np.float32), t2=t2.astype(np.float32),
                f2f=f2f.astype(np.float32), f2i=f2i.astype(np.float32))


FILT_ROWS = 512
FFT_UNROLL = 8
PITCH_PAD = 8
K1_GROUP = 2


def _filt_fft_kernel(h3_ref, w4f_ref, w4b_ref, delta_ref, t1g_ref, f2f_ref, hspec_ref, hb0_ref, g_scr, a_scr,
                     *, L, N1, N2):
    n = 2 * L
    gp = N2 + PITCH_PAD
    ap = 2 * N1 + PITCH_PAD
    delta = delta_ref[...]
    hb0_ref[...] = jnp.zeros_like(hb0_ref)

    def gen(c, carry):
        r0 = pl.multiple_of(c * FILT_ROWS, FILT_ROWS)
        row = r0 + lax.broadcasted_iota(jnp.int32, (FILT_ROWS, LANES), 0)
        lag = jnp.where(row < L, row, jnp.where(row == L, 0, n - row))
        window = jnp.exp(-(lag.astype(F32) * (1.0 / (L - 1))) * delta)
        w4 = jnp.where(r0 < L, w4f_ref[...], w4b_ref[...])
        g = _dot(h3_ref[pl.ds(r0, FILT_ROWS), :], w4) * window
        at_l = row == L
        hb0_ref[...] += jnp.sum(jnp.where(at_l, g, 0.0), axis=0, keepdims=True)
        g = jnp.where(at_l, 0.0, g)
        for q in range(FILT_ROWS // N2):
            dst = pl.multiple_of((c * (FILT_ROWS // N2) + q) * gp, SUBLANES)
            g_scr[pl.ds(dst, N2), :] = g[q * N2:(q + 1) * N2]
        return carry

    lax.fori_loop(0, n // FILT_ROWS, gen, 0)

    def stage1(j2, carry):
        x = g_scr[pl.ds(j2, N1, stride=gp), :]
        a_scr[pl.ds(pl.multiple_of(j2 * ap, SUBLANES), 2 * N1), :] = _dot(t1g_ref[j2], x)
        return carry

    lax.fori_loop(0, N2, stage1, 0, unroll=2 * FFT_UNROLL)

    def stage2(kp, carry):
        k1s = [kp * K1_GROUP + t for t in range(K1_GROUP)]
        x = jnp.concatenate([jnp.concatenate([a_scr[pl.ds(k1, N2, stride=ap), :],
                                              a_scr[pl.ds(N1 + k1, N2, stride=ap), :]], axis=0) for k1 in k1s], axis=1)
        z = _dot(f2f_ref[...], x).astype(hspec_ref.dtype)
        for t, k1 in enumerate(k1s):
            hspec_ref[pl.ds(pl.multiple_of(k1 * 2 * N2, 2 * N2), 2 * N2), :] = z[:, t * LANES:(t + 1) * LANES]
        return carry

    lax.fori_loop(0, N1 // K1_GROUP, stage2, 0, unroll=FFT_UNROLL // K1_GROUP)


def _filt_fft(h3, w4, L, d_hy):
    tb = _dft_tables(L)
    N1, N2 = tb["N1"], tb["N2"]
    n = 2 * L
    fw = h3.shape[1]
    nt = d_hy // LANES
    t1g = jnp.asarray(tb["t1g"]).astype(BF16)
    f2f = jnp.asarray(tb["f2f"]).astype(BF16)
    return pl.pallas_call(
        functools.partial(_filt_fft_kernel, L=L, N1=N1, N2=N2),
        grid=(nt,),
        in_specs=[pl.BlockSpec((n, fw), lambda c: (0, 0)),
                  pl.BlockSpec((fw, LANES), lambda c: (0, c)),
                  pl.BlockSpec((fw, LANES), lambda c: (0, c + nt)),
                  pl.BlockSpec((1, LANES), lambda c: (0, c)),
                  pl.BlockSpec(t1g.shape, lambda c: (0, 0, 0)),
                  pl.BlockSpec(f2f.shape, lambda c: (0, 0))],
        out_specs=[pl.BlockSpec((2 * n, LANES), lambda c: (0, c)),
                   pl.BlockSpec((SUBLANES, LANES), lambda c: (0, c))],
        out_shape=[jax.ShapeDtypeStruct((2 * n, d_hy), BF16), jax.ShapeDtypeStruct((SUBLANES, d_hy), F32)],
        scratch_shapes=[pltpu.VMEM((N1 * (N2 + PITCH_PAD), LANES), F32),
                        pltpu.VMEM((N2 * (2 * N1 + PITCH_PAD), LANES), F32)],
        compiler_params=_cparams(("parallel",), 48),
        name="filt_fft",
    )(h3, w4, w4, jnp.asarray(_decay_rates(d_hy)), t1g, f2f)


def _hy_conv_kernel(vg_ref, hspec_ref, skip_ref, hb0_ref, t1_ref, f2f_ref, f2i_ref, t2_ref, y_ref,
                    x_scr, a_scr, b_scr, *, N1, N2):
    N1h = N1 // 2
    xp = N2 + PITCH_PAD
    ap = 2 * N1 + PITCH_PAD
    bp = 2 * N2 + PITCH_PAD

    for b in range(2):
        for j1 in range(N1h):
            x_scr[b, pl.ds(j1 * xp, N2), :] = vg_ref[b, pl.ds(j1 * N2, N2), :].astype(F32)

    def stage1(j2, carry):
        x = jnp.concatenate([x_scr[0, pl.ds(j2, N1h, stride=xp), :],
                             x_scr[1, pl.ds(j2, N1h, stride=xp), :]], axis=0)
        a_scr[pl.ds(pl.multiple_of(j2 * ap, SUBLANES), 2 * N1), :] = _dot(t1_ref[j2], x)
        return carry

    lax.fori_loop(0, N2, stage1, 0, unroll=2 * FFT_UNROLL)

    def stage2(kp, carry):
        k1s = [kp * K1_GROUP + t for t in range(K1_GROUP)]
        x = jnp.concatenate([jnp.concatenate([a_scr[pl.ds(k1, N2, stride=ap), :],
                                              a_scr[pl.ds(N1 + k1, N2, stride=ap), :]], axis=0) for k1 in k1s], axis=1)
        z = _dot(f2f_ref[...], x)
        zr, zi = z[:N2], z[N2:]
        hs = [hspec_ref[pl.ds(pl.multiple_of(k1 * 2 * N2, 2 * N2), 2 * N2), :].astype(F32) for k1 in k1s]
        hr = jnp.concatenate([h[:N2] for h in hs], axis=1)
        hi = jnp.concatenate([h[N2:] for h in hs], axis=1)
        prod = jnp.concatenate([zr * hr - zi * hi, zr * hi + zi * hr], axis=0)
        b = _dot(f2i_ref[...], prod)
        for t, k1 in enumerate(k1s):
            b_scr[pl.ds(pl.multiple_of(k1 * bp, SUBLANES), 2 * N2), :] = b[:, t * LANES:(t + 1) * LANES]
        return carry

    lax.fori_loop(0, N1 // K1_GROUP, stage2, 0, unroll=FFT_UNROLL // K1_GROUP)

    skip = skip_ref[...] + hb0_ref[0:1, :]

    def stage3(j2, carry):
        b = jnp.concatenate([b_scr[pl.ds(j2, N1, stride=bp), :],
                             b_scr[pl.ds(N2 + j2, N1, stride=bp), :]], axis=0)
        y = _dot(t2_ref[j2], b)
        x_scr[0, pl.ds(j2, N1h, stride=xp), :] = y[:N1h] + x_scr[0, pl.ds(j2, N1h, stride=xp), :] * skip
        x_scr[1, pl.ds(j2, N1h, stride=xp), :] = y[N1h:] + x_scr[1, pl.ds(j2, N1h, stride=xp), :] * skip
        return carry

    lax.fori_loop(0, N2, stage3, 0, unroll=2 * FFT_UNROLL)

    for b in range(2):
        for j1 in range(N1h):
            y_ref[b, pl.ds(j1 * N2, N2), :] = x_scr[b, pl.ds(j1 * xp, N2), :]


def _hy_conv(vg, hspec, skip, hb0):
    B, L, d_hy = vg.shape
    assert B % 2 == 0
    tb = _dft_tables(L)
    N1, N2 = tb["N1"], tb["N2"]
    n = 2 * L
    nt = d_hy // LANES
    t1, t2 = (jnp.asarray(tb[k]).astype(BF16) for k in ("t1", "t2"))
    f2f, f2i = (jnp.asarray(tb[k]).astype(BF16) for k in ("f2f", "f2i"))
    const3 = lambda a: pl.BlockSpec(a.shape, lambda b, c: (0, 0, 0))
    const2 = lambda a: pl.BlockSpec(a.shape, lambda b, c: (0, 0))
    return pl.pallas_call(
        functools.partial(_hy_conv_kernel, N1=N1, N2=N2),
        grid=(B // 2, nt),
        in_specs=[pl.BlockSpec((2, L, LANES), lambda b, c: (b, 0, c)),
                  pl.BlockSpec((2 * n, LANES), lambda b, c: (0, c)),
                  pl.BlockSpec((1, LANES), lambda b, c: (0, c)),
                  pl.BlockSpec((SUBLANES, LANES), lambda b, c: (0, c)),
                  const3(t1), const2(f2f), const2(f2i), const3(t2)],
        out_specs=pl.BlockSpec((2, L, LANES), lambda b, c: (b, 0, c)),
        out_shape=jax.ShapeDtypeStruct((B, L, d_hy), F32),
        scratch_shapes=[pltpu.VMEM((2, (N1 // 2) * (N2 + PITCH_PAD), LANES), F32),
                        pltpu.VMEM((N2 * (2 * N1 + PITCH_PAD), LANES), F32),
                        pltpu.VMEM((N1 * (2 * N2 + PITCH_PAD), LANES), F32)],
        compiler_params=_cparams(("parallel", "parallel"), 58),
        name="hy_conv",
    )(vg, hspec, skip.reshape(1, d_hy).astype(F32), hb0, t1, f2f, f2i, t2)


def _hyena_branch(p3, conv_w, conv_b, fw1, fb1, fw2, fb2, fw3, fb3, fw4, freq, skip, d_hy):
    B, L, _ = p3.shape
    x0c, vg = _hy_pre(p3, conv_w, conv_b, d_hy)
    zt = jnp.asarray(_filter_positions(L, fw1.shape[0]).T)
    h3 = _filt_mlp(zt, fw1, fb1, fw2, fb2, fw3, fb3, freq).T
    hspec, hb0 = _filt_fft(h3, fw4, L, d_hy)
    return x0c, _hy_conv(vg, hspec, skip, hb0)


ROUTE_GATE, ROUTE_EXPERT, ROUTE_RANK = 0, 2, 4
MIX_ROWS = 256


def _mix_route_kernel(yc_ref, x0_ref, of_ref, ob_ref, z_ref, hnw_ref, gnw_ref, x_ref, wo_ref, n2w_ref, wr_ref, br_ref,
                      x2_ref, u_ref, route_ref, route_t_ref, cnt_ref, run_scr, *, n_heads, head_dim, n_groups,
                      per_group):
    tm = x_ref.shape[0]
    G, P = n_groups, per_group

    @pl.when(pl.program_id(0) == 0)
    def _():
        run_scr[...] = jnp.zeros_like(run_scr)

    run = run_scr[0:1, :]
    blocks = [pl.ds(s * MIX_ROWS, MIX_ROWS) for s in range(tm // MIX_ROWS)]
    us = []
    for blk, rs in enumerate(blocks):
        yh = yc_ref[rs, :] * x0_ref[rs, :].astype(F32)
        yh = yh * lax.rsqrt(jnp.mean(yh * yh, axis=-1, keepdims=True) + EPS) * hnw_ref[...]
        parts = [yh.astype(BF16)]
        for h in range(n_heads):
            sl = slice(h * head_dim, (h + 1) * head_dim)
            o = of_ref[rs, sl] + ob_ref[rs, sl]
            z = z_ref[rs, sl].astype(F32)
            o = o * lax.rsqrt(jnp.mean(o * o, axis=-1, keepdims=True) + EPS) * gnw_ref[...] * _silu(z)
            parts.append(o.astype(BF16))
        ymix = jnp.concatenate(parts, axis=-1)
        x2 = x_ref[rs, :] + jnp.dot(ymix, wo_ref[...], preferred_element_type=F32)
        x2_ref[rs, :] = x2
        u = x2 * lax.rsqrt(jnp.mean(x2 * x2, axis=-1, keepdims=True) + EPS) * n2w_ref[...]
        _store_token_tiles(u_ref, blk * MIX_ROWS, _pack_bf16_pairs(u))
        us.append(u)

    all_logits = [_dot(u, wr_ref[...]) + br_ref[...] for u in us]
    for s, (rs, logits) in enumerate(zip(blocks, all_logits)):
        lane = lax.broadcasted_iota(jnp.int32, logits.shape, 1)
        neg = jnp.float32(-jnp.inf)
        big = jnp.int32(4 * LANES)
        first = lambda hit: jnp.min(jnp.where(hit, lane, big), axis=-1, keepdims=True)
        gl = jnp.where(lane < G, logits, neg)
        gmax = jnp.max(gl, axis=-1, keepdims=True)
        gidx = first(gl == gmax)
        grp_gate = 1.0 / jnp.sum(jnp.exp(gl - gmax), axis=-1, keepdims=True)
        in_grp = (lane >= G) & (lane < G + G * P) & (((lane - G) // P) == gidx)
        ll = jnp.where(in_grp, logits, neg)
        m1 = jnp.max(ll, axis=-1, keepdims=True)
        i1 = first(ll == m1)
        denom = jnp.sum(jnp.exp(ll - m1), axis=-1, keepdims=True)
        ll2 = jnp.where(lane == i1, neg, ll)
        m2 = jnp.max(ll2, axis=-1, keepdims=True)
        i2 = first(ll2 == m2)
        p1 = 1.0 / denom
        p2 = jnp.exp(m2 - m1) / denom
        gate1 = grp_gate * (p1 / (p1 + p2))
        gate2 = grp_gate * (p2 / (p1 + p2))
        e1 = i1 - G
        e2 = i2 - G

        oh1 = jnp.where(lane == e1, 1.0, 0.0)
        oh2 = jnp.where(lane == e2, 1.0, 0.0)
        oh = oh1 + oh2
        ii = lax.broadcasted_iota(jnp.int32, (MIX_ROWS, MIX_ROWS), 0)
        jj = lax.broadcasted_iota(jnp.int32, (MIX_ROWS, MIX_ROWS), 1)
        before = _dot(jnp.where(ii > jj, 1.0, 0.0), oh) + run
        r1 = jnp.sum(oh1 * before, axis=-1, keepdims=True)
        r2 = jnp.sum(oh2 * before, axis=-1, keepdims=True)
        run = run + jnp.sum(oh, axis=0, keepdims=True)

        rec = jnp.where(lane == ROUTE_GATE, gate1, 0.0)
        rec = jnp.where(lane == ROUTE_GATE + 1, gate2, rec)
        rec = jnp.where(lane == ROUTE_EXPERT, e1.astype(F32), rec)
        rec = jnp.where(lane == ROUTE_EXPERT + 1, e2.astype(F32), rec)
        rec = jnp.where(lane == ROUTE_RANK, r1, rec)
        rec = jnp.where(lane == ROUTE_RANK + 1, r2, rec)
        route_ref[rs, :] = rec
        route_t_ref[:, s * MIX_ROWS:(s + 1) * MIX_ROWS] = jnp.transpose(rec)[:SUBLANES, :]

    run_scr[...] = jnp.broadcast_to(run, run_scr.shape)
    cnt_ref[...] = run_scr[...]


def _mix_route(yconv, x0c, o_f, o_b, p, z_col, hy_norm_w, gdn_norm_w, xf, w_out_bf16, norm2_w, wr, br,
               n_heads, head_dim, n_groups, per_group, tm=512):
    M, D = xf.shape
    d_hy = yconv.shape[1]
    d_gdn = o_f.shape[1]
    assert z_col % d_gdn == 0 and n_groups * (per_group + 1) <= LANES
    zb = z_col // d_gdn
    row = lambda i: (i, 0)
    const = lambda i: (0, 0)
    kern = functools.partial(_mix_route_kernel, n_heads=n_heads, head_dim=head_dim,
                             n_groups=n_groups, per_group=per_group)
    return pl.pallas_call(
        kern,
        grid=(M // tm,),
        in_specs=[pl.BlockSpec((tm, d_hy), row), pl.BlockSpec((tm, d_hy), row),
                  pl.BlockSpec((tm, d_gdn), row), pl.BlockSpec((tm, d_gdn), row),
                  pl.BlockSpec((tm, d_gdn), lambda i: (i, zb)),
                  pl.BlockSpec((1, d_hy), const), pl.BlockSpec((1, head_dim), const),
                  pl.BlockSpec((tm, D), row), pl.BlockSpec(w_out_bf16.shape, const, pipeline_mode=pl.Buffered(1)),
                  pl.BlockSpec((1, D), const), pl.BlockSpec((D, LANES), const), pl.BlockSpec((1, LANES), const)],
        out_specs=[pl.BlockSpec((tm, D), row), pl.BlockSpec((tm * (D // 2 // LANES), LANES), row),
                   pl.BlockSpec((tm, LANES), row), pl.BlockSpec((SUBLANES, tm), lambda i: (0, i)),
                   pl.BlockSpec((SUBLANES, LANES), const)],
        out_shape=[jax.ShapeDtypeStruct((M, D), F32), jax.ShapeDtypeStruct((M * (D // 2 // LANES), LANES), jnp.uint32),
                   jax.ShapeDtypeStruct((M, LANES), F32), jax.ShapeDtypeStruct((SUBLANES, M), F32),
                   jax.ShapeDtypeStruct((SUBLANES, LANES), F32)],
        scratch_shapes=[pltpu.VMEM((SUBLANES, LANES), F32)],
        compiler_params=_cparams(("arbitrary",), 58),
        name="mix_route",
    )(yconv, x0c, o_f, o_b, p, hy_norm_w.reshape(1, d_hy), gdn_norm_w.reshape(1, head_dim), xf, w_out_bf16,
      norm2_w.reshape(1, D), wr, br)


WAIT_GROUP = 8


def _experts_kernel(ts_ref, se_ref, nn_ref, nv_ref, tgt_nxt_ref, tgt_ref, tgt_prv_ref, u_hbm, w1_hbm, w3_hbm, w2_hbm,
                    out_hbm, x0, x1, y0, y1, wf1, wf3, wf2, w1b, w3b, w2b, gsem, ssem, wsem, *, n_tokens):
    i = pl.program_id(0)
    na = nn_ref[0]
    ns = nn_ref[1]
    CH = w1b.shape[0] // 2 // LANES
    TB = x0.shape[0] // CH
    W_SLOTS = wf1.shape[0]
    last_tile = pl.num_programs(0) - 1
    nv_cur = nv_ref[i]
    nv_nxt = jnp.where(i + 1 < na, nv_ref[jnp.minimum(i + 1, last_tile)], 0)
    nv_prv = jnp.where(i > 0, nv_ref[jnp.maximum(i - 1, 0)], 0)
    nv_pp = jnp.where(i > 1, nv_ref[jnp.maximum(i - 2, 0)], 0)

    def weight_copies(s):
        slot = lax.rem(s, W_SLOTS)
        e = se_ref[s]
        return [pltpu.make_async_copy(w_hbm.at[e], wf.at[slot], wsem.at[slot])
                for w_hbm, wf in ((w1_hbm, wf1), (w3_hbm, wf3), (w2_hbm, wf2))]

    def token_of(v):
        if n_tokens & (n_tokens - 1) == 0:
            return v & (n_tokens - 1)
        return lax.rem(v, n_tokens)

    def tile_of(row):
        return pl.ds(pl.multiple_of(row * CH, CH), CH)

    def gather(tgt, xbuf, s, r):
        return pltpu.make_async_copy(u_hbm.at[tile_of(token_of(tgt[r]))], xbuf.at[tile_of(r)], gsem.at[s])

    def scatter(ybuf, s, r, dst):
        return pltpu.make_async_copy(ybuf.at[tile_of(r)], out_hbm.at[tile_of(dst)], ssem.at[s])

    def wait_tokens(count, make):
        groups = count // WAIT_GROUP

        def grp(g, carry):
            make(WAIT_GROUP).wait()
            return carry

        lax.fori_loop(0, groups, grp, 0)

        def one(g, carry):
            make(1).wait()
            return carry

        lax.fori_loop(groups * WAIT_GROUP, count, one, 0)

    def wait_gather(xbuf, s, count):
        wait_tokens(count, lambda n: pltpu.make_async_copy(u_hbm.at[pl.ds(0, n * CH)], xbuf.at[pl.ds(0, n * CH)],
                                                             gsem.at[s]))

    def wait_scatter(ybuf, s, count):
        wait_tokens(count, lambda n: pltpu.make_async_copy(ybuf.at[pl.ds(0, n * CH)], out_hbm.at[pl.ds(0, n * CH)],
                                                             ssem.at[s]))

    def each_row(count, fn):
        def body(r, carry):
            fn(r)
            return carry
        lax.fori_loop(0, count, body, 0)

    def step(p):
        q = 1 - p
        x_cur, x_nxt = (x0, x1) if p == 0 else (x1, x0)
        y_cur, y_prv = (y0, y1) if p == 0 else (y1, y0)
        wait_gather(x_cur, p, nv_cur)
        wait_scatter(y_cur, p, nv_pp)

        x = _unpack_bf16_pairs(_load_token_tiles(x_cur, 0, TB, CH)).astype(BF16)
        for r in range(TB):
            @pl.when(r < nv_nxt)
            def _():
                gather(tgt_nxt_ref, x_nxt, q, r).start()
        for r in range(TB):
            @pl.when(r < nv_prv)
            def _():
                scatter(y_prv, q, r, tgt_prv_ref[r]).start()
        h = _silu(jnp.dot(x, w1b[...], preferred_element_type=F32)) * jnp.dot(x, w3b[...], preferred_element_type=F32)
        _store_token_tiles(y_cur, 0, _pack_bf16_pairs(jnp.dot(h.astype(BF16), w2b[...], preferred_element_type=F32)))

        @pl.when(i == na - 1)
        def _():
            wait_scatter(y_prv, q, nv_prv)
            each_row(nv_cur, lambda r: scatter(y_cur, p, r, tgt_ref[r]).start())
            wait_scatter(y_cur, p, nv_cur)

    @pl.when(i < na)
    def _():
        @pl.when(i == 0)
        def _():
            x0[...] = jnp.zeros_like(x0)
            x1[...] = jnp.zeros_like(x1)
            each_row(nv_cur, lambda r: gather(tgt_ref, x0, 0, r).start())
            for s0 in range(W_SLOTS - 1):
                @pl.when(s0 < ns)
                def _():
                    for c in weight_copies(s0):
                        c.start(priority=WEIGHT_DMA_PRIORITY)

        s = ts_ref[i]

        @pl.when((i == 0) | (s != ts_ref[jnp.maximum(i - 1, 0)]))
        def _():
            @pl.when(s + W_SLOTS - 1 < ns)
            def _():
                for c in weight_copies(s + W_SLOTS - 1):
                    c.start(priority=WEIGHT_DMA_PRIORITY)

            for c in weight_copies(s):
                c.wait()
            slot = lax.rem(s, W_SLOTS)
            w1b[...] = wf1[slot].astype(BF16)
            w3b[...] = wf3[slot].astype(BF16)
            w2b[...] = wf2[slot].astype(BF16)

        parity = lax.rem(i, 2)

        @pl.when(parity == 0)
        def _():
            step(0)

        @pl.when(parity == 1)
        def _():
            step(1)


WEIGHT_SLOTS = 3
WEIGHT_DMA_PRIORITY = 1


def _experts(u, slot_tgt, tile_seq, seq_expert, n_active_seq, tile_valid, w1, w3, w2):
    E, D, de = w1.shape
    CH = D // 2 // LANES
    assert CH % SUBLANES == 0
    T = u.shape[0] // CH
    TB = EXPERT_ROWS
    n_tiles = slot_tgt.shape[0] // TB
    table = lambda f: pl.BlockSpec((TB,), lambda i, ts, se, nn, nv: (f(i),), memory_space=pltpu.SMEM)
    hbm = pl.BlockSpec(memory_space=pl.ANY)
    grid_spec = pltpu.PrefetchScalarGridSpec(
        num_scalar_prefetch=4,
        grid=(n_tiles,),
        in_specs=[table(lambda i: jnp.minimum(i + 1, n_tiles - 1)), table(lambda i: i),
                  table(lambda i: jnp.maximum(i - 1, 0)), hbm, hbm, hbm, hbm],
        out_specs=hbm,
        scratch_shapes=[pltpu.VMEM((TB * CH, LANES), jnp.uint32)] * 4
                       + [pltpu.VMEM((WEIGHT_SLOTS, D, de), F32), pltpu.VMEM((WEIGHT_SLOTS, D, de), F32),
                          pltpu.VMEM((WEIGHT_SLOTS, de, D), F32),
                          pltpu.VMEM((D, de), BF16), pltpu.VMEM((D, de), BF16), pltpu.VMEM((de, D), BF16),
                          pltpu.SemaphoreType.DMA((2,)), pltpu.SemaphoreType.DMA((2,)),
                          pltpu.SemaphoreType.DMA((WEIGHT_SLOTS,))],
    )
    return pl.pallas_call(
        functools.partial(_experts_kernel, n_tokens=T),
        grid_spec=grid_spec,
        out_shape=jax.ShapeDtypeStruct((2 * T * CH, LANES), jnp.uint32),
        compiler_params=_cparams(("arbitrary",), 58),
        name="experts",
    )(tile_seq, seq_expert, n_active_seq, tile_valid, slot_tgt, slot_tgt, slot_tgt, u, w1, w3, w2)


def _slot_table_kernel(dest_ref, init_hbm, o_ref, sem):
    fill = pltpu.make_async_copy(init_hbm, o_ref, sem)
    fill.start()
    fill.wait()

    def put(a, carry):
        o_ref[dest_ref[a]] = a
        return carry

    lax.fori_loop(0, dest_ref.shape[0], put, 0, unroll=16)


def _slot_table(dest, init):
    smem = pl.BlockSpec(memory_space=pltpu.SMEM)
    return pl.pallas_call(
        _slot_table_kernel,
        in_specs=[smem, pl.BlockSpec(memory_space=pl.ANY)],
        out_specs=smem,
        out_shape=jax.ShapeDtypeStruct(init.shape, jnp.int32),
        scratch_shapes=[pltpu.SemaphoreType.DMA],
        name="slot_table",
    )(dest, init)


def _dispatch_tables(route_t, counts, n_experts):
    T = route_t.shape[1]
    TB = EXPERT_ROWS
    e = route_t[ROUTE_EXPERT:ROUTE_EXPERT + 2].astype(jnp.int32)
    rank = route_t[ROUTE_RANK:ROUTE_RANK + 2].astype(jnp.int32)
    cnt = counts[0, :n_experts].astype(jnp.int32)
    padded = (cnt + TB - 1) // TB * TB
    pad_end = jnp.cumsum(padded)
    pad_start = pad_end - padded
    ids = jnp.arange(n_experts, dtype=jnp.int32)[:, None, None]
    start_of = jnp.sum(jnp.where(e[None] == ids, pad_start[:, None, None], 0), axis=0)
    dest = start_of + rank
    n_tiles = -(-(2 * T + n_experts * (TB - 1)) // TB)
    slot_tgt = _slot_table(dest.reshape(-1), jnp.zeros((n_tiles * TB,), jnp.int32))
    start = jnp.arange(n_tiles, dtype=jnp.int32) * TB
    tile_expert = jnp.sum(start[:, None] >= pad_end[None, :], axis=1)
    tile_expert = jnp.minimum(tile_expert, n_experts - 1).astype(jnp.int32)
    owns = cnt > 0
    pos = jnp.cumsum(owns.astype(jnp.int32)) - 1
    experts = jnp.arange(n_experts, dtype=jnp.int32)
    seq_expert = jnp.sum(jnp.where(owns[None, :] & (pos[None, :] == experts[:, None]), experts[None, :], 0), axis=1)
    tile_seq = jnp.sum(jnp.where(tile_expert[:, None] == experts[None, :], pos[None, :], 0), axis=1)
    n_active_seq = jnp.stack([pad_end[-1] // TB, jnp.sum(owns.astype(jnp.int32))]).astype(jnp.int32)
    pick = tile_expert[:, None] == experts[None, :]
    in_expert = start - jnp.sum(jnp.where(pick, pad_start[None, :], 0), axis=1)
    tile_cnt = jnp.sum(jnp.where(pick, cnt[None, :], 0), axis=1)
    tile_valid = jnp.where(start < pad_end[-1], jnp.clip(tile_cnt - in_expert, 0, TB), 0).astype(jnp.int32)
    return slot_tgt, tile_seq.astype(jnp.int32), seq_expert.astype(jnp.int32), n_active_seq, tile_valid


def _combine_kernel(x2_ref, e0_ref, e1_ref, route_ref, w_ref, o_ref, *, final_norm):
    r = route_ref[...]
    tm = x2_ref.shape[0]
    ch = e0_ref.shape[0] // tm
    e0 = _unpack_bf16_pairs(_load_token_tiles(e0_ref, 0, tm, ch))
    e1 = _unpack_bf16_pairs(_load_token_tiles(e1_ref, 0, tm, ch))
    y = x2_ref[...] + r[:, ROUTE_GATE:ROUTE_GATE + 1] * e0 + r[:, ROUTE_GATE + 1:ROUTE_GATE + 2] * e1
    if final_norm:
        y = y * lax.rsqrt(jnp.mean(y * y, axis=-1, keepdims=True) + EPS) * w_ref[...]
    o_ref[...] = y


def _combine(x2, planes, route, norm_w, final_norm, tm=512):
    M, D = x2.shape
    return pl.pallas_call(
        functools.partial(_combine_kernel, final_norm=final_norm),
        grid=(M // tm,),
        in_specs=[pl.BlockSpec((tm, D), lambda i: (i, 0)),
                  pl.BlockSpec((tm * (D // 2 // LANES), LANES), lambda i: (i, 0)),
                  pl.BlockSpec((tm * (D // 2 // LANES), LANES), lambda i: (M // tm + i, 0)),
                  pl.BlockSpec((tm, LANES), lambda i: (i, 0)),
                  pl.BlockSpec((1, D), lambda i: (0, 0))],
        out_specs=pl.BlockSpec((tm, D), lambda i: (i, 0)),
        out_shape=jax.ShapeDtypeStruct((M, D), F32),
        compiler_params=_cparams(("parallel",), 48),
        name="combine",
    )(x2, planes, planes, route, norm_w.reshape(1, D))


def kernel(x, norm1_w, w_in, hy_conv_w, hy_conv_b, hy_filt_w1, hy_filt_b1, hy_filt_w2, hy_filt_b2, hy_filt_w3, hy_filt_b3, hy_filt_w4, hy_sin_freq, hy_skip, hy_norm_w, gdn_conv_w, gdn_a_log_f, gdn_a_log_b, gdn_dt_bias_f, gdn_dt_bias_b, gdn_norm_w, w_out, norm2_w, router_group_w, router_group_b, router_expert_w, router_expert_b, exp_w1, exp_w3, exp_w2, final_norm_w):
    B, L, D = x.shape
    M = B * L
    depth = w_in.shape[0]
    d_hy = hy_skip.shape[-1]
    H = gdn_a_log_f.shape[-1]
    Dh = gdn_norm_w.shape[-1]
    d_gdn = H * Dh
    n_main = 3 * d_hy + 4 * d_gdn
    G = router_group_w.shape[-1]
    E = router_expert_w.shape[-1]
    xf = x.reshape(M, D)
    for l in range(depth):
        p, pg = _inproj(xf, norm1_w[l], jnp.swapaxes(w_in[l], 0, 1), n_main, 4 * H)
        p3 = p.reshape(B, L, n_main)
        x0c, yconv = _hyena_branch(p3, hy_conv_w[l], hy_conv_b[l], hy_filt_w1[l], hy_filt_b1[l], hy_filt_w2[l],
                                   hy_filt_b2[l], hy_filt_w3[l], hy_filt_b3[l], hy_filt_w4[l], hy_sin_freq[l],
                                   hy_skip[l], d_hy)
        o_f, o_b = _gdn_branch(p3, pg, gdn_conv_w[l], gdn_a_log_f[l], gdn_a_log_b[l], gdn_dt_bias_f[l],
                               gdn_dt_bias_b[l], 3 * d_hy, H, Dh)
        wr = jnp.pad(jnp.concatenate([router_group_w[l], router_expert_w[l]], axis=1), ((0, 0), (0, LANES - G - E)))
        br = jnp.pad(jnp.concatenate([router_group_b[l], router_expert_b[l]]), (0, LANES - G - E)).reshape(1, LANES)
        x2, u, route, route_t, counts = _mix_route(
            yconv.reshape(M, d_hy), x0c.reshape(M, d_hy), o_f.reshape(M, d_gdn), o_b.reshape(M, d_gdn), p,
            3 * d_hy + 3 * d_gdn, hy_norm_w[l], gdn_norm_w[l], xf, w_out[l].astype(BF16), norm2_w[l], wr, br,
            H, Dh, G, E // G)
        slot_tgt, tile_seq, seq_expert, n_active_seq, tile_valid = _dispatch_tables(route_t, counts, E)
        planes = _experts(u, slot_tgt, tile_seq, seq_expert, n_active_seq, tile_valid, exp_w1[l], exp_w3[l],
                          exp_w2[l])
        xf = _combine(x2, planes, route, final_norm_w, final_norm=(l == depth - 1))
    return xf.reshape(B, L, D)
```

```python
import functools
import math

import jax
import jax.numpy as jnp
import numpy as np
from jax import lax
from jax.experimental import pallas as pl
from jax.experimental.pallas import tpu as pltpu

F32 = jnp.float32
BF16 = jnp.bfloat16
EPS = 1e-6
LANES = 128
SUBLANES = 8
VMEM_BYTES_V7X = 64 * 1024 * 1024
GDN_CHUNK = 64
FFT_N2 = 128
EXPERT_ROWS = 256
DECAY_TARGET = 1e-2
FAST_DECAY_PCT = 0.3
SLOW_DECAY_PCT = 1.5


def _cparams(sem, vmem_mb):
    return pltpu.CompilerParams(dimension_semantics=sem, vmem_limit_bytes=int(vmem_mb * 1024 * 1024))


def _dot(a, b):
    return jnp.dot(a.astype(BF16), b.astype(BF16), preferred_element_type=F32)


def _dot_nt(a, b):
    return lax.dot_general(a.astype(BF16), b.astype(BF16), (((1,), (1,)), ((), ())), preferred_element_type=F32)


def _dot_tn(a, b):
    return lax.dot_general(a.astype(BF16), b.astype(BF16), (((0,), (0,)), ((), ())), preferred_element_type=F32)


def _dot_hi(a, b):
    return jnp.dot(a, b, preferred_element_type=F32, precision=lax.Precision.HIGHEST)


def _silu(x):
    return x * jax.nn.sigmoid(x)


def _pack_bf16_pairs(x):
    c = x.shape[1] // 2
    lo = lax.bitcast_convert_type(x[:, :c].astype(BF16).astype(F32), jnp.uint32) >> 16
    hi = lax.bitcast_convert_type(x[:, c:].astype(BF16).astype(F32), jnp.uint32) & jnp.uint32(0xFFFF0000)
    return hi | lo


def _unpack_bf16_pairs(w):
    lo = lax.bitcast_convert_type(w << 16, F32)
    hi = lax.bitcast_convert_type(w & jnp.uint32(0xFFFF0000), F32)
    return jnp.concatenate([lo, hi], axis=1)


def _store_token_tiles(ref, row0, words):
    n, width = words.shape
    ch = width // LANES
    for c in range(ch):
        ref[pl.ds(row0 * ch + c, n, stride=ch), :] = words[:, c * LANES:(c + 1) * LANES]


def _load_token_tiles(ref, row0, n, ch):
    return jnp.concatenate([ref[pl.ds(row0 * ch + c, n, stride=ch), :] for c in range(ch)], axis=1)


def _inproj_kernel(x_hbm, nw_ref, wt_ref, wgt_ref, p_ref, g_ref, h_scr, x_scr, x_sem):
    i = pl.program_id(0)
    j = pl.program_id(1)
    tm = x_scr.shape[0]

    def x_copy(tile):
        return pltpu.make_async_copy(x_hbm.at[pl.ds(pl.multiple_of(tile * tm, tm), tm)], x_scr, x_sem)

    @pl.when((i == 0) & (j == 0))
    def _():
        x_copy(0).start()

    @pl.when((j == 1) & (i + 1 < pl.num_programs(0)))
    def _():
        x_copy(i + 1).start()

    @pl.when(j == 0)
    def _():
        x_copy(i).wait()
        x = x_scr[...]
        h = x * lax.rsqrt(jnp.mean(x * x, axis=-1, keepdims=True) + EPS) * nw_ref[...]
        h_scr[...] = h.astype(BF16)
        g = _dot_nt(h, wgt_ref[...])
        g_ref[...] = jnp.concatenate([g, jnp.zeros((g.shape[0], LANES - g.shape[1]), F32)], axis=1)

    p_ref[...] = _dot_nt(h_scr[...], wt_ref[...]).astype(p_ref.dtype)


def _inproj(xf, norm_w, wt, n_main, n_gate, tm=2048, tn=512):
    M, D = xf.shape
    assert n_main % tn == 0 and M % tm == 0 and n_main % n_gate == 0 and n_gate % SUBLANES == 0
    assert n_main // tn >= 2
    return pl.pallas_call(
        _inproj_kernel,
        grid=(M // tm, n_main // tn),
        in_specs=[
            pl.BlockSpec(memory_space=pl.ANY),
            pl.BlockSpec((1, D), lambda i, j: (0, 0)),
            pl.BlockSpec((tn, D), lambda i, j: (j, 0)),
            pl.BlockSpec((n_gate, D), lambda i, j: (n_main // n_gate, 0)),
        ],
        out_specs=[
            pl.BlockSpec((tm, tn), lambda i, j: (i, j)),
            pl.BlockSpec((tm, LANES), lambda i, j: (i, 0)),
        ],
        out_shape=[jax.ShapeDtypeStruct((M, n_main), BF16), jax.ShapeDtypeStruct((M, LANES), F32)],
        scratch_shapes=[pltpu.VMEM((tm, D), BF16), pltpu.VMEM((tm, D), F32), pltpu.SemaphoreType.DMA],
        compiler_params=_cparams(("arbitrary", "arbitrary"), 57),
        name="inproj",
    )(xf, norm_w.reshape(1, D), wt, wt)


def _conv3_rows(ref, r0, rows, w, n_rows):
    cur = ref[pl.ds(r0, rows), :].astype(F32)
    lo = jnp.maximum(r0 - HALO_ROWS, 0)
    hi = jnp.minimum(r0 + rows, n_rows - HALO_ROWS)
    prev_grp = ref[pl.ds(pl.multiple_of(lo, HALO_ROWS), HALO_ROWS), :].astype(F32)
    next_grp = ref[pl.ds(pl.multiple_of(hi, HALO_ROWS), HALO_ROWS), :].astype(F32)
    prev_row = jnp.where(r0 > 0, prev_grp[HALO_ROWS - 1:HALO_ROWS, :], 0.0)
    next_row = jnp.where(r0 + rows < n_rows, next_grp[0:1, :], 0.0)
    row = lax.broadcasted_iota(jnp.int32, cur.shape, 0)
    xm = jnp.where(row == 0, prev_row, pltpu.roll(cur, 1, 0))
    xp = jnp.where(row == rows - 1, next_row, pltpu.roll(cur, rows - 1, 0))
    return xm * w[0:1, :] + cur * w[1:2, :] + xp * w[2:3, :]


CONV_ROWS = 256
HALO_ROWS = 16


def _hy_pre_kernel(x0_ref, x1_ref, v_ref, w0_ref, w1_ref, w2_ref, b0_ref, b1_ref, b2_ref, x0c_ref, vg_ref):
    L = x0_ref.shape[0]
    w0, w1, w2 = w0_ref[...], w1_ref[...], w2_ref[...]
    b0, b1, b2 = b0_ref[...], b1_ref[...], b2_ref[...]

    def body(c, carry):
        r0 = pl.multiple_of(c * CONV_ROWS, CONV_ROWS)
        x0c_ref[pl.ds(r0, CONV_ROWS), :] = (_conv3_rows(x0_ref, r0, CONV_ROWS, w0, L) + b0).astype(x0c_ref.dtype)
        x1c = _conv3_rows(x1_ref, r0, CONV_ROWS, w1, L) + b1
        vc = _conv3_rows(v_ref, r0, CONV_ROWS, w2, L) + b2
        vg_ref[pl.ds(r0, CONV_ROWS), :] = (vc * x1c).astype(vg_ref.dtype)
        return carry

    lax.fori_loop(0, L // CONV_ROWS, body, 0, unroll=2)


def _hy_pre(p3, conv_w, conv_b, d_hy):
    B, L, _ = p3.shape
    nt = d_hy // LANES
    bias = conv_b.reshape(1, -1)
    pspec = lambda off: pl.BlockSpec((None, L, LANES), lambda b, c: (b, 0, c + off))
    wspec = lambda off: pl.BlockSpec((3, LANES), lambda b, c: (0, c + off))
    bspec = lambda off: pl.BlockSpec((1, LANES), lambda b, c: (0, c + off))
    ospec = pl.BlockSpec((None, L, LANES), lambda b, c: (b, 0, c))
    return pl.pallas_call(
        _hy_pre_kernel,
        grid=(B, nt),
        in_specs=[pspec(0), pspec(nt), pspec(2 * nt), wspec(0), wspec(nt), wspec(2 * nt),
                  bspec(0), bspec(nt), bspec(2 * nt)],
        out_specs=[ospec, ospec],
        out_shape=[jax.ShapeDtypeStruct((B, L, d_hy), BF16)] * 2,
        compiler_params=_cparams(("parallel", "parallel"), 40),
        name="hy_pre",
    )(p3, p3, p3, conv_w, conv_w, conv_w, bias, bias, bias)


def _gdn_pre_kernel(x_ref, w_ref, o_ref, *, n_heads, head_dim):
    L = x_ref.shape[0]
    w = w_ref[...]
    c = pl.program_id(1)
    q_scale = jnp.where(c < n_heads, head_dim ** -0.5, 1.0)
    is_qk = c < 2 * n_heads

    def body(i, carry):
        r0 = pl.multiple_of(i * CONV_ROWS, CONV_ROWS)
        y = _silu(_conv3_rows(x_ref, r0, CONV_ROWS, w, L))
        inv = lax.rsqrt(jnp.sum(y * y, axis=-1, keepdims=True) + EPS) * q_scale
        o_ref[pl.ds(r0, CONV_ROWS), :] = (y * jnp.where(is_qk, inv, 1.0)).astype(o_ref.dtype)
        return carry

    lax.fori_loop(0, L // CONV_ROWS, body, 0, unroll=2)


def _gdn_pre(p3, conv_w, col0, n_heads, head_dim):
    B, L, _ = p3.shape
    assert head_dim == LANES
    nt = 3 * n_heads
    off = col0 // LANES
    return pl.pallas_call(
        functools.partial(_gdn_pre_kernel, n_heads=n_heads, head_dim=head_dim),
        grid=(B, nt),
        in_specs=[pl.BlockSpec((None, L, LANES), lambda b, c: (b, 0, c + off)),
                  pl.BlockSpec((3, LANES), lambda b, c: (0, c))],
        out_specs=pl.BlockSpec((None, L, LANES), lambda b, c: (b, 0, c)),
        out_shape=jax.ShapeDtypeStruct((B, L, nt * LANES), BF16),
        compiler_params=_cparams(("parallel", "parallel"), 24),
        name="gdn_pre",
    )(p3, conv_w)


GATE_ROWS = 512


def _gdn_gates_kernel(pg_ref, alog_ref, dtb_ref, o_ref, *, n_heads):
    H = n_heads
    x = pg_ref[...]
    beta = jax.nn.sigmoid(x)
    z = x + dtb_ref[...]
    softplus = jnp.maximum(z, 0.0) + jnp.log1p(jnp.exp(-jnp.abs(z)))
    g = -jnp.exp(alog_ref[...]) * softplus
    pos = lax.broadcasted_iota(jnp.int32, x.shape, 0) & (GDN_CHUNK - 1)
    gc_f = g
    gc_b = g
    step = 1
    while step < GDN_CHUNK:
        gc_f = gc_f + jnp.where(pos >= step, pltpu.roll(gc_f, step, 0), 0.0)
        gc_b = gc_b + jnp.where(pos < GDN_CHUNK - step, pltpu.roll(gc_b, GATE_ROWS - step, 0), 0.0)
        step *= 2
    g_tot = pltpu.roll(gc_f + gc_b - g, 4 * H, 1)
    lane = lax.broadcasted_iota(jnp.int32, x.shape, 1)
    out = jnp.where(lane < 2 * H, beta,
                    jnp.where(lane < 3 * H, gc_f,
                              jnp.where(lane < 4 * H, gc_b,
                                        jnp.where((lane >= 6 * H) & (lane < 8 * H), g_tot, 0.0))))
    o_ref[...] = out


def _gdn_gates(pg, a_log_f, a_log_b, dt_bias_f, dt_bias_b, n_heads):
    M = pg.shape[0]
    H = n_heads
    assert 8 * H <= LANES
    pad = lambda a, b: jnp.concatenate([jnp.zeros((2 * H,), F32), a.astype(F32), b.astype(F32),
                                        jnp.zeros((LANES - 4 * H,), F32)]).reshape(1, LANES)
    return pl.pallas_call(
        functools.partial(_gdn_gates_kernel, n_heads=H),
        grid=(M // GATE_ROWS,),
        in_specs=[pl.BlockSpec((GATE_ROWS, LANES), lambda i: (i, 0)),
                  pl.BlockSpec((1, LANES), lambda i: (0, 0)),
                  pl.BlockSpec((1, LANES), lambda i: (0, 0))],
        out_specs=pl.BlockSpec((GATE_ROWS, LANES), lambda i: (i, 0)),
        out_shape=jax.ShapeDtypeStruct((M, LANES), F32),
        compiler_params=_cparams(("parallel",), 24),
        name="gdn_gates",
    )(pg, pad(a_log_f, a_log_b), pad(dt_bias_f, dt_bias_b))


def _delta_chunks(q, k, v, beta, gc_col, gc_row, gtot, state, lower):
    n = len(q)
    C = q[0].shape[0]
    D = k[0].shape[1]
    ii = lax.broadcasted_iota(jnp.int32, (C, C), 0)
    jj = lax.broadcasted_iota(jnp.int32, (C, C), 1)
    eye = jnp.where(ii == jj, 1.0, 0.0)
    incl = [(ii >= jj) if lo else (ii <= jj) for lo in lower]
    strict = [(ii > jj) if lo else (ii < jj) for lo in lower]
    rng = range(n)
    decay = [jnp.where(incl[i], jnp.exp(jnp.where(incl[i], gc_col[i] - gc_row[i], 0.0)), 0.0) for i in rng]
    kb = [k[i] * beta[i] for i in rng]
    kk = [_dot_nt(kb[i], k[i]) for i in rng]
    qk = [_dot_nt(q[i], k[i]) for i in rng]
    m = [jnp.where(strict[i], -(kk[i] * decay[i]), 0.0) for i in rng]
    r = [eye + m[i] for i in rng]
    m = [_dot(m[i], m[i]) for i in rng]
    for _ in range(int(math.log2(C)) - 2):
        rm = [_dot(jnp.concatenate([r[i], m[i]], axis=0), m[i]) for i in rng]
        r = [r[i] + rm[i][:C] for i in rng]
        m = [rm[i][C:] for i in rng]
    r = [r[i] + _dot(r[i], m[i]) for i in rng]
    eg = [jnp.exp(gc_col[i]) for i in rng]
    wu = [_dot(r[i], jnp.concatenate([kb[i] * eg[i], v[i] * beta[i]], axis=1)) for i in rng]
    ws = [_dot(jnp.concatenate([wu[i][:, :D], q[i] * eg[i]], axis=0), state[i]) for i in rng]
    v_new = [wu[i][:, D:] - ws[i][:C] for i in rng]
    qkm = [jnp.where(incl[i], qk[i] * decay[i], 0.0) for i in rng]
    out = [ws[i][C:] + _dot(qkm[i], v_new[i]) for i in rng]
    k_dec = [k[i] * jnp.exp(gtot[i] - gc_col[i]) for i in rng]
    new_state = [state[i] * jnp.exp(gtot[i][0:1, :]) + _dot_tn(k_dec[i], v_new[i]) for i in rng]
    return out, new_state


def _gdn_scan_kernel(qf_ref, kf_ref, vf_ref, qb_ref, kb_ref, vb_ref, gf_ref, gb_ref, rf_ref, rb_ref,
                     of_ref, ob_ref, s_scr, *, n_heads, head_dim):
    H, Dh = n_heads, head_dim
    B = qf_ref.shape[0]

    @pl.when(pl.program_id(0) == 0)
    def _():
        s_scr[...] = jnp.zeros_like(s_scr)

    col = lambda g, j: g[:, j:j + 1]
    sls = [slice(h * Dh, (h + 1) * Dh) for h in range(H)]
    q, k, v, beta, gc_col, gc_row, gtot, state = ([] for _ in range(8))
    for b in range(B):
        gf = gf_ref[b]
        gb = gb_ref[b]
        q += [qf_ref[b, :, sl] for sl in sls] + [qb_ref[b, :, sl] for sl in sls]
        k += [kf_ref[b, :, sl] for sl in sls] + [kb_ref[b, :, sl] for sl in sls]
        v += [vf_ref[b, :, sl] for sl in sls] + [vb_ref[b, :, sl] for sl in sls]
        beta += [col(gf, h) for h in range(H)] + [col(gb, H + h) for h in range(H)]
        gc_col += [col(gf, 2 * H + h) for h in range(H)] + [col(gb, 3 * H + h) for h in range(H)]
        gc_row += [rf_ref[b, h:h + 1, :] for h in range(H)] + [rb_ref[b, H + h:H + h + 1, :] for h in range(H)]
        gtot += [col(gf, 6 * H + h) for h in range(H)] + [col(gb, 7 * H + h) for h in range(H)]
        state += [s_scr[b, 0, h] for h in range(H)] + [s_scr[b, 1, h] for h in range(H)]
    out, new_state = _delta_chunks(q, k, v, beta, gc_col, gc_row, gtot, state, ([True] * H + [False] * H) * B)
    for b in range(B):
        for h in range(H):
            of_ref[b, :, sls[h]] = out[2 * H * b + h]
            ob_ref[b, :, sls[h]] = out[2 * H * b + H + h]
            s_scr[b, 0, h] = new_state[2 * H * b + h]
            s_scr[b, 1, h] = new_state[2 * H * b + H + h]


def _gdn_scan(qkv, gates, gates_row, n_heads, head_dim):
    B, L, _ = qkv.shape
    H, Dh = n_heads, head_dim
    d = H * Dh
    C = GDN_CHUNK
    N = L // C
    fwd = lambda col: pl.BlockSpec((B, C, d), lambda n: (0, n, col))
    bwd = lambda col: pl.BlockSpec((B, C, d), lambda n: (0, N - 1 - n, col))
    return pl.pallas_call(
        functools.partial(_gdn_scan_kernel, n_heads=H, head_dim=Dh),
        grid=(N,),
        in_specs=[fwd(0), fwd(1), fwd(2), bwd(0), bwd(1), bwd(2),
                  pl.BlockSpec((B, C, LANES), lambda n: (0, n, 0)),
                  pl.BlockSpec((B, C, LANES), lambda n: (0, N - 1 - n, 0)),
                  pl.BlockSpec((B, None, 2 * H, C), lambda n: (0, n, 0, 0)),
                  pl.BlockSpec((B, None, 2 * H, C), lambda n: (0, N - 1 - n, 0, 0))],
        out_specs=[pl.BlockSpec((B, C, d), lambda n: (0, n, 0)),
                   pl.BlockSpec((B, C, d), lambda n: (0, N - 1 - n, 0))],
        out_shape=[jax.ShapeDtypeStruct((B, L, d), F32)] * 2,
        scratch_shapes=[pltpu.VMEM((B, 2, H, Dh, Dh), F32)],
        compiler_params=_cparams(("arbitrary",), 32),
        name="gdn_scan",
    )(qkv, qkv, qkv, qkv, qkv, qkv, gates, gates, gates_row, gates_row)


def _gdn_branch(p3, pg, conv_w, a_log_f, a_log_b, dt_bias_f, dt_bias_b, col0, n_heads, head_dim):
    B, L, _ = p3.shape
    H = n_heads
    qkv = _gdn_pre(p3, conv_w, col0, H, head_dim)
    gates = _gdn_gates(pg, a_log_f, a_log_b, dt_bias_f, dt_bias_b, H).reshape(B, L, LANES)
    N = L // GDN_CHUNK
    gates_row = gates[..., 2 * H:4 * H].reshape(B, N, GDN_CHUNK, 2 * H).transpose(0, 1, 3, 2)
    return _gdn_scan(qkv, gates, gates_row, H, head_dim)


@functools.lru_cache(maxsize=None)
def _filter_positions(L, pos_emb_dim):
    n = 2 * L
    r = np.arange(n)
    k = np.where(r < L, r, np.where(r == L, 0, n - r)).astype(np.float64)
    t = k / (L - 1)
    bands = (pos_emb_dim - 1) // 2
    fb = np.linspace(1e-4, bands - 1, bands)
    ang = (2.0 * math.pi / L) * k[:, None] * fb[None, :]
    z = np.concatenate([t[:, None], np.cos(ang), -np.sin(ang)], axis=-1)
    return z.astype(np.float32)


@functools.lru_cache(maxsize=None)
def _decay_rates(d_hy):
    max_decay = math.log(DECAY_TARGET) / FAST_DECAY_PCT
    min_decay = math.log(DECAY_TARGET) / SLOW_DECAY_PCT
    return np.abs(np.linspace(min_decay, max_decay, d_hy)).astype(np.float32).reshape(1, d_hy)


def _filt_mlp_kernel(zt_ref, w1t_ref, b1_ref, w2t_ref, b2_ref, w3t_ref, b3_ref, fr_ref, o_ref):
    fr = fr_ref[...]
    h = jnp.sin(fr * (_dot_hi(w1t_ref[...], zt_ref[...]) + b1_ref[...]))
    h = jnp.sin(fr * (_dot_hi(w2t_ref[...], h) + b2_ref[...]))
    o_ref[...] = jnp.sin(fr * (_dot_hi(w3t_ref[...], h) + b3_ref[...]))


def _filt_mlp(zt, w1, b1, w2, b2, w3, b3, freq, tc=1024):
    pe, n = zt.shape
    fw = w1.shape[1]
    col = lambda a: a.reshape(-1, 1).astype(F32)
    full = lambda a: pl.BlockSpec(a.shape, lambda i: (0, 0))
    args = (zt, w1.T, col(b1), w2.T, col(b2), w3.T, col(b3), col(freq))
    return pl.pallas_call(
        _filt_mlp_kernel,
        grid=(n // tc,),
        in_specs=[pl.BlockSpec((pe, tc), lambda i: (0, i))] + [full(a) for a in args[1:]],
        out_specs=pl.BlockSpec((fw, tc), lambda i: (0, i)),
        out_shape=jax.ShapeDtypeStruct((fw, n), F32),
        compiler_params=_cparams(("parallel",), 24),
        name="filt_mlp",
    )(*args)


@functools.lru_cache(maxsize=None)
def _dft_tables(L):
    n = 2 * L
    N2 = FFT_N2
    N1 = n // N2
    N1h = N1 // 2
    j2 = np.arange(N2)[:, None, None]
    k1 = np.arange(N1)[None, :, None]

    def stage1(n_j1):
        j1 = np.arange(n_j1)[None, None, :]
        m = (k1 * (N2 * j1 + j2)) % n
        th = 2.0 * np.pi * m / n
        return np.cos(th), np.sin(th)

    c, s = stage1(N1h)
    t1 = np.concatenate([np.concatenate([c, s], axis=2), np.concatenate([-s, c], axis=2)], axis=1)
    c, s = stage1(N1)
    t1g = np.concatenate([c, -s], axis=1)
    c, s = stage1(N1h)
    ct, st = np.swapaxes(c, 1, 2) / n, np.swapaxes(s, 1, 2) / n
    t2 = np.concatenate([np.concatenate([ct, -st], axis=2), np.concatenate([st, ct], axis=2)], axis=1)
    a = np.arange(N2)
    th = 2.0 * np.pi * ((a[:, None] * a[None, :]) % N2) / N2
    c2, s2 = np.cos(th), np.sin(th)
    f2f = np.block([[c2, s2], [-s2, c2]])
    f2i = np.block([[c2, -s2], [s2, c2]])
    as_bf16 = lambda x: jnp.asarray(x, dtype=F32).astype(BF16)
    return dict(N1=N1, N2=N2, t1=t1.astype(np.float32), t1g=t1g.astype(np.float32), t2=t2.astype(np.float32),
                f2f=f2f.astype(np.float32), f2i=f2i.astype(np.float32))


FILT_ROWS = 512
FFT_UNROLL = 8
PITCH_PAD = 8
K1_GROUP = 2


def _filt_fft_kernel(h3_ref, w4f_ref, w4b_ref, delta_ref, t1g_ref, f2f_ref, hspec_ref, hb0_ref, g_scr, a_scr,
                     *, L, N1, N2):
    n = 2 * L
    gp = N2 + PITCH_PAD
    ap = 2 * N1 + PITCH_PAD
    delta = delta_ref[...]
    hb0_ref[...] = jnp.zeros_like(hb0_ref)

    def gen(c, carry):
        r0 = pl.multiple_of(c * FILT_ROWS, FILT_ROWS)
        row = r0 + lax.broadcasted_iota(jnp.int32, (FILT_ROWS, LANES), 0)
        lag = jnp.where(row < L, row, jnp.where(row == L, 0, n - row))
        window = jnp.exp(-(lag.astype(F32) * (1.0 / (L - 1))) * delta)
        w4 = jnp.where(r0 < L, w4f_ref[...], w4b_ref[...])
        g = _dot(h3_ref[pl.ds(r0, FILT_ROWS), :], w4) * window
        at_l = row == L
        hb0_ref[...] += jnp.sum(jnp.where(at_l, g, 0.0), axis=0, keepdims=True)
        g = jnp.where(at_l, 0.0, g)
        for q in range(FILT_ROWS // N2):
            dst = pl.multiple_of((c * (FILT_ROWS // N2) + q) * gp, SUBLANES)
            g_scr[pl.ds(dst, N2), :] = g[q * N2:(q + 1) * N2]
        return carry

    lax.fori_loop(0, n // FILT_ROWS, gen, 0)

    def stage1(j2, carry):
        x = g_scr[pl.ds(j2, N1, stride=gp), :]
        a_scr[pl.ds(pl.multiple_of(j2 * ap, SUBLANES), 2 * N1), :] = _dot(t1g_ref[j2], x)
        return carry

    lax.fori_loop(0, N2, stage1, 0, unroll=2 * FFT_UNROLL)

    def stage2(kp, carry):
        k1s = [kp * K1_GROUP + t for t in range(K1_GROUP)]
        x = jnp.concatenate([jnp.concatenate([a_scr[pl.ds(k1, N2, stride=ap), :],
                                              a_scr[pl.ds(N1 + k1, N2, stride=ap), :]], axis=0) for k1 in k1s], axis=1)
        z = _dot(f2f_ref[...], x).astype(hspec_ref.dtype)
        for t, k1 in enumerate(k1s):
            hspec_ref[pl.ds(pl.multiple_of(k1 * 2 * N2, 2 * N2), 2 * N2), :] = z[:, t * LANES:(t + 1) * LANES]
        return carry

    lax.fori_loop(0, N1 // K1_GROUP, stage2, 0, unroll=FFT_UNROLL // K1_GROUP)


def _filt_fft(h3, w4, L, d_hy):
    tb = _dft_tables(L)
    N1, N2 = tb["N1"], tb["N2"]
    n = 2 * L
    fw = h3.shape[1]
    nt = d_hy // LANES
    t1g = jnp.asarray(tb["t1g"]).astype(BF16)
    f2f = jnp.asarray(tb["f2f"]).astype(BF16)
    return pl.pallas_call(
        functools.partial(_filt_fft_kernel, L=L, N1=N1, N2=N2),
        grid=(nt,),
        in_specs=[pl.BlockSpec((n, fw), lambda c: (0, 0)),
                  pl.BlockSpec((fw, LANES), lambda c: (0, c)),
                  pl.BlockSpec((fw, LANES), lambda c: (0, c + nt)),
                  pl.BlockSpec((1, LANES), lambda c: (0, c)),
                  pl.BlockSpec(t1g.shape, lambda c: (0, 0, 0)),
                  pl.BlockSpec(f2f.shape, lambda c: (0, 0))],
        out_specs=[pl.BlockSpec((2 * n, LANES), lambda c: (0, c)),
                   pl.BlockSpec((SUBLANES, LANES), lambda c: (0, c))],
        out_shape=[jax.ShapeDtypeStruct((2 * n, d_hy), BF16), jax.ShapeDtypeStruct((SUBLANES, d_hy), F32)],
        scratch_shapes=[pltpu.VMEM((N1 * (N2 + PITCH_PAD), LANES), F32),
                        pltpu.VMEM((N2 * (2 * N1 + PITCH_PAD), LANES), F32)],
        compiler_params=_cparams(("parallel",), 48),
        name="filt_fft",
    )(h3, w4, w4, jnp.asarray(_decay_rates(d_hy)), t1g, f2f)


def _hy_conv_kernel(vg_ref, hspec_ref, skip_ref, hb0_ref, t1_ref, f2f_ref, f2i_ref, t2_ref, y_ref,
                    x_scr, a_scr, b_scr, *, N1, N2):
    N1h = N1 // 2
    xp = N2 + PITCH_PAD
    ap = 2 * N1 + PITCH_PAD
    bp = 2 * N2 + PITCH_PAD

    for b in range(2):
        for j1 in range(N1h):
            x_scr[b, pl.ds(j1 * xp, N2), :] = vg_ref[b, pl.ds(j1 * N2, N2), :].astype(F32)

    def stage1(j2, carry):
        x = jnp.concatenate([x_scr[0, pl.ds(j2, N1h, stride=xp), :],
                             x_scr[1, pl.ds(j2, N1h, stride=xp), :]], axis=0)
        a_scr[pl.ds(pl.multiple_of(j2 * ap, SUBLANES), 2 * N1), :] = _dot(t1_ref[j2], x)
        return carry

    lax.fori_loop(0, N2, stage1, 0, unroll=2 * FFT_UNROLL)

    def stage2(kp, carry):
        k1s = [kp * K1_GROUP + t for t in range(K1_GROUP)]
        x = jnp.concatenate([jnp.concatenate([a_scr[pl.ds(k1, N2, stride=ap), :],
                                              a_scr[pl.ds(N1 + k1, N2, stride=ap), :]], axis=0) for k1 in k1s], axis=1)
        z = _dot(f2f_ref[...], x)
        zr, zi = z[:N2], z[N2:]
        hs = [hspec_ref[pl.ds(pl.multiple_of(k1 * 2 * N2, 2 * N2), 2 * N2), :].astype(F32) for k1 in k1s]
        hr = jnp.concatenate([h[:N2] for h in hs], axis=1)
        hi = jnp.concatenate([h[N2:] for h in hs], axis=1)
        prod = jnp.concatenate([zr * hr - zi * hi, zr * hi + zi * hr], axis=0)
        b = _dot(f2i_ref[...], prod)
        for t, k1 in enumerate(k1s):
            b_scr[pl.ds(pl.multiple_of(k1 * bp, SUBLANES), 2 * N2), :] = b[:, t * LANES:(t + 1) * LANES]
        return carry

    lax.fori_loop(0, N1 // K1_GROUP, stage2, 0, unroll=FFT_UNROLL // K1_GROUP)

    skip = skip_ref[...] + hb0_ref[0:1, :]

    def stage3(j2, carry):
        b = jnp.concatenate([b_scr[pl.ds(j2, N1, stride=bp), :],
                             b_scr[pl.ds(N2 + j2, N1, stride=bp), :]], axis=0)
        y = _dot(t2_ref[j2], b)
        x_scr[0, pl.ds(j2, N1h, stride=xp), :] = y[:N1h] + x_scr[0, pl.ds(j2, N1h, stride=xp), :] * skip
        x_scr[1, pl.ds(j2, N1h, stride=xp), :] = y[N1h:] + x_scr[1, pl.ds(j2, N1h, stride=xp), :] * skip
        return carry

    lax.fori_loop(0, N2, stage3, 0, unroll=2 * FFT_UNROLL)

    for b in range(2):
        for j1 in range(N1h):
            y_ref[b, pl.ds(j1 * N2, N2), :] = x_scr[b, pl.ds(j1 * xp, N2), :]


def _hy_conv(vg, hspec, skip, hb0):
    B, L, d_hy = vg.shape
    assert B % 2 == 0
    tb = _dft_tables(L)
    N1, N2 = tb["N1"], tb["N2"]
    n = 2 * L
    nt = d_hy // LANES
    t1, t2 = (jnp.asarray(tb[k]).astype(BF16) for k in ("t1", "t2"))
    f2f, f2i = (jnp.asarray(tb[k]).astype(BF16) for k in ("f2f", "f2i"))
    const3 = lambda a: pl.BlockSpec(a.shape, lambda b, c: (0, 0, 0))
    const2 = lambda a: pl.BlockSpec(a.shape, lambda b, c: (0, 0))
    return pl.pallas_call(
        functools.partial(_hy_conv_kernel, N1=N1, N2=N2),
        grid=(B // 2, nt),
        in_specs=[pl.BlockSpec((2, L, LANES), lambda b, c: (b, 0, c)),
                  pl.BlockSpec((2 * n, LANES), lambda b, c: (0, c)),
                  pl.BlockSpec((1, LANES), lambda b, c: (0, c)),
                  pl.BlockSpec((SUBLANES, LANES), lambda b, c: (0, c)),
                  const3(t1), const2(f2f), const2(f2i), const3(t2)],
        out_specs=pl.BlockSpec((2, L, LANES), lambda b, c: (b, 0, c)),
        out_shape=jax.ShapeDtypeStruct((B, L, d_hy), F32),
        scratch_shapes=[pltpu.VMEM((2, (N1 // 2) * (N2 + PITCH_PAD), LANES), F32),
                        pltpu.VMEM((N2 * (2 * N1 + PITCH_PAD), LANES), F32),
                        pltpu.VMEM((N1 * (2 * N2 + PITCH_PAD), LANES), F32)],
        compiler_params=_cparams(("parallel", "parallel"), 58),
        name="hy_conv",
    )(vg, hspec, skip.reshape(1, d_hy).astype(F32), hb0, t1, f2f, f2i, t2)


def _hyena_branch(p3, conv_w, conv_b, fw1, fb1, fw2, fb2, fw3, fb3, fw4, freq, skip, d_hy):
    B, L, _ = p3.shape
    x0c, vg = _hy_pre(p3, conv_w, conv_b, d_hy)
    zt = jnp.asarray(_filter_positions(L, fw1.shape[0]).T)
    h3 = _filt_mlp(zt, fw1, fb1, fw2, fb2, fw3, fb3, freq).T
    hspec, hb0 = _filt_fft(h3, fw4, L, d_hy)
    return x0c, _hy_conv(vg, hspec, skip, hb0)


ROUTE_GATE, ROUTE_EXPERT, ROUTE_RANK = 0, 2, 4
MIX_ROWS = 256


def _mix_route_kernel(yc_ref, x0_ref, of_ref, ob_ref, z_ref, hnw_ref, gnw_ref, x_ref, wo_ref, n2w_ref, wr_ref, br_ref,
                      x2_ref, u_ref, route_ref, route_t_ref, cnt_ref, run_scr, *, n_heads, head_dim, n_groups,
                      per_group):
    tm = x_ref.shape[0]
    G, P = n_groups, per_group

    @pl.when(pl.program_id(0) == 0)
    def _():
        run_scr[...] = jnp.zeros_like(run_scr)

    run = run_scr[0:1, :]
    blocks = [pl.ds(s * MIX_ROWS, MIX_ROWS) for s in range(tm // MIX_ROWS)]
    us = []
    for blk, rs in enumerate(blocks):
        yh = yc_ref[rs, :] * x0_ref[rs, :].astype(F32)
        yh = yh * lax.rsqrt(jnp.mean(yh * yh, axis=-1, keepdims=True) + EPS) * hnw_ref[...]
        parts = [yh.astype(BF16)]
        for h in range(n_heads):
            sl = slice(h * head_dim, (h + 1) * head_dim)
            o = of_ref[rs, sl] + ob_ref[rs, sl]
            z = z_ref[rs, sl].astype(F32)
            o = o * lax.rsqrt(jnp.mean(o * o, axis=-1, keepdims=True) + EPS) * gnw_ref[...] * _silu(z)
            parts.append(o.astype(BF16))
        ymix = jnp.concatenate(parts, axis=-1)
        x2 = x_ref[rs, :] + jnp.dot(ymix, wo_ref[...], preferred_element_type=F32)
        x2_ref[rs, :] = x2
        u = x2 * lax.rsqrt(jnp.mean(x2 * x2, axis=-1, keepdims=True) + EPS) * n2w_ref[...]
        _store_token_tiles(u_ref, blk * MIX_ROWS, _pack_bf16_pairs(u))
        us.append(u)

    all_logits = [_dot(u, wr_ref[...]) + br_ref[...] for u in us]
    for s, (rs, logits) in enumerate(zip(blocks, all_logits)):
        lane = lax.broadcasted_iota(jnp.int32, logits.shape, 1)
        neg = jnp.float32(-jnp.inf)
        big = jnp.int32(4 * LANES)
        first = lambda hit: jnp.min(jnp.where(hit, lane, big), axis=-1, keepdims=True)
        gl = jnp.where(lane < G, logits, neg)
        gmax = jnp.max(gl, axis=-1, keepdims=True)
        gidx = first(gl == gmax)
        grp_gate = 1.0 / jnp.sum(jnp.exp(gl - gmax), axis=-1, keepdims=True)
        in_grp = (lane >= G) & (lane < G + G * P) & (((lane - G) // P) == gidx)
        ll = jnp.where(in_grp, logits, neg)
        m1 = jnp.max(ll, axis=-1, keepdims=True)
        i1 = first(ll == m1)
        denom = jnp.sum(jnp.exp(ll - m1), axis=-1, keepdims=True)
        ll2 = jnp.where(lane == i1, neg, ll)
        m2 = jnp.max(ll2, axis=-1, keepdims=True)
        i2 = first(ll2 == m2)
        p1 = 1.0 / denom
        p2 = jnp.exp(m2 - m1) / denom
        gate1 = grp_gate * (p1 / (p1 + p2))
        gate2 = grp_gate * (p2 / (p1 + p2))
        e1 = i1 - G
        e2 = i2 - G

        oh1 = jnp.where(lane == e1, 1.0, 0.0)
        oh2 = jnp.where(lane == e2, 1.0, 0.0)
        oh = oh1 + oh2
        ii = lax.broadcasted_iota(jnp.int32, (MIX_ROWS, MIX_ROWS), 0)
        jj = lax.broadcasted_iota(jnp.int32, (MIX_ROWS, MIX_ROWS), 1)
        before = _dot(jnp.where(ii > jj, 1.0, 0.0), oh) + run
        r1 = jnp.sum(oh1 * before, axis=-1, keepdims=True)
        r2 = jnp.sum(oh2 * before, axis=-1, keepdims=True)
        run = run + jnp.sum(oh, axis=0, keepdims=True)

        rec = jnp.where(lane == ROUTE_GATE, gate1, 0.0)
        rec = jnp.where(lane == ROUTE_GATE + 1, gate2, rec)
        rec = jnp.where(lane == ROUTE_EXPERT, e1.astype(F32), rec)
        rec = jnp.where(lane == ROUTE_EXPERT + 1, e2.astype(F32), rec)
        rec = jnp.where(lane == ROUTE_RANK, r1, rec)
        rec = jnp.where(lane == ROUTE_RANK + 1, r2, rec)
        route_ref[rs, :] = rec
        route_t_ref[:, s * MIX_ROWS:(s + 1) * MIX_ROWS] = jnp.transpose(rec)[:SUBLANES, :]

    run_scr[...] = jnp.broadcast_to(run, run_scr.shape)
    cnt_ref[...] = run_scr[...]


def _mix_route(yconv, x0c, o_f, o_b, p, z_col, hy_norm_w, gdn_norm_w, xf, w_out_bf16, norm2_w, wr, br,
               n_heads, head_dim, n_groups, per_group, tm=512):
    M, D = xf.shape
    d_hy = yconv.shape[1]
    d_gdn = o_f.shape[1]
    assert z_col % d_gdn == 0 and n_groups * (per_group + 1) <= LANES
    zb = z_col // d_gdn
    row = lambda i: (i, 0)
    const = lambda i: (0, 0)
    kern = functools.partial(_mix_route_kernel, n_heads=n_heads, head_dim=head_dim,
                             n_groups=n_groups, per_group=per_group)
    return pl.pallas_call(
        kern,
        grid=(M // tm,),
        in_specs=[pl.BlockSpec((tm, d_hy), row), pl.BlockSpec((tm, d_hy), row),
                  pl.BlockSpec((tm, d_gdn), row), pl.BlockSpec((tm, d_gdn), row),
                  pl.BlockSpec((tm, d_gdn), lambda i: (i, zb)),
                  pl.BlockSpec((1, d_hy), const), pl.BlockSpec((1, head_dim), const),
                  pl.BlockSpec((tm, D), row), pl.BlockSpec(w_out_bf16.shape, const, pipeline_mode=pl.Buffered(1)),
                  pl.BlockSpec((1, D), const), pl.BlockSpec((D, LANES), const), pl.BlockSpec((1, LANES), const)],
        out_specs=[pl.BlockSpec((tm, D), row), pl.BlockSpec((tm * (D // 2 // LANES), LANES), row),
                   pl.BlockSpec((tm, LANES), row), pl.BlockSpec((SUBLANES, tm), lambda i: (0, i)),
                   pl.BlockSpec((SUBLANES, LANES), const)],
        out_shape=[jax.ShapeDtypeStruct((M, D), F32), jax.ShapeDtypeStruct((M * (D // 2 // LANES), LANES), jnp.uint32),
                   jax.ShapeDtypeStruct((M, LANES), F32), jax.ShapeDtypeStruct((SUBLANES, M), F32),
                   jax.ShapeDtypeStruct((SUBLANES, LANES), F32)],
        scratch_shapes=[pltpu.VMEM((SUBLANES, LANES), F32)],
        compiler_params=_cparams(("arbitrary",), 58),
        name="mix_route",
    )(yconv, x0c, o_f, o_b, p, hy_norm_w.reshape(1, d_hy), gdn_norm_w.reshape(1, head_dim), xf, w_out_bf16,
      norm2_w.reshape(1, D), wr, br)


WAIT_GROUP = 8


def _experts_kernel(ts_ref, se_ref, nn_ref, nv_ref, tgt_nxt_ref, tgt_ref, tgt_prv_ref, u_hbm, w1_hbm, w3_hbm, w2_hbm,
                    out_hbm, x0, x1, y0, y1, wf1, wf3, wf2, w1b, w3b, w2b, gsem, ssem, wsem, *, n_tokens):
    i = pl.program_id(0)
    na = nn_ref[0]
    ns = nn_ref[1]
    CH = w1b.shape[0] // 2 // LANES
    TB = x0.shape[0] // CH
    W_SLOTS = wf1.shape[0]
    last_tile = pl.num_programs(0) - 1
    nv_cur = nv_ref[i]
    nv_nxt = jnp.where(i + 1 < na, nv_ref[jnp.minimum(i + 1, last_tile)], 0)
    nv_prv = jnp.where(i > 0, nv_ref[jnp.maximum(i - 1, 0)], 0)
    nv_pp = jnp.where(i > 1, nv_ref[jnp.maximum(i - 2, 0)], 0)

    def weight_copies(s):
        slot = lax.rem(s, W_SLOTS)
        e = se_ref[s]
        return [pltpu.make_async_copy(w_hbm.at[e], wf.at[slot], wsem.at[slot])
                for w_hbm, wf in ((w1_hbm, wf1), (w3_hbm, wf3), (w2_hbm, wf2))]

    def token_of(v):
        if n_tokens & (n_tokens - 1) == 0:
            return v & (n_tokens - 1)
        return lax.rem(v, n_tokens)

    def tile_of(row):
        return pl.ds(pl.multiple_of(row * CH, CH), CH)

    def gather(tgt, xbuf, s, r):
        return pltpu.make_async_copy(u_hbm.at[tile_of(token_of(tgt[r]))], xbuf.at[tile_of(r)], gsem.at[s])

    def scatter(ybuf, s, r, dst):
        return pltpu.make_async_copy(ybuf.at[tile_of(r)], out_hbm.at[tile_of(dst)], ssem.at[s])

    def wait_tokens(count, make):
        groups = count // WAIT_GROUP

        def grp(g, carry):
            make(WAIT_GROUP).wait()
            return carry

        lax.fori_loop(0, groups, grp, 0)

        def one(g, carry):
            make(1).wait()
            return carry

        lax.fori_loop(groups * WAIT_GROUP, count, one, 0)

    def wait_gather(xbuf, s, count):
        wait_tokens(count, lambda n: pltpu.make_async_copy(u_hbm.at[pl.ds(0, n * CH)], xbuf.at[pl.ds(0, n * CH)],
                                                             gsem.at[s]))

    def wait_scatter(ybuf, s, count):
        wait_tokens(count, lambda n: pltpu.make_async_copy(ybuf.at[pl.ds(0, n * CH)], out_hbm.at[pl.ds(0, n * CH)],
                                                             ssem.at[s]))

    def each_row(count, fn):
        def body(r, carry):
            fn(r)
            return carry
        lax.fori_loop(0, count, body, 0)

    def step(p):
        q = 1 - p
        x_cur, x_nxt = (x0, x1) if p == 0 else (x1, x0)
        y_cur, y_prv = (y0, y1) if p == 0 else (y1, y0)
        wait_gather(x_cur, p, nv_cur)
        wait_scatter(y_cur, p, nv_pp)

        x = _unpack_bf16_pairs(_load_token_tiles(x_cur, 0, TB, CH)).astype(BF16)
        for r in range(TB):
            @pl.when(r < nv_nxt)
            def _():
                gather(tgt_nxt_ref, x_nxt, q, r).start()
        for r in range(TB):
            @pl.when(r < nv_prv)
            def _():
                scatter(y_prv, q, r, tgt_prv_ref[r]).start()
        h = _silu(jnp.dot(x, w1b[...], preferred_element_type=F32)) * jnp.dot(x, w3b[...], preferred_element_type=F32)
        _store_token_tiles(y_cur, 0, _pack_bf16_pairs(jnp.dot(h.astype(BF16), w2b[...], preferred_element_type=F32)))

        @pl.when(i == na - 1)
        def _():
            wait_scatter(y_prv, q, nv_prv)
            each_row(nv_cur, lambda r: scatter(y_cur, p, r, tgt_ref[r]).start())
            wait_scatter(y_cur, p, nv_cur)

    @pl.when(i < na)
    def _():
        @pl.when(i == 0)
        def _():
            x0[...] = jnp.zeros_like(x0)
            x1[...] = jnp.zeros_like(x1)
            each_row(nv_cur, lambda r: gather(tgt_ref, x0, 0, r).start())
            for s0 in range(W_SLOTS - 1):
                @pl.when(s0 < ns)
                def _():
                    for c in weight_copies(s0):
                        c.start(priority=WEIGHT_DMA_PRIORITY)

        s = ts_ref[i]

        @pl.when((i == 0) | (s != ts_ref[jnp.maximum(i - 1, 0)]))
        def _():
            @pl.when(s + W_SLOTS - 1 < ns)
            def _():
                for c in weight_copies(s + W_SLOTS - 1):
                    c.start(priority=WEIGHT_DMA_PRIORITY)

            for c in weight_copies(s):
                c.wait()
            slot = lax.rem(s, W_SLOTS)
            w1b[...] = wf1[slot].astype(BF16)
            w3b[...] = wf3[slot].astype(BF16)
            w2b[...] = wf2[slot].astype(BF16)

        parity = lax.rem(i, 2)

        @pl.when(parity == 0)
        def _():
            step(0)

        @pl.when(parity == 1)
        def _():
            step(1)


WEIGHT_SLOTS = 3
WEIGHT_DMA_PRIORITY = 1


def _experts(u, slot_tgt, tile_seq, seq_expert, n_active_seq, tile_valid, w1, w3, w2):
    E, D, de = w1.shape
    CH = D // 2 // LANES
    assert CH % SUBLANES == 0
    T = u.shape[0] // CH
    TB = EXPERT_ROWS
    n_tiles = slot_tgt.shape[0] // TB
    table = lambda f: pl.BlockSpec((TB,), lambda i, ts, se, nn, nv: (f(i),), memory_space=pltpu.SMEM)
    hbm = pl.BlockSpec(memory_space=pl.ANY)
    grid_spec = pltpu.PrefetchScalarGridSpec(
        num_scalar_prefetch=4,
        grid=(n_tiles,),
        in_specs=[table(lambda i: jnp.minimum(i + 1, n_tiles - 1)), table(lambda i: i),
                  table(lambda i: jnp.maximum(i - 1, 0)), hbm, hbm, hbm, hbm],
        out_specs=hbm,
        scratch_shapes=[pltpu.VMEM((TB * CH, LANES), jnp.uint32)] * 4
                       + [pltpu.VMEM((WEIGHT_SLOTS, D, de), F32), pltpu.VMEM((WEIGHT_SLOTS, D, de), F32),
                          pltpu.VMEM((WEIGHT_SLOTS, de, D), F32),
                          pltpu.VMEM((D, de), BF16), pltpu.VMEM((D, de), BF16), pltpu.VMEM((de, D), BF16),
                          pltpu.SemaphoreType.DMA((2,)), pltpu.SemaphoreType.DMA((2,)),
                          pltpu.SemaphoreType.DMA((WEIGHT_SLOTS,))],
    )
    return pl.pallas_call(
        functools.partial(_experts_kernel, n_tokens=T),
        grid_spec=grid_spec,
        out_shape=jax.ShapeDtypeStruct((2 * T * CH, LANES), jnp.uint32),
        compiler_params=_cparams(("arbitrary",), 58),
        name="experts",
    )(tile_seq, seq_expert, n_active_seq, tile_valid, slot_tgt, slot_tgt, slot_tgt, u, w1, w3, w2)


def _slot_table_kernel(dest_ref, init_hbm, o_ref, sem):
    fill = pltpu.make_async_copy(init_hbm, o_ref, sem)
    fill.start()
    fill.wait()

    def put(a, carry):
        o_ref[dest_ref[a]] = a
        return carry

    lax.fori_loop(0, dest_ref.shape[0], put, 0, unroll=16)


def _slot_table(dest, init):
    smem = pl.BlockSpec(memory_space=pltpu.SMEM)
    return pl.pallas_call(
        _slot_table_kernel,
        in_specs=[smem, pl.BlockSpec(memory_space=pl.ANY)],
        out_specs=smem,
        out_shape=jax.ShapeDtypeStruct(init.shape, jnp.int32),
        scratch_shapes=[pltpu.SemaphoreType.DMA],
        name="slot_table",
    )(dest, init)


def _dispatch_tables(route_t, counts, n_experts):
    T = route_t.shape[1]
    TB = EXPERT_ROWS
    e = route_t[ROUTE_EXPERT:ROUTE_EXPERT + 2].astype(jnp.int32)
    rank = route_t[ROUTE_RANK:ROUTE_RANK + 2].astype(jnp.int32)
    cnt = counts[0, :n_experts].astype(jnp.int32)
    padded = (cnt + TB - 1) // TB * TB
    pad_end = jnp.cumsum(padded)
    pad_start = pad_end - padded
    ids = jnp.arange(n_experts, dtype=jnp.int32)[:, None, None]
    start_of = jnp.sum(jnp.where(e[None] == ids, pad_start[:, None, None], 0), axis=0)
    dest = start_of + rank
    n_tiles = -(-(2 * T + n_experts * (TB - 1)) // TB)
    slot_tgt = _slot_table(dest.reshape(-1), jnp.zeros((n_tiles * TB,), jnp.int32))
    start = jnp.arange(n_tiles, dtype=jnp.int32) * TB
    tile_expert = jnp.sum(start[:, None] >= pad_end[None, :], axis=1)
    tile_expert = jnp.minimum(tile_expert, n_experts - 1).astype(jnp.int32)
    owns = cnt > 0
    pos = jnp.cumsum(owns.astype(jnp.int32)) - 1
    experts = jnp.arange(n_experts, dtype=jnp.int32)
    seq_expert = jnp.sum(jnp.where(owns[None, :] & (pos[None, :] == experts[:, None]), experts[None, :], 0), axis=1)
    tile_seq = jnp.sum(jnp.where(tile_expert[:, None] == experts[None, :], pos[None, :], 0), axis=1)
    n_active_seq = jnp.stack([pad_end[-1] // TB, jnp.sum(owns.astype(jnp.int32))]).astype(jnp.int32)
    pick = tile_expert[:, None] == experts[None, :]
    in_expert = start - jnp.sum(jnp.where(pick, pad_start[None, :], 0), axis=1)
    tile_cnt = jnp.sum(jnp.where(pick, cnt[None, :], 0), axis=1)
    tile_valid = jnp.where(start < pad_end[-1], jnp.clip(tile_cnt - in_expert, 0, TB), 0).astype(jnp.int32)
    return slot_tgt, tile_seq.astype(jnp.int32), seq_expert.astype(jnp.int32), n_active_seq, tile_valid


def _combine_kernel(x2_ref, e0_ref, e1_ref, route_ref, w_ref, o_ref, *, final_norm):
    r = route_ref[...]
    tm = x2_ref.shape[0]
    ch = e0_ref.shape[0] // tm
    e0 = _unpack_bf16_pairs(_load_token_tiles(e0_ref, 0, tm, ch))
    e1 = _unpack_bf16_pairs(_load_token_tiles(e1_ref, 0, tm, ch))
    y = x2_ref[...] + r[:, ROUTE_GATE:ROUTE_GATE + 1] * e0 + r[:, ROUTE_GATE + 1:ROUTE_GATE + 2] * e1
    if final_norm:
        y = y * lax.rsqrt(jnp.mean(y * y, axis=-1, keepdims=True) + EPS) * w_ref[...]
    o_ref[...] = y


def _combine(x2, planes, route, norm_w, final_norm, tm=512):
    M, D = x2.shape
    return pl.pallas_call(
        functools.partial(_combine_kernel, final_norm=final_norm),
        grid=(M // tm,),
        in_specs=[pl.BlockSpec((tm, D), lambda i: (i, 0)),
                  pl.BlockSpec((tm * (D // 2 // LANES), LANES), lambda i: (i, 0)),
                  pl.BlockSpec((tm * (D // 2 // LANES), LANES), lambda i: (M // tm + i, 0)),
                  pl.BlockSpec((tm, LANES), lambda i: (i, 0)),
                  pl.BlockSpec((1, D), lambda i: (0, 0))],
        out_specs=pl.BlockSpec((tm, D), lambda i: (i, 0)),
        out_shape=jax.ShapeDtypeStruct((M, D), F32),
        compiler_params=_cparams(("parallel",), 48),
        name="combine",
    )(x2, planes, planes, route, norm_w.reshape(1, D))


def kernel(x, norm1_w, w_in, hy_conv_w, hy_conv_b, hy_filt_w1, hy_filt_b1, hy_filt_w2, hy_filt_b2, hy_filt_w3, hy_filt_b3, hy_filt_w4, hy_sin_freq, hy_skip, hy_norm_w, gdn_conv_w, gdn_a_log_f, gdn_a_log_b, gdn_dt_bias_f, gdn_dt_bias_b, gdn_norm_w, w_out, norm2_w, router_group_w, router_group_b, router_expert_w, router_expert_b, exp_w1, exp_w3, exp_w2, final_norm_w):
    B, L, D = x.shape
    M = B * L
    depth = w_in.shape[0]
    d_hy = hy_skip.shape[-1]
    H = gdn_a_log_f.shape[-1]
    Dh = gdn_norm_w.shape[-1]
    d_gdn = H * Dh
    n_main = 3 * d_hy + 4 * d_gdn
    G = router_group_w.shape[-1]
    E = router_expert_w.shape[-1]
    xf = x.reshape(M, D)
    for l in range(depth):
        p, pg = _inproj(xf, norm1_w[l], jnp.swapaxes(w_in[l], 0, 1), n_main, 4 * H)
        p3 = p.reshape(B, L, n_main)
        x0c, yconv = _hyena_branch(p3, hy_conv_w[l], hy_conv_b[l], hy_filt_w1[l], hy_filt_b1[l], hy_filt_w2[l],
                                   hy_filt_b2[l], hy_filt_w3[l], hy_filt_b3[l], hy_filt_w4[l], hy_sin_freq[l],
                                   hy_skip[l], d_hy)
        o_f, o_b = _gdn_branch(p3, pg, gdn_conv_w[l], gdn_a_log_f[l], gdn_a_log_b[l], gdn_dt_bias_f[l],
                               gdn_dt_bias_b[l], 3 * d_hy, H, Dh)
        wr = jnp.pad(jnp.concatenate([router_group_w[l], router_expert_w[l]], axis=1), ((0, 0), (0, LANES - G - E)))
        br = jnp.pad(jnp.concatenate([router_group_b[l], router_expert_b[l]]), (0, LANES - G - E)).reshape(1, LANES)
        x2, u, route, route_t, counts = _mix_route(
            yconv.reshape(M, d_hy), x0c.reshape(M, d_hy), o_f.reshape(M, d_gdn), o_b.reshape(M, d_gdn), p,
            3 * d_hy + 3 * d_gdn, hy_norm_w[l], gdn_norm_w[l], xf, w_out[l].astype(BF16), norm2_w[l], wr, br,
            H, Dh, G, E // G)
        slot_tgt, tile_seq, seq_expert, n_active_seq, tile_valid = _dispatch_tables(route_t, counts, E)
        planes = _experts(u, slot_tgt, tile_seq, seq_expert, n_active_seq, tile_valid, exp_w1[l], exp_w3[l],
                          exp_w2[l])
        xf = _combine(x2, planes, route, final_norm_w, final_norm=(l == depth - 1))
    return xf.reshape(B, L, D)
```

```python
import functools
import math

import jax
import jax.numpy as jnp
import numpy as np
from jax import lax
from jax.experimental import pallas as pl
from jax.experimental.pallas import tpu as pltpu

F32 = jnp.float32
BF16 = jnp.bfloat16
EPS = 1e-6
LANES = 128
SUBLANES = 8
VMEM_BYTES_V7X = 64 * 1024 * 1024
GDN_CHUNK = 64
FFT_N2 = 128
EXPERT_ROWS = 256
DECAY_TARGET = 1e-2
FAST_DECAY_PCT = 0.3
SLOW_DECAY_PCT = 1.5


def _cparams(sem, vmem_mib):
    limit = int(vmem_mib * 1024 * 1024)
    assert limit < VMEM_BYTES_V7X
    return pltpu.CompilerParams(dimension_semantics=sem, vmem_limit_bytes=limit)


def _dot(a, b):
    return jnp.dot(a.astype(BF16), b.astype(BF16), preferred_element_type=F32)


def _dot_nt(a, b):
    return lax.dot_general(a.astype(BF16), b.astype(BF16), (((1,), (1,)), ((), ())), preferred_element_type=F32)


def _dot_tn(a, b):
    return lax.dot_general(a.astype(BF16), b.astype(BF16), (((0,), (0,)), ((), ())), preferred_element_type=F32)


def _dot_hi(a, b):
    return jnp.dot(a, b, preferred_element_type=F32, precision=lax.Precision.HIGHEST)


def _silu(x):
    return x * jax.nn.sigmoid(x)


def _pack_bf16_pairs(x):
    c = x.shape[1] // 2
    lo = lax.bitcast_convert_type(x[:, :c].astype(BF16).astype(F32), jnp.uint32) >> 16
    hi = lax.bitcast_convert_type(x[:, c:].astype(BF16).astype(F32), jnp.uint32) & jnp.uint32(0xFFFF0000)
    return hi | lo


def _unpack_bf16_pairs(w):
    lo = lax.bitcast_convert_type(w << 16, F32)
    hi = lax.bitcast_convert_type(w & jnp.uint32(0xFFFF0000), F32)
    return jnp.concatenate([lo, hi], axis=1)


def _store_token_tiles(ref, row0, words):
    n, width = words.shape
    ch = width // LANES
    for c in range(ch):
        ref[pl.ds(row0 * ch + c, n, stride=ch), :] = words[:, c * LANES:(c + 1) * LANES]


def _load_token_tiles(ref, row0, n, ch):
    return jnp.concatenate([ref[pl.ds(row0 * ch + c, n, stride=ch), :] for c in range(ch)], axis=1)


def _inproj_kernel(x_hbm, nw_ref, wt_ref, wgt_ref, p_ref, g_ref, h_scr, x_scr, x_sem):
    i = pl.program_id(0)
    j = pl.program_id(1)
    tm = x_scr.shape[0]

    def x_copy(tile):
        return pltpu.make_async_copy(x_hbm.at[pl.ds(pl.multiple_of(tile * tm, tm), tm)], x_scr, x_sem)

    @pl.when((i == 0) & (j == 0))
    def _():
        x_copy(0).start()

    @pl.when((j == 1) & (i + 1 < pl.num_programs(0)))
    def _():
        x_copy(i + 1).start()

    @pl.when(j == 0)
    def _():
        x_copy(i).wait()
        x = x_scr[...]
        h = x * lax.rsqrt(jnp.mean(x * x, axis=-1, keepdims=True) + EPS) * nw_ref[...]
        h_scr[...] = h.astype(BF16)
        g = _dot_nt(h, wgt_ref[...])
        g_ref[...] = jnp.concatenate([g, jnp.zeros((g.shape[0], LANES - g.shape[1]), F32)], axis=1)

    p_ref[...] = _dot_nt(h_scr[...], wt_ref[...]).astype(p_ref.dtype)


def _inproj(xf, norm_w, wt, n_main, n_gate, tm=2048, tn=512):
    M, D = xf.shape
    assert n_main % tn == 0 and M % tm == 0 and n_main % n_gate == 0 and n_gate % SUBLANES == 0
    assert n_main // tn >= 2
    return pl.pallas_call(
        _inproj_kernel,
        grid=(M // tm, n_main // tn),
        in_specs=[
            pl.BlockSpec(memory_space=pl.ANY),
            pl.BlockSpec((1, D), lambda i, j: (0, 0)),
            pl.BlockSpec((tn, D), lambda i, j: (j, 0)),
            pl.BlockSpec((n_gate, D), lambda i, j: (n_main // n_gate, 0)),
        ],
        out_specs=[
            pl.BlockSpec((tm, tn), lambda i, j: (i, j)),
            pl.BlockSpec((tm, LANES), lambda i, j: (i, 0)),
        ],
        out_shape=[jax.ShapeDtypeStruct((M, n_main), BF16), jax.ShapeDtypeStruct((M, LANES), F32)],
        scratch_shapes=[pltpu.VMEM((tm, D), BF16), pltpu.VMEM((tm, D), F32), pltpu.SemaphoreType.DMA],
        compiler_params=_cparams(("arbitrary", "arbitrary"), 57),
        name="inproj",
    )(xf, norm_w.reshape(1, D), wt, wt)


def _conv3_rows(ref, r0, rows, w, n_rows):
    cur = ref[pl.ds(r0, rows), :].astype(F32)
    lo = jnp.maximum(r0 - HALO_ROWS, 0)
    hi = jnp.minimum(r0 + rows, n_rows - HALO_ROWS)
    prev_grp = ref[pl.ds(pl.multiple_of(lo, HALO_ROWS), HALO_ROWS), :].astype(F32)
    next_grp = ref[pl.ds(pl.multiple_of(hi, HALO_ROWS), HALO_ROWS), :].astype(F32)
    prev_row = jnp.where(r0 > 0, prev_grp[HALO_ROWS - 1:HALO_ROWS, :], 0.0)
    next_row = jnp.where(r0 + rows < n_rows, next_grp[0:1, :], 0.0)
    row = lax.broadcasted_iota(jnp.int32, cur.shape, 0)
    xm = jnp.where(row == 0, prev_row, pltpu.roll(cur, 1, 0))
    xp = jnp.where(row == rows - 1, next_row, pltpu.roll(cur, rows - 1, 0))
    return xm * w[0:1, :] + cur * w[1:2, :] + xp * w[2:3, :]


CONV_ROWS = 256
HALO_ROWS = 16


def _hy_pre_kernel(x0_ref, x1_ref, v_ref, w0_ref, w1_ref, w2_ref, b0_ref, b1_ref, b2_ref, x0c_ref, vg_ref):
    L = x0_ref.shape[0]
    w0, w1, w2 = w0_ref[...], w1_ref[...], w2_ref[...]
    b0, b1, b2 = b0_ref[...], b1_ref[...], b2_ref[...]

    def body(c, carry):
        r0 = pl.multiple_of(c * CONV_ROWS, CONV_ROWS)
        x0c_ref[pl.ds(r0, CONV_ROWS), :] = (_conv3_rows(x0_ref, r0, CONV_ROWS, w0, L) + b0).astype(x0c_ref.dtype)
        x1c = _conv3_rows(x1_ref, r0, CONV_ROWS, w1, L) + b1
        vc = _conv3_rows(v_ref, r0, CONV_ROWS, w2, L) + b2
        vg_ref[pl.ds(r0, CONV_ROWS), :] = (vc * x1c).astype(vg_ref.dtype)
        return carry

    lax.fori_loop(0, L // CONV_ROWS, body, 0, unroll=2)


def _hy_pre(p3, conv_w, conv_b, d_hy):
    B, L, _ = p3.shape
    nt = d_hy // LANES
    bias = conv_b.reshape(1, -1)
    pspec = lambda off: pl.BlockSpec((None, L, LANES), lambda b, c: (b, 0, c + off))
    wspec = lambda off: pl.BlockSpec((3, LANES), lambda b, c: (0, c + off))
    bspec = lambda off: pl.BlockSpec((1, LANES), lambda b, c: (0, c + off))
    ospec = pl.BlockSpec((None, L, LANES), lambda b, c: (b, 0, c))
    return pl.pallas_call(
        _hy_pre_kernel,
        grid=(B, nt),
        in_specs=[pspec(0), pspec(nt), pspec(2 * nt), wspec(0), wspec(nt), wspec(2 * nt),
                  bspec(0), bspec(nt), bspec(2 * nt)],
        out_specs=[ospec, ospec],
        out_shape=[jax.ShapeDtypeStruct((B, L, d_hy), BF16)] * 2,
        compiler_params=_cparams(("parallel", "parallel"), 40),
        name="hy_pre",
    )(p3, p3, p3, conv_w, conv_w, conv_w, bias, bias, bias)


def _gdn_pre_kernel(x_ref, w_ref, o_ref, *, n_heads, head_dim):
    L = x_ref.shape[0]
    w = w_ref[...]
    c = pl.program_id(1)
    q_scale = jnp.where(c < n_heads, head_dim ** -0.5, 1.0)
    is_qk = c < 2 * n_heads

    def body(i, carry):
        r0 = pl.multiple_of(i * CONV_ROWS, CONV_ROWS)
        y = _silu(_conv3_rows(x_ref, r0, CONV_ROWS, w, L))
        inv = lax.rsqrt(jnp.sum(y * y, axis=-1, keepdims=True) + EPS) * q_scale
        o_ref[pl.ds(r0, CONV_ROWS), :] = (y * jnp.where(is_qk, inv, 1.0)).astype(o_ref.dtype)
        return carry

    lax.fori_loop(0, L // CONV_ROWS, body, 0, unroll=2)


def _gdn_pre(p3, conv_w, col0, n_heads, head_dim):
    B, L, _ = p3.shape
    assert head_dim == LANES
    nt = 3 * n_heads
    off = col0 // LANES
    return pl.pallas_call(
        functools.partial(_gdn_pre_kernel, n_heads=n_heads, head_dim=head_dim),
        grid=(B, nt),
        in_specs=[pl.BlockSpec((None, L, LANES), lambda b, c: (b, 0, c + off)),
                  pl.BlockSpec((3, LANES), lambda b, c: (0, c))],
        out_specs=pl.BlockSpec((None, L, LANES), lambda b, c: (b, 0, c)),
        out_shape=jax.ShapeDtypeStruct((B, L, nt * LANES), BF16),
        compiler_params=_cparams(("parallel", "parallel"), 24),
        name="gdn_pre",
    )(p3, conv_w)


GATE_ROWS = 512


def _gdn_gates_kernel(pg_ref, alog_ref, dtb_ref, o_ref, *, n_heads):
    H = n_heads
    x = pg_ref[...]
    beta = jax.nn.sigmoid(x)
    z = x + dtb_ref[...]
    softplus = jnp.maximum(z, 0.0) + jnp.log1p(jnp.exp(-jnp.abs(z)))
    g = -jnp.exp(alog_ref[...]) * softplus
    pos = lax.broadcasted_iota(jnp.int32, x.shape, 0) & (GDN_CHUNK - 1)
    gc_f = g
    gc_b = g
    step = 1
    while step < GDN_CHUNK:
        gc_f = gc_f + jnp.where(pos >= step, pltpu.roll(gc_f, step, 0), 0.0)
        gc_b = gc_b + jnp.where(pos < GDN_CHUNK - step, pltpu.roll(gc_b, GATE_ROWS - step, 0), 0.0)
        step *= 2
    g_tot = pltpu.roll(gc_f + gc_b - g, 4 * H, 1)
    lane = lax.broadcasted_iota(jnp.int32, x.shape, 1)
    out = jnp.where(lane < 2 * H, beta,
                    jnp.where(lane < 3 * H, gc_f,
                              jnp.where(lane < 4 * H, gc_b,
                                        jnp.where((lane >= 6 * H) & (lane < 8 * H), g_tot, 0.0))))
    o_ref[...] = out


def _gdn_gates(pg, a_log_f, a_log_b, dt_bias_f, dt_bias_b, n_heads):
    M = pg.shape[0]
    H = n_heads
    assert 8 * H <= LANES
    pad = lambda a, b: jnp.concatenate([jnp.zeros((2 * H,), F32), a.astype(F32), b.astype(F32),
                                        jnp.zeros((LANES - 4 * H,), F32)]).reshape(1, LANES)
    return pl.pallas_call(
        functools.partial(_gdn_gates_kernel, n_heads=H),
        grid=(M // GATE_ROWS,),
        in_specs=[pl.BlockSpec((GATE_ROWS, LANES), lambda i: (i, 0)),
                  pl.BlockSpec((1, LANES), lambda i: (0, 0)),
                  pl.BlockSpec((1, LANES), lambda i: (0, 0))],
        out_specs=pl.BlockSpec((GATE_ROWS, LANES), lambda i: (i, 0)),
        out_shape=jax.ShapeDtypeStruct((M, LANES), F32),
        compiler_params=_cparams(("parallel",), 24),
        name="gdn_gates",
    )(pg, pad(a_log_f, a_log_b), pad(dt_bias_f, dt_bias_b))


def _delta_chunks(q, k, v, beta, gc_col, gc_row, gtot, state, lower):
    n = len(q)
    C = q[0].shape[0]
    D = k[0].shape[1]
    ii = lax.broadcasted_iota(jnp.int32, (C, C), 0)
    jj = lax.broadcasted_iota(jnp.int32, (C, C), 1)
    eye = jnp.where(ii == jj, 1.0, 0.0)
    incl = [(ii >= jj) if lo else (ii <= jj) for lo in lower]
    strict = [(ii > jj) if lo else (ii < jj) for lo in lower]
    rng = range(n)
    decay = [jnp.where(incl[i], jnp.exp(jnp.where(incl[i], gc_col[i] - gc_row[i], 0.0)), 0.0) for i in rng]
    kb = [k[i] * beta[i] for i in rng]
    kk = [_dot_nt(kb[i], k[i]) for i in rng]
    qk = [_dot_nt(q[i], k[i]) for i in rng]
    m = [jnp.where(strict[i], -(kk[i] * decay[i]), 0.0) for i in rng]
    r = [eye + m[i] for i in rng]
    m = [_dot(m[i], m[i]) for i in rng]
    for _ in range(int(math.log2(C)) - 2):
        rm = [_dot(jnp.concatenate([r[i], m[i]], axis=0), m[i]) for i in rng]
        r = [r[i] + rm[i][:C] for i in rng]
        m = [rm[i][C:] for i in rng]
    r = [r[i] + _dot(r[i], m[i]) for i in rng]
    eg = [jnp.exp(gc_col[i]) for i in rng]
    wu = [_dot(r[i], jnp.concatenate([kb[i] * eg[i], v[i] * beta[i]], axis=1)) for i in rng]
    ws = [_dot(jnp.concatenate([wu[i][:, :D], q[i] * eg[i]], axis=0), state[i]) for i in rng]
    v_new = [wu[i][:, D:] - ws[i][:C] for i in rng]
    qkm = [jnp.where(incl[i], qk[i] * decay[i], 0.0) for i in rng]
    out = [ws[i][C:] + _dot(qkm[i], v_new[i]) for i in rng]
    k_dec = [k[i] * jnp.exp(gtot[i] - gc_col[i]) for i in rng]
    new_state = [state[i] * jnp.exp(gtot[i][0:1, :]) + _dot_tn(k_dec[i], v_new[i]) for i in rng]
    return out, new_state


def _gdn_scan_kernel(qf_ref, kf_ref, vf_ref, qb_ref, kb_ref, vb_ref, gf_ref, gb_ref, rf_ref, rb_ref,
                     of_ref, ob_ref, s_scr, *, n_heads, head_dim):
    H, Dh = n_heads, head_dim
    B = qf_ref.shape[0]

    @pl.when(pl.program_id(0) == 0)
    def _():
        s_scr[...] = jnp.zeros_like(s_scr)

    col = lambda g, j: g[:, j:j + 1]
    sls = [slice(h * Dh, (h + 1) * Dh) for h in range(H)]
    q, k, v, beta, gc_col, gc_row, gtot, state = ([] for _ in range(8))
    for b in range(B):
        gf = gf_ref[b]
        gb = gb_ref[b]
        q += [qf_ref[b, :, sl] for sl in sls] + [qb_ref[b, :, sl] for sl in sls]
        k += [kf_ref[b, :, sl] for sl in sls] + [kb_ref[b, :, sl] for sl in sls]
        v += [vf_ref[b, :, sl] for sl in sls] + [vb_ref[b, :, sl] for sl in sls]
        beta += [col(gf, h) for h in range(H)] + [col(gb, H + h) for h in range(H)]
        gc_col += [col(gf, 2 * H + h) for h in range(H)] + [col(gb, 3 * H + h) for h in range(H)]
        gc_row += [rf_ref[b, h:h + 1, :] for h in range(H)] + [rb_ref[b, H + h:H + h + 1, :] for h in range(H)]
        gtot += [col(gf, 6 * H + h) for h in range(H)] + [col(gb, 7 * H + h) for h in range(H)]
        state += [s_scr[b, 0, h] for h in range(H)] + [s_scr[b, 1, h] for h in range(H)]
    out, new_state = _delta_chunks(q, k, v, beta, gc_col, gc_row, gtot, state, ([True] * H + [False] * H) * B)
    for b in range(B):
        for h in range(H):
            of_ref[b, :, sls[h]] = out[2 * H * b + h]
            ob_ref[b, :, sls[h]] = out[2 * H * b + H + h]
            s_scr[b, 0, h] = new_state[2 * H * b + h]
            s_scr[b, 1, h] = new_state[2 * H * b + H + h]


def _gdn_scan(qkv, gates, gates_row, n_heads, head_dim):
    B, L, _ = qkv.shape
    H, Dh = n_heads, head_dim
    d = H * Dh
    C = GDN_CHUNK
    N = L // C
    fwd = lambda col: pl.BlockSpec((B, C, d), lambda n: (0, n, col))
    bwd = lambda col: pl.BlockSpec((B, C, d), lambda n: (0, N - 1 - n, col))
    return pl.pallas_call(
        functools.partial(_gdn_scan_kernel, n_heads=H, head_dim=Dh),
        grid=(N,),
        in_specs=[fwd(0), fwd(1), fwd(2), bwd(0), bwd(1), bwd(2),
                  pl.BlockSpec((B, C, LANES), lambda n: (0, n, 0)),
                  pl.BlockSpec((B, C, LANES), lambda n: (0, N - 1 - n, 0)),
                  pl.BlockSpec((B, None, 2 * H, C), lambda n: (0, n, 0, 0)),
                  pl.BlockSpec((B, None, 2 * H, C), lambda n: (0, N - 1 - n, 0, 0))],
        out_specs=[pl.BlockSpec((B, C, d), lambda n: (0, n, 0)),
                   pl.BlockSpec((B, C, d), lambda n: (0, N - 1 - n, 0))],
        out_shape=[jax.ShapeDtypeStruct((B, L, d), F32)] * 2,
        scratch_shapes=[pltpu.VMEM((B, 2, H, Dh, Dh), F32)],
        compiler_params=_cparams(("arbitrary",), 32),
        name="gdn_scan",
    )(qkv, qkv, qkv, qkv, qkv, qkv, gates, gates, gates_row, gates_row)


def _gdn_branch(p3, pg, conv_w, a_log_f, a_log_b, dt_bias_f, dt_bias_b, col0, n_heads, head_dim):
    B, L, _ = p3.shape
    H = n_heads
    qkv = _gdn_pre(p3, conv_w, col0, H, head_dim)
    gates = _gdn_gates(pg, a_log_f, a_log_b, dt_bias_f, dt_bias_b, H).reshape(B, L, LANES)
    N = L // GDN_CHUNK
    gates_row = gates[..., 2 * H:4 * H].reshape(B, N, GDN_CHUNK, 2 * H).transpose(0, 1, 3, 2)
    return _gdn_scan(qkv, gates, gates_row, H, head_dim)


@functools.lru_cache(maxsize=None)
def _filter_positions(L, pos_emb_dim):
    n = 2 * L
    r = np.arange(n)
    k = np.where(r < L, r, np.where(r == L, 0, n - r)).astype(np.float64)
    t = k / (L - 1)
    bands = (pos_emb_dim - 1) // 2
    fb = np.linspace(1e-4, bands - 1, bands)
    ang = (2.0 * math.pi / L) * k[:, None] * fb[None, :]
    z = np.concatenate([t[:, None], np.cos(ang), -np.sin(ang)], axis=-1)
    return z.astype(np.float32)


@functools.lru_cache(maxsize=None)
def _decay_rates(d_hy):
    max_decay = math.log(DECAY_TARGET) / FAST_DECAY_PCT
    min_decay = math.log(DECAY_TARGET) / SLOW_DECAY_PCT
    return np.abs(np.linspace(min_decay, max_decay, d_hy)).astype(np.float32).reshape(1, d_hy)


def _filt_mlp_kernel(zt_ref, w1t_ref, b1_ref, w2t_ref, b2_ref, w3t_ref, b3_ref, fr_ref, o_ref):
    fr = fr_ref[...]
    h = jnp.sin(fr * (_dot_hi(w1t_ref[...], zt_ref[...]) + b1_ref[...]))
    h = jnp.sin(fr * (_dot_hi(w2t_ref[...], h) + b2_ref[...]))
    o_ref[...] = jnp.sin(fr * (_dot_hi(w3t_ref[...], h) + b3_ref[...]))


def _filt_mlp(zt, w1, b1, w2, b2, w3, b3, freq, tc=1024):
    pe, n = zt.shape
    fw = w1.shape[1]
    col = lambda a: a.reshape(-1, 1).astype(F32)
    full = lambda a: pl.BlockSpec(a.shape, lambda i: (0, 0))
    args = (zt, w1.T, col(b1), w2.T, col(b2), w3.T, col(b3), col(freq))
    return pl.pallas_call(
        _filt_mlp_kernel,
        grid=(n // tc,),
        in_specs=[pl.BlockSpec((pe, tc), lambda i: (0, i))] + [full(a) for a in args[1:]],
        out_specs=pl.BlockSpec((fw, tc), lambda i: (0, i)),
        out_shape=jax.ShapeDtypeStruct((fw, n), F32),
        compiler_params=_cparams(("parallel",), 24),
        name="filt_mlp",
    )(*args)


@functools.lru_cache(maxsize=None)
def _dft_tables(L):
    n = 2 * L
    N2 = FFT_N2
    N1 = n // N2
    N1h = N1 // 2
    j2 = np.arange(N2)[:, None, None]
    k1 = np.arange(N1)[None, :, None]

    def stage1(n_j1):
        j1 = np.arange(n_j1)[None, None, :]
        m = (k1 * (N2 * j1 + j2)) % n
        th = 2.0 * np.pi * m / n
        return np.cos(th), np.sin(th)

    c, s = stage1(N1h)
    t1 = np.concatenate([np.concatenate([c, s], axis=2), np.concatenate([-s, c], axis=2)], axis=1)
    c, s = stage1(N1)
    t1g = np.concatenate([c, -s], axis=1)
    c, s = stage1(N1h)
    ct, st = np.swapaxes(c, 1, 2) / n, np.swapaxes(s, 1, 2) / n
    t2 = np.concatenate([np.concatenate([ct, -st], axis=2), np.concatenate([st, ct], axis=2)], axis=1)
    a = np.arange(N2)
    th = 2.0 * np.pi * ((a[:, None] * a[None, :]) % N2) / N2
    c2, s2 = np.cos(th), np.sin(th)
    f2f = np.block([[c2, s2], [-s2, c2]])
    f2i = np.block([[c2, -s2], [s2, c2]])
    return dict(N1=N1, N2=N2, t1=t1.astype(np.float32), t1g=t1g.astype(np.float32), t2=t2.astype(np.float32),
                f2f=f2f.astype(np.float32), f2i=f2i.astype(np.float32))


FILT_ROWS = 512
FFT_UNROLL = 8
PITCH_PAD = 8
K1_GROUP = 2


def _filt_fft_kernel(h3_ref, w4f_ref, w4b_ref, delta_ref, t1g_ref, f2f_ref, hspec_ref, hb0_ref, g_scr, a_scr,
                     *, L, N1, N2):
    n = 2 * L
    gp = N2 + PITCH_PAD
    ap = 2 * N1 + PITCH_PAD
    delta = delta_ref[...]
    hb0_ref[...] = jnp.zeros_like(hb0_ref)

    def gen(c, carry):
        r0 = pl.multiple_of(c * FILT_ROWS, FILT_ROWS)
        row = r0 + lax.broadcasted_iota(jnp.int32, (FILT_ROWS, LANES), 0)
        lag = jnp.where(row < L, row, jnp.where(row == L, 0, n - row))
        window = jnp.exp(-(lag.astype(F32) * (1.0 / (L - 1))) * delta)
        w4 = jnp.where(r0 < L, w4f_ref[...], w4b_ref[...])
        g = _dot(h3_ref[pl.ds(r0, FILT_ROWS), :], w4) * window
        at_l = row == L
        hb0_ref[...] += jnp.sum(jnp.where(at_l, g, 0.0), axis=0, keepdims=True)
        g = jnp.where(at_l, 0.0, g)
        for q in range(FILT_ROWS // N2):
            dst = pl.multiple_of((c * (FILT_ROWS // N2) + q) * gp, SUBLANES)
            g_scr[pl.ds(dst, N2), :] = g[q * N2:(q + 1) * N2]
        return carry

    lax.fori_loop(0, n // FILT_ROWS, gen, 0)

    def stage1(j2, carry):
        x = g_scr[pl.ds(j2, N1, stride=gp), :]
        a_scr[pl.ds(pl.multiple_of(j2 * ap, SUBLANES), 2 * N1), :] = _dot(t1g_ref[j2], x)
        return carry

    lax.fori_loop(0, N2, stage1, 0, unroll=2 * FFT_UNROLL)

    def stage2(kp, carry):
        k1s = [kp * K1_GROUP + t for t in range(K1_GROUP)]
        x = jnp.concatenate([jnp.concatenate([a_scr[pl.ds(k1, N2, stride=ap), :],
                                              a_scr[pl.ds(N1 + k1, N2, stride=ap), :]], axis=0) for k1 in k1s], axis=1)
        z = _dot(f2f_ref[...], x).astype(hspec_ref.dtype)
        for t, k1 in enumerate(k1s):
            hspec_ref[pl.ds(pl.multiple_of(k1 * 2 * N2, 2 * N2), 2 * N2), :] = z[:, t * LANES:(t + 1) * LANES]
        return carry

    lax.fori_loop(0, N1 // K1_GROUP, stage2, 0, unroll=FFT_UNROLL)


def _filt_fft(h3, w4, L, d_hy):
    tb = _dft_tables(L)
    N1, N2 = tb["N1"], tb["N2"]
    n = 2 * L
    fw = h3.shape[1]
    nt = d_hy // LANES
    t1g = jnp.asarray(tb["t1g"]).astype(BF16)
    f2f = jnp.asarray(tb["f2f"]).astype(BF16)
    return pl.pallas_call(
        functools.partial(_filt_fft_kernel, L=L, N1=N1, N2=N2),
        grid=(nt,),
        in_specs=[pl.BlockSpec((n, fw), lambda c: (0, 0)),
                  pl.BlockSpec((fw, LANES), lambda c: (0, c)),
                  pl.BlockSpec((fw, LANES), lambda c: (0, c + nt)),
                  pl.BlockSpec((1, LANES), lambda c: (0, c)),
                  pl.BlockSpec(t1g.shape, lambda c: (0, 0, 0)),
                  pl.BlockSpec(f2f.shape, lambda c: (0, 0))],
        out_specs=[pl.BlockSpec((2 * n, LANES), lambda c: (0, c)),
                   pl.BlockSpec((SUBLANES, LANES), lambda c: (0, c))],
        out_shape=[jax.ShapeDtypeStruct((2 * n, d_hy), BF16), jax.ShapeDtypeStruct((SUBLANES, d_hy), F32)],
        scratch_shapes=[pltpu.VMEM((N1 * (N2 + PITCH_PAD), LANES), F32),
                        pltpu.VMEM((N2 * (2 * N1 + PITCH_PAD), LANES), F32)],
        compiler_params=_cparams(("parallel",), 48),
        name="filt_fft",
    )(h3, w4, w4, jnp.asarray(_decay_rates(d_hy)), t1g, f2f)


def _hy_conv_kernel(vg_ref, hspec_ref, skip_ref, hb0_ref, t1_ref, f2f_ref, f2i_ref, t2_ref, y_ref,
                    x_scr, a_scr, b_scr, *, N1, N2):
    N1h = N1 // 2
    xp = N2 + PITCH_PAD
    ap = 2 * N1 + PITCH_PAD
    bp = 2 * N2 + PITCH_PAD

    for b in range(2):
        for j1 in range(N1h):
            x_scr[b, pl.ds(j1 * xp, N2), :] = vg_ref[b, pl.ds(j1 * N2, N2), :].astype(F32)

    def stage1(j2, carry):
        x = jnp.concatenate([x_scr[0, pl.ds(j2, N1h, stride=xp), :],
                             x_scr[1, pl.ds(j2, N1h, stride=xp), :]], axis=0)
        a_scr[pl.ds(pl.multiple_of(j2 * ap, SUBLANES), 2 * N1), :] = _dot(t1_ref[j2], x)
        return carry

    lax.fori_loop(0, N2, stage1, 0, unroll=2 * FFT_UNROLL)

    def stage2(kp, carry):
        k1s = [kp * K1_GROUP + t for t in range(K1_GROUP)]
        x = jnp.concatenate([jnp.concatenate([a_scr[pl.ds(k1, N2, stride=ap), :],
                                              a_scr[pl.ds(N1 + k1, N2, stride=ap), :]], axis=0) for k1 in k1s], axis=1)
        z = _dot(f2f_ref[...], x)
        zr, zi = z[:N2], z[N2:]
        hs = [hspec_ref[pl.ds(pl.multiple_of(k1 * 2 * N2, 2 * N2), 2 * N2), :].astype(F32) for k1 in k1s]
        hr = jnp.concatenate([h[:N2] for h in hs], axis=1)
        hi = jnp.concatenate([h[N2:] for h in hs], axis=1)
        prod = jnp.concatenate([zr * hr - zi * hi, zr * hi + zi * hr], axis=0)
        b = _dot(f2i_ref[...], prod)
        for t, k1 in enumerate(k1s):
            b_scr[pl.ds(pl.multiple_of(k1 * bp, SUBLANES), 2 * N2), :] = b[:, t * LANES:(t + 1) * LANES]
        return carry

    lax.fori_loop(0, N1 // K1_GROUP, stage2, 0, unroll=FFT_UNROLL)

    skip = skip_ref[...] + hb0_ref[0:1, :]

    def stage3(j2, carry):
        b = jnp.concatenate([b_scr[pl.ds(j2, N1, stride=bp), :],
                             b_scr[pl.ds(N2 + j2, N1, stride=bp), :]], axis=0)
        y = _dot(t2_ref[j2], b)
        x_scr[0, pl.ds(j2, N1h, stride=xp), :] = y[:N1h] + x_scr[0, pl.ds(j2, N1h, stride=xp), :] * skip
        x_scr[1, pl.ds(j2, N1h, stride=xp), :] = y[N1h:] + x_scr[1, pl.ds(j2, N1h, stride=xp), :] * skip
        return carry

    lax.fori_loop(0, N2, stage3, 0, unroll=2 * FFT_UNROLL)

    for b in range(2):
        for j1 in range(N1h):
            y_ref[b, pl.ds(j1 * N2, N2), :] = x_scr[b, pl.ds(j1 * xp, N2), :]


def _hy_conv(vg, hspec, skip, hb0):
    B, L, d_hy = vg.shape
    assert B % 2 == 0
    tb = _dft_tables(L)
    N1, N2 = tb["N1"], tb["N2"]
    n = 2 * L
    nt = d_hy // LANES
    t1, t2 = (jnp.asarray(tb[k]).astype(BF16) for k in ("t1", "t2"))
    f2f, f2i = (jnp.asarray(tb[k]).astype(BF16) for k in ("f2f", "f2i"))
    const3 = lambda a: pl.BlockSpec(a.shape, lambda b, c: (0, 0, 0))
    const2 = lambda a: pl.BlockSpec(a.shape, lambda b, c: (0, 0))
    return pl.pallas_call(
        functools.partial(_hy_conv_kernel, N1=N1, N2=N2),
        grid=(B // 2, nt),
        in_specs=[pl.BlockSpec((2, L, LANES), lambda b, c: (b, 0, c)),
                  pl.BlockSpec((2 * n, LANES), lambda b, c: (0, c)),
                  pl.BlockSpec((1, LANES), lambda b, c: (0, c)),
                  pl.BlockSpec((SUBLANES, LANES), lambda b, c: (0, c)),
                  const3(t1), const2(f2f), const2(f2i), const3(t2)],
        out_specs=pl.BlockSpec((2, L, LANES), lambda b, c: (b, 0, c)),
        out_shape=jax.ShapeDtypeStruct((B, L, d_hy), F32),
        scratch_shapes=[pltpu.VMEM((2, (N1 // 2) * (N2 + PITCH_PAD), LANES), F32),
                        pltpu.VMEM((N2 * (2 * N1 + PITCH_PAD), LANES), F32),
                        pltpu.VMEM((N1 * (2 * N2 + PITCH_PAD), LANES), F32)],
        compiler_params=_cparams(("parallel", "parallel"), 58),
        name="hy_conv",
    )(vg, hspec, skip.reshape(1, d_hy).astype(F32), hb0, t1, f2f, f2i, t2)


def _hyena_branch(p3, conv_w, conv_b, fw1, fb1, fw2, fb2, fw3, fb3, fw4, freq, skip, d_hy):
    B, L, _ = p3.shape
    x0c, vg = _hy_pre(p3, conv_w, conv_b, d_hy)
    zt = jnp.asarray(_filter_positions(L, fw1.shape[0]).T)
    h3 = _filt_mlp(zt, fw1, fb1, fw2, fb2, fw3, fb3, freq).T
    hspec, hb0 = _filt_fft(h3, fw4, L, d_hy)
    return x0c, _hy_conv(vg, hspec, skip, hb0)


ROUTE_GATE, ROUTE_EXPERT, ROUTE_RANK = 0, 2, 4
MIX_ROWS = 256


def _mix_route_kernel(yc_ref, x0_ref, of_ref, ob_ref, z_ref, hnw_ref, gnw_ref, x_ref, wo_ref, n2w_ref, wr_ref, br_ref,
                      x2_ref, u_ref, route_ref, route_t_ref, cnt_ref, run_scr, *, n_heads, head_dim, n_groups,
                      per_group):
    tm = x_ref.shape[0]
    G, P = n_groups, per_group

    @pl.when(pl.program_id(0) == 0)
    def _():
        run_scr[...] = jnp.zeros_like(run_scr)

    run = run_scr[0:1, :]
    blocks = [pl.ds(s * MIX_ROWS, MIX_ROWS) for s in range(tm // MIX_ROWS)]
    us = []
    for blk, rs in enumerate(blocks):
        yh = yc_ref[rs, :] * x0_ref[rs, :].astype(F32)
        yh = yh * lax.rsqrt(jnp.mean(yh * yh, axis=-1, keepdims=True) + EPS) * hnw_ref[...]
        parts = [yh.astype(BF16)]
        for h in range(n_heads):
            sl = slice(h * head_dim, (h + 1) * head_dim)
            o = of_ref[rs, sl] + ob_ref[rs, sl]
            z = z_ref[rs, sl].astype(F32)
            o = o * lax.rsqrt(jnp.mean(o * o, axis=-1, keepdims=True) + EPS) * gnw_ref[...] * _silu(z)
            parts.append(o.astype(BF16))
        ymix = jnp.concatenate(parts, axis=-1)
        x2 = x_ref[rs, :] + jnp.dot(ymix, wo_ref[...], preferred_element_type=F32)
        x2_ref[rs, :] = x2
        u = x2 * lax.rsqrt(jnp.mean(x2 * x2, axis=-1, keepdims=True) + EPS) * n2w_ref[...]
        _store_token_tiles(u_ref, blk * MIX_ROWS, _pack_bf16_pairs(u))
        us.append(u)

    all_logits = [_dot(u, wr_ref[...]) + br_ref[...] for u in us]
    for s, (rs, logits) in enumerate(zip(blocks, all_logits)):
        lane = lax.broadcasted_iota(jnp.int32, logits.shape, 1)
        neg = jnp.float32(-jnp.inf)
        big = jnp.int32(4 * LANES)
        first = lambda hit: jnp.min(jnp.where(hit, lane, big), axis=-1, keepdims=True)
        gl = jnp.where(lane < G, logits, neg)
        gmax = jnp.max(gl, axis=-1, keepdims=True)
        gidx = first(gl == gmax)
        grp_gate = 1.0 / jnp.sum(jnp.exp(gl - gmax), axis=-1, keepdims=True)
        in_grp = (lane >= G) & (lane < G + G * P) & (((lane - G) // P) == gidx)
        ll = jnp.where(in_grp, logits, neg)
        m1 = jnp.max(ll, axis=-1, keepdims=True)
        i1 = first(ll == m1)
        denom = jnp.sum(jnp.exp(ll - m1), axis=-1, keepdims=True)
        ll2 = jnp.where(lane == i1, neg, ll)
        m2 = jnp.max(ll2, axis=-1, keepdims=True)
        i2 = first(ll2 == m2)
        p1 = 1.0 / denom
        p2 = jnp.exp(m2 - m1) / denom
        gate1 = grp_gate * (p1 / (p1 + p2))
        gate2 = grp_gate * (p2 / (p1 + p2))
        e1 = i1 - G
        e2 = i2 - G

        oh1 = jnp.where(lane == e1, 1.0, 0.0)
        oh2 = jnp.where(lane == e2, 1.0, 0.0)
        oh = oh1 + oh2
        ii = lax.broadcasted_iota(jnp.int32, (MIX_ROWS, MIX_ROWS), 0)
        jj = lax.broadcasted_iota(jnp.int32, (MIX_ROWS, MIX_ROWS), 1)
        before = _dot(jnp.where(ii > jj, 1.0, 0.0), oh) + run
        r1 = jnp.sum(oh1 * before, axis=-1, keepdims=True)
        r2 = jnp.sum(oh2 * before, axis=-1, keepdims=True)
        run = run + jnp.sum(oh, axis=0, keepdims=True)

        rec = jnp.where(lane == ROUTE_GATE, gate1, 0.0)
        rec = jnp.where(lane == ROUTE_GATE + 1, gate2, rec)
        rec = jnp.where(lane == ROUTE_EXPERT, e1.astype(F32), rec)
        rec = jnp.where(lane == ROUTE_EXPERT + 1, e2.astype(F32), rec)
        rec = jnp.where(lane == ROUTE_RANK, r1, rec)
        rec = jnp.where(lane == ROUTE_RANK + 1, r2, rec)
        route_ref[rs, :] = rec
        route_t_ref[:, s * MIX_ROWS:(s + 1) * MIX_ROWS] = jnp.transpose(rec)[:SUBLANES, :]

    run_scr[...] = jnp.broadcast_to(run, run_scr.shape)
    cnt_ref[...] = run_scr[...]


def _mix_route(yconv, x0c, o_f, o_b, p, z_col, hy_norm_w, gdn_norm_w, xf, w_out_bf16, norm2_w, wr, br,
               n_heads, head_dim, n_groups, per_group, tm=512):
    M, D = xf.shape
    d_hy = yconv.shape[1]
    d_gdn = o_f.shape[1]
    assert z_col % d_gdn == 0 and n_groups * (per_group + 1) <= LANES
    zb = z_col // d_gdn
    row = lambda i: (i, 0)
    const = lambda i: (0, 0)
    kern = functools.partial(_mix_route_kernel, n_heads=n_heads, head_dim=head_dim,
                             n_groups=n_groups, per_group=per_group)
    return pl.pallas_call(
        kern,
        grid=(M // tm,),
        in_specs=[pl.BlockSpec((tm, d_hy), row), pl.BlockSpec((tm, d_hy), row),
                  pl.BlockSpec((tm, d_gdn), row), pl.BlockSpec((tm, d_gdn), row),
                  pl.BlockSpec((tm, d_gdn), lambda i: (i, zb)),
                  pl.BlockSpec((1, d_hy), const), pl.BlockSpec((1, head_dim), const),
                  pl.BlockSpec((tm, D), row), pl.BlockSpec(w_out_bf16.shape, const, pipeline_mode=pl.Buffered(1)),
                  pl.BlockSpec((1, D), const), pl.BlockSpec((D, LANES), const), pl.BlockSpec((1, LANES), const)],
        out_specs=[pl.BlockSpec((tm, D), row), pl.BlockSpec((tm * (D // 2 // LANES), LANES), row),
                   pl.BlockSpec((tm, LANES), row), pl.BlockSpec((SUBLANES, tm), lambda i: (0, i)),
                   pl.BlockSpec((SUBLANES, LANES), const)],
        out_shape=[jax.ShapeDtypeStruct((M, D), F32), jax.ShapeDtypeStruct((M * (D // 2 // LANES), LANES), jnp.uint32),
                   jax.ShapeDtypeStruct((M, LANES), F32), jax.ShapeDtypeStruct((SUBLANES, M), F32),
                   jax.ShapeDtypeStruct((SUBLANES, LANES), F32)],
        scratch_shapes=[pltpu.VMEM((SUBLANES, LANES), F32)],
        compiler_params=_cparams(("arbitrary",), 58),
        name="mix_route",
    )(yconv, x0c, o_f, o_b, p, hy_norm_w.reshape(1, d_hy), gdn_norm_w.reshape(1, head_dim), xf, w_out_bf16,
      norm2_w.reshape(1, D), wr, br)


WAIT_GROUP = 8


def _experts_kernel(ts_ref, se_ref, nn_ref, nv_ref, tgt_nxt_ref, tgt_ref, tgt_prv_ref, u_hbm, w1_hbm, w3_hbm, w2_hbm,
                    out_hbm, x0, x1, y0, y1, wf1, wf3, wf2, w1b, w3b, w2b, gsem, ssem, wsem, *, n_tokens):
    i = pl.program_id(0)
    na = nn_ref[0]
    ns = nn_ref[1]
    CH = w1b.shape[0] // 2 // LANES
    TB = x0.shape[0] // CH
    W_SLOTS = wf1.shape[0]
    last_tile = pl.num_programs(0) - 1
    nv_cur = nv_ref[i]
    nv_nxt = jnp.where(i + 1 < na, nv_ref[jnp.minimum(i + 1, last_tile)], 0)
    nv_prv = jnp.where(i > 0, nv_ref[jnp.maximum(i - 1, 0)], 0)
    nv_pp = jnp.where(i > 1, nv_ref[jnp.maximum(i - 2, 0)], 0)

    def weight_copies(s):
        slot = lax.rem(s, W_SLOTS)
        e = se_ref[s]
        return [pltpu.make_async_copy(w_hbm.at[e], wf.at[slot], wsem.at[slot])
                for w_hbm, wf in ((w1_hbm, wf1), (w3_hbm, wf3), (w2_hbm, wf2))]

    def token_of(v):
        if n_tokens & (n_tokens - 1) == 0:
            return v & (n_tokens - 1)
        return lax.rem(v, n_tokens)

    def tile_of(row):
        return pl.ds(pl.multiple_of(row * CH, CH), CH)

    def gather(tgt, xbuf, s, r):
        return pltpu.make_async_copy(u_hbm.at[tile_of(token_of(tgt[r]))], xbuf.at[tile_of(r)], gsem.at[s])

    def scatter(ybuf, s, r, dst):
        return pltpu.make_async_copy(ybuf.at[tile_of(r)], out_hbm.at[tile_of(dst)], ssem.at[s])

    def wait_tokens(count, make):
        groups = count // WAIT_GROUP

        def grp(g, carry):
            make(WAIT_GROUP).wait()
            return carry

        lax.fori_loop(0, groups, grp, 0)

        def one(g, carry):
            make(1).wait()
            return carry

        lax.fori_loop(groups * WAIT_GROUP, count, one, 0)

    def wait_gather(xbuf, s, count):
        wait_tokens(count, lambda n: pltpu.make_async_copy(u_hbm.at[pl.ds(0, n * CH)], xbuf.at[pl.ds(0, n * CH)],
                                                             gsem.at[s]))

    def wait_scatter(ybuf, s, count):
        wait_tokens(count, lambda n: pltpu.make_async_copy(ybuf.at[pl.ds(0, n * CH)], out_hbm.at[pl.ds(0, n * CH)],
                                                             ssem.at[s]))

    def each_row(count, fn):
        def body(r, carry):
            fn(r)
            return carry
        lax.fori_loop(0, count, body, 0)

    def step(p):
        q = 1 - p
        x_cur, x_nxt = (x0, x1) if p == 0 else (x1, x0)
        y_cur, y_prv = (y0, y1) if p == 0 else (y1, y0)
        wait_gather(x_cur, p, nv_cur)
        wait_scatter(y_cur, p, nv_pp)

        x = _unpack_bf16_pairs(_load_token_tiles(x_cur, 0, TB, CH)).astype(BF16)
        for r in range(TB):
            @pl.when(r < nv_nxt)
            def _():
                gather(tgt_nxt_ref, x_nxt, q, r).start()
        for r in range(TB):
            @pl.when(r < nv_prv)
            def _():
                scatter(y_prv, q, r, tgt_prv_ref[r]).start()
        h = _silu(jnp.dot(x, w1b[...], preferred_element_type=F32)) * jnp.dot(x, w3b[...], preferred_element_type=F32)
        _store_token_tiles(y_cur, 0, _pack_bf16_pairs(jnp.dot(h.astype(BF16), w2b[...], preferred_element_type=F32)))

        @pl.when(i == na - 1)
        def _():
            wait_scatter(y_prv, q, nv_prv)
            each_row(nv_cur, lambda r: scatter(y_cur, p, r, tgt_ref[r]).start())
            wait_scatter(y_cur, p, nv_cur)

    @pl.when(i < na)
    def _():
        @pl.when(i == 0)
        def _():
            x0[...] = jnp.zeros_like(x0)
            x1[...] = jnp.zeros_like(x1)
            each_row(nv_cur, lambda r: gather(tgt_ref, x0, 0, r).start())
            for s0 in range(W_SLOTS - 1):
                @pl.when(s0 < ns)
                def _():
                    for c in weight_copies(s0):
                        c.start(priority=WEIGHT_DMA_PRIORITY)

        s = ts_ref[i]

        @pl.when((i == 0) | (s != ts_ref[jnp.maximum(i - 1, 0)]))
        def _():
            @pl.when(s + W_SLOTS - 1 < ns)
            def _():
                for c in weight_copies(s + W_SLOTS - 1):
                    c.start(priority=WEIGHT_DMA_PRIORITY)

            for c in weight_copies(s):
                c.wait()
            slot = lax.rem(s, W_SLOTS)
            w1b[...] = wf1[slot].astype(BF16)
            w3b[...] = wf3[slot].astype(BF16)
            w2b[...] = wf2[slot].astype(BF16)

        parity = lax.rem(i, 2)

        @pl.when(parity == 0)
        def _():
            step(0)

        @pl.when(parity == 1)
        def _():
            step(1)


WEIGHT_SLOTS = 3
WEIGHT_DMA_PRIORITY = 1


def _experts(u, slot_tgt, tile_seq, seq_expert, n_active_seq, tile_valid, w1, w3, w2):
    E, D, de = w1.shape
    CH = D // 2 // LANES
    assert CH % SUBLANES == 0
    T = u.shape[0] // CH
    TB = EXPERT_ROWS
    n_tiles = slot_tgt.shape[0] // TB
    table = lambda f: pl.BlockSpec((TB,), lambda i, ts, se, nn, nv: (f(i),), memory_space=pltpu.SMEM)
    hbm = pl.BlockSpec(memory_space=pl.ANY)
    grid_spec = pltpu.PrefetchScalarGridSpec(
        num_scalar_prefetch=4,
        grid=(n_tiles,),
        in_specs=[table(lambda i: jnp.minimum(i + 1, n_tiles - 1)), table(lambda i: i),
                  table(lambda i: jnp.maximum(i - 1, 0)), hbm, hbm, hbm, hbm],
        out_specs=hbm,
        scratch_shapes=[pltpu.VMEM((TB * CH, LANES), jnp.uint32)] * 4
                       + [pltpu.VMEM((WEIGHT_SLOTS, D, de), F32), pltpu.VMEM((WEIGHT_SLOTS, D, de), F32),
                          pltpu.VMEM((WEIGHT_SLOTS, de, D), F32),
                          pltpu.VMEM((D, de), BF16), pltpu.VMEM((D, de), BF16), pltpu.VMEM((de, D), BF16),
                          pltpu.SemaphoreType.DMA((2,)), pltpu.SemaphoreType.DMA((2,)),
                          pltpu.SemaphoreType.DMA((WEIGHT_SLOTS,))],
    )
    return pl.pallas_call(
        functools.partial(_experts_kernel, n_tokens=T),
        grid_spec=grid_spec,
        out_shape=jax.ShapeDtypeStruct((2 * T * CH, LANES), jnp.uint32),
        compiler_params=_cparams(("arbitrary",), 58),
        name="experts",
    )(tile_seq, seq_expert, n_active_seq, tile_valid, slot_tgt, slot_tgt, slot_tgt, u, w1, w3, w2)


def _slot_table_kernel(dest_ref, init_hbm, o_ref, sem):
    fill = pltpu.make_async_copy(init_hbm, o_ref, sem)
    fill.start()
    fill.wait()

    def put(a, carry):
        o_ref[dest_ref[a]] = a
        return carry

    lax.fori_loop(0, dest_ref.shape[0], put, 0, unroll=16)


def _slot_table(dest, init):
    smem = pl.BlockSpec(memory_space=pltpu.SMEM)
    return pl.pallas_call(
        _slot_table_kernel,
        in_specs=[smem, pl.BlockSpec(memory_space=pl.ANY)],
        out_specs=smem,
        out_shape=jax.ShapeDtypeStruct(init.shape, jnp.int32),
        scratch_shapes=[pltpu.SemaphoreType.DMA],
        name="slot_table",
    )(dest, init)


def _dispatch_tables(route_t, counts, n_experts):
    T = route_t.shape[1]
    TB = EXPERT_ROWS
    e = route_t[ROUTE_EXPERT:ROUTE_EXPERT + 2].astype(jnp.int32)
    rank = route_t[ROUTE_RANK:ROUTE_RANK + 2].astype(jnp.int32)
    cnt = counts[0, :n_experts].astype(jnp.int32)
    padded = (cnt + TB - 1) // TB * TB
    pad_end = jnp.cumsum(padded)
    pad_start = pad_end - padded
    ids = jnp.arange(n_experts, dtype=jnp.int32)[:, None, None]
    start_of = jnp.sum(jnp.where(e[None] == ids, pad_start[:, None, None], 0), axis=0)
    dest = start_of + rank
    n_tiles = -(-(2 * T + n_experts * (TB - 1)) // TB)
    slot_tgt = _slot_table(dest.reshape(-1), jnp.zeros((n_tiles * TB,), jnp.int32))
    start = jnp.arange(n_tiles, dtype=jnp.int32) * TB
    tile_expert = jnp.sum(start[:, None] >= pad_end[None, :], axis=1)
    tile_expert = jnp.minimum(tile_expert, n_experts - 1).astype(jnp.int32)
    owns = cnt > 0
    pos = jnp.cumsum(owns.astype(jnp.int32)) - 1
    experts = jnp.arange(n_experts, dtype=jnp.int32)
    seq_expert = jnp.sum(jnp.where(owns[None, :] & (pos[None, :] == experts[:, None]), experts[None, :], 0), axis=1)
    tile_seq = jnp.sum(jnp.where(tile_expert[:, None] == experts[None, :], pos[None, :], 0), axis=1)
    n_active_seq = jnp.stack([pad_end[-1] // TB, jnp.sum(owns.astype(jnp.int32))]).astype(jnp.int32)
    pick = tile_expert[:, None] == experts[None, :]
    in_expert = start - jnp.sum(jnp.where(pick, pad_start[None, :], 0), axis=1)
    tile_cnt = jnp.sum(jnp.where(pick, cnt[None, :], 0), axis=1)
    tile_valid = jnp.where(start < pad_end[-1], jnp.clip(tile_cnt - in_expert, 0, TB), 0).astype(jnp.int32)
    return slot_tgt, tile_seq.astype(jnp.int32), seq_expert.astype(jnp.int32), n_active_seq, tile_valid


def _combine_kernel(x2_ref, e0_ref, e1_ref, route_ref, w_ref, o_ref, *, final_norm):
    r = route_ref[...]
    tm = x2_ref.shape[0]
    ch = e0_ref.shape[0] // tm
    e0 = _unpack_bf16_pairs(_load_token_tiles(e0_ref, 0, tm, ch))
    e1 = _unpack_bf16_pairs(_load_token_tiles(e1_ref, 0, tm, ch))
    y = x2_ref[...] + r[:, ROUTE_GATE:ROUTE_GATE + 1] * e0 + r[:, ROUTE_GATE + 1:ROUTE_GATE + 2] * e1
    if final_norm:
        y = y * lax.rsqrt(jnp.mean(y * y, axis=-1, keepdims=True) + EPS) * w_ref[...]
    o_ref[...] = y


def _combine(x2, planes, route, norm_w, final_norm, tm=512):
    M, D = x2.shape
    return pl.pallas_call(
        functools.partial(_combine_kernel, final_norm=final_norm),
        grid=(M // tm,),
        in_specs=[pl.BlockSpec((tm, D), lambda i: (i, 0)),
                  pl.BlockSpec((tm * (D // 2 // LANES), LANES), lambda i: (i, 0)),
                  pl.BlockSpec((tm * (D // 2 // LANES), LANES), lambda i: (M // tm + i, 0)),
                  pl.BlockSpec((tm, LANES), lambda i: (i, 0)),
                  pl.BlockSpec((1, D), lambda i: (0, 0))],
        out_specs=pl.BlockSpec((tm, D), lambda i: (i, 0)),
        out_shape=jax.ShapeDtypeStruct((M, D), F32),
        compiler_params=_cparams(("parallel",), 48),
        name="combine",
    )(x2, planes, planes, route, norm_w.reshape(1, D))


def kernel(x, norm1_w, w_in, hy_conv_w, hy_conv_b, hy_filt_w1, hy_filt_b1, hy_filt_w2, hy_filt_b2, hy_filt_w3, hy_filt_b3, hy_filt_w4, hy_sin_freq, hy_skip, hy_norm_w, gdn_conv_w, gdn_a_log_f, gdn_a_log_b, gdn_dt_bias_f, gdn_dt_bias_b, gdn_norm_w, w_out, norm2_w, router_group_w, router_group_b, router_expert_w, router_expert_b, exp_w1, exp_w3, exp_w2, final_norm_w):
    B, L, D = x.shape
    M = B * L
    depth = w_in.shape[0]
    d_hy = hy_skip.shape[-1]
    H = gdn_a_log_f.shape[-1]
    Dh = gdn_norm_w.shape[-1]
    d_gdn = H * Dh
    n_main = 3 * d_hy + 4 * d_gdn
    G = router_group_w.shape[-1]
    E = router_expert_w.shape[-1]
    xf = x.reshape(M, D)
    for l in range(depth):
        p, pg = _inproj(xf, norm1_w[l], jnp.swapaxes(w_in[l], 0, 1), n_main, 4 * H)
        p3 = p.reshape(B, L, n_main)
        x0c, yconv = _hyena_branch(p3, hy_conv_w[l], hy_conv_b[l], hy_filt_w1[l], hy_filt_b1[l], hy_filt_w2[l],
                                   hy_filt_b2[l], hy_filt_w3[l], hy_filt_b3[l], hy_filt_w4[l], hy_sin_freq[l],
                                   hy_skip[l], d_hy)
        o_f, o_b = _gdn_branch(p3, pg, gdn_conv_w[l], gdn_a_log_f[l], gdn_a_log_b[l], gdn_dt_bias_f[l],
                               gdn_dt_bias_b[l], 3 * d_hy, H, Dh)
        wr = jnp.pad(jnp.concatenate([router_group_w[l], router_expert_w[l]], axis=1), ((0, 0), (0, LANES - G - E)))
        br = jnp.pad(jnp.concatenate([router_group_b[l], router_expert_b[l]]), (0, LANES - G - E)).reshape(1, LANES)
        x2, u, route, route_t, counts = _mix_route(
            yconv.reshape(M, d_hy), x0c.reshape(M, d_hy), o_f.reshape(M, d_gdn), o_b.reshape(M, d_gdn), p,
            3 * d_hy + 3 * d_gdn, hy_norm_w[l], gdn_norm_w[l], xf, w_out[l].astype(BF16), norm2_w[l], wr, br,
            H, Dh, G, E // G)
        slot_tgt, tile_seq, seq_expert, n_active_seq, tile_valid = _dispatch_tables(route_t, counts, E)
        planes = _experts(u, slot_tgt, tile_seq, seq_expert, n_active_seq, tile_valid, exp_w1[l], exp_w3[l],
                          exp_w2[l])
        xf = _combine(x2, planes, route, final_norm_w, final_norm=(l == depth - 1))
    return xf.reshape(B, L, D)
```

```python
import functools
import math

import jax
import jax.numpy as jnp
import numpy as np
from jax import lax
from jax.experimental import pallas as pl
from jax.experimental.pallas import tpu as pltpu

F32 = jnp.float32
BF16 = jnp.bfloat16
EPS = 1e-6
LANES = 128
SUBLANES = 8
VMEM_BYTES_V7X = 64 * 1024 * 1024
GDN_CHUNK = 64
FFT_N2 = 128
EXPERT_ROWS = 256
DECAY_TARGET = 1e-2
FAST_DECAY_PCT = 0.3
SLOW_DECAY_PCT = 1.5


def _cparams(sem, vmem_mib):
    limit = int(vmem_mib * 1024 * 1024)
    assert limit < VMEM_BYTES_V7X
    return pltpu.CompilerParams(dimension_semantics=sem, vmem_limit_bytes=limit)


def _dot(a, b):
    return jnp.dot(a.astype(BF16), b.astype(BF16), preferred_element_type=F32)


def _dot_nt(a, b):
    return lax.dot_general(a.astype(BF16), b.astype(BF16), (((1,), (1,)), ((), ())), preferred_element_type=F32)


def _dot_tn(a, b):
    return lax.dot_general(a.astype(BF16), b.astype(BF16), (((0,), (0,)), ((), ())), preferred_element_type=F32)


def _dot_hi(a, b):
    return jnp.dot(a, b, preferred_element_type=F32, precision=lax.Precision.HIGHEST)


def _silu(x):
    return x * jax.nn.sigmoid(x)


def _pack_bf16_pairs(x):
    c = x.shape[1] // 2
    lo = lax.bitcast_convert_type(x[:, :c].astype(BF16).astype(F32), jnp.uint32) >> 16
    hi = lax.bitcast_convert_type(x[:, c:].astype(BF16).astype(F32), jnp.uint32) & jnp.uint32(0xFFFF0000)
    return hi | lo


def _unpack_bf16_pairs(w):
    lo = lax.bitcast_convert_type(w << 16, F32)
    hi = lax.bitcast_convert_type(w & jnp.uint32(0xFFFF0000), F32)
    return jnp.concatenate([lo, hi], axis=1)


def _store_token_tiles(ref, row0, words):
    n, width = words.shape
    ch = width // LANES
    for c in range(ch):
        ref[pl.ds(row0 * ch + c, n, stride=ch), :] = words[:, c * LANES:(c + 1) * LANES]


def _load_token_tiles(ref, row0, n, ch):
    return jnp.concatenate([ref[pl.ds(row0 * ch + c, n, stride=ch), :] for c in range(ch)], axis=1)


def _inproj_kernel(x_hbm, nw_ref, wt_ref, wgt_ref, p_ref, g_ref, h_scr, x_scr, x_sem):
    i = pl.program_id(0)
    j = pl.program_id(1)
    tm = x_scr.shape[0]

    def x_copy(tile):
        return pltpu.make_async_copy(x_hbm.at[pl.ds(pl.multiple_of(tile * tm, tm), tm)], x_scr, x_sem)

    @pl.when((i == 0) & (j == 0))
    def _():
        x_copy(0).start()

    @pl.when((j == 1) & (i + 1 < pl.num_programs(0)))
    def _():
        x_copy(i + 1).start()

    @pl.when(j == 0)
    def _():
        x_copy(i).wait()
        x = x_scr[...]
        h = x * lax.rsqrt(jnp.mean(x * x, axis=-1, keepdims=True) + EPS) * nw_ref[...]
        h_scr[...] = h.astype(BF16)
        g = _dot_nt(h, wgt_ref[...])
        g_ref[...] = jnp.concatenate([g, jnp.zeros((g.shape[0], LANES - g.shape[1]), F32)], axis=1)

    p_ref[...] = _dot_nt(h_scr[...], wt_ref[...]).astype(p_ref.dtype)


def _inproj(xf, norm_w, wt, n_main, n_gate, tm=2048, tn=512):
    M, D = xf.shape
    assert n_main % tn == 0 and M % tm == 0 and n_main % n_gate == 0 and n_gate % SUBLANES == 0
    assert n_main // tn >= 2
    return pl.pallas_call(
        _inproj_kernel,
        grid=(M // tm, n_main // tn),
        in_specs=[
            pl.BlockSpec(memory_space=pl.ANY),
            pl.BlockSpec((1, D), lambda i, j: (0, 0)),
            pl.BlockSpec((tn, D), lambda i, j: (j, 0)),
            pl.BlockSpec((n_gate, D), lambda i, j: (n_main // n_gate, 0)),
        ],
        out_specs=[
            pl.BlockSpec((tm, tn), lambda i, j: (i, j)),
            pl.BlockSpec((tm, LANES), lambda i, j: (i, 0)),
        ],
        out_shape=[jax.ShapeDtypeStruct((M, n_main), BF16), jax.ShapeDtypeStruct((M, LANES), F32)],
        scratch_shapes=[pltpu.VMEM((tm, D), BF16), pltpu.VMEM((tm, D), F32), pltpu.SemaphoreType.DMA],
        compiler_params=_cparams(("arbitrary", "arbitrary"), 57),
        name="inproj",
    )(xf, norm_w.reshape(1, D), wt, wt)


def _conv3_rows(ref, r0, rows, w, n_rows):
    cur = ref[pl.ds(r0, rows), :].astype(F32)
    lo = jnp.maximum(r0 - HALO_ROWS, 0)
    hi = jnp.minimum(r0 + rows, n_rows - HALO_ROWS)
    prev_grp = ref[pl.ds(pl.multiple_of(lo, HALO_ROWS), HALO_ROWS), :].astype(F32)
    next_grp = ref[pl.ds(pl.multiple_of(hi, HALO_ROWS), HALO_ROWS), :].astype(F32)
    prev_row = jnp.where(r0 > 0, prev_grp[HALO_ROWS - 1:HALO_ROWS, :], 0.0)
    next_row = jnp.where(r0 + rows < n_rows, next_grp[0:1, :], 0.0)
    row = lax.broadcasted_iota(jnp.int32, cur.shape, 0)
    xm = jnp.where(row == 0, prev_row, pltpu.roll(cur, 1, 0))
    xp = jnp.where(row == rows - 1, next_row, pltpu.roll(cur, rows - 1, 0))
    return xm * w[0:1, :] + cur * w[1:2, :] + xp * w[2:3, :]


CONV_ROWS = 256
HALO_ROWS = 16


def _hy_pre_kernel(x0_ref, x1_ref, v_ref, w0_ref, w1_ref, w2_ref, b0_ref, b1_ref, b2_ref, x0c_ref, vg_ref):
    L = x0_ref.shape[0]
    w0, w1, w2 = w0_ref[...], w1_ref[...], w2_ref[...]
    b0, b1, b2 = b0_ref[...], b1_ref[...], b2_ref[...]

    def body(c, carry):
        r0 = pl.multiple_of(c * CONV_ROWS, CONV_ROWS)
        x0c_ref[pl.ds(r0, CONV_ROWS), :] = (_conv3_rows(x0_ref, r0, CONV_ROWS, w0, L) + b0).astype(x0c_ref.dtype)
        x1c = _conv3_rows(x1_ref, r0, CONV_ROWS, w1, L) + b1
        vc = _conv3_rows(v_ref, r0, CONV_ROWS, w2, L) + b2
        vg_ref[pl.ds(r0, CONV_ROWS), :] = (vc * x1c).astype(vg_ref.dtype)
        return carry

    lax.fori_loop(0, L // CONV_ROWS, body, 0, unroll=2)


def _hy_pre(p3, conv_w, conv_b, d_hy):
    B, L, _ = p3.shape
    nt = d_hy // LANES
    bias = conv_b.reshape(1, -1)
    pspec = lambda off: pl.BlockSpec((None, L, LANES), lambda b, c: (b, 0, c + off))
    wspec = lambda off: pl.BlockSpec((3, LANES), lambda b, c: (0, c + off))
    bspec = lambda off: pl.BlockSpec((1, LANES), lambda b, c: (0, c + off))
    ospec = pl.BlockSpec((None, L, LANES), lambda b, c: (b, 0, c))
    return pl.pallas_call(
        _hy_pre_kernel,
        grid=(B, nt),
        in_specs=[pspec(0), pspec(nt), pspec(2 * nt), wspec(0), wspec(nt), wspec(2 * nt),
                  bspec(0), bspec(nt), bspec(2 * nt)],
        out_specs=[ospec, ospec],
        out_shape=[jax.ShapeDtypeStruct((B, L, d_hy), BF16)] * 2,
        compiler_params=_cparams(("parallel", "parallel"), 40),
        name="hy_pre",
    )(p3, p3, p3, conv_w, conv_w, conv_w, bias, bias, bias)


def _gdn_pre_kernel(x_ref, w_ref, o_ref, *, n_heads, head_dim):
    L = x_ref.shape[0]
    w = w_ref[...]
    c = pl.program_id(1)
    q_scale = jnp.where(c < n_heads, head_dim ** -0.5, 1.0)
    is_qk = c < 2 * n_heads

    def body(i, carry):
        r0 = pl.multiple_of(i * CONV_ROWS, CONV_ROWS)
        y = _silu(_conv3_rows(x_ref, r0, CONV_ROWS, w, L))
        inv = lax.rsqrt(jnp.sum(y * y, axis=-1, keepdims=True) + EPS) * q_scale
        o_ref[pl.ds(r0, CONV_ROWS), :] = (y * jnp.where(is_qk, inv, 1.0)).astype(o_ref.dtype)
        return carry

    lax.fori_loop(0, L // CONV_ROWS, body, 0, unroll=2)


def _gdn_pre(p3, conv_w, col0, n_heads, head_dim):
    B, L, _ = p3.shape
    assert head_dim == LANES
    nt = 3 * n_heads
    off = col0 // LANES
    return pl.pallas_call(
        functools.partial(_gdn_pre_kernel, n_heads=n_heads, head_dim=head_dim),
        grid=(B, nt),
        in_specs=[pl.BlockSpec((None, L, LANES), lambda b, c: (b, 0, c + off)),
                  pl.BlockSpec((3, LANES), lambda b, c: (0, c))],
        out_specs=pl.BlockSpec((None, L, LANES), lambda b, c: (b, 0, c)),
        out_shape=jax.ShapeDtypeStruct((B, L, nt * LANES), BF16),
        compiler_params=_cparams(("parallel", "parallel"), 24),
        name="gdn_pre",
    )(p3, conv_w)


GATE_ROWS = 512


def _gdn_gates_kernel(pg_ref, alog_ref, dtb_ref, o_ref, *, n_heads):
    H = n_heads
    x = pg_ref[...]
    beta = jax.nn.sigmoid(x)
    z = x + dtb_ref[...]
    softplus = jnp.maximum(z, 0.0) + jnp.log1p(jnp.exp(-jnp.abs(z)))
    g = -jnp.exp(alog_ref[...]) * softplus
    pos = lax.broadcasted_iota(jnp.int32, x.shape, 0) & (GDN_CHUNK - 1)
    gc_f = g
    gc_b = g
    step = 1
    while step < GDN_CHUNK:
        gc_f = gc_f + jnp.where(pos >= step, pltpu.roll(gc_f, step, 0), 0.0)
        gc_b = gc_b + jnp.where(pos < GDN_CHUNK - step, pltpu.roll(gc_b, GATE_ROWS - step, 0), 0.0)
        step *= 2
    g_tot = pltpu.roll(gc_f + gc_b - g, 4 * H, 1)
    lane = lax.broadcasted_iota(jnp.int32, x.shape, 1)
    out = jnp.where(lane < 2 * H, beta,
                    jnp.where(lane < 3 * H, gc_f,
                              jnp.where(lane < 4 * H, gc_b,
                                        jnp.where((lane >= 6 * H) & (lane < 8 * H), g_tot, 0.0))))
    o_ref[...] = out


def _gdn_gates(pg, a_log_f, a_log_b, dt_bias_f, dt_bias_b, n_heads):
    M = pg.shape[0]
    H = n_heads
    assert 8 * H <= LANES
    pad = lambda a, b: jnp.concatenate([jnp.zeros((2 * H,), F32), a.astype(F32), b.astype(F32),
                                        jnp.zeros((LANES - 4 * H,), F32)]).reshape(1, LANES)
    return pl.pallas_call(
        functools.partial(_gdn_gates_kernel, n_heads=H),
        grid=(M // GATE_ROWS,),
        in_specs=[pl.BlockSpec((GATE_ROWS, LANES), lambda i: (i, 0)),
                  pl.BlockSpec((1, LANES), lambda i: (0, 0)),
                  pl.BlockSpec((1, LANES), lambda i: (0, 0))],
        out_specs=pl.BlockSpec((GATE_ROWS, LANES), lambda i: (i, 0)),
        out_shape=jax.ShapeDtypeStruct((M, LANES), F32),
        compiler_params=_cparams(("parallel",), 24),
        name="gdn_gates",
    )(pg, pad(a_log_f, a_log_b), pad(dt_bias_f, dt_bias_b))


def _delta_chunks(q, k, v, beta, gc_col, gc_row, gtot, state, lower):
    n = len(q)
    C = q[0].shape[0]
    D = k[0].shape[1]
    ii = lax.broadcasted_iota(jnp.int32, (C, C), 0)
    jj = lax.broadcasted_iota(jnp.int32, (C, C), 1)
    eye = jnp.where(ii == jj, 1.0, 0.0)
    incl = [(ii >= jj) if lo else (ii <= jj) for lo in lower]
    strict = [(ii > jj) if lo else (ii < jj) for lo in lower]
    rng = range(n)
    decay = [jnp.where(incl[i], jnp.exp(jnp.where(incl[i], gc_col[i] - gc_row[i], 0.0)), 0.0) for i in rng]
    kb = [k[i] * beta[i] for i in rng]
    kk = [_dot_nt(kb[i], k[i]) for i in rng]
    qk = [_dot_nt(q[i], k[i]) for i in rng]
    m = [jnp.where(strict[i], -(kk[i] * decay[i]), 0.0) for i in rng]
    r = [eye + m[i] for i in rng]
    m = [_dot(m[i], m[i]) for i in rng]
    for _ in range(int(math.log2(C)) - 2):
        rm = [_dot(jnp.concatenate([r[i], m[i]], axis=0), m[i]) for i in rng]
        r = [r[i] + rm[i][:C] for i in rng]
        m = [rm[i][C:] for i in rng]
    r = [r[i] + _dot(r[i], m[i]) for i in rng]
    eg = [jnp.exp(gc_col[i]) for i in rng]
    wu = [_dot(r[i], jnp.concatenate([kb[i] * eg[i], v[i] * beta[i]], axis=1)) for i in rng]
    ws = [_dot(jnp.concatenate([wu[i][:, :D], q[i] * eg[i]], axis=0), state[i]) for i in rng]
    v_new = [wu[i][:, D:] - ws[i][:C] for i in rng]
    qkm = [jnp.where(incl[i], qk[i] * decay[i], 0.0) for i in rng]
    out = [ws[i][C:] + _dot(qkm[i], v_new[i]) for i in rng]
    k_dec = [k[i] * jnp.exp(gtot[i] - gc_col[i]) for i in rng]
    new_state = [state[i] * jnp.exp(gtot[i][0:1, :]) + _dot_tn(k_dec[i], v_new[i]) for i in rng]
    return out, new_state


def _gdn_scan_kernel(qf_ref, kf_ref, vf_ref, qb_ref, kb_ref, vb_ref, gf_ref, gb_ref, rf_ref, rb_ref,
                     of_ref, ob_ref, s_scr, *, n_heads, head_dim):
    H, Dh = n_heads, head_dim
    B = qf_ref.shape[0]

    @pl.when(pl.program_id(0) == 0)
    def _():
        s_scr[...] = jnp.zeros_like(s_scr)

    col = lambda g, j: g[:, j:j + 1]
    sls = [slice(h * Dh, (h + 1) * Dh) for h in range(H)]
    q, k, v, beta, gc_col, gc_row, gtot, state = ([] for _ in range(8))
    for b in range(B):
        gf = gf_ref[b]
        gb = gb_ref[b]
        q += [qf_ref[b, :, sl] for sl in sls] + [qb_ref[b, :, sl] for sl in sls]
        k += [kf_ref[b, :, sl] for sl in sls] + [kb_ref[b, :, sl] for sl in sls]
        v += [vf_ref[b, :, sl] for sl in sls] + [vb_ref[b, :, sl] for sl in sls]
        beta += [col(gf, h) for h in range(H)] + [col(gb, H + h) for h in range(H)]
        gc_col += [col(gf, 2 * H + h) for h in range(H)] + [col(gb, 3 * H + h) for h in range(H)]
        gc_row += [rf_ref[b, h:h + 1, :] for h in range(H)] + [rb_ref[b, H + h:H + h + 1, :] for h in range(H)]
        gtot += [col(gf, 6 * H + h) for h in range(H)] + [col(gb, 7 * H + h) for h in range(H)]
        state += [s_scr[b, 0, h] for h in range(H)] + [s_scr[b, 1, h] for h in range(H)]
    out, new_state = _delta_chunks(q, k, v, beta, gc_col, gc_row, gtot, state, ([True] * H + [False] * H) * B)
    for b in range(B):
        for h in range(H):
            of_ref[b, :, sls[h]] = out[2 * H * b + h]
            ob_ref[b, :, sls[h]] = out[2 * H * b + H + h]
            s_scr[b, 0, h] = new_state[2 * H * b + h]
            s_scr[b, 1, h] = new_state[2 * H * b + H + h]


def _gdn_scan(qkv, gates, gates_row, n_heads, head_dim):
    B, L, _ = qkv.shape
    H, Dh = n_heads, head_dim
    d = H * Dh
    C = GDN_CHUNK
    N = L // C
    fwd = lambda col: pl.BlockSpec((B, C, d), lambda n: (0, n, col))
    bwd = lambda col: pl.BlockSpec((B, C, d), lambda n: (0, N - 1 - n, col))
    return pl.pallas_call(
        functools.partial(_gdn_scan_kernel, n_heads=H, head_dim=Dh),
        grid=(N,),
        in_specs=[fwd(0), fwd(1), fwd(2), bwd(0), bwd(1), bwd(2),
                  pl.BlockSpec((B, C, LANES), lambda n: (0, n, 0)),
                  pl.BlockSpec((B, C, LANES), lambda n: (0, N - 1 - n, 0)),
                  pl.BlockSpec((B, None, 2 * H, C), lambda n: (0, n, 0, 0)),
                  pl.BlockSpec((B, None, 2 * H, C), lambda n: (0, N - 1 - n, 0, 0))],
        out_specs=[pl.BlockSpec((B, C, d), lambda n: (0, n, 0)),
                   pl.BlockSpec((B, C, d), lambda n: (0, N - 1 - n, 0))],
        out_shape=[jax.ShapeDtypeStruct((B, L, d), F32)] * 2,
        scratch_shapes=[pltpu.VMEM((B, 2, H, Dh, Dh), F32)],
        compiler_params=_cparams(("arbitrary",), 32),
        name="gdn_scan",
    )(qkv, qkv, qkv, qkv, qkv, qkv, gates, gates, gates_row, gates_row)


def _gdn_branch(p3, pg, conv_w, a_log_f, a_log_b, dt_bias_f, dt_bias_b, col0, n_heads, head_dim):
    B, L, _ = p3.shape
    H = n_heads
    qkv = _gdn_pre(p3, conv_w, col0, H, head_dim)
    gates = _gdn_gates(pg, a_log_f, a_log_b, dt_bias_f, dt_bias_b, H).reshape(B, L, LANES)
    N = L // GDN_CHUNK
    gates_row = gates[..., 2 * H:4 * H].reshape(B, N, GDN_CHUNK, 2 * H).transpose(0, 1, 3, 2)
    return _gdn_scan(qkv, gates, gates_row, H, head_dim)


@functools.lru_cache(maxsize=None)
def _filter_positions(L, pos_emb_dim):
    n = 2 * L
    r = np.arange(n)
    k = np.where(r < L, r, np.where(r == L, 0, n - r)).astype(np.float64)
    t = k / (L - 1)
    bands = (pos_emb_dim - 1) // 2
    fb = np.linspace(1e-4, bands - 1, bands)
    ang = (2.0 * math.pi / L) * k[:, None] * fb[None, :]
    z = np.concatenate([t[:, None], np.cos(ang), -np.sin(ang)], axis=-1)
    return z.astype(np.float32)


@functools.lru_cache(maxsize=None)
def _decay_rates(d_hy):
    max_decay = math.log(DECAY_TARGET) / FAST_DECAY_PCT
    min_decay = math.log(DECAY_TARGET) / SLOW_DECAY_PCT
    return np.abs(np.linspace(min_decay, max_decay, d_hy)).astype(np.float32).reshape(1, d_hy)


def _filt_mlp_kernel(zt_ref, w1t_ref, b1_ref, w2t_ref, b2_ref, w3t_ref, b3_ref, fr_ref, o_ref):
    fr = fr_ref[...]
    h = jnp.sin(fr * (_dot_hi(w1t_ref[...], zt_ref[...]) + b1_ref[...]))
    h = jnp.sin(fr * (_dot_hi(w2t_ref[...], h) + b2_ref[...]))
    o_ref[...] = jnp.sin(fr * (_dot_hi(w3t_ref[...], h) + b3_ref[...]))


def _filt_mlp(zt, w1, b1, w2, b2, w3, b3, freq, tc=1024):
    pe, n = zt.shape
    fw = w1.shape[1]
    col = lambda a: a.reshape(-1, 1).astype(F32)
    full = lambda a: pl.BlockSpec(a.shape, lambda i: (0, 0))
    args = (zt, w1.T, col(b1), w2.T, col(b2), w3.T, col(b3), col(freq))
    return pl.pallas_call(
        _filt_mlp_kernel,
        grid=(n // tc,),
        in_specs=[pl.BlockSpec((pe, tc), lambda i: (0, i))] + [full(a) for a in args[1:]],
        out_specs=pl.BlockSpec((fw, tc), lambda i: (0, i)),
        out_shape=jax.ShapeDtypeStruct((fw, n), F32),
        compiler_params=_cparams(("parallel",), 24),
        name="filt_mlp",
    )(*args)


@functools.lru_cache(maxsize=None)
def _dft_tables(L):
    n = 2 * L
    N2 = FFT_N2
    N1 = n // N2
    N1h = N1 // 2
    j2 = np.arange(N2)[:, None, None]
    k1 = np.arange(N1)[None, :, None]

    def stage1(n_j1):
        j1 = np.arange(n_j1)[None, None, :]
        m = (k1 * (N2 * j1 + j2)) % n
        th = 2.0 * np.pi * m / n
        return np.cos(th), np.sin(th)

    c, s = stage1(N1h)
    t1 = np.concatenate([np.concatenate([c, s], axis=2), np.concatenate([-s, c], axis=2)], axis=1)
    c, s = stage1(N1)
    t1g = np.concatenate([c, -s], axis=1)
    c, s = stage1(N1h)
    ct, st = np.swapaxes(c, 1, 2) / n, np.swapaxes(s, 1, 2) / n
    t2 = np.concatenate([np.concatenate([ct, -st], axis=2), np.concatenate([st, ct], axis=2)], axis=1)
    a = np.arange(N2)
    th = 2.0 * np.pi * ((a[:, None] * a[None, :]) % N2) / N2
    c2, s2 = np.cos(th), np.sin(th)
    f2f = np.block([[c2, s2], [-s2, c2]])
    f2i = np.block([[c2, -s2], [s2, c2]])
    return dict(N1=N1, N2=N2, t1=t1.astype(np.float32), t1g=t1g.astype(np.float32), t2=t2.astype(np.float32),
                f2f=f2f.astype(np.float32), f2i=f2i.astype(np.float32))


FILT_ROWS = 512
FFT_UNROLL = 8
PITCH_PAD = 8
K1_GROUP = 2


def _filt_fft_kernel(h3_ref, w4f_ref, w4b_ref, delta_ref, t1g_ref, f2f_ref, hspec_ref, hb0_ref, g_scr, a_scr,
                     *, L, N1, N2):
    n = 2 * L
    gp = N2 + PITCH_PAD
    ap = 2 * N1 + PITCH_PAD
    delta = delta_ref[...]
    hb0_ref[...] = jnp.zeros_like(hb0_ref)

    def gen(c, carry):
        r0 = pl.multiple_of(c * FILT_ROWS, FILT_ROWS)
        row = r0 + lax.broadcasted_iota(jnp.int32, (FILT_ROWS, LANES), 0)
        lag = jnp.where(row < L, row, jnp.where(row == L, 0, n - row))
        window = jnp.exp(-(lag.astype(F32) * (1.0 / (L - 1))) * delta)
        w4 = jnp.where(r0 < L, w4f_ref[...], w4b_ref[...])
        g = _dot(h3_ref[pl.ds(r0, FILT_ROWS), :], w4) * window
        at_l = row == L
        hb0_ref[...] += jnp.sum(jnp.where(at_l, g, 0.0), axis=0, keepdims=True)
        g = jnp.where(at_l, 0.0, g)
        for q in range(FILT_ROWS // N2):
            dst = pl.multiple_of((c * (FILT_ROWS // N2) + q) * gp, SUBLANES)
            g_scr[pl.ds(dst, N2), :] = g[q * N2:(q + 1) * N2]
        return carry

    lax.fori_loop(0, n // FILT_ROWS, gen, 0)

    def stage1(j2, carry):
        x = g_scr[pl.ds(j2, N1, stride=gp), :]
        a_scr[pl.ds(pl.multiple_of(j2 * ap, SUBLANES), 2 * N1), :] = _dot(t1g_ref[j2], x)
        return carry

    lax.fori_loop(0, N2, stage1, 0, unroll=4 * FFT_UNROLL)

    def stage2(kp, carry):
        k1s = [kp * K1_GROUP + t for t in range(K1_GROUP)]
        x = jnp.concatenate([jnp.concatenate([a_scr[pl.ds(k1, N2, stride=ap), :],
                                              a_scr[pl.ds(N1 + k1, N2, stride=ap), :]], axis=0) for k1 in k1s], axis=1)
        z = _dot(f2f_ref[...], x).astype(hspec_ref.dtype)
        for t, k1 in enumerate(k1s):
            hspec_ref[pl.ds(pl.multiple_of(k1 * 2 * N2, 2 * N2), 2 * N2), :] = z[:, t * LANES:(t + 1) * LANES]
        return carry

    lax.fori_loop(0, N1 // K1_GROUP, stage2, 0, unroll=2 * FFT_UNROLL)


def _filt_fft(h3, w4, L, d_hy):
    tb = _dft_tables(L)
    N1, N2 = tb["N1"], tb["N2"]
    n = 2 * L
    fw = h3.shape[1]
    nt = d_hy // LANES
    t1g = jnp.asarray(tb["t1g"]).astype(BF16)
    f2f = jnp.asarray(tb["f2f"]).astype(BF16)
    return pl.pallas_call(
        functools.partial(_filt_fft_kernel, L=L, N1=N1, N2=N2),
        grid=(nt,),
        in_specs=[pl.BlockSpec((n, fw), lambda c: (0, 0)),
                  pl.BlockSpec((fw, LANES), lambda c: (0, c)),
                  pl.BlockSpec((fw, LANES), lambda c: (0, c + nt)),
                  pl.BlockSpec((1, LANES), lambda c: (0, c)),
                  pl.BlockSpec(t1g.shape, lambda c: (0, 0, 0)),
                  pl.BlockSpec(f2f.shape, lambda c: (0, 0))],
        out_specs=[pl.BlockSpec((2 * n, LANES), lambda c: (0, c)),
                   pl.BlockSpec((SUBLANES, LANES), lambda c: (0, c))],
        out_shape=[jax.ShapeDtypeStruct((2 * n, d_hy), BF16), jax.ShapeDtypeStruct((SUBLANES, d_hy), F32)],
        scratch_shapes=[pltpu.VMEM((N1 * (N2 + PITCH_PAD), LANES), F32),
                        pltpu.VMEM((N2 * (2 * N1 + PITCH_PAD), LANES), F32)],
        compiler_params=_cparams(("parallel",), 48),
        name="filt_fft",
    )(h3, w4, w4, jnp.asarray(_decay_rates(d_hy)), t1g, f2f)


def _hy_conv_kernel(vg_ref, hspec_ref, skip_ref, hb0_ref, t1_ref, f2f_ref, f2i_ref, t2_ref, y_ref,
                    x_scr, a_scr, b_scr, *, N1, N2):
    N1h = N1 // 2
    xp = N2 + PITCH_PAD
    ap = 2 * N1 + PITCH_PAD
    bp = 2 * N2 + PITCH_PAD

    for b in range(2):
        for j1 in range(N1h):
            x_scr[b, pl.ds(j1 * xp, N2), :] = vg_ref[b, pl.ds(j1 * N2, N2), :].astype(F32)

    def stage1(j2, carry):
        x = jnp.concatenate([x_scr[0, pl.ds(j2, N1h, stride=xp), :],
                             x_scr[1, pl.ds(j2, N1h, stride=xp), :]], axis=0)
        a_scr[pl.ds(pl.multiple_of(j2 * ap, SUBLANES), 2 * N1), :] = _dot(t1_ref[j2], x)
        return carry

    lax.fori_loop(0, N2, stage1, 0, unroll=4 * FFT_UNROLL)

    def stage2(kp, carry):
        k1s = [kp * K1_GROUP + t for t in range(K1_GROUP)]
        x = jnp.concatenate([jnp.concatenate([a_scr[pl.ds(k1, N2, stride=ap), :],
                                              a_scr[pl.ds(N1 + k1, N2, stride=ap), :]], axis=0) for k1 in k1s], axis=1)
        z = _dot(f2f_ref[...], x)
        zr, zi = z[:N2], z[N2:]
        hs = [hspec_ref[pl.ds(pl.multiple_of(k1 * 2 * N2, 2 * N2), 2 * N2), :].astype(F32) for k1 in k1s]
        hr = jnp.concatenate([h[:N2] for h in hs], axis=1)
        hi = jnp.concatenate([h[N2:] for h in hs], axis=1)
        prod = jnp.concatenate([zr * hr - zi * hi, zr * hi + zi * hr], axis=0)
        b = _dot(f2i_ref[...], prod)
        for t, k1 in enumerate(k1s):
            b_scr[pl.ds(pl.multiple_of(k1 * bp, SUBLANES), 2 * N2), :] = b[:, t * LANES:(t + 1) * LANES]
        return carry

    lax.fori_loop(0, N1 // K1_GROUP, stage2, 0, unroll=2 * FFT_UNROLL)

    skip = skip_ref[...] + hb0_ref[0:1, :]

    def stage3(j2, carry):
        b = jnp.concatenate([b_scr[pl.ds(j2, N1, stride=bp), :],
                             b_scr[pl.ds(N2 + j2, N1, stride=bp), :]], axis=0)
        y = _dot(t2_ref[j2], b)
        x_scr[0, pl.ds(j2, N1h, stride=xp), :] = y[:N1h] + x_scr[0, pl.ds(j2, N1h, stride=xp), :] * skip
        x_scr[1, pl.ds(j2, N1h, stride=xp), :] = y[N1h:] + x_scr[1, pl.ds(j2, N1h, stride=xp), :] * skip
        return carry

    lax.fori_loop(0, N2, stage3, 0, unroll=4 * FFT_UNROLL)

    for b in range(2):
        for j1 in range(N1h):
            y_ref[b, pl.ds(j1 * N2, N2), :] = x_scr[b, pl.ds(j1 * xp, N2), :]


def _hy_conv(vg, hspec, skip, hb0):
    B, L, d_hy = vg.shape
    assert B % 2 == 0
    tb = _dft_tables(L)
    N1, N2 = tb["N1"], tb["N2"]
    n = 2 * L
    nt = d_hy // LANES
    t1, t2 = (jnp.asarray(tb[k]).astype(BF16) for k in ("t1", "t2"))
    f2f, f2i = (jnp.asarray(tb[k]).astype(BF16) for k in ("f2f", "f2i"))
    const3 = lambda a: pl.BlockSpec(a.shape, lambda b, c: (0, 0, 0))
    const2 = lambda a: pl.BlockSpec(a.shape, lambda b, c: (0, 0))
    return pl.pallas_call(
        functools.partial(_hy_conv_kernel, N1=N1, N2=N2),
        grid=(B // 2, nt),
        in_specs=[pl.BlockSpec((2, L, LANES), lambda b, c: (b, 0, c)),
                  pl.BlockSpec((2 * n, LANES), lambda b, c: (0, c)),
                  pl.BlockSpec((1, LANES), lambda b, c: (0, c)),
                  pl.BlockSpec((SUBLANES, LANES), lambda b, c: (0, c)),
                  const3(t1), const2(f2f), const2(f2i), const3(t2)],
        out_specs=pl.BlockSpec((2, L, LANES), lambda b, c: (b, 0, c)),
        out_shape=jax.ShapeDtypeStruct((B, L, d_hy), F32),
        scratch_shapes=[pltpu.VMEM((2, (N1 // 2) * (N2 + PITCH_PAD), LANES), F32),
                        pltpu.VMEM((N2 * (2 * N1 + PITCH_PAD), LANES), F32),
                        pltpu.VMEM((N1 * (2 * N2 + PITCH_PAD), LANES), F32)],
        compiler_params=_cparams(("parallel", "parallel"), 58),
        name="hy_conv",
    )(vg, hspec, skip.reshape(1, d_hy).astype(F32), hb0, t1, f2f, f2i, t2)


def _hyena_branch(p3, conv_w, conv_b, fw1, fb1, fw2, fb2, fw3, fb3, fw4, freq, skip, d_hy):
    B, L, _ = p3.shape
    x0c, vg = _hy_pre(p3, conv_w, conv_b, d_hy)
    zt = jnp.asarray(_filter_positions(L, fw1.shape[0]).T)
    h3 = _filt_mlp(zt, fw1, fb1, fw2, fb2, fw3, fb3, freq).T
    hspec, hb0 = _filt_fft(h3, fw4, L, d_hy)
    return x0c, _hy_conv(vg, hspec, skip, hb0)


ROUTE_GATE, ROUTE_EXPERT, ROUTE_RANK = 0, 2, 4
MIX_ROWS = 256


def _mix_route_kernel(yc_ref, x0_ref, of_ref, ob_ref, z_ref, hnw_ref, gnw_ref, x_ref, wo_ref, n2w_ref, wr_ref, br_ref,
                      x2_ref, u_ref, route_ref, route_t_ref, cnt_ref, run_scr, *, n_heads, head_dim, n_groups,
                      per_group):
    tm = x_ref.shape[0]
    G, P = n_groups, per_group

    @pl.when(pl.program_id(0) == 0)
    def _():
        run_scr[...] = jnp.zeros_like(run_scr)

    run = run_scr[0:1, :]
    blocks = [pl.ds(s * MIX_ROWS, MIX_ROWS) for s in range(tm // MIX_ROWS)]
    us = []
    for blk, rs in enumerate(blocks):
        yh = yc_ref[rs, :] * x0_ref[rs, :].astype(F32)
        yh = yh * lax.rsqrt(jnp.mean(yh * yh, axis=-1, keepdims=True) + EPS) * hnw_ref[...]
        parts = [yh.astype(BF16)]
        for h in range(n_heads):
            sl = slice(h * head_dim, (h + 1) * head_dim)
            o = of_ref[rs, sl] + ob_ref[rs, sl]
            z = z_ref[rs, sl].astype(F32)
            o = o * lax.rsqrt(jnp.mean(o * o, axis=-1, keepdims=True) + EPS) * gnw_ref[...] * _silu(z)
            parts.append(o.astype(BF16))
        ymix = jnp.concatenate(parts, axis=-1)
        x2 = x_ref[rs, :] + jnp.dot(ymix, wo_ref[...], preferred_element_type=F32)
        x2_ref[rs, :] = x2
        u = x2 * lax.rsqrt(jnp.mean(x2 * x2, axis=-1, keepdims=True) + EPS) * n2w_ref[...]
        _store_token_tiles(u_ref, blk * MIX_ROWS, _pack_bf16_pairs(u))
        us.append(u)

    all_logits = [_dot(u, wr_ref[...]) + br_ref[...] for u in us]
    for s, (rs, logits) in enumerate(zip(blocks, all_logits)):
        lane = lax.broadcasted_iota(jnp.int32, logits.shape, 1)
        neg = jnp.float32(-jnp.inf)
        big = jnp.int32(4 * LANES)
        first = lambda hit: jnp.min(jnp.where(hit, lane, big), axis=-1, keepdims=True)
        gl = jnp.where(lane < G, logits, neg)
        gmax = jnp.max(gl, axis=-1, keepdims=True)
        gidx = first(gl == gmax)
        grp_gate = 1.0 / jnp.sum(jnp.exp(gl - gmax), axis=-1, keepdims=True)
        in_grp = (lane >= G) & (lane < G + G * P) & (((lane - G) // P) == gidx)
        ll = jnp.where(in_grp, logits, neg)
        m1 = jnp.max(ll, axis=-1, keepdims=True)
        i1 = first(ll == m1)
        denom = jnp.sum(jnp.exp(ll - m1), axis=-1, keepdims=True)
        ll2 = jnp.where(lane == i1, neg, ll)
        m2 = jnp.max(ll2, axis=-1, keepdims=True)
        i2 = first(ll2 == m2)
        p1 = 1.0 / denom
        p2 = jnp.exp(m2 - m1) / denom
        gate1 = grp_gate * (p1 / (p1 + p2))
        gate2 = grp_gate * (p2 / (p1 + p2))
        e1 = i1 - G
        e2 = i2 - G

        oh1 = jnp.where(lane == e1, 1.0, 0.0)
        oh2 = jnp.where(lane == e2, 1.0, 0.0)
        oh = oh1 + oh2
        ii = lax.broadcasted_iota(jnp.int32, (MIX_ROWS, MIX_ROWS), 0)
        jj = lax.broadcasted_iota(jnp.int32, (MIX_ROWS, MIX_ROWS), 1)
        before = _dot(jnp.where(ii > jj, 1.0, 0.0), oh) + run
        r1 = jnp.sum(oh1 * before, axis=-1, keepdims=True)
        r2 = jnp.sum(oh2 * before, axis=-1, keepdims=True)
        run = run + jnp.sum(oh, axis=0, keepdims=True)

        rec = jnp.where(lane == ROUTE_GATE, gate1, 0.0)
        rec = jnp.where(lane == ROUTE_GATE + 1, gate2, rec)
        rec = jnp.where(lane == ROUTE_EXPERT, e1.astype(F32), rec)
        rec = jnp.where(lane == ROUTE_EXPERT + 1, e2.astype(F32), rec)
        rec = jnp.where(lane == ROUTE_RANK, r1, rec)
        rec = jnp.where(lane == ROUTE_RANK + 1, r2, rec)
        route_ref[rs, :] = rec
        route_t_ref[:, s * MIX_ROWS:(s + 1) * MIX_ROWS] = jnp.transpose(rec)[:SUBLANES, :]

    run_scr[...] = jnp.broadcast_to(run, run_scr.shape)
    cnt_ref[...] = run_scr[...]


def _mix_route(yconv, x0c, o_f, o_b, p, z_col, hy_norm_w, gdn_norm_w, xf, w_out_bf16, norm2_w, wr, br,
               n_heads, head_dim, n_groups, per_group, tm=512):
    M, D = xf.shape
    d_hy = yconv.shape[1]
    d_gdn = o_f.shape[1]
    assert z_col % d_gdn == 0 and n_groups * (per_group + 1) <= LANES
    zb = z_col // d_gdn
    row = lambda i: (i, 0)
    const = lambda i: (0, 0)
    kern = functools.partial(_mix_route_kernel, n_heads=n_heads, head_dim=head_dim,
                             n_groups=n_groups, per_group=per_group)
    return pl.pallas_call(
        kern,
        grid=(M // tm,),
        in_specs=[pl.BlockSpec((tm, d_hy), row), pl.BlockSpec((tm, d_hy), row),
                  pl.BlockSpec((tm, d_gdn), row), pl.BlockSpec((tm, d_gdn), row),
                  pl.BlockSpec((tm, d_gdn), lambda i: (i, zb)),
                  pl.BlockSpec((1, d_hy), const), pl.BlockSpec((1, head_dim), const),
                  pl.BlockSpec((tm, D), row), pl.BlockSpec(w_out_bf16.shape, const, pipeline_mode=pl.Buffered(1)),
                  pl.BlockSpec((1, D), const), pl.BlockSpec((D, LANES), const), pl.BlockSpec((1, LANES), const)],
        out_specs=[pl.BlockSpec((tm, D), row), pl.BlockSpec((tm * (D // 2 // LANES), LANES), row),
                   pl.BlockSpec((tm, LANES), row), pl.BlockSpec((SUBLANES, tm), lambda i: (0, i)),
                   pl.BlockSpec((SUBLANES, LANES), const)],
        out_shape=[jax.ShapeDtypeStruct((M, D), F32), jax.ShapeDtypeStruct((M * (D // 2 // LANES), LANES), jnp.uint32),
                   jax.ShapeDtypeStruct((M, LANES), F32), jax.ShapeDtypeStruct((SUBLANES, M), F32),
                   jax.ShapeDtypeStruct((SUBLANES, LANES), F32)],
        scratch_shapes=[pltpu.VMEM((SUBLANES, LANES), F32)],
        compiler_params=_cparams(("arbitrary",), 58),
        name="mix_route",
    )(yconv, x0c, o_f, o_b, p, hy_norm_w.reshape(1, d_hy), gdn_norm_w.reshape(1, head_dim), xf, w_out_bf16,
      norm2_w.reshape(1, D), wr, br)


WAIT_GROUP = 8


def _experts_kernel(ts_ref, se_ref, nn_ref, nv_ref, tgt_nxt_ref, tgt_ref, tgt_prv_ref, u_hbm, w1_hbm, w3_hbm, w2_hbm,
                    out_hbm, x0, x1, y0, y1, wf1, wf3, wf2, w1b, w3b, w2b, gsem, ssem, wsem, *, n_tokens):
    i = pl.program_id(0)
    na = nn_ref[0]
    ns = nn_ref[1]
    CH = w1b.shape[0] // 2 // LANES
    TB = x0.shape[0] // CH
    W_SLOTS = wf1.shape[0]
    last_tile = pl.num_programs(0) - 1
    nv_cur = nv_ref[i]
    nv_nxt = jnp.where(i + 1 < na, nv_ref[jnp.minimum(i + 1, last_tile)], 0)
    nv_prv = jnp.where(i > 0, nv_ref[jnp.maximum(i - 1, 0)], 0)
    nv_pp = jnp.where(i > 1, nv_ref[jnp.maximum(i - 2, 0)], 0)

    def weight_copies(s):
        slot = lax.rem(s, W_SLOTS)
        e = se_ref[s]
        return [pltpu.make_async_copy(w_hbm.at[e], wf.at[slot], wsem.at[slot])
                for w_hbm, wf in ((w1_hbm, wf1), (w3_hbm, wf3), (w2_hbm, wf2))]

    def token_of(v):
        if n_tokens & (n_tokens - 1) == 0:
            return v & (n_tokens - 1)
        return lax.rem(v, n_tokens)

    def tile_of(row):
        return pl.ds(pl.multiple_of(row * CH, CH), CH)

    def gather(tgt, xbuf, s, r):
        return pltpu.make_async_copy(u_hbm.at[tile_of(token_of(tgt[r]))], xbuf.at[tile_of(r)], gsem.at[s])

    def scatter(ybuf, s, r, dst):
        return pltpu.make_async_copy(ybuf.at[tile_of(r)], out_hbm.at[tile_of(dst)], ssem.at[s])

    def wait_tokens(count, make):
        groups = count // WAIT_GROUP

        def grp(g, carry):
            make(WAIT_GROUP).wait()
            return carry

        lax.fori_loop(0, groups, grp, 0)

        def one(g, carry):
            make(1).wait()
            return carry

        lax.fori_loop(groups * WAIT_GROUP, count, one, 0)

    def wait_gather(xbuf, s, count):
        wait_tokens(count, lambda n: pltpu.make_async_copy(u_hbm.at[pl.ds(0, n * CH)], xbuf.at[pl.ds(0, n * CH)],
                                                             gsem.at[s]))

    def wait_scatter(ybuf, s, count):
        wait_tokens(count, lambda n: pltpu.make_async_copy(ybuf.at[pl.ds(0, n * CH)], out_hbm.at[pl.ds(0, n * CH)],
                                                             ssem.at[s]))

    def each_row(count, fn):
        def body(r, carry):
            fn(r)
            return carry
        lax.fori_loop(0, count, body, 0)

    def step(p):
        q = 1 - p
        x_cur, x_nxt = (x0, x1) if p == 0 else (x1, x0)
        y_cur, y_prv = (y0, y1) if p == 0 else (y1, y0)
        wait_gather(x_cur, p, nv_cur)
        wait_scatter(y_cur, p, nv_pp)

        x = _unpack_bf16_pairs(_load_token_tiles(x_cur, 0, TB, CH)).astype(BF16)
        for r in range(TB):
            @pl.when(r < nv_nxt)
            def _():
                gather(tgt_nxt_ref, x_nxt, q, r).start()
        for r in range(TB):
            @pl.when(r < nv_prv)
            def _():
                scatter(y_prv, q, r, tgt_prv_ref[r]).start()
        h = _silu(jnp.dot(x, w1b[...], preferred_element_type=F32)) * jnp.dot(x, w3b[...], preferred_element_type=F32)
        _store_token_tiles(y_cur, 0, _pack_bf16_pairs(jnp.dot(h.astype(BF16), w2b[...], preferred_element_type=F32)))

        @pl.when(i == na - 1)
        def _():
            wait_scatter(y_prv, q, nv_prv)
            each_row(nv_cur, lambda r: scatter(y_cur, p, r, tgt_ref[r]).start())
            wait_scatter(y_cur, p, nv_cur)

    @pl.when(i < na)
    def _():
        @pl.when(i == 0)
        def _():
            x0[...] = jnp.zeros_like(x0)
            x1[...] = jnp.zeros_like(x1)
            each_row(nv_cur, lambda r: gather(tgt_ref, x0, 0, r).start())
            for s0 in range(W_SLOTS - 1):
                @pl.when(s0 < ns)
                def _():
                    for c in weight_copies(s0):
                        c.start(priority=WEIGHT_DMA_PRIORITY)

        s = ts_ref[i]

        @pl.when((i == 0) | (s != ts_ref[jnp.maximum(i - 1, 0)]))
        def _():
            @pl.when(s + W_SLOTS - 1 < ns)
            def _():
                for c in weight_copies(s + W_SLOTS - 1):
                    c.start(priority=WEIGHT_DMA_PRIORITY)

            for c in weight_copies(s):
                c.wait()
            slot = lax.rem(s, W_SLOTS)
            w1b[...] = wf1[slot].astype(BF16)
            w3b[...] = wf3[slot].astype(BF16)
            w2b[...] = wf2[slot].astype(BF16)

        parity = lax.rem(i, 2)

        @pl.when(parity == 0)
        def _():
            step(0)

        @pl.when(parity == 1)
        def _():
            step(1)


WEIGHT_SLOTS = 3
WEIGHT_DMA_PRIORITY = 1


def _experts(u, slot_tgt, tile_seq, seq_expert, n_active_seq, tile_valid, w1, w3, w2):
    E, D, de = w1.shape
    CH = D // 2 // LANES
    assert CH % SUBLANES == 0
    T = u.shape[0] // CH
    TB = EXPERT_ROWS
    n_tiles = slot_tgt.shape[0] // TB
    table = lambda f: pl.BlockSpec((TB,), lambda i, ts, se, nn, nv: (f(i),), memory_space=pltpu.SMEM)
    hbm = pl.BlockSpec(memory_space=pl.ANY)
    grid_spec = pltpu.PrefetchScalarGridSpec(
        num_scalar_prefetch=4,
        grid=(n_tiles,),
        in_specs=[table(lambda i: jnp.minimum(i + 1, n_tiles - 1)), table(lambda i: i),
                  table(lambda i: jnp.maximum(i - 1, 0)), hbm, hbm, hbm, hbm],
        out_specs=hbm,
        scratch_shapes=[pltpu.VMEM((TB * CH, LANES), jnp.uint32)] * 4
                       + [pltpu.VMEM((WEIGHT_SLOTS, D, de), F32), pltpu.VMEM((WEIGHT_SLOTS, D, de), F32),
                          pltpu.VMEM((WEIGHT_SLOTS, de, D), F32),
                          pltpu.VMEM((D, de), BF16), pltpu.VMEM((D, de), BF16), pltpu.VMEM((de, D), BF16),
                          pltpu.SemaphoreType.DMA((2,)), pltpu.SemaphoreType.DMA((2,)),
                          pltpu.SemaphoreType.DMA((WEIGHT_SLOTS,))],
    )
    return pl.pallas_call(
        functools.partial(_experts_kernel, n_tokens=T),
        grid_spec=grid_spec,
        out_shape=jax.ShapeDtypeStruct((2 * T * CH, LANES), jnp.uint32),
        compiler_params=_cparams(("arbitrary",), 58),
        name="experts",
    )(tile_seq, seq_expert, n_active_seq, tile_valid, slot_tgt, slot_tgt, slot_tgt, u, w1, w3, w2)


def _slot_table_kernel(dest_ref, init_hbm, o_ref, sem):
    fill = pltpu.make_async_copy(init_hbm, o_ref, sem)
    fill.start()
    fill.wait()

    def put(a, carry):
        o_ref[dest_ref[a]] = a
        return carry

    lax.fori_loop(0, dest_ref.shape[0], put, 0, unroll=16)


def _slot_table(dest, init):
    smem = pl.BlockSpec(memory_space=pltpu.SMEM)
    return pl.pallas_call(
        _slot_table_kernel,
        in_specs=[smem, pl.BlockSpec(memory_space=pl.ANY)],
        out_specs=smem,
        out_shape=jax.ShapeDtypeStruct(init.shape, jnp.int32),
        scratch_shapes=[pltpu.SemaphoreType.DMA],
        name="slot_table",
    )(dest, init)


def _dispatch_tables(route_t, counts, n_experts):
    T = route_t.shape[1]
    TB = EXPERT_ROWS
    e = route_t[ROUTE_EXPERT:ROUTE_EXPERT + 2].astype(jnp.int32)
    rank = route_t[ROUTE_RANK:ROUTE_RANK + 2].astype(jnp.int32)
    cnt = counts[0, :n_experts].astype(jnp.int32)
    padded = (cnt + TB - 1) // TB * TB
    pad_end = jnp.cumsum(padded)
    pad_start = pad_end - padded
    ids = jnp.arange(n_experts, dtype=jnp.int32)[:, None, None]
    start_of = jnp.sum(jnp.where(e[None] == ids, pad_start[:, None, None], 0), axis=0)
    dest = start_of + rank
    n_tiles = -(-(2 * T + n_experts * (TB - 1)) // TB)
    slot_tgt = _slot_table(dest.reshape(-1), jnp.zeros((n_tiles * TB,), jnp.int32))
    start = jnp.arange(n_tiles, dtype=jnp.int32) * TB
    tile_expert = jnp.sum(start[:, None] >= pad_end[None, :], axis=1)
    tile_expert = jnp.minimum(tile_expert, n_experts - 1).astype(jnp.int32)
    owns = cnt > 0
    pos = jnp.cumsum(owns.astype(jnp.int32)) - 1
    experts = jnp.arange(n_experts, dtype=jnp.int32)
    seq_expert = jnp.sum(jnp.where(owns[None, :] & (pos[None, :] == experts[:, None]), experts[None, :], 0), axis=1)
    tile_seq = jnp.sum(jnp.where(tile_expert[:, None] == experts[None, :], pos[None, :], 0), axis=1)
    n_active_seq = jnp.stack([pad_end[-1] // TB, jnp.sum(owns.astype(jnp.int32))]).astype(jnp.int32)
    pick = tile_expert[:, None] == experts[None, :]
    in_expert = start - jnp.sum(jnp.where(pick, pad_start[None, :], 0), axis=1)
    tile_cnt = jnp.sum(jnp.where(pick, cnt[None, :], 0), axis=1)
    tile_valid = jnp.where(start < pad_end[-1], jnp.clip(tile_cnt - in_expert, 0, TB), 0).astype(jnp.int32)
    return slot_tgt, tile_seq.astype(jnp.int32), seq_expert.astype(jnp.int32), n_active_seq, tile_valid


def _combine_kernel(x2_ref, e0_ref, e1_ref, route_ref, w_ref, o_ref, *, final_norm):
    r = route_ref[...]
    tm = x2_ref.shape[0]
    ch = e0_ref.shape[0] // tm
    e0 = _unpack_bf16_pairs(_load_token_tiles(e0_ref, 0, tm, ch))
    e1 = _unpack_bf16_pairs(_load_token_tiles(e1_ref, 0, tm, ch))
    y = x2_ref[...] + r[:, ROUTE_GATE:ROUTE_GATE + 1] * e0 + r[:, ROUTE_GATE + 1:ROUTE_GATE + 2] * e1
    if final_norm:
        y = y * lax.rsqrt(jnp.mean(y * y, axis=-1, keepdims=True) + EPS) * w_ref[...]
    o_ref[...] = y


def _combine(x2, planes, route, norm_w, final_norm, tm=512):
    M, D = x2.shape
    return pl.pallas_call(
        functools.partial(_combine_kernel, final_norm=final_norm),
        grid=(M // tm,),
        in_specs=[pl.BlockSpec((tm, D), lambda i: (i, 0)),
                  pl.BlockSpec((tm * (D // 2 // LANES), LANES), lambda i: (i, 0)),
                  pl.BlockSpec((tm * (D // 2 // LANES), LANES), lambda i: (M // tm + i, 0)),
                  pl.BlockSpec((tm, LANES), lambda i: (i, 0)),
                  pl.BlockSpec((1, D), lambda i: (0, 0))],
        out_specs=pl.BlockSpec((tm, D), lambda i: (i, 0)),
        out_shape=jax.ShapeDtypeStruct((M, D), F32),
        compiler_params=_cparams(("parallel",), 48),
        name="combine",
    )(x2, planes, planes, route, norm_w.reshape(1, D))


def kernel(x, norm1_w, w_in, hy_conv_w, hy_conv_b, hy_filt_w1, hy_filt_b1, hy_filt_w2, hy_filt_b2, hy_filt_w3, hy_filt_b3, hy_filt_w4, hy_sin_freq, hy_skip, hy_norm_w, gdn_conv_w, gdn_a_log_f, gdn_a_log_b, gdn_dt_bias_f, gdn_dt_bias_b, gdn_norm_w, w_out, norm2_w, router_group_w, router_group_b, router_expert_w, router_expert_b, exp_w1, exp_w3, exp_w2, final_norm_w):
    B, L, D = x.shape
    M = B * L
    depth = w_in.shape[0]
    d_hy = hy_skip.shape[-1]
    H = gdn_a_log_f.shape[-1]
    Dh = gdn_norm_w.shape[-1]
    d_gdn = H * Dh
    n_main = 3 * d_hy + 4 * d_gdn
    G = router_group_w.shape[-1]
    E = router_expert_w.shape[-1]
    xf = x.reshape(M, D)
    for l in range(depth):
        p, pg = _inproj(xf, norm1_w[l], jnp.swapaxes(w_in[l], 0, 1), n_main, 4 * H)
        p3 = p.reshape(B, L, n_main)
        x0c, yconv = _hyena_branch(p3, hy_conv_w[l], hy_conv_b[l], hy_filt_w1[l], hy_filt_b1[l], hy_filt_w2[l],
                                   hy_filt_b2[l], hy_filt_w3[l], hy_filt_b3[l], hy_filt_w4[l], hy_sin_freq[l],
                                   hy_skip[l], d_hy)
        o_f, o_b = _gdn_branch(p3, pg, gdn_conv_w[l], gdn_a_log_f[l], gdn_a_log_b[l], gdn_dt_bias_f[l],
                               gdn_dt_bias_b[l], 3 * d_hy, H, Dh)
        wr = jnp.pad(jnp.concatenate([router_group_w[l], router_expert_w[l]], axis=1), ((0, 0), (0, LANES - G - E)))
        br = jnp.pad(jnp.concatenate([router_group_b[l], router_expert_b[l]]), (0, LANES - G - E)).reshape(1, LANES)
        x2, u, route, route_t, counts = _mix_route(
            yconv.reshape(M, d_hy), x0c.reshape(M, d_hy), o_f.reshape(M, d_gdn), o_b.reshape(M, d_gdn), p,
            3 * d_hy + 3 * d_gdn, hy_norm_w[l], gdn_norm_w[l], xf, w_out[l].astype(BF16), norm2_w[l], wr, br,
            H, Dh, G, E // G)
        slot_tgt, tile_seq, seq_expert, n_active_seq, tile_valid = _dispatch_tables(route_t, counts, E)
        planes = _experts(u, slot_tgt, tile_seq, seq_expert, n_active_seq, tile_valid, exp_w1[l], exp_w3[l],
                          exp_w2[l])
        xf = _combine(x2, planes, route, final_norm_w, final_norm=(l == depth - 1))
    return xf.reshape(B, L, D)
```

```python
import functools
import math

import jax
import jax.numpy as jnp
import numpy as np
from jax import lax
from jax.experimental import pallas as pl
from jax.experimental.pallas import tpu as pltpu

F32 = jnp.float32
BF16 = jnp.bfloat16
EPS = 1e-6
LANES = 128
SUBLANES = 8
VMEM_BYTES_V7X = 64 * 1024 * 1024
GDN_CHUNK = 64
FFT_N2 = 128
EXPERT_ROWS = 256
DECAY_TARGET = 1e-2
FAST_DECAY_PCT = 0.3
SLOW_DECAY_PCT = 1.5


def _cparams(sem, vmem_mib):
    limit = int(vmem_mib * 1024 * 1024)
    assert limit < VMEM_BYTES_V7X
    return pltpu.CompilerParams(dimension_semantics=sem, vmem_limit_bytes=limit)


def _dot(a, b):
    return jnp.dot(a.astype(BF16), b.astype(BF16), preferred_element_type=F32)


def _dot_nt(a, b):
    return lax.dot_general(a.astype(BF16), b.astype(BF16), (((1,), (1,)), ((), ())), preferred_element_type=F32)


def _dot_tn(a, b):
    return lax.dot_general(a.astype(BF16), b.astype(BF16), (((0,), (0,)), ((), ())), preferred_element_type=F32)


def _dot_hi(a, b):
    return jnp.dot(a, b, preferred_element_type=F32, precision=lax.Precision.HIGHEST)


def _silu(x):
    return x * jax.nn.sigmoid(x)


def _pack_bf16_pairs(x):
    c = x.shape[1] // 2
    lo = lax.bitcast_convert_type(x[:, :c].astype(BF16).astype(F32), jnp.uint32) >> 16
    hi = lax.bitcast_convert_type(x[:, c:].astype(BF16).astype(F32), jnp.uint32) & jnp.uint32(0xFFFF0000)
    return hi | lo


def _unpack_bf16_pairs(w):
    lo = lax.bitcast_convert_type(w << 16, F32)
    hi = lax.bitcast_convert_type(w & jnp.uint32(0xFFFF0000), F32)
    return jnp.concatenate([lo, hi], axis=1)


def _store_token_tiles(ref, row0, words):
    n, width = words.shape
    ch = width // LANES
    for c in range(ch):
        ref[pl.ds(row0 * ch + c, n, stride=ch), :] = words[:, c * LANES:(c + 1) * LANES]


def _load_token_tiles(ref, row0, n, ch):
    return jnp.concatenate([ref[pl.ds(row0 * ch + c, n, stride=ch), :] for c in range(ch)], axis=1)


def _inproj_kernel(x_hbm, nw_ref, wt_ref, wgt_ref, p_ref, g_ref, h_scr, x_scr, x_sem):
    i = pl.program_id(0)
    j = pl.program_id(1)
    tm = x_scr.shape[0]

    def x_copy(tile):
        return pltpu.make_async_copy(x_hbm.at[pl.ds(pl.multiple_of(tile * tm, tm), tm)], x_scr, x_sem)

    @pl.when((i == 0) & (j == 0))
    def _():
        x_copy(0).start()

    @pl.when((j == 1) & (i + 1 < pl.num_programs(0)))
    def _():
        x_copy(i + 1).start()

    @pl.when(j == 0)
    def _():
        x_copy(i).wait()
        x = x_scr[...]
        h = x * lax.rsqrt(jnp.mean(x * x, axis=-1, keepdims=True) + EPS) * nw_ref[...]
        h_scr[...] = h.astype(BF16)
        g = _dot_nt(h, wgt_ref[...])
        g_ref[...] = jnp.concatenate([g, jnp.zeros((g.shape[0], LANES - g.shape[1]), F32)], axis=1)

    p_ref[...] = _dot_nt(h_scr[...], wt_ref[...]).astype(p_ref.dtype)


def _inproj(xf, norm_w, wt, n_main, n_gate, tm=2048, tn=512):
    M, D = xf.shape
    assert n_main % tn == 0 and M % tm == 0 and n_main % n_gate == 0 and n_gate % SUBLANES == 0
    assert n_main // tn >= 2
    return pl.pallas_call(
        _inproj_kernel,
        grid=(M // tm, n_main // tn),
        in_specs=[
            pl.BlockSpec(memory_space=pl.ANY),
            pl.BlockSpec((1, D), lambda i, j: (0, 0)),
            pl.BlockSpec((tn, D), lambda i, j: (j, 0)),
            pl.BlockSpec((n_gate, D), lambda i, j: (n_main // n_gate, 0)),
        ],
        out_specs=[
            pl.BlockSpec((tm, tn), lambda i, j: (i, j)),
            pl.BlockSpec((tm, LANES), lambda i, j: (i, 0)),
        ],
        out_shape=[jax.ShapeDtypeStruct((M, n_main), BF16), jax.ShapeDtypeStruct((M, LANES), F32)],
        scratch_shapes=[pltpu.VMEM((tm, D), BF16), pltpu.VMEM((tm, D), F32), pltpu.SemaphoreType.DMA],
        compiler_params=_cparams(("arbitrary", "arbitrary"), 57),
        name="inproj",
    )(xf, norm_w.reshape(1, D), wt, wt)


def _conv3_rows(ref, r0, rows, w, n_rows):
    cur = ref[pl.ds(r0, rows), :].astype(F32)
    lo = jnp.maximum(r0 - HALO_ROWS, 0)
    hi = jnp.minimum(r0 + rows, n_rows - HALO_ROWS)
    prev_grp = ref[pl.ds(pl.multiple_of(lo, HALO_ROWS), HALO_ROWS), :].astype(F32)
    next_grp = ref[pl.ds(pl.multiple_of(hi, HALO_ROWS), HALO_ROWS), :].astype(F32)
    prev_row = jnp.where(r0 > 0, prev_grp[HALO_ROWS - 1:HALO_ROWS, :], 0.0)
    next_row = jnp.where(r0 + rows < n_rows, next_grp[0:1, :], 0.0)
    row = lax.broadcasted_iota(jnp.int32, cur.shape, 0)
    xm = jnp.where(row == 0, prev_row, pltpu.roll(cur, 1, 0))
    xp = jnp.where(row == rows - 1, next_row, pltpu.roll(cur, rows - 1, 0))
    return xm * w[0:1, :] + cur * w[1:2, :] + xp * w[2:3, :]


CONV_ROWS = 256
HALO_ROWS = 16


def _hy_pre_kernel(x0_ref, x1_ref, v_ref, w0_ref, w1_ref, w2_ref, b0_ref, b1_ref, b2_ref, x0c_ref, vg_ref):
    L = x0_ref.shape[0]
    w0, w1, w2 = w0_ref[...], w1_ref[...], w2_ref[...]
    b0, b1, b2 = b0_ref[...], b1_ref[...], b2_ref[...]

    def body(c, carry):
        r0 = pl.multiple_of(c * CONV_ROWS, CONV_ROWS)
        x0c_ref[pl.ds(r0, CONV_ROWS), :] = (_conv3_rows(x0_ref, r0, CONV_ROWS, w0, L) + b0).astype(x0c_ref.dtype)
        x1c = _conv3_rows(x1_ref, r0, CONV_ROWS, w1, L) + b1
        vc = _conv3_rows(v_ref, r0, CONV_ROWS, w2, L) + b2
        vg_ref[pl.ds(r0, CONV_ROWS), :] = (vc * x1c).astype(vg_ref.dtype)
        return carry

    lax.fori_loop(0, L // CONV_ROWS, body, 0, unroll=2)


def _hy_pre(p3, conv_w, conv_b, d_hy):
    B, L, _ = p3.shape
    nt = d_hy // LANES
    bias = conv_b.reshape(1, -1)
    pspec = lambda off: pl.BlockSpec((None, L, LANES), lambda b, c: (b, 0, c + off))
    wspec = lambda off: pl.BlockSpec((3, LANES), lambda b, c: (0, c + off))
    bspec = lambda off: pl.BlockSpec((1, LANES), lambda b, c: (0, c + off))
    ospec = pl.BlockSpec((None, L, LANES), lambda b, c: (b, 0, c))
    return pl.pallas_call(
        _hy_pre_kernel,
        grid=(B, nt),
        in_specs=[pspec(0), pspec(nt), pspec(2 * nt), wspec(0), wspec(nt), wspec(2 * nt),
                  bspec(0), bspec(nt), bspec(2 * nt)],
        out_specs=[ospec, ospec],
        out_shape=[jax.ShapeDtypeStruct((B, L, d_hy), BF16)] * 2,
        compiler_params=_cparams(("parallel", "parallel"), 40),
        name="hy_pre",
    )(p3, p3, p3, conv_w, conv_w, conv_w, bias, bias, bias)


def _gdn_pre_kernel(x_ref, w_ref, o_ref, *, n_heads, head_dim):
    L = x_ref.shape[0]
    w = w_ref[...]
    c = pl.program_id(1)
    q_scale = jnp.where(c < n_heads, head_dim ** -0.5, 1.0)
    is_qk = c < 2 * n_heads

    def body(i, carry):
        r0 = pl.multiple_of(i * CONV_ROWS, CONV_ROWS)
        y = _silu(_conv3_rows(x_ref, r0, CONV_ROWS, w, L))
        inv = lax.rsqrt(jnp.sum(y * y, axis=-1, keepdims=True) + EPS) * q_scale
        o_ref[pl.ds(r0, CONV_ROWS), :] = (y * jnp.where(is_qk, inv, 1.0)).astype(o_ref.dtype)
        return carry

    lax.fori_loop(0, L // CONV_ROWS, body, 0, unroll=4)


def _gdn_pre(p3, conv_w, col0, n_heads, head_dim):
    B, L, _ = p3.shape
    assert head_dim == LANES
    nt = 3 * n_heads
    off = col0 // LANES
    return pl.pallas_call(
        functools.partial(_gdn_pre_kernel, n_heads=n_heads, head_dim=head_dim),
        grid=(B, nt),
        in_specs=[pl.BlockSpec((None, L, LANES), lambda b, c: (b, 0, c + off)),
                  pl.BlockSpec((3, LANES), lambda b, c: (0, c))],
        out_specs=pl.BlockSpec((None, L, LANES), lambda b, c: (b, 0, c)),
        out_shape=jax.ShapeDtypeStruct((B, L, nt * LANES), BF16),
        compiler_params=_cparams(("parallel", "parallel"), 24),
        name="gdn_pre",
    )(p3, conv_w)


GATE_ROWS = 512


def _gdn_gates_kernel(pg_ref, alog_ref, dtb_ref, o_ref, *, n_heads):
    H = n_heads
    x = pg_ref[...]
    beta = jax.nn.sigmoid(x)
    z = x + dtb_ref[...]
    softplus = jnp.maximum(z, 0.0) + jnp.log1p(jnp.exp(-jnp.abs(z)))
    g = -jnp.exp(alog_ref[...]) * softplus
    pos = lax.broadcasted_iota(jnp.int32, x.shape, 0) & (GDN_CHUNK - 1)
    gc_f = g
    gc_b = g
    step = 1
    while step < GDN_CHUNK:
        gc_f = gc_f + jnp.where(pos >= step, pltpu.roll(gc_f, step, 0), 0.0)
        gc_b = gc_b + jnp.where(pos < GDN_CHUNK - step, pltpu.roll(gc_b, GATE_ROWS - step, 0), 0.0)
        step *= 2
    g_tot = pltpu.roll(gc_f + gc_b - g, 4 * H, 1)
    lane = lax.broadcasted_iota(jnp.int32, x.shape, 1)
    out = jnp.where(lane < 2 * H, beta,
                    jnp.where(lane < 3 * H, gc_f,
                              jnp.where(lane < 4 * H, gc_b,
                                        jnp.where((lane >= 6 * H) & (lane < 8 * H), g_tot, 0.0))))
    o_ref[...] = out


def _gdn_gates(pg, a_log_f, a_log_b, dt_bias_f, dt_bias_b, n_heads):
    M = pg.shape[0]
    H = n_heads
    assert 8 * H <= LANES
    pad = lambda a, b: jnp.concatenate([jnp.zeros((2 * H,), F32), a.astype(F32), b.astype(F32),
                                        jnp.zeros((LANES - 4 * H,), F32)]).reshape(1, LANES)
    return pl.pallas_call(
        functools.partial(_gdn_gates_kernel, n_heads=H),
        grid=(M // GATE_ROWS,),
        in_specs=[pl.BlockSpec((GATE_ROWS, LANES), lambda i: (i, 0)),
                  pl.BlockSpec((1, LANES), lambda i: (0, 0)),
                  pl.BlockSpec((1, LANES), lambda i: (0, 0))],
        out_specs=pl.BlockSpec((GATE_ROWS, LANES), lambda i: (i, 0)),
        out_shape=jax.ShapeDtypeStruct((M, LANES), F32),
        compiler_params=_cparams(("parallel",), 24),
        name="gdn_gates",
    )(pg, pad(a_log_f, a_log_b), pad(dt_bias_f, dt_bias_b))


def _delta_chunks(q, k, v, beta, gc_col, gc_row, gtot, state, lower):
    n = len(q)
    C = q[0].shape[0]
    D = k[0].shape[1]
    ii = lax.broadcasted_iota(jnp.int32, (C, C), 0)
    jj = lax.broadcasted_iota(jnp.int32, (C, C), 1)
    eye = jnp.where(ii == jj, 1.0, 0.0)
    incl = [(ii >= jj) if lo else (ii <= jj) for lo in lower]
    strict = [(ii > jj) if lo else (ii < jj) for lo in lower]
    rng = range(n)
    decay = [jnp.where(incl[i], jnp.exp(jnp.where(incl[i], gc_col[i] - gc_row[i], 0.0)), 0.0) for i in rng]
    kb = [k[i] * beta[i] for i in rng]
    kk = [_dot_nt(kb[i], k[i]) for i in rng]
    qk = [_dot_nt(q[i], k[i]) for i in rng]
    m = [jnp.where(strict[i], -(kk[i] * decay[i]), 0.0) for i in rng]
    r = [eye + m[i] for i in rng]
    m = [_dot(m[i], m[i]) for i in rng]
    for _ in range(int(math.log2(C)) - 2):
        rm = [_dot(jnp.concatenate([r[i], m[i]], axis=0), m[i]) for i in rng]
        r = [r[i] + rm[i][:C] for i in rng]
        m = [rm[i][C:] for i in rng]
    r = [r[i] + _dot(r[i], m[i]) for i in rng]
    eg = [jnp.exp(gc_col[i]) for i in rng]
    wu = [_dot(r[i], jnp.concatenate([kb[i] * eg[i], v[i] * beta[i]], axis=1)) for i in rng]
    ws = [_dot(jnp.concatenate([wu[i][:, :D], q[i] * eg[i]], axis=0), state[i]) for i in rng]
    v_new = [wu[i][:, D:] - ws[i][:C] for i in rng]
    qkm = [jnp.where(incl[i], qk[i] * decay[i], 0.0) for i in rng]
    out = [ws[i][C:] + _dot(qkm[i], v_new[i]) for i in rng]
    k_dec = [k[i] * jnp.exp(gtot[i] - gc_col[i]) for i in rng]
    new_state = [state[i] * jnp.exp(gtot[i][0:1, :]) + _dot_tn(k_dec[i], v_new[i]) for i in rng]
    return out, new_state


def _gdn_scan_kernel(qf_ref, kf_ref, vf_ref, qb_ref, kb_ref, vb_ref, gf_ref, gb_ref, rf_ref, rb_ref,
                     of_ref, ob_ref, s_scr, *, n_heads, head_dim):
    H, Dh = n_heads, head_dim
    B = qf_ref.shape[0]

    @pl.when(pl.program_id(0) == 0)
    def _():
        s_scr[...] = jnp.zeros_like(s_scr)

    col = lambda g, j: g[:, j:j + 1]
    sls = [slice(h * Dh, (h + 1) * Dh) for h in range(H)]
    q, k, v, beta, gc_col, gc_row, gtot, state = ([] for _ in range(8))
    for b in range(B):
        gf = gf_ref[b]
        gb = gb_ref[b]
        q += [qf_ref[b, :, sl] for sl in sls] + [qb_ref[b, :, sl] for sl in sls]
        k += [kf_ref[b, :, sl] for sl in sls] + [kb_ref[b, :, sl] for sl in sls]
        v += [vf_ref[b, :, sl] for sl in sls] + [vb_ref[b, :, sl] for sl in sls]
        beta += [col(gf, h) for h in range(H)] + [col(gb, H + h) for h in range(H)]
        gc_col += [col(gf, 2 * H + h) for h in range(H)] + [col(gb, 3 * H + h) for h in range(H)]
        gc_row += [rf_ref[b, h:h + 1, :] for h in range(H)] + [rb_ref[b, H + h:H + h + 1, :] for h in range(H)]
        gtot += [col(gf, 6 * H + h) for h in range(H)] + [col(gb, 7 * H + h) for h in range(H)]
        state += [s_scr[b, 0, h] for h in range(H)] + [s_scr[b, 1, h] for h in range(H)]
    out, new_state = _delta_chunks(q, k, v, beta, gc_col, gc_row, gtot, state, ([True] * H + [False] * H) * B)
    for b in range(B):
        for h in range(H):
            of_ref[b, :, sls[h]] = out[2 * H * b + h]
            ob_ref[b, :, sls[h]] = out[2 * H * b + H + h]
            s_scr[b, 0, h] = new_state[2 * H * b + h]
            s_scr[b, 1, h] = new_state[2 * H * b + H + h]


def _gdn_scan(qkv, gates, gates_row, n_heads, head_dim):
    B, L, _ = qkv.shape
    H, Dh = n_heads, head_dim
    d = H * Dh
    C = GDN_CHUNK
    N = L // C
    fwd = lambda col: pl.BlockSpec((B, C, d), lambda n: (0, n, col))
    bwd = lambda col: pl.BlockSpec((B, C, d), lambda n: (0, N - 1 - n, col))
    return pl.pallas_call(
        functools.partial(_gdn_scan_kernel, n_heads=H, head_dim=Dh),
        grid=(N,),
        in_specs=[fwd(0), fwd(1), fwd(2), bwd(0), bwd(1), bwd(2),
                  pl.BlockSpec((B, C, LANES), lambda n: (0, n, 0)),
                  pl.BlockSpec((B, C, LANES), lambda n: (0, N - 1 - n, 0)),
                  pl.BlockSpec((B, None, 2 * H, C), lambda n: (0, n, 0, 0)),
                  pl.BlockSpec((B, None, 2 * H, C), lambda n: (0, N - 1 - n, 0, 0))],
        out_specs=[pl.BlockSpec((B, C, d), lambda n: (0, n, 0)),
                   pl.BlockSpec((B, C, d), lambda n: (0, N - 1 - n, 0))],
        out_shape=[jax.ShapeDtypeStruct((B, L, d), F32)] * 2,
        scratch_shapes=[pltpu.VMEM((B, 2, H, Dh, Dh), F32)],
        compiler_params=_cparams(("arbitrary",), 32),
        name="gdn_scan",
    )(qkv, qkv, qkv, qkv, qkv, qkv, gates, gates, gates_row, gates_row)


def _gdn_branch(p3, pg, conv_w, a_log_f, a_log_b, dt_bias_f, dt_bias_b, col0, n_heads, head_dim):
    B, L, _ = p3.shape
    H = n_heads
    qkv = _gdn_pre(p3, conv_w, col0, H, head_dim)
    gates = _gdn_gates(pg, a_log_f, a_log_b, dt_bias_f, dt_bias_b, H).reshape(B, L, LANES)
    N = L // GDN_CHUNK
    gates_row = gates[..., 2 * H:4 * H].reshape(B, N, GDN_CHUNK, 2 * H).transpose(0, 1, 3, 2)
    return _gdn_scan(qkv, gates, gates_row, H, head_dim)


@functools.lru_cache(maxsize=None)
def _filter_positions(L, pos_emb_dim):
    n = 2 * L
    r = np.arange(n)
    k = np.where(r < L, r, np.where(r == L, 0, n - r)).astype(np.float64)
    t = k / (L - 1)
    bands = (pos_emb_dim - 1) // 2
    fb = np.linspace(1e-4, bands - 1, bands)
    ang = (2.0 * math.pi / L) * k[:, None] * fb[None, :]
    z = np.concatenate([t[:, None], np.cos(ang), -np.sin(ang)], axis=-1)
    return z.astype(np.float32)


@functools.lru_cache(maxsize=None)
def _decay_rates(d_hy):
    max_decay = math.log(DECAY_TARGET) / FAST_DECAY_PCT
    min_decay = math.log(DECAY_TARGET) / SLOW_DECAY_PCT
    return np.abs(np.linspace(min_decay, max_decay, d_hy)).astype(np.float32).reshape(1, d_hy)


def _filt_mlp_kernel(zt_ref, w1t_ref, b1_ref, w2t_ref, b2_ref, w3t_ref, b3_ref, fr_ref, o_ref):
    fr = fr_ref[...]
    h = jnp.sin(fr * (_dot_hi(w1t_ref[...], zt_ref[...]) + b1_ref[...]))
    h = jnp.sin(fr * (_dot_hi(w2t_ref[...], h) + b2_ref[...]))
    o_ref[...] = jnp.sin(fr * (_dot_hi(w3t_ref[...], h) + b3_ref[...]))


def _filt_mlp(zt, w1, b1, w2, b2, w3, b3, freq, tc=1024):
    pe, n = zt.shape
    fw = w1.shape[1]
    col = lambda a: a.reshape(-1, 1).astype(F32)
    full = lambda a: pl.BlockSpec(a.shape, lambda i: (0, 0))
    args = (zt, w1.T, col(b1), w2.T, col(b2), w3.T, col(b3), col(freq))
    return pl.pallas_call(
        _filt_mlp_kernel,
        grid=(n // tc,),
        in_specs=[pl.BlockSpec((pe, tc), lambda i: (0, i))] + [full(a) for a in args[1:]],
        out_specs=pl.BlockSpec((fw, tc), lambda i: (0, i)),
        out_shape=jax.ShapeDtypeStruct((fw, n), F32),
        compiler_params=_cparams(("parallel",), 24),
        name="filt_mlp",
    )(*args)


@functools.lru_cache(maxsize=None)
def _dft_tables(L):
    n = 2 * L
    N2 = FFT_N2
    N1 = n // N2
    N1h = N1 // 2
    j2 = np.arange(N2)[:, None, None]
    k1 = np.arange(N1)[None, :, None]

    def stage1(n_j1):
        j1 = np.arange(n_j1)[None, None, :]
        m = (k1 * (N2 * j1 + j2)) % n
        th = 2.0 * np.pi * m / n
        return np.cos(th), np.sin(th)

    c, s = stage1(N1h)
    t1 = np.concatenate([np.concatenate([c, s], axis=2), np.concatenate([-s, c], axis=2)], axis=1)
    c, s = stage1(N1)
    t1g = np.concatenate([c, -s], axis=1)
    c, s = stage1(N1h)
    ct, st = np.swapaxes(c, 1, 2) / n, np.swapaxes(s, 1, 2) / n
    t2 = np.concatenate([np.concatenate([ct, -st], axis=2), np.concatenate([st, ct], axis=2)], axis=1)
    a = np.arange(N2)
    th = 2.0 * np.pi * ((a[:, None] * a[None, :]) % N2) / N2
    c2, s2 = np.cos(th), np.sin(th)
    f2f = np.block([[c2, s2], [-s2, c2]])
    f2i = np.block([[c2, -s2], [s2, c2]])
    return dict(N1=N1, N2=N2, t1=t1.astype(np.float32), t1g=t1g.astype(np.float32), t2=t2.astype(np.float32),
                f2f=f2f.astype(np.float32), f2i=f2i.astype(np.float32))


FILT_ROWS = 512
FFT_UNROLL = 8
PITCH_PAD = 8
K1_GROUP = 2


def _filt_fft_kernel(h3_ref, w4f_ref, w4b_ref, delta_ref, t1g_ref, f2f_ref, hspec_ref, hb0_ref, g_scr, a_scr,
                     *, L, N1, N2):
    n = 2 * L
    gp = N2 + PITCH_PAD
    ap = 2 * N1 + PITCH_PAD
    delta = delta_ref[...]
    hb0_ref[...] = jnp.zeros_like(hb0_ref)

    def gen(c, carry):
        r0 = pl.multiple_of(c * FILT_ROWS, FILT_ROWS)
        row = r0 + lax.broadcasted_iota(jnp.int32, (FILT_ROWS, LANES), 0)
        lag = jnp.where(row < L, row, jnp.where(row == L, 0, n - row))
        window = jnp.exp(-(lag.astype(F32) * (1.0 / (L - 1))) * delta)
        w4 = jnp.where(r0 < L, w4f_ref[...], w4b_ref[...])
        g = _dot(h3_ref[pl.ds(r0, FILT_ROWS), :], w4) * window
        at_l = row == L
        hb0_ref[...] += jnp.sum(jnp.where(at_l, g, 0.0), axis=0, keepdims=True)
        g = jnp.where(at_l, 0.0, g)
        for q in range(FILT_ROWS // N2):
            dst = pl.multiple_of((c * (FILT_ROWS // N2) + q) * gp, SUBLANES)
            g_scr[pl.ds(dst, N2), :] = g[q * N2:(q + 1) * N2]
        return carry

    lax.fori_loop(0, n // FILT_ROWS, gen, 0, unroll=2)

    def stage1(j2, carry):
        x = g_scr[pl.ds(j2, N1, stride=gp), :]
        a_scr[pl.ds(pl.multiple_of(j2 * ap, SUBLANES), 2 * N1), :] = _dot(t1g_ref[j2], x)
        return carry

    lax.fori_loop(0, N2, stage1, 0, unroll=4 * FFT_UNROLL)

    def stage2(kp, carry):
        k1s = [kp * K1_GROUP + t for t in range(K1_GROUP)]
        x = jnp.concatenate([jnp.concatenate([a_scr[pl.ds(k1, N2, stride=ap), :],
                                              a_scr[pl.ds(N1 + k1, N2, stride=ap), :]], axis=0) for k1 in k1s], axis=1)
        z = _dot(f2f_ref[...], x).astype(hspec_ref.dtype)
        for t, k1 in enumerate(k1s):
            hspec_ref[pl.ds(pl.multiple_of(k1 * 2 * N2, 2 * N2), 2 * N2), :] = z[:, t * LANES:(t + 1) * LANES]
        return carry

    lax.fori_loop(0, N1 // K1_GROUP, stage2, 0, unroll=2 * FFT_UNROLL)


def _filt_fft(h3, w4, L, d_hy):
    tb = _dft_tables(L)
    N1, N2 = tb["N1"], tb["N2"]
    n = 2 * L
    fw = h3.shape[1]
    nt = d_hy // LANES
    t1g = jnp.asarray(tb["t1g"]).astype(BF16)
    f2f = jnp.asarray(tb["f2f"]).astype(BF16)
    return pl.pallas_call(
        functools.partial(_filt_fft_kernel, L=L, N1=N1, N2=N2),
        grid=(nt,),
        in_specs=[pl.BlockSpec((n, fw), lambda c: (0, 0)),
                  pl.BlockSpec((fw, LANES), lambda c: (0, c)),
                  pl.BlockSpec((fw, LANES), lambda c: (0, c + nt)),
                  pl.BlockSpec((1, LANES), lambda c: (0, c)),
                  pl.BlockSpec(t1g.shape, lambda c: (0, 0, 0)),
                  pl.BlockSpec(f2f.shape, lambda c: (0, 0))],
        out_specs=[pl.BlockSpec((2 * n, LANES), lambda c: (0, c)),
                   pl.BlockSpec((SUBLANES, LANES), lambda c: (0, c))],
        out_shape=[jax.ShapeDtypeStruct((2 * n, d_hy), BF16), jax.ShapeDtypeStruct((SUBLANES, d_hy), F32)],
        scratch_shapes=[pltpu.VMEM((N1 * (N2 + PITCH_PAD), LANES), F32),
                        pltpu.VMEM((N2 * (2 * N1 + PITCH_PAD), LANES), F32)],
        compiler_params=_cparams(("parallel",), 48),
        name="filt_fft",
    )(h3, w4, w4, jnp.asarray(_decay_rates(d_hy)), t1g, f2f)


def _hy_conv_kernel(vg_ref, hspec_ref, skip_ref, hb0_ref, t1_ref, f2f_ref, f2i_ref, t2_ref, y_ref,
                    x_scr, a_scr, b_scr, *, N1, N2):
    N1h = N1 // 2
    xp = N2 + PITCH_PAD
    ap = 2 * N1 + PITCH_PAD
    bp = 2 * N2 + PITCH_PAD

    for b in range(2):
        for j1 in range(N1h):
            x_scr[b, pl.ds(j1 * xp, N2), :] = vg_ref[b, pl.ds(j1 * N2, N2), :].astype(F32)

    def stage1(j2, carry):
        x = jnp.concatenate([x_scr[0, pl.ds(j2, N1h, stride=xp), :],
                             x_scr[1, pl.ds(j2, N1h, stride=xp), :]], axis=0)
        a_scr[pl.ds(pl.multiple_of(j2 * ap, SUBLANES), 2 * N1), :] = _dot(t1_ref[j2], x)
        return carry

    lax.fori_loop(0, N2, stage1, 0, unroll=4 * FFT_UNROLL)

    def stage2(kp, carry):
        k1s = [kp * K1_GROUP + t for t in range(K1_GROUP)]
        x = jnp.concatenate([jnp.concatenate([a_scr[pl.ds(k1, N2, stride=ap), :],
                                              a_scr[pl.ds(N1 + k1, N2, stride=ap), :]], axis=0) for k1 in k1s], axis=1)
        z = _dot(f2f_ref[...], x)
        zr, zi = z[:N2], z[N2:]
        hs = [hspec_ref[pl.ds(pl.multiple_of(k1 * 2 * N2, 2 * N2), 2 * N2), :].astype(F32) for k1 in k1s]
        hr = jnp.concatenate([h[:N2] for h in hs], axis=1)
        hi = jnp.concatenate([h[N2:] for h in hs], axis=1)
        prod = jnp.concatenate([zr * hr - zi * hi, zr * hi + zi * hr], axis=0)
        b = _dot(f2i_ref[...], prod)
        for t, k1 in enumerate(k1s):
            b_scr[pl.ds(pl.multiple_of(k1 * bp, SUBLANES), 2 * N2), :] = b[:, t * LANES:(t + 1) * LANES]
        return carry

    lax.fori_loop(0, N1 // K1_GROUP, stage2, 0, unroll=2 * FFT_UNROLL)

    skip = skip_ref[...] + hb0_ref[0:1, :]

    def stage3(j2, carry):
        b = jnp.concatenate([b_scr[pl.ds(j2, N1, stride=bp), :],
                             b_scr[pl.ds(N2 + j2, N1, stride=bp), :]], axis=0)
        y = _dot(t2_ref[j2], b)
        x_scr[0, pl.ds(j2, N1h, stride=xp), :] = y[:N1h] + x_scr[0, pl.ds(j2, N1h, stride=xp), :] * skip
        x_scr[1, pl.ds(j2, N1h, stride=xp), :] = y[N1h:] + x_scr[1, pl.ds(j2, N1h, stride=xp), :] * skip
        return carry

    lax.fori_loop(0, N2, stage3, 0, unroll=4 * FFT_UNROLL)

    for b in range(2):
        for j1 in range(N1h):
            y_ref[b, pl.ds(j1 * N2, N2), :] = x_scr[b, pl.ds(j1 * xp, N2), :]


def _hy_conv(vg, hspec, skip, hb0):
    B, L, d_hy = vg.shape
    assert B % 2 == 0
    tb = _dft_tables(L)
    N1, N2 = tb["N1"], tb["N2"]
    n = 2 * L
    nt = d_hy // LANES
    t1, t2 = (jnp.asarray(tb[k]).astype(BF16) for k in ("t1", "t2"))
    f2f, f2i = (jnp.asarray(tb[k]).astype(BF16) for k in ("f2f", "f2i"))
    const3 = lambda a: pl.BlockSpec(a.shape, lambda b, c: (0, 0, 0))
    const2 = lambda a: pl.BlockSpec(a.shape, lambda b, c: (0, 0))
    return pl.pallas_call(
        functools.partial(_hy_conv_kernel, N1=N1, N2=N2),
        grid=(B // 2, nt),
        in_specs=[pl.BlockSpec((2, L, LANES), lambda b, c: (b, 0, c)),
                  pl.BlockSpec((2 * n, LANES), lambda b, c: (0, c)),
                  pl.BlockSpec((1, LANES), lambda b, c: (0, c)),
                  pl.BlockSpec((SUBLANES, LANES), lambda b, c: (0, c)),
                  const3(t1), const2(f2f), const2(f2i), const3(t2)],
        out_specs=pl.BlockSpec((2, L, LANES), lambda b, c: (b, 0, c)),
        out_shape=jax.ShapeDtypeStruct((B, L, d_hy), F32),
        scratch_shapes=[pltpu.VMEM((2, (N1 // 2) * (N2 + PITCH_PAD), LANES), F32),
                        pltpu.VMEM((N2 * (2 * N1 + PITCH_PAD), LANES), F32),
                        pltpu.VMEM((N1 * (2 * N2 + PITCH_PAD), LANES), F32)],
        compiler_params=_cparams(("parallel", "parallel"), 58),
        name="hy_conv",
    )(vg, hspec, skip.reshape(1, d_hy).astype(F32), hb0, t1, f2f, f2i, t2)


def _hyena_branch(p3, conv_w, conv_b, fw1, fb1, fw2, fb2, fw3, fb3, fw4, freq, skip, d_hy):
    B, L, _ = p3.shape
    x0c, vg = _hy_pre(p3, conv_w, conv_b, d_hy)
    zt = jnp.asarray(_filter_positions(L, fw1.shape[0]).T)
    h3 = _filt_mlp(zt, fw1, fb1, fw2, fb2, fw3, fb3, freq).T
    hspec, hb0 = _filt_fft(h3, fw4, L, d_hy)
    return x0c, _hy_conv(vg, hspec, skip, hb0)


ROUTE_GATE, ROUTE_EXPERT, ROUTE_RANK = 0, 2, 4
MIX_ROWS = 256


def _mix_route_kernel(yc_ref, x0_ref, of_ref, ob_ref, z_ref, hnw_ref, gnw_ref, x_ref, wo_ref, n2w_ref, wr_ref, br_ref,
                      x2_ref, u_ref, route_ref, route_t_ref, cnt_ref, run_scr, *, n_heads, head_dim, n_groups,
                      per_group):
    tm = x_ref.shape[0]
    G, P = n_groups, per_group

    @pl.when(pl.program_id(0) == 0)
    def _():
        run_scr[...] = jnp.zeros_like(run_scr)

    run = run_scr[0:1, :]
    blocks = [pl.ds(s * MIX_ROWS, MIX_ROWS) for s in range(tm // MIX_ROWS)]
    us = []
    for blk, rs in enumerate(blocks):
        yh = yc_ref[rs, :] * x0_ref[rs, :].astype(F32)
        yh = yh * lax.rsqrt(jnp.mean(yh * yh, axis=-1, keepdims=True) + EPS) * hnw_ref[...]
        parts = [yh.astype(BF16)]
        for h in range(n_heads):
            sl = slice(h * head_dim, (h + 1) * head_dim)
            o = of_ref[rs, sl] + ob_ref[rs, sl]
            z = z_ref[rs, sl].astype(F32)
            o = o * lax.rsqrt(jnp.mean(o * o, axis=-1, keepdims=True) + EPS) * gnw_ref[...] * _silu(z)
            parts.append(o.astype(BF16))
        ymix = jnp.concatenate(parts, axis=-1)
        x2 = x_ref[rs, :] + jnp.dot(ymix, wo_ref[...], preferred_element_type=F32)
        x2_ref[rs, :] = x2
        u = x2 * lax.rsqrt(jnp.mean(x2 * x2, axis=-1, keepdims=True) + EPS) * n2w_ref[...]
        _store_token_tiles(u_ref, blk * MIX_ROWS, _pack_bf16_pairs(u))
        us.append(u)

    all_logits = [_dot(u, wr_ref[...]) + br_ref[...] for u in us]
    for s, (rs, logits) in enumerate(zip(blocks, all_logits)):
        lane = lax.broadcasted_iota(jnp.int32, logits.shape, 1)
        neg = jnp.float32(-jnp.inf)
        big = jnp.int32(4 * LANES)
        first = lambda hit: jnp.min(jnp.where(hit, lane, big), axis=-1, keepdims=True)
        gl = jnp.where(lane < G, logits, neg)
        gmax = jnp.max(gl, axis=-1, keepdims=True)
        gidx = first(gl == gmax)
        grp_gate = 1.0 / jnp.sum(jnp.exp(gl - gmax), axis=-1, keepdims=True)
        in_grp = (lane >= G) & (lane < G + G * P) & (((lane - G) // P) == gidx)
        ll = jnp.where(in_grp, logits, neg)
        m1 = jnp.max(ll, axis=-1, keepdims=True)
        i1 = first(ll == m1)
        denom = jnp.sum(jnp.exp(ll - m1), axis=-1, keepdims=True)
        ll2 = jnp.where(lane == i1, neg, ll)
        m2 = jnp.max(ll2, axis=-1, keepdims=True)
        i2 = first(ll2 == m2)
        p1 = 1.0 / denom
        p2 = jnp.exp(m2 - m1) / denom
        gate1 = grp_gate * (p1 / (p1 + p2))
        gate2 = grp_gate * (p2 / (p1 + p2))
        e1 = i1 - G
        e2 = i2 - G

        oh1 = jnp.where(lane == e1, 1.0, 0.0)
        oh2 = jnp.where(lane == e2, 1.0, 0.0)
        oh = oh1 + oh2
        ii = lax.broadcasted_iota(jnp.int32, (MIX_ROWS, MIX_ROWS), 0)
        jj = lax.broadcasted_iota(jnp.int32, (MIX_ROWS, MIX_ROWS), 1)
        before = _dot(jnp.where(ii > jj, 1.0, 0.0), oh) + run
        r1 = jnp.sum(oh1 * before, axis=-1, keepdims=True)
        r2 = jnp.sum(oh2 * before, axis=-1, keepdims=True)
        run = run + jnp.sum(oh, axis=0, keepdims=True)

        rec = jnp.where(lane == ROUTE_GATE, gate1, 0.0)
        rec = jnp.where(lane == ROUTE_GATE + 1, gate2, rec)
        rec = jnp.where(lane == ROUTE_EXPERT, e1.astype(F32), rec)
        rec = jnp.where(lane == ROUTE_EXPERT + 1, e2.astype(F32), rec)
        rec = jnp.where(lane == ROUTE_RANK, r1, rec)
        rec = jnp.where(lane == ROUTE_RANK + 1, r2, rec)
        route_ref[rs, :] = rec
        route_t_ref[:, s * MIX_ROWS:(s + 1) * MIX_ROWS] = jnp.transpose(rec)[:SUBLANES, :]

    run_scr[...] = jnp.broadcast_to(run, run_scr.shape)
    cnt_ref[...] = run_scr[...]


def _mix_route(yconv, x0c, o_f, o_b, p, z_col, hy_norm_w, gdn_norm_w, xf, w_out_bf16, norm2_w, wr, br,
               n_heads, head_dim, n_groups, per_group, tm=512):
    M, D = xf.shape
    d_hy = yconv.shape[1]
    d_gdn = o_f.shape[1]
    assert z_col % d_gdn == 0 and n_groups * (per_group + 1) <= LANES
    zb = z_col // d_gdn
    row = lambda i: (i, 0)
    const = lambda i: (0, 0)
    kern = functools.partial(_mix_route_kernel, n_heads=n_heads, head_dim=head_dim,
                             n_groups=n_groups, per_group=per_group)
    return pl.pallas_call(
        kern,
        grid=(M // tm,),
        in_specs=[pl.BlockSpec((tm, d_hy), row), pl.BlockSpec((tm, d_hy), row),
                  pl.BlockSpec((tm, d_gdn), row), pl.BlockSpec((tm, d_gdn), row),
                  pl.BlockSpec((tm, d_gdn), lambda i: (i, zb)),
                  pl.BlockSpec((1, d_hy), const), pl.BlockSpec((1, head_dim), const),
                  pl.BlockSpec((tm, D), row), pl.BlockSpec(w_out_bf16.shape, const, pipeline_mode=pl.Buffered(1)),
                  pl.BlockSpec((1, D), const), pl.BlockSpec((D, LANES), const), pl.BlockSpec((1, LANES), const)],
        out_specs=[pl.BlockSpec((tm, D), row), pl.BlockSpec((tm * (D // 2 // LANES), LANES), row),
                   pl.BlockSpec((tm, LANES), row), pl.BlockSpec((SUBLANES, tm), lambda i: (0, i)),
                   pl.BlockSpec((SUBLANES, LANES), const)],
        out_shape=[jax.ShapeDtypeStruct((M, D), F32), jax.ShapeDtypeStruct((M * (D // 2 // LANES), LANES), jnp.uint32),
                   jax.ShapeDtypeStruct((M, LANES), F32), jax.ShapeDtypeStruct((SUBLANES, M), F32),
                   jax.ShapeDtypeStruct((SUBLANES, LANES), F32)],
        scratch_shapes=[pltpu.VMEM((SUBLANES, LANES), F32)],
        compiler_params=_cparams(("arbitrary",), 58),
        name="mix_route",
    )(yconv, x0c, o_f, o_b, p, hy_norm_w.reshape(1, d_hy), gdn_norm_w.reshape(1, head_dim), xf, w_out_bf16,
      norm2_w.reshape(1, D), wr, br)


WAIT_GROUP = 8


def _experts_kernel(ts_ref, se_ref, nn_ref, nv_ref, tgt_nxt_ref, tgt_ref, tgt_prv_ref, u_hbm, w1_hbm, w3_hbm, w2_hbm,
                    out_hbm, x0, x1, y0, y1, wf1, wf3, wf2, w1b, w3b, w2b, gsem, ssem, wsem, *, n_tokens):
    i = pl.program_id(0)
    na = nn_ref[0]
    ns = nn_ref[1]
    CH = w1b.shape[0] // 2 // LANES
    TB = x0.shape[0] // CH
    W_SLOTS = wf1.shape[0]
    last_tile = pl.num_programs(0) - 1
    nv_cur = nv_ref[i]
    nv_nxt = jnp.where(i + 1 < na, nv_ref[jnp.minimum(i + 1, last_tile)], 0)
    nv_prv = jnp.where(i > 0, nv_ref[jnp.maximum(i - 1, 0)], 0)
    nv_pp = jnp.where(i > 1, nv_ref[jnp.maximum(i - 2, 0)], 0)

    def weight_copies(s):
        slot = lax.rem(s, W_SLOTS)
        e = se_ref[s]
        return [pltpu.make_async_copy(w_hbm.at[e], wf.at[slot], wsem.at[slot])
                for w_hbm, wf in ((w1_hbm, wf1), (w3_hbm, wf3), (w2_hbm, wf2))]

    def token_of(v):
        if n_tokens & (n_tokens - 1) == 0:
            return v & (n_tokens - 1)
        return lax.rem(v, n_tokens)

    def tile_of(row):
        return pl.ds(pl.multiple_of(row * CH, CH), CH)

    def gather(tgt, xbuf, s, r):
        return pltpu.make_async_copy(u_hbm.at[tile_of(token_of(tgt[r]))], xbuf.at[tile_of(r)], gsem.at[s])

    def scatter(ybuf, s, r, dst):
        return pltpu.make_async_copy(ybuf.at[tile_of(r)], out_hbm.at[tile_of(dst)], ssem.at[s])

    def wait_tokens(count, make):
        groups = count // WAIT_GROUP

        def grp(g, carry):
            make(WAIT_GROUP).wait()
            return carry

        lax.fori_loop(0, groups, grp, 0)

        def one(g, carry):
            make(1).wait()
            return carry

        lax.fori_loop(groups * WAIT_GROUP, count, one, 0)

    def wait_gather(xbuf, s, count):
        wait_tokens(count, lambda n: pltpu.make_async_copy(u_hbm.at[pl.ds(0, n * CH)], xbuf.at[pl.ds(0, n * CH)],
                                                             gsem.at[s]))

    def wait_scatter(ybuf, s, count):
        wait_tokens(count, lambda n: pltpu.make_async_copy(ybuf.at[pl.ds(0, n * CH)], out_hbm.at[pl.ds(0, n * CH)],
                                                             ssem.at[s]))

    def each_row(count, fn):
        def body(r, carry):
            fn(r)
            return carry
        lax.fori_loop(0, count, body, 0)

    def step(p):
        q = 1 - p
        x_cur, x_nxt = (x0, x1) if p == 0 else (x1, x0)
        y_cur, y_prv = (y0, y1) if p == 0 else (y1, y0)
        wait_gather(x_cur, p, nv_cur)
        wait_scatter(y_cur, p, nv_pp)

        x = _unpack_bf16_pairs(_load_token_tiles(x_cur, 0, TB, CH)).astype(BF16)
        for r in range(TB):
            @pl.when(r < nv_nxt)
            def _():
                gather(tgt_nxt_ref, x_nxt, q, r).start()
        for r in range(TB):
            @pl.when(r < nv_prv)
            def _():
                scatter(y_prv, q, r, tgt_prv_ref[r]).start()
        h = _silu(jnp.dot(x, w1b[...], preferred_element_type=F32)) * jnp.dot(x, w3b[...], preferred_element_type=F32)
        _store_token_tiles(y_cur, 0, _pack_bf16_pairs(jnp.dot(h.astype(BF16), w2b[...], preferred_element_type=F32)))

        @pl.when(i == na - 1)
        def _():
            wait_scatter(y_prv, q, nv_prv)
            each_row(nv_cur, lambda r: scatter(y_cur, p, r, tgt_ref[r]).start())
            wait_scatter(y_cur, p, nv_cur)

    @pl.when(i < na)
    def _():
        @pl.when(i == 0)
        def _():
            x0[...] = jnp.zeros_like(x0)
            x1[...] = jnp.zeros_like(x1)
            each_row(nv_cur, lambda r: gather(tgt_ref, x0, 0, r).start())
            for s0 in range(W_SLOTS - 1):
                @pl.when(s0 < ns)
                def _():
                    for c in weight_copies(s0):
                        c.start(priority=WEIGHT_DMA_PRIORITY)

        s = ts_ref[i]

        @pl.when((i == 0) | (s != ts_ref[jnp.maximum(i - 1, 0)]))
        def _():
            @pl.when(s + W_SLOTS - 1 < ns)
            def _():
                for c in weight_copies(s + W_SLOTS - 1):
                    c.start(priority=WEIGHT_DMA_PRIORITY)

            for c in weight_copies(s):
                c.wait()
            slot = lax.rem(s, W_SLOTS)
            w1b[...] = wf1[slot].astype(BF16)
            w3b[...] = wf3[slot].astype(BF16)
            w2b[...] = wf2[slot].astype(BF16)

        parity = lax.rem(i, 2)

        @pl.when(parity == 0)
        def _():
            step(0)

        @pl.when(parity == 1)
        def _():
            step(1)


WEIGHT_SLOTS = 3
WEIGHT_DMA_PRIORITY = 1


def _experts(u, slot_tgt, tile_seq, seq_expert, n_active_seq, tile_valid, w1, w3, w2):
    E, D, de = w1.shape
    CH = D // 2 // LANES
    assert CH % SUBLANES == 0
    T = u.shape[0] // CH
    TB = EXPERT_ROWS
    n_tiles = slot_tgt.shape[0] // TB
    table = lambda f: pl.BlockSpec((TB,), lambda i, ts, se, nn, nv: (f(i),), memory_space=pltpu.SMEM)
    hbm = pl.BlockSpec(memory_space=pl.ANY)
    grid_spec = pltpu.PrefetchScalarGridSpec(
        num_scalar_prefetch=4,
        grid=(n_tiles,),
        in_specs=[table(lambda i: jnp.minimum(i + 1, n_tiles - 1)), table(lambda i: i),
                  table(lambda i: jnp.maximum(i - 1, 0)), hbm, hbm, hbm, hbm],
        out_specs=hbm,
        scratch_shapes=[pltpu.VMEM((TB * CH, LANES), jnp.uint32)] * 4
                       + [pltpu.VMEM((WEIGHT_SLOTS, D, de), F32), pltpu.VMEM((WEIGHT_SLOTS, D, de), F32),
                          pltpu.VMEM((WEIGHT_SLOTS, de, D), F32),
                          pltpu.VMEM((D, de), BF16), pltpu.VMEM((D, de), BF16), pltpu.VMEM((de, D), BF16),
                          pltpu.SemaphoreType.DMA((2,)), pltpu.SemaphoreType.DMA((2,)),
                          pltpu.SemaphoreType.DMA((WEIGHT_SLOTS,))],
    )
    return pl.pallas_call(
        functools.partial(_experts_kernel, n_tokens=T),
        grid_spec=grid_spec,
        out_shape=jax.ShapeDtypeStruct((2 * T * CH, LANES), jnp.uint32),
        compiler_params=_cparams(("arbitrary",), 58),
        name="experts",
    )(tile_seq, seq_expert, n_active_seq, tile_valid, slot_tgt, slot_tgt, slot_tgt, u, w1, w3, w2)


def _slot_table_kernel(dest_ref, init_hbm, o_ref, sem):
    fill = pltpu.make_async_copy(init_hbm, o_ref, sem)
    fill.start()
    fill.wait()

    def put(a, carry):
        o_ref[dest_ref[a]] = a
        return carry

    lax.fori_loop(0, dest_ref.shape[0], put, 0, unroll=16)


def _slot_table(dest, init):
    smem = pl.BlockSpec(memory_space=pltpu.SMEM)
    return pl.pallas_call(
        _slot_table_kernel,
        in_specs=[smem, pl.BlockSpec(memory_space=pl.ANY)],
        out_specs=smem,
        out_shape=jax.ShapeDtypeStruct(init.shape, jnp.int32),
        scratch_shapes=[pltpu.SemaphoreType.DMA],
        name="slot_table",
    )(dest, init)


def _dispatch_tables(route_t, counts, n_experts):
    T = route_t.shape[1]
    TB = EXPERT_ROWS
    e = route_t[ROUTE_EXPERT:ROUTE_EXPERT + 2].astype(jnp.int32)
    rank = route_t[ROUTE_RANK:ROUTE_RANK + 2].astype(jnp.int32)
    cnt = counts[0, :n_experts].astype(jnp.int32)
    padded = (cnt + TB - 1) // TB * TB
    pad_end = jnp.cumsum(padded)
    pad_start = pad_end - padded
    ids = jnp.arange(n_experts, dtype=jnp.int32)[:, None, None]
    start_of = jnp.sum(jnp.where(e[None] == ids, pad_start[:, None, None], 0), axis=0)
    dest = start_of + rank
    n_tiles = -(-(2 * T + n_experts * (TB - 1)) // TB)
    slot_tgt = _slot_table(dest.reshape(-1), jnp.zeros((n_tiles * TB,), jnp.int32))
    start = jnp.arange(n_tiles, dtype=jnp.int32) * TB
    tile_expert = jnp.sum(start[:, None] >= pad_end[None, :], axis=1)
    tile_expert = jnp.minimum(tile_expert, n_experts - 1).astype(jnp.int32)
    owns = cnt > 0
    pos = jnp.cumsum(owns.astype(jnp.int32)) - 1
    experts = jnp.arange(n_experts, dtype=jnp.int32)
    seq_expert = jnp.sum(jnp.where(owns[None, :] & (pos[None, :] == experts[:, None]), experts[None, :], 0), axis=1)
    tile_seq = jnp.sum(jnp.where(tile_expert[:, None] == experts[None, :], pos[None, :], 0), axis=1)
    n_active_seq = jnp.stack([pad_end[-1] // TB, jnp.sum(owns.astype(jnp.int32))]).astype(jnp.int32)
    pick = tile_expert[:, None] == experts[None, :]
    in_expert = start - jnp.sum(jnp.where(pick, pad_start[None, :], 0), axis=1)
    tile_cnt = jnp.sum(jnp.where(pick, cnt[None, :], 0), axis=1)
    tile_valid = jnp.where(start < pad_end[-1], jnp.clip(tile_cnt - in_expert, 0, TB), 0).astype(jnp.int32)
    return slot_tgt, tile_seq.astype(jnp.int32), seq_expert.astype(jnp.int32), n_active_seq, tile_valid


def _combine_kernel(x2_ref, e0_ref, e1_ref, route_ref, w_ref, o_ref, *, final_norm):
    r = route_ref[...]
    tm = x2_ref.shape[0]
    ch = e0_ref.shape[0] // tm
    e0 = _unpack_bf16_pairs(_load_token_tiles(e0_ref, 0, tm, ch))
    e1 = _unpack_bf16_pairs(_load_token_tiles(e1_ref, 0, tm, ch))
    y = x2_ref[...] + r[:, ROUTE_GATE:ROUTE_GATE + 1] * e0 + r[:, ROUTE_GATE + 1:ROUTE_GATE + 2] * e1
    if final_norm:
        y = y * lax.rsqrt(jnp.mean(y * y, axis=-1, keepdims=True) + EPS) * w_ref[...]
    o_ref[...] = y


def _combine(x2, planes, route, norm_w, final_norm, tm=512):
    M, D = x2.shape
    return pl.pallas_call(
        functools.partial(_combine_kernel, final_norm=final_norm),
        grid=(M // tm,),
        in_specs=[pl.BlockSpec((tm, D), lambda i: (i, 0)),
                  pl.BlockSpec((tm * (D // 2 // LANES), LANES), lambda i: (i, 0)),
                  pl.BlockSpec((tm * (D // 2 // LANES), LANES), lambda i: (M // tm + i, 0)),
                  pl.BlockSpec((tm, LANES), lambda i: (i, 0)),
                  pl.BlockSpec((1, D), lambda i: (0, 0))],
        out_specs=pl.BlockSpec((tm, D), lambda i: (i, 0)),
        out_shape=jax.ShapeDtypeStruct((M, D), F32),
        compiler_params=_cparams(("parallel",), 48),
        name="combine",
    )(x2, planes, planes, route, norm_w.reshape(1, D))


def kernel(x, norm1_w, w_in, hy_conv_w, hy_conv_b, hy_filt_w1, hy_filt_b1, hy_filt_w2, hy_filt_b2, hy_filt_w3, hy_filt_b3, hy_filt_w4, hy_sin_freq, hy_skip, hy_norm_w, gdn_conv_w, gdn_a_log_f, gdn_a_log_b, gdn_dt_bias_f, gdn_dt_bias_b, gdn_norm_w, w_out, norm2_w, router_group_w, router_group_b, router_expert_w, router_expert_b, exp_w1, exp_w3, exp_w2, final_norm_w):
    B, L, D = x.shape
    M = B * L
    depth = w_in.shape[0]
    d_hy = hy_skip.shape[-1]
    H = gdn_a_log_f.shape[-1]
    Dh = gdn_norm_w.shape[-1]
    d_gdn = H * Dh
    n_main = 3 * d_hy + 4 * d_gdn
    G = router_group_w.shape[-1]
    E = router_expert_w.shape[-1]
    xf = x.reshape(M, D)
    for l in range(depth):
        p, pg = _inproj(xf, norm1_w[l], jnp.swapaxes(w_in[l], 0, 1), n_main, 4 * H)
        p3 = p.reshape(B, L, n_main)
        x0c, yconv = _hyena_branch(p3, hy_conv_w[l], hy_conv_b[l], hy_filt_w1[l], hy_filt_b1[l], hy_filt_w2[l],
                                   hy_filt_b2[l], hy_filt_w3[l], hy_filt_b3[l], hy_filt_w4[l], hy_sin_freq[l],
                                   hy_skip[l], d_hy)
        o_f, o_b = _gdn_branch(p3, pg, gdn_conv_w[l], gdn_a_log_f[l], gdn_a_log_b[l], gdn_dt_bias_f[l],
                               gdn_dt_bias_b[l], 3 * d_hy, H, Dh)
        wr = jnp.pad(jnp.concatenate([router_group_w[l], router_expert_w[l]], axis=1), ((0, 0), (0, LANES - G - E)))
        br = jnp.pad(jnp.concatenate([router_group_b[l], router_expert_b[l]]), (0, LANES - G - E)).reshape(1, LANES)
        x2, u, route, route_t, counts = _mix_route(
            yconv.reshape(M, d_hy), x0c.reshape(M, d_hy), o_f.reshape(M, d_gdn), o_b.reshape(M, d_gdn), p,
            3 * d_hy + 3 * d_gdn, hy_norm_w[l], gdn_norm_w[l], xf, w_out[l].astype(BF16), norm2_w[l], wr, br,
            H, Dh, G, E // G)
        slot_tgt, tile_seq, seq_expert, n_active_seq, tile_valid = _dispatch_tables(route_t, counts, E)
        planes = _experts(u, slot_tgt, tile_seq, seq_expert, n_active_seq, tile_valid, exp_w1[l], exp_w3[l],
                          exp_w2[l])
        xf = _combine(x2, planes, route, final_norm_w, final_norm=(l == depth - 1))
    return xf.reshape(B, L, D)
```

```python
import functools
import math

import jax
import jax.numpy as jnp
import numpy as np
from jax import lax
from jax.experimental import pallas as pl
from jax.experimental.pallas import tpu as pltpu

F32 = jnp.float32
BF16 = jnp.bfloat16
EPS = 1e-6
LANES = 128
SUBLANES = 8
VMEM_BYTES_V7X = 64 * 1024 * 1024
GDN_CHUNK = 64
FFT_N2 = 128
EXPERT_ROWS = 256
DECAY_TARGET = 1e-2
FAST_DECAY_PCT = 0.3
SLOW_DECAY_PCT = 1.5


def _cparams(sem, vmem_mib):
    limit = int(vmem_mib * 1024 * 1024)
    assert limit < VMEM_BYTES_V7X
    return pltpu.CompilerParams(dimension_semantics=sem, vmem_limit_bytes=limit)


def _dot(a, b):
    return jnp.dot(a.astype(BF16), b.astype(BF16), preferred_element_type=F32)


def _dot_nt(a, b):
    return lax.dot_general(a.astype(BF16), b.astype(BF16), (((1,), (1,)), ((), ())), preferred_element_type=F32)


def _dot_tn(a, b):
    return lax.dot_general(a.astype(BF16), b.astype(BF16), (((0,), (0,)), ((), ())), preferred_element_type=F32)


def _dot_hi(a, b):
    return jnp.dot(a, b, preferred_element_type=F32, precision=lax.Precision.HIGHEST)


def _silu(x):
    return x * jax.nn.sigmoid(x)


def _pack_bf16_pairs(x):
    c = x.shape[1] // 2
    lo = lax.bitcast_convert_type(x[:, :c].astype(BF16).astype(F32), jnp.uint32) >> 16
    hi = lax.bitcast_convert_type(x[:, c:].astype(BF16).astype(F32), jnp.uint32) & jnp.uint32(0xFFFF0000)
    return hi | lo


def _unpack_bf16_pairs(w):
    lo = lax.bitcast_convert_type(w << 16, F32)
    hi = lax.bitcast_convert_type(w & jnp.uint32(0xFFFF0000), F32)
    return jnp.concatenate([lo, hi], axis=1)


def _store_token_tiles(ref, row0, words):
    n, width = words.shape
    ch = width // LANES
    for c in range(ch):
        ref[pl.ds(row0 * ch + c, n, stride=ch), :] = words[:, c * LANES:(c + 1) * LANES]


def _load_token_tiles(ref, row0, n, ch):
    return jnp.concatenate([ref[pl.ds(row0 * ch + c, n, stride=ch), :] for c in range(ch)], axis=1)


def _inproj_kernel(x_hbm, nw_ref, wt_ref, wgt_ref, p_ref, g_ref, h_scr, x_scr, x_sem):
    i = pl.program_id(0)
    j = pl.program_id(1)
    tm = x_scr.shape[0]

    def x_copy(tile):
        return pltpu.make_async_copy(x_hbm.at[pl.ds(pl.multiple_of(tile * tm, tm), tm)], x_scr, x_sem)

    @pl.when((i == 0) & (j == 0))
    def _():
        x_copy(0).start()

    @pl.when((j == 1) & (i + 1 < pl.num_programs(0)))
    def _():
        x_copy(i + 1).start()

    @pl.when(j == 0)
    def _():
        x_copy(i).wait()
        x = x_scr[...]
        h = x * lax.rsqrt(jnp.mean(x * x, axis=-1, keepdims=True) + EPS) * nw_ref[...]
        h_scr[...] = h.astype(BF16)
        g = _dot_nt(h, wgt_ref[...])
        g_ref[...] = jnp.concatenate([g, jnp.zeros((g.shape[0], LANES - g.shape[1]), F32)], axis=1)

    p_ref[...] = _dot_nt(h_scr[...], wt_ref[...]).astype(p_ref.dtype)


def _inproj(xf, norm_w, wt, n_main, n_gate, tm=2048, tn=512):
    M, D = xf.shape
    assert n_main % tn == 0 and M % tm == 0 and n_main % n_gate == 0 and n_gate % SUBLANES == 0
    assert n_main // tn >= 2
    return pl.pallas_call(
        _inproj_kernel,
        grid=(M // tm, n_main // tn),
        in_specs=[
            pl.BlockSpec(memory_space=pl.ANY),
            pl.BlockSpec((1, D), lambda i, j: (0, 0)),
            pl.BlockSpec((tn, D), lambda i, j: (j, 0)),
            pl.BlockSpec((n_gate, D), lambda i, j: (n_main // n_gate, 0)),
        ],
        out_specs=[
            pl.BlockSpec((tm, tn), lambda i, j: (i, j)),
            pl.BlockSpec((tm, LANES), lambda i, j: (i, 0)),
        ],
        out_shape=[jax.ShapeDtypeStruct((M, n_main), BF16), jax.ShapeDtypeStruct((M, LANES), F32)],
        scratch_shapes=[pltpu.VMEM((tm, D), BF16), pltpu.VMEM((tm, D), F32), pltpu.SemaphoreType.DMA],
        compiler_params=_cparams(("arbitrary", "arbitrary"), 57),
        name="inproj",
    )(xf, norm_w.reshape(1, D), wt, wt)


def _conv3_rows(ref, r0, rows, w, n_rows):
    cur = ref[pl.ds(r0, rows), :].astype(F32)
    lo = jnp.maximum(r0 - HALO_ROWS, 0)
    hi = jnp.minimum(r0 + rows, n_rows - HALO_ROWS)
    prev_grp = ref[pl.ds(pl.multiple_of(lo, HALO_ROWS), HALO_ROWS), :].astype(F32)
    next_grp = ref[pl.ds(pl.multiple_of(hi, HALO_ROWS), HALO_ROWS), :].astype(F32)
    prev_row = jnp.where(r0 > 0, prev_grp[HALO_ROWS - 1:HALO_ROWS, :], 0.0)
    next_row = jnp.where(r0 + rows < n_rows, next_grp[0:1, :], 0.0)
    row = lax.broadcasted_iota(jnp.int32, cur.shape, 0)
    xm = jnp.where(row == 0, prev_row, pltpu.roll(cur, 1, 0))
    xp = jnp.where(row == rows - 1, next_row, pltpu.roll(cur, rows - 1, 0))
    return xm * w[0:1, :] + cur * w[1:2, :] + xp * w[2:3, :]


CONV_ROWS = 256
HALO_ROWS = 16


def _hy_pre_kernel(x0_ref, x1_ref, v_ref, w0_ref, w1_ref, w2_ref, b0_ref, b1_ref, b2_ref, x0c_ref, vg_ref):
    L = x0_ref.shape[0]
    w0, w1, w2 = w0_ref[...], w1_ref[...], w2_ref[...]
    b0, b1, b2 = b0_ref[...], b1_ref[...], b2_ref[...]

    def body(c, carry):
        r0 = pl.multiple_of(c * CONV_ROWS, CONV_ROWS)
        x0c_ref[pl.ds(r0, CONV_ROWS), :] = (_conv3_rows(x0_ref, r0, CONV_ROWS, w0, L) + b0).astype(x0c_ref.dtype)
        x1c = _conv3_rows(x1_ref, r0, CONV_ROWS, w1, L) + b1
        vc = _conv3_rows(v_ref, r0, CONV_ROWS, w2, L) + b2
        vg_ref[pl.ds(r0, CONV_ROWS), :] = (vc * x1c).astype(vg_ref.dtype)
        return carry

    lax.fori_loop(0, L // CONV_ROWS, body, 0, unroll=2)


def _hy_pre(p3, conv_w, conv_b, d_hy):
    B, L, _ = p3.shape
    nt = d_hy // LANES
    bias = conv_b.reshape(1, -1)
    pspec = lambda off: pl.BlockSpec((None, L, LANES), lambda b, c: (b, 0, c + off))
    wspec = lambda off: pl.BlockSpec((3, LANES), lambda b, c: (0, c + off))
    bspec = lambda off: pl.BlockSpec((1, LANES), lambda b, c: (0, c + off))
    ospec = pl.BlockSpec((None, L, LANES), lambda b, c: (b, 0, c))
    return pl.pallas_call(
        _hy_pre_kernel,
        grid=(B, nt),
        in_specs=[pspec(0), pspec(nt), pspec(2 * nt), wspec(0), wspec(nt), wspec(2 * nt),
                  bspec(0), bspec(nt), bspec(2 * nt)],
        out_specs=[ospec, ospec],
        out_shape=[jax.ShapeDtypeStruct((B, L, d_hy), BF16)] * 2,
        compiler_params=_cparams(("parallel", "parallel"), 40),
        name="hy_pre",
    )(p3, p3, p3, conv_w, conv_w, conv_w, bias, bias, bias)


def _gdn_pre_kernel(x_ref, w_ref, o_ref, *, n_heads, head_dim):
    L = x_ref.shape[0]
    w = w_ref[...]
    c = pl.program_id(1)
    q_scale = jnp.where(c < n_heads, head_dim ** -0.5, 1.0)
    is_qk = c < 2 * n_heads

    def body(i, carry):
        r0 = pl.multiple_of(i * CONV_ROWS, CONV_ROWS)
        y = _silu(_conv3_rows(x_ref, r0, CONV_ROWS, w, L))
        inv = lax.rsqrt(jnp.sum(y * y, axis=-1, keepdims=True) + EPS) * q_scale
        o_ref[pl.ds(r0, CONV_ROWS), :] = (y * jnp.where(is_qk, inv, 1.0)).astype(o_ref.dtype)
        return carry

    lax.fori_loop(0, L // CONV_ROWS, body, 0, unroll=8)


def _gdn_pre(p3, conv_w, col0, n_heads, head_dim):
    B, L, _ = p3.shape
    assert head_dim == LANES
    nt = 3 * n_heads
    off = col0 // LANES
    return pl.pallas_call(
        functools.partial(_gdn_pre_kernel, n_heads=n_heads, head_dim=head_dim),
        grid=(B, nt),
        in_specs=[pl.BlockSpec((None, L, LANES), lambda b, c: (b, 0, c + off)),
                  pl.BlockSpec((3, LANES), lambda b, c: (0, c))],
        out_specs=pl.BlockSpec((None, L, LANES), lambda b, c: (b, 0, c)),
        out_shape=jax.ShapeDtypeStruct((B, L, nt * LANES), BF16),
        compiler_params=_cparams(("parallel", "parallel"), 24),
        name="gdn_pre",
    )(p3, conv_w)


GATE_ROWS = 512


def _gdn_gates_kernel(pg_ref, alog_ref, dtb_ref, o_ref, *, n_heads):
    H = n_heads
    x = pg_ref[...]
    beta = jax.nn.sigmoid(x)
    z = x + dtb_ref[...]
    softplus = jnp.maximum(z, 0.0) + jnp.log1p(jnp.exp(-jnp.abs(z)))
    g = -jnp.exp(alog_ref[...]) * softplus
    pos = lax.broadcasted_iota(jnp.int32, x.shape, 0) & (GDN_CHUNK - 1)
    gc_f = g
    gc_b = g
    step = 1
    while step < GDN_CHUNK:
        gc_f = gc_f + jnp.where(pos >= step, pltpu.roll(gc_f, step, 0), 0.0)
        gc_b = gc_b + jnp.where(pos < GDN_CHUNK - step, pltpu.roll(gc_b, GATE_ROWS - step, 0), 0.0)
        step *= 2
    g_tot = pltpu.roll(gc_f + gc_b - g, 4 * H, 1)
    lane = lax.broadcasted_iota(jnp.int32, x.shape, 1)
    out = jnp.where(lane < 2 * H, beta,
                    jnp.where(lane < 3 * H, gc_f,
                              jnp.where(lane < 4 * H, gc_b,
                                        jnp.where((lane >= 6 * H) & (lane < 8 * H), g_tot, 0.0))))
    o_ref[...] = out


def _gdn_gates(pg, a_log_f, a_log_b, dt_bias_f, dt_bias_b, n_heads):
    M = pg.shape[0]
    H = n_heads
    assert 8 * H <= LANES
    pad = lambda a, b: jnp.concatenate([jnp.zeros((2 * H,), F32), a.astype(F32), b.astype(F32),
                                        jnp.zeros((LANES - 4 * H,), F32)]).reshape(1, LANES)
    return pl.pallas_call(
        functools.partial(_gdn_gates_kernel, n_heads=H),
        grid=(M // GATE_ROWS,),
        in_specs=[pl.BlockSpec((GATE_ROWS, LANES), lambda i: (i, 0)),
                  pl.BlockSpec((1, LANES), lambda i: (0, 0)),
                  pl.BlockSpec((1, LANES), lambda i: (0, 0))],
        out_specs=pl.BlockSpec((GATE_ROWS, LANES), lambda i: (i, 0)),
        out_shape=jax.ShapeDtypeStruct((M, LANES), F32),
        compiler_params=_cparams(("parallel",), 24),
        name="gdn_gates",
    )(pg, pad(a_log_f, a_log_b), pad(dt_bias_f, dt_bias_b))


def _delta_chunks(q, k, v, beta, gc_col, gc_row, gtot, state, lower):
    n = len(q)
    C = q[0].shape[0]
    D = k[0].shape[1]
    ii = lax.broadcasted_iota(jnp.int32, (C, C), 0)
    jj = lax.broadcasted_iota(jnp.int32, (C, C), 1)
    eye = jnp.where(ii == jj, 1.0, 0.0)
    incl = [(ii >= jj) if lo else (ii <= jj) for lo in lower]
    strict = [(ii > jj) if lo else (ii < jj) for lo in lower]
    rng = range(n)
    decay = [jnp.where(incl[i], jnp.exp(jnp.where(incl[i], gc_col[i] - gc_row[i], 0.0)), 0.0) for i in rng]
    kb = [k[i] * beta[i] for i in rng]
    kk = [_dot_nt(kb[i], k[i]) for i in rng]
    qk = [_dot_nt(q[i], k[i]) for i in rng]
    m = [jnp.where(strict[i], -(kk[i] * decay[i]), 0.0) for i in rng]
    r = [eye + m[i] for i in rng]
    m = [_dot(m[i], m[i]) for i in rng]
    for _ in range(int(math.log2(C)) - 2):
        rm = [_dot(jnp.concatenate([r[i], m[i]], axis=0), m[i]) for i in rng]
        r = [r[i] + rm[i][:C] for i in rng]
        m = [rm[i][C:] for i in rng]
    r = [r[i] + _dot(r[i], m[i]) for i in rng]
    eg = [jnp.exp(gc_col[i]) for i in rng]
    wu = [_dot(r[i], jnp.concatenate([kb[i] * eg[i], v[i] * beta[i]], axis=1)) for i in rng]
    ws = [_dot(jnp.concatenate([wu[i][:, :D], q[i] * eg[i]], axis=0), state[i]) for i in rng]
    v_new = [wu[i][:, D:] - ws[i][:C] for i in rng]
    qkm = [jnp.where(incl[i], qk[i] * decay[i], 0.0) for i in rng]
    out = [ws[i][C:] + _dot(qkm[i], v_new[i]) for i in rng]
    k_dec = [k[i] * jnp.exp(gtot[i] - gc_col[i]) for i in rng]
    new_state = [state[i] * jnp.exp(gtot[i][0:1, :]) + _dot_tn(k_dec[i], v_new[i]) for i in rng]
    return out, new_state


def _gdn_scan_kernel(qf_ref, kf_ref, vf_ref, qb_ref, kb_ref, vb_ref, gf_ref, gb_ref, rf_ref, rb_ref,
                     of_ref, ob_ref, s_scr, *, n_heads, head_dim):
    H, Dh = n_heads, head_dim
    B = qf_ref.shape[0]

    @pl.when(pl.program_id(0) == 0)
    def _():
        s_scr[...] = jnp.zeros_like(s_scr)

    col = lambda g, j: g[:, j:j + 1]
    sls = [slice(h * Dh, (h + 1) * Dh) for h in range(H)]
    q, k, v, beta, gc_col, gc_row, gtot, state = ([] for _ in range(8))
    for b in range(B):
        gf = gf_ref[b]
        gb = gb_ref[b]
        q += [qf_ref[b, :, sl] for sl in sls] + [qb_ref[b, :, sl] for sl in sls]
        k += [kf_ref[b, :, sl] for sl in sls] + [kb_ref[b, :, sl] for sl in sls]
        v += [vf_ref[b, :, sl] for sl in sls] + [vb_ref[b, :, sl] for sl in sls]
        beta += [col(gf, h) for h in range(H)] + [col(gb, H + h) for h in range(H)]
        gc_col += [col(gf, 2 * H + h) for h in range(H)] + [col(gb, 3 * H + h) for h in range(H)]
        gc_row += [rf_ref[b, h:h + 1, :] for h in range(H)] + [rb_ref[b, H + h:H + h + 1, :] for h in range(H)]
        gtot += [col(gf, 6 * H + h) for h in range(H)] + [col(gb, 7 * H + h) for h in range(H)]
        state += [s_scr[b, 0, h] for h in range(H)] + [s_scr[b, 1, h] for h in range(H)]
    out, new_state = _delta_chunks(q, k, v, beta, gc_col, gc_row, gtot, state, ([True] * H + [False] * H) * B)
    for b in range(B):
        for h in range(H):
            of_ref[b, :, sls[h]] = out[2 * H * b + h]
            ob_ref[b, :, sls[h]] = out[2 * H * b + H + h]
            s_scr[b, 0, h] = new_state[2 * H * b + h]
            s_scr[b, 1, h] = new_state[2 * H * b + H + h]


def _gdn_scan(qkv, gates, gates_row, n_heads, head_dim):
    B, L, _ = qkv.shape
    H, Dh = n_heads, head_dim
    d = H * Dh
    C = GDN_CHUNK
    N = L // C
    fwd = lambda col: pl.BlockSpec((B, C, d), lambda n: (0, n, col))
    bwd = lambda col: pl.BlockSpec((B, C, d), lambda n: (0, N - 1 - n, col))
    return pl.pallas_call(
        functools.partial(_gdn_scan_kernel, n_heads=H, head_dim=Dh),
        grid=(N,),
        in_specs=[fwd(0), fwd(1), fwd(2), bwd(0), bwd(1), bwd(2),
                  pl.BlockSpec((B, C, LANES), lambda n: (0, n, 0)),
                  pl.BlockSpec((B, C, LANES), lambda n: (0, N - 1 - n, 0)),
                  pl.BlockSpec((B, None, 2 * H, C), lambda n: (0, n, 0, 0)),
                  pl.BlockSpec((B, None, 2 * H, C), lambda n: (0, N - 1 - n, 0, 0))],
        out_specs=[pl.BlockSpec((B, C, d), lambda n: (0, n, 0)),
                   pl.BlockSpec((B, C, d), lambda n: (0, N - 1 - n, 0))],
        out_shape=[jax.ShapeDtypeStruct((B, L, d), F32)] * 2,
        scratch_shapes=[pltpu.VMEM((B, 2, H, Dh, Dh), F32)],
        compiler_params=_cparams(("arbitrary",), 32),
        name="gdn_scan",
    )(qkv, qkv, qkv, qkv, qkv, qkv, gates, gates, gates_row, gates_row)


def _gdn_branch(p3, pg, conv_w, a_log_f, a_log_b, dt_bias_f, dt_bias_b, col0, n_heads, head_dim):
    B, L, _ = p3.shape
    H = n_heads
    qkv = _gdn_pre(p3, conv_w, col0, H, head_dim)
    gates = _gdn_gates(pg, a_log_f, a_log_b, dt_bias_f, dt_bias_b, H).reshape(B, L, LANES)
    N = L // GDN_CHUNK
    gates_row = gates[..., 2 * H:4 * H].reshape(B, N, GDN_CHUNK, 2 * H).transpose(0, 1, 3, 2)
    return _gdn_scan(qkv, gates, gates_row, H, head_dim)


@functools.lru_cache(maxsize=None)
def _filter_positions(L, pos_emb_dim):
    n = 2 * L
    r = np.arange(n)
    k = np.where(r < L, r, np.where(r == L, 0, n - r)).astype(np.float64)
    t = k / (L - 1)
    bands = (pos_emb_dim - 1) // 2
    fb = np.linspace(1e-4, bands - 1, bands)
    ang = (2.0 * math.pi / L) * k[:, None] * fb[None, :]
    z = np.concatenate([t[:, None], np.cos(ang), -np.sin(ang)], axis=-1)
    return z.astype(np.float32)


@functools.lru_cache(maxsize=None)
def _decay_rates(d_hy):
    max_decay = math.log(DECAY_TARGET) / FAST_DECAY_PCT
    min_decay = math.log(DECAY_TARGET) / SLOW_DECAY_PCT
    return np.abs(np.linspace(min_decay, max_decay, d_hy)).astype(np.float32).reshape(1, d_hy)


def _filt_mlp_kernel(zt_ref, w1t_ref, b1_ref, w2t_ref, b2_ref, w3t_ref, b3_ref, fr_ref, o_ref):
    fr = fr_ref[...]
    h = jnp.sin(fr * (_dot_hi(w1t_ref[...], zt_ref[...]) + b1_ref[...]))
    h = jnp.sin(fr * (_dot_hi(w2t_ref[...], h) + b2_ref[...]))
    o_ref[...] = jnp.sin(fr * (_dot_hi(w3t_ref[...], h) + b3_ref[...]))


def _filt_mlp(zt, w1, b1, w2, b2, w3, b3, freq, tc=1024):
    pe, n = zt.shape
    fw = w1.shape[1]
    col = lambda a: a.reshape(-1, 1).astype(F32)
    full = lambda a: pl.BlockSpec(a.shape, lambda i: (0, 0))
    args = (zt, w1.T, col(b1), w2.T, col(b2), w3.T, col(b3), col(freq))
    return pl.pallas_call(
        _filt_mlp_kernel,
        grid=(n // tc,),
        in_specs=[pl.BlockSpec((pe, tc), lambda i: (0, i))] + [full(a) for a in args[1:]],
        out_specs=pl.BlockSpec((fw, tc), lambda i: (0, i)),
        out_shape=jax.ShapeDtypeStruct((fw, n), F32),
        compiler_params=_cparams(("parallel",), 24),
        name="filt_mlp",
    )(*args)


@functools.lru_cache(maxsize=None)
def _dft_tables(L):
    n = 2 * L
    N2 = FFT_N2
    N1 = n // N2
    N1h = N1 // 2
    j2 = np.arange(N2)[:, None, None]
    k1 = np.arange(N1)[None, :, None]

    def stage1(n_j1):
        j1 = np.arange(n_j1)[None, None, :]
        m = (k1 * (N2 * j1 + j2)) % n
        th = 2.0 * np.pi * m / n
        return np.cos(th), np.sin(th)

    c, s = stage1(N1h)
    t1 = np.concatenate([np.concatenate([c, s], axis=2), np.concatenate([-s, c], axis=2)], axis=1)
    c, s = stage1(N1)
    t1g = np.concatenate([c, -s], axis=1)
    c, s = stage1(N1h)
    ct, st = np.swapaxes(c, 1, 2) / n, np.swapaxes(s, 1, 2) / n
    t2 = np.concatenate([np.concatenate([ct, -st], axis=2), np.concatenate([st, ct], axis=2)], axis=1)
    a = np.arange(N2)
    th = 2.0 * np.pi * ((a[:, None] * a[None, :]) % N2) / N2
    c2, s2 = np.cos(th), np.sin(th)
    f2f = np.block([[c2, s2], [-s2, c2]])
    f2i = np.block([[c2, -s2], [s2, c2]])
    return dict(N1=N1, N2=N2, t1=t1.astype(np.float32), t1g=t1g.astype(np.float32), t2=t2.astype(np.float32),
                f2f=f2f.astype(np.float32), f2i=f2i.astype(np.float32))


FILT_ROWS = 512
FFT_UNROLL = 8
PITCH_PAD = 8
K1_GROUP = 2


def _filt_fft_kernel(h3_ref, w4f_ref, w4b_ref, delta_ref, t1g_ref, f2f_ref, hspec_ref, hb0_ref, g_scr, a_scr,
                     *, L, N1, N2):
    n = 2 * L
    gp = N2 + PITCH_PAD
    ap = 2 * N1 + PITCH_PAD
    delta = delta_ref[...]
    hb0_ref[...] = jnp.zeros_like(hb0_ref)

    def gen(c, carry):
        r0 = pl.multiple_of(c * FILT_ROWS, FILT_ROWS)
        row = r0 + lax.broadcasted_iota(jnp.int32, (FILT_ROWS, LANES), 0)
        lag = jnp.where(row < L, row, jnp.where(row == L, 0, n - row))
        window = jnp.exp(-(lag.astype(F32) * (1.0 / (L - 1))) * delta)
        w4 = jnp.where(r0 < L, w4f_ref[...], w4b_ref[...])
        g = _dot(h3_ref[pl.ds(r0, FILT_ROWS), :], w4) * window
        at_l = row == L
        hb0_ref[...] += jnp.sum(jnp.where(at_l, g, 0.0), axis=0, keepdims=True)
        g = jnp.where(at_l, 0.0, g)
        for q in range(FILT_ROWS // N2):
            dst = pl.multiple_of((c * (FILT_ROWS // N2) + q) * gp, SUBLANES)
            g_scr[pl.ds(dst, N2), :] = g[q * N2:(q + 1) * N2]
        return carry

    lax.fori_loop(0, n // FILT_ROWS, gen, 0, unroll=4)

    def stage1(j2, carry):
        x = g_scr[pl.ds(j2, N1, stride=gp), :]
        a_scr[pl.ds(pl.multiple_of(j2 * ap, SUBLANES), 2 * N1), :] = _dot(t1g_ref[j2], x)
        return carry

    lax.fori_loop(0, N2, stage1, 0, unroll=4 * FFT_UNROLL)

    def stage2(kp, carry):
        k1s = [kp * K1_GROUP + t for t in range(K1_GROUP)]
        x = jnp.concatenate([jnp.concatenate([a_scr[pl.ds(k1, N2, stride=ap), :],
                                              a_scr[pl.ds(N1 + k1, N2, stride=ap), :]], axis=0) for k1 in k1s], axis=1)
        z = _dot(f2f_ref[...], x).astype(hspec_ref.dtype)
        for t, k1 in enumerate(k1s):
            hspec_ref[pl.ds(pl.multiple_of(k1 * 2 * N2, 2 * N2), 2 * N2), :] = z[:, t * LANES:(t + 1) * LANES]
        return carry

    lax.fori_loop(0, N1 // K1_GROUP, stage2, 0, unroll=2 * FFT_UNROLL)


def _filt_fft(h3, w4, L, d_hy):
    tb = _dft_tables(L)
    N1, N2 = tb["N1"], tb["N2"]
    n = 2 * L
    fw = h3.shape[1]
    nt = d_hy // LANES
    t1g = jnp.asarray(tb["t1g"]).astype(BF16)
    f2f = jnp.asarray(tb["f2f"]).astype(BF16)
    return pl.pallas_call(
        functools.partial(_filt_fft_kernel, L=L, N1=N1, N2=N2),
        grid=(nt,),
        in_specs=[pl.BlockSpec((n, fw), lambda c: (0, 0)),
                  pl.BlockSpec((fw, LANES), lambda c: (0, c)),
                  pl.BlockSpec((fw, LANES), lambda c: (0, c + nt)),
                  pl.BlockSpec((1, LANES), lambda c: (0, c)),
                  pl.BlockSpec(t1g.shape, lambda c: (0, 0, 0)),
                  pl.BlockSpec(f2f.shape, lambda c: (0, 0))],
        out_specs=[pl.BlockSpec((2 * n, LANES), lambda c: (0, c)),
                   pl.BlockSpec((SUBLANES, LANES), lambda c: (0, c))],
        out_shape=[jax.ShapeDtypeStruct((2 * n, d_hy), BF16), jax.ShapeDtypeStruct((SUBLANES, d_hy), F32)],
        scratch_shapes=[pltpu.VMEM((N1 * (N2 + PITCH_PAD), LANES), F32),
                        pltpu.VMEM((N2 * (2 * N1 + PITCH_PAD), LANES), F32)],
        compiler_params=_cparams(("parallel",), 48),
        name="filt_fft",
    )(h3, w4, w4, jnp.asarray(_decay_rates(d_hy)), t1g, f2f)


def _hy_conv_kernel(vg_ref, hspec_ref, skip_ref, hb0_ref, t1_ref, f2f_ref, f2i_ref, t2_ref, y_ref,
                    x_scr, a_scr, b_scr, *, N1, N2):
    N1h = N1 // 2
    xp = N2 + PITCH_PAD
    ap = 2 * N1 + PITCH_PAD
    bp = 2 * N2 + PITCH_PAD

    for b in range(2):
        for j1 in range(N1h):
            x_scr[b, pl.ds(j1 * xp, N2), :] = vg_ref[b, pl.ds(j1 * N2, N2), :].astype(F32)

    def stage1(j2, carry):
        x = jnp.concatenate([x_scr[0, pl.ds(j2, N1h, stride=xp), :],
                             x_scr[1, pl.ds(j2, N1h, stride=xp), :]], axis=0)
        a_scr[pl.ds(pl.multiple_of(j2 * ap, SUBLANES), 2 * N1), :] = _dot(t1_ref[j2], x)
        return carry

    lax.fori_loop(0, N2, stage1, 0, unroll=4 * FFT_UNROLL)

    def stage2(kp, carry):
        k1s = [kp * K1_GROUP + t for t in range(K1_GROUP)]
        x = jnp.concatenate([jnp.concatenate([a_scr[pl.ds(k1, N2, stride=ap), :],
                                              a_scr[pl.ds(N1 + k1, N2, stride=ap), :]], axis=0) for k1 in k1s], axis=1)
        z = _dot(f2f_ref[...], x)
        zr, zi = z[:N2], z[N2:]
        hs = [hspec_ref[pl.ds(pl.multiple_of(k1 * 2 * N2, 2 * N2), 2 * N2), :].astype(F32) for k1 in k1s]
        hr = jnp.concatenate([h[:N2] for h in hs], axis=1)
        hi = jnp.concatenate([h[N2:] for h in hs], axis=1)
        prod = jnp.concatenate([zr * hr - zi * hi, zr * hi + zi * hr], axis=0)
        b = _dot(f2i_ref[...], prod)
        for t, k1 in enumerate(k1s):
            b_scr[pl.ds(pl.multiple_of(k1 * bp, SUBLANES), 2 * N2), :] = b[:, t * LANES:(t + 1) * LANES]
        return carry

    lax.fori_loop(0, N1 // K1_GROUP, stage2, 0, unroll=2 * FFT_UNROLL)

    skip = skip_ref[...] + hb0_ref[0:1, :]

    def stage3(j2, carry):
        b = jnp.concatenate([b_scr[pl.ds(j2, N1, stride=bp), :],
                             b_scr[pl.ds(N2 + j2, N1, stride=bp), :]], axis=0)
        y = _dot(t2_ref[j2], b)
        x_scr[0, pl.ds(j2, N1h, stride=xp), :] = y[:N1h] + x_scr[0, pl.ds(j2, N1h, stride=xp), :] * skip
        x_scr[1, pl.ds(j2, N1h, stride=xp), :] = y[N1h:] + x_scr[1, pl.ds(j2, N1h, stride=xp), :] * skip
        return carry

    lax.fori_loop(0, N2, stage3, 0, unroll=4 * FFT_UNROLL)

    for b in range(2):
        for j1 in range(N1h):
            y_ref[b, pl.ds(j1 * N2, N2), :] = x_scr[b, pl.ds(j1 * xp, N2), :]


def _hy_conv(vg, hspec, skip, hb0):
    B, L, d_hy = vg.shape
    assert B % 2 == 0
    tb = _dft_tables(L)
    N1, N2 = tb["N1"], tb["N2"]
    n = 2 * L
    nt = d_hy // LANES
    t1, t2 = (jnp.asarray(tb[k]).astype(BF16) for k in ("t1", "t2"))
    f2f, f2i = (jnp.asarray(tb[k]).astype(BF16) for k in ("f2f", "f2i"))
    const3 = lambda a: pl.BlockSpec(a.shape, lambda b, c: (0, 0, 0))
    const2 = lambda a: pl.BlockSpec(a.shape, lambda b, c: (0, 0))
    return pl.pallas_call(
        functools.partial(_hy_conv_kernel, N1=N1, N2=N2),
        grid=(B // 2, nt),
        in_specs=[pl.BlockSpec((2, L, LANES), lambda b, c: (b, 0, c)),
                  pl.BlockSpec((2 * n, LANES), lambda b, c: (0, c)),
                  pl.BlockSpec((1, LANES), lambda b, c: (0, c)),
                  pl.BlockSpec((SUBLANES, LANES), lambda b, c: (0, c)),
                  const3(t1), const2(f2f), const2(f2i), const3(t2)],
        out_specs=pl.BlockSpec((2, L, LANES), lambda b, c: (b, 0, c)),
        out_shape=jax.ShapeDtypeStruct((B, L, d_hy), F32),
        scratch_shapes=[pltpu.VMEM((2, (N1 // 2) * (N2 + PITCH_PAD), LANES), F32),
                        pltpu.VMEM((N2 * (2 * N1 + PITCH_PAD), LANES), F32),
                        pltpu.VMEM((N1 * (2 * N2 + PITCH_PAD), LANES), F32)],
        compiler_params=_cparams(("parallel", "parallel"), 58),
        name="hy_conv",
    )(vg, hspec, skip.reshape(1, d_hy).astype(F32), hb0, t1, f2f, f2i, t2)


def _hyena_branch(p3, conv_w, conv_b, fw1, fb1, fw2, fb2, fw3, fb3, fw4, freq, skip, d_hy):
    B, L, _ = p3.shape
    x0c, vg = _hy_pre(p3, conv_w, conv_b, d_hy)
    zt = jnp.asarray(_filter_positions(L, fw1.shape[0]).T)
    h3 = _filt_mlp(zt, fw1, fb1, fw2, fb2, fw3, fb3, freq).T
    hspec, hb0 = _filt_fft(h3, fw4, L, d_hy)
    return x0c, _hy_conv(vg, hspec, skip, hb0)


ROUTE_GATE, ROUTE_EXPERT, ROUTE_RANK = 0, 2, 4
MIX_ROWS = 256


def _mix_route_kernel(yc_ref, x0_ref, of_ref, ob_ref, z_ref, hnw_ref, gnw_ref, x_ref, wo_ref, n2w_ref, wr_ref, br_ref,
                      x2_ref, u_ref, route_ref, route_t_ref, cnt_ref, run_scr, *, n_heads, head_dim, n_groups,
                      per_group):
    tm = x_ref.shape[0]
    G, P = n_groups, per_group

    @pl.when(pl.program_id(0) == 0)
    def _():
        run_scr[...] = jnp.zeros_like(run_scr)

    run = run_scr[0:1, :]
    blocks = [pl.ds(s * MIX_ROWS, MIX_ROWS) for s in range(tm // MIX_ROWS)]
    us = []
    for blk, rs in enumerate(blocks):
        yh = yc_ref[rs, :] * x0_ref[rs, :].astype(F32)
        yh = yh * lax.rsqrt(jnp.mean(yh * yh, axis=-1, keepdims=True) + EPS) * hnw_ref[...]
        parts = [yh.astype(BF16)]
        for h in range(n_heads):
            sl = slice(h * head_dim, (h + 1) * head_dim)
            o = of_ref[rs, sl] + ob_ref[rs, sl]
            z = z_ref[rs, sl].astype(F32)
            o = o * lax.rsqrt(jnp.mean(o * o, axis=-1, keepdims=True) + EPS) * gnw_ref[...] * _silu(z)
            parts.append(o.astype(BF16))
        ymix = jnp.concatenate(parts, axis=-1)
        x2 = x_ref[rs, :] + jnp.dot(ymix, wo_ref[...], preferred_element_type=F32)
        x2_ref[rs, :] = x2
        u = x2 * lax.rsqrt(jnp.mean(x2 * x2, axis=-1, keepdims=True) + EPS) * n2w_ref[...]
        _store_token_tiles(u_ref, blk * MIX_ROWS, _pack_bf16_pairs(u))
        us.append(u)

    all_logits = [_dot(u, wr_ref[...]) + br_ref[...] for u in us]
    for s, (rs, logits) in enumerate(zip(blocks, all_logits)):
        lane = lax.broadcasted_iota(jnp.int32, logits.shape, 1)
        neg = jnp.float32(-jnp.inf)
        big = jnp.int32(4 * LANES)
        first = lambda hit: jnp.min(jnp.where(hit, lane, big), axis=-1, keepdims=True)
        gl = jnp.where(lane < G, logits, neg)
        gmax = jnp.max(gl, axis=-1, keepdims=True)
        gidx = first(gl == gmax)
        grp_gate = 1.0 / jnp.sum(jnp.exp(gl - gmax), axis=-1, keepdims=True)
        in_grp = (lane >= G) & (lane < G + G * P) & (((lane - G) // P) == gidx)
        ll = jnp.where(in_grp, logits, neg)
        m1 = jnp.max(ll, axis=-1, keepdims=True)
        i1 = first(ll == m1)
        denom = jnp.sum(jnp.exp(ll - m1), axis=-1, keepdims=True)
        ll2 = jnp.where(lane == i1, neg, ll)
        m2 = jnp.max(ll2, axis=-1, keepdims=True)
        i2 = first(ll2 == m2)
        p1 = 1.0 / denom
        p2 = jnp.exp(m2 - m1) / denom
        gate1 = grp_gate * (p1 / (p1 + p2))
        gate2 = grp_gate * (p2 / (p1 + p2))
        e1 = i1 - G
        e2 = i2 - G

        oh1 = jnp.where(lane == e1, 1.0, 0.0)
        oh2 = jnp.where(lane == e2, 1.0, 0.0)
        oh = oh1 + oh2
        ii = lax.broadcasted_iota(jnp.int32, (MIX_ROWS, MIX_ROWS), 0)
        jj = lax.broadcasted_iota(jnp.int32, (MIX_ROWS, MIX_ROWS), 1)
        before = _dot(jnp.where(ii > jj, 1.0, 0.0), oh) + run
        r1 = jnp.sum(oh1 * before, axis=-1, keepdims=True)
        r2 = jnp.sum(oh2 * before, axis=-1, keepdims=True)
        run = run + jnp.sum(oh, axis=0, keepdims=True)

        rec = jnp.where(lane == ROUTE_GATE, gate1, 0.0)
        rec = jnp.where(lane == ROUTE_GATE + 1, gate2, rec)
        rec = jnp.where(lane == ROUTE_EXPERT, e1.astype(F32), rec)
        rec = jnp.where(lane == ROUTE_EXPERT + 1, e2.astype(F32), rec)
        rec = jnp.where(lane == ROUTE_RANK, r1, rec)
        rec = jnp.where(lane == ROUTE_RANK + 1, r2, rec)
        route_ref[rs, :] = rec
        route_t_ref[:, s * MIX_ROWS:(s + 1) * MIX_ROWS] = jnp.transpose(rec)[:SUBLANES, :]

    run_scr[...] = jnp.broadcast_to(run, run_scr.shape)
    cnt_ref[...] = run_scr[...]


def _mix_route(yconv, x0c, o_f, o_b, p, z_col, hy_norm_w, gdn_norm_w, xf, w_out_bf16, norm2_w, wr, br,
               n_heads, head_dim, n_groups, per_group, tm=512):
    M, D = xf.shape
    d_hy = yconv.shape[1]
    d_gdn = o_f.shape[1]
    assert z_col % d_gdn == 0 and n_groups * (per_group + 1) <= LANES
    zb = z_col // d_gdn
    row = lambda i: (i, 0)
    const = lambda i: (0, 0)
    kern = functools.partial(_mix_route_kernel, n_heads=n_heads, head_dim=head_dim,
                             n_groups=n_groups, per_group=per_group)
    return pl.pallas_call(
        kern,
        grid=(M // tm,),
        in_specs=[pl.BlockSpec((tm, d_hy), row), pl.BlockSpec((tm, d_hy), row),
                  pl.BlockSpec((tm, d_gdn), row), pl.BlockSpec((tm, d_gdn), row),
                  pl.BlockSpec((tm, d_gdn), lambda i: (i, zb)),
                  pl.BlockSpec((1, d_hy), const), pl.BlockSpec((1, head_dim), const),
                  pl.BlockSpec((tm, D), row), pl.BlockSpec(w_out_bf16.shape, const, pipeline_mode=pl.Buffered(1)),
                  pl.BlockSpec((1, D), const), pl.BlockSpec((D, LANES), const), pl.BlockSpec((1, LANES), const)],
        out_specs=[pl.BlockSpec((tm, D), row), pl.BlockSpec((tm * (D // 2 // LANES), LANES), row),
                   pl.BlockSpec((tm, LANES), row), pl.BlockSpec((SUBLANES, tm), lambda i: (0, i)),
                   pl.BlockSpec((SUBLANES, LANES), const)],
        out_shape=[jax.ShapeDtypeStruct((M, D), F32), jax.ShapeDtypeStruct((M * (D // 2 // LANES), LANES), jnp.uint32),
                   jax.ShapeDtypeStruct((M, LANES), F32), jax.ShapeDtypeStruct((SUBLANES, M), F32),
                   jax.ShapeDtypeStruct((SUBLANES, LANES), F32)],
        scratch_shapes=[pltpu.VMEM((SUBLANES, LANES), F32)],
        compiler_params=_cparams(("arbitrary",), 58),
        name="mix_route",
    )(yconv, x0c, o_f, o_b, p, hy_norm_w.reshape(1, d_hy), gdn_norm_w.reshape(1, head_dim), xf, w_out_bf16,
      norm2_w.reshape(1, D), wr, br)


WAIT_GROUP = 8


def _experts_kernel(ts_ref, se_ref, nn_ref, nv_ref, tgt_nxt_ref, tgt_ref, tgt_prv_ref, u_hbm, w1_hbm, w3_hbm, w2_hbm,
                    out_hbm, x0, x1, y0, y1, wf1, wf3, wf2, w1b, w3b, w2b, gsem, ssem, wsem, *, n_tokens):
    i = pl.program_id(0)
    na = nn_ref[0]
    ns = nn_ref[1]
    CH = w1b.shape[0] // 2 // LANES
    TB = x0.shape[0] // CH
    W_SLOTS = wf1.shape[0]
    last_tile = pl.num_programs(0) - 1
    nv_cur = nv_ref[i]
    nv_nxt = jnp.where(i + 1 < na, nv_ref[jnp.minimum(i + 1, last_tile)], 0)
    nv_prv = jnp.where(i > 0, nv_ref[jnp.maximum(i - 1, 0)], 0)
    nv_pp = jnp.where(i > 1, nv_ref[jnp.maximum(i - 2, 0)], 0)

    def weight_copies(s):
        slot = lax.rem(s, W_SLOTS)
        e = se_ref[s]
        return [pltpu.make_async_copy(w_hbm.at[e], wf.at[slot], wsem.at[slot])
                for w_hbm, wf in ((w1_hbm, wf1), (w3_hbm, wf3), (w2_hbm, wf2))]

    def token_of(v):
        if n_tokens & (n_tokens - 1) == 0:
            return v & (n_tokens - 1)
        return lax.rem(v, n_tokens)

    def tile_of(row):
        return pl.ds(pl.multiple_of(row * CH, CH), CH)

    def gather(tgt, xbuf, s, r):
        return pltpu.make_async_copy(u_hbm.at[tile_of(token_of(tgt[r]))], xbuf.at[tile_of(r)], gsem.at[s])

    def scatter(ybuf, s, r, dst):
        return pltpu.make_async_copy(ybuf.at[tile_of(r)], out_hbm.at[tile_of(dst)], ssem.at[s])

    def wait_tokens(count, make):
        groups = count // WAIT_GROUP

        def grp(g, carry):
            make(WAIT_GROUP).wait()
            return carry

        lax.fori_loop(0, groups, grp, 0)

        def one(g, carry):
            make(1).wait()
            return carry

        lax.fori_loop(groups * WAIT_GROUP, count, one, 0)

    def wait_gather(xbuf, s, count):
        wait_tokens(count, lambda n: pltpu.make_async_copy(u_hbm.at[pl.ds(0, n * CH)], xbuf.at[pl.ds(0, n * CH)],
                                                             gsem.at[s]))

    def wait_scatter(ybuf, s, count):
        wait_tokens(count, lambda n: pltpu.make_async_copy(ybuf.at[pl.ds(0, n * CH)], out_hbm.at[pl.ds(0, n * CH)],
                                                             ssem.at[s]))

    def each_row(count, fn):
        def body(r, carry):
            fn(r)
            return carry
        lax.fori_loop(0, count, body, 0)

    def step(p):
        q = 1 - p
        x_cur, x_nxt = (x0, x1) if p == 0 else (x1, x0)
        y_cur, y_prv = (y0, y1) if p == 0 else (y1, y0)
        wait_gather(x_cur, p, nv_cur)
        wait_scatter(y_cur, p, nv_pp)

        x = _unpack_bf16_pairs(_load_token_tiles(x_cur, 0, TB, CH)).astype(BF16)
        for r in range(TB):
            @pl.when(r < nv_nxt)
            def _():
                gather(tgt_nxt_ref, x_nxt, q, r).start()
        for r in range(TB):
            @pl.when(r < nv_prv)
            def _():
                scatter(y_prv, q, r, tgt_prv_ref[r]).start()
        h = _silu(jnp.dot(x, w1b[...], preferred_element_type=F32)) * jnp.dot(x, w3b[...], preferred_element_type=F32)
        _store_token_tiles(y_cur, 0, _pack_bf16_pairs(jnp.dot(h.astype(BF16), w2b[...], preferred_element_type=F32)))

        @pl.when(i == na - 1)
        def _():
            wait_scatter(y_prv, q, nv_prv)
            each_row(nv_cur, lambda r: scatter(y_cur, p, r, tgt_ref[r]).start())
            wait_scatter(y_cur, p, nv_cur)

    @pl.when(i < na)
    def _():
        @pl.when(i == 0)
        def _():
            x0[...] = jnp.zeros_like(x0)
            x1[...] = jnp.zeros_like(x1)
            each_row(nv_cur, lambda r: gather(tgt_ref, x0, 0, r).start())
            for s0 in range(W_SLOTS - 1):
                @pl.when(s0 < ns)
                def _():
                    for c in weight_copies(s0):
                        c.start(priority=WEIGHT_DMA_PRIORITY)

        s = ts_ref[i]

        @pl.when((i == 0) | (s != ts_ref[jnp.maximum(i - 1, 0)]))
        def _():
            @pl.when(s + W_SLOTS - 1 < ns)
            def _():
                for c in weight_copies(s + W_SLOTS - 1):
                    c.start(priority=WEIGHT_DMA_PRIORITY)

            for c in weight_copies(s):
                c.wait()
            slot = lax.rem(s, W_SLOTS)
            w1b[...] = wf1[slot].astype(BF16)
            w3b[...] = wf3[slot].astype(BF16)
            w2b[...] = wf2[slot].astype(BF16)

        parity = lax.rem(i, 2)

        @pl.when(parity == 0)
        def _():
            step(0)

        @pl.when(parity == 1)
        def _():
            step(1)


WEIGHT_SLOTS = 3
WEIGHT_DMA_PRIORITY = 1


def _experts(u, slot_tgt, tile_seq, seq_expert, n_active_seq, tile_valid, w1, w3, w2):
    E, D, de = w1.shape
    CH = D // 2 // LANES
    assert CH % SUBLANES == 0
    T = u.shape[0] // CH
    TB = EXPERT_ROWS
    n_tiles = slot_tgt.shape[0] // TB
    table = lambda f: pl.BlockSpec((TB,), lambda i, ts, se, nn, nv: (f(i),), memory_space=pltpu.SMEM)
    hbm = pl.BlockSpec(memory_space=pl.ANY)
    grid_spec = pltpu.PrefetchScalarGridSpec(
        num_scalar_prefetch=4,
        grid=(n_tiles,),
        in_specs=[table(lambda i: jnp.minimum(i + 1, n_tiles - 1)), table(lambda i: i),
                  table(lambda i: jnp.maximum(i - 1, 0)), hbm, hbm, hbm, hbm],
        out_specs=hbm,
        scratch_shapes=[pltpu.VMEM((TB * CH, LANES), jnp.uint32)] * 4
                       + [pltpu.VMEM((WEIGHT_SLOTS, D, de), F32), pltpu.VMEM((WEIGHT_SLOTS, D, de), F32),
                          pltpu.VMEM((WEIGHT_SLOTS, de, D), F32),
                          pltpu.VMEM((D, de), BF16), pltpu.VMEM((D, de), BF16), pltpu.VMEM((de, D), BF16),
                          pltpu.SemaphoreType.DMA((2,)), pltpu.SemaphoreType.DMA((2,)),
                          pltpu.SemaphoreType.DMA((WEIGHT_SLOTS,))],
    )
    return pl.pallas_call(
        functools.partial(_experts_kernel, n_tokens=T),
        grid_spec=grid_spec,
        out_shape=jax.ShapeDtypeStruct((2 * T * CH, LANES), jnp.uint32),
        compiler_params=_cparams(("arbitrary",), 58),
        name="experts",
    )(tile_seq, seq_expert, n_active_seq, tile_valid, slot_tgt, slot_tgt, slot_tgt, u, w1, w3, w2)


def _slot_table_kernel(dest_ref, init_hbm, o_ref, sem):
    fill = pltpu.make_async_copy(init_hbm, o_ref, sem)
    fill.start()
    fill.wait()

    def put(a, carry):
        o_ref[dest_ref[a]] = a
        return carry

    lax.fori_loop(0, dest_ref.shape[0], put, 0, unroll=16)


def _slot_table(dest, init):
    smem = pl.BlockSpec(memory_space=pltpu.SMEM)
    return pl.pallas_call(
        _slot_table_kernel,
        in_specs=[smem, pl.BlockSpec(memory_space=pl.ANY)],
        out_specs=smem,
        out_shape=jax.ShapeDtypeStruct(init.shape, jnp.int32),
        scratch_shapes=[pltpu.SemaphoreType.DMA],
        name="slot_table",
    )(dest, init)


def _dispatch_tables(route_t, counts, n_experts):
    T = route_t.shape[1]
    TB = EXPERT_ROWS
    e = route_t[ROUTE_EXPERT:ROUTE_EXPERT + 2].astype(jnp.int32)
    rank = route_t[ROUTE_RANK:ROUTE_RANK + 2].astype(jnp.int32)
    cnt = counts[0, :n_experts].astype(jnp.int32)
    padded = (cnt + TB - 1) // TB * TB
    pad_end = jnp.cumsum(padded)
    pad_start = pad_end - padded
    ids = jnp.arange(n_experts, dtype=jnp.int32)[:, None, None]
    start_of = jnp.sum(jnp.where(e[None] == ids, pad_start[:, None, None], 0), axis=0)
    dest = start_of + rank
    n_tiles = -(-(2 * T + n_experts * (TB - 1)) // TB)
    slot_tgt = _slot_table(dest.reshape(-1), jnp.zeros((n_tiles * TB,), jnp.int32))
    start = jnp.arange(n_tiles, dtype=jnp.int32) * TB
    tile_expert = jnp.sum(start[:, None] >= pad_end[None, :], axis=1)
    tile_expert = jnp.minimum(tile_expert, n_experts - 1).astype(jnp.int32)
    owns = cnt > 0
    pos = jnp.cumsum(owns.astype(jnp.int32)) - 1
    experts = jnp.arange(n_experts, dtype=jnp.int32)
    seq_expert = jnp.sum(jnp.where(owns[None, :] & (pos[None, :] == experts[:, None]), experts[None, :], 0), axis=1)
    tile_seq = jnp.sum(jnp.where(tile_expert[:, None] == experts[None, :], pos[None, :], 0), axis=1)
    n_active_seq = jnp.stack([pad_end[-1] // TB, jnp.sum(owns.astype(jnp.int32))]).astype(jnp.int32)
    pick = tile_expert[:, None] == experts[None, :]
    in_expert = start - jnp.sum(jnp.where(pick, pad_start[None, :], 0), axis=1)
    tile_cnt = jnp.sum(jnp.where(pick, cnt[None, :], 0), axis=1)
    tile_valid = jnp.where(start < pad_end[-1], jnp.clip(tile_cnt - in_expert, 0, TB), 0).astype(jnp.int32)
    return slot_tgt, tile_seq.astype(jnp.int32), seq_expert.astype(jnp.int32), n_active_seq, tile_valid


def _combine_kernel(x2_ref, e0_ref, e1_ref, route_ref, w_ref, o_ref, *, final_norm):
    r = route_ref[...]
    tm = x2_ref.shape[0]
    ch = e0_ref.shape[0] // tm
    e0 = _unpack_bf16_pairs(_load_token_tiles(e0_ref, 0, tm, ch))
    e1 = _unpack_bf16_pairs(_load_token_tiles(e1_ref, 0, tm, ch))
    y = x2_ref[...] + r[:, ROUTE_GATE:ROUTE_GATE + 1] * e0 + r[:, ROUTE_GATE + 1:ROUTE_GATE + 2] * e1
    if final_norm:
        y = y * lax.rsqrt(jnp.mean(y * y, axis=-1, keepdims=True) + EPS) * w_ref[...]
    o_ref[...] = y


def _combine(x2, planes, route, norm_w, final_norm, tm=512):
    M, D = x2.shape
    return pl.pallas_call(
        functools.partial(_combine_kernel, final_norm=final_norm),
        grid=(M // tm,),
        in_specs=[pl.BlockSpec((tm, D), lambda i: (i, 0)),
                  pl.BlockSpec((tm * (D // 2 // LANES), LANES), lambda i: (i, 0)),
                  pl.BlockSpec((tm * (D // 2 // LANES), LANES), lambda i: (M // tm + i, 0)),
                  pl.BlockSpec((tm, LANES), lambda i: (i, 0)),
                  pl.BlockSpec((1, D), lambda i: (0, 0))],
        out_specs=pl.BlockSpec((tm, D), lambda i: (i, 0)),
        out_shape=jax.ShapeDtypeStruct((M, D), F32),
        compiler_params=_cparams(("parallel",), 48),
        name="combine",
    )(x2, planes, planes, route, norm_w.reshape(1, D))


def kernel(x, norm1_w, w_in, hy_conv_w, hy_conv_b, hy_filt_w1, hy_filt_b1, hy_filt_w2, hy_filt_b2, hy_filt_w3, hy_filt_b3, hy_filt_w4, hy_sin_freq, hy_skip, hy_norm_w, gdn_conv_w, gdn_a_log_f, gdn_a_log_b, gdn_dt_bias_f, gdn_dt_bias_b, gdn_norm_w, w_out, norm2_w, router_group_w, router_group_b, router_expert_w, router_expert_b, exp_w1, exp_w3, exp_w2, final_norm_w):
    B, L, D = x.shape
    M = B * L
    depth = w_in.shape[0]
    d_hy = hy_skip.shape[-1]
    H = gdn_a_log_f.shape[-1]
    Dh = gdn_norm_w.shape[-1]
    d_gdn = H * Dh
    n_main = 3 * d_hy + 4 * d_gdn
    G = router_group_w.shape[-1]
    E = router_expert_w.shape[-1]
    xf = x.reshape(M, D)
    for l in range(depth):
        p, pg = _inproj(xf, norm1_w[l], jnp.swapaxes(w_in[l], 0, 1), n_main, 4 * H)
        p3 = p.reshape(B, L, n_main)
        x0c, yconv = _hyena_branch(p3, hy_conv_w[l], hy_conv_b[l], hy_filt_w1[l], hy_filt_b1[l], hy_filt_w2[l],
                                   hy_filt_b2[l], hy_filt_w3[l], hy_filt_b3[l], hy_filt_w4[l], hy_sin_freq[l],
                                   hy_skip[l], d_hy)
        o_f, o_b = _gdn_branch(p3, pg, gdn_conv_w[l], gdn_a_log_f[l], gdn_a_log_b[l], gdn_dt_bias_f[l],
                               gdn_dt_bias_b[l], 3 * d_hy, H, Dh)
        wr = jnp.pad(jnp.concatenate([router_group_w[l], router_expert_w[l]], axis=1), ((0, 0), (0, LANES - G - E)))
        br = jnp.pad(jnp.concatenate([router_group_b[l], router_expert_b[l]]), (0, LANES - G - E)).reshape(1, LANES)
        x2, u, route, route_t, counts = _mix_route(
            yconv.reshape(M, d_hy), x0c.reshape(M, d_hy), o_f.reshape(M, d_gdn), o_b.reshape(M, d_gdn), p,
            3 * d_hy + 3 * d_gdn, hy_norm_w[l], gdn_norm_w[l], xf, w_out[l].astype(BF16), norm2_w[l], wr, br,
            H, Dh, G, E // G)
        slot_tgt, tile_seq, seq_expert, n_active_seq, tile_valid = _dispatch_tables(route_t, counts, E)
        planes = _experts(u, slot_tgt, tile_seq, seq_expert, n_active_seq, tile_valid, exp_w1[l], exp_w3[l],
                          exp_w2[l])
        xf = _combine(x2, planes, route, final_norm_w, final_norm=(l == depth - 1))
    return xf.reshape(B, L, D)
```
